```python
import math
import jax, jax.numpy as jnp
from jax import lax
import numpy as np

D_MODEL = 2048
BATCH = 2
SEQ = 4096
DEPTH = 1

CHUNK = 64
Q_BLOCK = 128
ROPE_THETA = 10000.0
NORM_EPS = 1e-6
NEG_INF = -1e30
D_MIX = D_MODEL
DIFF_WIDTH = D_MIX // 2
DIFF_HEADS = 8
DIFF_VDIM = DIFF_WIDTH // DIFF_HEADS
DIFF_QKDIM = DIFF_VDIM // 2
GLA_WIDTH = D_MIX - DIFF_WIDTH
GLA_HEADS = 4
GLA_VDIM = GLA_WIDTH // GLA_HEADS
GLA_KDIM = GLA_VDIM // 2
GLA_GATE_RANK = 16
GLA_TAU = 16.0
N_MEM = 256
CROSS_HEADS = 4
CROSS_HDIM = D_MODEL // CROSS_HEADS
N_GROUPS = 4
EXPERTS_PER_GROUP = 8
N_EXPERTS = N_GROUPS * EXPERTS_PER_GROUP
TOP_K = 2
D_EXPERT = D_MODEL // 4
EXPERT_BLOCK = 128
IN_SIZES = (DIFF_HEADS * 2 * DIFF_QKDIM, DIFF_HEADS * 2 * DIFF_QKDIM, DIFF_HEADS * DIFF_VDIM,
            GLA_HEADS * GLA_KDIM, GLA_HEADS * GLA_KDIM, GLA_HEADS * GLA_VDIM,
            GLA_GATE_RANK, GLA_WIDTH)
IN_COLS = sum(IN_SIZES)

kernel_name = 'hybrid_diffattn_gla_hmoe_block'


def _split_points():
    pts, acc = [], 0
    for s in IN_SIZES[:-1]:
        acc += s
        pts.append(acc)
    return pts


def rmsnorm(x, g):
    xf = x.astype(jnp.float32)
    y = xf * lax.rsqrt(jnp.mean(xf * xf, axis=-1, keepdims=True) + NORM_EPS)
    return (y * g.astype(jnp.float32)).astype(x.dtype)


def rope(t, positions):
    half = t.shape[-1] // 2
    freq = ROPE_THETA ** (-jnp.arange(half, dtype=jnp.float32) / half)
    ang = positions.astype(jnp.float32)[..., None] * freq
    cos = jnp.cos(ang)[:, :, None, None, :]
    sin = jnp.sin(ang)[:, :, None, None, :]
    tf = t.astype(jnp.float32)
    t1, t2 = tf[..., :half], tf[..., half:]
    return jnp.concatenate([t1 * cos - t2 * sin, t2 * cos + t1 * sin], axis=-1).astype(t.dtype)


def diff_attention(q, k, v, lam):
    B, S, H, _, Dh = q.shape
    Dv = v.shape[-1]
    nqb = S // Q_BLOCK
    qb = q.reshape(B, nqb, Q_BLOCK, H, 2, Dh).transpose(1, 0, 2, 3, 4, 5)
    key_chunk = jnp.arange(S) // CHUNK
    scale = Dh ** -0.5

    def block(args):
        qblk, b = args
        s = jnp.einsum('bqhcd,bkhcd->bhcqk', qblk, k).astype(jnp.float32) * scale
        q_chunk = (b * Q_BLOCK + jnp.arange(Q_BLOCK)) // CHUNK
        allowed = key_chunk[None, :] <= q_chunk[:, None]
        p = jax.nn.softmax(jnp.where(allowed, s, NEG_INF), axis=-1)
        a = p[:, :, 0] - lam * p[:, :, 1]
        return jnp.einsum('bhqk,bkhd->bqhd', a.astype(v.dtype), v)

    out = lax.map(block, (qb, jnp.arange(nqb)))
    return out.transpose(1, 0, 2, 3, 4).reshape(B, S, H, Dv)


def gla(q, k, v, log_a):
    B, S, H, Dk = q.shape
    Dv = v.shape[-1]
    NC = S // CHUNK
    qc = q.reshape(B, NC, CHUNK, H, Dk).astype(jnp.float32)
    kc = k.reshape(B, NC, CHUNK, H, Dk).astype(jnp.float32)
    vc = v.reshape(B, NC, CHUNK, H, Dv).astype(jnp.float32)
    cum = jnp.cumsum(log_a.reshape(B, NC, CHUNK, H, Dk).astype(jnp.float32), axis=2)
    cum_end = cum[:, :, -1:]
    k_dec = kc * jnp.exp(cum_end - cum)
    d_state = jnp.einsum('bnchk,bnchv->nbhkv', k_dec, vc)
    decay = jnp.exp(cum_end[:, :, 0]).transpose(1, 0, 2, 3)

    def step(s_prev, inp):
        dcy, ds = inp
        s_new = dcy[..., None] * s_prev + ds
        return s_new, s_new

    _, states = lax.scan(step, jnp.zeros((B, H, Dk, Dv), jnp.float32), (decay, d_state))
    o = jnp.einsum('bnchk,nbhkv->bnchv', qc * (Dk ** -0.5), states)
    return o.reshape(B, S, H, Dv).astype(q.dtype)


def cross_attention(xn, memn, w_cq, w_ckv, qg, kg, w_co):
    B, S, D = xn.shape
    M = memn.shape[1]
    q = rmsnorm((xn @ w_cq).reshape(B, S, CROSS_HEADS, CROSS_HDIM), qg)
    k, v = jnp.split(memn @ w_ckv, 2, axis=-1)
    k = rmsnorm(k.reshape(B, M, CROSS_HEADS, CROSS_HDIM), kg)
    v = v.reshape(B, M, CROSS_HEADS, CROSS_HDIM)
    s = jnp.einsum('bshd,bmhd->bhsm', q, k).astype(jnp.float32) * (CROSS_HDIM ** -0.5)
    p = jax.nn.softmax(s, axis=-1)
    o = jnp.einsum('bhsm,bmhd->bshd', p.astype(v.dtype), v).reshape(B, S, D)
    return o @ w_co


def hmoe(xn, w_rg, b_rg, w_re, b_re, w_gate, w_up, w_down):
    B, S, D = xn.shape
    T = B * S
    xt = xn.reshape(T, D)
    tok = jnp.arange(T)
    grp_logits = (xt @ w_rg).astype(jnp.float32) + b_rg.astype(jnp.float32)
    grp_prob = jax.nn.softmax(grp_logits, axis=-1)
    grp = jnp.argmax(grp_logits, axis=-1)
    grp_w = jnp.max(grp_prob, axis=-1, keepdims=True)
    exp_logits = ((xt @ w_re).astype(jnp.float32) + b_re.astype(jnp.float32)).reshape(T, N_GROUPS, EXPERTS_PER_GROUP)
    in_prob = jax.nn.softmax(exp_logits[tok, grp], axis=-1)
    top_p, top_i = lax.top_k(in_prob, TOP_K)
    gate = grp_w * top_p / jnp.sum(top_p, axis=-1, keepdims=True)
    expert = grp[:, None] * EXPERTS_PER_GROUP + top_i

    TK = T * TOP_K
    flat_e = expert.reshape(-1)
    flat_tok = jnp.repeat(tok.astype(jnp.int32), TOP_K)
    flat_w = gate.reshape(-1)
    order = jnp.argsort(flat_e)
    e_s, tok_s, w_s = flat_e[order], flat_tok[order], flat_w[order]
    counts = jnp.bincount(flat_e, length=N_EXPERTS)
    starts = jnp.cumsum(counts) - counts
    pcounts = ((counts + EXPERT_BLOCK - 1) // EXPERT_BLOCK) * EXPERT_BLOCK
    pends = jnp.cumsum(pcounts)
    pstarts = pends - pcounts
    dest = pstarts[e_s] + (jnp.arange(TK) - starts[e_s])
    NB = (TK + N_EXPERTS * (EXPERT_BLOCK - 1) + EXPERT_BLOCK - 1) // EXPERT_BLOCK
    P = NB * EXPERT_BLOCK
    buf_tok = jnp.full((P,), T, jnp.int32).at[dest].set(tok_s)
    buf_w = jnp.zeros((P,), jnp.float32).at[dest].set(w_s)
    blk_e = jnp.minimum(jnp.searchsorted(pends, jnp.arange(NB) * EXPERT_BLOCK, side='right'), N_EXPERTS - 1)
    x_pad = jnp.concatenate([xt, jnp.zeros((1, D), xt.dtype)], axis=0)
    xb = x_pad[buf_tok].reshape(NB, EXPERT_BLOCK, D)

    def expert_block(args):
        xblk, e = args
        hdn = jax.nn.silu(xblk @ w_gate[e]) * (xblk @ w_up[e])
        return hdn @ w_down[e]

    yb = lax.map(expert_block, (xb, blk_e)).reshape(P, D)
    y = jnp.zeros((T + 1, D), jnp.float32).at[buf_tok].add(yb.astype(jnp.float32) * buf_w[:, None])
    return y[:T].reshape(B, S, D).astype(xn.dtype)


def setup_inputs(seed: int = 0) -> dict:
    key = jax.random.key(seed)
    ks = jax.random.split(key, 32)
    f32 = jnp.float32
    L, D = DEPTH, D_MODEL

    def nrm(k, shape, scale):
        return jax.random.normal(k, shape, f32) * scale

    def gain(k, shape):
        return 1.0 + 0.05 * jax.random.normal(k, shape, f32)

    start = jax.random.randint(ks[2], (BATCH, 1), 0, 64, dtype=jnp.int32) * CHUNK
    positions = (start + jnp.arange(SEQ, dtype=jnp.int32)[None, :]).astype(jnp.int32)
    return {
        'x': nrm(ks[0], (BATCH, SEQ, D), 1.0),
        'mem': nrm(ks[1], (BATCH, N_MEM, D), 1.0),
        'positions': positions,
        'g_attn': gain(ks[3], (L, D)),
        'w_in': nrm(ks[4], (L, D, IN_COLS), D ** -0.5),
        'q_norm_g': gain(ks[5], (L, DIFF_QKDIM)),
        'k_norm_g': gain(ks[6], (L, DIFF_QKDIM)),
        'lambda_q1': nrm(ks[7], (L, DIFF_QKDIM), 0.1),
        'lambda_k1': nrm(ks[8], (L, DIFF_QKDIM), 0.1),
        'lambda_q2': nrm(ks[9], (L, DIFF_QKDIM), 0.1),
        'lambda_k2': nrm(ks[10], (L, DIFF_QKDIM), 0.1),
        'diff_subln_g': gain(ks[11], (L, DIFF_VDIM)),
        'gla_w_a2': nrm(ks[12], (L, GLA_GATE_RANK, GLA_HEADS * GLA_KDIM), GLA_GATE_RANK ** -0.5),
        'gla_b_a': nrm(ks[13], (L, GLA_HEADS * GLA_KDIM), 0.1),
        'gla_out_g': gain(ks[14], (L, GLA_VDIM)),
        'w_out': nrm(ks[15], (L, D_MIX, D), D_MIX ** -0.5),
        'g_cross': gain(ks[16], (L, D)),
        'g_mem': gain(ks[17], (L, D)),
        'w_cq': nrm(ks[18], (L, D, D), D ** -0.5),
        'w_ckv': nrm(ks[19], (L, D, 2 * D), D ** -0.5),
        'cq_norm_g': gain(ks[20], (L, CROSS_HDIM)),
        'ck_norm_g': gain(ks[21], (L, CROSS_HDIM)),
        'w_co': nrm(ks[22], (L, D, D), D ** -0.5),
        'g_ffn': gain(ks[23], (L, D)),
        'w_router_grp': nrm(ks[24], (L, D, N_GROUPS), D ** -0.5),
        'b_router_grp': nrm(ks[25], (L, N_GROUPS), 0.01),
        'w_router_exp': nrm(ks[26], (L, D, N_EXPERTS), D ** -0.5),
        'b_router_exp': nrm(ks[27], (L, N_EXPERTS), 0.01),
        'w_gate': nrm(ks[28], (L, N_EXPERTS, D, D_EXPERT), D ** -0.5),
        'w_up': nrm(ks[29], (L, N_EXPERTS, D, D_EXPERT), D ** -0.5),
        'w_down': nrm(ks[30], (L, N_EXPERTS, D_EXPERT, D), D_EXPERT ** -0.5),
    }


def reference(x, mem, positions, g_attn, w_in, q_norm_g, k_norm_g, lambda_q1, lambda_k1,
              lambda_q2, lambda_k2, diff_subln_g, gla_w_a2, gla_b_a, gla_out_g, w_out,
              g_cross, g_mem, w_cq, w_ckv, cq_norm_g, ck_norm_g, w_co, g_ffn,
              w_router_grp, b_router_grp, w_router_exp, b_router_exp, w_gate, w_up, w_down):
    B, S, D = x.shape
    h = x
    for l in range(DEPTH):
        n = rmsnorm(h, g_attn[l])
        dq, dk, dv, gq, gk, gv, g_lr, g_r = jnp.split(n @ w_in[l], _split_points(), axis=-1)
        dq = rope(rmsnorm(dq.reshape(B, S, DIFF_HEADS, 2, DIFF_QKDIM), q_norm_g[l]), positions)
        dk = rope(rmsnorm(dk.reshape(B, S, DIFF_HEADS, 2, DIFF_QKDIM), k_norm_g[l]), positions)
        dv = dv.reshape(B, S, DIFF_HEADS, DIFF_VDIM)
        lam_init = 0.8 - 0.6 * math.exp(-0.3 * l)
        lam = (jnp.exp(jnp.sum(lambda_q1[l] * lambda_k1[l]).astype(jnp.float32))
               - jnp.exp(jnp.sum(lambda_q2[l] * lambda_k2[l]).astype(jnp.float32)) + lam_init)
        d_out = diff_attention(dq, dk, dv, lam)
        d_out = rmsnorm(d_out, diff_subln_g[l]) * (1.0 - lam_init)

        log_a = jax.nn.log_sigmoid((g_lr @ gla_w_a2[l] + gla_b_a[l]).astype(jnp.float32)) / GLA_TAU
        g_out = gla(gq.reshape(B, S, GLA_HEADS, GLA_KDIM), gk.reshape(B, S, GLA_HEADS, GLA_KDIM),
                    gv.reshape(B, S, GLA_HEADS, GLA_VDIM), log_a.reshape(B, S, GLA_HEADS, GLA_KDIM))
        g_out = rmsnorm(g_out, gla_out_g[l]) * jax.nn.silu(g_r.reshape(B, S, GLA_HEADS, GLA_VDIM))

        mix = jnp.concatenate([d_out.reshape(B, S, DIFF_WIDTH), g_out.reshape(B, S, GLA_WIDTH)], axis=-1)
        h = h + mix @ w_out[l]

        h = h + cross_attention(rmsnorm(h, g_cross[l]), rmsnorm(mem, g_mem[l]),
                                w_cq[l], w_ckv[l], cq_norm_g[l], ck_norm_g[l], w_co[l])

        h = h + hmoe(rmsnorm(h, g_ffn[l]), w_router_grp[l], b_router_grp[l], w_router_exp[l],
                     b_router_exp[l], w_gate[l], w_up[l], w_down[l])
    return h
```

```python
import functools
import math

import jax
import jax.numpy as jnp
from jax import lax
from jax.experimental import pallas as pl
from jax.experimental.pallas import tpu as pltpu

F32 = jnp.float32
BF16 = jnp.bfloat16
I32 = jnp.int32

LANES = 128
SUBLANES = 8

CHUNK = 64
ROPE_THETA = 10000.0
NORM_EPS = 1e-6
NEG_INF = -1e30
DIFF_HEADS = 8
DIFF_VDIM = 128
DIFF_QKDIM = 64
GLA_HEADS = 4
GLA_VDIM = 256
GLA_KDIM = 128
GLA_GATE_RANK = 16
GLA_TAU = 16.0
CROSS_HEADS = 4
N_GROUPS = 4
EXPERTS_PER_GROUP = 8
N_EXPERTS = N_GROUPS * EXPERTS_PER_GROUP
TOP_K = 2
LAM_INIT = 0.8 - 0.6 * math.exp(-0.3 * 0)

NT_DIMS = (((1,), (1,)), ((), ()))


def _cparams(semantics, vmem_mib):
    return pltpu.CompilerParams(dimension_semantics=semantics,
                                vmem_limit_bytes=vmem_mib * 1024 * 1024)


def _dot(a, b):
    return jnp.dot(a, b, preferred_element_type=F32)


def _rms(x, g):
    ms = jnp.mean(x * x, axis=-1, keepdims=True)
    return x * lax.rsqrt(ms + NORM_EPS) * g


def _split_bf16(x):
    hi = x.astype(BF16)
    lo = (x - hi.astype(F32)).astype(BF16)
    return hi, lo


TN = 512
J_QK = 4
J_MID = 6
J_LR = J_QK + J_MID
J_GR = J_LR + 1
N_J = J_GR + 2


def _inproj_kernel(x_ref, g_ref, pos_ref, freq_ref, qkg_ref, w_ref, wgr_ref, wlr_ref, wa2_ref, ba_ref,
                   qk_ref, mid_ref, loga_ref, sgr_ref, n_scr, cos_scr, sin_scr):
    j = pl.program_id(1)

    @pl.when(j == 0)
    def _():
        n_scr[...] = _rms(x_ref[...], g_ref[...]).astype(BF16)
        ang = pos_ref[...].astype(F32) * freq_ref[...]
        cos_scr[...] = jnp.cos(ang)
        sin_scr[...] = jnp.sin(ang)

    @pl.when(j < J_QK)
    def _():
        y = _dot(n_scr[...], w_ref[...].astype(BF16))
        lane = lax.broadcasted_iota(I32, (1, LANES), 1)
        low_seg = lane < DIFF_QKDIM
        first_half = (lane % DIFF_QKDIM) < (DIFF_QKDIM // 2)
        gain = qkg_ref[pl.ds(j // (J_QK // 2), 1), :]
        cos = cos_scr[...]
        sin = sin_scr[...]
        for c in range(TN // LANES):
            yb = y[:, c * LANES:(c + 1) * LANES]
            y2 = yb * yb
            s_lo = jnp.sum(jnp.where(low_seg, y2, 0.0), axis=-1, keepdims=True)
            s_hi = jnp.sum(jnp.where(low_seg, 0.0, y2), axis=-1, keepdims=True)
            ms = jnp.where(low_seg, s_lo, s_hi) * (1.0 / DIFF_QKDIM)
            yn = yb * lax.rsqrt(ms + NORM_EPS) * gain
            rot = jnp.where(first_half,
                            -pltpu.roll(yn, LANES - DIFF_QKDIM // 2, 1),
                            pltpu.roll(yn, DIFF_QKDIM // 2, 1))
            qk_ref[:, c * LANES:(c + 1) * LANES] = (yn * cos + rot * sin).astype(BF16)

    @pl.when((j >= J_QK) & (j < J_LR))
    def _():
        mid_ref[...] = _dot(n_scr[...], w_ref[...].astype(BF16)).astype(BF16)

    @pl.when(j == J_LR)
    def _():
        lr = _dot(n_scr[...], wlr_ref[...])
        z = _dot(lr.astype(BF16), wa2_ref[...]) + ba_ref[...]
        log_sig = jnp.minimum(z, 0.0) - jnp.log(1.0 + jnp.exp(-jnp.abs(z)))
        loga_ref[...] = log_sig * (1.0 / GLA_TAU)

    @pl.when(j >= J_GR)
    def _():
        y = _dot(n_scr[...], wgr_ref[...].astype(BF16))
        sgr_ref[...] = (y / (1.0 + jnp.exp(-y))).astype(BF16)


def _inproj(x2, g_attn, pos2, freq, qkg, w_in, w_gr, w_lr, w_a2, b_a, *, tm):
    T, D = x2.shape
    n_mid = J_MID * TN
    n_gk = GLA_HEADS * GLA_KDIM
    n_gr = w_gr.shape[1]
    return pl.pallas_call(
        _inproj_kernel,
        grid=(T // tm, N_J),
        in_specs=[
            pl.BlockSpec((tm, D), lambda i, j: (i, 0)),
            pl.BlockSpec((1, D), lambda i, j: (0, 0)),
            pl.BlockSpec((tm, 1), lambda i, j: (i, 0)),
            pl.BlockSpec((1, LANES), lambda i, j: (0, 0)),
            pl.BlockSpec((2, LANES), lambda i, j: (0, 0)),
            pl.BlockSpec((D, TN), lambda i, j: (0, jnp.minimum(j, J_LR - 1))),
            pl.BlockSpec((D, TN), lambda i, j: (0, jnp.clip(j - J_GR, 0, 1))),
            pl.BlockSpec((D, LANES), lambda i, j: (0, 0)),
            pl.BlockSpec((LANES, n_gk), lambda i, j: (0, 0)),
            pl.BlockSpec((1, n_gk), lambda i, j: (0, 0)),
        ],
        out_specs=[
            pl.BlockSpec((tm, TN), lambda i, j: (i, jnp.minimum(j, J_QK - 1))),
            pl.BlockSpec((tm, TN), lambda i, j: (i, jnp.clip(j - J_QK, 0, J_MID - 1))),
            pl.BlockSpec((tm, n_gk), lambda i, j: (i, 0)),
            pl.BlockSpec((tm, TN), lambda i, j: (i, jnp.clip(j - J_GR, 0, 1))),
        ],
        out_shape=[
            jax.ShapeDtypeStruct((T, J_QK * TN), BF16),
            jax.ShapeDtypeStruct((T, n_mid), BF16),
            jax.ShapeDtypeStruct((T, n_gk), F32),
            jax.ShapeDtypeStruct((T, n_gr), BF16),
        ],
        scratch_shapes=[
            pltpu.VMEM((tm, D), BF16),
            pltpu.VMEM((tm, LANES), F32),
            pltpu.VMEM((tm, LANES), F32),
        ],
        compiler_params=_cparams(("parallel", "arbitrary"), 56),
        name="inproj",
    )(x2, g_attn, pos2, freq, qkg, w_in, w_gr, w_lr, w_a2, b_a)


def _diffattn_kernel(lv_ref, q_ref, k_ref, v_ref, sg_ref, o_ref, acc1, acc2, m1, l1, m2, l2, *, tq):
    i = pl.program_id(2)
    lv = lv_ref[...]
    lam = (jnp.exp(jnp.sum(lv[0:1] * lv[1:2], axis=-1, keepdims=True))
           - jnp.exp(jnp.sum(lv[2:3] * lv[3:4], axis=-1, keepdims=True)) + LAM_INIT)

    q = q_ref[...]
    lane = lax.broadcasted_iota(I32, (1, LANES), 1)
    zero = jnp.zeros_like(q)
    q1 = jnp.where(lane < DIFF_QKDIM, q, zero)
    q2 = jnp.where(lane < DIFF_QKDIM, zero, q)

    for ref in (acc1, acc2, l1, l2):
        ref[...] = jnp.zeros(ref.shape, F32)
    for ref in (m1, m2):
        ref[...] = jnp.full(ref.shape, NEG_INF, F32)

    row_chunk = lax.broadcasted_iota(I32, (tq, tq), 0) // CHUNK
    col_chunk = lax.broadcasted_iota(I32, (tq, tq), 1) // CHUNK
    allowed = col_chunk <= row_chunk

    def step(j, masked):
        start = pl.multiple_of(j * tq, tq)
        k = k_ref[pl.ds(start, tq), :]
        v = v_ref[pl.ds(start, tq), :]
        for qc, acc, m, l in ((q1, acc1, m1, l1), (q2, acc2, m2, l2)):
            s = lax.dot_general(qc, k, NT_DIMS, preferred_element_type=F32)
            if masked:
                s = jnp.where(allowed, s, NEG_INF)
            m_old = m[...]
            m_new = jnp.maximum(m_old, jnp.max(s, axis=-1, keepdims=True))
            alpha = jnp.exp(m_old - m_new)
            p = jnp.exp(s - m_new)
            l[...] = alpha * l[...] + jnp.sum(p, axis=-1, keepdims=True)
            acc[...] = alpha * acc[...] + _dot(p.astype(BF16), v)
            m[...] = m_new

    def body(j, carry):
        step(j, False)
        return carry

    lax.fori_loop(0, i, body, 0)
    step(i, True)

    o = acc1[...] / l1[...] - lam * (acc2[...] / l2[...])
    o_ref[...] = (_rms(o, sg_ref[...]) * (1.0 - LAM_INIT)).astype(BF16)


def _diffattn(lvec, qk, mid, subln_g, *, B, S, tq):
    T = B * S
    nq = S // tq
    kern = functools.partial(_diffattn_kernel, tq=tq)
    return pl.pallas_call(
        kern,
        grid=(B, DIFF_HEADS, nq),
        in_specs=[
            pl.BlockSpec((4, DIFF_QKDIM), lambda b, h, i: (0, 0)),
            pl.BlockSpec((tq, LANES), lambda b, h, i: (b * nq + i, h)),
            pl.BlockSpec((S, LANES), lambda b, h, i: (b, DIFF_HEADS + h)),
            pl.BlockSpec((S, LANES), lambda b, h, i: (b, h)),
            pl.BlockSpec((1, DIFF_VDIM), lambda b, h, i: (0, 0)),
        ],
        out_specs=pl.BlockSpec((tq, DIFF_VDIM), lambda b, h, i: (b * nq + i, h)),
        out_shape=jax.ShapeDtypeStruct((T, DIFF_HEADS * DIFF_VDIM), BF16),
        scratch_shapes=[
            pltpu.VMEM((tq, DIFF_VDIM), F32),
            pltpu.VMEM((tq, DIFF_VDIM), F32),
            pltpu.VMEM((tq, 1), F32),
            pltpu.VMEM((tq, 1), F32),
            pltpu.VMEM((tq, 1), F32),
            pltpu.VMEM((tq, 1), F32),
        ],
        compiler_params=_cparams(("parallel", "parallel", "arbitrary"), 32),
        name="diffattn",
    )(lvec, qk, qk, mid, subln_g)


def _gla_kernel(q_ref, k_ref, v_ref, la_ref, sgr_ref, g_ref, o_ref, state, *, blk):
    @pl.when(pl.program_id(2) == 0)
    def _():
        state[...] = jnp.zeros(state.shape, F32)

    la_t = la_ref[...].T
    k_t = k_ref[...].astype(F32).T
    r = lax.broadcasted_iota(I32, (blk, blk), 0)
    c = lax.broadcasted_iota(I32, (blk, blk), 1)
    same = (r // CHUNK) == (c // CHUNK)
    tri = jnp.where(same & (r <= c), 1.0, 0.0).astype(BF16)
    ones = jnp.where(same, 1.0, 0.0).astype(BF16)
    hi, lo = _split_bf16(la_t)
    cum_t = _dot(hi, tri) + _dot(lo, tri)
    tot_t = _dot(hi, ones) + _dot(lo, ones)
    kd_t = k_t * jnp.exp(tot_t - cum_t)

    lane = lax.broadcasted_iota(I32, (1, LANES), 1)
    scale = GLA_KDIM ** -0.5
    for ck in range(blk // CHUNK):
        pair = slice((ck // 2) * LANES, (ck // 2 + 1) * LANES)
        rows = slice(ck * CHUNK, (ck + 1) * CHUNK)
        in_chunk = (lane // CHUNK) == (ck % 2)
        kd = jnp.where(in_chunk, kd_t[:, pair], 0.0).astype(BF16)
        d_state = _dot(kd, v_ref[pair, :])
        decay = jnp.exp(tot_t[:, ck * CHUNK:ck * CHUNK + 1])
        st = decay * state[...] + d_state
        state[...] = st
        o = _dot(q_ref[rows, :], st.astype(BF16)) * scale
        gate = sgr_ref[rows, :].astype(F32)
        o_ref[rows, :] = (_rms(o, g_ref[...]) * gate).astype(BF16)


def _gla(mid, log_a, sgr, out_g, *, B, S, blk):
    T = B * S
    ns = S // blk
    kern = functools.partial(_gla_kernel, blk=blk)
    q_col0 = (DIFF_HEADS * DIFF_VDIM) // GLA_KDIM
    k_col0 = q_col0 + GLA_HEADS
    v_col0 = (DIFF_HEADS * DIFF_VDIM + 2 * GLA_HEADS * GLA_KDIM) // GLA_VDIM
    return pl.pallas_call(
        kern,
        grid=(B, GLA_HEADS, ns),
        in_specs=[
            pl.BlockSpec((blk, GLA_KDIM), lambda b, h, s: (b * ns + s, q_col0 + h)),
            pl.BlockSpec((blk, GLA_KDIM), lambda b, h, s: (b * ns + s, k_col0 + h)),
            pl.BlockSpec((blk, GLA_VDIM), lambda b, h, s: (b * ns + s, v_col0 + h)),
            pl.BlockSpec((blk, GLA_KDIM), lambda b, h, s: (b * ns + s, h)),
            pl.BlockSpec((blk, GLA_VDIM), lambda b, h, s: (b * ns + s, h)),
            pl.BlockSpec((1, GLA_VDIM), lambda b, h, s: (0, 0)),
        ],
        out_specs=pl.BlockSpec((blk, GLA_VDIM), lambda b, h, s: (b * ns + s, h)),
        out_shape=jax.ShapeDtypeStruct((T, GLA_HEADS * GLA_VDIM), BF16),
        scratch_shapes=[pltpu.VMEM((GLA_KDIM, GLA_VDIM), F32)],
        compiler_params=_cparams(("parallel", "parallel", "arbitrary"), 32),
        name="gla",
    )(mid, mid, mid, log_a, sgr, out_g)


def _outproj_kernel(a_ref, b_ref, wa_ref, wb_ref, x_ref, o_ref):
    acc = _dot(a_ref[...], wa_ref[...].astype(BF16)) + _dot(b_ref[...], wb_ref[...].astype(BF16))
    o_ref[...] = x_ref[...] + acc


def _outproj(a, b, w_out, x2, *, tm, tn):
    T, ka = a.shape
    kb = b.shape[1]
    assert ka == kb
    D = w_out.shape[1]
    return pl.pallas_call(
        _outproj_kernel,
        grid=(T // tm, D // tn),
        in_specs=[
            pl.BlockSpec((tm, ka), lambda i, j: (i, 0)),
            pl.BlockSpec((tm, kb), lambda i, j: (i, 0)),
            pl.BlockSpec((ka, tn), lambda i, j: (0, j)),
            pl.BlockSpec((kb, tn), lambda i, j: (1, j)),
            pl.BlockSpec((tm, tn), lambda i, j: (i, j)),
        ],
        out_specs=pl.BlockSpec((tm, tn), lambda i, j: (i, j)),
        out_shape=jax.ShapeDtypeStruct((T, D), F32),
        compiler_params=_cparams(("parallel", "arbitrary"), 40),
        name="outproj",
    )(a, b, w_out, w_out, x2)


def _normproj_kernel(x_ref, g_ref, w_ref, hg_ref, o_ref, n_scr, *, n_norm):
    j = pl.program_id(1)

    @pl.when(j == 0)
    def _():
        n_scr[...] = _rms(x_ref[...], g_ref[...]).astype(BF16)

    y = _dot(n_scr[...], w_ref[...].astype(BF16))

    @pl.when(j < n_norm)
    def _():
        o_ref[...] = _rms(y, hg_ref[...]).astype(BF16)

    @pl.when(j >= n_norm)
    def _():
        o_ref[...] = y.astype(BF16)


def _normproj(x2, g, w, head_g, *, tm, tn, n_norm, name):
    T, D = x2.shape
    N = w.shape[1]
    kern = functools.partial(_normproj_kernel, n_norm=n_norm)
    return pl.pallas_call(
        kern,
        grid=(T // tm, N // tn),
        in_specs=[
            pl.BlockSpec((tm, D), lambda i, j: (i, 0)),
            pl.BlockSpec((1, D), lambda i, j: (0, 0)),
            pl.BlockSpec((D, tn), lambda i, j: (0, j)),
            pl.BlockSpec((1, tn), lambda i, j: (0, 0)),
        ],
        out_specs=pl.BlockSpec((tm, tn), lambda i, j: (i, j)),
        out_shape=jax.ShapeDtypeStruct((T, N), BF16),
        scratch_shapes=[pltpu.VMEM((tm, D), BF16)],
        compiler_params=_cparams(("parallel", "arbitrary"), 48),
        name=name,
    )(x2, g, w, head_g)


def _cross_kernel(q_ref, k_ref, v_ref, w_ref, h_ref, o_ref, att_scr, *, hdim):
    @pl.when(pl.program_id(1) == 0)
    def _():
        for hd in range(CROSS_HEADS):
            cols = slice(hd * hdim, (hd + 1) * hdim)
            s = lax.dot_general(q_ref[:, cols], k_ref[:, cols], NT_DIMS, preferred_element_type=F32)
            p = jnp.exp(s - jnp.max(s, axis=-1, keepdims=True))
            l = jnp.sum(p, axis=-1, keepdims=True)
            att_scr[:, cols] = (_dot(p.astype(BF16), v_ref[:, cols]) / l).astype(BF16)

    o_ref[...] = h_ref[...] + _dot(att_scr[...], w_ref[...].astype(BF16))


def _cross(qc, kv, w_co, h1, *, S, n_mem, tm, tn):
    T, D = qc.shape
    per_b = S // tm
    kern = functools.partial(_cross_kernel, hdim=D // CROSS_HEADS)
    return pl.pallas_call(
        kern,
        grid=(T // tm, D // tn),
        in_specs=[
            pl.BlockSpec((tm, D), lambda i, j: (i, 0)),
            pl.BlockSpec((n_mem, D), lambda i, j: (i // per_b, 0)),
            pl.BlockSpec((n_mem, D), lambda i, j: (i // per_b, 1)),
            pl.BlockSpec((D, tn), lambda i, j: (0, j)),
            pl.BlockSpec((tm, tn), lambda i, j: (i, j)),
        ],
        out_specs=pl.BlockSpec((tm, tn), lambda i, j: (i, j)),
        out_shape=jax.ShapeDtypeStruct((T, D), F32),
        scratch_shapes=[pltpu.VMEM((tm, D), BF16)],
        compiler_params=_cparams(("parallel", "arbitrary"), 40),
        name="cross",
    )(qc, kv, kv, w_co, h1)


R_ROWS = SUBLANES + N_EXPERTS


def _router_kernel(h_ref, g_ref, wt_ref, b_ref, eid_ref, gate_ref):
    n = _rms(h_ref[...], g_ref[...])
    nh, nl = _split_bf16(n)
    wh, wl = _split_bf16(wt_ref[...])
    nt = functools.partial(lax.dot_general, dimension_numbers=NT_DIMS, preferred_element_type=F32)
    lg = nt(wh, nh) + nt(wh, nl) + nt(wl, nh) + b_ref[...]

    tm = lg.shape[1]
    row = lax.broadcasted_iota(I32, (SUBLANES, tm), 0)

    def first_argmax(v, vmax):
        return jnp.min(jnp.where(v == vmax, row, SUBLANES), axis=0, keepdims=True)

    gl = jnp.where(row < N_GROUPS, lg[0:SUBLANES], NEG_INF)
    gmax = jnp.max(gl, axis=0, keepdims=True)
    grp = first_argmax(gl, gmax)
    grp_w = 1.0 / jnp.sum(jnp.exp(gl - gmax), axis=0, keepdims=True)

    sel = jnp.zeros((SUBLANES, tm), F32)
    for gi in range(N_GROUPS):
        lo = SUBLANES + gi * EXPERTS_PER_GROUP
        sel = jnp.where(grp == gi, lg[lo:lo + EXPERTS_PER_GROUP], sel)
    e = jnp.exp(sel - jnp.max(sel, axis=0, keepdims=True))
    prob = e / jnp.sum(e, axis=0, keepdims=True)
    p1 = jnp.max(prob, axis=0, keepdims=True)
    i1 = first_argmax(prob, p1)
    rest = jnp.where(row == i1, -1.0, prob)
    p2 = jnp.max(rest, axis=0, keepdims=True)
    i2 = first_argmax(rest, p2)
    den = p1 + p2
    base = grp * EXPERTS_PER_GROUP
    eid_ref[...] = jnp.where(row == 0, base + i1, jnp.where(row == 1, base + i2, 0))
    gate_ref[...] = jnp.where(row == 0, grp_w * p1 / den, jnp.where(row == 1, grp_w * p2 / den, 0.0))


def _router(h2, g_ffn, w_rt, b_r, *, tm):
    T, D = h2.shape
    return pl.pallas_call(
        _router_kernel,
        grid=(T // tm,),
        in_specs=[
            pl.BlockSpec((tm, D), lambda i: (i, 0)),
            pl.BlockSpec((1, D), lambda i: (0, 0)),
            pl.BlockSpec((R_ROWS, D), lambda i: (0, 0)),
            pl.BlockSpec((R_ROWS, 1), lambda i: (0, 0)),
        ],
        out_specs=[
            pl.BlockSpec((SUBLANES, tm), lambda i: (0, i)),
            pl.BlockSpec((SUBLANES, tm), lambda i: (0, i)),
        ],
        out_shape=[
            jax.ShapeDtypeStruct((SUBLANES, T), I32),
            jax.ShapeDtypeStruct((SUBLANES, T), F32),
        ],
        compiler_params=_cparams(("parallel",), 32),
        name="router",
    )(h2, g_ffn, w_rt, b_r)


def _rank_kernel(eid_ref, rank_ref, cnt_ref, carry):
    @pl.when(pl.program_id(0) == 0)
    def _():
        carry[...] = jnp.zeros(carry.shape, F32)

    tm = eid_ref.shape[1]
    e0 = eid_ref[0:1, :]
    e1 = eid_ref[1:2, :]
    erow = lax.broadcasted_iota(I32, (N_EXPERTS, tm), 0)
    hit = jnp.where((erow == e0) | (erow == e1), 1.0, 0.0)
    r = lax.broadcasted_iota(I32, (tm, tm), 0)
    c = lax.broadcasted_iota(I32, (tm, tm), 1)
    before = jnp.where(r < c, 1.0, 0.0).astype(BF16)
    pre = _dot(hit.astype(BF16), before) + carry[:, 0:1]
    rank0 = jnp.sum(jnp.where(erow == e0, pre, 0.0), axis=0, keepdims=True)
    rank1 = jnp.sum(jnp.where(erow == e1, pre, 0.0), axis=0, keepdims=True)
    row = lax.broadcasted_iota(I32, (SUBLANES, tm), 0)
    rank_ref[...] = jnp.where(row == 0, rank0, jnp.where(row == 1, rank1, 0.0)).astype(I32)
    total = carry[...] + jnp.sum(hit, axis=1, keepdims=True)
    carry[...] = total
    cnt_ref[...] = total.astype(I32)


def _rank(eid, *, tm):
    T = eid.shape[1]
    return pl.pallas_call(
        _rank_kernel,
        grid=(T // tm,),
        in_specs=[pl.BlockSpec((SUBLANES, tm), lambda i: (0, i))],
        out_specs=[
            pl.BlockSpec((SUBLANES, tm), lambda i: (0, i)),
            pl.BlockSpec((N_EXPERTS, LANES), lambda i: (0, 0)),
        ],
        out_shape=[
            jax.ShapeDtypeStruct((SUBLANES, T), I32),
            jax.ShapeDtypeStruct((N_EXPERTS, LANES), I32),
        ],
        scratch_shapes=[pltpu.VMEM((N_EXPERTS, LANES), F32)],
        compiler_params=_cparams(("arbitrary",), 32),
        name="rank",
    )(eid)


def _expert_kernel(tok_ref, be_ref, nb_ref, h_hbm, g_ref, wg_ref, wu_ref, wd_ref, y_ref,
                   xbuf, sem, wg_b, wu_b, wd_b, *, tb):
    i = pl.program_id(0)
    nb = nb_ref[0]

    def row_copy(tok, r, slot):
        return pltpu.make_async_copy(h_hbm.at[pl.ds(tok, 1), :], xbuf.at[slot, pl.ds(r, 1), :], sem.at[slot])

    def start_gather(blk, slot):
        def body(r, carry):
            row_copy(tok_ref[blk * tb + r], r, slot).start()
            return carry
        lax.fori_loop(0, tb, body, 0)

    def wait_gather(slot):
        def body(r, carry):
            row_copy(0, r, slot).wait()
            return carry
        lax.fori_loop(0, tb, body, 0)

    @pl.when(i == 0)
    def _():
        start_gather(0, 0)

    @pl.when(i + 1 < nb)
    def _():
        start_gather(i + 1, (i + 1) % 2)

    @pl.when(i < nb)
    def _():
        slot = i % 2
        wait_gather(slot)

        @pl.when((i == 0) | (be_ref[i] != be_ref[jnp.maximum(i - 1, 0)]))
        def _():
            wg_b[...] = wg_ref[0].astype(BF16)
            wu_b[...] = wu_ref[0].astype(BF16)
            wd_b[...] = wd_ref[0].astype(BF16)

        n = _rms(xbuf[slot], g_ref[...]).astype(BF16)
        a = _dot(n, wg_b[...])
        hdn = (a / (1.0 + jnp.exp(-a))) * _dot(n, wu_b[...])
        y_ref[...] = _dot(hdn.astype(BF16), wd_b[...])

    @pl.when(i >= nb)
    def _():
        y_ref[...] = jnp.zeros(y_ref.shape, F32)


def _experts(buf_tok, blk_e, nb, h2, g_ffn, w_gate, w_up, w_down, *, tb, n_blocks):
    T, D = h2.shape
    De = w_gate.shape[2]
    kern = functools.partial(_expert_kernel, tb=tb)

    def w_map(i, tok, be, nbr):
        return (be[jnp.minimum(i, nbr[0] - 1)], 0, 0)

    grid_spec = pltpu.PrefetchScalarGridSpec(
        num_scalar_prefetch=3,
        grid=(n_blocks,),
        in_specs=[
            pl.BlockSpec(memory_space=pl.ANY),
            pl.BlockSpec((1, D), lambda i, tok, be, nbr: (0, 0)),
            pl.BlockSpec((1, D, De), w_map),
            pl.BlockSpec((1, D, De), w_map),
            pl.BlockSpec((1, De, D), w_map),
        ],
        out_specs=pl.BlockSpec((tb, D), lambda i, tok, be, nbr: (i, 0)),
        scratch_shapes=[
            pltpu.VMEM((2, tb, D), F32),
            pltpu.SemaphoreType.DMA((2,)),
            pltpu.VMEM((D, De), BF16),
            pltpu.VMEM((D, De), BF16),
            pltpu.VMEM((De, D), BF16),
        ],
    )
    return pl.pallas_call(
        kern,
        grid_spec=grid_spec,
        out_shape=jax.ShapeDtypeStruct((n_blocks * tb, D), F32),
        compiler_params=_cparams(("arbitrary",), 56),
        name="experts",
    )(buf_tok, blk_e, nb, h2, g_ffn, w_gate, w_up, w_down)


def _combine_kernel(dest_ref, y_hbm, h_ref, gate_ref, o_ref, ybuf, sem, *, tm, n_tok):
    i = pl.program_id(0)
    n = pl.num_programs(0)

    def row_copy(d, k, r, slot):
        return pltpu.make_async_copy(y_hbm.at[pl.ds(d, 1), :], ybuf.at[slot, k, pl.ds(r, 1), :], sem.at[slot])

    def start_gather(blk, slot):
        def body(r, carry):
            for k in range(TOP_K):
                row_copy(dest_ref[k * n_tok + blk * tm + r], k, r, slot).start()
            return carry
        lax.fori_loop(0, tm, body, 0)

    def wait_gather(slot):
        def body(r, carry):
            for k in range(TOP_K):
                row_copy(0, k, r, slot).wait()
            return carry
        lax.fori_loop(0, tm, body, 0)

    @pl.when(i == 0)
    def _():
        start_gather(0, 0)

    @pl.when(i + 1 < n)
    def _():
        start_gather(i + 1, (i + 1) % 2)

    slot = i % 2
    wait_gather(slot)
    gt = gate_ref[...]
    o_ref[...] = h_ref[...] + gt[:, 0:1] * ybuf[slot, 0] + gt[:, 1:2] * ybuf[slot, 1]


def _combine(dest, yb, h2, gate, *, tm):
    T, D = h2.shape
    kern = functools.partial(_combine_kernel, tm=tm, n_tok=T)
    grid_spec = pltpu.PrefetchScalarGridSpec(
        num_scalar_prefetch=1,
        grid=(T // tm,),
        in_specs=[
            pl.BlockSpec(memory_space=pl.ANY),
            pl.BlockSpec((tm, D), lambda i, d: (i, 0)),
            pl.BlockSpec((tm, TOP_K), lambda i, d: (i, 0)),
        ],
        out_specs=pl.BlockSpec((tm, D), lambda i, d: (i, 0)),
        scratch_shapes=[
            pltpu.VMEM((2, TOP_K, tm, D), F32),
            pltpu.SemaphoreType.DMA((2,)),
        ],
    )
    return pl.pallas_call(
        kern,
        grid_spec=grid_spec,
        out_shape=jax.ShapeDtypeStruct((T, D), F32),
        compiler_params=_cparams(("arbitrary",), 40),
        name="combine",
    )(dest, yb, h2, gate)


EXPERT_ROWS = 256


def kernel(x, mem, positions, g_attn, w_in, q_norm_g, k_norm_g, lambda_q1, lambda_k1, lambda_q2, lambda_k2, diff_subln_g, gla_w_a2, gla_b_a, gla_out_g, w_out, g_cross, g_mem, w_cq, w_ckv, cq_norm_g, ck_norm_g, w_co, g_ffn, w_router_grp, b_router_grp, w_router_exp, b_router_exp, w_gate, w_up, w_down):
    B, S, D = x.shape
    T = B * S
    n_mem = mem.shape[1]
    l = 0
    x2 = x.reshape(T, D)

    half = DIFF_QKDIM // 2
    freq = ROPE_THETA ** (-jnp.arange(half, dtype=F32) / half)
    freq = jnp.tile(freq, LANES // half)[None, :]
    qkg = jnp.stack([jnp.tile(q_norm_g[l], 2) * (DIFF_QKDIM ** -0.5), jnp.tile(k_norm_g[l], 2)])
    lr0 = J_LR * TN
    w_lr = jnp.pad(w_in[l][:, lr0:lr0 + GLA_GATE_RANK], ((0, 0), (0, LANES - GLA_GATE_RANK))).astype(BF16)
    w_gr = w_in[l][:, lr0 + GLA_GATE_RANK:]
    w_a2 = jnp.pad(gla_w_a2[l], ((0, LANES - GLA_GATE_RANK), (0, 0))).astype(BF16)
    lvec = jnp.stack([lambda_q1[l], lambda_k1[l], lambda_q2[l], lambda_k2[l]])

    qk, mid, log_a, sgr = _inproj(x2, g_attn[l][None], positions.reshape(T, 1), freq, qkg, w_in[l], w_gr,
                                  w_lr, w_a2, gla_b_a[l][None], tm=1024)
    mix_d = _diffattn(lvec, qk, mid, diff_subln_g[l][None], B=B, S=S, tq=512)
    mix_g = _gla(mid, log_a, sgr, gla_out_g[l][None], B=B, S=S, blk=512)
    h1 = _outproj(mix_d, mix_g, w_out[l], x2, tm=1024, tn=512)

    hdim = D // CROSS_HEADS
    qc = _normproj(h1, g_cross[l][None], w_cq[l], cq_norm_g[l][None] * (hdim ** -0.5),
                   tm=1024, tn=hdim, n_norm=CROSS_HEADS, name="cq")
    kv = _normproj(mem.reshape(B * n_mem, D), g_mem[l][None], w_ckv[l], ck_norm_g[l][None],
                   tm=B * n_mem, tn=hdim, n_norm=CROSS_HEADS, name="ckv")
    h2 = _cross(qc, kv, w_co[l], h1, S=S, n_mem=n_mem, tm=512, tn=512)

    w_rt = jnp.concatenate([w_router_grp[l].T, jnp.zeros((SUBLANES - N_GROUPS, D), F32), w_router_exp[l].T])
    b_r = jnp.concatenate([b_router_grp[l], jnp.zeros((SUBLANES - N_GROUPS,), F32), b_router_exp[l]])[:, None]
    eid, gate = _router(h2, g_ffn[l][None], w_rt, b_r, tm=512)
    rank, cnt = _rank(eid, tm=512)

    tb = EXPERT_ROWS
    n_blocks = (T * TOP_K + N_EXPERTS * (tb - 1) + tb - 1) // tb
    counts = cnt[:, 0]
    pcounts = ((counts + tb - 1) // tb) * tb
    pends = jnp.cumsum(pcounts)
    pstarts = pends - pcounts
    nb = (pends[-1:] // tb).astype(I32)
    blk_e = jnp.minimum(jnp.searchsorted(pends, jnp.arange(n_blocks, dtype=I32) * tb, side='right'),
                        N_EXPERTS - 1).astype(I32)
    dest = (pstarts[eid[:TOP_K]] + rank[:TOP_K]).astype(I32).reshape(-1)
    tok = jnp.tile(jnp.arange(T, dtype=I32), TOP_K)
    buf_tok = jnp.zeros((n_blocks * tb,), I32).at[dest].set(tok)

    yb = _experts(buf_tok, blk_e, nb, h2, g_ffn[l][None], w_gate[l], w_up[l], w_down[l],
                  tb=tb, n_blocks=n_blocks)
    out = _combine(dest, yb, h2, gate[:TOP_K].T, tm=256)
    return out.reshape(B, S, D)
```

```python
import functools
import math

import jax
import jax.numpy as jnp
from jax import lax
from jax.experimental import pallas as pl
from jax.experimental.pallas import tpu as pltpu

F32 = jnp.float32
BF16 = jnp.bfloat16
I32 = jnp.int32

LANES = 128
SUBLANES = 8

CHUNK = 64
ROPE_THETA = 10000.0
NORM_EPS = 1e-6
NEG_INF = -1e30
DIFF_HEADS = 8
DIFF_VDIM = 128
DIFF_QKDIM = 64
GLA_HEADS = 4
GLA_VDIM = 256
GLA_KDIM = 128
GLA_GATE_RANK = 16
GLA_TAU = 16.0
CROSS_HEADS = 4
N_GROUPS = 4
EXPERTS_PER_GROUP = 8
N_EXPERTS = N_GROUPS * EXPERTS_PER_GROUP
TOP_K = 2
LAM_INIT = 0.8 - 0.6 * math.exp(-0.3 * 0)

NT_DIMS = (((1,), (1,)), ((), ()))


def _cparams(semantics, vmem_mib):
    return pltpu.CompilerParams(dimension_semantics=semantics,
                                vmem_limit_bytes=vmem_mib * 1024 * 1024)


def _dot(a, b):
    return jnp.dot(a, b, preferred_element_type=F32)


def _dot_nt(a, b):
    return lax.dot_general(a, b, NT_DIMS, preferred_element_type=F32)


def _rms(x, g):
    ms = jnp.mean(x * x, axis=-1, keepdims=True)
    return x * lax.rsqrt(ms + NORM_EPS) * g


def _split_bf16(x):
    hi = x.astype(BF16)
    lo = (x - hi.astype(F32)).astype(BF16)
    return hi, lo


TN = 512
J_QK = 4
J_MID = 6
J_LR = J_QK + J_MID
J_GR = J_LR + 1
N_J = J_GR + 2


def _inproj_kernel(x_ref, g_ref, pos_ref, freq_ref, qkg_ref, w_ref, wgr_ref, wlr_ref, wa2_ref, ba_ref,
                   qk_ref, mid_ref, loga_ref, sgr_ref, n_scr, cos_scr, sin_scr):
    j = pl.program_id(1)

    @pl.when(j == 0)
    def _():
        n_scr[...] = _rms(x_ref[...], g_ref[...]).astype(BF16)
        ang = pos_ref[...].astype(F32) * freq_ref[...]
        cos_scr[...] = jnp.cos(ang)
        sin_scr[...] = jnp.sin(ang)

    @pl.when(j < J_QK)
    def _():
        y = _dot_nt(n_scr[...], w_ref[...].astype(BF16))
        lane = lax.broadcasted_iota(I32, (1, LANES), 1)
        low_seg = lane < DIFF_QKDIM
        first_half = (lane % DIFF_QKDIM) < (DIFF_QKDIM // 2)
        gain = qkg_ref[pl.ds(j // (J_QK // 2), 1), :]
        cos = cos_scr[...]
        sin = sin_scr[...]
        for c in range(TN // LANES):
            yb = y[:, c * LANES:(c + 1) * LANES]
            y2 = yb * yb
            s_lo = jnp.sum(jnp.where(low_seg, y2, 0.0), axis=-1, keepdims=True)
            s_hi = jnp.sum(jnp.where(low_seg, 0.0, y2), axis=-1, keepdims=True)
            ms = jnp.where(low_seg, s_lo, s_hi) * (1.0 / DIFF_QKDIM)
            yn = yb * lax.rsqrt(ms + NORM_EPS) * gain
            rot = jnp.where(first_half,
                            -pltpu.roll(yn, LANES - DIFF_QKDIM // 2, 1),
                            pltpu.roll(yn, DIFF_QKDIM // 2, 1))
            qk_ref[:, c * LANES:(c + 1) * LANES] = (yn * cos + rot * sin).astype(BF16)

    @pl.when((j >= J_QK) & (j < J_LR))
    def _():
        mid_ref[...] = _dot_nt(n_scr[...], w_ref[...].astype(BF16)).astype(BF16)

    @pl.when(j == J_LR)
    def _():
        lr = _dot_nt(n_scr[...], wlr_ref[...].astype(BF16))
        z = _dot(lr.astype(BF16), wa2_ref[...].astype(BF16)) + ba_ref[...]
        log_sig = jnp.minimum(z, 0.0) - jnp.log(1.0 + jnp.exp(-jnp.abs(z)))
        loga_ref[...] = log_sig * (1.0 / GLA_TAU)

    @pl.when(j >= J_GR)
    def _():
        y = _dot_nt(n_scr[...], wgr_ref[...].astype(BF16))
        sgr_ref[...] = (y / (1.0 + jnp.exp(-y))).astype(BF16)


def _inproj(x2, g_attn, pos2, freq, qkg, w_t, w_a2, b_a, *, tm):
    T, D = x2.shape
    n_mid = J_MID * TN
    n_gk = GLA_HEADS * GLA_KDIM
    lr0 = J_LR * TN
    gr0 = lr0 + GLA_GATE_RANK
    n_gr = w_t.shape[0] - gr0
    assert n_gr == 2 * TN and lr0 % GLA_GATE_RANK == 0
    return pl.pallas_call(
        _inproj_kernel,
        grid=(T // tm, N_J),
        in_specs=[
            pl.BlockSpec((tm, D), lambda i, j: (i, 0)),
            pl.BlockSpec((1, D), lambda i, j: (0, 0)),
            pl.BlockSpec((tm, 1), lambda i, j: (i, 0)),
            pl.BlockSpec((1, LANES), lambda i, j: (0, 0)),
            pl.BlockSpec((2, LANES), lambda i, j: (0, 0)),
            pl.BlockSpec((TN, D), lambda i, j: (jnp.minimum(j, J_LR - 1), 0)),
            pl.BlockSpec((pl.Element(TN), pl.Element(D)),
                         lambda i, j: (pl.multiple_of(gr0 + TN * jnp.clip(j - J_GR, 0, 1), SUBLANES), 0)),
            pl.BlockSpec((GLA_GATE_RANK, D), lambda i, j: (lr0 // GLA_GATE_RANK, 0)),
            pl.BlockSpec((GLA_GATE_RANK, n_gk), lambda i, j: (0, 0)),
            pl.BlockSpec((1, n_gk), lambda i, j: (0, 0)),
        ],
        out_specs=[
            pl.BlockSpec((tm, TN), lambda i, j: (i, jnp.minimum(j, J_QK - 1))),
            pl.BlockSpec((tm, TN), lambda i, j: (i, jnp.clip(j - J_QK, 0, J_MID - 1))),
            pl.BlockSpec((tm, n_gk), lambda i, j: (i, 0)),
            pl.BlockSpec((tm, TN), lambda i, j: (i, jnp.clip(j - J_GR, 0, 1))),
        ],
        out_shape=[
            jax.ShapeDtypeStruct((T, J_QK * TN), BF16),
            jax.ShapeDtypeStruct((T, n_mid), BF16),
            jax.ShapeDtypeStruct((T, n_gk), F32),
            jax.ShapeDtypeStruct((T, n_gr), BF16),
        ],
        scratch_shapes=[
            pltpu.VMEM((tm, D), BF16),
            pltpu.VMEM((tm, LANES), F32),
            pltpu.VMEM((tm, LANES), F32),
        ],
        compiler_params=_cparams(("parallel", "arbitrary"), 56),
        name="inproj",
    )(x2, g_attn, pos2, freq, qkg, w_t, w_t, w_t, w_a2, b_a)


SCORE_BOUND = 80.0


def _diffattn_kernel(lv_ref, q_ref, k_ref, v_ref, sg_ref, o_ref, vext, diag_mask, acc1, acc2, m1, m2,
                     *, tq, bounded):
    i = pl.program_id(2)
    lv = lv_ref[...]
    lam = (jnp.exp(jnp.sum(lv[0:1] * lv[1:2], axis=-1, keepdims=True))
           - jnp.exp(jnp.sum(lv[2:3] * lv[3:4], axis=-1, keepdims=True)) + LAM_INIT)

    @pl.when(i == 0)
    def _():
        S = v_ref.shape[0]
        lane_s = lax.broadcasted_iota(I32, (S, LANES), 1)
        vext[:, 0:DIFF_VDIM] = v_ref[...]
        vext[:, DIFF_VDIM:] = jnp.where(lane_s == 0, 1.0, 0.0).astype(BF16)
        row_chunk = lax.broadcasted_iota(I32, (tq, tq), 0) // CHUNK
        col_chunk = lax.broadcasted_iota(I32, (tq, tq), 1) // CHUNK
        diag_mask[...] = jnp.where(col_chunk <= row_chunk, 1.0, 0.0).astype(BF16)

    q = q_ref[...]
    lane = lax.broadcasted_iota(I32, (1, LANES), 1)
    zero = jnp.zeros_like(q)
    q1 = jnp.where(lane < DIFF_QKDIM, q, zero)
    q2 = jnp.where(lane < DIFF_QKDIM, zero, q)

    acc1[...] = jnp.zeros(acc1.shape, F32)
    acc2[...] = jnp.zeros(acc2.shape, F32)
    if not bounded:
        m1[...] = jnp.full(m1.shape, NEG_INF, F32)
        m2[...] = jnp.full(m2.shape, NEG_INF, F32)

    def step(j, masked):
        start = pl.multiple_of(j * tq, tq)
        k = k_ref[pl.ds(start, tq), :]
        v = vext[pl.ds(start, tq), :]
        for qc, acc, m in ((q1, acc1, m1), (q2, acc2, m2)):
            s = lax.dot_general(qc, k, NT_DIMS, preferred_element_type=F32)
            if bounded:
                p = jnp.exp2(s).astype(BF16)
                if masked:
                    p = p * diag_mask[...]
                acc[...] += _dot(p, v)
            else:
                if masked:
                    s = jnp.where(diag_mask[...] > 0, s, NEG_INF)
                m_old = m[...]
                m_new = jnp.maximum(m_old, jnp.max(s, axis=-1, keepdims=True))
                p = jnp.exp2(s - m_new)
                acc[...] = jnp.exp2(m_old - m_new) * acc[...] + _dot(p.astype(BF16), v)
                m[...] = m_new

    def body(j, carry):
        step(j, False)
        return carry

    lax.fori_loop(0, i, body, 0)
    step(i, True)

    a1 = acc1[...]
    a2 = acc2[...]
    o = (a1[:, :DIFF_VDIM] / a1[:, DIFF_VDIM:DIFF_VDIM + 1]
         - lam * (a2[:, :DIFF_VDIM] / a2[:, DIFF_VDIM:DIFF_VDIM + 1]))
    o_ref[...] = (_rms(o, sg_ref[...]) * (1.0 - LAM_INIT)).astype(BF16)


def _diffattn(lvec, qk, mid, subln_g, *, B, S, tq, bounded):
    T = B * S
    nq = S // tq
    kern = functools.partial(_diffattn_kernel, tq=tq, bounded=bounded)
    return pl.pallas_call(
        kern,
        grid=(B, DIFF_HEADS, nq),
        in_specs=[
            pl.BlockSpec((4, DIFF_QKDIM), lambda b, h, i: (0, 0)),
            pl.BlockSpec((tq, LANES), lambda b, h, i: (b * nq + i, h)),
            pl.BlockSpec((S, LANES), lambda b, h, i: (b, DIFF_HEADS + h)),
            pl.BlockSpec((S, LANES), lambda b, h, i: (b, h)),
            pl.BlockSpec((1, DIFF_VDIM), lambda b, h, i: (0, 0)),
        ],
        out_specs=pl.BlockSpec((tq, DIFF_VDIM), lambda b, h, i: (b * nq + i, h)),
        out_shape=jax.ShapeDtypeStruct((T, DIFF_HEADS * DIFF_VDIM), BF16),
        scratch_shapes=[
            pltpu.VMEM((S, 2 * DIFF_VDIM), BF16),
            pltpu.VMEM((tq, tq), BF16),
            pltpu.VMEM((tq, 2 * DIFF_VDIM), F32),
            pltpu.VMEM((tq, 2 * DIFF_VDIM), F32),
            pltpu.VMEM((tq, 1), F32),
            pltpu.VMEM((tq, 1), F32),
        ],
        compiler_params=_cparams(("parallel", "parallel", "arbitrary"), 32),
        name="diffattn_bounded" if bounded else "diffattn_online",
    )(lvec, qk, qk, mid, subln_g)


def _gla_kernel(q_ref, k_ref, v_ref, la_ref, sgr_ref, g_ref, o_ref, state, *, blk):
    @pl.when(pl.program_id(2) == 0)
    def _():
        state[...] = jnp.zeros(state.shape, F32)

    la_t = la_ref[...].T
    k_t = k_ref[...].astype(F32).T
    r = lax.broadcasted_iota(I32, (blk, blk), 0)
    c = lax.broadcasted_iota(I32, (blk, blk), 1)
    same = (r // CHUNK) == (c // CHUNK)
    tri = jnp.where(same & (r <= c), 1.0, 0.0).astype(BF16)
    ones = jnp.where(same, 1.0, 0.0).astype(BF16)
    hi, lo = _split_bf16(la_t)
    cum_t = _dot(hi, tri) + _dot(lo, tri)
    tot_t = _dot(hi, ones) + _dot(lo, ones)
    kd_t = k_t * jnp.exp(tot_t - cum_t)

    lane = lax.broadcasted_iota(I32, (1, LANES), 1)
    scale = GLA_KDIM ** -0.5
    for ck in range(blk // CHUNK):
        pair = slice((ck // 2) * LANES, (ck // 2 + 1) * LANES)
        rows = slice(ck * CHUNK, (ck + 1) * CHUNK)
        in_chunk = (lane // CHUNK) == (ck % 2)
        kd = jnp.where(in_chunk, kd_t[:, pair], 0.0).astype(BF16)
        d_state = _dot(kd, v_ref[pair, :])
        decay = jnp.exp(tot_t[:, ck * CHUNK:ck * CHUNK + 1])
        st = decay * state[...] + d_state
        state[...] = st
        o = _dot(q_ref[rows, :], st.astype(BF16)) * scale
        gate = sgr_ref[rows, :].astype(F32)
        o_ref[rows, :] = (_rms(o, g_ref[...]) * gate).astype(BF16)


def _gla(mid, log_a, sgr, out_g, *, B, S, blk):
    T = B * S
    ns = S // blk
    kern = functools.partial(_gla_kernel, blk=blk)
    q_col0 = (DIFF_HEADS * DIFF_VDIM) // GLA_KDIM
    k_col0 = q_col0 + GLA_HEADS
    v_col0 = (DIFF_HEADS * DIFF_VDIM + 2 * GLA_HEADS * GLA_KDIM) // GLA_VDIM
    return pl.pallas_call(
        kern,
        grid=(B, GLA_HEADS, ns),
        in_specs=[
            pl.BlockSpec((blk, GLA_KDIM), lambda b, h, s: (b * ns + s, q_col0 + h)),
            pl.BlockSpec((blk, GLA_KDIM), lambda b, h, s: (b * ns + s, k_col0 + h)),
            pl.BlockSpec((blk, GLA_VDIM), lambda b, h, s: (b * ns + s, v_col0 + h)),
            pl.BlockSpec((blk, GLA_KDIM), lambda b, h, s: (b * ns + s, h)),
            pl.BlockSpec((blk, GLA_VDIM), lambda b, h, s: (b * ns + s, h)),
            pl.BlockSpec((1, GLA_VDIM), lambda b, h, s: (0, 0)),
        ],
        out_specs=pl.BlockSpec((blk, GLA_VDIM), lambda b, h, s: (b * ns + s, h)),
        out_shape=jax.ShapeDtypeStruct((T, GLA_HEADS * GLA_VDIM), BF16),
        scratch_shapes=[pltpu.VMEM((GLA_KDIM, GLA_VDIM), F32)],
        compiler_params=_cparams(("parallel", "parallel", "arbitrary"), 32),
        name="gla",
    )(mid, mid, mid, log_a, sgr, out_g)


def _outproj_kernel(a_ref, b_ref, wa_ref, wb_ref, x_ref, o_ref):
    acc = _dot(a_ref[...], wa_ref[...].astype(BF16)) + _dot(b_ref[...], wb_ref[...].astype(BF16))
    o_ref[...] = x_ref[...] + acc


def _outproj(a, b, w_out, x2, *, tm, tn):
    T, ka = a.shape
    kb = b.shape[1]
    assert ka == kb
    D = w_out.shape[1]
    return pl.pallas_call(
        _outproj_kernel,
        grid=(T // tm, D // tn),
        in_specs=[
            pl.BlockSpec((tm, ka), lambda i, j: (i, 0)),
            pl.BlockSpec((tm, kb), lambda i, j: (i, 0)),
            pl.BlockSpec((ka, tn), lambda i, j: (0, j)),
            pl.BlockSpec((kb, tn), lambda i, j: (1, j)),
            pl.BlockSpec((tm, tn), lambda i, j: (i, j)),
        ],
        out_specs=pl.BlockSpec((tm, tn), lambda i, j: (i, j)),
        out_shape=jax.ShapeDtypeStruct((T, D), F32),
        compiler_params=_cparams(("parallel", "arbitrary"), 40),
        name="outproj",
    )(a, b, w_out, w_out, x2)


def _normproj_kernel(x_ref, g_ref, w_ref, hg_ref, o_ref, n_scr, *, n_norm):
    j = pl.program_id(1)

    @pl.when(j == 0)
    def _():
        n_scr[...] = _rms(x_ref[...], g_ref[...]).astype(BF16)

    y = _dot(n_scr[...], w_ref[...].astype(BF16))

    @pl.when(j < n_norm)
    def _():
        o_ref[...] = _rms(y, hg_ref[...]).astype(BF16)

    @pl.when(j >= n_norm)
    def _():
        o_ref[...] = y.astype(BF16)


def _normproj(x2, g, w, head_g, *, tm, tn, n_norm, name):
    T, D = x2.shape
    N = w.shape[1]
    kern = functools.partial(_normproj_kernel, n_norm=n_norm)
    return pl.pallas_call(
        kern,
        grid=(T // tm, N // tn),
        in_specs=[
            pl.BlockSpec((tm, D), lambda i, j: (i, 0)),
            pl.BlockSpec((1, D), lambda i, j: (0, 0)),
            pl.BlockSpec((D, tn), lambda i, j: (0, j)),
            pl.BlockSpec((1, tn), lambda i, j: (0, 0)),
        ],
        out_specs=pl.BlockSpec((tm, tn), lambda i, j: (i, j)),
        out_shape=jax.ShapeDtypeStruct((T, N), BF16),
        scratch_shapes=[pltpu.VMEM((tm, D), BF16)],
        compiler_params=_cparams(("parallel", "arbitrary"), 48),
        name=name,
    )(x2, g, w, head_g)


def _cross_kernel(q_ref, k_ref, v_ref, w_ref, h_ref, o_ref, att_scr, *, hdim):
    @pl.when(pl.program_id(1) == 0)
    def _():
        for hd in range(CROSS_HEADS):
            cols = slice(hd * hdim, (hd + 1) * hdim)
            s = lax.dot_general(q_ref[:, cols], k_ref[:, cols], NT_DIMS, preferred_element_type=F32)
            p = jnp.exp(s - jnp.max(s, axis=-1, keepdims=True))
            l = jnp.sum(p, axis=-1, keepdims=True)
            att_scr[:, cols] = (_dot(p.astype(BF16), v_ref[:, cols]) / l).astype(BF16)

    o_ref[...] = h_ref[...] + _dot(att_scr[...], w_ref[...].astype(BF16))


def _cross(qc, kv, w_co, h1, *, S, n_mem, tm, tn):
    T, D = qc.shape
    per_b = S // tm
    kern = functools.partial(_cross_kernel, hdim=D // CROSS_HEADS)
    return pl.pallas_call(
        kern,
        grid=(T // tm, D // tn),
        in_specs=[
            pl.BlockSpec((tm, D), lambda i, j: (i, 0)),
            pl.BlockSpec((n_mem, D), lambda i, j: (i // per_b, 0)),
            pl.BlockSpec((n_mem, D), lambda i, j: (i // per_b, 1)),
            pl.BlockSpec((D, tn), lambda i, j: (0, j)),
            pl.BlockSpec((tm, tn), lambda i, j: (i, j)),
        ],
        out_specs=pl.BlockSpec((tm, tn), lambda i, j: (i, j)),
        out_shape=jax.ShapeDtypeStruct((T, D), F32),
        scratch_shapes=[pltpu.VMEM((tm, D), BF16)],
        compiler_params=_cparams(("parallel", "arbitrary"), 40),
        name="cross",
    )(qc, kv, kv, w_co, h1)


R_ROWS = SUBLANES + N_EXPERTS


def _router_kernel(h_ref, g_ref, wt_ref, b_ref, eid_ref, gate_ref):
    n = _rms(h_ref[...], g_ref[...])
    nh, nl = _split_bf16(n)
    wh, wl = _split_bf16(wt_ref[...])
    nt = functools.partial(lax.dot_general, dimension_numbers=NT_DIMS, preferred_element_type=F32)
    lg = nt(wh, nh) + nt(wh, nl) + nt(wl, nh) + b_ref[...]

    tm = lg.shape[1]
    row = lax.broadcasted_iota(I32, (SUBLANES, tm), 0)

    def first_argmax(v, vmax):
        return jnp.min(jnp.where(v == vmax, row, SUBLANES), axis=0, keepdims=True)

    gl = jnp.where(row < N_GROUPS, lg[0:SUBLANES], NEG_INF)
    gmax = jnp.max(gl, axis=0, keepdims=True)
    grp = first_argmax(gl, gmax)
    grp_w = 1.0 / jnp.sum(jnp.exp(gl - gmax), axis=0, keepdims=True)

    sel = jnp.zeros((SUBLANES, tm), F32)
    for gi in range(N_GROUPS):
        lo = SUBLANES + gi * EXPERTS_PER_GROUP
        sel = jnp.where(grp == gi, lg[lo:lo + EXPERTS_PER_GROUP], sel)
    e = jnp.exp(sel - jnp.max(sel, axis=0, keepdims=True))
    prob = e / jnp.sum(e, axis=0, keepdims=True)
    p1 = jnp.max(prob, axis=0, keepdims=True)
    i1 = first_argmax(prob, p1)
    rest = jnp.where(row == i1, -1.0, prob)
    p2 = jnp.max(rest, axis=0, keepdims=True)
    i2 = first_argmax(rest, p2)
    den = p1 + p2
    base = grp * EXPERTS_PER_GROUP
    eid_ref[...] = jnp.where(row == 0, base + i1, jnp.where(row == 1, base + i2, 0))
    gate_ref[...] = jnp.where(row == 0, grp_w * p1 / den, jnp.where(row == 1, grp_w * p2 / den, 0.0))


def _router(h2, g_ffn, w_rt, b_r, *, tm):
    T, D = h2.shape
    return pl.pallas_call(
        _router_kernel,
        grid=(T // tm,),
        in_specs=[
            pl.BlockSpec((tm, D), lambda i: (i, 0)),
            pl.BlockSpec((1, D), lambda i: (0, 0)),
            pl.BlockSpec((R_ROWS, D), lambda i: (0, 0)),
            pl.BlockSpec((R_ROWS, 1), lambda i: (0, 0)),
        ],
        out_specs=[
            pl.BlockSpec((SUBLANES, tm), lambda i: (0, i)),
            pl.BlockSpec((SUBLANES, tm), lambda i: (0, i)),
        ],
        out_shape=[
            jax.ShapeDtypeStruct((SUBLANES, T), I32),
            jax.ShapeDtypeStruct((SUBLANES, T), F32),
        ],
        compiler_params=_cparams(("parallel",), 32),
        name="router",
    )(h2, g_ffn, w_rt, b_r)


def _rank_kernel(eid_ref, rank_ref, cnt_ref, carry):
    @pl.when(pl.program_id(0) == 0)
    def _():
        carry[...] = jnp.zeros(carry.shape, F32)

    tm = eid_ref.shape[1]
    e0 = eid_ref[0:1, :]
    e1 = eid_ref[1:2, :]
    erow = lax.broadcasted_iota(I32, (N_EXPERTS, tm), 0)
    hit = jnp.where((erow == e0) | (erow == e1), 1.0, 0.0)
    r = lax.broadcasted_iota(I32, (tm, tm), 0)
    c = lax.broadcasted_iota(I32, (tm, tm), 1)
    before = jnp.where(r < c, 1.0, 0.0).astype(BF16)
    pre = _dot(hit.astype(BF16), before) + carry[:, 0:1]
    rank0 = jnp.sum(jnp.where(erow == e0, pre, 0.0), axis=0, keepdims=True)
    rank1 = jnp.sum(jnp.where(erow == e1, pre, 0.0), axis=0, keepdims=True)
    row = lax.broadcasted_iota(I32, (SUBLANES, tm), 0)
    rank_ref[...] = jnp.where(row == 0, rank0, jnp.where(row == 1, rank1, 0.0)).astype(I32)
    total = carry[...] + jnp.sum(hit, axis=1, keepdims=True)
    carry[...] = total
    cnt_ref[...] = total.astype(I32)


def _rank(eid, *, tm):
    T = eid.shape[1]
    return pl.pallas_call(
        _rank_kernel,
        grid=(T // tm,),
        in_specs=[pl.BlockSpec((SUBLANES, tm), lambda i: (0, i))],
        out_specs=[
            pl.BlockSpec((SUBLANES, tm), lambda i: (0, i)),
            pl.BlockSpec((N_EXPERTS, LANES), lambda i: (0, 0)),
        ],
        out_shape=[
            jax.ShapeDtypeStruct((SUBLANES, T), I32),
            jax.ShapeDtypeStruct((N_EXPERTS, LANES), I32),
        ],
        scratch_shapes=[pltpu.VMEM((N_EXPERTS, LANES), F32)],
        compiler_params=_cparams(("arbitrary",), 32),
        name="rank",
    )(eid)


def _expert_kernel(tok_ref, be_ref, nb_ref, h_hbm, g_ref, wg_ref, wu_ref, wd_ref, y_ref,
                   xbuf, sem, wg_b, wu_b, wd_b, *, tb, n_blocks):
    i = pl.program_id(0)
    nb = nb_ref[0]
    slot = i % 2

    def row_copy(tok, r, s):
        return pltpu.make_async_copy(h_hbm.at[pl.ds(tok, 1), :], xbuf.at[s, pl.ds(r, 1), :], sem.at[s])

    @pl.when(i == 0)
    def _():
        def body(r, carry):
            row_copy(tok_ref[r], r, 0).start()
            return carry
        lax.fori_loop(0, tb, body, 0)

    @pl.when(i <= nb)
    def _():
        pltpu.make_async_copy(h_hbm.at[pl.ds(0, tb), :], xbuf.at[slot], sem.at[slot]).wait()

    @pl.when((i < nb) & ((i == 0) | (be_ref[jnp.minimum(i, n_blocks - 1)] != be_ref[jnp.maximum(i - 1, 0)])))
    def _():
        wg_b[...] = wg_ref[0].astype(BF16)
        wu_b[...] = wu_ref[0].astype(BF16)
        wd_b[...] = wd_ref[0].astype(BF16)

    @pl.when(i < nb)
    def _():
        n = _rms(xbuf[slot], g_ref[...]).astype(BF16)
        base = (i + 1) * tb
        for r in range(tb):
            row_copy(tok_ref[base + r], r, 1 - slot).start()
        a = _dot(n, wg_b[...])
        hdn = (a / (1.0 + jnp.exp(-a))) * _dot(n, wu_b[...])
        y_ref[...] = _dot(hdn.astype(BF16), wd_b[...])

    @pl.when((i >= nb) & (i < n_blocks))
    def _():
        y_ref[...] = jnp.zeros(y_ref.shape, F32)


def _experts(buf_tok, blk_e, nb, h2, g_ffn, w_gate, w_up, w_down, *, tb, n_blocks):
    T, D = h2.shape
    De = w_gate.shape[2]
    kern = functools.partial(_expert_kernel, tb=tb, n_blocks=n_blocks)

    def w_map(i, tok, be, nbr):
        return (be[jnp.minimum(i, nbr[0] - 1)], 0, 0)

    grid_spec = pltpu.PrefetchScalarGridSpec(
        num_scalar_prefetch=3,
        grid=(n_blocks + 1,),
        in_specs=[
            pl.BlockSpec(memory_space=pl.ANY),
            pl.BlockSpec((1, D), lambda i, tok, be, nbr: (0, 0)),
            pl.BlockSpec((1, D, De), w_map),
            pl.BlockSpec((1, D, De), w_map),
            pl.BlockSpec((1, De, D), w_map),
        ],
        out_specs=pl.BlockSpec((tb, D), lambda i, tok, be, nbr: (jnp.minimum(i, n_blocks - 1), 0)),
        scratch_shapes=[
            pltpu.VMEM((2, tb, D), F32),
            pltpu.SemaphoreType.DMA((2,)),
            pltpu.VMEM((D, De), BF16),
            pltpu.VMEM((D, De), BF16),
            pltpu.VMEM((De, D), BF16),
        ],
    )
    return pl.pallas_call(
        kern,
        grid_spec=grid_spec,
        out_shape=jax.ShapeDtypeStruct((n_blocks * tb, D), F32),
        compiler_params=_cparams(("arbitrary",), 56),
        name="experts",
    )(buf_tok, blk_e, nb, h2, g_ffn, w_gate, w_up, w_down)


def _combine_kernel(dest_ref, y_hbm, h_ref, gate_ref, o_ref, ybuf, sem, *, tm, n_tok):
    i = pl.program_id(0)
    n = pl.num_programs(0)

    def row_copy(d, k, r, slot):
        return pltpu.make_async_copy(y_hbm.at[pl.ds(d, 1), :], ybuf.at[slot, k, pl.ds(r, 1), :], sem.at[slot])

    def start_gather(blk, slot):
        def body(r, carry):
            for k in range(TOP_K):
                row_copy(dest_ref[k * n_tok + blk * tm + r], k, r, slot).start()
            return carry
        lax.fori_loop(0, tm, body, 0, unroll=8)

    def wait_gather(slot):
        for k in range(TOP_K):
            pltpu.make_async_copy(y_hbm.at[pl.ds(0, tm), :], ybuf.at[slot, k], sem.at[slot]).wait()

    @pl.when(i == 0)
    def _():
        start_gather(0, 0)

    @pl.when(i + 1 < n)
    def _():
        start_gather(i + 1, (i + 1) % 2)

    slot = i % 2
    wait_gather(slot)
    gt = gate_ref[...]
    o_ref[...] = h_ref[...] + gt[:, 0:1] * ybuf[slot, 0] + gt[:, 1:2] * ybuf[slot, 1]


def _combine(dest, yb, h2, gate, *, tm):
    T, D = h2.shape
    kern = functools.partial(_combine_kernel, tm=tm, n_tok=T)
    grid_spec = pltpu.PrefetchScalarGridSpec(
        num_scalar_prefetch=1,
        grid=(T // tm,),
        in_specs=[
            pl.BlockSpec(memory_space=pl.ANY),
            pl.BlockSpec((tm, D), lambda i, d: (i, 0)),
            pl.BlockSpec((tm, TOP_K), lambda i, d: (i, 0)),
        ],
        out_specs=pl.BlockSpec((tm, D), lambda i, d: (i, 0)),
        scratch_shapes=[
            pltpu.VMEM((2, TOP_K, tm, D), F32),
            pltpu.SemaphoreType.DMA((2,)),
        ],
    )
    return pl.pallas_call(
        kern,
        grid_spec=grid_spec,
        out_shape=jax.ShapeDtypeStruct((T, D), F32),
        compiler_params=_cparams(("arbitrary",), 40),
        name="combine",
    )(dest, yb, h2, gate)


EXPERT_ROWS = 256


def kernel(x, mem, positions, g_attn, w_in, q_norm_g, k_norm_g, lambda_q1, lambda_k1, lambda_q2, lambda_k2, diff_subln_g, gla_w_a2, gla_b_a, gla_out_g, w_out, g_cross, g_mem, w_cq, w_ckv, cq_norm_g, ck_norm_g, w_co, g_ffn, w_router_grp, b_router_grp, w_router_exp, b_router_exp, w_gate, w_up, w_down):
    B, S, D = x.shape
    T = B * S
    n_mem = mem.shape[1]
    l = 0
    x2 = x.reshape(T, D)

    half = DIFF_QKDIM // 2
    freq = ROPE_THETA ** (-jnp.arange(half, dtype=F32) / half)
    freq = jnp.tile(freq, LANES // half)[None, :]
    q_scale = math.log2(math.e) * DIFF_QKDIM ** -0.5
    qkg = jnp.stack([jnp.tile(q_norm_g[l], 2) * q_scale, jnp.tile(k_norm_g[l], 2)])
    score_bound = 1.01 * DIFF_QKDIM * q_scale * jnp.max(jnp.abs(q_norm_g[l])) * jnp.max(jnp.abs(k_norm_g[l]))
    lvec = jnp.stack([lambda_q1[l], lambda_k1[l], lambda_q2[l], lambda_k2[l]])

    qk, mid, log_a, sgr = _inproj(x2, g_attn[l][None], positions.reshape(T, 1), freq, qkg, w_in[l].T,
                                  gla_w_a2[l], gla_b_a[l][None], tm=1024)
    diffattn = functools.partial(_diffattn, lvec, qk, mid, diff_subln_g[l][None], B=B, S=S, tq=512)
    mix_d = lax.cond(score_bound <= SCORE_BOUND,
                     functools.partial(diffattn, bounded=True), functools.partial(diffattn, bounded=False))
    mix_g = _gla(mid, log_a, sgr, gla_out_g[l][None], B=B, S=S, blk=512)
    h1 = _outproj(mix_d, mix_g, w_out[l], x2, tm=1024, tn=512)

    hdim = D // CROSS_HEADS
    qc = _normproj(h1, g_cross[l][None], w_cq[l], cq_norm_g[l][None] * (hdim ** -0.5),
                   tm=1024, tn=hdim, n_norm=CROSS_HEADS, name="cq")
    kv = _normproj(mem.reshape(B * n_mem, D), g_mem[l][None], w_ckv[l], ck_norm_g[l][None],
                   tm=B * n_mem, tn=hdim, n_norm=CROSS_HEADS, name="ckv")
    h2 = _cross(qc, kv, w_co[l], h1, S=S, n_mem=n_mem, tm=512, tn=512)

    w_rt = jnp.concatenate([w_router_grp[l].T, jnp.zeros((SUBLANES - N_GROUPS, D), F32), w_router_exp[l].T])
    b_r = jnp.concatenate([b_router_grp[l], jnp.zeros((SUBLANES - N_GROUPS,), F32), b_router_exp[l]])[:, None]
    eid, gate = _router(h2, g_ffn[l][None], w_rt, b_r, tm=512)
    rank, cnt = _rank(eid, tm=512)

    tb = EXPERT_ROWS
    n_blocks = (T * TOP_K + N_EXPERTS * (tb - 1) + tb - 1) // tb
    counts = cnt[:, 0]
    pcounts = ((counts + tb - 1) // tb) * tb
    pends = jnp.cumsum(pcounts)
    pstarts = pends - pcounts
    nb = (pends[-1:] // tb).astype(I32)
    blk_e = jnp.minimum(jnp.searchsorted(pends, jnp.arange(n_blocks, dtype=I32) * tb, side='right'),
                        N_EXPERTS - 1).astype(I32)
    pick = eid[:TOP_K]
    pstart_of = jnp.sum(jnp.where(pick[..., None] == jnp.arange(N_EXPERTS, dtype=I32), pstarts, 0), axis=-1)
    dest = (pstart_of + rank[:TOP_K]).astype(I32).reshape(-1)
    tok = jnp.tile(jnp.arange(T, dtype=I32), TOP_K)
    buf_tok = jnp.zeros(((n_blocks + 1) * tb,), I32).at[dest].set(tok)

    yb = _experts(buf_tok, blk_e, nb, h2, g_ffn[l][None], w_gate[l], w_up[l], w_down[l],
                  tb=tb, n_blocks=n_blocks)
    out = _combine(dest, yb, h2, gate[:TOP_K].T, tm=256)
    return out.reshape(B, S, D)
```

```python
import functools
import math

import jax
import jax.numpy as jnp
from jax import lax
from jax.experimental import pallas as pl
from jax.experimental.pallas import tpu as pltpu

F32 = jnp.float32
BF16 = jnp.bfloat16
I32 = jnp.int32

LANES = 128
SUBLANES = 8

CHUNK = 64
ROPE_THETA = 10000.0
NORM_EPS = 1e-6
NEG_INF = -1e30
DIFF_HEADS = 8
DIFF_VDIM = 128
DIFF_QKDIM = 64
GLA_HEADS = 4
GLA_VDIM = 256
GLA_KDIM = 128
GLA_GATE_RANK = 16
GLA_TAU = 16.0
CROSS_HEADS = 4
N_GROUPS = 4
EXPERTS_PER_GROUP = 8
N_EXPERTS = N_GROUPS * EXPERTS_PER_GROUP
TOP_K = 2
LAM_INIT = 0.8 - 0.6 * math.exp(-0.3 * 0)

NT_DIMS = (((1,), (1,)), ((), ()))


def _cparams(semantics, vmem_mib):
    return pltpu.CompilerParams(dimension_semantics=semantics,
                                vmem_limit_bytes=vmem_mib * 1024 * 1024)


def _dot(a, b):
    return jnp.dot(a, b, preferred_element_type=F32)


def _dot_nt(a, b):
    return lax.dot_general(a, b, NT_DIMS, preferred_element_type=F32)


def _rms(x, g):
    ms = jnp.mean(x * x, axis=-1, keepdims=True)
    return x * lax.rsqrt(ms + NORM_EPS) * g


def _split_bf16(x):
    hi = x.astype(BF16)
    lo = (x - hi.astype(F32)).astype(BF16)
    return hi, lo


TN = 512
J_QK = 4
J_MID = 6
J_LR = J_QK + J_MID
J_GR = J_LR + 1
N_J = J_GR + 2


def _inproj_kernel(x_ref, g_ref, pos_ref, freq_ref, qkg_ref, w_ref, wgr_ref, wlr_ref, wa2_ref, ba_ref,
                   qk_ref, mid_ref, loga_ref, sgr_ref, n_scr, cos_scr, sin_scr):
    j = pl.program_id(1)

    @pl.when(j == 0)
    def _():
        n_scr[...] = _rms(x_ref[...], g_ref[...]).astype(BF16)
        ang = pos_ref[...].astype(F32) * freq_ref[...]
        cos_scr[...] = jnp.cos(ang)
        sin_scr[...] = jnp.sin(ang)

    @pl.when(j < J_QK)
    def _():
        y = _dot_nt(n_scr[...], w_ref[...].astype(BF16))
        lane = lax.broadcasted_iota(I32, (1, LANES), 1)
        low_seg = lane < DIFF_QKDIM
        first_half = (lane % DIFF_QKDIM) < (DIFF_QKDIM // 2)
        gain = qkg_ref[pl.ds(j // (J_QK // 2), 1), :]
        cos = cos_scr[...]
        sin = sin_scr[...]
        for c in range(TN // LANES):
            yb = y[:, c * LANES:(c + 1) * LANES]
            y2 = yb * yb
            s_lo = jnp.sum(jnp.where(low_seg, y2, 0.0), axis=-1, keepdims=True)
            s_hi = jnp.sum(jnp.where(low_seg, 0.0, y2), axis=-1, keepdims=True)
            ms = jnp.where(low_seg, s_lo, s_hi) * (1.0 / DIFF_QKDIM)
            yn = yb * lax.rsqrt(ms + NORM_EPS) * gain
            rot = jnp.where(first_half,
                            -pltpu.roll(yn, LANES - DIFF_QKDIM // 2, 1),
                            pltpu.roll(yn, DIFF_QKDIM // 2, 1))
            qk_ref[:, c * LANES:(c + 1) * LANES] = (yn * cos + rot * sin).astype(BF16)

    @pl.when((j >= J_QK) & (j < J_LR))
    def _():
        mid_ref[...] = _dot_nt(n_scr[...], w_ref[...].astype(BF16)).astype(BF16)

    @pl.when(j == J_LR)
    def _():
        lr = _dot_nt(n_scr[...], wlr_ref[...].astype(BF16))
        z = _dot(lr.astype(BF16), wa2_ref[...].astype(BF16)) + ba_ref[...]
        log_sig = jnp.minimum(z, 0.0) - jnp.log(1.0 + jnp.exp(-jnp.abs(z)))
        loga_ref[...] = log_sig * (1.0 / GLA_TAU)

    @pl.when(j >= J_GR)
    def _():
        y = _dot_nt(n_scr[...], wgr_ref[...].astype(BF16))
        sgr_ref[...] = (y / (1.0 + jnp.exp(-y))).astype(BF16)


def _inproj(x2, g_attn, pos2, freq, qkg, w_t, w_a2, b_a, *, tm):
    T, D = x2.shape
    n_mid = J_MID * TN
    n_gk = GLA_HEADS * GLA_KDIM
    lr0 = J_LR * TN
    gr0 = lr0 + GLA_GATE_RANK
    n_gr = w_t.shape[0] - gr0
    assert n_gr == 2 * TN and lr0 % GLA_GATE_RANK == 0
    return pl.pallas_call(
        _inproj_kernel,
        grid=(T // tm, N_J),
        in_specs=[
            pl.BlockSpec((tm, D), lambda i, j: (i, 0)),
            pl.BlockSpec((1, D), lambda i, j: (0, 0)),
            pl.BlockSpec((tm, 1), lambda i, j: (i, 0)),
            pl.BlockSpec((1, LANES), lambda i, j: (0, 0)),
            pl.BlockSpec((2, LANES), lambda i, j: (0, 0)),
            pl.BlockSpec((TN, D), lambda i, j: (jnp.minimum(j, J_LR - 1), 0)),
            pl.BlockSpec((pl.Element(TN), pl.Element(D)),
                         lambda i, j: (pl.multiple_of(gr0 + TN * jnp.clip(j - J_GR, 0, 1), SUBLANES), 0)),
            pl.BlockSpec((GLA_GATE_RANK, D), lambda i, j: (lr0 // GLA_GATE_RANK, 0)),
            pl.BlockSpec((GLA_GATE_RANK, n_gk), lambda i, j: (0, 0)),
            pl.BlockSpec((1, n_gk), lambda i, j: (0, 0)),
        ],
        out_specs=[
            pl.BlockSpec((tm, TN), lambda i, j: (i, jnp.minimum(j, J_QK - 1))),
            pl.BlockSpec((tm, TN), lambda i, j: (i, jnp.clip(j - J_QK, 0, J_MID - 1))),
            pl.BlockSpec((tm, n_gk), lambda i, j: (i, 0)),
            pl.BlockSpec((tm, TN), lambda i, j: (i, jnp.clip(j - J_GR, 0, 1))),
        ],
        out_shape=[
            jax.ShapeDtypeStruct((T, J_QK * TN), BF16),
            jax.ShapeDtypeStruct((T, n_mid), BF16),
            jax.ShapeDtypeStruct((T, n_gk), F32),
            jax.ShapeDtypeStruct((T, n_gr), BF16),
        ],
        scratch_shapes=[
            pltpu.VMEM((tm, D), BF16),
            pltpu.VMEM((tm, LANES), F32),
            pltpu.VMEM((tm, LANES), F32),
        ],
        compiler_params=_cparams(("parallel", "arbitrary"), 56),
        name="inproj",
    )(x2, g_attn, pos2, freq, qkg, w_t, w_t, w_t, w_a2, b_a)


SCORE_BOUND = 80.0


def _diffattn_kernel(lv_ref, q_ref, k_ref, v_ref, sg_ref, o_ref, vext, diag_mask, acc1, acc2, m1, m2,
                     *, tq, bounded):
    i = pl.program_id(2)
    lv = lv_ref[...]
    lam = (jnp.exp(jnp.sum(lv[0:1] * lv[1:2], axis=-1, keepdims=True))
           - jnp.exp(jnp.sum(lv[2:3] * lv[3:4], axis=-1, keepdims=True)) + LAM_INIT)

    @pl.when(i == 0)
    def _():
        S = v_ref.shape[0]
        lane_s = lax.broadcasted_iota(I32, (S, LANES), 1)
        vext[:, 0:DIFF_VDIM] = v_ref[...]
        vext[:, DIFF_VDIM:] = jnp.where(lane_s == 0, 1.0, 0.0).astype(BF16)
        row_chunk = lax.broadcasted_iota(I32, (tq, tq), 0) // CHUNK
        col_chunk = lax.broadcasted_iota(I32, (tq, tq), 1) // CHUNK
        diag_mask[...] = jnp.where(col_chunk <= row_chunk, 1.0, 0.0).astype(BF16)

    q = q_ref[...]
    lane = lax.broadcasted_iota(I32, (1, LANES), 1)
    zero = jnp.zeros_like(q)
    q1 = jnp.where(lane < DIFF_QKDIM, q, zero)
    q2 = jnp.where(lane < DIFF_QKDIM, zero, q)

    acc1[...] = jnp.zeros(acc1.shape, F32)
    acc2[...] = jnp.zeros(acc2.shape, F32)
    if not bounded:
        m1[...] = jnp.full(m1.shape, NEG_INF, F32)
        m2[...] = jnp.full(m2.shape, NEG_INF, F32)

    def step(j, masked):
        start = pl.multiple_of(j * tq, tq)
        k = k_ref[pl.ds(start, tq), :]
        v = vext[pl.ds(start, tq), :]
        for qc, acc, m in ((q1, acc1, m1), (q2, acc2, m2)):
            s = lax.dot_general(qc, k, NT_DIMS, preferred_element_type=F32)
            if bounded:
                p = jnp.exp2(s).astype(BF16)
                if masked:
                    p = p * diag_mask[...]
                acc[...] += _dot(p, v)
            else:
                if masked:
                    s = jnp.where(diag_mask[...] > 0, s, NEG_INF)
                m_old = m[...]
                m_new = jnp.maximum(m_old, jnp.max(s, axis=-1, keepdims=True))
                p = jnp.exp2(s - m_new)
                acc[...] = jnp.exp2(m_old - m_new) * acc[...] + _dot(p.astype(BF16), v)
                m[...] = m_new

    def body(j, carry):
        step(j, False)
        return carry

    lax.fori_loop(0, i, body, 0)
    step(i, True)

    a1 = acc1[...]
    a2 = acc2[...]
    o = (a1[:, :DIFF_VDIM] / a1[:, DIFF_VDIM:DIFF_VDIM + 1]
         - lam * (a2[:, :DIFF_VDIM] / a2[:, DIFF_VDIM:DIFF_VDIM + 1]))
    o_ref[...] = (_rms(o, sg_ref[...]) * (1.0 - LAM_INIT)).astype(BF16)


def _diffattn(lvec, qk, mid, subln_g, *, B, S, tq, bounded):
    T = B * S
    nq = S // tq
    kern = functools.partial(_diffattn_kernel, tq=tq, bounded=bounded)
    return pl.pallas_call(
        kern,
        grid=(B, DIFF_HEADS, nq),
        in_specs=[
            pl.BlockSpec((4, DIFF_QKDIM), lambda b, h, i: (0, 0)),
            pl.BlockSpec((tq, LANES), lambda b, h, i: (b * nq + i, h)),
            pl.BlockSpec((S, LANES), lambda b, h, i: (b, DIFF_HEADS + h)),
            pl.BlockSpec((S, LANES), lambda b, h, i: (b, h)),
            pl.BlockSpec((1, DIFF_VDIM), lambda b, h, i: (0, 0)),
        ],
        out_specs=pl.BlockSpec((tq, DIFF_VDIM), lambda b, h, i: (b * nq + i, h)),
        out_shape=jax.ShapeDtypeStruct((T, DIFF_HEADS * DIFF_VDIM), BF16),
        scratch_shapes=[
            pltpu.VMEM((S, 2 * DIFF_VDIM), BF16),
            pltpu.VMEM((tq, tq), BF16),
            pltpu.VMEM((tq, 2 * DIFF_VDIM), F32),
            pltpu.VMEM((tq, 2 * DIFF_VDIM), F32),
            pltpu.VMEM((tq, 1), F32),
            pltpu.VMEM((tq, 1), F32),
        ],
        compiler_params=_cparams(("parallel", "parallel", "arbitrary"), 32),
        name="diffattn_bounded" if bounded else "diffattn_online",
    )(lvec, qk, qk, mid, subln_g)


def _gla_kernel(q_ref, k_ref, v_ref, la_ref, sgr_ref, g_ref, o_ref, state, *, blk):
    @pl.when(pl.program_id(2) == 0)
    def _():
        state[...] = jnp.zeros(state.shape, F32)

    la_t = la_ref[...].T
    k_t = k_ref[...].astype(F32).T
    r = lax.broadcasted_iota(I32, (blk, blk), 0)
    c = lax.broadcasted_iota(I32, (blk, blk), 1)
    same = (r // CHUNK) == (c // CHUNK)
    tri = jnp.where(same & (r <= c), 1.0, 0.0).astype(BF16)
    ones = jnp.where(same, 1.0, 0.0).astype(BF16)
    hi, lo = _split_bf16(la_t)
    cum_t = _dot(hi, tri) + _dot(lo, tri)
    tot_t = _dot(hi, ones) + _dot(lo, ones)
    kd_t = k_t * jnp.exp(tot_t - cum_t)

    lane = lax.broadcasted_iota(I32, (1, LANES), 1)
    scale = GLA_KDIM ** -0.5
    for ck in range(blk // CHUNK):
        pair = slice((ck // 2) * LANES, (ck // 2 + 1) * LANES)
        rows = slice(ck * CHUNK, (ck + 1) * CHUNK)
        in_chunk = (lane // CHUNK) == (ck % 2)
        kd = jnp.where(in_chunk, kd_t[:, pair], 0.0).astype(BF16)
        d_state = _dot(kd, v_ref[pair, :])
        decay = jnp.exp(tot_t[:, ck * CHUNK:ck * CHUNK + 1])
        st = decay * state[...] + d_state
        state[...] = st
        o = _dot(q_ref[rows, :], st.astype(BF16)) * scale
        gate = sgr_ref[rows, :].astype(F32)
        o_ref[rows, :] = (_rms(o, g_ref[...]) * gate).astype(BF16)


def _gla(mid, log_a, sgr, out_g, *, B, S, blk):
    T = B * S
    ns = S // blk
    kern = functools.partial(_gla_kernel, blk=blk)
    q_col0 = (DIFF_HEADS * DIFF_VDIM) // GLA_KDIM
    k_col0 = q_col0 + GLA_HEADS
    v_col0 = (DIFF_HEADS * DIFF_VDIM + 2 * GLA_HEADS * GLA_KDIM) // GLA_VDIM
    return pl.pallas_call(
        kern,
        grid=(B, GLA_HEADS, ns),
        in_specs=[
            pl.BlockSpec((blk, GLA_KDIM), lambda b, h, s: (b * ns + s, q_col0 + h)),
            pl.BlockSpec((blk, GLA_KDIM), lambda b, h, s: (b * ns + s, k_col0 + h)),
            pl.BlockSpec((blk, GLA_VDIM), lambda b, h, s: (b * ns + s, v_col0 + h)),
            pl.BlockSpec((blk, GLA_KDIM), lambda b, h, s: (b * ns + s, h)),
            pl.BlockSpec((blk, GLA_VDIM), lambda b, h, s: (b * ns + s, h)),
            pl.BlockSpec((1, GLA_VDIM), lambda b, h, s: (0, 0)),
        ],
        out_specs=pl.BlockSpec((blk, GLA_VDIM), lambda b, h, s: (b * ns + s, h)),
        out_shape=jax.ShapeDtypeStruct((T, GLA_HEADS * GLA_VDIM), BF16),
        scratch_shapes=[pltpu.VMEM((GLA_KDIM, GLA_VDIM), F32)],
        compiler_params=_cparams(("parallel", "parallel", "arbitrary"), 32),
        name="gla",
    )(mid, mid, mid, log_a, sgr, out_g)


def _resident_w_map(n_j):
    return lambda i, j: (0, jnp.where(i == 0, j, n_j - 1))


def _outproj_kernel(a_ref, b_ref, wa_ref, wb_ref, x_ref, o_ref, w_scr):
    j = pl.program_id(1)
    ka = a_ref.shape[1]

    @pl.when(pl.program_id(0) == 0)
    def _():
        w_scr[j, 0:ka, :] = wa_ref[...].astype(BF16)
        w_scr[j, ka:, :] = wb_ref[...].astype(BF16)

    acc = _dot(a_ref[...], w_scr[j, 0:ka, :]) + _dot(b_ref[...], w_scr[j, ka:, :])
    o_ref[...] = x_ref[...] + acc


def _outproj(a, b, w_out, x2, *, tm, tn):
    T, ka = a.shape
    kb = b.shape[1]
    assert ka == kb
    D = w_out.shape[1]
    n_j = D // tn
    return pl.pallas_call(
        _outproj_kernel,
        grid=(T // tm, n_j),
        in_specs=[
            pl.BlockSpec((tm, ka), lambda i, j: (i, 0)),
            pl.BlockSpec((tm, kb), lambda i, j: (i, 0)),
            pl.BlockSpec((ka, tn), _resident_w_map(n_j)),
            pl.BlockSpec((kb, tn), lambda i, j: (1, jnp.where(i == 0, j, n_j - 1))),
            pl.BlockSpec((tm, tn), lambda i, j: (i, j)),
        ],
        out_specs=pl.BlockSpec((tm, tn), lambda i, j: (i, j)),
        out_shape=jax.ShapeDtypeStruct((T, D), F32),
        scratch_shapes=[pltpu.VMEM((n_j, ka + kb, tn), BF16)],
        compiler_params=_cparams(("arbitrary", "arbitrary"), 48),
        name="outproj",
    )(a, b, w_out, w_out, x2)


def _normproj_kernel(x_ref, g_ref, w_ref, hg_ref, o_ref, n_scr, *w_scr, n_norm):
    j = pl.program_id(1)

    @pl.when(j == 0)
    def _():
        n_scr[...] = _rms(x_ref[...], g_ref[...]).astype(BF16)

    if w_scr:
        @pl.when(pl.program_id(0) == 0)
        def _():
            w_scr[0][j] = w_ref[...].astype(BF16)
        y = _dot(n_scr[...], w_scr[0][j])
    else:
        y = _dot(n_scr[...], w_ref[...].astype(BF16))

    @pl.when(j < n_norm)
    def _():
        o_ref[...] = _rms(y, hg_ref[...]).astype(BF16)

    @pl.when(j >= n_norm)
    def _():
        o_ref[...] = y.astype(BF16)


def _normproj(x2, g, w, head_g, *, tm, tn, n_norm, name):
    T, D = x2.shape
    N = w.shape[1]
    kern = functools.partial(_normproj_kernel, n_norm=n_norm)
    n_j = N // tn
    resident = T // tm > 1
    return pl.pallas_call(
        kern,
        grid=(T // tm, n_j),
        in_specs=[
            pl.BlockSpec((tm, D), lambda i, j: (i, 0)),
            pl.BlockSpec((1, D), lambda i, j: (0, 0)),
            pl.BlockSpec((D, tn), _resident_w_map(n_j) if resident else (lambda i, j: (0, j))),
            pl.BlockSpec((1, tn), lambda i, j: (0, 0)),
        ],
        out_specs=pl.BlockSpec((tm, tn), lambda i, j: (i, j)),
        out_shape=jax.ShapeDtypeStruct((T, N), BF16),
        scratch_shapes=[pltpu.VMEM((tm, D), BF16)] + ([pltpu.VMEM((n_j, D, tn), BF16)] if resident else []),
        compiler_params=_cparams(("arbitrary", "arbitrary"), 48),
        name=name,
    )(x2, g, w, head_g)


def _cross_kernel(q_ref, k_ref, v_ref, w_ref, h_ref, o_ref, att_scr, w_scr, *, hdim):
    j = pl.program_id(1)

    @pl.when(pl.program_id(0) == 0)
    def _():
        w_scr[j] = w_ref[...].astype(BF16)

    @pl.when(j == 0)
    def _():
        for hd in range(CROSS_HEADS):
            cols = slice(hd * hdim, (hd + 1) * hdim)
            s = lax.dot_general(q_ref[:, cols], k_ref[:, cols], NT_DIMS, preferred_element_type=F32)
            p = jnp.exp(s - jnp.max(s, axis=-1, keepdims=True))
            l = jnp.sum(p, axis=-1, keepdims=True)
            att_scr[:, cols] = (_dot(p.astype(BF16), v_ref[:, cols]) / l).astype(BF16)

    o_ref[...] = h_ref[...] + _dot(att_scr[...], w_scr[j])


def _cross(qc, kv, w_co, h1, *, S, n_mem, tm, tn):
    T, D = qc.shape
    per_b = S // tm
    kern = functools.partial(_cross_kernel, hdim=D // CROSS_HEADS)
    n_j = D // tn
    return pl.pallas_call(
        kern,
        grid=(T // tm, n_j),
        in_specs=[
            pl.BlockSpec((tm, D), lambda i, j: (i, 0)),
            pl.BlockSpec((n_mem, D), lambda i, j: (i // per_b, 0)),
            pl.BlockSpec((n_mem, D), lambda i, j: (i // per_b, 1)),
            pl.BlockSpec((D, tn), _resident_w_map(n_j)),
            pl.BlockSpec((tm, tn), lambda i, j: (i, j)),
        ],
        out_specs=pl.BlockSpec((tm, tn), lambda i, j: (i, j)),
        out_shape=jax.ShapeDtypeStruct((T, D), F32),
        scratch_shapes=[pltpu.VMEM((tm, D), BF16), pltpu.VMEM((n_j, D, tn), BF16)],
        compiler_params=_cparams(("arbitrary", "arbitrary"), 48),
        name="cross",
    )(qc, kv, kv, w_co, h1)


R_ROWS = SUBLANES + N_EXPERTS


def _router_kernel(h_ref, g_ref, wt_ref, b_ref, eid_ref, gate_ref):
    n = _rms(h_ref[...], g_ref[...])
    nh, nl = _split_bf16(n)
    wh, wl = _split_bf16(wt_ref[...])
    nt = functools.partial(lax.dot_general, dimension_numbers=NT_DIMS, preferred_element_type=F32)
    lg = nt(wh, nh) + nt(wh, nl) + nt(wl, nh) + b_ref[...]

    tm = lg.shape[1]
    row = lax.broadcasted_iota(I32, (SUBLANES, tm), 0)

    def first_argmax(v, vmax):
        return jnp.min(jnp.where(v == vmax, row, SUBLANES), axis=0, keepdims=True)

    gl = jnp.where(row < N_GROUPS, lg[0:SUBLANES], NEG_INF)
    gmax = jnp.max(gl, axis=0, keepdims=True)
    grp = first_argmax(gl, gmax)
    grp_w = 1.0 / jnp.sum(jnp.exp(gl - gmax), axis=0, keepdims=True)

    sel = jnp.zeros((SUBLANES, tm), F32)
    for gi in range(N_GROUPS):
        lo = SUBLANES + gi * EXPERTS_PER_GROUP
        sel = jnp.where(grp == gi, lg[lo:lo + EXPERTS_PER_GROUP], sel)
    e = jnp.exp(sel - jnp.max(sel, axis=0, keepdims=True))
    prob = e / jnp.sum(e, axis=0, keepdims=True)
    p1 = jnp.max(prob, axis=0, keepdims=True)
    i1 = first_argmax(prob, p1)
    rest = jnp.where(row == i1, -1.0, prob)
    p2 = jnp.max(rest, axis=0, keepdims=True)
    i2 = first_argmax(rest, p2)
    den = p1 + p2
    base = grp * EXPERTS_PER_GROUP
    eid_ref[...] = jnp.where(row == 0, base + i1, jnp.where(row == 1, base + i2, 0))
    gate_ref[...] = jnp.where(row == 0, grp_w * p1 / den, jnp.where(row == 1, grp_w * p2 / den, 0.0))


def _router(h2, g_ffn, w_rt, b_r, *, tm):
    T, D = h2.shape
    return pl.pallas_call(
        _router_kernel,
        grid=(T // tm,),
        in_specs=[
            pl.BlockSpec((tm, D), lambda i: (i, 0)),
            pl.BlockSpec((1, D), lambda i: (0, 0)),
            pl.BlockSpec((R_ROWS, D), lambda i: (0, 0)),
            pl.BlockSpec((R_ROWS, 1), lambda i: (0, 0)),
        ],
        out_specs=[
            pl.BlockSpec((SUBLANES, tm), lambda i: (0, i)),
            pl.BlockSpec((SUBLANES, tm), lambda i: (0, i)),
        ],
        out_shape=[
            jax.ShapeDtypeStruct((SUBLANES, T), I32),
            jax.ShapeDtypeStruct((SUBLANES, T), F32),
        ],
        compiler_params=_cparams(("parallel",), 32),
        name="router",
    )(h2, g_ffn, w_rt, b_r)


def _rank_kernel(eid_ref, rank_ref, cnt_ref, carry):
    @pl.when(pl.program_id(0) == 0)
    def _():
        carry[...] = jnp.zeros(carry.shape, F32)

    tm = eid_ref.shape[1]
    e0 = eid_ref[0:1, :]
    e1 = eid_ref[1:2, :]
    erow = lax.broadcasted_iota(I32, (N_EXPERTS, tm), 0)
    hit = jnp.where((erow == e0) | (erow == e1), 1.0, 0.0)
    r = lax.broadcasted_iota(I32, (tm, tm), 0)
    c = lax.broadcasted_iota(I32, (tm, tm), 1)
    before = jnp.where(r < c, 1.0, 0.0).astype(BF16)
    pre = _dot(hit.astype(BF16), before) + carry[:, 0:1]
    rank0 = jnp.sum(jnp.where(erow == e0, pre, 0.0), axis=0, keepdims=True)
    rank1 = jnp.sum(jnp.where(erow == e1, pre, 0.0), axis=0, keepdims=True)
    row = lax.broadcasted_iota(I32, (SUBLANES, tm), 0)
    rank_ref[...] = jnp.where(row == 0, rank0, jnp.where(row == 1, rank1, 0.0)).astype(I32)
    total = carry[...] + jnp.sum(hit, axis=1, keepdims=True)
    carry[...] = total
    cnt_ref[...] = total.astype(I32)


def _rank(eid, *, tm):
    T = eid.shape[1]
    return pl.pallas_call(
        _rank_kernel,
        grid=(T // tm,),
        in_specs=[pl.BlockSpec((SUBLANES, tm), lambda i: (0, i))],
        out_specs=[
            pl.BlockSpec((SUBLANES, tm), lambda i: (0, i)),
            pl.BlockSpec((N_EXPERTS, LANES), lambda i: (0, 0)),
        ],
        out_shape=[
            jax.ShapeDtypeStruct((SUBLANES, T), I32),
            jax.ShapeDtypeStruct((N_EXPERTS, LANES), I32),
        ],
        scratch_shapes=[pltpu.VMEM((N_EXPERTS, LANES), F32)],
        compiler_params=_cparams(("arbitrary",), 32),
        name="rank",
    )(eid)


def _expert_kernel(be_ref, nb_ref, tok0_ref, tok_ref, h_hbm, g_ref, wg_ref, wu_ref, wd_ref, y_ref,
                   xbuf, sem, wg_b, wu_b, wd_b, *, tb, n_blocks):
    i = pl.program_id(0)
    nb = nb_ref[0]
    slot = i % 2

    def row_copy(tok, r, s):
        return pltpu.make_async_copy(h_hbm.at[pl.ds(tok, 1), :], xbuf.at[s, pl.ds(r, 1), :], sem.at[s])

    @pl.when(i == 0)
    def _():
        def body(r, carry):
            row_copy(tok0_ref[0, 0, r], r, 0).start()
            return carry
        lax.fori_loop(0, tb, body, 0)

    @pl.when(i <= nb)
    def _():
        pltpu.make_async_copy(h_hbm.at[pl.ds(0, tb), :], xbuf.at[slot], sem.at[slot]).wait()

    @pl.when((i < nb) & ((i == 0) | (be_ref[jnp.minimum(i, n_blocks - 1)] != be_ref[jnp.maximum(i - 1, 0)])))
    def _():
        wg_b[...] = wg_ref[0].astype(BF16)
        wu_b[...] = wu_ref[0].astype(BF16)
        wd_b[...] = wd_ref[0].astype(BF16)

    @pl.when(i < nb)
    def _():
        n = _rms(xbuf[slot], g_ref[...]).astype(BF16)
        for r in range(tb):
            row_copy(tok_ref[0, 0, r], r, 1 - slot).start()
        a = _dot(n, wg_b[...])
        hdn = (a / (1.0 + jnp.exp(-a))) * _dot(n, wu_b[...])
        y_ref[...] = _dot(hdn.astype(BF16), wd_b[...])

    @pl.when((i >= nb) & (i < n_blocks))
    def _():
        y_ref[...] = jnp.zeros(y_ref.shape, F32)


def _experts(buf_tok, blk_e, nb, h2, g_ffn, w_gate, w_up, w_down, *, tb, n_blocks):
    T, D = h2.shape
    De = w_gate.shape[2]
    kern = functools.partial(_expert_kernel, tb=tb, n_blocks=n_blocks)

    def w_map(i, be, nbr):
        return (be[jnp.minimum(i, nbr[0] - 1)], 0, 0)

    tok3 = buf_tok.reshape(n_blocks + 1, 1, tb)

    grid_spec = pltpu.PrefetchScalarGridSpec(
        num_scalar_prefetch=2,
        grid=(n_blocks + 1,),
        in_specs=[
            pl.BlockSpec((1, 1, tb), lambda i, be, nbr: (0, 0, 0), memory_space=pltpu.SMEM),
            pl.BlockSpec((1, 1, tb), lambda i, be, nbr: (jnp.minimum(i + 1, n_blocks), 0, 0),
                         memory_space=pltpu.SMEM),
            pl.BlockSpec(memory_space=pl.ANY),
            pl.BlockSpec((1, D), lambda i, be, nbr: (0, 0)),
            pl.BlockSpec((1, D, De), w_map),
            pl.BlockSpec((1, D, De), w_map),
            pl.BlockSpec((1, De, D), w_map),
        ],
        out_specs=pl.BlockSpec((tb, D), lambda i, be, nbr: (jnp.minimum(i, n_blocks - 1), 0)),
        scratch_shapes=[
            pltpu.VMEM((2, tb, D), F32),
            pltpu.SemaphoreType.DMA((2,)),
            pltpu.VMEM((D, De), BF16),
            pltpu.VMEM((D, De), BF16),
            pltpu.VMEM((De, D), BF16),
        ],
    )
    return pl.pallas_call(
        kern,
        grid_spec=grid_spec,
        out_shape=jax.ShapeDtypeStruct((n_blocks * tb, D), F32),
        compiler_params=_cparams(("arbitrary",), 56),
        name="experts",
    )(blk_e, nb, tok3, tok3, h2, g_ffn, w_gate, w_up, w_down)


def _combine_kernel(dest_ref, y_hbm, h_ref, gate_ref, o_ref, ybuf, sem, *, tm, n_tok):
    i = pl.program_id(0)
    n = pl.num_programs(0)

    def row_copy(d, k, r, slot):
        return pltpu.make_async_copy(y_hbm.at[pl.ds(d, 1), :], ybuf.at[slot, k, pl.ds(r, 1), :], sem.at[slot])

    def start_gather(blk, slot):
        def body(r, carry):
            for k in range(TOP_K):
                row_copy(dest_ref[k * n_tok + blk * tm + r], k, r, slot).start()
            return carry
        lax.fori_loop(0, tm, body, 0, unroll=8)

    def wait_gather(slot):
        for k in range(TOP_K):
            pltpu.make_async_copy(y_hbm.at[pl.ds(0, tm), :], ybuf.at[slot, k], sem.at[slot]).wait()

    @pl.when(i == 0)
    def _():
        start_gather(0, 0)

    @pl.when(i + 1 < n)
    def _():
        start_gather(i + 1, (i + 1) % 2)

    slot = i % 2
    wait_gather(slot)
    gt = gate_ref[...]
    o_ref[...] = h_ref[...] + gt[:, 0:1] * ybuf[slot, 0] + gt[:, 1:2] * ybuf[slot, 1]


def _combine(dest, yb, h2, gate, *, tm):
    T, D = h2.shape
    kern = functools.partial(_combine_kernel, tm=tm, n_tok=T)
    grid_spec = pltpu.PrefetchScalarGridSpec(
        num_scalar_prefetch=1,
        grid=(T // tm,),
        in_specs=[
            pl.BlockSpec(memory_space=pl.ANY),
            pl.BlockSpec((tm, D), lambda i, d: (i, 0)),
            pl.BlockSpec((tm, TOP_K), lambda i, d: (i, 0)),
        ],
        out_specs=pl.BlockSpec((tm, D), lambda i, d: (i, 0)),
        scratch_shapes=[
            pltpu.VMEM((2, TOP_K, tm, D), F32),
            pltpu.SemaphoreType.DMA((2,)),
        ],
    )
    return pl.pallas_call(
        kern,
        grid_spec=grid_spec,
        out_shape=jax.ShapeDtypeStruct((T, D), F32),
        compiler_params=_cparams(("arbitrary",), 40),
        name="combine",
    )(dest, yb, h2, gate)


EXPERT_ROWS = 256


def kernel(x, mem, positions, g_attn, w_in, q_norm_g, k_norm_g, lambda_q1, lambda_k1, lambda_q2, lambda_k2, diff_subln_g, gla_w_a2, gla_b_a, gla_out_g, w_out, g_cross, g_mem, w_cq, w_ckv, cq_norm_g, ck_norm_g, w_co, g_ffn, w_router_grp, b_router_grp, w_router_exp, b_router_exp, w_gate, w_up, w_down):
    B, S, D = x.shape
    T = B * S
    n_mem = mem.shape[1]
    l = 0
    x2 = x.reshape(T, D)

    half = DIFF_QKDIM // 2
    freq = ROPE_THETA ** (-jnp.arange(half, dtype=F32) / half)
    freq = jnp.tile(freq, LANES // half)[None, :]
    q_scale = math.log2(math.e) * DIFF_QKDIM ** -0.5
    qkg = jnp.stack([jnp.tile(q_norm_g[l], 2) * q_scale, jnp.tile(k_norm_g[l], 2)])
    score_bound = 1.01 * DIFF_QKDIM * q_scale * jnp.max(jnp.abs(q_norm_g[l])) * jnp.max(jnp.abs(k_norm_g[l]))
    lvec = jnp.stack([lambda_q1[l], lambda_k1[l], lambda_q2[l], lambda_k2[l]])

    qk, mid, log_a, sgr = _inproj(x2, g_attn[l][None], positions.reshape(T, 1), freq, qkg, w_in[l].T,
                                  gla_w_a2[l], gla_b_a[l][None], tm=1024)
    diffattn = functools.partial(_diffattn, lvec, qk, mid, diff_subln_g[l][None], B=B, S=S, tq=512)
    mix_d = lax.cond(score_bound <= SCORE_BOUND,
                     functools.partial(diffattn, bounded=True), functools.partial(diffattn, bounded=False))
    mix_g = _gla(mid, log_a, sgr, gla_out_g[l][None], B=B, S=S, blk=512)
    h1 = _outproj(mix_d, mix_g, w_out[l], x2, tm=1024, tn=512)

    hdim = D // CROSS_HEADS
    qc = _normproj(h1, g_cross[l][None], w_cq[l], cq_norm_g[l][None] * (hdim ** -0.5),
                   tm=1024, tn=hdim, n_norm=CROSS_HEADS, name="cq")
    kv = _normproj(mem.reshape(B * n_mem, D), g_mem[l][None], w_ckv[l], ck_norm_g[l][None],
                   tm=B * n_mem, tn=hdim, n_norm=CROSS_HEADS, name="ckv")
    h2 = _cross(qc, kv, w_co[l], h1, S=S, n_mem=n_mem, tm=512, tn=512)

    w_rt = jnp.concatenate([w_router_grp[l].T, jnp.zeros((SUBLANES - N_GROUPS, D), F32), w_router_exp[l].T])
    b_r = jnp.concatenate([b_router_grp[l], jnp.zeros((SUBLANES - N_GROUPS,), F32), b_router_exp[l]])[:, None]
    eid, gate = _router(h2, g_ffn[l][None], w_rt, b_r, tm=512)
    rank, cnt = _rank(eid, tm=512)

    tb = EXPERT_ROWS
    n_blocks = (T * TOP_K + N_EXPERTS * (tb - 1) + tb - 1) // tb
    counts = cnt[:, 0]
    pcounts = ((counts + tb - 1) // tb) * tb
    pends = jnp.cumsum(pcounts)
    pstarts = pends - pcounts
    nb = (pends[-1:] // tb).astype(I32)
    blk_e = jnp.minimum(jnp.searchsorted(pends, jnp.arange(n_blocks, dtype=I32) * tb, side='right'),
                        N_EXPERTS - 1).astype(I32)
    pick = eid[:TOP_K]
    pstart_of = jnp.sum(jnp.where(pick[..., None] == jnp.arange(N_EXPERTS, dtype=I32), pstarts, 0), axis=-1)
    dest = (pstart_of + rank[:TOP_K]).astype(I32).reshape(-1)
    tok = jnp.tile(jnp.arange(T, dtype=I32), TOP_K)
    buf_tok = jnp.zeros(((n_blocks + 1) * tb,), I32).at[dest].set(tok)

    yb = _experts(buf_tok, blk_e, nb, h2, g_ffn[l][None], w_gate[l], w_up[l], w_down[l],
                  tb=tb, n_blocks=n_blocks)
    out = _combine(dest, yb, h2, gate[:TOP_K].T, tm=256)
    return out.reshape(B, S, D)
```

```python
import functools
import math

import jax
import jax.numpy as jnp
from jax import lax
from jax.experimental import pallas as pl
from jax.experimental.pallas import tpu as pltpu

F32 = jnp.float32
BF16 = jnp.bfloat16
I32 = jnp.int32

LANES = 128
SUBLANES = 8

CHUNK = 64
ROPE_THETA = 10000.0
NORM_EPS = 1e-6
NEG_INF = -1e30
DIFF_HEADS = 8
DIFF_VDIM = 128
DIFF_QKDIM = 64
GLA_HEADS = 4
GLA_VDIM = 256
GLA_KDIM = 128
GLA_GATE_RANK = 16
GLA_TAU = 16.0
CROSS_HEADS = 4
N_GROUPS = 4
EXPERTS_PER_GROUP = 8
N_EXPERTS = N_GROUPS * EXPERTS_PER_GROUP
TOP_K = 2
LAM_INIT = 0.8 - 0.6 * math.exp(-0.3 * 0)

NT_DIMS = (((1,), (1,)), ((), ()))


def _cparams(semantics, vmem_mib):
    return pltpu.CompilerParams(dimension_semantics=semantics,
                                vmem_limit_bytes=vmem_mib * 1024 * 1024)


def _dot(a, b):
    return jnp.dot(a, b, preferred_element_type=F32)


def _dot_nt(a, b):
    return lax.dot_general(a, b, NT_DIMS, preferred_element_type=F32)


def _rms(x, g):
    ms = jnp.mean(x * x, axis=-1, keepdims=True)
    return x * lax.rsqrt(ms + NORM_EPS) * g


def _split_bf16(x):
    hi = x.astype(BF16)
    lo = (x - hi.astype(F32)).astype(BF16)
    return hi, lo


TN = 512
J_QK = 4
J_MID = 6
J_LR = J_QK + J_MID
J_GR = J_LR + 1
N_J = J_GR + 2


def _inproj_kernel(x_ref, g_ref, pos_ref, freq_ref, qkg_ref, w_ref, wgr_ref, wlr_ref, wa2_ref, ba_ref,
                   qk_ref, mid_ref, loga_ref, sgr_ref, n_scr, cos_scr, sin_scr):
    j = pl.program_id(1)

    @pl.when(j == 0)
    def _():
        n_scr[...] = _rms(x_ref[...], g_ref[...]).astype(BF16)
        ang = pos_ref[...].astype(F32) * freq_ref[...]
        cos_scr[...] = jnp.cos(ang)
        sin_scr[...] = jnp.sin(ang)

    @pl.when(j < J_QK)
    def _():
        y = _dot_nt(n_scr[...], w_ref[...].astype(BF16))
        lane = lax.broadcasted_iota(I32, (1, LANES), 1)
        low_seg = lane < DIFF_QKDIM
        first_half = (lane % DIFF_QKDIM) < (DIFF_QKDIM // 2)
        gain = qkg_ref[pl.ds(j // (J_QK // 2), 1), :]
        cos = cos_scr[...]
        sin = sin_scr[...]
        for c in range(TN // LANES):
            yb = y[:, c * LANES:(c + 1) * LANES]
            y2 = yb * yb
            s_lo = jnp.sum(jnp.where(low_seg, y2, 0.0), axis=-1, keepdims=True)
            s_hi = jnp.sum(jnp.where(low_seg, 0.0, y2), axis=-1, keepdims=True)
            ms = jnp.where(low_seg, s_lo, s_hi) * (1.0 / DIFF_QKDIM)
            yn = yb * lax.rsqrt(ms + NORM_EPS) * gain
            rot = jnp.where(first_half,
                            -pltpu.roll(yn, LANES - DIFF_QKDIM // 2, 1),
                            pltpu.roll(yn, DIFF_QKDIM // 2, 1))
            qk_ref[:, c * LANES:(c + 1) * LANES] = (yn * cos + rot * sin).astype(BF16)

    @pl.when((j >= J_QK) & (j < J_LR))
    def _():
        mid_ref[...] = _dot_nt(n_scr[...], w_ref[...].astype(BF16)).astype(BF16)

    @pl.when(j == J_LR)
    def _():
        lr = _dot_nt(n_scr[...], wlr_ref[...].astype(BF16))
        z = _dot(lr.astype(BF16), wa2_ref[...].astype(BF16)) + ba_ref[...]
        log_sig = jnp.minimum(z, 0.0) - jnp.log(1.0 + jnp.exp(-jnp.abs(z)))
        loga_ref[...] = log_sig * (1.0 / GLA_TAU)

    @pl.when(j >= J_GR)
    def _():
        y = _dot_nt(n_scr[...], wgr_ref[...].astype(BF16))
        sgr_ref[...] = (y / (1.0 + jnp.exp(-y))).astype(BF16)


def _inproj(x2, g_attn, pos2, freq, qkg, w_t, w_a2, b_a, *, tm):
    T, D = x2.shape
    n_mid = J_MID * TN
    n_gk = GLA_HEADS * GLA_KDIM
    lr0 = J_LR * TN
    gr0 = lr0 + GLA_GATE_RANK
    n_gr = w_t.shape[0] - gr0
    assert n_gr == 2 * TN and lr0 % GLA_GATE_RANK == 0
    return pl.pallas_call(
        _inproj_kernel,
        grid=(T // tm, N_J),
        in_specs=[
            pl.BlockSpec((tm, D), lambda i, j: (i, 0)),
            pl.BlockSpec((1, D), lambda i, j: (0, 0)),
            pl.BlockSpec((tm, 1), lambda i, j: (i, 0)),
            pl.BlockSpec((1, LANES), lambda i, j: (0, 0)),
            pl.BlockSpec((2, LANES), lambda i, j: (0, 0)),
            pl.BlockSpec((TN, D), lambda i, j: (jnp.minimum(j, J_LR - 1), 0)),
            pl.BlockSpec((pl.Element(TN), pl.Element(D)),
                         lambda i, j: (pl.multiple_of(gr0 + TN * jnp.clip(j - J_GR, 0, 1), SUBLANES), 0)),
            pl.BlockSpec((GLA_GATE_RANK, D), lambda i, j: (lr0 // GLA_GATE_RANK, 0)),
            pl.BlockSpec((GLA_GATE_RANK, n_gk), lambda i, j: (0, 0)),
            pl.BlockSpec((1, n_gk), lambda i, j: (0, 0)),
        ],
        out_specs=[
            pl.BlockSpec((tm, TN), lambda i, j: (i, jnp.minimum(j, J_QK - 1))),
            pl.BlockSpec((tm, TN), lambda i, j: (i, jnp.clip(j - J_QK, 0, J_MID - 1))),
            pl.BlockSpec((tm, n_gk), lambda i, j: (i, 0)),
            pl.BlockSpec((tm, TN), lambda i, j: (i, jnp.clip(j - J_GR, 0, 1))),
        ],
        out_shape=[
            jax.ShapeDtypeStruct((T, J_QK * TN), BF16),
            jax.ShapeDtypeStruct((T, n_mid), BF16),
            jax.ShapeDtypeStruct((T, n_gk), F32),
            jax.ShapeDtypeStruct((T, n_gr), BF16),
        ],
        scratch_shapes=[
            pltpu.VMEM((tm, D), BF16),
            pltpu.VMEM((tm, LANES), F32),
            pltpu.VMEM((tm, LANES), F32),
        ],
        compiler_params=_cparams(("parallel", "arbitrary"), 56),
        name="inproj",
    )(x2, g_attn, pos2, freq, qkg, w_t, w_t, w_t, w_a2, b_a)


SCORE_BOUND = 80.0


def _diffattn_kernel(lv_ref, q_ref, k_ref, v_ref, sg_ref, o_ref, vext, diag_mask, acc1, acc2, m1, m2,
                     *, tq, bounded):
    i = pl.program_id(2)
    lv = lv_ref[...]
    lam = (jnp.exp(jnp.sum(lv[0:1] * lv[1:2], axis=-1, keepdims=True))
           - jnp.exp(jnp.sum(lv[2:3] * lv[3:4], axis=-1, keepdims=True)) + LAM_INIT)

    @pl.when(i == 0)
    def _():
        S = v_ref.shape[0]
        lane_s = lax.broadcasted_iota(I32, (S, LANES), 1)
        vext[:, 0:DIFF_VDIM] = v_ref[...]
        vext[:, DIFF_VDIM:] = jnp.where(lane_s == 0, 1.0, 0.0).astype(BF16)
        row_chunk = lax.broadcasted_iota(I32, (tq, tq), 0) // CHUNK
        col_chunk = lax.broadcasted_iota(I32, (tq, tq), 1) // CHUNK
        diag_mask[...] = jnp.where(col_chunk <= row_chunk, 1.0, 0.0).astype(BF16)

    q = q_ref[...]
    lane = lax.broadcasted_iota(I32, (1, LANES), 1)
    zero = jnp.zeros_like(q)
    q1 = jnp.where(lane < DIFF_QKDIM, q, zero)
    q2 = jnp.where(lane < DIFF_QKDIM, zero, q)

    acc1[...] = jnp.zeros(acc1.shape, F32)
    acc2[...] = jnp.zeros(acc2.shape, F32)
    if not bounded:
        m1[...] = jnp.full(m1.shape, NEG_INF, F32)
        m2[...] = jnp.full(m2.shape, NEG_INF, F32)

    def step(j, masked):
        start = pl.multiple_of(j * tq, tq)
        k = k_ref[pl.ds(start, tq), :]
        v = vext[pl.ds(start, tq), :]
        for qc, acc, m in ((q1, acc1, m1), (q2, acc2, m2)):
            s = lax.dot_general(qc, k, NT_DIMS, preferred_element_type=F32)
            if bounded:
                p = jnp.exp2(s).astype(BF16)
                if masked:
                    p = p * diag_mask[...]
                acc[...] += _dot(p, v)
            else:
                if masked:
                    s = jnp.where(diag_mask[...] > 0, s, NEG_INF)
                m_old = m[...]
                m_new = jnp.maximum(m_old, jnp.max(s, axis=-1, keepdims=True))
                p = jnp.exp2(s - m_new)
                acc[...] = jnp.exp2(m_old - m_new) * acc[...] + _dot(p.astype(BF16), v)
                m[...] = m_new

    def body(j, carry):
        step(j, False)
        return carry

    lax.fori_loop(0, i, body, 0)
    step(i, True)

    a1 = acc1[...]
    a2 = acc2[...]
    o = (a1[:, :DIFF_VDIM] / a1[:, DIFF_VDIM:DIFF_VDIM + 1]
         - lam * (a2[:, :DIFF_VDIM] / a2[:, DIFF_VDIM:DIFF_VDIM + 1]))
    o_ref[...] = (_rms(o, sg_ref[...]) * (1.0 - LAM_INIT)).astype(BF16)


def _diffattn(lvec, qk, mid, subln_g, *, B, S, tq, bounded):
    T = B * S
    nq = S // tq
    kern = functools.partial(_diffattn_kernel, tq=tq, bounded=bounded)
    return pl.pallas_call(
        kern,
        grid=(B, DIFF_HEADS, nq),
        in_specs=[
            pl.BlockSpec((4, DIFF_QKDIM), lambda b, h, i: (0, 0)),
            pl.BlockSpec((tq, LANES), lambda b, h, i: (b * nq + i, h)),
            pl.BlockSpec((S, LANES), lambda b, h, i: (b, DIFF_HEADS + h)),
            pl.BlockSpec((S, LANES), lambda b, h, i: (b, h)),
            pl.BlockSpec((1, DIFF_VDIM), lambda b, h, i: (0, 0)),
        ],
        out_specs=pl.BlockSpec((tq, DIFF_VDIM), lambda b, h, i: (b * nq + i, h)),
        out_shape=jax.ShapeDtypeStruct((T, DIFF_HEADS * DIFF_VDIM), BF16),
        scratch_shapes=[
            pltpu.VMEM((S, 2 * DIFF_VDIM), BF16),
            pltpu.VMEM((tq, tq), BF16),
            pltpu.VMEM((tq, 2 * DIFF_VDIM), F32),
            pltpu.VMEM((tq, 2 * DIFF_VDIM), F32),
            pltpu.VMEM((tq, 1), F32),
            pltpu.VMEM((tq, 1), F32),
        ],
        compiler_params=_cparams(("parallel", "parallel", "arbitrary"), 32),
        name="diffattn_bounded" if bounded else "diffattn_online",
    )(lvec, qk, qk, mid, subln_g)


def _gla_kernel(q_ref, k_ref, v_ref, la_ref, sgr_ref, g_ref, o_ref, state, *, blk):
    @pl.when(pl.program_id(2) == 0)
    def _():
        state[...] = jnp.zeros(state.shape, F32)

    la_t = la_ref[...].T
    k_t = k_ref[...].astype(F32).T
    r = lax.broadcasted_iota(I32, (blk, blk), 0)
    c = lax.broadcasted_iota(I32, (blk, blk), 1)
    same = (r // CHUNK) == (c // CHUNK)
    tri = jnp.where(same & (r <= c), 1.0, 0.0).astype(BF16)
    ones = jnp.where(same, 1.0, 0.0).astype(BF16)
    hi, lo = _split_bf16(la_t)
    cum_t = _dot(hi, tri) + _dot(lo, tri)
    tot_t = _dot(hi, ones) + _dot(lo, ones)
    kd_t = k_t * jnp.exp(tot_t - cum_t)

    lane = lax.broadcasted_iota(I32, (1, LANES), 1)
    scale = GLA_KDIM ** -0.5
    for ck in range(blk // CHUNK):
        pair = slice((ck // 2) * LANES, (ck // 2 + 1) * LANES)
        rows = slice(ck * CHUNK, (ck + 1) * CHUNK)
        in_chunk = (lane // CHUNK) == (ck % 2)
        kd = jnp.where(in_chunk, kd_t[:, pair], 0.0).astype(BF16)
        d_state = _dot(kd, v_ref[pair, :])
        decay = jnp.exp(tot_t[:, ck * CHUNK:ck * CHUNK + 1])
        st = decay * state[...] + d_state
        state[...] = st
        o = _dot(q_ref[rows, :], st.astype(BF16)) * scale
        gate = sgr_ref[rows, :].astype(F32)
        o_ref[rows, :] = (_rms(o, g_ref[...]) * gate).astype(BF16)


def _gla(mid, log_a, sgr, out_g, *, B, S, blk):
    T = B * S
    ns = S // blk
    kern = functools.partial(_gla_kernel, blk=blk)
    q_col0 = (DIFF_HEADS * DIFF_VDIM) // GLA_KDIM
    k_col0 = q_col0 + GLA_HEADS
    v_col0 = (DIFF_HEADS * DIFF_VDIM + 2 * GLA_HEADS * GLA_KDIM) // GLA_VDIM
    return pl.pallas_call(
        kern,
        grid=(B, GLA_HEADS, ns),
        in_specs=[
            pl.BlockSpec((blk, GLA_KDIM), lambda b, h, s: (b * ns + s, q_col0 + h)),
            pl.BlockSpec((blk, GLA_KDIM), lambda b, h, s: (b * ns + s, k_col0 + h)),
            pl.BlockSpec((blk, GLA_VDIM), lambda b, h, s: (b * ns + s, v_col0 + h)),
            pl.BlockSpec((blk, GLA_KDIM), lambda b, h, s: (b * ns + s, h)),
            pl.BlockSpec((blk, GLA_VDIM), lambda b, h, s: (b * ns + s, h)),
            pl.BlockSpec((1, GLA_VDIM), lambda b, h, s: (0, 0)),
        ],
        out_specs=pl.BlockSpec((blk, GLA_VDIM), lambda b, h, s: (b * ns + s, h)),
        out_shape=jax.ShapeDtypeStruct((T, GLA_HEADS * GLA_VDIM), BF16),
        scratch_shapes=[pltpu.VMEM((GLA_KDIM, GLA_VDIM), F32)],
        compiler_params=_cparams(("parallel", "parallel", "arbitrary"), 32),
        name="gla",
    )(mid, mid, mid, log_a, sgr, out_g)


def _resident_w_map(n_j):
    return lambda i, j: (0, jnp.where(i == 0, j, n_j - 1))


def _outproj_kernel(a_ref, b_ref, wa_ref, wb_ref, x_ref, o_ref, w_scr):
    j = pl.program_id(1)
    ka = a_ref.shape[1]

    @pl.when(pl.program_id(0) == 0)
    def _():
        w_scr[j, 0:ka, :] = wa_ref[...].astype(BF16)
        w_scr[j, ka:, :] = wb_ref[...].astype(BF16)

    acc = _dot(a_ref[...], w_scr[j, 0:ka, :]) + _dot(b_ref[...], w_scr[j, ka:, :])
    o_ref[...] = x_ref[...] + acc


def _outproj(a, b, w_out, x2, *, tm, tn):
    T, ka = a.shape
    kb = b.shape[1]
    assert ka == kb
    D = w_out.shape[1]
    n_j = D // tn
    return pl.pallas_call(
        _outproj_kernel,
        grid=(T // tm, n_j),
        in_specs=[
            pl.BlockSpec((tm, ka), lambda i, j: (i, 0)),
            pl.BlockSpec((tm, kb), lambda i, j: (i, 0)),
            pl.BlockSpec((ka, tn), _resident_w_map(n_j)),
            pl.BlockSpec((kb, tn), lambda i, j: (1, jnp.where(i == 0, j, n_j - 1))),
            pl.BlockSpec((tm, tn), lambda i, j: (i, j)),
        ],
        out_specs=pl.BlockSpec((tm, tn), lambda i, j: (i, j)),
        out_shape=jax.ShapeDtypeStruct((T, D), F32),
        scratch_shapes=[pltpu.VMEM((n_j, ka + kb, tn), BF16)],
        compiler_params=_cparams(("arbitrary", "arbitrary"), 48),
        name="outproj",
    )(a, b, w_out, w_out, x2)


def _normproj_kernel(x_ref, g_ref, w_ref, hg_ref, o_ref, n_scr, *w_scr, n_norm):
    j = pl.program_id(1)

    @pl.when(j == 0)
    def _():
        n_scr[...] = _rms(x_ref[...], g_ref[...]).astype(BF16)

    if w_scr:
        @pl.when(pl.program_id(0) == 0)
        def _():
            w_scr[0][j] = w_ref[...].astype(BF16)
        y = _dot(n_scr[...], w_scr[0][j])
    else:
        y = _dot(n_scr[...], w_ref[...].astype(BF16))

    @pl.when(j < n_norm)
    def _():
        o_ref[...] = _rms(y, hg_ref[...]).astype(BF16)

    @pl.when(j >= n_norm)
    def _():
        o_ref[...] = y.astype(BF16)


def _normproj(x2, g, w, head_g, *, tm, tn, n_norm, name):
    T, D = x2.shape
    N = w.shape[1]
    kern = functools.partial(_normproj_kernel, n_norm=n_norm)
    n_j = N // tn
    resident = T // tm > 1
    return pl.pallas_call(
        kern,
        grid=(T // tm, n_j),
        in_specs=[
            pl.BlockSpec((tm, D), lambda i, j: (i, 0)),
            pl.BlockSpec((1, D), lambda i, j: (0, 0)),
            pl.BlockSpec((D, tn), _resident_w_map(n_j) if resident else (lambda i, j: (0, j))),
            pl.BlockSpec((1, tn), lambda i, j: (0, 0)),
        ],
        out_specs=pl.BlockSpec((tm, tn), lambda i, j: (i, j)),
        out_shape=jax.ShapeDtypeStruct((T, N), BF16),
        scratch_shapes=[pltpu.VMEM((tm, D), BF16)] + ([pltpu.VMEM((n_j, D, tn), BF16)] if resident else []),
        compiler_params=_cparams(("arbitrary", "arbitrary"), 48),
        name=name,
    )(x2, g, w, head_g)


def _cross_kernel(q_ref, k_ref, v_ref, w_ref, h_ref, o_ref, att_scr, w_scr, *, hdim):
    j = pl.program_id(1)

    @pl.when(pl.program_id(0) == 0)
    def _():
        w_scr[j] = w_ref[...].astype(BF16)

    @pl.when(j == 0)
    def _():
        for hd in range(CROSS_HEADS):
            cols = slice(hd * hdim, (hd + 1) * hdim)
            s = lax.dot_general(q_ref[:, cols], k_ref[:, cols], NT_DIMS, preferred_element_type=F32)
            p = jnp.exp(s - jnp.max(s, axis=-1, keepdims=True))
            l = jnp.sum(p, axis=-1, keepdims=True)
            att_scr[:, cols] = (_dot(p.astype(BF16), v_ref[:, cols]) / l).astype(BF16)

    o_ref[...] = h_ref[...] + _dot(att_scr[...], w_scr[j])


def _cross(qc, kv, w_co, h1, *, S, n_mem, tm, tn):
    T, D = qc.shape
    per_b = S // tm
    kern = functools.partial(_cross_kernel, hdim=D // CROSS_HEADS)
    n_j = D // tn
    return pl.pallas_call(
        kern,
        grid=(T // tm, n_j),
        in_specs=[
            pl.BlockSpec((tm, D), lambda i, j: (i, 0)),
            pl.BlockSpec((n_mem, D), lambda i, j: (i // per_b, 0)),
            pl.BlockSpec((n_mem, D), lambda i, j: (i // per_b, 1)),
            pl.BlockSpec((D, tn), _resident_w_map(n_j)),
            pl.BlockSpec((tm, tn), lambda i, j: (i, j)),
        ],
        out_specs=pl.BlockSpec((tm, tn), lambda i, j: (i, j)),
        out_shape=jax.ShapeDtypeStruct((T, D), F32),
        scratch_shapes=[pltpu.VMEM((tm, D), BF16), pltpu.VMEM((n_j, D, tn), BF16)],
        compiler_params=_cparams(("arbitrary", "arbitrary"), 48),
        name="cross",
    )(qc, kv, kv, w_co, h1)


R_ROWS = SUBLANES + N_EXPERTS


def _router_kernel(h_ref, g_ref, wt_ref, b_ref, eid_ref, gate_ref):
    n = _rms(h_ref[...], g_ref[...])
    nh, nl = _split_bf16(n)
    wh, wl = _split_bf16(wt_ref[...])
    nt = functools.partial(lax.dot_general, dimension_numbers=NT_DIMS, preferred_element_type=F32)
    lg = nt(wh, nh) + nt(wh, nl) + nt(wl, nh) + b_ref[...]

    tm = lg.shape[1]
    row = lax.broadcasted_iota(I32, (SUBLANES, tm), 0)

    def first_argmax(v, vmax):
        return jnp.min(jnp.where(v == vmax, row, SUBLANES), axis=0, keepdims=True)

    gl = jnp.where(row < N_GROUPS, lg[0:SUBLANES], NEG_INF)
    gmax = jnp.max(gl, axis=0, keepdims=True)
    grp = first_argmax(gl, gmax)
    grp_w = 1.0 / jnp.sum(jnp.exp(gl - gmax), axis=0, keepdims=True)

    sel = jnp.zeros((SUBLANES, tm), F32)
    for gi in range(N_GROUPS):
        lo = SUBLANES + gi * EXPERTS_PER_GROUP
        sel = jnp.where(grp == gi, lg[lo:lo + EXPERTS_PER_GROUP], sel)
    e = jnp.exp(sel - jnp.max(sel, axis=0, keepdims=True))
    prob = e / jnp.sum(e, axis=0, keepdims=True)
    p1 = jnp.max(prob, axis=0, keepdims=True)
    i1 = first_argmax(prob, p1)
    rest = jnp.where(row == i1, -1.0, prob)
    p2 = jnp.max(rest, axis=0, keepdims=True)
    i2 = first_argmax(rest, p2)
    den = p1 + p2
    base = grp * EXPERTS_PER_GROUP
    eid_ref[...] = jnp.where(row == 0, base + i1, jnp.where(row == 1, base + i2, 0))
    gate_ref[...] = jnp.where(row == 0, grp_w * p1 / den, jnp.where(row == 1, grp_w * p2 / den, 0.0))


def _router(h2, g_ffn, w_rt, b_r, *, tm):
    T, D = h2.shape
    return pl.pallas_call(
        _router_kernel,
        grid=(T // tm,),
        in_specs=[
            pl.BlockSpec((tm, D), lambda i: (i, 0)),
            pl.BlockSpec((1, D), lambda i: (0, 0)),
            pl.BlockSpec((R_ROWS, D), lambda i: (0, 0)),
            pl.BlockSpec((R_ROWS, 1), lambda i: (0, 0)),
        ],
        out_specs=[
            pl.BlockSpec((SUBLANES, tm), lambda i: (0, i)),
            pl.BlockSpec((SUBLANES, tm), lambda i: (0, i)),
        ],
        out_shape=[
            jax.ShapeDtypeStruct((SUBLANES, T), I32),
            jax.ShapeDtypeStruct((SUBLANES, T), F32),
        ],
        compiler_params=_cparams(("parallel",), 32),
        name="router",
    )(h2, g_ffn, w_rt, b_r)


def _rank_kernel(eid_ref, rank_ref, cnt_ref, carry):
    @pl.when(pl.program_id(0) == 0)
    def _():
        carry[...] = jnp.zeros(carry.shape, F32)

    tm = eid_ref.shape[1]
    e0 = eid_ref[0:1, :]
    e1 = eid_ref[1:2, :]
    erow = lax.broadcasted_iota(I32, (N_EXPERTS, tm), 0)
    hit = jnp.where((erow == e0) | (erow == e1), 1.0, 0.0)
    r = lax.broadcasted_iota(I32, (tm, tm), 0)
    c = lax.broadcasted_iota(I32, (tm, tm), 1)
    before = jnp.where(r < c, 1.0, 0.0).astype(BF16)
    pre = _dot(hit.astype(BF16), before) + carry[:, 0:1]
    rank0 = jnp.sum(jnp.where(erow == e0, pre, 0.0), axis=0, keepdims=True)
    rank1 = jnp.sum(jnp.where(erow == e1, pre, 0.0), axis=0, keepdims=True)
    row = lax.broadcasted_iota(I32, (SUBLANES, tm), 0)
    rank_ref[...] = jnp.where(row == 0, rank0, jnp.where(row == 1, rank1, 0.0)).astype(I32)
    total = carry[...] + jnp.sum(hit, axis=1, keepdims=True)
    carry[...] = total
    cnt_ref[...] = total.astype(I32)


def _rank(eid, *, tm):
    T = eid.shape[1]
    return pl.pallas_call(
        _rank_kernel,
        grid=(T // tm,),
        in_specs=[pl.BlockSpec((SUBLANES, tm), lambda i: (0, i))],
        out_specs=[
            pl.BlockSpec((SUBLANES, tm), lambda i: (0, i)),
            pl.BlockSpec((N_EXPERTS, LANES), lambda i: (0, 0)),
        ],
        out_shape=[
            jax.ShapeDtypeStruct((SUBLANES, T), I32),
            jax.ShapeDtypeStruct((N_EXPERTS, LANES), I32),
        ],
        scratch_shapes=[pltpu.VMEM((N_EXPERTS, LANES), F32)],
        compiler_params=_cparams(("arbitrary",), 32),
        name="rank",
    )(eid)


def _expert_kernel(be_ref, first_ref, nxt_ref, nb_ref, tok0_ref, tok_ref, h_hbm, g_ref,
                   wg_hbm, wu_hbm, wd_hbm, y_ref,
                   xbuf0, xbuf1, xsem, wg_f, wu_f, wd_f, wsem, wg_b, wu_b, wd_b, *, tb, n_blocks):
    i = pl.program_id(0)
    nb = nb_ref[0]
    ic = jnp.minimum(i, n_blocks - 1)
    xbufs = (xbuf0, xbuf1)

    def row_copy(tok, r, p):
        return pltpu.make_async_copy(h_hbm.at[pl.ds(tok, 1), :], xbufs[p].at[pl.ds(r, 1), :], xsem.at[p])

    def weight_copies(e):
        return (pltpu.make_async_copy(wg_hbm.at[e], wg_f, wsem.at[0]),
                pltpu.make_async_copy(wu_hbm.at[e], wu_f, wsem.at[1]),
                pltpu.make_async_copy(wd_hbm.at[e], wd_f, wsem.at[2]))

    @pl.when(i == 0)
    def _():
        for cp in weight_copies(be_ref[0]):
            cp.start()

        def body(r, carry):
            row_copy(tok0_ref[0, 0, r], r, 0).start()
            return carry
        lax.fori_loop(0, tb, body, 0)

    first = (i < nb) & (first_ref[ic] == 1)

    @pl.when(first)
    def _():
        for cp in weight_copies(0):
            cp.wait()
        wg_b[...] = wg_f[...].astype(BF16)
        wu_b[...] = wu_f[...].astype(BF16)
        wd_b[...] = wd_f[...].astype(BF16)

    @pl.when(first & (nxt_ref[ic] >= 0))
    def _():
        for cp in weight_copies(jnp.maximum(nxt_ref[ic], 0)):
            cp.start()

    for p in range(2):
        cur = xbufs[p]

        @pl.when((i <= nb) & (i % 2 == p))
        def _():
            pltpu.make_async_copy(h_hbm.at[pl.ds(0, tb), :], cur, xsem.at[p]).wait()

        @pl.when((i < nb) & (i % 2 == p))
        def _():
            for r in range(tb):
                row_copy(tok_ref[0, 0, r], r, 1 - p).start()
            n = _rms(cur[...], g_ref[...]).astype(BF16)
            a = _dot(n, wg_b[...])
            hdn = (a / (1.0 + jnp.exp(-a))) * _dot(n, wu_b[...])
            y_ref[...] = _dot(hdn.astype(BF16), wd_b[...])

    @pl.when((i >= nb) & (i < n_blocks))
    def _():
        y_ref[...] = jnp.zeros(y_ref.shape, F32)


def _experts(buf_tok, blk_e, blk_first, blk_next, nb, h2, g_ffn, w_gate, w_up, w_down, *, tb, n_blocks):
    T, D = h2.shape
    De = w_gate.shape[2]
    kern = functools.partial(_expert_kernel, tb=tb, n_blocks=n_blocks)

    tok3 = buf_tok.reshape(n_blocks + 1, 1, tb)

    grid_spec = pltpu.PrefetchScalarGridSpec(
        num_scalar_prefetch=4,
        grid=(n_blocks + 1,),
        in_specs=[
            pl.BlockSpec((1, 1, tb), lambda i, *_: (0, 0, 0), memory_space=pltpu.SMEM),
            pl.BlockSpec((1, 1, tb), lambda i, *_: (jnp.minimum(i + 1, n_blocks), 0, 0),
                         memory_space=pltpu.SMEM),
            pl.BlockSpec(memory_space=pl.ANY),
            pl.BlockSpec((1, D), lambda i, *_: (0, 0)),
            pl.BlockSpec(memory_space=pl.ANY),
            pl.BlockSpec(memory_space=pl.ANY),
            pl.BlockSpec(memory_space=pl.ANY),
        ],
        out_specs=pl.BlockSpec((tb, D), lambda i, *_: (jnp.minimum(i, n_blocks - 1), 0)),
        scratch_shapes=[
            pltpu.VMEM((tb, D), F32),
            pltpu.VMEM((tb, D), F32),
            pltpu.SemaphoreType.DMA((2,)),
            pltpu.VMEM((D, De), F32),
            pltpu.VMEM((D, De), F32),
            pltpu.VMEM((De, D), F32),
            pltpu.SemaphoreType.DMA((3,)),
            pltpu.VMEM((D, De), BF16),
            pltpu.VMEM((D, De), BF16),
            pltpu.VMEM((De, D), BF16),
        ],
    )
    return pl.pallas_call(
        kern,
        grid_spec=grid_spec,
        out_shape=jax.ShapeDtypeStruct((n_blocks * tb, D), F32),
        compiler_params=_cparams(("arbitrary",), 48),
        name="experts",
    )(blk_e, blk_first, blk_next, nb, tok3, tok3, h2, g_ffn, w_gate, w_up, w_down)


def _combine_kernel(dest_ref, y_hbm, h_ref, gate_ref, o_ref, ybuf, sem, *, tm, n_tok):
    i = pl.program_id(0)
    n = pl.num_programs(0)

    def row_copy(d, k, r, slot):
        return pltpu.make_async_copy(y_hbm.at[pl.ds(d, 1), :], ybuf.at[slot, k, pl.ds(r, 1), :], sem.at[slot])

    def start_gather(blk, slot):
        def body(r, carry):
            for k in range(TOP_K):
                row_copy(dest_ref[k * n_tok + blk * tm + r], k, r, slot).start()
            return carry
        lax.fori_loop(0, tm, body, 0, unroll=8)

    def wait_gather(slot):
        for k in range(TOP_K):
            pltpu.make_async_copy(y_hbm.at[pl.ds(0, tm), :], ybuf.at[slot, k], sem.at[slot]).wait()

    @pl.when(i == 0)
    def _():
        start_gather(0, 0)

    @pl.when(i + 1 < n)
    def _():
        start_gather(i + 1, (i + 1) % 2)

    slot = i % 2
    wait_gather(slot)
    gt = gate_ref[...]
    o_ref[...] = h_ref[...] + gt[:, 0:1] * ybuf[slot, 0] + gt[:, 1:2] * ybuf[slot, 1]


def _combine(dest, yb, h2, gate, *, tm):
    T, D = h2.shape
    kern = functools.partial(_combine_kernel, tm=tm, n_tok=T)
    grid_spec = pltpu.PrefetchScalarGridSpec(
        num_scalar_prefetch=1,
        grid=(T // tm,),
        in_specs=[
            pl.BlockSpec(memory_space=pl.ANY),
            pl.BlockSpec((tm, D), lambda i, d: (i, 0)),
            pl.BlockSpec((tm, TOP_K), lambda i, d: (i, 0)),
        ],
        out_specs=pl.BlockSpec((tm, D), lambda i, d: (i, 0)),
        scratch_shapes=[
            pltpu.VMEM((2, TOP_K, tm, D), F32),
            pltpu.SemaphoreType.DMA((2,)),
        ],
    )
    return pl.pallas_call(
        kern,
        grid_spec=grid_spec,
        out_shape=jax.ShapeDtypeStruct((T, D), F32),
        compiler_params=_cparams(("arbitrary",), 40),
        name="combine",
    )(dest, yb, h2, gate)


EXPERT_ROWS = 256


def kernel(x, mem, positions, g_attn, w_in, q_norm_g, k_norm_g, lambda_q1, lambda_k1, lambda_q2, lambda_k2, diff_subln_g, gla_w_a2, gla_b_a, gla_out_g, w_out, g_cross, g_mem, w_cq, w_ckv, cq_norm_g, ck_norm_g, w_co, g_ffn, w_router_grp, b_router_grp, w_router_exp, b_router_exp, w_gate, w_up, w_down):
    B, S, D = x.shape
    T = B * S
    n_mem = mem.shape[1]
    l = 0
    x2 = x.reshape(T, D)

    half = DIFF_QKDIM // 2
    freq = ROPE_THETA ** (-jnp.arange(half, dtype=F32) / half)
    freq = jnp.tile(freq, LANES // half)[None, :]
    q_scale = math.log2(math.e) * DIFF_QKDIM ** -0.5
    qkg = jnp.stack([jnp.tile(q_norm_g[l], 2) * q_scale, jnp.tile(k_norm_g[l], 2)])
    score_bound = 1.01 * DIFF_QKDIM * q_scale * jnp.max(jnp.abs(q_norm_g[l])) * jnp.max(jnp.abs(k_norm_g[l]))
    lvec = jnp.stack([lambda_q1[l], lambda_k1[l], lambda_q2[l], lambda_k2[l]])

    qk, mid, log_a, sgr = _inproj(x2, g_attn[l][None], positions.reshape(T, 1), freq, qkg, w_in[l].T,
                                  gla_w_a2[l], gla_b_a[l][None], tm=1024)
    diffattn = functools.partial(_diffattn, lvec, qk, mid, diff_subln_g[l][None], B=B, S=S, tq=512)
    mix_d = lax.cond(score_bound <= SCORE_BOUND,
                     functools.partial(diffattn, bounded=True), functools.partial(diffattn, bounded=False))
    mix_g = _gla(mid, log_a, sgr, gla_out_g[l][None], B=B, S=S, blk=512)
    h1 = _outproj(mix_d, mix_g, w_out[l], x2, tm=1024, tn=512)

    hdim = D // CROSS_HEADS
    qc = _normproj(h1, g_cross[l][None], w_cq[l], cq_norm_g[l][None] * (hdim ** -0.5),
                   tm=1024, tn=hdim, n_norm=CROSS_HEADS, name="cq")
    kv = _normproj(mem.reshape(B * n_mem, D), g_mem[l][None], w_ckv[l], ck_norm_g[l][None],
                   tm=B * n_mem, tn=hdim, n_norm=CROSS_HEADS, name="ckv")
    h2 = _cross(qc, kv, w_co[l], h1, S=S, n_mem=n_mem, tm=512, tn=512)

    w_rt = jnp.concatenate([w_router_grp[l].T, jnp.zeros((SUBLANES - N_GROUPS, D), F32), w_router_exp[l].T])
    b_r = jnp.concatenate([b_router_grp[l], jnp.zeros((SUBLANES - N_GROUPS,), F32), b_router_exp[l]])[:, None]
    eid, gate = _router(h2, g_ffn[l][None], w_rt, b_r, tm=512)
    rank, cnt = _rank(eid, tm=512)

    tb = EXPERT_ROWS
    n_blocks = (T * TOP_K + N_EXPERTS * (tb - 1) + tb - 1) // tb
    counts = cnt[:, 0]
    pcounts = ((counts + tb - 1) // tb) * tb
    pends = jnp.cumsum(pcounts)
    pstarts = pends - pcounts
    nb = (pends[-1:] // tb).astype(I32)
    eids = jnp.arange(N_EXPERTS, dtype=I32)
    blk_start = jnp.arange(n_blocks, dtype=I32) * tb
    blk_e = jnp.minimum(jnp.sum(pends[None, :] <= blk_start[:, None], axis=1), N_EXPERTS - 1).astype(I32)
    blk_first = jnp.concatenate([jnp.ones((1,), I32), (blk_e[1:] != blk_e[:-1]).astype(I32)])
    used = jnp.where(counts > 0, eids, N_EXPERTS)
    next_used = jnp.concatenate([lax.cummin(used[::-1])[::-1][1:], jnp.full((1,), N_EXPERTS, I32)])
    next_used = jnp.where(next_used < N_EXPERTS, next_used, -1)
    blk_next = jnp.sum(jnp.where(blk_e[:, None] == eids, next_used, 0), axis=1).astype(I32)
    pick = eid[:TOP_K]
    pstart_of = jnp.sum(jnp.where(pick[..., None] == eids, pstarts, 0), axis=-1)
    dest = (pstart_of + rank[:TOP_K]).astype(I32).reshape(-1)
    tok = jnp.tile(jnp.arange(T, dtype=I32), TOP_K)
    buf_tok = jnp.zeros(((n_blocks + 1) * tb,), I32).at[dest].set(tok)

    yb = _experts(buf_tok, blk_e, blk_first, blk_next, nb, h2, g_ffn[l][None], w_gate[l], w_up[l], w_down[l],
                  tb=tb, n_blocks=n_blocks)
    out = _combine(dest, yb, h2, gate[:TOP_K].T, tm=256)
    return out.reshape(B, S, D)
```

```python
import functools
import math

import jax
import jax.numpy as jnp
from jax import lax
from jax.experimental import pallas as pl
from jax.experimental.pallas import tpu as pltpu

F32 = jnp.float32
BF16 = jnp.bfloat16
I32 = jnp.int32

LANES = 128
SUBLANES = 8

CHUNK = 64
ROPE_THETA = 10000.0
NORM_EPS = 1e-6
NEG_INF = -1e30
DIFF_HEADS = 8
DIFF_VDIM = 128
DIFF_QKDIM = 64
GLA_HEADS = 4
GLA_VDIM = 256
GLA_KDIM = 128
GLA_GATE_RANK = 16
GLA_TAU = 16.0
CROSS_HEADS = 4
N_GROUPS = 4
EXPERTS_PER_GROUP = 8
N_EXPERTS = N_GROUPS * EXPERTS_PER_GROUP
TOP_K = 2
LAM_INIT = 0.8 - 0.6 * math.exp(-0.3 * 0)

NT_DIMS = (((1,), (1,)), ((), ()))


def _cparams(semantics, vmem_mib):
    return pltpu.CompilerParams(dimension_semantics=semantics,
                                vmem_limit_bytes=vmem_mib * 1024 * 1024)


def _dot(a, b):
    return jnp.dot(a, b, preferred_element_type=F32)


def _dot_nt(a, b):
    return lax.dot_general(a, b, NT_DIMS, preferred_element_type=F32)


def _rms(x, g):
    ms = jnp.mean(x * x, axis=-1, keepdims=True)
    return x * lax.rsqrt(ms + NORM_EPS) * g


def _split_bf16(x):
    hi = x.astype(BF16)
    lo = (x - hi.astype(F32)).astype(BF16)
    return hi, lo


TN = 512
J_QK = 4
J_MID = 6
J_LR = J_QK + J_MID
J_GR = J_LR + 1
N_J = J_GR + 2


def _inproj_kernel(x_ref, g_ref, pos_ref, freq_ref, qkg_ref, w_ref, wgr_ref, wlr_ref, wa2_ref, ba_ref,
                   qk_ref, mid_ref, loga_ref, sgr_ref, n_scr, cos_scr, sin_scr, y_scr):
    j = pl.program_id(1)

    @pl.when(j == 0)
    def _():
        n_scr[...] = _rms(x_ref[...], g_ref[...]).astype(BF16)
        ang = pos_ref[...].astype(F32) * freq_ref[...]
        cos_scr[...] = jnp.cos(ang)
        sin_scr[...] = jnp.sin(ang)

    def qk_epilogue(jq):
        y_prev = y_scr.at[jq % 2]
        lane = lax.broadcasted_iota(I32, (1, LANES), 1)
        low_seg = lane < DIFF_QKDIM
        first_half = (lane % DIFF_QKDIM) < (DIFF_QKDIM // 2)
        gain = qkg_ref[jq // (J_QK // 2):jq // (J_QK // 2) + 1, :]
        cos = cos_scr[...]
        sin = sin_scr[...]
        for c in range(TN // LANES):
            yb = y_prev[:, c * LANES:(c + 1) * LANES]
            y2 = yb * yb
            s_lo = jnp.sum(jnp.where(low_seg, y2, 0.0), axis=-1, keepdims=True)
            s_hi = jnp.sum(jnp.where(low_seg, 0.0, y2), axis=-1, keepdims=True)
            ms = jnp.where(low_seg, s_lo, s_hi) * (1.0 / DIFF_QKDIM)
            yn = yb * lax.rsqrt(ms + NORM_EPS) * gain
            rot = jnp.where(first_half,
                            -pltpu.roll(yn, LANES - DIFF_QKDIM // 2, 1),
                            pltpu.roll(yn, DIFF_QKDIM // 2, 1))
            qk_ref[:, c * LANES:(c + 1) * LANES] = (yn * cos + rot * sin).astype(BF16)

    for jq in range(J_QK + 1):
        @pl.when(j == jq)
        def _():
            y = _dot_nt(n_scr[...], w_ref[...].astype(BF16))
            if jq > 0:
                qk_epilogue(jq - 1)
            if jq < J_QK:
                y_scr[jq % 2] = y
            else:
                mid_ref[...] = y.astype(BF16)

    @pl.when((j > J_QK) & (j < J_LR))
    def _():
        mid_ref[...] = _dot_nt(n_scr[...], w_ref[...].astype(BF16)).astype(BF16)

    @pl.when(j == J_LR)
    def _():
        lr = _dot_nt(n_scr[...], wlr_ref[...].astype(BF16))
        z = _dot(lr.astype(BF16), wa2_ref[...].astype(BF16)) + ba_ref[...]
        log_sig = jnp.minimum(z, 0.0) - jnp.log(1.0 + jnp.exp(-jnp.abs(z)))
        loga_ref[...] = log_sig * (1.0 / GLA_TAU)

    @pl.when(j >= J_GR)
    def _():
        y = _dot_nt(n_scr[...], wgr_ref[...].astype(BF16))
        sgr_ref[...] = (y / (1.0 + jnp.exp(-y))).astype(BF16)


def _inproj(x2, g_attn, pos2, freq, qkg, w_t, w_a2, b_a, *, tm):
    T, D = x2.shape
    n_mid = J_MID * TN
    n_gk = GLA_HEADS * GLA_KDIM
    lr0 = J_LR * TN
    gr0 = lr0 + GLA_GATE_RANK
    n_gr = w_t.shape[0] - gr0
    assert n_gr == 2 * TN and lr0 % GLA_GATE_RANK == 0
    return pl.pallas_call(
        _inproj_kernel,
        grid=(T // tm, N_J),
        in_specs=[
            pl.BlockSpec((tm, D), lambda i, j: (i, 0)),
            pl.BlockSpec((1, D), lambda i, j: (0, 0)),
            pl.BlockSpec((tm, 1), lambda i, j: (i, 0)),
            pl.BlockSpec((1, LANES), lambda i, j: (0, 0)),
            pl.BlockSpec((2, LANES), lambda i, j: (0, 0)),
            pl.BlockSpec((TN, D), lambda i, j: (jnp.minimum(j, J_LR - 1), 0)),
            pl.BlockSpec((pl.Element(TN), pl.Element(D)),
                         lambda i, j: (pl.multiple_of(gr0 + TN * jnp.clip(j - J_GR, 0, 1), SUBLANES), 0)),
            pl.BlockSpec((GLA_GATE_RANK, D), lambda i, j: (lr0 // GLA_GATE_RANK, 0)),
            pl.BlockSpec((GLA_GATE_RANK, n_gk), lambda i, j: (0, 0)),
            pl.BlockSpec((1, n_gk), lambda i, j: (0, 0)),
        ],
        out_specs=[
            pl.BlockSpec((tm, TN), lambda i, j: (i, jnp.clip(j - 1, 0, J_QK - 1))),
            pl.BlockSpec((tm, TN), lambda i, j: (i, jnp.clip(j - J_QK, 0, J_MID - 1))),
            pl.BlockSpec((tm, n_gk), lambda i, j: (i, 0)),
            pl.BlockSpec((tm, TN), lambda i, j: (i, jnp.clip(j - J_GR, 0, 1))),
        ],
        out_shape=[
            jax.ShapeDtypeStruct((T, J_QK * TN), BF16),
            jax.ShapeDtypeStruct((T, n_mid), BF16),
            jax.ShapeDtypeStruct((T, n_gk), F32),
            jax.ShapeDtypeStruct((T, n_gr), BF16),
        ],
        scratch_shapes=[
            pltpu.VMEM((tm, D), BF16),
            pltpu.VMEM((tm, LANES), F32),
            pltpu.VMEM((tm, LANES), F32),
            pltpu.VMEM((2, tm, TN), F32),
        ],
        compiler_params=_cparams(("parallel", "arbitrary"), 56),
        name="inproj",
    )(x2, g_attn, pos2, freq, qkg, w_t, w_t, w_t, w_a2, b_a)


SCORE_BOUND = 80.0


def _diffattn_kernel(lv_ref, q_ref, k_ref, v_ref, sg_ref, o_ref, vext, diag_mask, acc1, acc2, m1, m2,
                     *, tq, bounded):
    i = pl.program_id(2)
    lv = lv_ref[...]
    lam = (jnp.exp(jnp.sum(lv[0:1] * lv[1:2], axis=-1, keepdims=True))
           - jnp.exp(jnp.sum(lv[2:3] * lv[3:4], axis=-1, keepdims=True)) + LAM_INIT)

    @pl.when(i == 0)
    def _():
        S = v_ref.shape[0]
        lane_s = lax.broadcasted_iota(I32, (S, LANES), 1)
        vext[:, 0:DIFF_VDIM] = v_ref[...]
        vext[:, DIFF_VDIM:] = jnp.where(lane_s == 0, 1.0, 0.0).astype(BF16)
        row_chunk = lax.broadcasted_iota(I32, (tq, tq), 0) // CHUNK
        col_chunk = lax.broadcasted_iota(I32, (tq, tq), 1) // CHUNK
        diag_mask[...] = jnp.where(col_chunk <= row_chunk, 1.0, 0.0).astype(BF16)

    q = q_ref[...]
    lane = lax.broadcasted_iota(I32, (1, LANES), 1)
    zero = jnp.zeros_like(q)
    q1 = jnp.where(lane < DIFF_QKDIM, q, zero)
    q2 = jnp.where(lane < DIFF_QKDIM, zero, q)

    acc1[...] = jnp.zeros(acc1.shape, F32)
    acc2[...] = jnp.zeros(acc2.shape, F32)
    if not bounded:
        m1[...] = jnp.full(m1.shape, NEG_INF, F32)
        m2[...] = jnp.full(m2.shape, NEG_INF, F32)

    def step(j, masked):
        start = pl.multiple_of(j * tq, tq)
        k = k_ref[pl.ds(start, tq), :]
        v = vext[pl.ds(start, tq), :]
        for qc, acc, m in ((q1, acc1, m1), (q2, acc2, m2)):
            s = lax.dot_general(qc, k, NT_DIMS, preferred_element_type=F32)
            if bounded:
                p = jnp.exp2(s).astype(BF16)
                if masked:
                    p = p * diag_mask[...]
                acc[...] += _dot(p, v)
            else:
                if masked:
                    s = jnp.where(diag_mask[...] > 0, s, NEG_INF)
                m_old = m[...]
                m_new = jnp.maximum(m_old, jnp.max(s, axis=-1, keepdims=True))
                p = jnp.exp2(s - m_new)
                acc[...] = jnp.exp2(m_old - m_new) * acc[...] + _dot(p.astype(BF16), v)
                m[...] = m_new

    def body(jj, carry):
        step(2 * jj, False)
        step(2 * jj + 1, False)
        return carry

    lax.fori_loop(0, i // 2, body, 0)

    @pl.when(i % 2 == 1)
    def _():
        step(i - 1, False)
        step(i, True)

    @pl.when(i % 2 == 0)
    def _():
        step(i, True)

    a1 = acc1[...]
    a2 = acc2[...]
    o = (a1[:, :DIFF_VDIM] / a1[:, DIFF_VDIM:DIFF_VDIM + 1]
         - lam * (a2[:, :DIFF_VDIM] / a2[:, DIFF_VDIM:DIFF_VDIM + 1]))
    o_ref[...] = (_rms(o, sg_ref[...]) * (1.0 - LAM_INIT)).astype(BF16)


def _diffattn(lvec, qk, mid, subln_g, *, B, S, tq, bounded):
    T = B * S
    nq = S // tq
    kern = functools.partial(_diffattn_kernel, tq=tq, bounded=bounded)
    return pl.pallas_call(
        kern,
        grid=(B, DIFF_HEADS, nq),
        in_specs=[
            pl.BlockSpec((4, DIFF_QKDIM), lambda b, h, i: (0, 0)),
            pl.BlockSpec((tq, LANES), lambda b, h, i: (b * nq + i, h)),
            pl.BlockSpec((S, LANES), lambda b, h, i: (b, DIFF_HEADS + h)),
            pl.BlockSpec((S, LANES), lambda b, h, i: (b, h)),
            pl.BlockSpec((1, DIFF_VDIM), lambda b, h, i: (0, 0)),
        ],
        out_specs=pl.BlockSpec((tq, DIFF_VDIM), lambda b, h, i: (b * nq + i, h)),
        out_shape=jax.ShapeDtypeStruct((T, DIFF_HEADS * DIFF_VDIM), BF16),
        scratch_shapes=[
            pltpu.VMEM((S, 2 * DIFF_VDIM), BF16),
            pltpu.VMEM((tq, tq), BF16),
            pltpu.VMEM((tq, 2 * DIFF_VDIM), F32),
            pltpu.VMEM((tq, 2 * DIFF_VDIM), F32),
            pltpu.VMEM((tq, 1), F32),
            pltpu.VMEM((tq, 1), F32),
        ],
        compiler_params=_cparams(("parallel", "parallel", "arbitrary"), 32),
        name="diffattn_bounded" if bounded else "diffattn_online",
    )(lvec, qk, qk, mid, subln_g)


def _gla_kernel(q_ref, k_ref, v_ref, la_ref, sgr_ref, g_ref, o_ref, state, *, blk):
    @pl.when(pl.program_id(2) == 0)
    def _():
        state[...] = jnp.zeros(state.shape, F32)

    la_t = la_ref[...].T
    k_t = k_ref[...].astype(F32).T
    r = lax.broadcasted_iota(I32, (blk, blk), 0)
    c = lax.broadcasted_iota(I32, (blk, blk), 1)
    same = (r // CHUNK) == (c // CHUNK)
    tri = jnp.where(same & (r <= c), 1.0, 0.0).astype(BF16)
    ones = jnp.where(same, 1.0, 0.0).astype(BF16)
    hi, lo = _split_bf16(la_t)
    cum_t = _dot(hi, tri) + _dot(lo, tri)
    tot_t = _dot(hi, ones) + _dot(lo, ones)
    kd_t = k_t * jnp.exp(tot_t - cum_t)

    lane = lax.broadcasted_iota(I32, (1, LANES), 1)
    scale = GLA_KDIM ** -0.5
    for ck in range(blk // CHUNK):
        pair = slice((ck // 2) * LANES, (ck // 2 + 1) * LANES)
        rows = slice(ck * CHUNK, (ck + 1) * CHUNK)
        in_chunk = (lane // CHUNK) == (ck % 2)
        kd = jnp.where(in_chunk, kd_t[:, pair], 0.0).astype(BF16)
        d_state = _dot(kd, v_ref[pair, :])
        decay = jnp.exp(tot_t[:, ck * CHUNK:ck * CHUNK + 1])
        st = decay * state[...] + d_state
        state[...] = st
        o = _dot(q_ref[rows, :], st.astype(BF16)) * scale
        gate = sgr_ref[rows, :].astype(F32)
        o_ref[rows, :] = (_rms(o, g_ref[...]) * gate).astype(BF16)


def _gla(mid, log_a, sgr, out_g, *, B, S, blk):
    T = B * S
    ns = S // blk
    kern = functools.partial(_gla_kernel, blk=blk)
    q_col0 = (DIFF_HEADS * DIFF_VDIM) // GLA_KDIM
    k_col0 = q_col0 + GLA_HEADS
    v_col0 = (DIFF_HEADS * DIFF_VDIM + 2 * GLA_HEADS * GLA_KDIM) // GLA_VDIM
    return pl.pallas_call(
        kern,
        grid=(B, GLA_HEADS, ns),
        in_specs=[
            pl.BlockSpec((blk, GLA_KDIM), lambda b, h, s: (b * ns + s, q_col0 + h)),
            pl.BlockSpec((blk, GLA_KDIM), lambda b, h, s: (b * ns + s, k_col0 + h)),
            pl.BlockSpec((blk, GLA_VDIM), lambda b, h, s: (b * ns + s, v_col0 + h)),
            pl.BlockSpec((blk, GLA_KDIM), lambda b, h, s: (b * ns + s, h)),
            pl.BlockSpec((blk, GLA_VDIM), lambda b, h, s: (b * ns + s, h)),
            pl.BlockSpec((1, GLA_VDIM), lambda b, h, s: (0, 0)),
        ],
        out_specs=pl.BlockSpec((blk, GLA_VDIM), lambda b, h, s: (b * ns + s, h)),
        out_shape=jax.ShapeDtypeStruct((T, GLA_HEADS * GLA_VDIM), BF16),
        scratch_shapes=[pltpu.VMEM((GLA_KDIM, GLA_VDIM), F32)],
        compiler_params=_cparams(("parallel", "parallel", "arbitrary"), 32),
        name="gla",
    )(mid, mid, mid, log_a, sgr, out_g)


def _resident_w_map(n_j):
    return lambda i, j: (0, jnp.where(i == 0, j, n_j - 1))


def _outproj_kernel(a_ref, b_ref, wa_ref, wb_ref, x_ref, o_ref, w_scr):
    j = pl.program_id(1)
    ka = a_ref.shape[1]

    @pl.when(pl.program_id(0) == 0)
    def _():
        w_scr[j, 0:ka, :] = wa_ref[...].astype(BF16)
        w_scr[j, ka:, :] = wb_ref[...].astype(BF16)

    acc = _dot(a_ref[...], w_scr[j, 0:ka, :]) + _dot(b_ref[...], w_scr[j, ka:, :])
    o_ref[...] = x_ref[...] + acc


def _outproj(a, b, w_out, x2, *, tm, tn):
    T, ka = a.shape
    kb = b.shape[1]
    assert ka == kb
    D = w_out.shape[1]
    n_j = D // tn
    return pl.pallas_call(
        _outproj_kernel,
        grid=(T // tm, n_j),
        in_specs=[
            pl.BlockSpec((tm, ka), lambda i, j: (i, 0)),
            pl.BlockSpec((tm, kb), lambda i, j: (i, 0)),
            pl.BlockSpec((ka, tn), _resident_w_map(n_j)),
            pl.BlockSpec((kb, tn), lambda i, j: (1, jnp.where(i == 0, j, n_j - 1))),
            pl.BlockSpec((tm, tn), lambda i, j: (i, j)),
        ],
        out_specs=pl.BlockSpec((tm, tn), lambda i, j: (i, j)),
        out_shape=jax.ShapeDtypeStruct((T, D), F32),
        scratch_shapes=[pltpu.VMEM((n_j, ka + kb, tn), BF16)],
        compiler_params=_cparams(("arbitrary", "arbitrary"), 48),
        name="outproj",
    )(a, b, w_out, w_out, x2)


def _normproj_kernel(x_ref, g_ref, w_ref, hg_ref, o_ref, n_scr, *w_scr, n_norm):
    j = pl.program_id(1)

    @pl.when(j == 0)
    def _():
        n_scr[...] = _rms(x_ref[...], g_ref[...]).astype(BF16)

    if w_scr:
        @pl.when(pl.program_id(0) == 0)
        def _():
            w_scr[0][j] = w_ref[...].astype(BF16)
        y = _dot(n_scr[...], w_scr[0][j])
    else:
        y = _dot(n_scr[...], w_ref[...].astype(BF16))

    @pl.when(j < n_norm)
    def _():
        o_ref[...] = _rms(y, hg_ref[...]).astype(BF16)

    @pl.when(j >= n_norm)
    def _():
        o_ref[...] = y.astype(BF16)


def _normproj(x2, g, w, head_g, *, tm, tn, n_norm, name):
    T, D = x2.shape
    N = w.shape[1]
    kern = functools.partial(_normproj_kernel, n_norm=n_norm)
    n_j = N // tn
    resident = T // tm > 1
    return pl.pallas_call(
        kern,
        grid=(T // tm, n_j),
        in_specs=[
            pl.BlockSpec((tm, D), lambda i, j: (i, 0)),
            pl.BlockSpec((1, D), lambda i, j: (0, 0)),
            pl.BlockSpec((D, tn), _resident_w_map(n_j) if resident else (lambda i, j: (0, j))),
            pl.BlockSpec((1, tn), lambda i, j: (0, 0)),
        ],
        out_specs=pl.BlockSpec((tm, tn), lambda i, j: (i, j)),
        out_shape=jax.ShapeDtypeStruct((T, N), BF16),
        scratch_shapes=[pltpu.VMEM((tm, D), BF16)] + ([pltpu.VMEM((n_j, D, tn), BF16)] if resident else []),
        compiler_params=_cparams(("arbitrary", "arbitrary"), 48),
        name=name,
    )(x2, g, w, head_g)


def _cross_kernel(q_ref, k_ref, v_ref, w_ref, h_ref, o_ref, att_scr, w_scr, *, hdim):
    j = pl.program_id(1)

    @pl.when(pl.program_id(0) == 0)
    def _():
        w_scr[j] = w_ref[...].astype(BF16)

    @pl.when(j == 0)
    def _():
        for hd in range(CROSS_HEADS):
            cols = slice(hd * hdim, (hd + 1) * hdim)
            s = lax.dot_general(q_ref[:, cols], k_ref[:, cols], NT_DIMS, preferred_element_type=F32)
            p = jnp.exp(s - jnp.max(s, axis=-1, keepdims=True))
            l = jnp.sum(p, axis=-1, keepdims=True)
            att_scr[:, cols] = (_dot(p.astype(BF16), v_ref[:, cols]) / l).astype(BF16)

    o_ref[...] = h_ref[...] + _dot(att_scr[...], w_scr[j])


def _cross(qc, kv, w_co, h1, *, S, n_mem, tm, tn):
    T, D = qc.shape
    per_b = S // tm
    kern = functools.partial(_cross_kernel, hdim=D // CROSS_HEADS)
    n_j = D // tn
    return pl.pallas_call(
        kern,
        grid=(T // tm, n_j),
        in_specs=[
            pl.BlockSpec((tm, D), lambda i, j: (i, 0)),
            pl.BlockSpec((n_mem, D), lambda i, j: (i // per_b, 0)),
            pl.BlockSpec((n_mem, D), lambda i, j: (i // per_b, 1)),
            pl.BlockSpec((D, tn), _resident_w_map(n_j)),
            pl.BlockSpec((tm, tn), lambda i, j: (i, j)),
        ],
        out_specs=pl.BlockSpec((tm, tn), lambda i, j: (i, j)),
        out_shape=jax.ShapeDtypeStruct((T, D), F32),
        scratch_shapes=[pltpu.VMEM((tm, D), BF16), pltpu.VMEM((n_j, D, tn), BF16)],
        compiler_params=_cparams(("arbitrary", "arbitrary"), 48),
        name="cross",
    )(qc, kv, kv, w_co, h1)


R_ROWS = SUBLANES + N_EXPERTS


def _router_kernel(h_ref, g_ref, wt_ref, b_ref, eid_ref, gate_ref):
    n = _rms(h_ref[...], g_ref[...])
    nh, nl = _split_bf16(n)
    wh, wl = _split_bf16(wt_ref[...])
    nt = functools.partial(lax.dot_general, dimension_numbers=NT_DIMS, preferred_element_type=F32)
    lg = nt(wh, nh) + nt(wh, nl) + nt(wl, nh) + b_ref[...]

    tm = lg.shape[1]
    row = lax.broadcasted_iota(I32, (SUBLANES, tm), 0)

    def first_argmax(v, vmax):
        return jnp.min(jnp.where(v == vmax, row, SUBLANES), axis=0, keepdims=True)

    gl = jnp.where(row < N_GROUPS, lg[0:SUBLANES], NEG_INF)
    gmax = jnp.max(gl, axis=0, keepdims=True)
    grp = first_argmax(gl, gmax)
    grp_w = 1.0 / jnp.sum(jnp.exp(gl - gmax), axis=0, keepdims=True)

    sel = jnp.zeros((SUBLANES, tm), F32)
    for gi in range(N_GROUPS):
        lo = SUBLANES + gi * EXPERTS_PER_GROUP
        sel = jnp.where(grp == gi, lg[lo:lo + EXPERTS_PER_GROUP], sel)
    e = jnp.exp(sel - jnp.max(sel, axis=0, keepdims=True))
    prob = e / jnp.sum(e, axis=0, keepdims=True)
    p1 = jnp.max(prob, axis=0, keepdims=True)
    i1 = first_argmax(prob, p1)
    rest = jnp.where(row == i1, -1.0, prob)
    p2 = jnp.max(rest, axis=0, keepdims=True)
    i2 = first_argmax(rest, p2)
    den = p1 + p2
    base = grp * EXPERTS_PER_GROUP
    eid_ref[...] = jnp.where(row == 0, base + i1, jnp.where(row == 1, base + i2, 0))
    gate_ref[...] = jnp.where(row == 0, grp_w * p1 / den, jnp.where(row == 1, grp_w * p2 / den, 0.0))


def _router(h2, g_ffn, w_rt, b_r, *, tm):
    T, D = h2.shape
    return pl.pallas_call(
        _router_kernel,
        grid=(T // tm,),
        in_specs=[
            pl.BlockSpec((tm, D), lambda i: (i, 0)),
            pl.BlockSpec((1, D), lambda i: (0, 0)),
            pl.BlockSpec((R_ROWS, D), lambda i: (0, 0)),
            pl.BlockSpec((R_ROWS, 1), lambda i: (0, 0)),
        ],
        out_specs=[
            pl.BlockSpec((SUBLANES, tm), lambda i: (0, i)),
            pl.BlockSpec((SUBLANES, tm), lambda i: (0, i)),
        ],
        out_shape=[
            jax.ShapeDtypeStruct((SUBLANES, T), I32),
            jax.ShapeDtypeStruct((SUBLANES, T), F32),
        ],
        compiler_params=_cparams(("parallel",), 32),
        name="router",
    )(h2, g_ffn, w_rt, b_r)


def _rank_kernel(eid_ref, rank_ref, cnt_ref, carry):
    @pl.when(pl.program_id(0) == 0)
    def _():
        carry[...] = jnp.zeros(carry.shape, F32)

    tm = eid_ref.shape[1]
    e0 = eid_ref[0:1, :]
    e1 = eid_ref[1:2, :]
    erow = lax.broadcasted_iota(I32, (N_EXPERTS, tm), 0)
    hit = jnp.where((erow == e0) | (erow == e1), 1.0, 0.0)
    r = lax.broadcasted_iota(I32, (tm, tm), 0)
    c = lax.broadcasted_iota(I32, (tm, tm), 1)
    before = jnp.where(r < c, 1.0, 0.0).astype(BF16)
    pre = _dot(hit.astype(BF16), before) + carry[:, 0:1]
    rank0 = jnp.sum(jnp.where(erow == e0, pre, 0.0), axis=0, keepdims=True)
    rank1 = jnp.sum(jnp.where(erow == e1, pre, 0.0), axis=0, keepdims=True)
    row = lax.broadcasted_iota(I32, (SUBLANES, tm), 0)
    rank_ref[...] = jnp.where(row == 0, rank0, jnp.where(row == 1, rank1, 0.0)).astype(I32)
    total = carry[...] + jnp.sum(hit, axis=1, keepdims=True)
    carry[...] = total
    cnt_ref[...] = total.astype(I32)


def _rank(eid, *, tm):
    T = eid.shape[1]
    return pl.pallas_call(
        _rank_kernel,
        grid=(T // tm,),
        in_specs=[pl.BlockSpec((SUBLANES, tm), lambda i: (0, i))],
        out_specs=[
            pl.BlockSpec((SUBLANES, tm), lambda i: (0, i)),
            pl.BlockSpec((N_EXPERTS, LANES), lambda i: (0, 0)),
        ],
        out_shape=[
            jax.ShapeDtypeStruct((SUBLANES, T), I32),
            jax.ShapeDtypeStruct((N_EXPERTS, LANES), I32),
        ],
        scratch_shapes=[pltpu.VMEM((N_EXPERTS, LANES), F32)],
        compiler_params=_cparams(("arbitrary",), 32),
        name="rank",
    )(eid)


def _expert_kernel(be_ref, first_ref, nxt_ref, nb_ref, tok0_ref, tok_ref, h_hbm, g_ref,
                   wg_hbm, wu_hbm, wd_hbm, y_ref,
                   xbuf0, xbuf1, xsem, wg_f, wu_f, wd_f, wsem, wg_b, wu_b, wd_b, *, tb, n_blocks):
    i = pl.program_id(0)
    nb = nb_ref[0]
    ic = jnp.minimum(i, n_blocks - 1)
    xbufs = (xbuf0, xbuf1)

    def row_copy(tok, r, p):
        return pltpu.make_async_copy(h_hbm.at[pl.ds(tok, 1), :], xbufs[p].at[pl.ds(r, 1), :], xsem.at[p])

    def weight_copies(e):
        return (pltpu.make_async_copy(wg_hbm.at[e], wg_f, wsem.at[0]),
                pltpu.make_async_copy(wu_hbm.at[e], wu_f, wsem.at[1]),
                pltpu.make_async_copy(wd_hbm.at[e], wd_f, wsem.at[2]))

    @pl.when(i == 0)
    def _():
        for cp in weight_copies(be_ref[0]):
            cp.start()

        def body(r, carry):
            row_copy(tok0_ref[0, 0, r], r, 0).start()
            return carry
        lax.fori_loop(0, tb, body, 0)

    first = (i < nb) & (first_ref[ic] == 1)

    @pl.when(first)
    def _():
        for cp in weight_copies(0):
            cp.wait()
        wg_b[...] = wg_f[...].astype(BF16)
        wu_b[...] = wu_f[...].astype(BF16)
        wd_b[...] = wd_f[...].astype(BF16)

    @pl.when(first & (nxt_ref[ic] >= 0))
    def _():
        for cp in weight_copies(jnp.maximum(nxt_ref[ic], 0)):
            cp.start(priority=1)

    for p in range(2):
        cur = xbufs[p]

        @pl.when((i <= nb) & (i % 2 == p))
        def _():
            pltpu.make_async_copy(h_hbm.at[pl.ds(0, tb), :], cur, xsem.at[p]).wait()

        @pl.when((i < nb) & (i % 2 == p))
        def _():
            for r in range(tb):
                row_copy(tok_ref[0, 0, r], r, 1 - p).start()
            n = _rms(cur[...], g_ref[...]).astype(BF16)
            a = _dot(n, wg_b[...])
            hdn = (a / (1.0 + jnp.exp(-a))) * _dot(n, wu_b[...])
            y_ref[...] = _dot(hdn.astype(BF16), wd_b[...])

    @pl.when((i >= nb) & (i < n_blocks))
    def _():
        y_ref[...] = jnp.zeros(y_ref.shape, F32)


def _experts(buf_tok, blk_e, blk_first, blk_next, nb, h2, g_ffn, w_gate, w_up, w_down, *, tb, n_blocks):
    T, D = h2.shape
    De = w_gate.shape[2]
    kern = functools.partial(_expert_kernel, tb=tb, n_blocks=n_blocks)

    tok3 = buf_tok.reshape(n_blocks + 1, 1, tb)

    grid_spec = pltpu.PrefetchScalarGridSpec(
        num_scalar_prefetch=4,
        grid=(n_blocks + 1,),
        in_specs=[
            pl.BlockSpec((1, 1, tb), lambda i, *_: (0, 0, 0), memory_space=pltpu.SMEM),
            pl.BlockSpec((1, 1, tb), lambda i, *_: (jnp.minimum(i + 1, n_blocks), 0, 0),
                         memory_space=pltpu.SMEM),
            pl.BlockSpec(memory_space=pl.ANY),
            pl.BlockSpec((1, D), lambda i, *_: (0, 0)),
            pl.BlockSpec(memory_space=pl.ANY),
            pl.BlockSpec(memory_space=pl.ANY),
            pl.BlockSpec(memory_space=pl.ANY),
        ],
        out_specs=pl.BlockSpec((tb, D), lambda i, *_: (jnp.minimum(i, n_blocks - 1), 0)),
        scratch_shapes=[
            pltpu.VMEM((tb, D), F32),
            pltpu.VMEM((tb, D), F32),
            pltpu.SemaphoreType.DMA((2,)),
            pltpu.VMEM((D, De), F32),
            pltpu.VMEM((D, De), F32),
            pltpu.VMEM((De, D), F32),
            pltpu.SemaphoreType.DMA((3,)),
            pltpu.VMEM((D, De), BF16),
            pltpu.VMEM((D, De), BF16),
            pltpu.VMEM((De, D), BF16),
        ],
    )
    return pl.pallas_call(
        kern,
        grid_spec=grid_spec,
        out_shape=jax.ShapeDtypeStruct((n_blocks * tb, D), F32),
        compiler_params=_cparams(("arbitrary",), 48),
        name="experts",
    )(blk_e, blk_first, blk_next, nb, tok3, tok3, h2, g_ffn, w_gate, w_up, w_down)


def _combine_kernel(dest_ref, y_hbm, h_ref, gate_ref, o_ref, ybuf, sem, *, tm, n_tok):
    i = pl.program_id(0)
    n = pl.num_programs(0)

    def row_copy(d, k, r, slot):
        return pltpu.make_async_copy(y_hbm.at[pl.ds(d, 1), :], ybuf.at[slot, k, pl.ds(r, 1), :], sem.at[slot])

    def start_gather(blk, slot):
        def body(r, carry):
            for k in range(TOP_K):
                row_copy(dest_ref[k * n_tok + blk * tm + r], k, r, slot).start()
            return carry
        lax.fori_loop(0, tm, body, 0, unroll=8)

    def wait_gather(slot):
        for k in range(TOP_K):
            pltpu.make_async_copy(y_hbm.at[pl.ds(0, tm), :], ybuf.at[slot, k], sem.at[slot]).wait()

    @pl.when(i == 0)
    def _():
        start_gather(0, 0)

    @pl.when(i + 1 < n)
    def _():
        start_gather(i + 1, (i + 1) % 2)

    slot = i % 2
    wait_gather(slot)
    gt = gate_ref[...]
    o_ref[...] = h_ref[...] + gt[:, 0:1] * ybuf[slot, 0] + gt[:, 1:2] * ybuf[slot, 1]


def _combine(dest, yb, h2, gate, *, tm):
    T, D = h2.shape
    kern = functools.partial(_combine_kernel, tm=tm, n_tok=T)
    grid_spec = pltpu.PrefetchScalarGridSpec(
        num_scalar_prefetch=1,
        grid=(T // tm,),
        in_specs=[
            pl.BlockSpec(memory_space=pl.ANY),
            pl.BlockSpec((tm, D), lambda i, d: (i, 0)),
            pl.BlockSpec((tm, TOP_K), lambda i, d: (i, 0)),
        ],
        out_specs=pl.BlockSpec((tm, D), lambda i, d: (i, 0)),
        scratch_shapes=[
            pltpu.VMEM((2, TOP_K, tm, D), F32),
            pltpu.SemaphoreType.DMA((2,)),
        ],
    )
    return pl.pallas_call(
        kern,
        grid_spec=grid_spec,
        out_shape=jax.ShapeDtypeStruct((T, D), F32),
        compiler_params=_cparams(("arbitrary",), 40),
        name="combine",
    )(dest, yb, h2, gate)


EXPERT_ROWS = 256


def kernel(x, mem, positions, g_attn, w_in, q_norm_g, k_norm_g, lambda_q1, lambda_k1, lambda_q2, lambda_k2, diff_subln_g, gla_w_a2, gla_b_a, gla_out_g, w_out, g_cross, g_mem, w_cq, w_ckv, cq_norm_g, ck_norm_g, w_co, g_ffn, w_router_grp, b_router_grp, w_router_exp, b_router_exp, w_gate, w_up, w_down):
    B, S, D = x.shape
    T = B * S
    n_mem = mem.shape[1]
    l = 0
    x2 = x.reshape(T, D)

    half = DIFF_QKDIM // 2
    freq = ROPE_THETA ** (-jnp.arange(half, dtype=F32) / half)
    freq = jnp.tile(freq, LANES // half)[None, :]
    q_scale = math.log2(math.e) * DIFF_QKDIM ** -0.5
    qkg = jnp.stack([jnp.tile(q_norm_g[l], 2) * q_scale, jnp.tile(k_norm_g[l], 2)])
    score_bound = 1.01 * DIFF_QKDIM * q_scale * jnp.max(jnp.abs(q_norm_g[l])) * jnp.max(jnp.abs(k_norm_g[l]))
    lvec = jnp.stack([lambda_q1[l], lambda_k1[l], lambda_q2[l], lambda_k2[l]])

    qk, mid, log_a, sgr = _inproj(x2, g_attn[l][None], positions.reshape(T, 1), freq, qkg, w_in[l].T,
                                  gla_w_a2[l], gla_b_a[l][None], tm=1024)
    diffattn = functools.partial(_diffattn, lvec, qk, mid, diff_subln_g[l][None], B=B, S=S, tq=512)
    mix_d = lax.cond(score_bound <= SCORE_BOUND,
                     functools.partial(diffattn, bounded=True), functools.partial(diffattn, bounded=False))
    mix_g = _gla(mid, log_a, sgr, gla_out_g[l][None], B=B, S=S, blk=512)
    h1 = _outproj(mix_d, mix_g, w_out[l], x2, tm=1024, tn=512)

    hdim = D // CROSS_HEADS
    qc = _normproj(h1, g_cross[l][None], w_cq[l], cq_norm_g[l][None] * (hdim ** -0.5),
                   tm=1024, tn=hdim, n_norm=CROSS_HEADS, name="cq")
    kv = _normproj(mem.reshape(B * n_mem, D), g_mem[l][None], w_ckv[l], ck_norm_g[l][None],
                   tm=B * n_mem, tn=hdim, n_norm=CROSS_HEADS, name="ckv")
    h2 = _cross(qc, kv, w_co[l], h1, S=S, n_mem=n_mem, tm=512, tn=512)

    w_rt = jnp.concatenate([w_router_grp[l].T, jnp.zeros((SUBLANES - N_GROUPS, D), F32), w_router_exp[l].T])
    b_r = jnp.concatenate([b_router_grp[l], jnp.zeros((SUBLANES - N_GROUPS,), F32), b_router_exp[l]])[:, None]
    eid, gate = _router(h2, g_ffn[l][None], w_rt, b_r, tm=512)
    rank, cnt = _rank(eid, tm=512)

    tb = EXPERT_ROWS
    n_blocks = (T * TOP_K + N_EXPERTS * (tb - 1) + tb - 1) // tb
    counts = cnt[:, 0]
    pcounts = ((counts + tb - 1) // tb) * tb
    pends = jnp.cumsum(pcounts)
    pstarts = pends - pcounts
    nb = (pends[-1:] // tb).astype(I32)
    eids = jnp.arange(N_EXPERTS, dtype=I32)
    blk_start = jnp.arange(n_blocks, dtype=I32) * tb
    blk_e = jnp.minimum(jnp.sum(pends[None, :] <= blk_start[:, None], axis=1), N_EXPERTS - 1).astype(I32)
    blk_first = jnp.concatenate([jnp.ones((1,), I32), (blk_e[1:] != blk_e[:-1]).astype(I32)])
    used = jnp.where(counts > 0, eids, N_EXPERTS)
    next_used = jnp.concatenate([lax.cummin(used[::-1])[::-1][1:], jnp.full((1,), N_EXPERTS, I32)])
    next_used = jnp.where(next_used < N_EXPERTS, next_used, -1)
    blk_next = jnp.sum(jnp.where(blk_e[:, None] == eids, next_used, 0), axis=1).astype(I32)
    pick = eid[:TOP_K]
    pstart_of = jnp.sum(jnp.where(pick[..., None] == eids, pstarts, 0), axis=-1)
    dest = (pstart_of + rank[:TOP_K]).astype(I32).reshape(-1)
    tok = jnp.tile(jnp.arange(T, dtype=I32), TOP_K)
    buf_tok = jnp.zeros(((n_blocks + 1) * tb,), I32).at[dest].set(tok)

    yb = _experts(buf_tok, blk_e, blk_first, blk_next, nb, h2, g_ffn[l][None], w_gate[l], w_up[l], w_down[l],
                  tb=tb, n_blocks=n_blocks)
    out = _combine(dest, yb, h2, gate[:TOP_K].T, tm=256)
    return out.reshape(B, S, D)
```

```python
import functools
import math

import jax
import jax.numpy as jnp
from jax import lax
from jax.experimental import pallas as pl
from jax.experimental.pallas import tpu as pltpu

F32 = jnp.float32
BF16 = jnp.bfloat16
I32 = jnp.int32

LANES = 128
SUBLANES = 8

CHUNK = 64
ROPE_THETA = 10000.0
NORM_EPS = 1e-6
NEG_INF = -1e30
DIFF_HEADS = 8
DIFF_VDIM = 128
DIFF_QKDIM = 64
GLA_HEADS = 4
GLA_VDIM = 256
GLA_KDIM = 128
GLA_GATE_RANK = 16
GLA_TAU = 16.0
CROSS_HEADS = 4
N_GROUPS = 4
EXPERTS_PER_GROUP = 8
N_EXPERTS = N_GROUPS * EXPERTS_PER_GROUP
TOP_K = 2
LAM_INIT = 0.8 - 0.6 * math.exp(-0.3 * 0)

NT_DIMS = (((1,), (1,)), ((), ()))


def _cparams(semantics, vmem_mib):
    return pltpu.CompilerParams(dimension_semantics=semantics,
                                vmem_limit_bytes=vmem_mib * 1024 * 1024)


def _dot(a, b):
    return jnp.dot(a, b, preferred_element_type=F32)


def _dot_nt(a, b):
    return lax.dot_general(a, b, NT_DIMS, preferred_element_type=F32)


def _rms(x, g):
    ms = jnp.mean(x * x, axis=-1, keepdims=True)
    return x * lax.rsqrt(ms + NORM_EPS) * g


def _split_bf16(x):
    hi = x.astype(BF16)
    lo = (x - hi.astype(F32)).astype(BF16)
    return hi, lo


TN = 512
J_QK = 4
J_MID = 6
J_LR = J_QK + J_MID
J_GR = J_LR + 1
N_J = J_GR + 2


def _inproj_kernel(x_ref, g_ref, pos_ref, freq_ref, qkg_ref, w_ref, wgr_ref, wlr_ref, wa2_ref, ba_ref,
                   qk_ref, mid_ref, loga_ref, sgr_ref, n_scr, cos_scr, sin_scr, y_scr):
    j = pl.program_id(1)

    @pl.when(j == 0)
    def _():
        n_scr[...] = _rms(x_ref[...], g_ref[...]).astype(BF16)
        ang = pos_ref[...].astype(F32) * freq_ref[...]
        cos_scr[...] = jnp.cos(ang)
        sin_scr[...] = jnp.sin(ang)

    def qk_epilogue(jq):
        y_prev = y_scr.at[jq % 2]
        lane = lax.broadcasted_iota(I32, (1, LANES), 1)
        low_seg = lane < DIFF_QKDIM
        first_half = (lane % DIFF_QKDIM) < (DIFF_QKDIM // 2)
        gain = qkg_ref[jq // (J_QK // 2):jq // (J_QK // 2) + 1, :]
        cos = cos_scr[...]
        sin = sin_scr[...]
        for c in range(TN // LANES):
            yb = y_prev[:, c * LANES:(c + 1) * LANES]
            y2 = yb * yb
            s_lo = jnp.sum(jnp.where(low_seg, y2, 0.0), axis=-1, keepdims=True)
            s_hi = jnp.sum(jnp.where(low_seg, 0.0, y2), axis=-1, keepdims=True)
            ms = jnp.where(low_seg, s_lo, s_hi) * (1.0 / DIFF_QKDIM)
            yn = yb * lax.rsqrt(ms + NORM_EPS) * gain
            rot = jnp.where(first_half,
                            -pltpu.roll(yn, LANES - DIFF_QKDIM // 2, 1),
                            pltpu.roll(yn, DIFF_QKDIM // 2, 1))
            qk_ref[:, c * LANES:(c + 1) * LANES] = (yn * cos + rot * sin).astype(BF16)

    for jq in range(J_QK + 1):
        @pl.when(j == jq)
        def _():
            y = _dot_nt(n_scr[...], w_ref[...].astype(BF16))
            if jq > 0:
                qk_epilogue(jq - 1)
            if jq < J_QK:
                y_scr[jq % 2] = y
            else:
                mid_ref[...] = y.astype(BF16)

    @pl.when((j > J_QK) & (j < J_LR))
    def _():
        mid_ref[...] = _dot_nt(n_scr[...], w_ref[...].astype(BF16)).astype(BF16)

    @pl.when(j == J_LR)
    def _():
        lr = _dot_nt(n_scr[...], wlr_ref[...].astype(BF16))
        z = _dot(lr.astype(BF16), wa2_ref[...].astype(BF16)) + ba_ref[...]
        log_sig = jnp.minimum(z, 0.0) - jnp.log(1.0 + jnp.exp(-jnp.abs(z)))
        loga_ref[...] = log_sig * (1.0 / GLA_TAU)

    @pl.when(j >= J_GR)
    def _():
        y = _dot_nt(n_scr[...], wgr_ref[...].astype(BF16))
        sgr_ref[...] = (y / (1.0 + jnp.exp(-y))).astype(BF16)


def _inproj(x2, g_attn, pos2, freq, qkg, w_t, w_a2, b_a, *, tm):
    T, D = x2.shape
    n_mid = J_MID * TN
    n_gk = GLA_HEADS * GLA_KDIM
    lr0 = J_LR * TN
    gr0 = lr0 + GLA_GATE_RANK
    n_gr = w_t.shape[0] - gr0
    assert n_gr == 2 * TN and lr0 % GLA_GATE_RANK == 0
    return pl.pallas_call(
        _inproj_kernel,
        grid=(T // tm, N_J),
        in_specs=[
            pl.BlockSpec((tm, D), lambda i, j: (i, 0)),
            pl.BlockSpec((1, D), lambda i, j: (0, 0)),
            pl.BlockSpec((tm, 1), lambda i, j: (i, 0)),
            pl.BlockSpec((1, LANES), lambda i, j: (0, 0)),
            pl.BlockSpec((2, LANES), lambda i, j: (0, 0)),
            pl.BlockSpec((TN, D), lambda i, j: (jnp.minimum(j, J_LR - 1), 0)),
            pl.BlockSpec((pl.Element(TN), pl.Element(D)),
                         lambda i, j: (pl.multiple_of(gr0 + TN * jnp.clip(j - J_GR, 0, 1), SUBLANES), 0)),
            pl.BlockSpec((GLA_GATE_RANK, D), lambda i, j: (lr0 // GLA_GATE_RANK, 0)),
            pl.BlockSpec((GLA_GATE_RANK, n_gk), lambda i, j: (0, 0)),
            pl.BlockSpec((1, n_gk), lambda i, j: (0, 0)),
        ],
        out_specs=[
            pl.BlockSpec((tm, TN), lambda i, j: (i, jnp.clip(j - 1, 0, J_QK - 1))),
            pl.BlockSpec((tm, TN), lambda i, j: (i, jnp.clip(j - J_QK, 0, J_MID - 1))),
            pl.BlockSpec((tm, n_gk), lambda i, j: (i, 0)),
            pl.BlockSpec((tm, TN), lambda i, j: (i, jnp.clip(j - J_GR, 0, 1))),
        ],
        out_shape=[
            jax.ShapeDtypeStruct((T, J_QK * TN), BF16),
            jax.ShapeDtypeStruct((T, n_mid), BF16),
            jax.ShapeDtypeStruct((T, n_gk), F32),
            jax.ShapeDtypeStruct((T, n_gr), BF16),
        ],
        scratch_shapes=[
            pltpu.VMEM((tm, D), BF16),
            pltpu.VMEM((tm, LANES), F32),
            pltpu.VMEM((tm, LANES), F32),
            pltpu.VMEM((2, tm, TN), F32),
        ],
        compiler_params=_cparams(("parallel", "arbitrary"), 56),
        name="inproj",
    )(x2, g_attn, pos2, freq, qkg, w_t, w_t, w_t, w_a2, b_a)


SCORE_BOUND = 80.0


def _diffattn_kernel(lv_ref, q_ref, k_ref, v_ref, sg_ref, o_ref, vext, diag_mask, acc1, acc2, m1, m2,
                     *, tq, bounded):
    i = pl.program_id(2)
    lv = lv_ref[...]
    lam = (jnp.exp(jnp.sum(lv[0:1] * lv[1:2], axis=-1, keepdims=True))
           - jnp.exp(jnp.sum(lv[2:3] * lv[3:4], axis=-1, keepdims=True)) + LAM_INIT)

    @pl.when(i == 0)
    def _():
        S = v_ref.shape[0]
        lane_s = lax.broadcasted_iota(I32, (S, LANES), 1)
        vext[:, 0:DIFF_VDIM] = v_ref[...]
        vext[:, DIFF_VDIM:] = jnp.where(lane_s == 0, 1.0, 0.0).astype(BF16)
        row_chunk = lax.broadcasted_iota(I32, (tq, tq), 0) // CHUNK
        col_chunk = lax.broadcasted_iota(I32, (tq, tq), 1) // CHUNK
        diag_mask[...] = jnp.where(col_chunk <= row_chunk, 1.0, 0.0).astype(BF16)

    q = q_ref[...]
    lane = lax.broadcasted_iota(I32, (1, LANES), 1)
    zero = jnp.zeros_like(q)
    q1 = jnp.where(lane < DIFF_QKDIM, q, zero)
    q2 = jnp.where(lane < DIFF_QKDIM, zero, q)

    acc1[...] = jnp.zeros(acc1.shape, F32)
    acc2[...] = jnp.zeros(acc2.shape, F32)
    if not bounded:
        m1[...] = jnp.full(m1.shape, NEG_INF, F32)
        m2[...] = jnp.full(m2.shape, NEG_INF, F32)

    def step(j, masked):
        start = pl.multiple_of(j * tq, tq)
        k = k_ref[pl.ds(start, tq), :]
        v = vext[pl.ds(start, tq), :]
        for qc, acc, m in ((q1, acc1, m1), (q2, acc2, m2)):
            s = lax.dot_general(qc, k, NT_DIMS, preferred_element_type=F32)
            if bounded:
                p = jnp.exp2(s).astype(BF16)
                if masked:
                    p = p * diag_mask[...]
                acc[...] += _dot(p, v)
            else:
                if masked:
                    s = jnp.where(diag_mask[...] > 0, s, NEG_INF)
                m_old = m[...]
                m_new = jnp.maximum(m_old, jnp.max(s, axis=-1, keepdims=True))
                p = jnp.exp2(s - m_new)
                acc[...] = jnp.exp2(m_old - m_new) * acc[...] + _dot(p.astype(BF16), v)
                m[...] = m_new

    def body(jj, carry):
        step(2 * jj, False)
        step(2 * jj + 1, False)
        return carry

    lax.fori_loop(0, i // 2, body, 0)

    @pl.when(i % 2 == 1)
    def _():
        step(i - 1, False)
        step(i, True)

    @pl.when(i % 2 == 0)
    def _():
        step(i, True)

    a1 = acc1[...]
    a2 = acc2[...]
    o = (a1[:, :DIFF_VDIM] / a1[:, DIFF_VDIM:DIFF_VDIM + 1]
         - lam * (a2[:, :DIFF_VDIM] / a2[:, DIFF_VDIM:DIFF_VDIM + 1]))
    o_ref[...] = (_rms(o, sg_ref[...]) * (1.0 - LAM_INIT)).astype(BF16)


def _diffattn(lvec, qk, mid, subln_g, *, B, S, tq, bounded):
    T = B * S
    nq = S // tq
    kern = functools.partial(_diffattn_kernel, tq=tq, bounded=bounded)
    return pl.pallas_call(
        kern,
        grid=(B, DIFF_HEADS, nq),
        in_specs=[
            pl.BlockSpec((4, DIFF_QKDIM), lambda b, h, i: (0, 0)),
            pl.BlockSpec((tq, LANES), lambda b, h, i: (b * nq + i, h)),
            pl.BlockSpec((S, LANES), lambda b, h, i: (b, DIFF_HEADS + h)),
            pl.BlockSpec((S, LANES), lambda b, h, i: (b, h)),
            pl.BlockSpec((1, DIFF_VDIM), lambda b, h, i: (0, 0)),
        ],
        out_specs=pl.BlockSpec((tq, DIFF_VDIM), lambda b, h, i: (b * nq + i, h)),
        out_shape=jax.ShapeDtypeStruct((T, DIFF_HEADS * DIFF_VDIM), BF16),
        scratch_shapes=[
            pltpu.VMEM((S, 2 * DIFF_VDIM), BF16),
            pltpu.VMEM((tq, tq), BF16),
            pltpu.VMEM((tq, 2 * DIFF_VDIM), F32),
            pltpu.VMEM((tq, 2 * DIFF_VDIM), F32),
            pltpu.VMEM((tq, 1), F32),
            pltpu.VMEM((tq, 1), F32),
        ],
        compiler_params=_cparams(("parallel", "parallel", "arbitrary"), 32),
        name="diffattn_bounded" if bounded else "diffattn_online",
    )(lvec, qk, qk, mid, subln_g)


def _gla_kernel(q_ref, k_ref, v_ref, la_ref, sgr_ref, g_ref, o_ref, state, *, blk):
    @pl.when(pl.program_id(2) == 0)
    def _():
        state[...] = jnp.zeros(state.shape, F32)

    la_t = la_ref[...].T
    k_t = k_ref[...].astype(F32).T
    r = lax.broadcasted_iota(I32, (blk, blk), 0)
    c = lax.broadcasted_iota(I32, (blk, blk), 1)
    same = (r // CHUNK) == (c // CHUNK)
    tri = jnp.where(same & (r <= c), 1.0, 0.0).astype(BF16)
    ones = jnp.where(same, 1.0, 0.0).astype(BF16)
    hi, lo = _split_bf16(la_t)
    cum_t = _dot(hi, tri) + _dot(lo, tri)
    tot_t = _dot(hi, ones) + _dot(lo, ones)
    kd_t = k_t * jnp.exp(tot_t - cum_t)

    lane = lax.broadcasted_iota(I32, (1, LANES), 1)
    scale = GLA_KDIM ** -0.5
    for ck in range(blk // CHUNK):
        pair = slice((ck // 2) * LANES, (ck // 2 + 1) * LANES)
        rows = slice(ck * CHUNK, (ck + 1) * CHUNK)
        in_chunk = (lane // CHUNK) == (ck % 2)
        kd = jnp.where(in_chunk, kd_t[:, pair], 0.0).astype(BF16)
        d_state = _dot(kd, v_ref[pair, :])
        decay = jnp.exp(tot_t[:, ck * CHUNK:ck * CHUNK + 1])
        st = decay * state[...] + d_state
        state[...] = st
        o = _dot(q_ref[rows, :], st.astype(BF16)) * scale
        gate = sgr_ref[rows, :].astype(F32)
        o_ref[rows, :] = (_rms(o, g_ref[...]) * gate).astype(BF16)


def _gla(mid, log_a, sgr, out_g, *, B, S, blk):
    T = B * S
    ns = S // blk
    kern = functools.partial(_gla_kernel, blk=blk)
    q_col0 = (DIFF_HEADS * DIFF_VDIM) // GLA_KDIM
    k_col0 = q_col0 + GLA_HEADS
    v_col0 = (DIFF_HEADS * DIFF_VDIM + 2 * GLA_HEADS * GLA_KDIM) // GLA_VDIM
    return pl.pallas_call(
        kern,
        grid=(B, GLA_HEADS, ns),
        in_specs=[
            pl.BlockSpec((blk, GLA_KDIM), lambda b, h, s: (b * ns + s, q_col0 + h)),
            pl.BlockSpec((blk, GLA_KDIM), lambda b, h, s: (b * ns + s, k_col0 + h)),
            pl.BlockSpec((blk, GLA_VDIM), lambda b, h, s: (b * ns + s, v_col0 + h)),
            pl.BlockSpec((blk, GLA_KDIM), lambda b, h, s: (b * ns + s, h)),
            pl.BlockSpec((blk, GLA_VDIM), lambda b, h, s: (b * ns + s, h)),
            pl.BlockSpec((1, GLA_VDIM), lambda b, h, s: (0, 0)),
        ],
        out_specs=pl.BlockSpec((blk, GLA_VDIM), lambda b, h, s: (b * ns + s, h)),
        out_shape=jax.ShapeDtypeStruct((T, GLA_HEADS * GLA_VDIM), BF16),
        scratch_shapes=[pltpu.VMEM((GLA_KDIM, GLA_VDIM), F32)],
        compiler_params=_cparams(("parallel", "parallel", "arbitrary"), 32),
        name="gla",
    )(mid, mid, mid, log_a, sgr, out_g)


def _resident_w_map(n_j):
    return lambda i, j: (0, jnp.where(i == 0, j, n_j - 1))


def _outproj_kernel(a_ref, b_ref, wa_ref, wb_ref, x_ref, o_ref, w_scr):
    j = pl.program_id(1)
    ka = a_ref.shape[1]

    @pl.when(pl.program_id(0) == 0)
    def _():
        w_scr[j, 0:ka, :] = wa_ref[...].astype(BF16)
        w_scr[j, ka:, :] = wb_ref[...].astype(BF16)

    acc = _dot(a_ref[...], w_scr[j, 0:ka, :]) + _dot(b_ref[...], w_scr[j, ka:, :])
    o_ref[...] = x_ref[...] + acc


def _outproj(a, b, w_out, x2, *, tm, tn):
    T, ka = a.shape
    kb = b.shape[1]
    assert ka == kb
    D = w_out.shape[1]
    n_j = D // tn
    return pl.pallas_call(
        _outproj_kernel,
        grid=(T // tm, n_j),
        in_specs=[
            pl.BlockSpec((tm, ka), lambda i, j: (i, 0)),
            pl.BlockSpec((tm, kb), lambda i, j: (i, 0)),
            pl.BlockSpec((ka, tn), _resident_w_map(n_j)),
            pl.BlockSpec((kb, tn), lambda i, j: (1, jnp.where(i == 0, j, n_j - 1))),
            pl.BlockSpec((tm, tn), lambda i, j: (i, j)),
        ],
        out_specs=pl.BlockSpec((tm, tn), lambda i, j: (i, j)),
        out_shape=jax.ShapeDtypeStruct((T, D), F32),
        scratch_shapes=[pltpu.VMEM((n_j, ka + kb, tn), BF16)],
        compiler_params=_cparams(("arbitrary", "arbitrary"), 48),
        name="outproj",
    )(a, b, w_out, w_out, x2)


def _normproj_kernel(x_ref, g_ref, w_ref, hg_ref, o_ref, n_scr, *w_scr, n_norm):
    j = pl.program_id(1)

    @pl.when(j == 0)
    def _():
        n_scr[...] = _rms(x_ref[...], g_ref[...]).astype(BF16)

    if w_scr:
        @pl.when(pl.program_id(0) == 0)
        def _():
            w_scr[0][j] = w_ref[...].astype(BF16)
        y = _dot(n_scr[...], w_scr[0][j])
    else:
        y = _dot(n_scr[...], w_ref[...].astype(BF16))

    @pl.when(j < n_norm)
    def _():
        o_ref[...] = _rms(y, hg_ref[...]).astype(BF16)

    @pl.when(j >= n_norm)
    def _():
        o_ref[...] = y.astype(BF16)


def _normproj(x2, g, w, head_g, *, tm, tn, n_norm, name):
    T, D = x2.shape
    N = w.shape[1]
    kern = functools.partial(_normproj_kernel, n_norm=n_norm)
    n_j = N // tn
    resident = T // tm > 1
    return pl.pallas_call(
        kern,
        grid=(T // tm, n_j),
        in_specs=[
            pl.BlockSpec((tm, D), lambda i, j: (i, 0)),
            pl.BlockSpec((1, D), lambda i, j: (0, 0)),
            pl.BlockSpec((D, tn), _resident_w_map(n_j) if resident else (lambda i, j: (0, j))),
            pl.BlockSpec((1, tn), lambda i, j: (0, 0)),
        ],
        out_specs=pl.BlockSpec((tm, tn), lambda i, j: (i, j)),
        out_shape=jax.ShapeDtypeStruct((T, N), BF16),
        scratch_shapes=[pltpu.VMEM((tm, D), BF16)] + ([pltpu.VMEM((n_j, D, tn), BF16)] if resident else []),
        compiler_params=_cparams(("arbitrary", "arbitrary"), 48),
        name=name,
    )(x2, g, w, head_g)


def _cross_kernel(q_ref, k_ref, v_ref, w_ref, h_ref, o_ref, att_scr, w_scr, *, hdim):
    j = pl.program_id(1)

    @pl.when(pl.program_id(0) == 0)
    def _():
        w_scr[j] = w_ref[...].astype(BF16)

    @pl.when(j == 0)
    def _():
        for hd in range(CROSS_HEADS):
            cols = slice(hd * hdim, (hd + 1) * hdim)
            s = lax.dot_general(q_ref[:, cols], k_ref[:, cols], NT_DIMS, preferred_element_type=F32)
            p = jnp.exp(s - jnp.max(s, axis=-1, keepdims=True))
            l = jnp.sum(p, axis=-1, keepdims=True)
            att_scr[:, cols] = (_dot(p.astype(BF16), v_ref[:, cols]) / l).astype(BF16)

    o_ref[...] = h_ref[...] + _dot(att_scr[...], w_scr[j])


def _cross(qc, kv, w_co, h1, *, S, n_mem, tm, tn):
    T, D = qc.shape
    per_b = S // tm
    kern = functools.partial(_cross_kernel, hdim=D // CROSS_HEADS)
    n_j = D // tn
    return pl.pallas_call(
        kern,
        grid=(T // tm, n_j),
        in_specs=[
            pl.BlockSpec((tm, D), lambda i, j: (i, 0)),
            pl.BlockSpec((n_mem, D), lambda i, j: (i // per_b, 0)),
            pl.BlockSpec((n_mem, D), lambda i, j: (i // per_b, 1)),
            pl.BlockSpec((D, tn), _resident_w_map(n_j)),
            pl.BlockSpec((tm, tn), lambda i, j: (i, j)),
        ],
        out_specs=pl.BlockSpec((tm, tn), lambda i, j: (i, j)),
        out_shape=jax.ShapeDtypeStruct((T, D), F32),
        scratch_shapes=[pltpu.VMEM((tm, D), BF16), pltpu.VMEM((n_j, D, tn), BF16)],
        compiler_params=_cparams(("arbitrary", "arbitrary"), 48),
        name="cross",
    )(qc, kv, kv, w_co, h1)


R_ROWS = SUBLANES + N_EXPERTS


def _pack_bf16_pairs(xb16):
    c = xb16.shape[1] // 2
    u = lax.bitcast_convert_type(xb16.astype(F32), jnp.uint32)
    return (u[:, :c] >> 16) | (u[:, c:] & jnp.uint32(0xFFFF0000))


def _unpack_bf16_pairs(w):
    lo = lax.bitcast_convert_type(w << 16, F32).astype(BF16)
    hi = lax.bitcast_convert_type(w & jnp.uint32(0xFFFF0000), F32).astype(BF16)
    return lo, hi


def _router_kernel(h_ref, g_ref, wt_ref, b_ref, eid_ref, gate_ref, xn_ref):
    n = _rms(h_ref[...], g_ref[...])
    nh, nl = _split_bf16(n)
    xn_ref[...] = _pack_bf16_pairs(nh)
    wh, wl = _split_bf16(wt_ref[...])
    nt = functools.partial(lax.dot_general, dimension_numbers=NT_DIMS, preferred_element_type=F32)
    lg = nt(wh, nh) + nt(wh, nl) + nt(wl, nh) + b_ref[...]

    tm = lg.shape[1]
    row = lax.broadcasted_iota(I32, (SUBLANES, tm), 0)

    def first_argmax(v, vmax):
        return jnp.min(jnp.where(v == vmax, row, SUBLANES), axis=0, keepdims=True)

    gl = jnp.where(row < N_GROUPS, lg[0:SUBLANES], NEG_INF)
    gmax = jnp.max(gl, axis=0, keepdims=True)
    grp = first_argmax(gl, gmax)
    grp_w = 1.0 / jnp.sum(jnp.exp(gl - gmax), axis=0, keepdims=True)

    sel = jnp.zeros((SUBLANES, tm), F32)
    for gi in range(N_GROUPS):
        lo = SUBLANES + gi * EXPERTS_PER_GROUP
        sel = jnp.where(grp == gi, lg[lo:lo + EXPERTS_PER_GROUP], sel)
    e = jnp.exp(sel - jnp.max(sel, axis=0, keepdims=True))
    prob = e / jnp.sum(e, axis=0, keepdims=True)
    p1 = jnp.max(prob, axis=0, keepdims=True)
    i1 = first_argmax(prob, p1)
    rest = jnp.where(row == i1, -1.0, prob)
    p2 = jnp.max(rest, axis=0, keepdims=True)
    i2 = first_argmax(rest, p2)
    den = p1 + p2
    base = grp * EXPERTS_PER_GROUP
    eid_ref[...] = jnp.where(row == 0, base + i1, jnp.where(row == 1, base + i2, 0))
    gate_ref[...] = jnp.where(row == 0, grp_w * p1 / den, jnp.where(row == 1, grp_w * p2 / den, 0.0))


def _router(h2, g_ffn, w_rt, b_r, *, tm):
    T, D = h2.shape
    return pl.pallas_call(
        _router_kernel,
        grid=(T // tm,),
        in_specs=[
            pl.BlockSpec((tm, D), lambda i: (i, 0)),
            pl.BlockSpec((1, D), lambda i: (0, 0)),
            pl.BlockSpec((R_ROWS, D), lambda i: (0, 0)),
            pl.BlockSpec((R_ROWS, 1), lambda i: (0, 0)),
        ],
        out_specs=[
            pl.BlockSpec((SUBLANES, tm), lambda i: (0, i)),
            pl.BlockSpec((SUBLANES, tm), lambda i: (0, i)),
            pl.BlockSpec((tm, D // 2), lambda i: (i, 0)),
        ],
        out_shape=[
            jax.ShapeDtypeStruct((SUBLANES, T), I32),
            jax.ShapeDtypeStruct((SUBLANES, T), F32),
            jax.ShapeDtypeStruct((T, D // 2), jnp.uint32),
        ],
        compiler_params=_cparams(("parallel",), 32),
        name="router",
    )(h2, g_ffn, w_rt, b_r)


def _rank_kernel(eid_ref, rank_ref, cnt_ref, carry):
    @pl.when(pl.program_id(0) == 0)
    def _():
        carry[...] = jnp.zeros(carry.shape, F32)

    tm = eid_ref.shape[1]
    e0 = eid_ref[0:1, :]
    e1 = eid_ref[1:2, :]
    erow = lax.broadcasted_iota(I32, (N_EXPERTS, tm), 0)
    hit = jnp.where((erow == e0) | (erow == e1), 1.0, 0.0)
    r = lax.broadcasted_iota(I32, (tm, tm), 0)
    c = lax.broadcasted_iota(I32, (tm, tm), 1)
    before = jnp.where(r < c, 1.0, 0.0).astype(BF16)
    pre = _dot(hit.astype(BF16), before) + carry[:, 0:1]
    rank0 = jnp.sum(jnp.where(erow == e0, pre, 0.0), axis=0, keepdims=True)
    rank1 = jnp.sum(jnp.where(erow == e1, pre, 0.0), axis=0, keepdims=True)
    row = lax.broadcasted_iota(I32, (SUBLANES, tm), 0)
    rank_ref[...] = jnp.where(row == 0, rank0, jnp.where(row == 1, rank1, 0.0)).astype(I32)
    total = carry[...] + jnp.sum(hit, axis=1, keepdims=True)
    carry[...] = total
    cnt_ref[...] = total.astype(I32)


def _rank(eid, *, tm):
    T = eid.shape[1]
    return pl.pallas_call(
        _rank_kernel,
        grid=(T // tm,),
        in_specs=[pl.BlockSpec((SUBLANES, tm), lambda i: (0, i))],
        out_specs=[
            pl.BlockSpec((SUBLANES, tm), lambda i: (0, i)),
            pl.BlockSpec((N_EXPERTS, LANES), lambda i: (0, 0)),
        ],
        out_shape=[
            jax.ShapeDtypeStruct((SUBLANES, T), I32),
            jax.ShapeDtypeStruct((N_EXPERTS, LANES), I32),
        ],
        scratch_shapes=[pltpu.VMEM((N_EXPERTS, LANES), F32)],
        compiler_params=_cparams(("arbitrary",), 32),
        name="rank",
    )(eid)


def _dispatch_kernel(dest_ref, pad0_ref, npad_ref, nb_ref, xn_hbm, xb_hbm, zbuf, sem, psem,
                     *, n_tok, tb, n_blocks):
    zbuf[...] = jnp.zeros(zbuf.shape, zbuf.dtype)

    def row_body(t, carry):
        for k in range(TOP_K):
            pltpu.make_async_copy(xn_hbm.at[pl.ds(t, 1), :],
                                  xb_hbm.at[pl.ds(dest_ref[k * n_tok + t], 1), :], sem).start()
        return carry
    lax.fori_loop(0, n_tok, row_body, 0, unroll=8)

    def pad_copy(slot):
        return pltpu.make_async_copy(zbuf.at[pl.ds(0, 1), :], xb_hbm.at[pl.ds(slot, 1), :], psem.at[0])

    def tail_copy(blk):
        return pltpu.make_async_copy(zbuf, xb_hbm.at[pl.ds(pl.multiple_of(blk * tb, tb), tb), :], psem.at[1])

    def for_each_pad(fn):
        def expert_body(e, carry):
            def body(r, c):
                fn(pad0_ref[e] + r)
                return c
            lax.fori_loop(0, npad_ref[e], body, 0)
            return carry
        lax.fori_loop(0, N_EXPERTS, expert_body, 0)

    def for_each_tail(fn):
        def body(b, c):
            fn(b)
            return c
        lax.fori_loop(nb_ref[0], n_blocks, body, 0)

    for_each_pad(lambda slot: pad_copy(slot).start())
    for_each_tail(lambda b: tail_copy(b).start())
    for_each_pad(lambda slot: pad_copy(slot).wait())
    for_each_tail(lambda b: tail_copy(b).wait())
    for k in range(TOP_K):
        pltpu.make_async_copy(xn_hbm, xb_hbm.at[pl.ds(0, n_tok), :], sem).wait()


def _dispatch(dest, pad0, npad, nb, xn, *, tb, n_blocks):
    T, C = xn.shape
    kern = functools.partial(_dispatch_kernel, n_tok=T, tb=tb, n_blocks=n_blocks)
    grid_spec = pltpu.PrefetchScalarGridSpec(
        num_scalar_prefetch=4,
        grid=(1,),
        in_specs=[pl.BlockSpec(memory_space=pl.ANY)],
        out_specs=pl.BlockSpec(memory_space=pl.ANY),
        scratch_shapes=[
            pltpu.VMEM((tb, C), xn.dtype),
            pltpu.SemaphoreType.DMA(()),
            pltpu.SemaphoreType.DMA((2,)),
        ],
    )
    return pl.pallas_call(
        kern,
        grid_spec=grid_spec,
        out_shape=jax.ShapeDtypeStruct((n_blocks * tb, C), xn.dtype),
        compiler_params=_cparams(("arbitrary",), 32),
        name="dispatch",
    )(dest, pad0, npad, nb, xn)


def _expert_kernel(be_ref, first_ref, nxt_ref, nb_ref, x_ref, wg_hbm, wu_hbm, wd_hbm, y_ref,
                   wg_f, wu_f, wd_f, wsem, wg_b, wu_b, wd_b):
    i = pl.program_id(0)
    nb = nb_ref[0]

    def weight_copies(e):
        return (pltpu.make_async_copy(wg_hbm.at[e], wg_f, wsem.at[0]),
                pltpu.make_async_copy(wu_hbm.at[e], wu_f, wsem.at[1]),
                pltpu.make_async_copy(wd_hbm.at[e], wd_f, wsem.at[2]))

    @pl.when(i == 0)
    def _():
        for cp in weight_copies(be_ref[0]):
            cp.start()

    first = (i < nb) & (first_ref[i] == 1)

    @pl.when(first)
    def _():
        for cp in weight_copies(0):
            cp.wait()
        wg_b[...] = wg_f[...].astype(BF16)
        wu_b[...] = wu_f[...].astype(BF16)
        wd_b[...] = wd_f[...].astype(BF16)

    @pl.when(first & (nxt_ref[i] >= 0))
    def _():
        for cp in weight_copies(jnp.maximum(nxt_ref[i], 0)):
            cp.start()

    @pl.when(i < nb)
    def _():
        n_lo, n_hi = _unpack_bf16_pairs(x_ref[...])
        half = n_lo.shape[1]
        a = _dot(n_lo, wg_b[0:half, :]) + _dot(n_hi, wg_b[half:, :])
        u = _dot(n_lo, wu_b[0:half, :]) + _dot(n_hi, wu_b[half:, :])
        hdn = (a / (1.0 + jnp.exp(-a))) * u
        y_ref[...] = _dot(hdn.astype(BF16), wd_b[...])

    @pl.when(i >= nb)
    def _():
        y_ref[...] = jnp.zeros(y_ref.shape, F32)


def _experts(xb, blk_e, blk_first, blk_next, nb, w_gate, w_up, w_down, *, tb, n_blocks):
    D, De = w_gate.shape[1:]
    C = xb.shape[1]

    def x_map(i, be, first, nxt, nbr):
        return (jnp.minimum(i, nbr[0] - 1), 0)

    grid_spec = pltpu.PrefetchScalarGridSpec(
        num_scalar_prefetch=4,
        grid=(n_blocks,),
        in_specs=[
            pl.BlockSpec((tb, C), x_map),
            pl.BlockSpec(memory_space=pl.ANY),
            pl.BlockSpec(memory_space=pl.ANY),
            pl.BlockSpec(memory_space=pl.ANY),
        ],
        out_specs=pl.BlockSpec((tb, D), lambda i, *_: (i, 0)),
        scratch_shapes=[
            pltpu.VMEM((D, De), F32),
            pltpu.VMEM((D, De), F32),
            pltpu.VMEM((De, D), F32),
            pltpu.SemaphoreType.DMA((3,)),
            pltpu.VMEM((D, De), BF16),
            pltpu.VMEM((D, De), BF16),
            pltpu.VMEM((De, D), BF16),
        ],
    )
    return pl.pallas_call(
        _expert_kernel,
        grid_spec=grid_spec,
        out_shape=jax.ShapeDtypeStruct((n_blocks * tb, D), F32),
        compiler_params=_cparams(("arbitrary",), 48),
        name="experts",
    )(blk_e, blk_first, blk_next, nb, xb, w_gate, w_up, w_down)


def _combine_kernel(dest_ref, y_hbm, h_ref, gate_ref, o_ref, ybuf, sem, *, tm, n_tok):
    i = pl.program_id(0)
    n = pl.num_programs(0)

    def row_copy(d, k, r, slot):
        return pltpu.make_async_copy(y_hbm.at[pl.ds(d, 1), :], ybuf.at[slot, k, pl.ds(r, 1), :], sem.at[slot])

    def start_gather(blk, slot):
        def body(r, carry):
            for k in range(TOP_K):
                row_copy(dest_ref[k * n_tok + blk * tm + r], k, r, slot).start()
            return carry
        lax.fori_loop(0, tm, body, 0, unroll=8)

    def wait_gather(slot):
        for k in range(TOP_K):
            pltpu.make_async_copy(y_hbm.at[pl.ds(0, tm), :], ybuf.at[slot, k], sem.at[slot]).wait()

    @pl.when(i == 0)
    def _():
        start_gather(0, 0)

    @pl.when(i + 1 < n)
    def _():
        start_gather(i + 1, (i + 1) % 2)

    slot = i % 2
    wait_gather(slot)
    gt = gate_ref[...]
    o_ref[...] = h_ref[...] + gt[:, 0:1] * ybuf[slot, 0] + gt[:, 1:2] * ybuf[slot, 1]


def _combine(dest, yb, h2, gate, *, tm):
    T, D = h2.shape
    kern = functools.partial(_combine_kernel, tm=tm, n_tok=T)
    grid_spec = pltpu.PrefetchScalarGridSpec(
        num_scalar_prefetch=1,
        grid=(T // tm,),
        in_specs=[
            pl.BlockSpec(memory_space=pl.ANY),
            pl.BlockSpec((tm, D), lambda i, d: (i, 0)),
            pl.BlockSpec((tm, TOP_K), lambda i, d: (i, 0)),
        ],
        out_specs=pl.BlockSpec((tm, D), lambda i, d: (i, 0)),
        scratch_shapes=[
            pltpu.VMEM((2, TOP_K, tm, D), F32),
            pltpu.SemaphoreType.DMA((2,)),
        ],
    )
    return pl.pallas_call(
        kern,
        grid_spec=grid_spec,
        out_shape=jax.ShapeDtypeStruct((T, D), F32),
        compiler_params=_cparams(("arbitrary",), 40),
        name="combine",
    )(dest, yb, h2, gate)


EXPERT_ROWS = 256


def kernel(x, mem, positions, g_attn, w_in, q_norm_g, k_norm_g, lambda_q1, lambda_k1, lambda_q2, lambda_k2, diff_subln_g, gla_w_a2, gla_b_a, gla_out_g, w_out, g_cross, g_mem, w_cq, w_ckv, cq_norm_g, ck_norm_g, w_co, g_ffn, w_router_grp, b_router_grp, w_router_exp, b_router_exp, w_gate, w_up, w_down):
    B, S, D = x.shape
    T = B * S
    n_mem = mem.shape[1]
    l = 0
    x2 = x.reshape(T, D)

    half = DIFF_QKDIM // 2
    freq = ROPE_THETA ** (-jnp.arange(half, dtype=F32) / half)
    freq = jnp.tile(freq, LANES // half)[None, :]
    q_scale = math.log2(math.e) * DIFF_QKDIM ** -0.5
    qkg = jnp.stack([jnp.tile(q_norm_g[l], 2) * q_scale, jnp.tile(k_norm_g[l], 2)])
    score_bound = 1.01 * DIFF_QKDIM * q_scale * jnp.max(jnp.abs(q_norm_g[l])) * jnp.max(jnp.abs(k_norm_g[l]))
    lvec = jnp.stack([lambda_q1[l], lambda_k1[l], lambda_q2[l], lambda_k2[l]])

    qk, mid, log_a, sgr = _inproj(x2, g_attn[l][None], positions.reshape(T, 1), freq, qkg, w_in[l].T,
                                  gla_w_a2[l], gla_b_a[l][None], tm=1024)
    diffattn = functools.partial(_diffattn, lvec, qk, mid, diff_subln_g[l][None], B=B, S=S, tq=512)
    mix_d = lax.cond(score_bound <= SCORE_BOUND,
                     functools.partial(diffattn, bounded=True), functools.partial(diffattn, bounded=False))
    mix_g = _gla(mid, log_a, sgr, gla_out_g[l][None], B=B, S=S, blk=512)
    h1 = _outproj(mix_d, mix_g, w_out[l], x2, tm=1024, tn=512)

    hdim = D // CROSS_HEADS
    qc = _normproj(h1, g_cross[l][None], w_cq[l], cq_norm_g[l][None] * (hdim ** -0.5),
                   tm=1024, tn=hdim, n_norm=CROSS_HEADS, name="cq")
    kv = _normproj(mem.reshape(B * n_mem, D), g_mem[l][None], w_ckv[l], ck_norm_g[l][None],
                   tm=B * n_mem, tn=hdim, n_norm=CROSS_HEADS, name="ckv")
    h2 = _cross(qc, kv, w_co[l], h1, S=S, n_mem=n_mem, tm=512, tn=512)

    w_rt = jnp.concatenate([w_router_grp[l].T, jnp.zeros((SUBLANES - N_GROUPS, D), F32), w_router_exp[l].T])
    b_r = jnp.concatenate([b_router_grp[l], jnp.zeros((SUBLANES - N_GROUPS,), F32), b_router_exp[l]])[:, None]
    eid, gate, xn = _router(h2, g_ffn[l][None], w_rt, b_r, tm=512)
    rank, cnt = _rank(eid, tm=512)

    tb = EXPERT_ROWS
    n_blocks = (T * TOP_K + N_EXPERTS * (tb - 1) + tb - 1) // tb
    counts = cnt[:, 0]
    pcounts = ((counts + tb - 1) // tb) * tb
    pends = jnp.cumsum(pcounts)
    pstarts = pends - pcounts
    nb = (pends[-1:] // tb).astype(I32)
    eids = jnp.arange(N_EXPERTS, dtype=I32)
    blk_start = jnp.arange(n_blocks, dtype=I32) * tb
    blk_e = jnp.minimum(jnp.sum(pends[None, :] <= blk_start[:, None], axis=1), N_EXPERTS - 1).astype(I32)
    blk_first = jnp.concatenate([jnp.ones((1,), I32), (blk_e[1:] != blk_e[:-1]).astype(I32)])
    used = jnp.where(counts > 0, eids, N_EXPERTS)
    next_used = jnp.concatenate([lax.cummin(used[::-1])[::-1][1:], jnp.full((1,), N_EXPERTS, I32)])
    next_used = jnp.where(next_used < N_EXPERTS, next_used, -1)
    blk_next = jnp.sum(jnp.where(blk_e[:, None] == eids, next_used, 0), axis=1).astype(I32)
    pick = eid[:TOP_K]
    pstart_of = jnp.sum(jnp.where(pick[..., None] == eids, pstarts, 0), axis=-1)
    dest = (pstart_of + rank[:TOP_K]).astype(I32).reshape(-1)
    pad0 = (pstarts + counts).astype(I32)
    npad = (pcounts - counts).astype(I32)

    xb = _dispatch(dest, pad0, npad, nb, xn, tb=tb, n_blocks=n_blocks)
    yb = _experts(xb, blk_e, blk_first, blk_next, nb, w_gate[l], w_up[l], w_down[l], tb=tb, n_blocks=n_blocks)
    out = _combine(dest, yb, h2, gate[:TOP_K].T, tm=256)
    return out.reshape(B, S, D)
```

```python
import functools
import math

import jax
import jax.numpy as jnp
from jax import lax
from jax.experimental import pallas as pl
from jax.experimental.pallas import tpu as pltpu

F32 = jnp.float32
BF16 = jnp.bfloat16
I32 = jnp.int32

LANES = 128
SUBLANES = 8

CHUNK = 64
ROPE_THETA = 10000.0
NORM_EPS = 1e-6
NEG_INF = -1e30
DIFF_HEADS = 8
DIFF_VDIM = 128
DIFF_QKDIM = 64
GLA_HEADS = 4
GLA_VDIM = 256
GLA_KDIM = 128
GLA_GATE_RANK = 16
GLA_TAU = 16.0
CROSS_HEADS = 4
N_GROUPS = 4
EXPERTS_PER_GROUP = 8
N_EXPERTS = N_GROUPS * EXPERTS_PER_GROUP
TOP_K = 2
LAM_INIT = 0.8 - 0.6 * math.exp(-0.3 * 0)

NT_DIMS = (((1,), (1,)), ((), ()))


def _cparams(semantics, vmem_mib):
    return pltpu.CompilerParams(dimension_semantics=semantics,
                                vmem_limit_bytes=vmem_mib * 1024 * 1024)


def _dot(a, b):
    return jnp.dot(a, b, preferred_element_type=F32)


def _dot_nt(a, b):
    return lax.dot_general(a, b, NT_DIMS, preferred_element_type=F32)


def _rms(x, g):
    ms = jnp.mean(x * x, axis=-1, keepdims=True)
    return x * lax.rsqrt(ms + NORM_EPS) * g


def _split_bf16(x):
    hi = x.astype(BF16)
    lo = (x - hi.astype(F32)).astype(BF16)
    return hi, lo


TN = 512
J_QK = 4
J_MID = 6
J_LR = J_QK + J_MID
J_GR = J_LR + 1
N_J = J_GR + 2


def _inproj_kernel(x_ref, g_ref, pos_ref, freq_ref, qkg_ref, w_ref, wgr_ref, wlr_ref, wa2_ref, ba_ref,
                   qk_ref, mid_ref, loga_ref, sgr_ref, n_scr, cos_scr, sin_scr, y_scr):
    j = pl.program_id(1)

    @pl.when(j == 0)
    def _():
        n_scr[...] = _rms(x_ref[...], g_ref[...]).astype(BF16)
        ang = pos_ref[...].astype(F32) * freq_ref[...]
        cos_scr[...] = jnp.cos(ang)
        sin_scr[...] = jnp.sin(ang)

    def qk_epilogue(jq):
        y_prev = y_scr.at[jq % 2]
        lane = lax.broadcasted_iota(I32, (1, LANES), 1)
        low_seg = lane < DIFF_QKDIM
        first_half = (lane % DIFF_QKDIM) < (DIFF_QKDIM // 2)
        gain = qkg_ref[jq // (J_QK // 2):jq // (J_QK // 2) + 1, :]
        cos = cos_scr[...]
        sin = sin_scr[...]
        for c in range(TN // LANES):
            yb = y_prev[:, c * LANES:(c + 1) * LANES]
            y2 = yb * yb
            s_lo = jnp.sum(jnp.where(low_seg, y2, 0.0), axis=-1, keepdims=True)
            s_hi = jnp.sum(jnp.where(low_seg, 0.0, y2), axis=-1, keepdims=True)
            ms = jnp.where(low_seg, s_lo, s_hi) * (1.0 / DIFF_QKDIM)
            yn = yb * lax.rsqrt(ms + NORM_EPS) * gain
            rot = jnp.where(first_half,
                            -pltpu.roll(yn, LANES - DIFF_QKDIM // 2, 1),
                            pltpu.roll(yn, DIFF_QKDIM // 2, 1))
            qk_ref[:, c * LANES:(c + 1) * LANES] = (yn * cos + rot * sin).astype(BF16)

    for jq in range(J_QK + 1):
        @pl.when(j == jq)
        def _():
            y = _dot_nt(n_scr[...], w_ref[...].astype(BF16))
            if jq > 0:
                qk_epilogue(jq - 1)
            if jq < J_QK:
                y_scr[jq % 2] = y
            else:
                mid_ref[...] = y.astype(BF16)

    @pl.when((j > J_QK) & (j < J_LR))
    def _():
        mid_ref[...] = _dot_nt(n_scr[...], w_ref[...].astype(BF16)).astype(BF16)

    @pl.when(j == J_LR)
    def _():
        lr = _dot_nt(n_scr[...], wlr_ref[...].astype(BF16))
        z = _dot(lr.astype(BF16), wa2_ref[...].astype(BF16)) + ba_ref[...]
        log_sig = jnp.minimum(z, 0.0) - jnp.log(1.0 + jnp.exp(-jnp.abs(z)))
        loga_ref[...] = log_sig * (1.0 / GLA_TAU)

    @pl.when(j >= J_GR)
    def _():
        y = _dot_nt(n_scr[...], wgr_ref[...].astype(BF16))
        sgr_ref[...] = (y / (1.0 + jnp.exp(-y))).astype(BF16)


def _inproj(x2, g_attn, pos2, freq, qkg, w_t, w_a2, b_a, *, tm):
    T, D = x2.shape
    n_mid = J_MID * TN
    n_gk = GLA_HEADS * GLA_KDIM
    lr0 = J_LR * TN
    gr0 = lr0 + GLA_GATE_RANK
    n_gr = w_t.shape[0] - gr0
    assert n_gr == 2 * TN and lr0 % GLA_GATE_RANK == 0
    return pl.pallas_call(
        _inproj_kernel,
        grid=(T // tm, N_J),
        in_specs=[
            pl.BlockSpec((tm, D), lambda i, j: (i, 0)),
            pl.BlockSpec((1, D), lambda i, j: (0, 0)),
            pl.BlockSpec((tm, 1), lambda i, j: (i, 0)),
            pl.BlockSpec((1, LANES), lambda i, j: (0, 0)),
            pl.BlockSpec((2, LANES), lambda i, j: (0, 0)),
            pl.BlockSpec((TN, D), lambda i, j: (jnp.minimum(j, J_LR - 1), 0)),
            pl.BlockSpec((pl.Element(TN), pl.Element(D)),
                         lambda i, j: (pl.multiple_of(gr0 + TN * jnp.clip(j - J_GR, 0, 1), SUBLANES), 0)),
            pl.BlockSpec((GLA_GATE_RANK, D), lambda i, j: (lr0 // GLA_GATE_RANK, 0)),
            pl.BlockSpec((GLA_GATE_RANK, n_gk), lambda i, j: (0, 0)),
            pl.BlockSpec((1, n_gk), lambda i, j: (0, 0)),
        ],
        out_specs=[
            pl.BlockSpec((tm, TN), lambda i, j: (i, jnp.clip(j - 1, 0, J_QK - 1))),
            pl.BlockSpec((tm, TN), lambda i, j: (i, jnp.clip(j - J_QK, 0, J_MID - 1))),
            pl.BlockSpec((tm, n_gk), lambda i, j: (i, 0)),
            pl.BlockSpec((tm, TN), lambda i, j: (i, jnp.clip(j - J_GR, 0, 1))),
        ],
        out_shape=[
            jax.ShapeDtypeStruct((T, J_QK * TN), BF16),
            jax.ShapeDtypeStruct((T, n_mid), BF16),
            jax.ShapeDtypeStruct((T, n_gk), F32),
            jax.ShapeDtypeStruct((T, n_gr), BF16),
        ],
        scratch_shapes=[
            pltpu.VMEM((tm, D), BF16),
            pltpu.VMEM((tm, LANES), F32),
            pltpu.VMEM((tm, LANES), F32),
            pltpu.VMEM((2, tm, TN), F32),
        ],
        compiler_params=_cparams(("parallel", "arbitrary"), 56),
        name="inproj",
    )(x2, g_attn, pos2, freq, qkg, w_t, w_t, w_t, w_a2, b_a)


SCORE_BOUND = 80.0


def _diffattn_kernel(lv_ref, q_ref, k_ref, v_ref, sg_ref, o_ref, vext, diag_mask, acc1, acc2, m1, m2,
                     *, tq, bounded):
    i = pl.program_id(2)
    lv = lv_ref[...]
    lam = (jnp.exp(jnp.sum(lv[0:1] * lv[1:2], axis=-1, keepdims=True))
           - jnp.exp(jnp.sum(lv[2:3] * lv[3:4], axis=-1, keepdims=True)) + LAM_INIT)

    @pl.when(i == 0)
    def _():
        S = v_ref.shape[0]
        lane_s = lax.broadcasted_iota(I32, (S, LANES), 1)
        vext[:, 0:DIFF_VDIM] = v_ref[...]
        vext[:, DIFF_VDIM:] = jnp.where(lane_s == 0, 1.0, 0.0).astype(BF16)
        row_chunk = lax.broadcasted_iota(I32, (tq, tq), 0) // CHUNK
        col_chunk = lax.broadcasted_iota(I32, (tq, tq), 1) // CHUNK
        diag_mask[...] = jnp.where(col_chunk <= row_chunk, 1.0, 0.0).astype(BF16)

    q = q_ref[...]
    lane = lax.broadcasted_iota(I32, (1, LANES), 1)
    zero = jnp.zeros_like(q)
    q1 = jnp.where(lane < DIFF_QKDIM, q, zero)
    q2 = jnp.where(lane < DIFF_QKDIM, zero, q)

    acc1[...] = jnp.zeros(acc1.shape, F32)
    acc2[...] = jnp.zeros(acc2.shape, F32)
    if not bounded:
        m1[...] = jnp.full(m1.shape, NEG_INF, F32)
        m2[...] = jnp.full(m2.shape, NEG_INF, F32)

    def step(j, masked):
        start = pl.multiple_of(j * tq, tq)
        k = k_ref[pl.ds(start, tq), :]
        v = vext[pl.ds(start, tq), :]
        for qc, acc, m in ((q1, acc1, m1), (q2, acc2, m2)):
            s = lax.dot_general(qc, k, NT_DIMS, preferred_element_type=F32)
            if bounded:
                p = jnp.exp2(s).astype(BF16)
                if masked:
                    p = p * diag_mask[...]
                acc[...] += _dot(p, v)
            else:
                if masked:
                    s = jnp.where(diag_mask[...] > 0, s, NEG_INF)
                m_old = m[...]
                m_new = jnp.maximum(m_old, jnp.max(s, axis=-1, keepdims=True))
                p = jnp.exp2(s - m_new)
                acc[...] = jnp.exp2(m_old - m_new) * acc[...] + _dot(p.astype(BF16), v)
                m[...] = m_new

    def body(jj, carry):
        step(2 * jj, False)
        step(2 * jj + 1, False)
        return carry

    lax.fori_loop(0, i // 2, body, 0)

    @pl.when(i % 2 == 1)
    def _():
        step(i - 1, False)
        step(i, True)

    @pl.when(i % 2 == 0)
    def _():
        step(i, True)

    a1 = acc1[...]
    a2 = acc2[...]
    o = (a1[:, :DIFF_VDIM] / a1[:, DIFF_VDIM:DIFF_VDIM + 1]
         - lam * (a2[:, :DIFF_VDIM] / a2[:, DIFF_VDIM:DIFF_VDIM + 1]))
    o_ref[...] = (_rms(o, sg_ref[...]) * (1.0 - LAM_INIT)).astype(BF16)


def _diffattn(lvec, qk, mid, subln_g, *, B, S, tq, bounded):
    T = B * S
    nq = S // tq
    kern = functools.partial(_diffattn_kernel, tq=tq, bounded=bounded)
    return pl.pallas_call(
        kern,
        grid=(B, DIFF_HEADS, nq),
        in_specs=[
            pl.BlockSpec((4, DIFF_QKDIM), lambda b, h, i: (0, 0)),
            pl.BlockSpec((tq, LANES), lambda b, h, i: (b * nq + i, h)),
            pl.BlockSpec((S, LANES), lambda b, h, i: (b, DIFF_HEADS + h)),
            pl.BlockSpec((S, LANES), lambda b, h, i: (b, h)),
            pl.BlockSpec((1, DIFF_VDIM), lambda b, h, i: (0, 0)),
        ],
        out_specs=pl.BlockSpec((tq, DIFF_VDIM), lambda b, h, i: (b * nq + i, h)),
        out_shape=jax.ShapeDtypeStruct((T, DIFF_HEADS * DIFF_VDIM), BF16),
        scratch_shapes=[
            pltpu.VMEM((S, 2 * DIFF_VDIM), BF16),
            pltpu.VMEM((tq, tq), BF16),
            pltpu.VMEM((tq, 2 * DIFF_VDIM), F32),
            pltpu.VMEM((tq, 2 * DIFF_VDIM), F32),
            pltpu.VMEM((tq, 1), F32),
            pltpu.VMEM((tq, 1), F32),
        ],
        compiler_params=_cparams(("parallel", "parallel", "arbitrary"), 32),
        name="diffattn_bounded" if bounded else "diffattn_online",
    )(lvec, qk, qk, mid, subln_g)


def _gla_kernel(q_ref, k_ref, v_ref, la_ref, sgr_ref, g_ref, o_ref, state, *, blk):
    @pl.when(pl.program_id(2) == 0)
    def _():
        state[...] = jnp.zeros(state.shape, F32)

    la_t = la_ref[...].T
    k_t = k_ref[...].astype(F32).T
    r = lax.broadcasted_iota(I32, (blk, blk), 0)
    c = lax.broadcasted_iota(I32, (blk, blk), 1)
    same = (r // CHUNK) == (c // CHUNK)
    tri = jnp.where(same & (r <= c), 1.0, 0.0).astype(BF16)
    ones = jnp.where(same, 1.0, 0.0).astype(BF16)
    hi, lo = _split_bf16(la_t)
    cum_t = _dot(hi, tri) + _dot(lo, tri)
    tot_t = _dot(hi, ones) + _dot(lo, ones)
    kd_t = k_t * jnp.exp(tot_t - cum_t)

    lane = lax.broadcasted_iota(I32, (1, LANES), 1)
    scale = GLA_KDIM ** -0.5
    for ck in range(blk // CHUNK):
        pair = slice((ck // 2) * LANES, (ck // 2 + 1) * LANES)
        rows = slice(ck * CHUNK, (ck + 1) * CHUNK)
        in_chunk = (lane // CHUNK) == (ck % 2)
        kd = jnp.where(in_chunk, kd_t[:, pair], 0.0).astype(BF16)
        d_state = _dot(kd, v_ref[pair, :])
        decay = jnp.exp(tot_t[:, ck * CHUNK:ck * CHUNK + 1])
        st = decay * state[...] + d_state
        state[...] = st
        o = _dot(q_ref[rows, :], st.astype(BF16)) * scale
        gate = sgr_ref[rows, :].astype(F32)
        o_ref[rows, :] = (_rms(o, g_ref[...]) * gate).astype(BF16)


def _gla(mid, log_a, sgr, out_g, *, B, S, blk):
    T = B * S
    ns = S // blk
    kern = functools.partial(_gla_kernel, blk=blk)
    q_col0 = (DIFF_HEADS * DIFF_VDIM) // GLA_KDIM
    k_col0 = q_col0 + GLA_HEADS
    v_col0 = (DIFF_HEADS * DIFF_VDIM + 2 * GLA_HEADS * GLA_KDIM) // GLA_VDIM
    return pl.pallas_call(
        kern,
        grid=(B, GLA_HEADS, ns),
        in_specs=[
            pl.BlockSpec((blk, GLA_KDIM), lambda b, h, s: (b * ns + s, q_col0 + h)),
            pl.BlockSpec((blk, GLA_KDIM), lambda b, h, s: (b * ns + s, k_col0 + h)),
            pl.BlockSpec((blk, GLA_VDIM), lambda b, h, s: (b * ns + s, v_col0 + h)),
            pl.BlockSpec((blk, GLA_KDIM), lambda b, h, s: (b * ns + s, h)),
            pl.BlockSpec((blk, GLA_VDIM), lambda b, h, s: (b * ns + s, h)),
            pl.BlockSpec((1, GLA_VDIM), lambda b, h, s: (0, 0)),
        ],
        out_specs=pl.BlockSpec((blk, GLA_VDIM), lambda b, h, s: (b * ns + s, h)),
        out_shape=jax.ShapeDtypeStruct((T, GLA_HEADS * GLA_VDIM), BF16),
        scratch_shapes=[pltpu.VMEM((GLA_KDIM, GLA_VDIM), F32)],
        compiler_params=_cparams(("parallel", "parallel", "arbitrary"), 32),
        name="gla",
    )(mid, mid, mid, log_a, sgr, out_g)


def _resident_w_map(n_j):
    return lambda i, j: (0, jnp.where(i == 0, j, n_j - 1))


def _outproj_kernel(a_ref, b_ref, wa_ref, wb_ref, x_ref, o_ref, w_scr):
    j = pl.program_id(1)
    ka = a_ref.shape[1]

    @pl.when(pl.program_id(0) == 0)
    def _():
        w_scr[j, 0:ka, :] = wa_ref[...].astype(BF16)
        w_scr[j, ka:, :] = wb_ref[...].astype(BF16)

    acc = _dot(a_ref[...], w_scr[j, 0:ka, :]) + _dot(b_ref[...], w_scr[j, ka:, :])
    o_ref[...] = x_ref[...] + acc


def _outproj(a, b, w_out, x2, *, tm, tn):
    T, ka = a.shape
    kb = b.shape[1]
    assert ka == kb
    D = w_out.shape[1]
    n_j = D // tn
    return pl.pallas_call(
        _outproj_kernel,
        grid=(T // tm, n_j),
        in_specs=[
            pl.BlockSpec((tm, ka), lambda i, j: (i, 0)),
            pl.BlockSpec((tm, kb), lambda i, j: (i, 0)),
            pl.BlockSpec((ka, tn), _resident_w_map(n_j)),
            pl.BlockSpec((kb, tn), lambda i, j: (1, jnp.where(i == 0, j, n_j - 1))),
            pl.BlockSpec((tm, tn), lambda i, j: (i, j)),
        ],
        out_specs=pl.BlockSpec((tm, tn), lambda i, j: (i, j)),
        out_shape=jax.ShapeDtypeStruct((T, D), F32),
        scratch_shapes=[pltpu.VMEM((n_j, ka + kb, tn), BF16)],
        compiler_params=_cparams(("arbitrary", "arbitrary"), 48),
        name="outproj",
    )(a, b, w_out, w_out, x2)


def _normproj_kernel(x_ref, g_ref, w_ref, hg_ref, o_ref, n_scr, *w_scr, n_norm):
    j = pl.program_id(1)

    @pl.when(j == 0)
    def _():
        n_scr[...] = _rms(x_ref[...], g_ref[...]).astype(BF16)

    if w_scr:
        @pl.when(pl.program_id(0) == 0)
        def _():
            w_scr[0][j] = w_ref[...].astype(BF16)
        y = _dot(n_scr[...], w_scr[0][j])
    else:
        y = _dot(n_scr[...], w_ref[...].astype(BF16))

    @pl.when(j < n_norm)
    def _():
        o_ref[...] = _rms(y, hg_ref[...]).astype(BF16)

    @pl.when(j >= n_norm)
    def _():
        o_ref[...] = y.astype(BF16)


def _normproj(x2, g, w, head_g, *, tm, tn, n_norm, name):
    T, D = x2.shape
    N = w.shape[1]
    kern = functools.partial(_normproj_kernel, n_norm=n_norm)
    n_j = N // tn
    resident = T // tm > 1
    return pl.pallas_call(
        kern,
        grid=(T // tm, n_j),
        in_specs=[
            pl.BlockSpec((tm, D), lambda i, j: (i, 0)),
            pl.BlockSpec((1, D), lambda i, j: (0, 0)),
            pl.BlockSpec((D, tn), _resident_w_map(n_j) if resident else (lambda i, j: (0, j))),
            pl.BlockSpec((1, tn), lambda i, j: (0, 0)),
        ],
        out_specs=pl.BlockSpec((tm, tn), lambda i, j: (i, j)),
        out_shape=jax.ShapeDtypeStruct((T, N), BF16),
        scratch_shapes=[pltpu.VMEM((tm, D), BF16)] + ([pltpu.VMEM((n_j, D, tn), BF16)] if resident else []),
        compiler_params=_cparams(("arbitrary", "arbitrary"), 48),
        name=name,
    )(x2, g, w, head_g)


def _cross_kernel(q_ref, k_ref, v_ref, w_ref, h_ref, o_ref, att_scr, w_scr, *, hdim):
    j = pl.program_id(1)

    @pl.when(pl.program_id(0) == 0)
    def _():
        w_scr[j] = w_ref[...].astype(BF16)

    @pl.when(j == 0)
    def _():
        for hd in range(CROSS_HEADS):
            cols = slice(hd * hdim, (hd + 1) * hdim)
            s = lax.dot_general(q_ref[:, cols], k_ref[:, cols], NT_DIMS, preferred_element_type=F32)
            p = jnp.exp(s - jnp.max(s, axis=-1, keepdims=True))
            l = jnp.sum(p, axis=-1, keepdims=True)
            att_scr[:, cols] = (_dot(p.astype(BF16), v_ref[:, cols]) / l).astype(BF16)

    o_ref[...] = h_ref[...] + _dot(att_scr[...], w_scr[j])


def _cross(qc, kv, w_co, h1, *, S, n_mem, tm, tn):
    T, D = qc.shape
    per_b = S // tm
    kern = functools.partial(_cross_kernel, hdim=D // CROSS_HEADS)
    n_j = D // tn
    return pl.pallas_call(
        kern,
        grid=(T // tm, n_j),
        in_specs=[
            pl.BlockSpec((tm, D), lambda i, j: (i, 0)),
            pl.BlockSpec((n_mem, D), lambda i, j: (i // per_b, 0)),
            pl.BlockSpec((n_mem, D), lambda i, j: (i // per_b, 1)),
            pl.BlockSpec((D, tn), _resident_w_map(n_j)),
            pl.BlockSpec((tm, tn), lambda i, j: (i, j)),
        ],
        out_specs=pl.BlockSpec((tm, tn), lambda i, j: (i, j)),
        out_shape=jax.ShapeDtypeStruct((T, D), F32),
        scratch_shapes=[pltpu.VMEM((tm, D), BF16), pltpu.VMEM((n_j, D, tn), BF16)],
        compiler_params=_cparams(("arbitrary", "arbitrary"), 48),
        name="cross",
    )(qc, kv, kv, w_co, h1)


R_ROWS = SUBLANES + N_EXPERTS


def _pack_bf16_pairs(xb16):
    c = xb16.shape[1] // 2
    u = lax.bitcast_convert_type(xb16.astype(F32), jnp.uint32)
    return (u[:, :c] >> 16) | (u[:, c:] & jnp.uint32(0xFFFF0000))


def _unpack_bf16_pairs(w):
    lo = lax.bitcast_convert_type(w << 16, F32).astype(BF16)
    hi = lax.bitcast_convert_type(w & jnp.uint32(0xFFFF0000), F32).astype(BF16)
    return lo, hi


def _router_kernel(h_ref, g_ref, wt_ref, b_ref, eid_ref, gate_ref, xn_ref):
    n = _rms(h_ref[...], g_ref[...])
    nh, nl = _split_bf16(n)
    xn_ref[...] = _pack_bf16_pairs(nh)
    wh, wl = _split_bf16(wt_ref[...])
    nt = functools.partial(lax.dot_general, dimension_numbers=NT_DIMS, preferred_element_type=F32)
    lg = nt(wh, nh) + nt(wh, nl) + nt(wl, nh) + b_ref[...]

    tm = lg.shape[1]
    row = lax.broadcasted_iota(I32, (SUBLANES, tm), 0)

    def first_argmax(v, vmax):
        return jnp.min(jnp.where(v == vmax, row, SUBLANES), axis=0, keepdims=True)

    gl = jnp.where(row < N_GROUPS, lg[0:SUBLANES], NEG_INF)
    gmax = jnp.max(gl, axis=0, keepdims=True)
    grp = first_argmax(gl, gmax)
    grp_w = 1.0 / jnp.sum(jnp.exp(gl - gmax), axis=0, keepdims=True)

    sel = jnp.zeros((SUBLANES, tm), F32)
    for gi in range(N_GROUPS):
        lo = SUBLANES + gi * EXPERTS_PER_GROUP
        sel = jnp.where(grp == gi, lg[lo:lo + EXPERTS_PER_GROUP], sel)
    e = jnp.exp(sel - jnp.max(sel, axis=0, keepdims=True))
    prob = e / jnp.sum(e, axis=0, keepdims=True)
    p1 = jnp.max(prob, axis=0, keepdims=True)
    i1 = first_argmax(prob, p1)
    rest = jnp.where(row == i1, -1.0, prob)
    p2 = jnp.max(rest, axis=0, keepdims=True)
    i2 = first_argmax(rest, p2)
    den = p1 + p2
    base = grp * EXPERTS_PER_GROUP
    eid_ref[...] = jnp.where(row == 0, base + i1, jnp.where(row == 1, base + i2, 0))
    gate_ref[...] = jnp.where(row == 0, grp_w * p1 / den, jnp.where(row == 1, grp_w * p2 / den, 0.0))


def _router(h2, g_ffn, w_rt, b_r, *, tm):
    T, D = h2.shape
    return pl.pallas_call(
        _router_kernel,
        grid=(T // tm,),
        in_specs=[
            pl.BlockSpec((tm, D), lambda i: (i, 0)),
            pl.BlockSpec((1, D), lambda i: (0, 0)),
            pl.BlockSpec((R_ROWS, D), lambda i: (0, 0)),
            pl.BlockSpec((R_ROWS, 1), lambda i: (0, 0)),
        ],
        out_specs=[
            pl.BlockSpec((SUBLANES, tm), lambda i: (0, i)),
            pl.BlockSpec((SUBLANES, tm), lambda i: (0, i)),
            pl.BlockSpec((tm, D // 2), lambda i: (i, 0)),
        ],
        out_shape=[
            jax.ShapeDtypeStruct((SUBLANES, T), I32),
            jax.ShapeDtypeStruct((SUBLANES, T), F32),
            jax.ShapeDtypeStruct((T, D // 2), jnp.uint32),
        ],
        compiler_params=_cparams(("parallel",), 32),
        name="router",
    )(h2, g_ffn, w_rt, b_r)


def _rank_kernel(eid_ref, rank_ref, cnt_ref, carry):
    @pl.when(pl.program_id(0) == 0)
    def _():
        carry[...] = jnp.zeros(carry.shape, F32)

    tm = eid_ref.shape[1]
    e0 = eid_ref[0:1, :]
    e1 = eid_ref[1:2, :]
    erow = lax.broadcasted_iota(I32, (N_EXPERTS, tm), 0)
    hit = jnp.where((erow == e0) | (erow == e1), 1.0, 0.0)
    r = lax.broadcasted_iota(I32, (tm, tm), 0)
    c = lax.broadcasted_iota(I32, (tm, tm), 1)
    before = jnp.where(r < c, 1.0, 0.0).astype(BF16)
    pre = _dot(hit.astype(BF16), before) + carry[:, 0:1]
    rank0 = jnp.sum(jnp.where(erow == e0, pre, 0.0), axis=0, keepdims=True)
    rank1 = jnp.sum(jnp.where(erow == e1, pre, 0.0), axis=0, keepdims=True)
    row = lax.broadcasted_iota(I32, (SUBLANES, tm), 0)
    rank_ref[...] = jnp.where(row == 0, rank0, jnp.where(row == 1, rank1, 0.0)).astype(I32)
    total = carry[...] + jnp.sum(hit, axis=1, keepdims=True)
    carry[...] = total
    cnt_ref[...] = total.astype(I32)


def _rank(eid, *, tm):
    T = eid.shape[1]
    return pl.pallas_call(
        _rank_kernel,
        grid=(T // tm,),
        in_specs=[pl.BlockSpec((SUBLANES, tm), lambda i: (0, i))],
        out_specs=[
            pl.BlockSpec((SUBLANES, tm), lambda i: (0, i)),
            pl.BlockSpec((N_EXPERTS, LANES), lambda i: (0, 0)),
        ],
        out_shape=[
            jax.ShapeDtypeStruct((SUBLANES, T), I32),
            jax.ShapeDtypeStruct((N_EXPERTS, LANES), I32),
        ],
        scratch_shapes=[pltpu.VMEM((N_EXPERTS, LANES), F32)],
        compiler_params=_cparams(("arbitrary",), 32),
        name="rank",
    )(eid)


def _dispatch_kernel(dest_ref, pad0_ref, npad_ref, nb_ref, xn_ref, xb_hbm, zbuf, sem, psem,
                     *, n_tok, tb, n_blocks):
    i = pl.program_id(0)
    tm = xn_ref.shape[0]
    base = i * tm

    def row_body(r, carry):
        for k in range(TOP_K):
            pltpu.make_async_copy(xn_ref.at[pl.ds(r, 1), :],
                                  xb_hbm.at[pl.ds(dest_ref[k * n_tok + base + r], 1), :], sem).start()
        return carry
    lax.fori_loop(0, tm, row_body, 0, unroll=8)

    @pl.when(i == 0)
    def _():
        _dispatch_fill(pad0_ref, npad_ref, nb_ref, xb_hbm, zbuf, psem, tb=tb, n_blocks=n_blocks)

    for k in range(TOP_K):
        pltpu.make_async_copy(xn_ref, xb_hbm.at[pl.ds(0, tm), :], sem).wait()


def _dispatch_fill(pad0_ref, npad_ref, nb_ref, xb_hbm, zbuf, psem, *, tb, n_blocks):
    zbuf[...] = jnp.zeros(zbuf.shape, zbuf.dtype)

    def pad_copy(slot):
        return pltpu.make_async_copy(zbuf.at[pl.ds(0, 1), :], xb_hbm.at[pl.ds(slot, 1), :], psem.at[0])

    def tail_copy(blk):
        return pltpu.make_async_copy(zbuf, xb_hbm.at[pl.ds(pl.multiple_of(blk * tb, tb), tb), :], psem.at[1])

    def for_each_pad(fn):
        def expert_body(e, carry):
            def body(r, c):
                fn(pad0_ref[e] + r)
                return c
            lax.fori_loop(0, npad_ref[e], body, 0)
            return carry
        lax.fori_loop(0, N_EXPERTS, expert_body, 0)

    def for_each_tail(fn):
        def body(b, c):
            fn(b)
            return c
        lax.fori_loop(nb_ref[0], n_blocks, body, 0)

    for_each_pad(lambda slot: pad_copy(slot).start())
    for_each_tail(lambda b: tail_copy(b).start())
    for_each_pad(lambda slot: pad_copy(slot).wait())
    for_each_tail(lambda b: tail_copy(b).wait())


def _dispatch(dest, pad0, npad, nb, xn, *, tb, n_blocks, tm):
    T, C = xn.shape
    kern = functools.partial(_dispatch_kernel, n_tok=T, tb=tb, n_blocks=n_blocks)
    grid_spec = pltpu.PrefetchScalarGridSpec(
        num_scalar_prefetch=4,
        grid=(T // tm,),
        in_specs=[pl.BlockSpec((tm, C), lambda i, *_: (i, 0))],
        out_specs=pl.BlockSpec(memory_space=pl.ANY),
        scratch_shapes=[
            pltpu.VMEM((tb, C), xn.dtype),
            pltpu.SemaphoreType.DMA(()),
            pltpu.SemaphoreType.DMA((2,)),
        ],
    )
    return pl.pallas_call(
        kern,
        grid_spec=grid_spec,
        out_shape=jax.ShapeDtypeStruct((n_blocks * tb, C), xn.dtype),
        compiler_params=_cparams(("arbitrary",), 32),
        name="dispatch",
    )(dest, pad0, npad, nb, xn)


def _expert_kernel(be_ref, first_ref, nxt_ref, nb_ref, x_ref, wg_hbm, wu_hbm, wd_hbm, y_ref,
                   wg_f, wu_f, wd_f, wsem, wg_b, wu_b, wd_b):
    i = pl.program_id(0)
    nb = nb_ref[0]

    def weight_copies(e):
        return (pltpu.make_async_copy(wg_hbm.at[e], wg_f, wsem.at[0]),
                pltpu.make_async_copy(wu_hbm.at[e], wu_f, wsem.at[1]),
                pltpu.make_async_copy(wd_hbm.at[e], wd_f, wsem.at[2]))

    @pl.when(i == 0)
    def _():
        for cp in weight_copies(be_ref[0]):
            cp.start()

    first = (i < nb) & (first_ref[i] == 1)

    @pl.when(first)
    def _():
        for cp in weight_copies(0):
            cp.wait()
        wg_b[...] = wg_f[...].astype(BF16)
        wu_b[...] = wu_f[...].astype(BF16)
        wd_b[...] = wd_f[...].astype(BF16)

    @pl.when(first & (nxt_ref[i] >= 0))
    def _():
        for cp in weight_copies(jnp.maximum(nxt_ref[i], 0)):
            cp.start()

    @pl.when(i < nb)
    def _():
        n_lo, n_hi = _unpack_bf16_pairs(x_ref[...])
        half = n_lo.shape[1]
        a = _dot(n_lo, wg_b[0:half, :]) + _dot(n_hi, wg_b[half:, :])
        u = _dot(n_lo, wu_b[0:half, :]) + _dot(n_hi, wu_b[half:, :])
        hdn = (a / (1.0 + jnp.exp(-a))) * u
        y_ref[...] = _dot(hdn.astype(BF16), wd_b[...])

    @pl.when(i >= nb)
    def _():
        y_ref[...] = jnp.zeros(y_ref.shape, F32)


def _experts(xb, blk_e, blk_first, blk_next, nb, w_gate, w_up, w_down, *, tb, n_blocks):
    D, De = w_gate.shape[1:]
    C = xb.shape[1]

    def x_map(i, be, first, nxt, nbr):
        return (jnp.minimum(i, nbr[0] - 1), 0)

    grid_spec = pltpu.PrefetchScalarGridSpec(
        num_scalar_prefetch=4,
        grid=(n_blocks,),
        in_specs=[
            pl.BlockSpec((tb, C), x_map),
            pl.BlockSpec(memory_space=pl.ANY),
            pl.BlockSpec(memory_space=pl.ANY),
            pl.BlockSpec(memory_space=pl.ANY),
        ],
        out_specs=pl.BlockSpec((tb, D), lambda i, *_: (i, 0)),
        scratch_shapes=[
            pltpu.VMEM((D, De), F32),
            pltpu.VMEM((D, De), F32),
            pltpu.VMEM((De, D), F32),
            pltpu.SemaphoreType.DMA((3,)),
            pltpu.VMEM((D, De), BF16),
            pltpu.VMEM((D, De), BF16),
            pltpu.VMEM((De, D), BF16),
        ],
    )
    return pl.pallas_call(
        _expert_kernel,
        grid_spec=grid_spec,
        out_shape=jax.ShapeDtypeStruct((n_blocks * tb, D), F32),
        compiler_params=_cparams(("arbitrary",), 48),
        name="experts",
    )(blk_e, blk_first, blk_next, nb, xb, w_gate, w_up, w_down)


def _combine_kernel(dest_ref, y_hbm, h_ref, gate_ref, o_ref, ybuf, sem, *, tm, n_tok):
    i = pl.program_id(0)
    n = pl.num_programs(0)

    def row_copy(d, k, r, slot):
        return pltpu.make_async_copy(y_hbm.at[pl.ds(d, 1), :], ybuf.at[slot, k, pl.ds(r, 1), :], sem.at[slot])

    def start_gather(blk, slot):
        def body(r, carry):
            for k in range(TOP_K):
                row_copy(dest_ref[k * n_tok + blk * tm + r], k, r, slot).start()
            return carry
        lax.fori_loop(0, tm, body, 0, unroll=8)

    def wait_gather(slot):
        for k in range(TOP_K):
            pltpu.make_async_copy(y_hbm.at[pl.ds(0, tm), :], ybuf.at[slot, k], sem.at[slot]).wait()

    @pl.when(i == 0)
    def _():
        start_gather(0, 0)

    @pl.when(i + 1 < n)
    def _():
        start_gather(i + 1, (i + 1) % 2)

    slot = i % 2
    wait_gather(slot)
    gt = gate_ref[...]
    o_ref[...] = h_ref[...] + gt[:, 0:1] * ybuf[slot, 0] + gt[:, 1:2] * ybuf[slot, 1]


def _combine(dest, yb, h2, gate, *, tm):
    T, D = h2.shape
    kern = functools.partial(_combine_kernel, tm=tm, n_tok=T)
    grid_spec = pltpu.PrefetchScalarGridSpec(
        num_scalar_prefetch=1,
        grid=(T // tm,),
        in_specs=[
            pl.BlockSpec(memory_space=pl.ANY),
            pl.BlockSpec((tm, D), lambda i, d: (i, 0)),
            pl.BlockSpec((tm, TOP_K), lambda i, d: (i, 0)),
        ],
        out_specs=pl.BlockSpec((tm, D), lambda i, d: (i, 0)),
        scratch_shapes=[
            pltpu.VMEM((2, TOP_K, tm, D), F32),
            pltpu.SemaphoreType.DMA((2,)),
        ],
    )
    return pl.pallas_call(
        kern,
        grid_spec=grid_spec,
        out_shape=jax.ShapeDtypeStruct((T, D), F32),
        compiler_params=_cparams(("arbitrary",), 40),
        name="combine",
    )(dest, yb, h2, gate)


EXPERT_ROWS = 256


def kernel(x, mem, positions, g_attn, w_in, q_norm_g, k_norm_g, lambda_q1, lambda_k1, lambda_q2, lambda_k2, diff_subln_g, gla_w_a2, gla_b_a, gla_out_g, w_out, g_cross, g_mem, w_cq, w_ckv, cq_norm_g, ck_norm_g, w_co, g_ffn, w_router_grp, b_router_grp, w_router_exp, b_router_exp, w_gate, w_up, w_down):
    B, S, D = x.shape
    T = B * S
    n_mem = mem.shape[1]
    l = 0
    x2 = x.reshape(T, D)

    half = DIFF_QKDIM // 2
    freq = ROPE_THETA ** (-jnp.arange(half, dtype=F32) / half)
    freq = jnp.tile(freq, LANES // half)[None, :]
    q_scale = math.log2(math.e) * DIFF_QKDIM ** -0.5
    qkg = jnp.stack([jnp.tile(q_norm_g[l], 2) * q_scale, jnp.tile(k_norm_g[l], 2)])
    score_bound = 1.01 * DIFF_QKDIM * q_scale * jnp.max(jnp.abs(q_norm_g[l])) * jnp.max(jnp.abs(k_norm_g[l]))
    lvec = jnp.stack([lambda_q1[l], lambda_k1[l], lambda_q2[l], lambda_k2[l]])

    qk, mid, log_a, sgr = _inproj(x2, g_attn[l][None], positions.reshape(T, 1), freq, qkg, w_in[l].T,
                                  gla_w_a2[l], gla_b_a[l][None], tm=1024)
    diffattn = functools.partial(_diffattn, lvec, qk, mid, diff_subln_g[l][None], B=B, S=S, tq=512)
    mix_d = lax.cond(score_bound <= SCORE_BOUND,
                     functools.partial(diffattn, bounded=True), functools.partial(diffattn, bounded=False))
    mix_g = _gla(mid, log_a, sgr, gla_out_g[l][None], B=B, S=S, blk=512)
    h1 = _outproj(mix_d, mix_g, w_out[l], x2, tm=1024, tn=512)

    hdim = D // CROSS_HEADS
    qc = _normproj(h1, g_cross[l][None], w_cq[l], cq_norm_g[l][None] * (hdim ** -0.5),
                   tm=1024, tn=hdim, n_norm=CROSS_HEADS, name="cq")
    kv = _normproj(mem.reshape(B * n_mem, D), g_mem[l][None], w_ckv[l], ck_norm_g[l][None],
                   tm=B * n_mem, tn=hdim, n_norm=CROSS_HEADS, name="ckv")
    h2 = _cross(qc, kv, w_co[l], h1, S=S, n_mem=n_mem, tm=512, tn=512)

    w_rt = jnp.concatenate([w_router_grp[l].T, jnp.zeros((SUBLANES - N_GROUPS, D), F32), w_router_exp[l].T])
    b_r = jnp.concatenate([b_router_grp[l], jnp.zeros((SUBLANES - N_GROUPS,), F32), b_router_exp[l]])[:, None]
    eid, gate, xn = _router(h2, g_ffn[l][None], w_rt, b_r, tm=512)
    rank, cnt = _rank(eid, tm=512)

    tb = EXPERT_ROWS
    n_blocks = (T * TOP_K + N_EXPERTS * (tb - 1) + tb - 1) // tb
    counts = cnt[:, 0]
    pcounts = ((counts + tb - 1) // tb) * tb
    pends = jnp.cumsum(pcounts)
    pstarts = pends - pcounts
    nb = (pends[-1:] // tb).astype(I32)
    eids = jnp.arange(N_EXPERTS, dtype=I32)
    blk_start = jnp.arange(n_blocks, dtype=I32) * tb
    blk_e = jnp.minimum(jnp.sum(pends[None, :] <= blk_start[:, None], axis=1), N_EXPERTS - 1).astype(I32)
    blk_first = jnp.concatenate([jnp.ones((1,), I32), (blk_e[1:] != blk_e[:-1]).astype(I32)])
    used = jnp.where(counts > 0, eids, N_EXPERTS)
    next_used = jnp.concatenate([lax.cummin(used[::-1])[::-1][1:], jnp.full((1,), N_EXPERTS, I32)])
    next_used = jnp.where(next_used < N_EXPERTS, next_used, -1)
    blk_next = jnp.sum(jnp.where(blk_e[:, None] == eids, next_used, 0), axis=1).astype(I32)
    pick = eid[:TOP_K]
    pstart_of = jnp.sum(jnp.where(pick[..., None] == eids, pstarts, 0), axis=-1)
    dest = (pstart_of + rank[:TOP_K]).astype(I32).reshape(-1)
    pad0 = (pstarts + counts).astype(I32)
    npad = (pcounts - counts).astype(I32)

    xb = _dispatch(dest, pad0, npad, nb, xn, tb=tb, n_blocks=n_blocks, tm=512)
    yb = _experts(xb, blk_e, blk_first, blk_next, nb, w_gate[l], w_up[l], w_down[l], tb=tb, n_blocks=n_blocks)
    out = _combine(dest, yb, h2, gate[:TOP_K].T, tm=256)
    return out.reshape(B, S, D)
```

```python
import functools
import math

import jax
import jax.numpy as jnp
from jax import lax
from jax.experimental import pallas as pl
from jax.experimental.pallas import tpu as pltpu

F32 = jnp.float32
BF16 = jnp.bfloat16
I32 = jnp.int32

LANES = 128
SUBLANES = 8

CHUNK = 64
ROPE_THETA = 10000.0
NORM_EPS = 1e-6
NEG_INF = -1e30
DIFF_HEADS = 8
DIFF_VDIM = 128
DIFF_QKDIM = 64
GLA_HEADS = 4
GLA_VDIM = 256
GLA_KDIM = 128
GLA_GATE_RANK = 16
GLA_TAU = 16.0
CROSS_HEADS = 4
N_GROUPS = 4
EXPERTS_PER_GROUP = 8
N_EXPERTS = N_GROUPS * EXPERTS_PER_GROUP
TOP_K = 2
LAM_INIT = 0.8 - 0.6 * math.exp(-0.3 * 0)

NT_DIMS = (((1,), (1,)), ((), ()))


def _cparams(semantics, vmem_mib):
    return pltpu.CompilerParams(dimension_semantics=semantics,
                                vmem_limit_bytes=vmem_mib * 1024 * 1024)


def _dot(a, b):
    return jnp.dot(a, b, preferred_element_type=F32)


def _dot_nt(a, b):
    return lax.dot_general(a, b, NT_DIMS, preferred_element_type=F32)


def _rms(x, g):
    ms = jnp.mean(x * x, axis=-1, keepdims=True)
    return x * lax.rsqrt(ms + NORM_EPS) * g


def _split_bf16(x):
    hi = x.astype(BF16)
    lo = (x - hi.astype(F32)).astype(BF16)
    return hi, lo


TN = 512
J_QK = 4
J_MID = 6
J_LR = J_QK + J_MID
J_GR = J_LR + 1
N_J = J_GR + 2


def _inproj_kernel(x_ref, g_ref, pos_ref, freq_ref, qkg_ref, w_ref, wgr_ref, wlr_ref, wa2_ref, ba_ref,
                   qk_ref, mid_ref, loga_ref, sgr_ref, n_scr, cos_scr, sin_scr, y_scr):
    j = pl.program_id(1)

    @pl.when(j == 0)
    def _():
        n_scr[...] = _rms(x_ref[...], g_ref[...]).astype(BF16)
        ang = pos_ref[...].astype(F32) * freq_ref[...]
        cos_scr[...] = jnp.cos(ang)
        sin_scr[...] = jnp.sin(ang)

    def qk_epilogue(jq):
        y_prev = y_scr.at[jq % 2]
        lane = lax.broadcasted_iota(I32, (1, LANES), 1)
        low_seg = lane < DIFF_QKDIM
        first_half = (lane % DIFF_QKDIM) < (DIFF_QKDIM // 2)
        gain = qkg_ref[jq // (J_QK // 2):jq // (J_QK // 2) + 1, :]
        cos = cos_scr[...]
        sin = sin_scr[...]
        for c in range(TN // LANES):
            yb = y_prev[:, c * LANES:(c + 1) * LANES]
            y2 = yb * yb
            s_lo = jnp.sum(jnp.where(low_seg, y2, 0.0), axis=-1, keepdims=True)
            s_hi = jnp.sum(jnp.where(low_seg, 0.0, y2), axis=-1, keepdims=True)
            ms = jnp.where(low_seg, s_lo, s_hi) * (1.0 / DIFF_QKDIM)
            yn = yb * lax.rsqrt(ms + NORM_EPS) * gain
            rot = jnp.where(first_half,
                            -pltpu.roll(yn, LANES - DIFF_QKDIM // 2, 1),
                            pltpu.roll(yn, DIFF_QKDIM // 2, 1))
            qk_ref[:, c * LANES:(c + 1) * LANES] = (yn * cos + rot * sin).astype(BF16)

    for jq in range(J_QK + 1):
        @pl.when(j == jq)
        def _():
            y = _dot_nt(n_scr[...], w_ref[...].astype(BF16))
            if jq > 0:
                qk_epilogue(jq - 1)
            if jq < J_QK:
                y_scr[jq % 2] = y
            else:
                mid_ref[...] = y.astype(BF16)

    @pl.when((j > J_QK) & (j < J_LR))
    def _():
        mid_ref[...] = _dot_nt(n_scr[...], w_ref[...].astype(BF16)).astype(BF16)

    @pl.when(j == J_LR)
    def _():
        lr = _dot_nt(n_scr[...], wlr_ref[...].astype(BF16))
        z = _dot(lr.astype(BF16), wa2_ref[...].astype(BF16)) + ba_ref[...]
        log_sig = jnp.minimum(z, 0.0) - jnp.log(1.0 + jnp.exp(-jnp.abs(z)))
        loga_ref[...] = log_sig * (1.0 / GLA_TAU)

    @pl.when(j >= J_GR)
    def _():
        y = _dot_nt(n_scr[...], wgr_ref[...].astype(BF16))
        sgr_ref[...] = (y / (1.0 + jnp.exp(-y))).astype(BF16)


def _inproj(x2, g_attn, pos2, freq, qkg, w_t, w_a2, b_a, *, tm):
    T, D = x2.shape
    n_mid = J_MID * TN
    n_gk = GLA_HEADS * GLA_KDIM
    lr0 = J_LR * TN
    gr0 = lr0 + GLA_GATE_RANK
    n_gr = w_t.shape[0] - gr0
    assert n_gr == 2 * TN and lr0 % GLA_GATE_RANK == 0
    return pl.pallas_call(
        _inproj_kernel,
        grid=(T // tm, N_J),
        in_specs=[
            pl.BlockSpec((tm, D), lambda i, j: (i, 0)),
            pl.BlockSpec((1, D), lambda i, j: (0, 0)),
            pl.BlockSpec((tm, 1), lambda i, j: (i, 0)),
            pl.BlockSpec((1, LANES), lambda i, j: (0, 0)),
            pl.BlockSpec((2, LANES), lambda i, j: (0, 0)),
            pl.BlockSpec((TN, D), lambda i, j: (jnp.minimum(j, J_LR - 1), 0)),
            pl.BlockSpec((pl.Element(TN), pl.Element(D)),
                         lambda i, j: (pl.multiple_of(gr0 + TN * jnp.clip(j - J_GR, 0, 1), SUBLANES), 0)),
            pl.BlockSpec((GLA_GATE_RANK, D), lambda i, j: (lr0 // GLA_GATE_RANK, 0)),
            pl.BlockSpec((GLA_GATE_RANK, n_gk), lambda i, j: (0, 0)),
            pl.BlockSpec((1, n_gk), lambda i, j: (0, 0)),
        ],
        out_specs=[
            pl.BlockSpec((tm, TN), lambda i, j: (i, jnp.clip(j - 1, 0, J_QK - 1))),
            pl.BlockSpec((tm, TN), lambda i, j: (i, jnp.clip(j - J_QK, 0, J_MID - 1))),
            pl.BlockSpec((tm, n_gk), lambda i, j: (i, 0)),
            pl.BlockSpec((tm, TN), lambda i, j: (i, jnp.clip(j - J_GR, 0, 1))),
        ],
        out_shape=[
            jax.ShapeDtypeStruct((T, J_QK * TN), BF16),
            jax.ShapeDtypeStruct((T, n_mid), BF16),
            jax.ShapeDtypeStruct((T, n_gk), F32),
            jax.ShapeDtypeStruct((T, n_gr), BF16),
        ],
        scratch_shapes=[
            pltpu.VMEM((tm, D), BF16),
            pltpu.VMEM((tm, LANES), F32),
            pltpu.VMEM((tm, LANES), F32),
            pltpu.VMEM((2, tm, TN), F32),
        ],
        compiler_params=_cparams(("parallel", "arbitrary"), 56),
        name="inproj",
    )(x2, g_attn, pos2, freq, qkg, w_t, w_t, w_t, w_a2, b_a)


SCORE_BOUND = 80.0


def _diffattn_kernel(lv_ref, q_ref, k_ref, v_ref, sg_ref, o_ref, vext, diag_mask, acc1, acc2, m1, m2,
                     *, tq, bounded):
    i = pl.program_id(2)

    @pl.when(i == 0)
    def _():
        S = v_ref.shape[0]
        lane_s = lax.broadcasted_iota(I32, (S, LANES), 1)
        vext[:, 0:DIFF_VDIM] = v_ref[...]
        vext[:, DIFF_VDIM:] = jnp.where(lane_s == 0, 1.0, 0.0).astype(BF16)
        row_chunk = lax.broadcasted_iota(I32, (tq, tq), 0) // CHUNK
        col_chunk = lax.broadcasted_iota(I32, (tq, tq), 1) // CHUNK
        diag_mask[...] = jnp.where(col_chunk <= row_chunk, 1.0, 0.0).astype(BF16)

    q = q_ref[...]
    lane = lax.broadcasted_iota(I32, (1, LANES), 1)
    zero = jnp.zeros_like(q)
    q1 = jnp.where(lane < DIFF_QKDIM, q, zero)
    q2 = jnp.where(lane < DIFF_QKDIM, zero, q)

    half = tq // 2
    comps = ((q1, acc1, m1), (q2, acc2, m2))

    def step(j):
        start = pl.multiple_of(j * tq, tq)
        k = k_ref[pl.ds(start, tq), :]
        v = vext[pl.ds(start, tq), :]
        for qc, acc, m in comps:
            s = _dot_nt(qc, k)
            if bounded:
                acc[...] += _dot(jnp.exp2(s).astype(BF16), v)
            else:
                m_old = m[...]
                m_new = jnp.maximum(m_old, jnp.max(s, axis=-1, keepdims=True))
                p = jnp.exp2(s - m_new)
                acc[...] = jnp.exp2(m_old - m_new) * acc[...] + _dot(p.astype(BF16), v)
                m[...] = m_new

    def diag_step():
        start = pl.multiple_of(i * tq, tq)
        for qc, acc, m in comps:
            for rows, n_keys in ((slice(0, half), half), (slice(half, tq), tq)):
                k = k_ref[pl.ds(start, n_keys), :]
                v = vext[pl.ds(start, n_keys), :]
                mask = diag_mask[rows, 0:n_keys]
                s = _dot_nt(qc[rows], k)
                if bounded:
                    acc[rows, :] = _dot(jnp.exp2(s).astype(BF16) * mask, v)
                else:
                    s = jnp.where(mask > 0, s, NEG_INF)
                    m_new = jnp.max(s, axis=-1, keepdims=True)
                    acc[rows, :] = _dot(jnp.exp2(s - m_new).astype(BF16), v)
                    m[rows, :] = m_new

    @pl.when(i % 2 == 1)
    def _():
        diag_step()
        step(i - 1)

    @pl.when(i % 2 == 0)
    def _():
        diag_step()

    def body(jj, carry):
        step(2 * jj)
        step(2 * jj + 1)
        return carry

    lax.fori_loop(0, i // 2, body, 0)

    lv = lv_ref[...]
    lam = (jnp.exp(jnp.sum(lv[0:1] * lv[1:2], axis=-1, keepdims=True))
           - jnp.exp(jnp.sum(lv[2:3] * lv[3:4], axis=-1, keepdims=True)) + LAM_INIT)
    a1 = acc1[...]
    a2 = acc2[...]
    o = (a1[:, :DIFF_VDIM] / a1[:, DIFF_VDIM:DIFF_VDIM + 1]
         - lam * (a2[:, :DIFF_VDIM] / a2[:, DIFF_VDIM:DIFF_VDIM + 1]))
    o_ref[...] = (_rms(o, sg_ref[...]) * (1.0 - LAM_INIT)).astype(BF16)


def _diffattn(lvec, qk, mid, subln_g, *, B, S, tq, bounded):
    T = B * S
    nq = S // tq
    kern = functools.partial(_diffattn_kernel, tq=tq, bounded=bounded)
    return pl.pallas_call(
        kern,
        grid=(B, DIFF_HEADS, nq),
        in_specs=[
            pl.BlockSpec((4, DIFF_QKDIM), lambda b, h, i: (0, 0)),
            pl.BlockSpec((tq, LANES), lambda b, h, i: (b * nq + i, h)),
            pl.BlockSpec((S, LANES), lambda b, h, i: (b, DIFF_HEADS + h)),
            pl.BlockSpec((S, LANES), lambda b, h, i: (b, h)),
            pl.BlockSpec((1, DIFF_VDIM), lambda b, h, i: (0, 0)),
        ],
        out_specs=pl.BlockSpec((tq, DIFF_VDIM), lambda b, h, i: (b * nq + i, h)),
        out_shape=jax.ShapeDtypeStruct((T, DIFF_HEADS * DIFF_VDIM), BF16),
        scratch_shapes=[
            pltpu.VMEM((S, 2 * DIFF_VDIM), BF16),
            pltpu.VMEM((tq, tq), BF16),
            pltpu.VMEM((tq, 2 * DIFF_VDIM), F32),
            pltpu.VMEM((tq, 2 * DIFF_VDIM), F32),
            pltpu.VMEM((tq, 1), F32),
            pltpu.VMEM((tq, 1), F32),
        ],
        compiler_params=_cparams(("parallel", "parallel", "arbitrary"), 32),
        name="diffattn_bounded" if bounded else "diffattn_online",
    )(lvec, qk, qk, mid, subln_g)


def _gla_kernel(q_ref, k_ref, v_ref, la_ref, sgr_ref, g_ref, tri_ref, ones_ref, o_ref, state, *, blk):
    @pl.when(pl.program_id(2) == 0)
    def _():
        state[...] = jnp.zeros(state.shape, F32)

    la_t = la_ref[...].T
    k_t = k_ref[...].astype(F32).T
    hi, lo = _split_bf16(la_t)
    tri = tri_ref[...]
    ones = ones_ref[...]
    cum_t = _dot(hi, tri) + _dot(lo, tri)
    tot_t = _dot(hi, ones) + _dot(lo, ones)
    kd_t = k_t * jnp.exp(tot_t - cum_t)

    n_chunks = blk // CHUNK
    lane = lax.broadcasted_iota(I32, (1, LANES), 1)
    d_states = []
    for ck in range(n_chunks):
        pair = slice((ck // 2) * LANES, (ck // 2 + 1) * LANES)
        in_chunk = (lane // CHUNK) == (ck % 2)
        kd = jnp.where(in_chunk, kd_t[:, pair], 0.0).astype(BF16)
        d_states.append(_dot(kd, v_ref[pair, :]))

    st = state[...]
    states = []
    for ck in range(n_chunks):
        decay = jnp.exp(tot_t[:, ck * CHUNK:ck * CHUNK + 1])
        st = decay * st + d_states[ck]
        states.append(st.astype(BF16))
    state[...] = st

    o = jnp.concatenate([_dot(q_ref[ck * CHUNK:(ck + 1) * CHUNK, :], states[ck]) for ck in range(n_chunks)],
                        axis=0) * (GLA_KDIM ** -0.5)
    o_ref[...] = (_rms(o, g_ref[...]) * sgr_ref[...].astype(F32)).astype(BF16)


def _gla(mid, log_a, sgr, out_g, *, B, S, blk):
    T = B * S
    ns = S // blk
    kern = functools.partial(_gla_kernel, blk=blk)
    q_col0 = (DIFF_HEADS * DIFF_VDIM) // GLA_KDIM
    k_col0 = q_col0 + GLA_HEADS
    v_col0 = (DIFF_HEADS * DIFF_VDIM + 2 * GLA_HEADS * GLA_KDIM) // GLA_VDIM
    r = jnp.arange(blk, dtype=I32)[:, None]
    c = jnp.arange(blk, dtype=I32)[None, :]
    same = (r // CHUNK) == (c // CHUNK)
    tri = (same & (r <= c)).astype(BF16)
    ones = same.astype(BF16)
    return pl.pallas_call(
        kern,
        grid=(B, GLA_HEADS, ns),
        in_specs=[
            pl.BlockSpec((blk, GLA_KDIM), lambda b, h, s: (b * ns + s, q_col0 + h)),
            pl.BlockSpec((blk, GLA_KDIM), lambda b, h, s: (b * ns + s, k_col0 + h)),
            pl.BlockSpec((blk, GLA_VDIM), lambda b, h, s: (b * ns + s, v_col0 + h)),
            pl.BlockSpec((blk, GLA_KDIM), lambda b, h, s: (b * ns + s, h)),
            pl.BlockSpec((blk, GLA_VDIM), lambda b, h, s: (b * ns + s, h)),
            pl.BlockSpec((1, GLA_VDIM), lambda b, h, s: (0, 0)),
            pl.BlockSpec((blk, blk), lambda b, h, s: (0, 0)),
            pl.BlockSpec((blk, blk), lambda b, h, s: (0, 0)),
        ],
        out_specs=pl.BlockSpec((blk, GLA_VDIM), lambda b, h, s: (b * ns + s, h)),
        out_shape=jax.ShapeDtypeStruct((T, GLA_HEADS * GLA_VDIM), BF16),
        scratch_shapes=[pltpu.VMEM((GLA_KDIM, GLA_VDIM), F32)],
        compiler_params=_cparams(("parallel", "parallel", "arbitrary"), 32),
        name="gla",
    )(mid, mid, mid, log_a, sgr, out_g, tri, ones)


def _resident_w_map(n_j):
    return lambda i, j: (0, jnp.where(i == 0, j, n_j - 1))


def _outproj_kernel(a_ref, b_ref, wa_ref, wb_ref, x_ref, o_ref, w_scr):
    j = pl.program_id(1)
    ka = a_ref.shape[1]

    @pl.when(pl.program_id(0) == 0)
    def _():
        w_scr[j, 0:ka, :] = wa_ref[...].astype(BF16)
        w_scr[j, ka:, :] = wb_ref[...].astype(BF16)

    acc = _dot(a_ref[...], w_scr[j, 0:ka, :]) + _dot(b_ref[...], w_scr[j, ka:, :])
    o_ref[...] = x_ref[...] + acc


def _outproj(a, b, w_out, x2, *, tm, tn):
    T, ka = a.shape
    kb = b.shape[1]
    assert ka == kb
    D = w_out.shape[1]
    n_j = D // tn
    return pl.pallas_call(
        _outproj_kernel,
        grid=(T // tm, n_j),
        in_specs=[
            pl.BlockSpec((tm, ka), lambda i, j: (i, 0)),
            pl.BlockSpec((tm, kb), lambda i, j: (i, 0)),
            pl.BlockSpec((ka, tn), _resident_w_map(n_j)),
            pl.BlockSpec((kb, tn), lambda i, j: (1, jnp.where(i == 0, j, n_j - 1))),
            pl.BlockSpec((tm, tn), lambda i, j: (i, j)),
        ],
        out_specs=pl.BlockSpec((tm, tn), lambda i, j: (i, j)),
        out_shape=jax.ShapeDtypeStruct((T, D), F32),
        scratch_shapes=[pltpu.VMEM((n_j, ka + kb, tn), BF16)],
        compiler_params=_cparams(("arbitrary", "arbitrary"), 48),
        name="outproj",
    )(a, b, w_out, w_out, x2)


def _normproj_kernel(x_ref, g_ref, w_ref, hg_ref, o_ref, n_scr, *w_scr, n_norm):
    j = pl.program_id(1)

    @pl.when(j == 0)
    def _():
        n_scr[...] = _rms(x_ref[...], g_ref[...]).astype(BF16)

    if w_scr:
        @pl.when(pl.program_id(0) == 0)
        def _():
            w_scr[0][j] = w_ref[...].astype(BF16)
        y = _dot(n_scr[...], w_scr[0][j])
    else:
        y = _dot(n_scr[...], w_ref[...].astype(BF16))

    @pl.when(j < n_norm)
    def _():
        o_ref[...] = _rms(y, hg_ref[...]).astype(BF16)

    @pl.when(j >= n_norm)
    def _():
        o_ref[...] = y.astype(BF16)


def _normproj(x2, g, w, head_g, *, tm, tn, n_norm, name):
    T, D = x2.shape
    N = w.shape[1]
    kern = functools.partial(_normproj_kernel, n_norm=n_norm)
    n_j = N // tn
    resident = T // tm > 1
    return pl.pallas_call(
        kern,
        grid=(T // tm, n_j),
        in_specs=[
            pl.BlockSpec((tm, D), lambda i, j: (i, 0)),
            pl.BlockSpec((1, D), lambda i, j: (0, 0)),
            pl.BlockSpec((D, tn), _resident_w_map(n_j) if resident else (lambda i, j: (0, j))),
            pl.BlockSpec((1, tn), lambda i, j: (0, 0)),
        ],
        out_specs=pl.BlockSpec((tm, tn), lambda i, j: (i, j)),
        out_shape=jax.ShapeDtypeStruct((T, N), BF16),
        scratch_shapes=[pltpu.VMEM((tm, D), BF16)] + ([pltpu.VMEM((n_j, D, tn), BF16)] if resident else []),
        compiler_params=_cparams(("arbitrary", "arbitrary"), 48),
        name=name,
    )(x2, g, w, head_g)


def _cross_kernel(q_ref, k_ref, v_ref, w_ref, h_ref, o_ref, att_scr, w_scr, *, hdim):
    j = pl.program_id(1)

    @pl.when(pl.program_id(0) == 0)
    def _():
        w_scr[j] = w_ref[...].astype(BF16)

    @pl.when(j == 0)
    def _():
        for hd in range(CROSS_HEADS):
            cols = slice(hd * hdim, (hd + 1) * hdim)
            s = lax.dot_general(q_ref[:, cols], k_ref[:, cols], NT_DIMS, preferred_element_type=F32)
            p = jnp.exp(s - jnp.max(s, axis=-1, keepdims=True))
            l = jnp.sum(p, axis=-1, keepdims=True)
            att_scr[:, cols] = (_dot(p.astype(BF16), v_ref[:, cols]) / l).astype(BF16)

    o_ref[...] = h_ref[...] + _dot(att_scr[...], w_scr[j])


def _cross(qc, kv, w_co, h1, *, S, n_mem, tm, tn):
    T, D = qc.shape
    per_b = S // tm
    kern = functools.partial(_cross_kernel, hdim=D // CROSS_HEADS)
    n_j = D // tn
    return pl.pallas_call(
        kern,
        grid=(T // tm, n_j),
        in_specs=[
            pl.BlockSpec((tm, D), lambda i, j: (i, 0)),
            pl.BlockSpec((n_mem, D), lambda i, j: (i // per_b, 0)),
            pl.BlockSpec((n_mem, D), lambda i, j: (i // per_b, 1)),
            pl.BlockSpec((D, tn), _resident_w_map(n_j)),
            pl.BlockSpec((tm, tn), lambda i, j: (i, j)),
        ],
        out_specs=pl.BlockSpec((tm, tn), lambda i, j: (i, j)),
        out_shape=jax.ShapeDtypeStruct((T, D), F32),
        scratch_shapes=[pltpu.VMEM((tm, D), BF16), pltpu.VMEM((n_j, D, tn), BF16)],
        compiler_params=_cparams(("arbitrary", "arbitrary"), 48),
        name="cross",
    )(qc, kv, kv, w_co, h1)


R_ROWS = SUBLANES + N_EXPERTS


def _pack_bf16_pairs(xb16):
    c = xb16.shape[1] // 2
    u = lax.bitcast_convert_type(xb16.astype(F32), jnp.uint32)
    return (u[:, :c] >> 16) | (u[:, c:] & jnp.uint32(0xFFFF0000))


def _unpack_bf16_pairs(w):
    lo = lax.bitcast_convert_type(w << 16, F32).astype(BF16)
    hi = lax.bitcast_convert_type(w & jnp.uint32(0xFFFF0000), F32).astype(BF16)
    return lo, hi


def _router_kernel(h_ref, g_ref, wt_ref, b_ref, eid_ref, gate_ref, xn_ref):
    n = _rms(h_ref[...], g_ref[...])
    nh, nl = _split_bf16(n)
    xn_ref[...] = _pack_bf16_pairs(nh)
    wh, wl = _split_bf16(wt_ref[...])
    nt = functools.partial(lax.dot_general, dimension_numbers=NT_DIMS, preferred_element_type=F32)
    lg = nt(wh, nh) + nt(wh, nl) + nt(wl, nh) + b_ref[...]

    tm = lg.shape[1]
    row = lax.broadcasted_iota(I32, (SUBLANES, tm), 0)

    def first_argmax(v, vmax):
        return jnp.min(jnp.where(v == vmax, row, SUBLANES), axis=0, keepdims=True)

    gl = jnp.where(row < N_GROUPS, lg[0:SUBLANES], NEG_INF)
    gmax = jnp.max(gl, axis=0, keepdims=True)
    grp = first_argmax(gl, gmax)
    grp_w = 1.0 / jnp.sum(jnp.exp(gl - gmax), axis=0, keepdims=True)

    sel = jnp.zeros((SUBLANES, tm), F32)
    for gi in range(N_GROUPS):
        lo = SUBLANES + gi * EXPERTS_PER_GROUP
        sel = jnp.where(grp == gi, lg[lo:lo + EXPERTS_PER_GROUP], sel)
    e = jnp.exp(sel - jnp.max(sel, axis=0, keepdims=True))
    prob = e / jnp.sum(e, axis=0, keepdims=True)
    p1 = jnp.max(prob, axis=0, keepdims=True)
    i1 = first_argmax(prob, p1)
    rest = jnp.where(row == i1, -1.0, prob)
    p2 = jnp.max(rest, axis=0, keepdims=True)
    i2 = first_argmax(rest, p2)
    den = p1 + p2
    base = grp * EXPERTS_PER_GROUP
    eid_ref[...] = jnp.where(row == 0, base + i1, jnp.where(row == 1, base + i2, 0))
    gate_ref[...] = jnp.where(row == 0, grp_w * p1 / den, jnp.where(row == 1, grp_w * p2 / den, 0.0))


def _router(h2, g_ffn, w_rt, b_r, *, tm):
    T, D = h2.shape
    return pl.pallas_call(
        _router_kernel,
        grid=(T // tm,),
        in_specs=[
            pl.BlockSpec((tm, D), lambda i: (i, 0)),
            pl.BlockSpec((1, D), lambda i: (0, 0)),
            pl.BlockSpec((R_ROWS, D), lambda i: (0, 0)),
            pl.BlockSpec((R_ROWS, 1), lambda i: (0, 0)),
        ],
        out_specs=[
            pl.BlockSpec((SUBLANES, tm), lambda i: (0, i)),
            pl.BlockSpec((SUBLANES, tm), lambda i: (0, i)),
            pl.BlockSpec((tm, D // 2), lambda i: (i, 0)),
        ],
        out_shape=[
            jax.ShapeDtypeStruct((SUBLANES, T), I32),
            jax.ShapeDtypeStruct((SUBLANES, T), F32),
            jax.ShapeDtypeStruct((T, D // 2), jnp.uint32),
        ],
        compiler_params=_cparams(("parallel",), 32),
        name="router",
    )(h2, g_ffn, w_rt, b_r)


def _rank_kernel(eid_ref, rank_ref, cnt_ref, carry):
    @pl.when(pl.program_id(0) == 0)
    def _():
        carry[...] = jnp.zeros(carry.shape, F32)

    tm = eid_ref.shape[1]
    e0 = eid_ref[0:1, :]
    e1 = eid_ref[1:2, :]
    erow = lax.broadcasted_iota(I32, (N_EXPERTS, tm), 0)
    hit = jnp.where((erow == e0) | (erow == e1), 1.0, 0.0)
    r = lax.broadcasted_iota(I32, (tm, tm), 0)
    c = lax.broadcasted_iota(I32, (tm, tm), 1)
    before = jnp.where(r < c, 1.0, 0.0).astype(BF16)
    pre = _dot(hit.astype(BF16), before) + carry[:, 0:1]
    rank0 = jnp.sum(jnp.where(erow == e0, pre, 0.0), axis=0, keepdims=True)
    rank1 = jnp.sum(jnp.where(erow == e1, pre, 0.0), axis=0, keepdims=True)
    row = lax.broadcasted_iota(I32, (SUBLANES, tm), 0)
    rank_ref[...] = jnp.where(row == 0, rank0, jnp.where(row == 1, rank1, 0.0)).astype(I32)
    total = carry[...] + jnp.sum(hit, axis=1, keepdims=True)
    carry[...] = total
    cnt_ref[...] = total.astype(I32)


def _rank(eid, *, tm):
    T = eid.shape[1]
    return pl.pallas_call(
        _rank_kernel,
        grid=(T // tm,),
        in_specs=[pl.BlockSpec((SUBLANES, tm), lambda i: (0, i))],
        out_specs=[
            pl.BlockSpec((SUBLANES, tm), lambda i: (0, i)),
            pl.BlockSpec((N_EXPERTS, LANES), lambda i: (0, 0)),
        ],
        out_shape=[
            jax.ShapeDtypeStruct((SUBLANES, T), I32),
            jax.ShapeDtypeStruct((N_EXPERTS, LANES), I32),
        ],
        scratch_shapes=[pltpu.VMEM((N_EXPERTS, LANES), F32)],
        compiler_params=_cparams(("arbitrary",), 32),
        name="rank",
    )(eid)


def _dispatch_kernel(dest_ref, pad0_ref, npad_ref, nb_ref, xn_ref, xb_hbm, zbuf, sem, psem,
                     *, n_tok, tb, n_blocks):
    i = pl.program_id(0)
    tm = xn_ref.shape[0]
    base = i * tm

    def row_body(r, carry):
        for k in range(TOP_K):
            pltpu.make_async_copy(xn_ref.at[pl.ds(r, 1), :],
                                  xb_hbm.at[pl.ds(dest_ref[k * n_tok + base + r], 1), :], sem).start()
        return carry
    lax.fori_loop(0, tm, row_body, 0, unroll=8)

    @pl.when(i == 0)
    def _():
        _dispatch_fill(pad0_ref, npad_ref, nb_ref, xb_hbm, zbuf, psem, tb=tb, n_blocks=n_blocks)

    for k in range(TOP_K):
        pltpu.make_async_copy(xn_ref, xb_hbm.at[pl.ds(0, tm), :], sem).wait()


def _dispatch_fill(pad0_ref, npad_ref, nb_ref, xb_hbm, zbuf, psem, *, tb, n_blocks):
    zbuf[...] = jnp.zeros(zbuf.shape, zbuf.dtype)

    def pad_copy(slot):
        return pltpu.make_async_copy(zbuf.at[pl.ds(0, 1), :], xb_hbm.at[pl.ds(slot, 1), :], psem.at[0])

    def tail_copy(blk):
        return pltpu.make_async_copy(zbuf, xb_hbm.at[pl.ds(pl.multiple_of(blk * tb, tb), tb), :], psem.at[1])

    def for_each_pad(fn):
        def expert_body(e, carry):
            def body(r, c):
                fn(pad0_ref[e] + r)
                return c
            lax.fori_loop(0, npad_ref[e], body, 0)
            return carry
        lax.fori_loop(0, N_EXPERTS, expert_body, 0)

    def for_each_tail(fn):
        def body(b, c):
            fn(b)
            return c
        lax.fori_loop(nb_ref[0], n_blocks, body, 0)

    for_each_pad(lambda slot: pad_copy(slot).start())
    for_each_tail(lambda b: tail_copy(b).start())
    for_each_pad(lambda slot: pad_copy(slot).wait())
    for_each_tail(lambda b: tail_copy(b).wait())


def _dispatch(dest, pad0, npad, nb, xn, *, tb, n_blocks, tm):
    T, C = xn.shape
    kern = functools.partial(_dispatch_kernel, n_tok=T, tb=tb, n_blocks=n_blocks)
    grid_spec = pltpu.PrefetchScalarGridSpec(
        num_scalar_prefetch=4,
        grid=(T // tm,),
        in_specs=[pl.BlockSpec((tm, C), lambda i, *_: (i, 0))],
        out_specs=pl.BlockSpec(memory_space=pl.ANY),
        scratch_shapes=[
            pltpu.VMEM((tb, C), xn.dtype),
            pltpu.SemaphoreType.DMA(()),
            pltpu.SemaphoreType.DMA((2,)),
        ],
    )
    return pl.pallas_call(
        kern,
        grid_spec=grid_spec,
        out_shape=jax.ShapeDtypeStruct((n_blocks * tb, C), xn.dtype),
        compiler_params=_cparams(("arbitrary",), 32),
        name="dispatch",
    )(dest, pad0, npad, nb, xn)


def _expert_kernel(be_ref, first_ref, nxt_ref, nb_ref, x_ref, wg_hbm, wu_hbm, wd_hbm, y_ref,
                   wg_f, wu_f, wd_f, wsem, wg_b, wu_b, wd_b):
    i = pl.program_id(0)
    nb = nb_ref[0]

    def weight_copies(e):
        return (pltpu.make_async_copy(wg_hbm.at[e], wg_f, wsem.at[0]),
                pltpu.make_async_copy(wu_hbm.at[e], wu_f, wsem.at[1]),
                pltpu.make_async_copy(wd_hbm.at[e], wd_f, wsem.at[2]))

    @pl.when(i == 0)
    def _():
        for cp in weight_copies(be_ref[0]):
            cp.start()

    first = (i < nb) & (first_ref[i] == 1)

    @pl.when(first)
    def _():
        for cp in weight_copies(0):
            cp.wait()
        wg_b[...] = wg_f[...].astype(BF16)
        wu_b[...] = wu_f[...].astype(BF16)
        wd_b[...] = wd_f[...].astype(BF16)

    @pl.when(first & (nxt_ref[i] >= 0))
    def _():
        for cp in weight_copies(jnp.maximum(nxt_ref[i], 0)):
            cp.start()

    @pl.when(i < nb)
    def _():
        n_lo, n_hi = _unpack_bf16_pairs(x_ref[...])
        half = n_lo.shape[1]
        a = _dot(n_lo, wg_b[0:half, :]) + _dot(n_hi, wg_b[half:, :])
        u = _dot(n_lo, wu_b[0:half, :]) + _dot(n_hi, wu_b[half:, :])
        hdn = (a / (1.0 + jnp.exp(-a))) * u
        y_ref[...] = _dot(hdn.astype(BF16), wd_b[...])

    @pl.when(i >= nb)
    def _():
        y_ref[...] = jnp.zeros(y_ref.shape, F32)


def _experts(xb, blk_e, blk_first, blk_next, nb, w_gate, w_up, w_down, *, tb, n_blocks):
    D, De = w_gate.shape[1:]
    C = xb.shape[1]

    def x_map(i, be, first, nxt, nbr):
        return (jnp.minimum(i, nbr[0] - 1), 0)

    grid_spec = pltpu.PrefetchScalarGridSpec(
        num_scalar_prefetch=4,
        grid=(n_blocks,),
        in_specs=[
            pl.BlockSpec((tb, C), x_map),
            pl.BlockSpec(memory_space=pl.ANY),
            pl.BlockSpec(memory_space=pl.ANY),
            pl.BlockSpec(memory_space=pl.ANY),
        ],
        out_specs=pl.BlockSpec((tb, D), lambda i, *_: (i, 0)),
        scratch_shapes=[
            pltpu.VMEM((D, De), F32),
            pltpu.VMEM((D, De), F32),
            pltpu.VMEM((De, D), F32),
            pltpu.SemaphoreType.DMA((3,)),
            pltpu.VMEM((D, De), BF16),
            pltpu.VMEM((D, De), BF16),
            pltpu.VMEM((De, D), BF16),
        ],
    )
    return pl.pallas_call(
        _expert_kernel,
        grid_spec=grid_spec,
        out_shape=jax.ShapeDtypeStruct((n_blocks * tb, D), F32),
        compiler_params=_cparams(("arbitrary",), 48),
        name="experts",
    )(blk_e, blk_first, blk_next, nb, xb, w_gate, w_up, w_down)


def _combine_kernel(dest_ref, y_hbm, h_ref, gate_ref, o_ref, ybuf, sem, *, tm, n_tok):
    i = pl.program_id(0)
    n = pl.num_programs(0)

    def row_copy(d, k, r, slot):
        return pltpu.make_async_copy(y_hbm.at[pl.ds(d, 1), :], ybuf.at[slot, k, pl.ds(r, 1), :], sem.at[slot])

    def start_gather(blk, slot):
        def body(r, carry):
            for k in range(TOP_K):
                row_copy(dest_ref[k * n_tok + blk * tm + r], k, r, slot).start()
            return carry
        lax.fori_loop(0, tm, body, 0, unroll=8)

    def wait_gather(slot):
        for k in range(TOP_K):
            pltpu.make_async_copy(y_hbm.at[pl.ds(0, tm), :], ybuf.at[slot, k], sem.at[slot]).wait()

    @pl.when(i == 0)
    def _():
        start_gather(0, 0)

    @pl.when(i + 1 < n)
    def _():
        start_gather(i + 1, (i + 1) % 2)

    slot = i % 2
    wait_gather(slot)
    gt = gate_ref[...]
    o_ref[...] = h_ref[...] + gt[:, 0:1] * ybuf[slot, 0] + gt[:, 1:2] * ybuf[slot, 1]


def _combine(dest, yb, h2, gate, *, tm):
    T, D = h2.shape
    kern = functools.partial(_combine_kernel, tm=tm, n_tok=T)
    grid_spec = pltpu.PrefetchScalarGridSpec(
        num_scalar_prefetch=1,
        grid=(T // tm,),
        in_specs=[
            pl.BlockSpec(memory_space=pl.ANY),
            pl.BlockSpec((tm, D), lambda i, d: (i, 0)),
            pl.BlockSpec((tm, TOP_K), lambda i, d: (i, 0)),
        ],
        out_specs=pl.BlockSpec((tm, D), lambda i, d: (i, 0)),
        scratch_shapes=[
            pltpu.VMEM((2, TOP_K, tm, D), F32),
            pltpu.SemaphoreType.DMA((2,)),
        ],
    )
    return pl.pallas_call(
        kern,
        grid_spec=grid_spec,
        out_shape=jax.ShapeDtypeStruct((T, D), F32),
        compiler_params=_cparams(("arbitrary",), 40),
        name="combine",
    )(dest, yb, h2, gate)


EXPERT_ROWS = 256


def kernel(x, mem, positions, g_attn, w_in, q_norm_g, k_norm_g, lambda_q1, lambda_k1, lambda_q2, lambda_k2, diff_subln_g, gla_w_a2, gla_b_a, gla_out_g, w_out, g_cross, g_mem, w_cq, w_ckv, cq_norm_g, ck_norm_g, w_co, g_ffn, w_router_grp, b_router_grp, w_router_exp, b_router_exp, w_gate, w_up, w_down):
    B, S, D = x.shape
    T = B * S
    n_mem = mem.shape[1]
    l = 0
    x2 = x.reshape(T, D)

    half = DIFF_QKDIM // 2
    freq = ROPE_THETA ** (-jnp.arange(half, dtype=F32) / half)
    freq = jnp.tile(freq, LANES // half)[None, :]
    q_scale = math.log2(math.e) * DIFF_QKDIM ** -0.5
    qkg = jnp.stack([jnp.tile(q_norm_g[l], 2) * q_scale, jnp.tile(k_norm_g[l], 2)])
    score_bound = 1.01 * DIFF_QKDIM * q_scale * jnp.max(jnp.abs(q_norm_g[l])) * jnp.max(jnp.abs(k_norm_g[l]))
    lvec = jnp.stack([lambda_q1[l], lambda_k1[l], lambda_q2[l], lambda_k2[l]])

    qk, mid, log_a, sgr = _inproj(x2, g_attn[l][None], positions.reshape(T, 1), freq, qkg, w_in[l].T,
                                  gla_w_a2[l], gla_b_a[l][None], tm=1024)
    diffattn = functools.partial(_diffattn, lvec, qk, mid, diff_subln_g[l][None], B=B, S=S, tq=512)
    mix_d = lax.cond(score_bound <= SCORE_BOUND,
                     functools.partial(diffattn, bounded=True), functools.partial(diffattn, bounded=False))
    mix_g = _gla(mid, log_a, sgr, gla_out_g[l][None], B=B, S=S, blk=512)
    h1 = _outproj(mix_d, mix_g, w_out[l], x2, tm=1024, tn=512)

    hdim = D // CROSS_HEADS
    qc = _normproj(h1, g_cross[l][None], w_cq[l], cq_norm_g[l][None] * (hdim ** -0.5),
                   tm=1024, tn=hdim, n_norm=CROSS_HEADS, name="cq")
    kv = _normproj(mem.reshape(B * n_mem, D), g_mem[l][None], w_ckv[l], ck_norm_g[l][None],
                   tm=B * n_mem, tn=hdim, n_norm=CROSS_HEADS, name="ckv")
    h2 = _cross(qc, kv, w_co[l], h1, S=S, n_mem=n_mem, tm=512, tn=512)

    w_rt = jnp.concatenate([w_router_grp[l].T, jnp.zeros((SUBLANES - N_GROUPS, D), F32), w_router_exp[l].T])
    b_r = jnp.concatenate([b_router_grp[l], jnp.zeros((SUBLANES - N_GROUPS,), F32), b_router_exp[l]])[:, None]
    eid, gate, xn = _router(h2, g_ffn[l][None], w_rt, b_r, tm=512)
    rank, cnt = _rank(eid, tm=512)

    tb = EXPERT_ROWS
    n_blocks = (T * TOP_K + N_EXPERTS * (tb - 1) + tb - 1) // tb
    counts = cnt[:, 0]
    pcounts = ((counts + tb - 1) // tb) * tb
    pends = jnp.cumsum(pcounts)
    pstarts = pends - pcounts
    nb = (pends[-1:] // tb).astype(I32)
    eids = jnp.arange(N_EXPERTS, dtype=I32)
    blk_start = jnp.arange(n_blocks, dtype=I32) * tb
    blk_e = jnp.minimum(jnp.sum(pends[None, :] <= blk_start[:, None], axis=1), N_EXPERTS - 1).astype(I32)
    blk_first = jnp.concatenate([jnp.ones((1,), I32), (blk_e[1:] != blk_e[:-1]).astype(I32)])
    used = jnp.where(counts > 0, eids, N_EXPERTS)
    next_used = jnp.concatenate([lax.cummin(used[::-1])[::-1][1:], jnp.full((1,), N_EXPERTS, I32)])
    next_used = jnp.where(next_used < N_EXPERTS, next_used, -1)
    blk_next = jnp.sum(jnp.where(blk_e[:, None] == eids, next_used, 0), axis=1).astype(I32)
    pick = eid[:TOP_K]
    pstart_of = jnp.sum(jnp.where(pick[..., None] == eids, pstarts, 0), axis=-1)
    dest = (pstart_of + rank[:TOP_K]).astype(I32).reshape(-1)
    pad0 = (pstarts + counts).astype(I32)
    npad = (pcounts - counts).astype(I32)

    xb = _dispatch(dest, pad0, npad, nb, xn, tb=tb, n_blocks=n_blocks, tm=512)
    yb = _experts(xb, blk_e, blk_first, blk_next, nb, w_gate[l], w_up[l], w_down[l], tb=tb, n_blocks=n_blocks)
    out = _combine(dest, yb, h2, gate[:TOP_K].T, tm=256)
    return out.reshape(B, S, D)
```

```python
import functools
import math

import jax
import jax.numpy as jnp
from jax import lax
from jax.experimental import pallas as pl
from jax.experimental.pallas import tpu as pltpu

F32 = jnp.float32
BF16 = jnp.bfloat16
I32 = jnp.int32

LANES = 128
SUBLANES = 8

CHUNK = 64
ROPE_THETA = 10000.0
NORM_EPS = 1e-6
NEG_INF = -1e30
DIFF_HEADS = 8
DIFF_VDIM = 128
DIFF_QKDIM = 64
GLA_HEADS = 4
GLA_VDIM = 256
GLA_KDIM = 128
GLA_GATE_RANK = 16
GLA_TAU = 16.0
CROSS_HEADS = 4
N_GROUPS = 4
EXPERTS_PER_GROUP = 8
N_EXPERTS = N_GROUPS * EXPERTS_PER_GROUP
TOP_K = 2
LAM_INIT = 0.8 - 0.6 * math.exp(-0.3 * 0)

NT_DIMS = (((1,), (1,)), ((), ()))


def _cparams(semantics, vmem_mib):
    return pltpu.CompilerParams(dimension_semantics=semantics,
                                vmem_limit_bytes=vmem_mib * 1024 * 1024)


def _dot(a, b):
    return jnp.dot(a, b, preferred_element_type=F32)


def _dot_nt(a, b):
    return lax.dot_general(a, b, NT_DIMS, preferred_element_type=F32)


def _rms(x, g):
    ms = jnp.mean(x * x, axis=-1, keepdims=True)
    return x * lax.rsqrt(ms + NORM_EPS) * g


def _split_bf16(x):
    hi = x.astype(BF16)
    lo = (x - hi.astype(F32)).astype(BF16)
    return hi, lo


TN = 512
J_QK = 4
J_MID = 6
J_LR = J_QK + J_MID
J_GR = J_LR + 1
N_J = J_GR + 2


def _inproj_kernel(x_ref, g_ref, pos_ref, freq_ref, qkg_ref, w_ref, wgr_ref, wlr_ref, wa2_ref, ba_ref,
                   qk_ref, mid_ref, loga_ref, sgr_ref, n_scr, cos_scr, sin_scr, y_scr):
    j = pl.program_id(1)

    @pl.when(j == 0)
    def _():
        n_scr[...] = _rms(x_ref[...], g_ref[...]).astype(BF16)
        ang = pos_ref[...].astype(F32) * freq_ref[...]
        cos_scr[...] = jnp.cos(ang)
        sin_scr[...] = jnp.sin(ang)

    def qk_epilogue(jq):
        y_prev = y_scr.at[jq % 2]
        lane = lax.broadcasted_iota(I32, (1, LANES), 1)
        low_seg = lane < DIFF_QKDIM
        first_half = (lane % DIFF_QKDIM) < (DIFF_QKDIM // 2)
        gain = qkg_ref[jq // (J_QK // 2):jq // (J_QK // 2) + 1, :]
        cos = cos_scr[...]
        sin = sin_scr[...]
        for c in range(TN // LANES):
            yb = y_prev[:, c * LANES:(c + 1) * LANES]
            y2 = yb * yb
            s_lo = jnp.sum(jnp.where(low_seg, y2, 0.0), axis=-1, keepdims=True)
            s_hi = jnp.sum(jnp.where(low_seg, 0.0, y2), axis=-1, keepdims=True)
            ms = jnp.where(low_seg, s_lo, s_hi) * (1.0 / DIFF_QKDIM)
            yn = yb * lax.rsqrt(ms + NORM_EPS) * gain
            rot = jnp.where(first_half,
                            -pltpu.roll(yn, LANES - DIFF_QKDIM // 2, 1),
                            pltpu.roll(yn, DIFF_QKDIM // 2, 1))
            qk_ref[:, c * LANES:(c + 1) * LANES] = (yn * cos + rot * sin).astype(BF16)

    for jq in range(J_QK + 1):
        @pl.when(j == jq)
        def _():
            y = _dot_nt(n_scr[...], w_ref[...].astype(BF16))
            if jq > 0:
                qk_epilogue(jq - 1)
            if jq < J_QK:
                y_scr[jq % 2] = y
            else:
                mid_ref[...] = y.astype(BF16)

    @pl.when((j > J_QK) & (j < J_LR))
    def _():
        mid_ref[...] = _dot_nt(n_scr[...], w_ref[...].astype(BF16)).astype(BF16)

    @pl.when(j == J_LR)
    def _():
        lr = _dot_nt(n_scr[...], wlr_ref[...].astype(BF16))
        z = _dot(lr.astype(BF16), wa2_ref[...].astype(BF16)) + ba_ref[...]
        log_sig = jnp.minimum(z, 0.0) - jnp.log(1.0 + jnp.exp(-jnp.abs(z)))
        loga_ref[...] = log_sig * (1.0 / GLA_TAU)

    @pl.when(j >= J_GR)
    def _():
        y = _dot_nt(n_scr[...], wgr_ref[...].astype(BF16))
        sgr_ref[...] = (y / (1.0 + jnp.exp(-y))).astype(BF16)


def _inproj(x2, g_attn, pos2, freq, qkg, w_t, w_a2, b_a, *, tm):
    T, D = x2.shape
    n_mid = J_MID * TN
    n_gk = GLA_HEADS * GLA_KDIM
    lr0 = J_LR * TN
    gr0 = lr0 + GLA_GATE_RANK
    n_gr = w_t.shape[0] - gr0
    assert n_gr == 2 * TN and lr0 % GLA_GATE_RANK == 0
    return pl.pallas_call(
        _inproj_kernel,
        grid=(T // tm, N_J),
        in_specs=[
            pl.BlockSpec((tm, D), lambda i, j: (i, 0)),
            pl.BlockSpec((1, D), lambda i, j: (0, 0)),
            pl.BlockSpec((tm, 1), lambda i, j: (i, 0)),
            pl.BlockSpec((1, LANES), lambda i, j: (0, 0)),
            pl.BlockSpec((2, LANES), lambda i, j: (0, 0)),
            pl.BlockSpec((TN, D), lambda i, j: (jnp.minimum(j, J_LR - 1), 0)),
            pl.BlockSpec((pl.Element(TN), pl.Element(D)),
                         lambda i, j: (pl.multiple_of(gr0 + TN * jnp.clip(j - J_GR, 0, 1), SUBLANES), 0)),
            pl.BlockSpec((GLA_GATE_RANK, D), lambda i, j: (lr0 // GLA_GATE_RANK, 0)),
            pl.BlockSpec((GLA_GATE_RANK, n_gk), lambda i, j: (0, 0)),
            pl.BlockSpec((1, n_gk), lambda i, j: (0, 0)),
        ],
        out_specs=[
            pl.BlockSpec((tm, TN), lambda i, j: (i, jnp.clip(j - 1, 0, J_QK - 1))),
            pl.BlockSpec((tm, TN), lambda i, j: (i, jnp.clip(j - J_QK, 0, J_MID - 1))),
            pl.BlockSpec((tm, n_gk), lambda i, j: (i, 0)),
            pl.BlockSpec((tm, TN), lambda i, j: (i, jnp.clip(j - J_GR, 0, 1))),
        ],
        out_shape=[
            jax.ShapeDtypeStruct((T, J_QK * TN), BF16),
            jax.ShapeDtypeStruct((T, n_mid), BF16),
            jax.ShapeDtypeStruct((T, n_gk), F32),
            jax.ShapeDtypeStruct((T, n_gr), BF16),
        ],
        scratch_shapes=[
            pltpu.VMEM((tm, D), BF16),
            pltpu.VMEM((tm, LANES), F32),
            pltpu.VMEM((tm, LANES), F32),
            pltpu.VMEM((2, tm, TN), F32),
        ],
        compiler_params=_cparams(("parallel", "arbitrary"), 56),
        name="inproj",
    )(x2, g_attn, pos2, freq, qkg, w_t, w_t, w_t, w_a2, b_a)


SCORE_BOUND = 80.0


def _diffattn_kernel(lv_ref, q_ref, k_ref, v_ref, sg_ref, o_ref, vext, diag_mask, acc1, acc2, m1, m2,
                     *, tq, bounded):
    i = pl.program_id(2)

    @pl.when(i == 0)
    def _():
        S = v_ref.shape[0]
        lane_s = lax.broadcasted_iota(I32, (S, LANES), 1)
        vext[:, 0:DIFF_VDIM] = v_ref[...]
        vext[:, DIFF_VDIM:] = jnp.where(lane_s == 0, 1.0, 0.0).astype(BF16)
        row_chunk = lax.broadcasted_iota(I32, (tq, tq), 0) // CHUNK
        col_chunk = lax.broadcasted_iota(I32, (tq, tq), 1) // CHUNK
        diag_mask[...] = jnp.where(col_chunk <= row_chunk, 1.0, 0.0).astype(BF16)

    q = q_ref[...]
    lane = lax.broadcasted_iota(I32, (1, LANES), 1)
    zero = jnp.zeros_like(q)
    q1 = jnp.where(lane < DIFF_QKDIM, q, zero)
    q2 = jnp.where(lane < DIFF_QKDIM, zero, q)

    half = tq // 2
    comps = ((q1, acc1, m1), (q2, acc2, m2))

    def step(j):
        start = pl.multiple_of(j * tq, tq)
        k = k_ref[pl.ds(start, tq), :]
        v = vext[pl.ds(start, tq), :]
        for qc, acc, m in comps:
            s = _dot_nt(qc, k)
            if bounded:
                acc[...] += _dot(jnp.exp2(s).astype(BF16), v)
            else:
                m_old = m[...]
                m_new = jnp.maximum(m_old, jnp.max(s, axis=-1, keepdims=True))
                p = jnp.exp2(s - m_new)
                acc[...] = jnp.exp2(m_old - m_new) * acc[...] + _dot(p.astype(BF16), v)
                m[...] = m_new

    def diag_step():
        start = pl.multiple_of(i * tq, tq)
        for qc, acc, m in comps:
            for rows, n_keys in ((slice(0, half), half), (slice(half, tq), tq)):
                k = k_ref[pl.ds(start, n_keys), :]
                v = vext[pl.ds(start, n_keys), :]
                mask = diag_mask[rows, 0:n_keys]
                s = _dot_nt(qc[rows], k)
                if bounded:
                    acc[rows, :] = _dot(jnp.exp2(s).astype(BF16) * mask, v)
                else:
                    s = jnp.where(mask > 0, s, NEG_INF)
                    m_new = jnp.max(s, axis=-1, keepdims=True)
                    acc[rows, :] = _dot(jnp.exp2(s - m_new).astype(BF16), v)
                    m[rows, :] = m_new

    @pl.when(i % 2 == 1)
    def _():
        diag_step()
        step(i - 1)

    @pl.when(i % 2 == 0)
    def _():
        diag_step()

    n_quads = i // 4

    def body(jj, carry):
        for u in range(4):
            step(4 * jj + u)
        return carry

    lax.fori_loop(0, n_quads, body, 0)

    @pl.when((i // 2) % 2 == 1)
    def _():
        step(4 * n_quads)
        step(4 * n_quads + 1)

    lv = lv_ref[...]
    lam = (jnp.exp(jnp.sum(lv[0:1] * lv[1:2], axis=-1, keepdims=True))
           - jnp.exp(jnp.sum(lv[2:3] * lv[3:4], axis=-1, keepdims=True)) + LAM_INIT)
    a1 = acc1[...]
    a2 = acc2[...]
    o = (a1[:, :DIFF_VDIM] / a1[:, DIFF_VDIM:DIFF_VDIM + 1]
         - lam * (a2[:, :DIFF_VDIM] / a2[:, DIFF_VDIM:DIFF_VDIM + 1]))
    o_ref[...] = (_rms(o, sg_ref[...]) * (1.0 - LAM_INIT)).astype(BF16)


def _diffattn(lvec, qk, mid, subln_g, *, B, S, tq, bounded):
    T = B * S
    nq = S // tq
    kern = functools.partial(_diffattn_kernel, tq=tq, bounded=bounded)
    return pl.pallas_call(
        kern,
        grid=(B, DIFF_HEADS, nq),
        in_specs=[
            pl.BlockSpec((4, DIFF_QKDIM), lambda b, h, i: (0, 0)),
            pl.BlockSpec((tq, LANES), lambda b, h, i: (b * nq + i, h)),
            pl.BlockSpec((S, LANES), lambda b, h, i: (b, DIFF_HEADS + h)),
            pl.BlockSpec((S, LANES), lambda b, h, i: (b, h)),
            pl.BlockSpec((1, DIFF_VDIM), lambda b, h, i: (0, 0)),
        ],
        out_specs=pl.BlockSpec((tq, DIFF_VDIM), lambda b, h, i: (b * nq + i, h)),
        out_shape=jax.ShapeDtypeStruct((T, DIFF_HEADS * DIFF_VDIM), BF16),
        scratch_shapes=[
            pltpu.VMEM((S, 2 * DIFF_VDIM), BF16),
            pltpu.VMEM((tq, tq), BF16),
            pltpu.VMEM((tq, 2 * DIFF_VDIM), F32),
            pltpu.VMEM((tq, 2 * DIFF_VDIM), F32),
            pltpu.VMEM((tq, 1), F32),
            pltpu.VMEM((tq, 1), F32),
        ],
        compiler_params=_cparams(("parallel", "parallel", "arbitrary"), 32),
        name="diffattn_bounded" if bounded else "diffattn_online",
    )(lvec, qk, qk, mid, subln_g)


def _gla_kernel(q_ref, k_ref, v_ref, la_ref, sgr_ref, g_ref, tri_ref, ones_ref, o_ref, state, *, blk):
    @pl.when(pl.program_id(2) == 0)
    def _():
        state[...] = jnp.zeros(state.shape, F32)

    la_t = la_ref[...].T
    k_t = k_ref[...].astype(F32).T
    hi, lo = _split_bf16(la_t)
    tri = tri_ref[...]
    ones = ones_ref[...]
    cum_t = _dot(hi, tri) + _dot(lo, tri)
    tot_t = _dot(hi, ones) + _dot(lo, ones)
    kd_t = k_t * jnp.exp(tot_t - cum_t)

    n_chunks = blk // CHUNK
    lane = lax.broadcasted_iota(I32, (1, LANES), 1)
    d_states = []
    for ck in range(n_chunks):
        pair = slice((ck // 2) * LANES, (ck // 2 + 1) * LANES)
        in_chunk = (lane // CHUNK) == (ck % 2)
        kd = jnp.where(in_chunk, kd_t[:, pair], 0.0).astype(BF16)
        d_states.append(_dot(kd, v_ref[pair, :]))

    st = state[...]
    states = []
    for ck in range(n_chunks):
        decay = jnp.exp(tot_t[:, ck * CHUNK:ck * CHUNK + 1])
        st = decay * st + d_states[ck]
        states.append(st.astype(BF16))
    state[...] = st

    o = jnp.concatenate([_dot(q_ref[ck * CHUNK:(ck + 1) * CHUNK, :], states[ck]) for ck in range(n_chunks)],
                        axis=0) * (GLA_KDIM ** -0.5)
    o_ref[...] = (_rms(o, g_ref[...]) * sgr_ref[...].astype(F32)).astype(BF16)


def _gla(mid, log_a, sgr, out_g, *, B, S, blk):
    T = B * S
    ns = S // blk
    kern = functools.partial(_gla_kernel, blk=blk)
    q_col0 = (DIFF_HEADS * DIFF_VDIM) // GLA_KDIM
    k_col0 = q_col0 + GLA_HEADS
    v_col0 = (DIFF_HEADS * DIFF_VDIM + 2 * GLA_HEADS * GLA_KDIM) // GLA_VDIM
    r = jnp.arange(blk, dtype=I32)[:, None]
    c = jnp.arange(blk, dtype=I32)[None, :]
    same = (r // CHUNK) == (c // CHUNK)
    tri = (same & (r <= c)).astype(BF16)
    ones = same.astype(BF16)
    return pl.pallas_call(
        kern,
        grid=(B, GLA_HEADS, ns),
        in_specs=[
            pl.BlockSpec((blk, GLA_KDIM), lambda b, h, s: (b * ns + s, q_col0 + h)),
            pl.BlockSpec((blk, GLA_KDIM), lambda b, h, s: (b * ns + s, k_col0 + h)),
            pl.BlockSpec((blk, GLA_VDIM), lambda b, h, s: (b * ns + s, v_col0 + h)),
            pl.BlockSpec((blk, GLA_KDIM), lambda b, h, s: (b * ns + s, h)),
            pl.BlockSpec((blk, GLA_VDIM), lambda b, h, s: (b * ns + s, h)),
            pl.BlockSpec((1, GLA_VDIM), lambda b, h, s: (0, 0)),
            pl.BlockSpec((blk, blk), lambda b, h, s: (0, 0)),
            pl.BlockSpec((blk, blk), lambda b, h, s: (0, 0)),
        ],
        out_specs=pl.BlockSpec((blk, GLA_VDIM), lambda b, h, s: (b * ns + s, h)),
        out_shape=jax.ShapeDtypeStruct((T, GLA_HEADS * GLA_VDIM), BF16),
        scratch_shapes=[pltpu.VMEM((GLA_KDIM, GLA_VDIM), F32)],
        compiler_params=_cparams(("parallel", "parallel", "arbitrary"), 32),
        name="gla",
    )(mid, mid, mid, log_a, sgr, out_g, tri, ones)


def _resident_w_map(n_j):
    return lambda i, j: (0, jnp.where(i == 0, j, n_j - 1))


def _outproj_kernel(a_ref, b_ref, wa_ref, wb_ref, x_ref, o_ref, w_scr):
    j = pl.program_id(1)
    ka = a_ref.shape[1]

    @pl.when(pl.program_id(0) == 0)
    def _():
        w_scr[j, 0:ka, :] = wa_ref[...].astype(BF16)
        w_scr[j, ka:, :] = wb_ref[...].astype(BF16)

    acc = _dot(a_ref[...], w_scr[j, 0:ka, :]) + _dot(b_ref[...], w_scr[j, ka:, :])
    o_ref[...] = x_ref[...] + acc


def _outproj(a, b, w_out, x2, *, tm, tn):
    T, ka = a.shape
    kb = b.shape[1]
    assert ka == kb
    D = w_out.shape[1]
    n_j = D // tn
    return pl.pallas_call(
        _outproj_kernel,
        grid=(T // tm, n_j),
        in_specs=[
            pl.BlockSpec((tm, ka), lambda i, j: (i, 0)),
            pl.BlockSpec((tm, kb), lambda i, j: (i, 0)),
            pl.BlockSpec((ka, tn), _resident_w_map(n_j)),
            pl.BlockSpec((kb, tn), lambda i, j: (1, jnp.where(i == 0, j, n_j - 1))),
            pl.BlockSpec((tm, tn), lambda i, j: (i, j)),
        ],
        out_specs=pl.BlockSpec((tm, tn), lambda i, j: (i, j)),
        out_shape=jax.ShapeDtypeStruct((T, D), F32),
        scratch_shapes=[pltpu.VMEM((n_j, ka + kb, tn), BF16)],
        compiler_params=_cparams(("arbitrary", "arbitrary"), 48),
        name="outproj",
    )(a, b, w_out, w_out, x2)


def _normproj_kernel(x_ref, g_ref, w_ref, hg_ref, o_ref, n_scr, *w_scr, n_norm):
    j = pl.program_id(1)

    @pl.when(j == 0)
    def _():
        n_scr[...] = _rms(x_ref[...], g_ref[...]).astype(BF16)

    if w_scr:
        @pl.when(pl.program_id(0) == 0)
        def _():
            w_scr[0][j] = w_ref[...].astype(BF16)
        y = _dot(n_scr[...], w_scr[0][j])
    else:
        y = _dot(n_scr[...], w_ref[...].astype(BF16))

    @pl.when(j < n_norm)
    def _():
        o_ref[...] = _rms(y, hg_ref[...]).astype(BF16)

    @pl.when(j >= n_norm)
    def _():
        o_ref[...] = y.astype(BF16)


def _normproj(x2, g, w, head_g, *, tm, tn, n_norm, name):
    T, D = x2.shape
    N = w.shape[1]
    kern = functools.partial(_normproj_kernel, n_norm=n_norm)
    n_j = N // tn
    resident = T // tm > 1
    return pl.pallas_call(
        kern,
        grid=(T // tm, n_j),
        in_specs=[
            pl.BlockSpec((tm, D), lambda i, j: (i, 0)),
            pl.BlockSpec((1, D), lambda i, j: (0, 0)),
            pl.BlockSpec((D, tn), _resident_w_map(n_j) if resident else (lambda i, j: (0, j))),
            pl.BlockSpec((1, tn), lambda i, j: (0, 0)),
        ],
        out_specs=pl.BlockSpec((tm, tn), lambda i, j: (i, j)),
        out_shape=jax.ShapeDtypeStruct((T, N), BF16),
        scratch_shapes=[pltpu.VMEM((tm, D), BF16)] + ([pltpu.VMEM((n_j, D, tn), BF16)] if resident else []),
        compiler_params=_cparams(("arbitrary", "arbitrary"), 48),
        name=name,
    )(x2, g, w, head_g)


def _cross_kernel(q_ref, k_ref, v_ref, w_ref, h_ref, o_ref, att_scr, w_scr, *, hdim):
    j = pl.program_id(1)

    @pl.when(pl.program_id(0) == 0)
    def _():
        w_scr[j] = w_ref[...].astype(BF16)

    @pl.when(j == 0)
    def _():
        for hd in range(CROSS_HEADS):
            cols = slice(hd * hdim, (hd + 1) * hdim)
            s = lax.dot_general(q_ref[:, cols], k_ref[:, cols], NT_DIMS, preferred_element_type=F32)
            p = jnp.exp(s - jnp.max(s, axis=-1, keepdims=True))
            l = jnp.sum(p, axis=-1, keepdims=True)
            att_scr[:, cols] = (_dot(p.astype(BF16), v_ref[:, cols]) / l).astype(BF16)

    o_ref[...] = h_ref[...] + _dot(att_scr[...], w_scr[j])


def _cross(qc, kv, w_co, h1, *, S, n_mem, tm, tn):
    T, D = qc.shape
    per_b = S // tm
    kern = functools.partial(_cross_kernel, hdim=D // CROSS_HEADS)
    n_j = D // tn
    return pl.pallas_call(
        kern,
        grid=(T // tm, n_j),
        in_specs=[
            pl.BlockSpec((tm, D), lambda i, j: (i, 0)),
            pl.BlockSpec((n_mem, D), lambda i, j: (i // per_b, 0)),
            pl.BlockSpec((n_mem, D), lambda i, j: (i // per_b, 1)),
            pl.BlockSpec((D, tn), _resident_w_map(n_j)),
            pl.BlockSpec((tm, tn), lambda i, j: (i, j)),
        ],
        out_specs=pl.BlockSpec((tm, tn), lambda i, j: (i, j)),
        out_shape=jax.ShapeDtypeStruct((T, D), F32),
        scratch_shapes=[pltpu.VMEM((tm, D), BF16), pltpu.VMEM((n_j, D, tn), BF16)],
        compiler_params=_cparams(("arbitrary", "arbitrary"), 48),
        name="cross",
    )(qc, kv, kv, w_co, h1)


R_ROWS = SUBLANES + N_EXPERTS


def _pack_bf16_pairs(xb16):
    c = xb16.shape[1] // 2
    u = lax.bitcast_convert_type(xb16.astype(F32), jnp.uint32)
    return (u[:, :c] >> 16) | (u[:, c:] & jnp.uint32(0xFFFF0000))


def _unpack_bf16_pairs_f32(w):
    lo = lax.bitcast_convert_type(w << 16, F32)
    hi = lax.bitcast_convert_type(w & jnp.uint32(0xFFFF0000), F32)
    return lo, hi


def _unpack_bf16_pairs(w):
    lo, hi = _unpack_bf16_pairs_f32(w)
    return lo.astype(BF16), hi.astype(BF16)


def _router_kernel(h_ref, g_ref, wt_ref, b_ref, eid_ref, gate_ref, xn_ref):
    n = _rms(h_ref[...], g_ref[...])
    nh, nl = _split_bf16(n)
    xn_ref[...] = _pack_bf16_pairs(nh)
    wh, wl = _split_bf16(wt_ref[...])
    nt = functools.partial(lax.dot_general, dimension_numbers=NT_DIMS, preferred_element_type=F32)
    lg = nt(wh, nh) + nt(wh, nl) + nt(wl, nh) + b_ref[...]

    tm = lg.shape[1]
    row = lax.broadcasted_iota(I32, (SUBLANES, tm), 0)

    def first_argmax(v, vmax):
        return jnp.min(jnp.where(v == vmax, row, SUBLANES), axis=0, keepdims=True)

    gl = jnp.where(row < N_GROUPS, lg[0:SUBLANES], NEG_INF)
    gmax = jnp.max(gl, axis=0, keepdims=True)
    grp = first_argmax(gl, gmax)
    grp_w = 1.0 / jnp.sum(jnp.exp(gl - gmax), axis=0, keepdims=True)

    sel = jnp.zeros((SUBLANES, tm), F32)
    for gi in range(N_GROUPS):
        lo = SUBLANES + gi * EXPERTS_PER_GROUP
        sel = jnp.where(grp == gi, lg[lo:lo + EXPERTS_PER_GROUP], sel)
    e = jnp.exp(sel - jnp.max(sel, axis=0, keepdims=True))
    prob = e / jnp.sum(e, axis=0, keepdims=True)
    p1 = jnp.max(prob, axis=0, keepdims=True)
    i1 = first_argmax(prob, p1)
    rest = jnp.where(row == i1, -1.0, prob)
    p2 = jnp.max(rest, axis=0, keepdims=True)
    i2 = first_argmax(rest, p2)
    den = p1 + p2
    base = grp * EXPERTS_PER_GROUP
    eid_ref[...] = jnp.where(row == 0, base + i1, jnp.where(row == 1, base + i2, 0))
    gate_ref[...] = jnp.where(row == 0, grp_w * p1 / den, jnp.where(row == 1, grp_w * p2 / den, 0.0))


def _router(h2, g_ffn, w_rt, b_r, *, tm):
    T, D = h2.shape
    return pl.pallas_call(
        _router_kernel,
        grid=(T // tm,),
        in_specs=[
            pl.BlockSpec((tm, D), lambda i: (i, 0)),
            pl.BlockSpec((1, D), lambda i: (0, 0)),
            pl.BlockSpec((R_ROWS, D), lambda i: (0, 0)),
            pl.BlockSpec((R_ROWS, 1), lambda i: (0, 0)),
        ],
        out_specs=[
            pl.BlockSpec((SUBLANES, tm), lambda i: (0, i)),
            pl.BlockSpec((SUBLANES, tm), lambda i: (0, i)),
            pl.BlockSpec((tm, D // 2), lambda i: (i, 0)),
        ],
        out_shape=[
            jax.ShapeDtypeStruct((SUBLANES, T), I32),
            jax.ShapeDtypeStruct((SUBLANES, T), F32),
            jax.ShapeDtypeStruct((T, D // 2), jnp.uint32),
        ],
        compiler_params=_cparams(("parallel",), 32),
        name="router",
    )(h2, g_ffn, w_rt, b_r)


def _rank_kernel(eid_ref, rank_ref, cnt_ref, carry):
    @pl.when(pl.program_id(0) == 0)
    def _():
        carry[...] = jnp.zeros(carry.shape, F32)

    tm = eid_ref.shape[1]
    e0 = eid_ref[0:1, :]
    e1 = eid_ref[1:2, :]
    erow = lax.broadcasted_iota(I32, (N_EXPERTS, tm), 0)
    hit = jnp.where((erow == e0) | (erow == e1), 1.0, 0.0)
    r = lax.broadcasted_iota(I32, (tm, tm), 0)
    c = lax.broadcasted_iota(I32, (tm, tm), 1)
    before = jnp.where(r < c, 1.0, 0.0).astype(BF16)
    pre = _dot(hit.astype(BF16), before) + carry[:, 0:1]
    rank0 = jnp.sum(jnp.where(erow == e0, pre, 0.0), axis=0, keepdims=True)
    rank1 = jnp.sum(jnp.where(erow == e1, pre, 0.0), axis=0, keepdims=True)
    row = lax.broadcasted_iota(I32, (SUBLANES, tm), 0)
    rank_ref[...] = jnp.where(row == 0, rank0, jnp.where(row == 1, rank1, 0.0)).astype(I32)
    total = carry[...] + jnp.sum(hit, axis=1, keepdims=True)
    carry[...] = total
    cnt_ref[...] = total.astype(I32)


def _rank(eid, *, tm):
    T = eid.shape[1]
    return pl.pallas_call(
        _rank_kernel,
        grid=(T // tm,),
        in_specs=[pl.BlockSpec((SUBLANES, tm), lambda i: (0, i))],
        out_specs=[
            pl.BlockSpec((SUBLANES, tm), lambda i: (0, i)),
            pl.BlockSpec((N_EXPERTS, LANES), lambda i: (0, 0)),
        ],
        out_shape=[
            jax.ShapeDtypeStruct((SUBLANES, T), I32),
            jax.ShapeDtypeStruct((N_EXPERTS, LANES), I32),
        ],
        scratch_shapes=[pltpu.VMEM((N_EXPERTS, LANES), F32)],
        compiler_params=_cparams(("arbitrary",), 32),
        name="rank",
    )(eid)


def _dispatch_kernel(dest_ref, pad0_ref, npad_ref, nb_ref, xn_ref, xb_hbm, zbuf, sem, psem,
                     *, n_tok, tb, n_blocks):
    i = pl.program_id(0)
    tm = xn_ref.shape[0]
    base = i * tm

    def row_body(r, carry):
        for k in range(TOP_K):
            pltpu.make_async_copy(xn_ref.at[pl.ds(r, 1), :],
                                  xb_hbm.at[pl.ds(dest_ref[k * n_tok + base + r], 1), :], sem).start()
        return carry
    lax.fori_loop(0, tm, row_body, 0, unroll=8)

    @pl.when(i == 0)
    def _():
        _dispatch_fill(pad0_ref, npad_ref, nb_ref, xb_hbm, zbuf, psem, tb=tb, n_blocks=n_blocks)

    for k in range(TOP_K):
        pltpu.make_async_copy(xn_ref, xb_hbm.at[pl.ds(0, tm), :], sem).wait()


def _dispatch_fill(pad0_ref, npad_ref, nb_ref, xb_hbm, zbuf, psem, *, tb, n_blocks):
    zbuf[...] = jnp.zeros(zbuf.shape, zbuf.dtype)

    def pad_copy(slot):
        return pltpu.make_async_copy(zbuf.at[pl.ds(0, 1), :], xb_hbm.at[pl.ds(slot, 1), :], psem.at[0])

    def tail_copy(blk):
        return pltpu.make_async_copy(zbuf, xb_hbm.at[pl.ds(pl.multiple_of(blk * tb, tb), tb), :], psem.at[1])

    def for_each_pad(fn):
        def expert_body(e, carry):
            def body(r, c):
                fn(pad0_ref[e] + r)
                return c
            lax.fori_loop(0, npad_ref[e], body, 0)
            return carry
        lax.fori_loop(0, N_EXPERTS, expert_body, 0)

    def for_each_tail(fn):
        def body(b, c):
            fn(b)
            return c
        lax.fori_loop(nb_ref[0], n_blocks, body, 0)

    for_each_pad(lambda slot: pad_copy(slot).start())
    for_each_tail(lambda b: tail_copy(b).start())
    for_each_pad(lambda slot: pad_copy(slot).wait())
    for_each_tail(lambda b: tail_copy(b).wait())


def _dispatch(dest, pad0, npad, nb, xn, *, tb, n_blocks, tm):
    T, C = xn.shape
    kern = functools.partial(_dispatch_kernel, n_tok=T, tb=tb, n_blocks=n_blocks)
    grid_spec = pltpu.PrefetchScalarGridSpec(
        num_scalar_prefetch=4,
        grid=(T // tm,),
        in_specs=[pl.BlockSpec((tm, C), lambda i, *_: (i, 0))],
        out_specs=pl.BlockSpec(memory_space=pl.ANY),
        scratch_shapes=[
            pltpu.VMEM((tb, C), xn.dtype),
            pltpu.SemaphoreType.DMA(()),
            pltpu.SemaphoreType.DMA((2,)),
        ],
    )
    return pl.pallas_call(
        kern,
        grid_spec=grid_spec,
        out_shape=jax.ShapeDtypeStruct((n_blocks * tb, C), xn.dtype),
        compiler_params=_cparams(("arbitrary",), 32),
        name="dispatch",
    )(dest, pad0, npad, nb, xn)


def _expert_kernel(be_ref, first_ref, nxt_ref, nb_ref, x_ref, wg_hbm, wu_hbm, wd_hbm, y_ref,
                   wg_f, wu_f, wd_f, wsem, wg_b, wu_b, wd_b):
    i = pl.program_id(0)
    nb = nb_ref[0]

    def weight_copies(e):
        return (pltpu.make_async_copy(wg_hbm.at[e], wg_f, wsem.at[0]),
                pltpu.make_async_copy(wu_hbm.at[e], wu_f, wsem.at[1]),
                pltpu.make_async_copy(wd_hbm.at[e], wd_f, wsem.at[2]))

    @pl.when(i == 0)
    def _():
        for cp in weight_copies(be_ref[0]):
            cp.start()

    first = (i < nb) & (first_ref[i] == 1)

    @pl.when(first)
    def _():
        for cp in weight_copies(0):
            cp.wait()
        wg_b[...] = wg_f[...].astype(BF16)
        wu_b[...] = wu_f[...].astype(BF16)
        wd_b[...] = wd_f[...].astype(BF16)

    @pl.when(first & (nxt_ref[i] >= 0))
    def _():
        for cp in weight_copies(jnp.maximum(nxt_ref[i], 0)):
            cp.start()

    @pl.when(i < nb)
    def _():
        n_lo, n_hi = _unpack_bf16_pairs(x_ref[...])
        half = n_lo.shape[1]
        a = _dot(n_lo, wg_b[0:half, :]) + _dot(n_hi, wg_b[half:, :])
        u = _dot(n_lo, wu_b[0:half, :]) + _dot(n_hi, wu_b[half:, :])
        hdn = (a / (1.0 + jnp.exp(-a))) * u
        y_ref[...] = _pack_bf16_pairs(_dot(hdn.astype(BF16), wd_b[...]).astype(BF16))

    @pl.when(i >= nb)
    def _():
        y_ref[...] = jnp.zeros(y_ref.shape, y_ref.dtype)


def _experts(xb, blk_e, blk_first, blk_next, nb, w_gate, w_up, w_down, *, tb, n_blocks):
    D, De = w_gate.shape[1:]
    C = xb.shape[1]

    def x_map(i, be, first, nxt, nbr):
        return (jnp.minimum(i, nbr[0] - 1), 0)

    grid_spec = pltpu.PrefetchScalarGridSpec(
        num_scalar_prefetch=4,
        grid=(n_blocks,),
        in_specs=[
            pl.BlockSpec((tb, C), x_map),
            pl.BlockSpec(memory_space=pl.ANY),
            pl.BlockSpec(memory_space=pl.ANY),
            pl.BlockSpec(memory_space=pl.ANY),
        ],
        out_specs=pl.BlockSpec((tb, D // 2), lambda i, *_: (i, 0)),
        scratch_shapes=[
            pltpu.VMEM((D, De), F32),
            pltpu.VMEM((D, De), F32),
            pltpu.VMEM((De, D), F32),
            pltpu.SemaphoreType.DMA((3,)),
            pltpu.VMEM((D, De), BF16),
            pltpu.VMEM((D, De), BF16),
            pltpu.VMEM((De, D), BF16),
        ],
    )
    return pl.pallas_call(
        _expert_kernel,
        grid_spec=grid_spec,
        out_shape=jax.ShapeDtypeStruct((n_blocks * tb, D // 2), jnp.uint32),
        compiler_params=_cparams(("arbitrary",), 48),
        name="experts",
    )(blk_e, blk_first, blk_next, nb, xb, w_gate, w_up, w_down)


def _combine_kernel(dest_ref, y_hbm, h_ref, gate_ref, o_ref, ybuf, sem, *, tm, n_tok):
    i = pl.program_id(0)
    n = pl.num_programs(0)

    def row_copy(d, k, r, slot):
        return pltpu.make_async_copy(y_hbm.at[pl.ds(d, 1), :], ybuf.at[slot, k, pl.ds(r, 1), :], sem.at[slot])

    def start_gather(blk, slot):
        def body(r, carry):
            for k in range(TOP_K):
                row_copy(dest_ref[k * n_tok + blk * tm + r], k, r, slot).start()
            return carry
        lax.fori_loop(0, tm, body, 0, unroll=8)

    def wait_gather(slot):
        for k in range(TOP_K):
            pltpu.make_async_copy(y_hbm.at[pl.ds(0, tm), :], ybuf.at[slot, k], sem.at[slot]).wait()

    @pl.when(i == 0)
    def _():
        start_gather(0, 0)

    @pl.when(i + 1 < n)
    def _():
        start_gather(i + 1, (i + 1) % 2)

    slot = i % 2
    wait_gather(slot)
    gt = gate_ref[...]
    half = h_ref.shape[1] // 2
    y0_lo, y0_hi = _unpack_bf16_pairs_f32(ybuf[slot, 0])
    y1_lo, y1_hi = _unpack_bf16_pairs_f32(ybuf[slot, 1])
    o_ref[:, 0:half] = h_ref[:, 0:half] + gt[:, 0:1] * y0_lo + gt[:, 1:2] * y1_lo
    o_ref[:, half:] = h_ref[:, half:] + gt[:, 0:1] * y0_hi + gt[:, 1:2] * y1_hi


def _combine(dest, yb, h2, gate, *, tm):
    T, D = h2.shape
    kern = functools.partial(_combine_kernel, tm=tm, n_tok=T)
    grid_spec = pltpu.PrefetchScalarGridSpec(
        num_scalar_prefetch=1,
        grid=(T // tm,),
        in_specs=[
            pl.BlockSpec(memory_space=pl.ANY),
            pl.BlockSpec((tm, D), lambda i, d: (i, 0)),
            pl.BlockSpec((tm, TOP_K), lambda i, d: (i, 0)),
        ],
        out_specs=pl.BlockSpec((tm, D), lambda i, d: (i, 0)),
        scratch_shapes=[
            pltpu.VMEM((2, TOP_K, tm, D // 2), jnp.uint32),
            pltpu.SemaphoreType.DMA((2,)),
        ],
    )
    return pl.pallas_call(
        kern,
        grid_spec=grid_spec,
        out_shape=jax.ShapeDtypeStruct((T, D), F32),
        compiler_params=_cparams(("arbitrary",), 40),
        name="combine",
    )(dest, yb, h2, gate)


EXPERT_ROWS = 256


def kernel(x, mem, positions, g_attn, w_in, q_norm_g, k_norm_g, lambda_q1, lambda_k1, lambda_q2, lambda_k2, diff_subln_g, gla_w_a2, gla_b_a, gla_out_g, w_out, g_cross, g_mem, w_cq, w_ckv, cq_norm_g, ck_norm_g, w_co, g_ffn, w_router_grp, b_router_grp, w_router_exp, b_router_exp, w_gate, w_up, w_down):
    B, S, D = x.shape
    T = B * S
    n_mem = mem.shape[1]
    l = 0
    x2 = x.reshape(T, D)

    half = DIFF_QKDIM // 2
    freq = ROPE_THETA ** (-jnp.arange(half, dtype=F32) / half)
    freq = jnp.tile(freq, LANES // half)[None, :]
    q_scale = math.log2(math.e) * DIFF_QKDIM ** -0.5
    qkg = jnp.stack([jnp.tile(q_norm_g[l], 2) * q_scale, jnp.tile(k_norm_g[l], 2)])
    score_bound = 1.01 * DIFF_QKDIM * q_scale * jnp.max(jnp.abs(q_norm_g[l])) * jnp.max(jnp.abs(k_norm_g[l]))
    lvec = jnp.stack([lambda_q1[l], lambda_k1[l], lambda_q2[l], lambda_k2[l]])

    qk, mid, log_a, sgr = _inproj(x2, g_attn[l][None], positions.reshape(T, 1), freq, qkg, w_in[l].T,
                                  gla_w_a2[l], gla_b_a[l][None], tm=1024)
    diffattn = functools.partial(_diffattn, lvec, qk, mid, diff_subln_g[l][None], B=B, S=S, tq=512)
    mix_d = lax.cond(score_bound <= SCORE_BOUND,
                     functools.partial(diffattn, bounded=True), functools.partial(diffattn, bounded=False))
    mix_g = _gla(mid, log_a, sgr, gla_out_g[l][None], B=B, S=S, blk=512)
    h1 = _outproj(mix_d, mix_g, w_out[l], x2, tm=1024, tn=512)

    hdim = D // CROSS_HEADS
    qc = _normproj(h1, g_cross[l][None], w_cq[l], cq_norm_g[l][None] * (hdim ** -0.5),
                   tm=1024, tn=hdim, n_norm=CROSS_HEADS, name="cq")
    kv = _normproj(mem.reshape(B * n_mem, D), g_mem[l][None], w_ckv[l], ck_norm_g[l][None],
                   tm=B * n_mem, tn=hdim, n_norm=CROSS_HEADS, name="ckv")
    h2 = _cross(qc, kv, w_co[l], h1, S=S, n_mem=n_mem, tm=512, tn=512)

    w_rt = jnp.concatenate([w_router_grp[l].T, jnp.zeros((SUBLANES - N_GROUPS, D), F32), w_router_exp[l].T])
    b_r = jnp.concatenate([b_router_grp[l], jnp.zeros((SUBLANES - N_GROUPS,), F32), b_router_exp[l]])[:, None]
    eid, gate, xn = _router(h2, g_ffn[l][None], w_rt, b_r, tm=512)
    rank, cnt = _rank(eid, tm=512)

    tb = EXPERT_ROWS
    n_blocks = (T * TOP_K + N_EXPERTS * (tb - 1) + tb - 1) // tb
    counts = cnt[:, 0]
    pcounts = ((counts + tb - 1) // tb) * tb
    pends = jnp.cumsum(pcounts)
    pstarts = pends - pcounts
    nb = (pends[-1:] // tb).astype(I32)
    eids = jnp.arange(N_EXPERTS, dtype=I32)
    blk_start = jnp.arange(n_blocks, dtype=I32) * tb
    blk_e = jnp.minimum(jnp.sum(pends[None, :] <= blk_start[:, None], axis=1), N_EXPERTS - 1).astype(I32)
    blk_first = jnp.concatenate([jnp.ones((1,), I32), (blk_e[1:] != blk_e[:-1]).astype(I32)])
    used = jnp.where(counts > 0, eids, N_EXPERTS)
    next_used = jnp.concatenate([lax.cummin(used[::-1])[::-1][1:], jnp.full((1,), N_EXPERTS, I32)])
    next_used = jnp.where(next_used < N_EXPERTS, next_used, -1)
    blk_next = jnp.sum(jnp.where(blk_e[:, None] == eids, next_used, 0), axis=1).astype(I32)
    pick = eid[:TOP_K]
    pstart_of = jnp.sum(jnp.where(pick[..., None] == eids, pstarts, 0), axis=-1)
    dest = (pstart_of + rank[:TOP_K]).astype(I32).reshape(-1)
    pad0 = (pstarts + counts).astype(I32)
    npad = (pcounts - counts).astype(I32)

    xb = _dispatch(dest, pad0, npad, nb, xn, tb=tb, n_blocks=n_blocks, tm=512)
    yb = _experts(xb, blk_e, blk_first, blk_next, nb, w_gate[l], w_up[l], w_down[l], tb=tb, n_blocks=n_blocks)
    out = _combine(dest, yb, h2, gate[:TOP_K].T, tm=256)
    return out.reshape(B, S, D)
```

```python
import functools
import math

import jax
import jax.numpy as jnp
from jax import lax
from jax.experimental import pallas as pl
from jax.experimental.pallas import tpu as pltpu

F32 = jnp.float32
BF16 = jnp.bfloat16
I32 = jnp.int32

LANES = 128
SUBLANES = 8

CHUNK = 64
ROPE_THETA = 10000.0
NORM_EPS = 1e-6
NEG_INF = -1e30
DIFF_HEADS = 8
DIFF_VDIM = 128
DIFF_QKDIM = 64
GLA_HEADS = 4
GLA_VDIM = 256
GLA_KDIM = 128
GLA_GATE_RANK = 16
GLA_TAU = 16.0
CROSS_HEADS = 4
N_GROUPS = 4
EXPERTS_PER_GROUP = 8
N_EXPERTS = N_GROUPS * EXPERTS_PER_GROUP
TOP_K = 2
LAM_INIT = 0.8 - 0.6 * math.exp(-0.3 * 0)

NT_DIMS = (((1,), (1,)), ((), ()))


def _cparams(semantics, vmem_mib):
    return pltpu.CompilerParams(dimension_semantics=semantics,
                                vmem_limit_bytes=vmem_mib * 1024 * 1024)


def _dot(a, b):
    return jnp.dot(a, b, preferred_element_type=F32)


def _dot_nt(a, b):
    return lax.dot_general(a, b, NT_DIMS, preferred_element_type=F32)


def _rms(x, g):
    ms = jnp.mean(x * x, axis=-1, keepdims=True)
    return x * lax.rsqrt(ms + NORM_EPS) * g


def _split_bf16(x):
    hi = x.astype(BF16)
    lo = (x - hi.astype(F32)).astype(BF16)
    return hi, lo


TN = 512
J_QK = 4
J_MID = 6
J_LR = J_QK + J_MID
J_GR = J_LR + 1
N_J = J_GR + 2


def _inproj_kernel(x_ref, g_ref, pos_ref, freq_ref, qkg_ref, w_ref, wgr_ref, wlr_ref, wa2_ref, ba_ref,
                   qk_ref, mid_ref, loga_ref, sgr_ref, n_scr, cos_scr, sin_scr, y_scr):
    j = pl.program_id(1)

    @pl.when(j == 0)
    def _():
        n_scr[...] = _rms(x_ref[...], g_ref[...]).astype(BF16)
        ang = pos_ref[...].astype(F32) * freq_ref[...]
        cos_scr[...] = jnp.cos(ang)
        sin_scr[...] = jnp.sin(ang)

    def qk_epilogue(jq):
        y_prev = y_scr.at[jq % 2]
        lane = lax.broadcasted_iota(I32, (1, LANES), 1)
        low_seg = lane < DIFF_QKDIM
        first_half = (lane % DIFF_QKDIM) < (DIFF_QKDIM // 2)
        gain = qkg_ref[jq // (J_QK // 2):jq // (J_QK // 2) + 1, :]
        cos = cos_scr[...]
        sin = sin_scr[...]
        for c in range(TN // LANES):
            yb = y_prev[:, c * LANES:(c + 1) * LANES]
            y2 = yb * yb
            s_lo = jnp.sum(jnp.where(low_seg, y2, 0.0), axis=-1, keepdims=True)
            s_hi = jnp.sum(jnp.where(low_seg, 0.0, y2), axis=-1, keepdims=True)
            ms = jnp.where(low_seg, s_lo, s_hi) * (1.0 / DIFF_QKDIM)
            yn = yb * lax.rsqrt(ms + NORM_EPS) * gain
            rot = jnp.where(first_half,
                            -pltpu.roll(yn, LANES - DIFF_QKDIM // 2, 1),
                            pltpu.roll(yn, DIFF_QKDIM // 2, 1))
            qk_ref[:, c * LANES:(c + 1) * LANES] = (yn * cos + rot * sin).astype(BF16)

    for jq in range(J_QK + 1):
        @pl.when(j == jq)
        def _():
            y = _dot_nt(n_scr[...], w_ref[...].astype(BF16))
            if jq > 0:
                qk_epilogue(jq - 1)
            if jq < J_QK:
                y_scr[jq % 2] = y
            else:
                mid_ref[...] = y.astype(BF16)

    @pl.when((j > J_QK) & (j < J_LR))
    def _():
        mid_ref[...] = _dot_nt(n_scr[...], w_ref[...].astype(BF16)).astype(BF16)

    @pl.when(j == J_LR)
    def _():
        lr = _dot_nt(n_scr[...], wlr_ref[...].astype(BF16))
        z = _dot(lr.astype(BF16), wa2_ref[...].astype(BF16)) + ba_ref[...]
        log_sig = jnp.minimum(z, 0.0) - jnp.log(1.0 + jnp.exp(-jnp.abs(z)))
        loga_ref[...] = log_sig * (1.0 / GLA_TAU)

    @pl.when(j >= J_GR)
    def _():
        y = _dot_nt(n_scr[...], wgr_ref[...].astype(BF16))
        sgr_ref[...] = (y / (1.0 + jnp.exp(-y))).astype(BF16)


def _inproj(x2, g_attn, pos2, freq, qkg, w_t, w_a2, b_a, *, tm):
    T, D = x2.shape
    n_mid = J_MID * TN
    n_gk = GLA_HEADS * GLA_KDIM
    lr0 = J_LR * TN
    gr0 = lr0 + GLA_GATE_RANK
    n_gr = w_t.shape[0] - gr0
    assert n_gr == 2 * TN and lr0 % GLA_GATE_RANK == 0
    return pl.pallas_call(
        _inproj_kernel,
        grid=(T // tm, N_J),
        in_specs=[
            pl.BlockSpec((tm, D), lambda i, j: (i, 0)),
            pl.BlockSpec((1, D), lambda i, j: (0, 0)),
            pl.BlockSpec((tm, 1), lambda i, j: (i, 0)),
            pl.BlockSpec((1, LANES), lambda i, j: (0, 0)),
            pl.BlockSpec((2, LANES), lambda i, j: (0, 0)),
            pl.BlockSpec((TN, D), lambda i, j: (jnp.minimum(j, J_LR - 1), 0)),
            pl.BlockSpec((pl.Element(TN), pl.Element(D)),
                         lambda i, j: (pl.multiple_of(gr0 + TN * jnp.clip(j - J_GR, 0, 1), SUBLANES), 0)),
            pl.BlockSpec((GLA_GATE_RANK, D), lambda i, j: (lr0 // GLA_GATE_RANK, 0)),
            pl.BlockSpec((GLA_GATE_RANK, n_gk), lambda i, j: (0, 0)),
            pl.BlockSpec((1, n_gk), lambda i, j: (0, 0)),
        ],
        out_specs=[
            pl.BlockSpec((tm, TN), lambda i, j: (i, jnp.clip(j - 1, 0, J_QK - 1))),
            pl.BlockSpec((tm, TN), lambda i, j: (i, jnp.clip(j - J_QK, 0, J_MID - 1))),
            pl.BlockSpec((tm, n_gk), lambda i, j: (i, 0)),
            pl.BlockSpec((tm, TN), lambda i, j: (i, jnp.clip(j - J_GR, 0, 1))),
        ],
        out_shape=[
            jax.ShapeDtypeStruct((T, J_QK * TN), BF16),
            jax.ShapeDtypeStruct((T, n_mid), BF16),
            jax.ShapeDtypeStruct((T, n_gk), F32),
            jax.ShapeDtypeStruct((T, n_gr), BF16),
        ],
        scratch_shapes=[
            pltpu.VMEM((tm, D), BF16),
            pltpu.VMEM((tm, LANES), F32),
            pltpu.VMEM((tm, LANES), F32),
            pltpu.VMEM((2, tm, TN), F32),
        ],
        compiler_params=_cparams(("parallel", "arbitrary"), 56),
        name="inproj",
    )(x2, g_attn, pos2, freq, qkg, w_t, w_t, w_t, w_a2, b_a)


SCORE_BOUND = 80.0


def _diffattn_kernel(lv_ref, q_ref, k_ref, v_ref, sg_ref, o_ref, vext, diag_mask, acc1, acc2, m1, m2,
                     *, tq, bounded):
    i = pl.program_id(2)

    @pl.when(i == 0)
    def _():
        S = v_ref.shape[0]
        lane_s = lax.broadcasted_iota(I32, (S, LANES), 1)
        vext[:, 0:DIFF_VDIM] = v_ref[...]
        vext[:, DIFF_VDIM:] = jnp.where(lane_s == 0, 1.0, 0.0).astype(BF16)
        row_chunk = lax.broadcasted_iota(I32, (tq, tq), 0) // CHUNK
        col_chunk = lax.broadcasted_iota(I32, (tq, tq), 1) // CHUNK
        diag_mask[...] = jnp.where(col_chunk <= row_chunk, 1.0, 0.0).astype(BF16)

    q = q_ref[...]
    lane = lax.broadcasted_iota(I32, (1, LANES), 1)
    zero = jnp.zeros_like(q)
    q1 = jnp.where(lane < DIFF_QKDIM, q, zero)
    q2 = jnp.where(lane < DIFF_QKDIM, zero, q)

    half = tq // 2
    comps = ((q1, acc1, m1), (q2, acc2, m2))

    def step(j):
        start = pl.multiple_of(j * tq, tq)
        k = k_ref[pl.ds(start, tq), :]
        v = vext[pl.ds(start, tq), :]
        for qc, acc, m in comps:
            s = _dot_nt(qc, k)
            if bounded:
                acc[...] += _dot(jnp.exp2(s).astype(BF16), v)
            else:
                m_old = m[...]
                m_new = jnp.maximum(m_old, jnp.max(s, axis=-1, keepdims=True))
                p = jnp.exp2(s - m_new)
                acc[...] = jnp.exp2(m_old - m_new) * acc[...] + _dot(p.astype(BF16), v)
                m[...] = m_new

    def diag_step():
        start = pl.multiple_of(i * tq, tq)
        for qc, acc, m in comps:
            for rows, n_keys in ((slice(0, half), half), (slice(half, tq), tq)):
                k = k_ref[pl.ds(start, n_keys), :]
                v = vext[pl.ds(start, n_keys), :]
                mask = diag_mask[rows, 0:n_keys]
                s = _dot_nt(qc[rows], k)
                if bounded:
                    acc[rows, :] = _dot(jnp.exp2(s).astype(BF16) * mask, v)
                else:
                    s = jnp.where(mask > 0, s, NEG_INF)
                    m_new = jnp.max(s, axis=-1, keepdims=True)
                    acc[rows, :] = _dot(jnp.exp2(s - m_new).astype(BF16), v)
                    m[rows, :] = m_new

    @pl.when(i % 2 == 1)
    def _():
        diag_step()
        step(i - 1)

    @pl.when(i % 2 == 0)
    def _():
        diag_step()

    n_quads = i // 4

    def body(jj, carry):
        for u in range(4):
            step(4 * jj + u)
        return carry

    lax.fori_loop(0, n_quads, body, 0)

    @pl.when((i // 2) % 2 == 1)
    def _():
        step(4 * n_quads)
        step(4 * n_quads + 1)

    lv = lv_ref[...]
    lam = (jnp.exp(jnp.sum(lv[0:1] * lv[1:2], axis=-1, keepdims=True))
           - jnp.exp(jnp.sum(lv[2:3] * lv[3:4], axis=-1, keepdims=True)) + LAM_INIT)
    a1 = acc1[...]
    a2 = acc2[...]
    o = (a1[:, :DIFF_VDIM] / a1[:, DIFF_VDIM:DIFF_VDIM + 1]
         - lam * (a2[:, :DIFF_VDIM] / a2[:, DIFF_VDIM:DIFF_VDIM + 1]))
    o_ref[...] = (_rms(o, sg_ref[...]) * (1.0 - LAM_INIT)).astype(BF16)


def _diffattn(lvec, qk, mid, subln_g, *, B, S, tq, bounded):
    T = B * S
    nq = S // tq
    kern = functools.partial(_diffattn_kernel, tq=tq, bounded=bounded)
    return pl.pallas_call(
        kern,
        grid=(B, DIFF_HEADS, nq),
        in_specs=[
            pl.BlockSpec((4, DIFF_QKDIM), lambda b, h, i: (0, 0)),
            pl.BlockSpec((tq, LANES), lambda b, h, i: (b * nq + i, h)),
            pl.BlockSpec((S, LANES), lambda b, h, i: (b, DIFF_HEADS + h)),
            pl.BlockSpec((S, LANES), lambda b, h, i: (b, h)),
            pl.BlockSpec((1, DIFF_VDIM), lambda b, h, i: (0, 0)),
        ],
        out_specs=pl.BlockSpec((tq, DIFF_VDIM), lambda b, h, i: (b * nq + i, h)),
        out_shape=jax.ShapeDtypeStruct((T, DIFF_HEADS * DIFF_VDIM), BF16),
        scratch_shapes=[
            pltpu.VMEM((S, 2 * DIFF_VDIM), BF16),
            pltpu.VMEM((tq, tq), BF16),
            pltpu.VMEM((tq, 2 * DIFF_VDIM), F32),
            pltpu.VMEM((tq, 2 * DIFF_VDIM), F32),
            pltpu.VMEM((tq, 1), F32),
            pltpu.VMEM((tq, 1), F32),
        ],
        compiler_params=_cparams(("parallel", "parallel", "arbitrary"), 32),
        name="diffattn_bounded" if bounded else "diffattn_online",
    )(lvec, qk, qk, mid, subln_g)


def _gla_kernel(q_ref, k_ref, v_ref, la_ref, sgr_ref, g_ref, tri_ref, ones_ref, o_ref, state, *, blk):
    @pl.when(pl.program_id(2) == 0)
    def _():
        state[...] = jnp.zeros(state.shape, F32)

    la_t = la_ref[...].T
    k_t = k_ref[...].astype(F32).T
    hi, lo = _split_bf16(la_t)
    tri = tri_ref[...]
    ones = ones_ref[...]
    cum_t = _dot(hi, tri) + _dot(lo, tri)
    tot_t = _dot(hi, ones) + _dot(lo, ones)
    kd_t = k_t * jnp.exp(tot_t - cum_t)

    n_chunks = blk // CHUNK
    lane = lax.broadcasted_iota(I32, (1, LANES), 1)
    d_states = []
    for ck in range(n_chunks):
        pair = slice((ck // 2) * LANES, (ck // 2 + 1) * LANES)
        in_chunk = (lane // CHUNK) == (ck % 2)
        kd = jnp.where(in_chunk, kd_t[:, pair], 0.0).astype(BF16)
        d_states.append(_dot(kd, v_ref[pair, :]))

    st = state[...]
    states = []
    for ck in range(n_chunks):
        decay = jnp.exp(tot_t[:, ck * CHUNK:ck * CHUNK + 1])
        st = decay * st + d_states[ck]
        states.append(st.astype(BF16))
    state[...] = st

    o = jnp.concatenate([_dot(q_ref[ck * CHUNK:(ck + 1) * CHUNK, :], states[ck]) for ck in range(n_chunks)],
                        axis=0) * (GLA_KDIM ** -0.5)
    o_ref[...] = (_rms(o, g_ref[...]) * sgr_ref[...].astype(F32)).astype(BF16)


def _gla(mid, log_a, sgr, out_g, *, B, S, blk):
    T = B * S
    ns = S // blk
    kern = functools.partial(_gla_kernel, blk=blk)
    q_col0 = (DIFF_HEADS * DIFF_VDIM) // GLA_KDIM
    k_col0 = q_col0 + GLA_HEADS
    v_col0 = (DIFF_HEADS * DIFF_VDIM + 2 * GLA_HEADS * GLA_KDIM) // GLA_VDIM
    r = jnp.arange(blk, dtype=I32)[:, None]
    c = jnp.arange(blk, dtype=I32)[None, :]
    same = (r // CHUNK) == (c // CHUNK)
    tri = (same & (r <= c)).astype(BF16)
    ones = same.astype(BF16)
    return pl.pallas_call(
        kern,
        grid=(B, GLA_HEADS, ns),
        in_specs=[
            pl.BlockSpec((blk, GLA_KDIM), lambda b, h, s: (b * ns + s, q_col0 + h)),
            pl.BlockSpec((blk, GLA_KDIM), lambda b, h, s: (b * ns + s, k_col0 + h)),
            pl.BlockSpec((blk, GLA_VDIM), lambda b, h, s: (b * ns + s, v_col0 + h)),
            pl.BlockSpec((blk, GLA_KDIM), lambda b, h, s: (b * ns + s, h)),
            pl.BlockSpec((blk, GLA_VDIM), lambda b, h, s: (b * ns + s, h)),
            pl.BlockSpec((1, GLA_VDIM), lambda b, h, s: (0, 0)),
            pl.BlockSpec((blk, blk), lambda b, h, s: (0, 0)),
            pl.BlockSpec((blk, blk), lambda b, h, s: (0, 0)),
        ],
        out_specs=pl.BlockSpec((blk, GLA_VDIM), lambda b, h, s: (b * ns + s, h)),
        out_shape=jax.ShapeDtypeStruct((T, GLA_HEADS * GLA_VDIM), BF16),
        scratch_shapes=[pltpu.VMEM((GLA_KDIM, GLA_VDIM), F32)],
        compiler_params=_cparams(("parallel", "parallel", "arbitrary"), 32),
        name="gla",
    )(mid, mid, mid, log_a, sgr, out_g, tri, ones)


def _resident_w_map(n_j):
    return lambda i, j: (0, jnp.where(i == 0, j, n_j - 1))


def _outproj_kernel(a_ref, b_ref, wa_ref, wb_ref, x_ref, o_ref, w_scr):
    j = pl.program_id(1)
    ka = a_ref.shape[1]

    @pl.when(pl.program_id(0) == 0)
    def _():
        w_scr[j, 0:ka, :] = wa_ref[...].astype(BF16)
        w_scr[j, ka:, :] = wb_ref[...].astype(BF16)

    acc = _dot(a_ref[...], w_scr[j, 0:ka, :]) + _dot(b_ref[...], w_scr[j, ka:, :])
    o_ref[...] = x_ref[...] + acc


def _outproj(a, b, w_out, x2, *, tm, tn):
    T, ka = a.shape
    kb = b.shape[1]
    assert ka == kb
    D = w_out.shape[1]
    n_j = D // tn
    return pl.pallas_call(
        _outproj_kernel,
        grid=(T // tm, n_j),
        in_specs=[
            pl.BlockSpec((tm, ka), lambda i, j: (i, 0)),
            pl.BlockSpec((tm, kb), lambda i, j: (i, 0)),
            pl.BlockSpec((ka, tn), _resident_w_map(n_j)),
            pl.BlockSpec((kb, tn), lambda i, j: (1, jnp.where(i == 0, j, n_j - 1))),
            pl.BlockSpec((tm, tn), lambda i, j: (i, j)),
        ],
        out_specs=pl.BlockSpec((tm, tn), lambda i, j: (i, j)),
        out_shape=jax.ShapeDtypeStruct((T, D), F32),
        scratch_shapes=[pltpu.VMEM((n_j, ka + kb, tn), BF16)],
        compiler_params=_cparams(("arbitrary", "arbitrary"), 48),
        name="outproj",
    )(a, b, w_out, w_out, x2)


def _normproj_kernel(x_ref, g_ref, w_ref, hg_ref, o_ref, n_scr, *w_scr, n_norm):
    j = pl.program_id(1)

    @pl.when(j == 0)
    def _():
        n_scr[...] = _rms(x_ref[...], g_ref[...]).astype(BF16)

    if w_scr:
        @pl.when(pl.program_id(0) == 0)
        def _():
            w_scr[0][j] = w_ref[...].astype(BF16)
        y = _dot(n_scr[...], w_scr[0][j])
    else:
        y = _dot(n_scr[...], w_ref[...].astype(BF16))

    @pl.when(j < n_norm)
    def _():
        o_ref[...] = _rms(y, hg_ref[...]).astype(BF16)

    @pl.when(j >= n_norm)
    def _():
        o_ref[...] = y.astype(BF16)


def _normproj(x2, g, w, head_g, *, tm, tn, n_norm, name):
    T, D = x2.shape
    N = w.shape[1]
    kern = functools.partial(_normproj_kernel, n_norm=n_norm)
    n_j = N // tn
    resident = T // tm > 1
    return pl.pallas_call(
        kern,
        grid=(T // tm, n_j),
        in_specs=[
            pl.BlockSpec((tm, D), lambda i, j: (i, 0)),
            pl.BlockSpec((1, D), lambda i, j: (0, 0)),
            pl.BlockSpec((D, tn), _resident_w_map(n_j) if resident else (lambda i, j: (0, j))),
            pl.BlockSpec((1, tn), lambda i, j: (0, 0)),
        ],
        out_specs=pl.BlockSpec((tm, tn), lambda i, j: (i, j)),
        out_shape=jax.ShapeDtypeStruct((T, N), BF16),
        scratch_shapes=[pltpu.VMEM((tm, D), BF16)] + ([pltpu.VMEM((n_j, D, tn), BF16)] if resident else []),
        compiler_params=_cparams(("arbitrary", "arbitrary"), 48),
        name=name,
    )(x2, g, w, head_g)


def _cross_kernel(q_ref, k_ref, v_ref, w_ref, h_ref, o_ref, att_scr, w_scr, *, hdim):
    j = pl.program_id(1)

    @pl.when(pl.program_id(0) == 0)
    def _():
        w_scr[j] = w_ref[...].astype(BF16)

    @pl.when(j == 0)
    def _():
        for hd in range(CROSS_HEADS):
            cols = slice(hd * hdim, (hd + 1) * hdim)
            s = lax.dot_general(q_ref[:, cols], k_ref[:, cols], NT_DIMS, preferred_element_type=F32)
            p = jnp.exp(s - jnp.max(s, axis=-1, keepdims=True))
            l = jnp.sum(p, axis=-1, keepdims=True)
            att_scr[:, cols] = (_dot(p.astype(BF16), v_ref[:, cols]) / l).astype(BF16)

    o_ref[...] = h_ref[...] + _dot(att_scr[...], w_scr[j])


def _cross(qc, kv, w_co, h1, *, S, n_mem, tm, tn):
    T, D = qc.shape
    per_b = S // tm
    kern = functools.partial(_cross_kernel, hdim=D // CROSS_HEADS)
    n_j = D // tn
    return pl.pallas_call(
        kern,
        grid=(T // tm, n_j),
        in_specs=[
            pl.BlockSpec((tm, D), lambda i, j: (i, 0)),
            pl.BlockSpec((n_mem, D), lambda i, j: (i // per_b, 0)),
            pl.BlockSpec((n_mem, D), lambda i, j: (i // per_b, 1)),
            pl.BlockSpec((D, tn), _resident_w_map(n_j)),
            pl.BlockSpec((tm, tn), lambda i, j: (i, j)),
        ],
        out_specs=pl.BlockSpec((tm, tn), lambda i, j: (i, j)),
        out_shape=jax.ShapeDtypeStruct((T, D), F32),
        scratch_shapes=[pltpu.VMEM((tm, D), BF16), pltpu.VMEM((n_j, D, tn), BF16)],
        compiler_params=_cparams(("arbitrary", "arbitrary"), 48),
        name="cross",
    )(qc, kv, kv, w_co, h1)


R_ROWS = SUBLANES + N_EXPERTS


def _pack_bf16_pairs(xb16):
    c = xb16.shape[1] // 2
    u = lax.bitcast_convert_type(xb16.astype(F32), jnp.uint32)
    return (u[:, :c] >> 16) | (u[:, c:] & jnp.uint32(0xFFFF0000))


def _store_row_tiles(ref, x):
    for g in range(SUBLANES):
        ref[:, g, :] = x[:, g * LANES:(g + 1) * LANES]


def _load_row_tiles(ref):
    return jnp.concatenate([ref[:, g, :] for g in range(SUBLANES)], axis=1)


def _unpack_bf16_pairs_f32(w):
    lo = lax.bitcast_convert_type(w << 16, F32)
    hi = lax.bitcast_convert_type(w & jnp.uint32(0xFFFF0000), F32)
    return lo, hi


def _unpack_bf16_pairs(w):
    lo, hi = _unpack_bf16_pairs_f32(w)
    return lo.astype(BF16), hi.astype(BF16)


def _router_kernel(h_ref, g_ref, wt_ref, b_ref, eid_ref, gate_ref, xn_ref):
    n = _rms(h_ref[...], g_ref[...])
    nh, nl = _split_bf16(n)
    _store_row_tiles(xn_ref, _pack_bf16_pairs(nh))
    wh, wl = _split_bf16(wt_ref[...])
    nt = functools.partial(lax.dot_general, dimension_numbers=NT_DIMS, preferred_element_type=F32)
    lg = nt(wh, nh) + nt(wh, nl) + nt(wl, nh) + b_ref[...]

    tm = lg.shape[1]
    row = lax.broadcasted_iota(I32, (SUBLANES, tm), 0)

    def first_argmax(v, vmax):
        return jnp.min(jnp.where(v == vmax, row, SUBLANES), axis=0, keepdims=True)

    gl = jnp.where(row < N_GROUPS, lg[0:SUBLANES], NEG_INF)
    gmax = jnp.max(gl, axis=0, keepdims=True)
    grp = first_argmax(gl, gmax)
    grp_w = 1.0 / jnp.sum(jnp.exp(gl - gmax), axis=0, keepdims=True)

    sel = jnp.zeros((SUBLANES, tm), F32)
    for gi in range(N_GROUPS):
        lo = SUBLANES + gi * EXPERTS_PER_GROUP
        sel = jnp.where(grp == gi, lg[lo:lo + EXPERTS_PER_GROUP], sel)
    e = jnp.exp(sel - jnp.max(sel, axis=0, keepdims=True))
    prob = e / jnp.sum(e, axis=0, keepdims=True)
    p1 = jnp.max(prob, axis=0, keepdims=True)
    i1 = first_argmax(prob, p1)
    rest = jnp.where(row == i1, -1.0, prob)
    p2 = jnp.max(rest, axis=0, keepdims=True)
    i2 = first_argmax(rest, p2)
    den = p1 + p2
    base = grp * EXPERTS_PER_GROUP
    eid_ref[...] = jnp.where(row == 0, base + i1, jnp.where(row == 1, base + i2, 0))
    gate_ref[...] = jnp.where(row == 0, grp_w * p1 / den, jnp.where(row == 1, grp_w * p2 / den, 0.0))


def _router(h2, g_ffn, w_rt, b_r, *, tm):
    T, D = h2.shape
    return pl.pallas_call(
        _router_kernel,
        grid=(T // tm,),
        in_specs=[
            pl.BlockSpec((tm, D), lambda i: (i, 0)),
            pl.BlockSpec((1, D), lambda i: (0, 0)),
            pl.BlockSpec((R_ROWS, D), lambda i: (0, 0)),
            pl.BlockSpec((R_ROWS, 1), lambda i: (0, 0)),
        ],
        out_specs=[
            pl.BlockSpec((SUBLANES, tm), lambda i: (0, i)),
            pl.BlockSpec((SUBLANES, tm), lambda i: (0, i)),
            pl.BlockSpec((tm, SUBLANES, LANES), lambda i: (i, 0, 0)),
        ],
        out_shape=[
            jax.ShapeDtypeStruct((SUBLANES, T), I32),
            jax.ShapeDtypeStruct((SUBLANES, T), F32),
            jax.ShapeDtypeStruct((T, SUBLANES, LANES), jnp.uint32),
        ],
        compiler_params=_cparams(("parallel",), 32),
        name="router",
    )(h2, g_ffn, w_rt, b_r)


def _rank_kernel(eid_ref, rank_ref, cnt_ref, carry):
    @pl.when(pl.program_id(0) == 0)
    def _():
        carry[...] = jnp.zeros(carry.shape, F32)

    tm = eid_ref.shape[1]
    e0 = eid_ref[0:1, :]
    e1 = eid_ref[1:2, :]
    erow = lax.broadcasted_iota(I32, (N_EXPERTS, tm), 0)
    hit = jnp.where((erow == e0) | (erow == e1), 1.0, 0.0)
    r = lax.broadcasted_iota(I32, (tm, tm), 0)
    c = lax.broadcasted_iota(I32, (tm, tm), 1)
    before = jnp.where(r < c, 1.0, 0.0).astype(BF16)
    pre = _dot(hit.astype(BF16), before) + carry[:, 0:1]
    rank0 = jnp.sum(jnp.where(erow == e0, pre, 0.0), axis=0, keepdims=True)
    rank1 = jnp.sum(jnp.where(erow == e1, pre, 0.0), axis=0, keepdims=True)
    row = lax.broadcasted_iota(I32, (SUBLANES, tm), 0)
    rank_ref[...] = jnp.where(row == 0, rank0, jnp.where(row == 1, rank1, 0.0)).astype(I32)
    total = carry[...] + jnp.sum(hit, axis=1, keepdims=True)
    carry[...] = total
    cnt_ref[...] = total.astype(I32)


def _rank(eid, *, tm):
    T = eid.shape[1]
    return pl.pallas_call(
        _rank_kernel,
        grid=(T // tm,),
        in_specs=[pl.BlockSpec((SUBLANES, tm), lambda i: (0, i))],
        out_specs=[
            pl.BlockSpec((SUBLANES, tm), lambda i: (0, i)),
            pl.BlockSpec((N_EXPERTS, LANES), lambda i: (0, 0)),
        ],
        out_shape=[
            jax.ShapeDtypeStruct((SUBLANES, T), I32),
            jax.ShapeDtypeStruct((N_EXPERTS, LANES), I32),
        ],
        scratch_shapes=[pltpu.VMEM((N_EXPERTS, LANES), F32)],
        compiler_params=_cparams(("arbitrary",), 32),
        name="rank",
    )(eid)


def _dispatch_kernel(dest_ref, pad0_ref, npad_ref, nb_ref, xn_ref, xb_hbm, zbuf, sem, psem,
                     *, n_tok, tb, n_blocks):
    i = pl.program_id(0)
    tm = xn_ref.shape[0]
    base = i * tm

    def row_body(r, carry):
        for k in range(TOP_K):
            pltpu.make_async_copy(xn_ref.at[r], xb_hbm.at[dest_ref[k * n_tok + base + r]], sem).start()
        return carry
    lax.fori_loop(0, tm, row_body, 0, unroll=8)

    @pl.when(i == 0)
    def _():
        _dispatch_fill(pad0_ref, npad_ref, nb_ref, xb_hbm, zbuf, psem, tb=tb, n_blocks=n_blocks)

    for k in range(TOP_K):
        pltpu.make_async_copy(xn_ref, xb_hbm.at[pl.ds(0, tm)], sem).wait()


def _dispatch_fill(pad0_ref, npad_ref, nb_ref, xb_hbm, zbuf, psem, *, tb, n_blocks):
    zbuf[...] = jnp.zeros(zbuf.shape, zbuf.dtype)

    def pad_copy(e):
        n = npad_ref[e]
        return pltpu.make_async_copy(zbuf.at[pl.ds(0, n)], xb_hbm.at[pl.ds(pad0_ref[e], n)], psem.at[0])

    def tail_copy(blk):
        return pltpu.make_async_copy(zbuf, xb_hbm.at[pl.ds(pl.multiple_of(blk * tb, tb), tb)], psem.at[1])

    def for_each_pad(fn):
        def body(e, c):
            @pl.when(npad_ref[e] > 0)
            def _():
                fn(e)
            return c
        lax.fori_loop(0, N_EXPERTS, body, 0)

    def for_each_tail(fn):
        def body(b, c):
            fn(b)
            return c
        lax.fori_loop(nb_ref[0], n_blocks, body, 0)

    for_each_pad(lambda e: pad_copy(e).start())
    for_each_tail(lambda b: tail_copy(b).start())
    for_each_pad(lambda e: pad_copy(e).wait())
    for_each_tail(lambda b: tail_copy(b).wait())


def _dispatch(dest, pad0, npad, nb, xn, *, tb, n_blocks, tm):
    T = xn.shape[0]
    tile = xn.shape[1:]
    kern = functools.partial(_dispatch_kernel, n_tok=T, tb=tb, n_blocks=n_blocks)
    grid_spec = pltpu.PrefetchScalarGridSpec(
        num_scalar_prefetch=4,
        grid=(T // tm,),
        in_specs=[pl.BlockSpec((tm,) + tile, lambda i, *_: (i, 0, 0))],
        out_specs=pl.BlockSpec(memory_space=pl.ANY),
        scratch_shapes=[
            pltpu.VMEM((tb,) + tile, xn.dtype),
            pltpu.SemaphoreType.DMA(()),
            pltpu.SemaphoreType.DMA((2,)),
        ],
    )
    return pl.pallas_call(
        kern,
        grid_spec=grid_spec,
        out_shape=jax.ShapeDtypeStruct((n_blocks * tb,) + tile, xn.dtype),
        compiler_params=_cparams(("arbitrary",), 32),
        name="dispatch",
    )(dest, pad0, npad, nb, xn)


def _expert_kernel(be_ref, first_ref, nxt_ref, nb_ref, x_ref, wg_hbm, wu_hbm, wd_hbm, y_ref,
                   wg_f, wu_f, wd_f, wsem, wg_b, wu_b, wd_b):
    i = pl.program_id(0)
    nb = nb_ref[0]

    def weight_copies(e):
        return (pltpu.make_async_copy(wg_hbm.at[e], wg_f, wsem.at[0]),
                pltpu.make_async_copy(wu_hbm.at[e], wu_f, wsem.at[1]),
                pltpu.make_async_copy(wd_hbm.at[e], wd_f, wsem.at[2]))

    @pl.when(i == 0)
    def _():
        for cp in weight_copies(be_ref[0]):
            cp.start()

    first = (i < nb) & (first_ref[i] == 1)

    @pl.when(first)
    def _():
        for cp in weight_copies(0):
            cp.wait()
        wg_b[...] = wg_f[...].astype(BF16)
        wu_b[...] = wu_f[...].astype(BF16)
        wd_b[...] = wd_f[...].astype(BF16)

    @pl.when(first & (nxt_ref[i] >= 0))
    def _():
        for cp in weight_copies(jnp.maximum(nxt_ref[i], 0)):
            cp.start()

    @pl.when(i < nb)
    def _():
        n_lo, n_hi = _unpack_bf16_pairs(_load_row_tiles(x_ref))
        half = n_lo.shape[1]
        a = _dot(n_lo, wg_b[0:half, :]) + _dot(n_hi, wg_b[half:, :])
        u = _dot(n_lo, wu_b[0:half, :]) + _dot(n_hi, wu_b[half:, :])
        hdn = (a / (1.0 + jnp.exp(-a))) * u
        y_ref[...] = _pack_bf16_pairs(_dot(hdn.astype(BF16), wd_b[...]).astype(BF16))

    @pl.when(i >= nb)
    def _():
        y_ref[...] = jnp.zeros(y_ref.shape, y_ref.dtype)


def _experts(xb, blk_e, blk_first, blk_next, nb, w_gate, w_up, w_down, *, tb, n_blocks):
    D, De = w_gate.shape[1:]
    assert xb.shape[1:] == (SUBLANES, LANES) and D == 2 * SUBLANES * LANES

    def x_map(i, be, first, nxt, nbr):
        return (jnp.minimum(i, nbr[0] - 1), 0, 0)

    grid_spec = pltpu.PrefetchScalarGridSpec(
        num_scalar_prefetch=4,
        grid=(n_blocks,),
        in_specs=[
            pl.BlockSpec((tb, SUBLANES, LANES), x_map),
            pl.BlockSpec(memory_space=pl.ANY),
            pl.BlockSpec(memory_space=pl.ANY),
            pl.BlockSpec(memory_space=pl.ANY),
        ],
        out_specs=pl.BlockSpec((tb, D // 2), lambda i, *_: (i, 0)),
        scratch_shapes=[
            pltpu.VMEM((D, De), F32),
            pltpu.VMEM((D, De), F32),
            pltpu.VMEM((De, D), F32),
            pltpu.SemaphoreType.DMA((3,)),
            pltpu.VMEM((D, De), BF16),
            pltpu.VMEM((D, De), BF16),
            pltpu.VMEM((De, D), BF16),
        ],
    )
    return pl.pallas_call(
        _expert_kernel,
        grid_spec=grid_spec,
        out_shape=jax.ShapeDtypeStruct((n_blocks * tb, D // 2), jnp.uint32),
        compiler_params=_cparams(("arbitrary",), 48),
        name="experts",
    )(blk_e, blk_first, blk_next, nb, xb, w_gate, w_up, w_down)


def _combine_kernel(dest_ref, y_hbm, h_ref, gate_ref, o_ref, ybuf, sem, *, tm, n_tok):
    i = pl.program_id(0)
    n = pl.num_programs(0)

    def row_copy(d, k, r, slot):
        return pltpu.make_async_copy(y_hbm.at[pl.ds(d, 1), :], ybuf.at[slot, k, pl.ds(r, 1), :], sem.at[slot])

    def start_gather(blk, slot):
        def body(r, carry):
            for k in range(TOP_K):
                row_copy(dest_ref[k * n_tok + blk * tm + r], k, r, slot).start()
            return carry
        lax.fori_loop(0, tm, body, 0, unroll=8)

    def wait_gather(slot):
        for k in range(TOP_K):
            pltpu.make_async_copy(y_hbm.at[pl.ds(0, tm), :], ybuf.at[slot, k], sem.at[slot]).wait()

    @pl.when(i == 0)
    def _():
        start_gather(0, 0)

    @pl.when(i + 1 < n)
    def _():
        start_gather(i + 1, (i + 1) % 2)

    slot = i % 2
    wait_gather(slot)
    gt = gate_ref[...]
    half = h_ref.shape[1] // 2
    y0_lo, y0_hi = _unpack_bf16_pairs_f32(ybuf[slot, 0])
    y1_lo, y1_hi = _unpack_bf16_pairs_f32(ybuf[slot, 1])
    o_ref[:, 0:half] = h_ref[:, 0:half] + gt[:, 0:1] * y0_lo + gt[:, 1:2] * y1_lo
    o_ref[:, half:] = h_ref[:, half:] + gt[:, 0:1] * y0_hi + gt[:, 1:2] * y1_hi


def _combine(dest, yb, h2, gate, *, tm):
    T, D = h2.shape
    kern = functools.partial(_combine_kernel, tm=tm, n_tok=T)
    grid_spec = pltpu.PrefetchScalarGridSpec(
        num_scalar_prefetch=1,
        grid=(T // tm,),
        in_specs=[
            pl.BlockSpec(memory_space=pl.ANY),
            pl.BlockSpec((tm, D), lambda i, d: (i, 0)),
            pl.BlockSpec((tm, TOP_K), lambda i, d: (i, 0)),
        ],
        out_specs=pl.BlockSpec((tm, D), lambda i, d: (i, 0)),
        scratch_shapes=[
            pltpu.VMEM((2, TOP_K, tm, D // 2), jnp.uint32),
            pltpu.SemaphoreType.DMA((2,)),
        ],
    )
    return pl.pallas_call(
        kern,
        grid_spec=grid_spec,
        out_shape=jax.ShapeDtypeStruct((T, D), F32),
        compiler_params=_cparams(("arbitrary",), 40),
        name="combine",
    )(dest, yb, h2, gate)


EXPERT_ROWS = 256


def kernel(x, mem, positions, g_attn, w_in, q_norm_g, k_norm_g, lambda_q1, lambda_k1, lambda_q2, lambda_k2, diff_subln_g, gla_w_a2, gla_b_a, gla_out_g, w_out, g_cross, g_mem, w_cq, w_ckv, cq_norm_g, ck_norm_g, w_co, g_ffn, w_router_grp, b_router_grp, w_router_exp, b_router_exp, w_gate, w_up, w_down):
    B, S, D = x.shape
    T = B * S
    n_mem = mem.shape[1]
    l = 0
    x2 = x.reshape(T, D)

    half = DIFF_QKDIM // 2
    freq = ROPE_THETA ** (-jnp.arange(half, dtype=F32) / half)
    freq = jnp.tile(freq, LANES // half)[None, :]
    q_scale = math.log2(math.e) * DIFF_QKDIM ** -0.5
    qkg = jnp.stack([jnp.tile(q_norm_g[l], 2) * q_scale, jnp.tile(k_norm_g[l], 2)])
    score_bound = 1.01 * DIFF_QKDIM * q_scale * jnp.max(jnp.abs(q_norm_g[l])) * jnp.max(jnp.abs(k_norm_g[l]))
    lvec = jnp.stack([lambda_q1[l], lambda_k1[l], lambda_q2[l], lambda_k2[l]])

    qk, mid, log_a, sgr = _inproj(x2, g_attn[l][None], positions.reshape(T, 1), freq, qkg, w_in[l].T,
                                  gla_w_a2[l], gla_b_a[l][None], tm=1024)
    diffattn = functools.partial(_diffattn, lvec, qk, mid, diff_subln_g[l][None], B=B, S=S, tq=512)
    mix_d = lax.cond(score_bound <= SCORE_BOUND,
                     functools.partial(diffattn, bounded=True), functools.partial(diffattn, bounded=False))
    mix_g = _gla(mid, log_a, sgr, gla_out_g[l][None], B=B, S=S, blk=512)
    h1 = _outproj(mix_d, mix_g, w_out[l], x2, tm=1024, tn=512)

    hdim = D // CROSS_HEADS
    qc = _normproj(h1, g_cross[l][None], w_cq[l], cq_norm_g[l][None] * (hdim ** -0.5),
                   tm=1024, tn=hdim, n_norm=CROSS_HEADS, name="cq")
    kv = _normproj(mem.reshape(B * n_mem, D), g_mem[l][None], w_ckv[l], ck_norm_g[l][None],
                   tm=B * n_mem, tn=hdim, n_norm=CROSS_HEADS, name="ckv")
    h2 = _cross(qc, kv, w_co[l], h1, S=S, n_mem=n_mem, tm=512, tn=512)

    w_rt = jnp.concatenate([w_router_grp[l].T, jnp.zeros((SUBLANES - N_GROUPS, D), F32), w_router_exp[l].T])
    b_r = jnp.concatenate([b_router_grp[l], jnp.zeros((SUBLANES - N_GROUPS,), F32), b_router_exp[l]])[:, None]
    eid, gate, xn = _router(h2, g_ffn[l][None], w_rt, b_r, tm=512)
    rank, cnt = _rank(eid, tm=512)

    tb = EXPERT_ROWS
    n_blocks = (T * TOP_K + N_EXPERTS * (tb - 1) + tb - 1) // tb
    counts = cnt[:, 0]
    pcounts = ((counts + tb - 1) // tb) * tb
    pends = jnp.cumsum(pcounts)
    pstarts = pends - pcounts
    nb = (pends[-1:] // tb).astype(I32)
    eids = jnp.arange(N_EXPERTS, dtype=I32)
    blk_start = jnp.arange(n_blocks, dtype=I32) * tb
    blk_e = jnp.minimum(jnp.sum(pends[None, :] <= blk_start[:, None], axis=1), N_EXPERTS - 1).astype(I32)
    blk_first = jnp.concatenate([jnp.ones((1,), I32), (blk_e[1:] != blk_e[:-1]).astype(I32)])
    used = jnp.where(counts > 0, eids, N_EXPERTS)
    next_used = jnp.concatenate([lax.cummin(used[::-1])[::-1][1:], jnp.full((1,), N_EXPERTS, I32)])
    next_used = jnp.where(next_used < N_EXPERTS, next_used, -1)
    blk_next = jnp.sum(jnp.where(blk_e[:, None] == eids, next_used, 0), axis=1).astype(I32)
    pick = eid[:TOP_K]
    pstart_of = jnp.sum(jnp.where(pick[..., None] == eids, pstarts, 0), axis=-1)
    dest = (pstart_of + rank[:TOP_K]).astype(I32).reshape(-1)
    pad0 = (pstarts + counts).astype(I32)
    npad = (pcounts - counts).astype(I32)

    xb = _dispatch(dest, pad0, npad, nb, xn, tb=tb, n_blocks=n_blocks, tm=512)
    yb = _experts(xb, blk_e, blk_first, blk_next, nb, w_gate[l], w_up[l], w_down[l], tb=tb, n_blocks=n_blocks)
    out = _combine(dest, yb, h2, gate[:TOP_K].T, tm=256)
    return out.reshape(B, S, D)
```

```python
import functools
import math

import jax
import jax.numpy as jnp
from jax import lax
from jax.experimental import pallas as pl
from jax.experimental.pallas import tpu as pltpu

F32 = jnp.float32
BF16 = jnp.bfloat16
I32 = jnp.int32

LANES = 128
SUBLANES = 8

CHUNK = 64
ROPE_THETA = 10000.0
NORM_EPS = 1e-6
NEG_INF = -1e30
DIFF_HEADS = 8
DIFF_VDIM = 128
DIFF_QKDIM = 64
GLA_HEADS = 4
GLA_VDIM = 256
GLA_KDIM = 128
GLA_GATE_RANK = 16
GLA_TAU = 16.0
CROSS_HEADS = 4
N_GROUPS = 4
EXPERTS_PER_GROUP = 8
N_EXPERTS = N_GROUPS * EXPERTS_PER_GROUP
TOP_K = 2
LAM_INIT = 0.8 - 0.6 * math.exp(-0.3 * 0)

NT_DIMS = (((1,), (1,)), ((), ()))


def _cparams(semantics, vmem_mib):
    return pltpu.CompilerParams(dimension_semantics=semantics,
                                vmem_limit_bytes=vmem_mib * 1024 * 1024)


def _dot(a, b):
    return jnp.dot(a, b, preferred_element_type=F32)


def _dot_nt(a, b):
    return lax.dot_general(a, b, NT_DIMS, preferred_element_type=F32)


def _rms(x, g):
    ms = jnp.mean(x * x, axis=-1, keepdims=True)
    return x * lax.rsqrt(ms + NORM_EPS) * g


def _split_bf16(x):
    hi = x.astype(BF16)
    lo = (x - hi.astype(F32)).astype(BF16)
    return hi, lo


TN = 512
J_QK = 4
J_MID = 6
J_LR = J_QK + J_MID
J_GR = J_LR + 1
N_J = J_GR + 2


def _inproj_kernel(x_ref, g_ref, pos_ref, freq_ref, qkg_ref, w_ref, wgr_ref, wlr_ref, wa2_ref, ba_ref,
                   qk_ref, mid_ref, loga_ref, sgr_ref, n_scr, cos_scr, sin_scr, y_scr):
    j = pl.program_id(1)

    @pl.when(j == 0)
    def _():
        n_scr[...] = _rms(x_ref[...], g_ref[...]).astype(BF16)
        ang = pos_ref[...].astype(F32) * freq_ref[...]
        cos_scr[...] = jnp.cos(ang)
        sin_scr[...] = jnp.sin(ang)

    def qk_epilogue(jq):
        y_prev = y_scr.at[jq % 2]
        lane = lax.broadcasted_iota(I32, (1, LANES), 1)
        low_seg = lane < DIFF_QKDIM
        first_half = (lane % DIFF_QKDIM) < (DIFF_QKDIM // 2)
        gain = qkg_ref[jq // (J_QK // 2):jq // (J_QK // 2) + 1, :]
        cos = cos_scr[...]
        sin = sin_scr[...]
        for c in range(TN // LANES):
            yb = y_prev[:, c * LANES:(c + 1) * LANES]
            y2 = yb * yb
            s_lo = jnp.sum(jnp.where(low_seg, y2, 0.0), axis=-1, keepdims=True)
            s_hi = jnp.sum(jnp.where(low_seg, 0.0, y2), axis=-1, keepdims=True)
            ms = jnp.where(low_seg, s_lo, s_hi) * (1.0 / DIFF_QKDIM)
            yn = yb * lax.rsqrt(ms + NORM_EPS) * gain
            rot = jnp.where(first_half,
                            -pltpu.roll(yn, LANES - DIFF_QKDIM // 2, 1),
                            pltpu.roll(yn, DIFF_QKDIM // 2, 1))
            qk_ref[:, c * LANES:(c + 1) * LANES] = (yn * cos + rot * sin).astype(BF16)

    for jq in range(J_QK + 1):
        @pl.when(j == jq)
        def _():
            y = _dot_nt(n_scr[...], w_ref[...].astype(BF16))
            if jq > 0:
                qk_epilogue(jq - 1)
            if jq < J_QK:
                y_scr[jq % 2] = y
            else:
                mid_ref[...] = y.astype(BF16)

    @pl.when((j > J_QK) & (j < J_LR))
    def _():
        mid_ref[...] = _dot_nt(n_scr[...], w_ref[...].astype(BF16)).astype(BF16)

    @pl.when(j == J_LR)
    def _():
        lr = _dot_nt(n_scr[...], wlr_ref[...].astype(BF16))
        z = _dot(lr.astype(BF16), wa2_ref[...].astype(BF16)) + ba_ref[...]
        log_sig = jnp.minimum(z, 0.0) - jnp.log(1.0 + jnp.exp(-jnp.abs(z)))
        loga_ref[...] = log_sig * (1.0 / GLA_TAU)

    @pl.when(j >= J_GR)
    def _():
        y = _dot_nt(n_scr[...], wgr_ref[...].astype(BF16))
        sgr_ref[...] = (y / (1.0 + jnp.exp(-y))).astype(BF16)


def _inproj(x2, g_attn, pos2, freq, qkg, w_t, w_a2, b_a, *, tm):
    T, D = x2.shape
    n_mid = J_MID * TN
    n_gk = GLA_HEADS * GLA_KDIM
    lr0 = J_LR * TN
    gr0 = lr0 + GLA_GATE_RANK
    n_gr = w_t.shape[0] - gr0
    assert n_gr == 2 * TN and lr0 % GLA_GATE_RANK == 0
    return pl.pallas_call(
        _inproj_kernel,
        grid=(T // tm, N_J),
        in_specs=[
            pl.BlockSpec((tm, D), lambda i, j: (i, 0)),
            pl.BlockSpec((1, D), lambda i, j: (0, 0)),
            pl.BlockSpec((tm, 1), lambda i, j: (i, 0)),
            pl.BlockSpec((1, LANES), lambda i, j: (0, 0)),
            pl.BlockSpec((2, LANES), lambda i, j: (0, 0)),
            pl.BlockSpec((TN, D), lambda i, j: (jnp.minimum(j, J_LR - 1), 0)),
            pl.BlockSpec((pl.Element(TN), pl.Element(D)),
                         lambda i, j: (pl.multiple_of(gr0 + TN * jnp.clip(j - J_GR, 0, 1), SUBLANES), 0)),
            pl.BlockSpec((GLA_GATE_RANK, D), lambda i, j: (lr0 // GLA_GATE_RANK, 0)),
            pl.BlockSpec((GLA_GATE_RANK, n_gk), lambda i, j: (0, 0)),
            pl.BlockSpec((1, n_gk), lambda i, j: (0, 0)),
        ],
        out_specs=[
            pl.BlockSpec((tm, TN), lambda i, j: (i, jnp.clip(j - 1, 0, J_QK - 1))),
            pl.BlockSpec((tm, TN), lambda i, j: (i, jnp.clip(j - J_QK, 0, J_MID - 1))),
            pl.BlockSpec((tm, n_gk), lambda i, j: (i, 0)),
            pl.BlockSpec((tm, TN), lambda i, j: (i, jnp.clip(j - J_GR, 0, 1))),
        ],
        out_shape=[
            jax.ShapeDtypeStruct((T, J_QK * TN), BF16),
            jax.ShapeDtypeStruct((T, n_mid), BF16),
            jax.ShapeDtypeStruct((T, n_gk), F32),
            jax.ShapeDtypeStruct((T, n_gr), BF16),
        ],
        scratch_shapes=[
            pltpu.VMEM((tm, D), BF16),
            pltpu.VMEM((tm, LANES), F32),
            pltpu.VMEM((tm, LANES), F32),
            pltpu.VMEM((2, tm, TN), F32),
        ],
        compiler_params=_cparams(("parallel", "arbitrary"), 56),
        name="inproj",
    )(x2, g_attn, pos2, freq, qkg, w_t, w_t, w_t, w_a2, b_a)


SCORE_BOUND = 80.0


def _diffattn_kernel(lv_ref, q_ref, k_ref, v_ref, sg_ref, o_ref, vext, diag_mask, acc1, acc2, m1, m2,
                     *, tq, bounded):
    i = pl.program_id(2)

    @pl.when(i == 0)
    def _():
        S = v_ref.shape[0]
        lane_s = lax.broadcasted_iota(I32, (S, LANES), 1)
        vext[:, 0:DIFF_VDIM] = v_ref[...]
        vext[:, DIFF_VDIM:] = jnp.where(lane_s == 0, 1.0, 0.0).astype(BF16)
        row_chunk = lax.broadcasted_iota(I32, (tq, tq), 0) // CHUNK
        col_chunk = lax.broadcasted_iota(I32, (tq, tq), 1) // CHUNK
        diag_mask[...] = jnp.where(col_chunk <= row_chunk, 1.0, 0.0).astype(BF16)

    q = q_ref[...]
    lane = lax.broadcasted_iota(I32, (1, LANES), 1)
    zero = jnp.zeros_like(q)
    q1 = jnp.where(lane < DIFF_QKDIM, q, zero)
    q2 = jnp.where(lane < DIFF_QKDIM, zero, q)

    half = tq // 2
    comps = ((q1, acc1, m1), (q2, acc2, m2))

    def step(j):
        start = pl.multiple_of(j * tq, tq)
        k = k_ref[pl.ds(start, tq), :]
        v = vext[pl.ds(start, tq), :]
        for qc, acc, m in comps:
            s = _dot_nt(qc, k)
            if bounded:
                acc[...] += _dot(jnp.exp2(s).astype(BF16), v)
            else:
                m_old = m[...]
                m_new = jnp.maximum(m_old, jnp.max(s, axis=-1, keepdims=True))
                p = jnp.exp2(s - m_new)
                acc[...] = jnp.exp2(m_old - m_new) * acc[...] + _dot(p.astype(BF16), v)
                m[...] = m_new

    def diag_step():
        start = pl.multiple_of(i * tq, tq)
        for qc, acc, m in comps:
            for rows, n_keys in ((slice(0, half), half), (slice(half, tq), tq)):
                k = k_ref[pl.ds(start, n_keys), :]
                v = vext[pl.ds(start, n_keys), :]
                mask = diag_mask[rows, 0:n_keys]
                s = _dot_nt(qc[rows], k)
                if bounded:
                    acc[rows, :] = _dot(jnp.exp2(s).astype(BF16) * mask, v)
                else:
                    s = jnp.where(mask > 0, s, NEG_INF)
                    m_new = jnp.max(s, axis=-1, keepdims=True)
                    acc[rows, :] = _dot(jnp.exp2(s - m_new).astype(BF16), v)
                    m[rows, :] = m_new

    @pl.when(i % 2 == 1)
    def _():
        diag_step()
        step(i - 1)

    @pl.when(i % 2 == 0)
    def _():
        diag_step()

    n_quads = i // 4

    def body(jj, carry):
        for u in range(4):
            step(4 * jj + u)
        return carry

    lax.fori_loop(0, n_quads, body, 0)

    @pl.when((i // 2) % 2 == 1)
    def _():
        step(4 * n_quads)
        step(4 * n_quads + 1)

    lv = lv_ref[...]
    lam = (jnp.exp(jnp.sum(lv[0:1] * lv[1:2], axis=-1, keepdims=True))
           - jnp.exp(jnp.sum(lv[2:3] * lv[3:4], axis=-1, keepdims=True)) + LAM_INIT)
    a1 = acc1[...]
    a2 = acc2[...]
    o = (a1[:, :DIFF_VDIM] / a1[:, DIFF_VDIM:DIFF_VDIM + 1]
         - lam * (a2[:, :DIFF_VDIM] / a2[:, DIFF_VDIM:DIFF_VDIM + 1]))
    o_ref[...] = (_rms(o, sg_ref[...]) * (1.0 - LAM_INIT)).astype(BF16)


def _diffattn(lvec, qk, mid, subln_g, *, B, S, tq, bounded):
    T = B * S
    nq = S // tq
    kern = functools.partial(_diffattn_kernel, tq=tq, bounded=bounded)
    return pl.pallas_call(
        kern,
        grid=(B, DIFF_HEADS, nq),
        in_specs=[
            pl.BlockSpec((4, DIFF_QKDIM), lambda b, h, i: (0, 0)),
            pl.BlockSpec((tq, LANES), lambda b, h, i: (b * nq + i, h)),
            pl.BlockSpec((S, LANES), lambda b, h, i: (b, DIFF_HEADS + h)),
            pl.BlockSpec((S, LANES), lambda b, h, i: (b, h)),
            pl.BlockSpec((1, DIFF_VDIM), lambda b, h, i: (0, 0)),
        ],
        out_specs=pl.BlockSpec((tq, DIFF_VDIM), lambda b, h, i: (b * nq + i, h)),
        out_shape=jax.ShapeDtypeStruct((T, DIFF_HEADS * DIFF_VDIM), BF16),
        scratch_shapes=[
            pltpu.VMEM((S, 2 * DIFF_VDIM), BF16),
            pltpu.VMEM((tq, tq), BF16),
            pltpu.VMEM((tq, 2 * DIFF_VDIM), F32),
            pltpu.VMEM((tq, 2 * DIFF_VDIM), F32),
            pltpu.VMEM((tq, 1), F32),
            pltpu.VMEM((tq, 1), F32),
        ],
        compiler_params=_cparams(("parallel", "parallel", "arbitrary"), 32),
        name="diffattn_bounded" if bounded else "diffattn_online",
    )(lvec, qk, qk, mid, subln_g)


def _gla_kernel(q_ref, k_ref, v_ref, la_ref, sgr_ref, g_ref, tri_ref, ones_ref, o_ref, state, *, blk):
    @pl.when(pl.program_id(2) == 0)
    def _():
        state[...] = jnp.zeros(state.shape, F32)

    la_t = la_ref[...].T
    k_t = k_ref[...].astype(F32).T
    hi, lo = _split_bf16(la_t)
    tri = tri_ref[...]
    ones = ones_ref[...]
    cum_t = _dot(hi, tri) + _dot(lo, tri)
    tot_t = _dot(hi, ones) + _dot(lo, ones)
    kd_t = k_t * jnp.exp(tot_t - cum_t)

    n_chunks = blk // CHUNK
    lane = lax.broadcasted_iota(I32, (1, LANES), 1)
    d_states = []
    for ck in range(n_chunks):
        pair = slice((ck // 2) * LANES, (ck // 2 + 1) * LANES)
        in_chunk = (lane // CHUNK) == (ck % 2)
        kd = jnp.where(in_chunk, kd_t[:, pair], 0.0).astype(BF16)
        d_states.append(_dot(kd, v_ref[pair, :]))

    st = state[...]
    states = []
    for ck in range(n_chunks):
        decay = jnp.exp(tot_t[:, ck * CHUNK:ck * CHUNK + 1])
        st = decay * st + d_states[ck]
        states.append(st.astype(BF16))
    state[...] = st

    o = jnp.concatenate([_dot(q_ref[ck * CHUNK:(ck + 1) * CHUNK, :], states[ck]) for ck in range(n_chunks)],
                        axis=0) * (GLA_KDIM ** -0.5)
    o_ref[...] = (_rms(o, g_ref[...]) * sgr_ref[...].astype(F32)).astype(BF16)


def _gla(mid, log_a, sgr, out_g, *, B, S, blk):
    T = B * S
    ns = S // blk
    kern = functools.partial(_gla_kernel, blk=blk)
    q_col0 = (DIFF_HEADS * DIFF_VDIM) // GLA_KDIM
    k_col0 = q_col0 + GLA_HEADS
    v_col0 = (DIFF_HEADS * DIFF_VDIM + 2 * GLA_HEADS * GLA_KDIM) // GLA_VDIM
    r = jnp.arange(blk, dtype=I32)[:, None]
    c = jnp.arange(blk, dtype=I32)[None, :]
    same = (r // CHUNK) == (c // CHUNK)
    tri = (same & (r <= c)).astype(BF16)
    ones = same.astype(BF16)
    return pl.pallas_call(
        kern,
        grid=(B, GLA_HEADS, ns),
        in_specs=[
            pl.BlockSpec((blk, GLA_KDIM), lambda b, h, s: (b * ns + s, q_col0 + h)),
            pl.BlockSpec((blk, GLA_KDIM), lambda b, h, s: (b * ns + s, k_col0 + h)),
            pl.BlockSpec((blk, GLA_VDIM), lambda b, h, s: (b * ns + s, v_col0 + h)),
            pl.BlockSpec((blk, GLA_KDIM), lambda b, h, s: (b * ns + s, h)),
            pl.BlockSpec((blk, GLA_VDIM), lambda b, h, s: (b * ns + s, h)),
            pl.BlockSpec((1, GLA_VDIM), lambda b, h, s: (0, 0)),
            pl.BlockSpec((blk, blk), lambda b, h, s: (0, 0)),
            pl.BlockSpec((blk, blk), lambda b, h, s: (0, 0)),
        ],
        out_specs=pl.BlockSpec((blk, GLA_VDIM), lambda b, h, s: (b * ns + s, h)),
        out_shape=jax.ShapeDtypeStruct((T, GLA_HEADS * GLA_VDIM), BF16),
        scratch_shapes=[pltpu.VMEM((GLA_KDIM, GLA_VDIM), F32)],
        compiler_params=_cparams(("parallel", "parallel", "arbitrary"), 32),
        name="gla",
    )(mid, mid, mid, log_a, sgr, out_g, tri, ones)


def _resident_w_map(n_j):
    return lambda i, j: (0, jnp.where(i == 0, j, n_j - 1))


def _outproj_kernel(a_ref, b_ref, wa_ref, wb_ref, x_ref, o_ref, w_scr):
    j = pl.program_id(1)
    ka = a_ref.shape[1]

    @pl.when(pl.program_id(0) == 0)
    def _():
        w_scr[j, 0:ka, :] = wa_ref[...].astype(BF16)
        w_scr[j, ka:, :] = wb_ref[...].astype(BF16)

    acc = _dot(a_ref[...], w_scr[j, 0:ka, :]) + _dot(b_ref[...], w_scr[j, ka:, :])
    o_ref[...] = x_ref[...] + acc


def _outproj(a, b, w_out, x2, *, tm, tn):
    T, ka = a.shape
    kb = b.shape[1]
    assert ka == kb
    D = w_out.shape[1]
    n_j = D // tn
    return pl.pallas_call(
        _outproj_kernel,
        grid=(T // tm, n_j),
        in_specs=[
            pl.BlockSpec((tm, ka), lambda i, j: (i, 0)),
            pl.BlockSpec((tm, kb), lambda i, j: (i, 0)),
            pl.BlockSpec((ka, tn), _resident_w_map(n_j)),
            pl.BlockSpec((kb, tn), lambda i, j: (1, jnp.where(i == 0, j, n_j - 1))),
            pl.BlockSpec((tm, tn), lambda i, j: (i, j)),
        ],
        out_specs=pl.BlockSpec((tm, tn), lambda i, j: (i, j)),
        out_shape=jax.ShapeDtypeStruct((T, D), F32),
        scratch_shapes=[pltpu.VMEM((n_j, ka + kb, tn), BF16)],
        compiler_params=_cparams(("arbitrary", "arbitrary"), 48),
        name="outproj",
    )(a, b, w_out, w_out, x2)


def _normproj_kernel(x_ref, g_ref, w_ref, hg_ref, o_ref, n_scr, *w_scr, n_norm):
    j = pl.program_id(1)

    @pl.when(j == 0)
    def _():
        n_scr[...] = _rms(x_ref[...], g_ref[...]).astype(BF16)

    if w_scr:
        @pl.when(pl.program_id(0) == 0)
        def _():
            w_scr[0][j] = w_ref[...].astype(BF16)
        y = _dot(n_scr[...], w_scr[0][j])
    else:
        y = _dot(n_scr[...], w_ref[...].astype(BF16))

    @pl.when(j < n_norm)
    def _():
        o_ref[...] = _rms(y, hg_ref[...]).astype(BF16)

    @pl.when(j >= n_norm)
    def _():
        o_ref[...] = y.astype(BF16)


def _normproj(x2, g, w, head_g, *, tm, tn, n_norm, name):
    T, D = x2.shape
    N = w.shape[1]
    kern = functools.partial(_normproj_kernel, n_norm=n_norm)
    n_j = N // tn
    resident = T // tm > 1
    return pl.pallas_call(
        kern,
        grid=(T // tm, n_j),
        in_specs=[
            pl.BlockSpec((tm, D), lambda i, j: (i, 0)),
            pl.BlockSpec((1, D), lambda i, j: (0, 0)),
            pl.BlockSpec((D, tn), _resident_w_map(n_j) if resident else (lambda i, j: (0, j))),
            pl.BlockSpec((1, tn), lambda i, j: (0, 0)),
        ],
        out_specs=pl.BlockSpec((tm, tn), lambda i, j: (i, j)),
        out_shape=jax.ShapeDtypeStruct((T, N), BF16),
        scratch_shapes=[pltpu.VMEM((tm, D), BF16)] + ([pltpu.VMEM((n_j, D, tn), BF16)] if resident else []),
        compiler_params=_cparams(("arbitrary", "arbitrary"), 48),
        name=name,
    )(x2, g, w, head_g)


def _cross_kernel(q_ref, k_ref, v_ref, w_ref, h_ref, o_ref, att_scr, w_scr, *, hdim):
    j = pl.program_id(1)

    @pl.when(pl.program_id(0) == 0)
    def _():
        w_scr[j] = w_ref[...].astype(BF16)

    @pl.when(j == 0)
    def _():
        for hd in range(CROSS_HEADS):
            cols = slice(hd * hdim, (hd + 1) * hdim)
            s = lax.dot_general(q_ref[:, cols], k_ref[:, cols], NT_DIMS, preferred_element_type=F32)
            p = jnp.exp(s - jnp.max(s, axis=-1, keepdims=True))
            l = jnp.sum(p, axis=-1, keepdims=True)
            att_scr[:, cols] = (_dot(p.astype(BF16), v_ref[:, cols]) / l).astype(BF16)

    o_ref[...] = h_ref[...] + _dot(att_scr[...], w_scr[j])


def _cross(qc, kv, w_co, h1, *, S, n_mem, tm, tn):
    T, D = qc.shape
    per_b = S // tm
    kern = functools.partial(_cross_kernel, hdim=D // CROSS_HEADS)
    n_j = D // tn
    return pl.pallas_call(
        kern,
        grid=(T // tm, n_j),
        in_specs=[
            pl.BlockSpec((tm, D), lambda i, j: (i, 0)),
            pl.BlockSpec((n_mem, D), lambda i, j: (i // per_b, 0)),
            pl.BlockSpec((n_mem, D), lambda i, j: (i // per_b, 1)),
            pl.BlockSpec((D, tn), _resident_w_map(n_j)),
            pl.BlockSpec((tm, tn), lambda i, j: (i, j)),
        ],
        out_specs=pl.BlockSpec((tm, tn), lambda i, j: (i, j)),
        out_shape=jax.ShapeDtypeStruct((T, D), F32),
        scratch_shapes=[pltpu.VMEM((tm, D), BF16), pltpu.VMEM((n_j, D, tn), BF16)],
        compiler_params=_cparams(("arbitrary", "arbitrary"), 48),
        name="cross",
    )(qc, kv, kv, w_co, h1)


R_ROWS = SUBLANES + N_EXPERTS


def _pack_bf16_pairs(xb16):
    c = xb16.shape[1] // 2
    u = lax.bitcast_convert_type(xb16.astype(F32), jnp.uint32)
    return (u[:, :c] >> 16) | (u[:, c:] & jnp.uint32(0xFFFF0000))


def _store_row_tiles(ref, x):
    for g in range(SUBLANES):
        ref[:, g, :] = x[:, g * LANES:(g + 1) * LANES]


def _load_row_tiles(ref):
    return jnp.concatenate([ref[:, g, :] for g in range(SUBLANES)], axis=1)


def _unpack_bf16_pairs_f32(w):
    lo = lax.bitcast_convert_type(w << 16, F32)
    hi = lax.bitcast_convert_type(w & jnp.uint32(0xFFFF0000), F32)
    return lo, hi


def _unpack_bf16_pairs(w):
    lo, hi = _unpack_bf16_pairs_f32(w)
    return lo.astype(BF16), hi.astype(BF16)


def _router_kernel(h_ref, g_ref, wt_ref, b_ref, eid_ref, gate_ref, xn_ref):
    n = _rms(h_ref[...], g_ref[...])
    nh, nl = _split_bf16(n)
    _store_row_tiles(xn_ref, _pack_bf16_pairs(nh))
    wh, wl = _split_bf16(wt_ref[...])
    nt = functools.partial(lax.dot_general, dimension_numbers=NT_DIMS, preferred_element_type=F32)
    lg = nt(wh, nh) + nt(wh, nl) + nt(wl, nh) + b_ref[...]

    tm = lg.shape[1]
    row = lax.broadcasted_iota(I32, (SUBLANES, tm), 0)

    def first_argmax(v, vmax):
        return jnp.min(jnp.where(v == vmax, row, SUBLANES), axis=0, keepdims=True)

    gl = jnp.where(row < N_GROUPS, lg[0:SUBLANES], NEG_INF)
    gmax = jnp.max(gl, axis=0, keepdims=True)
    grp = first_argmax(gl, gmax)
    grp_w = 1.0 / jnp.sum(jnp.exp(gl - gmax), axis=0, keepdims=True)

    sel = jnp.zeros((SUBLANES, tm), F32)
    for gi in range(N_GROUPS):
        lo = SUBLANES + gi * EXPERTS_PER_GROUP
        sel = jnp.where(grp == gi, lg[lo:lo + EXPERTS_PER_GROUP], sel)
    e = jnp.exp(sel - jnp.max(sel, axis=0, keepdims=True))
    prob = e / jnp.sum(e, axis=0, keepdims=True)
    p1 = jnp.max(prob, axis=0, keepdims=True)
    i1 = first_argmax(prob, p1)
    rest = jnp.where(row == i1, -1.0, prob)
    p2 = jnp.max(rest, axis=0, keepdims=True)
    i2 = first_argmax(rest, p2)
    den = p1 + p2
    base = grp * EXPERTS_PER_GROUP
    eid_ref[...] = jnp.where(row == 0, base + i1, jnp.where(row == 1, base + i2, 0))
    gate_ref[...] = jnp.where(row == 0, grp_w * p1 / den, jnp.where(row == 1, grp_w * p2 / den, 0.0))


def _router(h2, g_ffn, w_rt, b_r, *, tm):
    T, D = h2.shape
    return pl.pallas_call(
        _router_kernel,
        grid=(T // tm,),
        in_specs=[
            pl.BlockSpec((tm, D), lambda i: (i, 0)),
            pl.BlockSpec((1, D), lambda i: (0, 0)),
            pl.BlockSpec((R_ROWS, D), lambda i: (0, 0)),
            pl.BlockSpec((R_ROWS, 1), lambda i: (0, 0)),
        ],
        out_specs=[
            pl.BlockSpec((SUBLANES, tm), lambda i: (0, i)),
            pl.BlockSpec((SUBLANES, tm), lambda i: (0, i)),
            pl.BlockSpec((tm, SUBLANES, LANES), lambda i: (i, 0, 0)),
        ],
        out_shape=[
            jax.ShapeDtypeStruct((SUBLANES, T), I32),
            jax.ShapeDtypeStruct((SUBLANES, T), F32),
            jax.ShapeDtypeStruct((T, SUBLANES, LANES), jnp.uint32),
        ],
        compiler_params=_cparams(("parallel",), 32),
        name="router",
    )(h2, g_ffn, w_rt, b_r)


def _rank_kernel(eid_ref, rank_ref, cnt_ref, carry):
    @pl.when(pl.program_id(0) == 0)
    def _():
        carry[...] = jnp.zeros(carry.shape, F32)

    tm = eid_ref.shape[1]
    e0 = eid_ref[0:1, :]
    e1 = eid_ref[1:2, :]
    erow = lax.broadcasted_iota(I32, (N_EXPERTS, tm), 0)
    hit = jnp.where((erow == e0) | (erow == e1), 1.0, 0.0)
    r = lax.broadcasted_iota(I32, (tm, tm), 0)
    c = lax.broadcasted_iota(I32, (tm, tm), 1)
    before = jnp.where(r < c, 1.0, 0.0).astype(BF16)
    pre = _dot(hit.astype(BF16), before) + carry[:, 0:1]
    rank0 = jnp.sum(jnp.where(erow == e0, pre, 0.0), axis=0, keepdims=True)
    rank1 = jnp.sum(jnp.where(erow == e1, pre, 0.0), axis=0, keepdims=True)
    row = lax.broadcasted_iota(I32, (SUBLANES, tm), 0)
    rank_ref[...] = jnp.where(row == 0, rank0, jnp.where(row == 1, rank1, 0.0)).astype(I32)
    total = carry[...] + jnp.sum(hit, axis=1, keepdims=True)
    carry[...] = total
    cnt_ref[...] = total.astype(I32)


def _rank(eid, *, tm):
    T = eid.shape[1]
    return pl.pallas_call(
        _rank_kernel,
        grid=(T // tm,),
        in_specs=[pl.BlockSpec((SUBLANES, tm), lambda i: (0, i))],
        out_specs=[
            pl.BlockSpec((SUBLANES, tm), lambda i: (0, i)),
            pl.BlockSpec((N_EXPERTS, LANES), lambda i: (0, 0)),
        ],
        out_shape=[
            jax.ShapeDtypeStruct((SUBLANES, T), I32),
            jax.ShapeDtypeStruct((N_EXPERTS, LANES), I32),
        ],
        scratch_shapes=[pltpu.VMEM((N_EXPERTS, LANES), F32)],
        compiler_params=_cparams(("arbitrary",), 32),
        name="rank",
    )(eid)


def _dispatch_kernel(dest_ref, pad0_ref, npad_ref, nb_ref, xn_ref, xb_hbm, zbuf, sem, psem,
                     *, n_tok, tb, n_blocks):
    i = pl.program_id(0)
    tm = xn_ref.shape[0]
    base = i * tm

    def row_body(r, carry):
        for k in range(TOP_K):
            pltpu.make_async_copy(xn_ref.at[r], xb_hbm.at[dest_ref[k * n_tok + base + r]], sem).start()
        return carry
    lax.fori_loop(0, tm, row_body, 0, unroll=8)

    @pl.when(i == 0)
    def _():
        _dispatch_fill(pad0_ref, npad_ref, nb_ref, xb_hbm, zbuf, psem, tb=tb, n_blocks=n_blocks)

    for k in range(TOP_K):
        pltpu.make_async_copy(xn_ref, xb_hbm.at[pl.ds(0, tm)], sem).wait()


def _dispatch_fill(pad0_ref, npad_ref, nb_ref, xb_hbm, zbuf, psem, *, tb, n_blocks):
    zbuf[...] = jnp.zeros(zbuf.shape, zbuf.dtype)

    def pad_copy(e):
        n = npad_ref[e]
        return pltpu.make_async_copy(zbuf.at[pl.ds(0, n)], xb_hbm.at[pl.ds(pad0_ref[e], n)], psem.at[0])

    def tail_copy(blk):
        return pltpu.make_async_copy(zbuf, xb_hbm.at[pl.ds(pl.multiple_of(blk * tb, tb), tb)], psem.at[1])

    def for_each_pad(fn):
        def body(e, c):
            @pl.when(npad_ref[e] > 0)
            def _():
                fn(e)
            return c
        lax.fori_loop(0, N_EXPERTS, body, 0)

    def for_each_tail(fn):
        def body(b, c):
            fn(b)
            return c
        lax.fori_loop(nb_ref[0], n_blocks, body, 0)

    for_each_pad(lambda e: pad_copy(e).start())
    for_each_tail(lambda b: tail_copy(b).start())
    for_each_pad(lambda e: pad_copy(e).wait())
    for_each_tail(lambda b: tail_copy(b).wait())


def _dispatch(dest, pad0, npad, nb, xn, *, tb, n_blocks, tm):
    T = xn.shape[0]
    tile = xn.shape[1:]
    kern = functools.partial(_dispatch_kernel, n_tok=T, tb=tb, n_blocks=n_blocks)
    grid_spec = pltpu.PrefetchScalarGridSpec(
        num_scalar_prefetch=4,
        grid=(T // tm,),
        in_specs=[pl.BlockSpec((tm,) + tile, lambda i, *_: (i, 0, 0))],
        out_specs=pl.BlockSpec(memory_space=pl.ANY),
        scratch_shapes=[
            pltpu.VMEM((tb,) + tile, xn.dtype),
            pltpu.SemaphoreType.DMA(()),
            pltpu.SemaphoreType.DMA((2,)),
        ],
    )
    return pl.pallas_call(
        kern,
        grid_spec=grid_spec,
        out_shape=jax.ShapeDtypeStruct((n_blocks * tb,) + tile, xn.dtype),
        compiler_params=_cparams(("arbitrary",), 32),
        name="dispatch",
    )(dest, pad0, npad, nb, xn)


def _expert_kernel(be_ref, first_ref, slot_ref, nxt_ref, run1_ref, nb_ref, x_ref, wg_hbm, wu_hbm, wd_hbm, y_ref,
                   wg_f, wu_f, wd_f, wsem, wg_b, wu_b, wd_b):
    i = pl.program_id(0)
    nb = nb_ref[0]

    def weight_copies(e, slot):
        return (pltpu.make_async_copy(wg_hbm.at[e], wg_f.at[slot], wsem.at[3 * slot]),
                pltpu.make_async_copy(wu_hbm.at[e], wu_f.at[slot], wsem.at[3 * slot + 1]),
                pltpu.make_async_copy(wd_hbm.at[e], wd_f.at[slot], wsem.at[3 * slot + 2]))

    @pl.when(i == 0)
    def _():
        for cp in weight_copies(be_ref[0], 0):
            cp.start()

    @pl.when((i == 0) & (run1_ref[0] >= 0))
    def _():
        for cp in weight_copies(jnp.maximum(run1_ref[0], 0), 1):
            cp.start()

    first = (i < nb) & (first_ref[i] == 1)
    slot = slot_ref[i]

    @pl.when(first)
    def _():
        for cp in weight_copies(0, slot):
            cp.wait()
        wg_b[...] = wg_f[slot].astype(BF16)
        wu_b[...] = wu_f[slot].astype(BF16)
        wd_b[...] = wd_f[slot].astype(BF16)

    @pl.when(first & (nxt_ref[i] >= 0))
    def _():
        for cp in weight_copies(jnp.maximum(nxt_ref[i], 0), slot):
            cp.start()

    @pl.when(i < nb)
    def _():
        n_lo, n_hi = _unpack_bf16_pairs(_load_row_tiles(x_ref))
        half = n_lo.shape[1]
        a = _dot(n_lo, wg_b[0:half, :]) + _dot(n_hi, wg_b[half:, :])
        u = _dot(n_lo, wu_b[0:half, :]) + _dot(n_hi, wu_b[half:, :])
        hdn = (a / (1.0 + jnp.exp(-a))) * u
        y_ref[...] = _pack_bf16_pairs(_dot(hdn.astype(BF16), wd_b[...]).astype(BF16))

    @pl.when(i >= nb)
    def _():
        y_ref[...] = jnp.zeros(y_ref.shape, y_ref.dtype)


def _experts(xb, blk_e, blk_first, blk_slot, blk_next, run1, nb, w_gate, w_up, w_down, *, tb, n_blocks):
    D, De = w_gate.shape[1:]
    assert xb.shape[1:] == (SUBLANES, LANES) and D == 2 * SUBLANES * LANES

    def x_map(i, be, first, slot, nxt, r1, nbr):
        return (jnp.minimum(i, nbr[0] - 1), 0, 0)

    grid_spec = pltpu.PrefetchScalarGridSpec(
        num_scalar_prefetch=6,
        grid=(n_blocks,),
        in_specs=[
            pl.BlockSpec((tb, SUBLANES, LANES), x_map),
            pl.BlockSpec(memory_space=pl.ANY),
            pl.BlockSpec(memory_space=pl.ANY),
            pl.BlockSpec(memory_space=pl.ANY),
        ],
        out_specs=pl.BlockSpec((tb, D // 2), lambda i, *_: (i, 0)),
        scratch_shapes=[
            pltpu.VMEM((2, D, De), F32),
            pltpu.VMEM((2, D, De), F32),
            pltpu.VMEM((2, De, D), F32),
            pltpu.SemaphoreType.DMA((6,)),
            pltpu.VMEM((D, De), BF16),
            pltpu.VMEM((D, De), BF16),
            pltpu.VMEM((De, D), BF16),
        ],
    )
    return pl.pallas_call(
        _expert_kernel,
        grid_spec=grid_spec,
        out_shape=jax.ShapeDtypeStruct((n_blocks * tb, D // 2), jnp.uint32),
        compiler_params=_cparams(("arbitrary",), 52),
        name="experts",
    )(blk_e, blk_first, blk_slot, blk_next, run1, nb, xb, w_gate, w_up, w_down)


def _combine_kernel(dest_ref, y_hbm, h_ref, gate_ref, o_ref, ybuf, sem, *, tm, n_tok):
    i = pl.program_id(0)
    n = pl.num_programs(0)

    def row_copy(d, k, r, slot):
        return pltpu.make_async_copy(y_hbm.at[pl.ds(d, 1), :], ybuf.at[slot, k, pl.ds(r, 1), :], sem.at[slot])

    def start_gather(blk, slot):
        def body(r, carry):
            for k in range(TOP_K):
                row_copy(dest_ref[k * n_tok + blk * tm + r], k, r, slot).start()
            return carry
        lax.fori_loop(0, tm, body, 0, unroll=8)

    def wait_gather(slot):
        for k in range(TOP_K):
            pltpu.make_async_copy(y_hbm.at[pl.ds(0, tm), :], ybuf.at[slot, k], sem.at[slot]).wait()

    @pl.when(i == 0)
    def _():
        start_gather(0, 0)

    @pl.when(i + 1 < n)
    def _():
        start_gather(i + 1, (i + 1) % 2)

    slot = i % 2
    wait_gather(slot)
    gt = gate_ref[...]
    half = h_ref.shape[1] // 2
    y0_lo, y0_hi = _unpack_bf16_pairs_f32(ybuf[slot, 0])
    y1_lo, y1_hi = _unpack_bf16_pairs_f32(ybuf[slot, 1])
    o_ref[:, 0:half] = h_ref[:, 0:half] + gt[:, 0:1] * y0_lo + gt[:, 1:2] * y1_lo
    o_ref[:, half:] = h_ref[:, half:] + gt[:, 0:1] * y0_hi + gt[:, 1:2] * y1_hi


def _combine(dest, yb, h2, gate, *, tm):
    T, D = h2.shape
    kern = functools.partial(_combine_kernel, tm=tm, n_tok=T)
    grid_spec = pltpu.PrefetchScalarGridSpec(
        num_scalar_prefetch=1,
        grid=(T // tm,),
        in_specs=[
            pl.BlockSpec(memory_space=pl.ANY),
            pl.BlockSpec((tm, D), lambda i, d: (i, 0)),
            pl.BlockSpec((tm, TOP_K), lambda i, d: (i, 0)),
        ],
        out_specs=pl.BlockSpec((tm, D), lambda i, d: (i, 0)),
        scratch_shapes=[
            pltpu.VMEM((2, TOP_K, tm, D // 2), jnp.uint32),
            pltpu.SemaphoreType.DMA((2,)),
        ],
    )
    return pl.pallas_call(
        kern,
        grid_spec=grid_spec,
        out_shape=jax.ShapeDtypeStruct((T, D), F32),
        compiler_params=_cparams(("arbitrary",), 40),
        name="combine",
    )(dest, yb, h2, gate)


EXPERT_ROWS = 256


def kernel(x, mem, positions, g_attn, w_in, q_norm_g, k_norm_g, lambda_q1, lambda_k1, lambda_q2, lambda_k2, diff_subln_g, gla_w_a2, gla_b_a, gla_out_g, w_out, g_cross, g_mem, w_cq, w_ckv, cq_norm_g, ck_norm_g, w_co, g_ffn, w_router_grp, b_router_grp, w_router_exp, b_router_exp, w_gate, w_up, w_down):
    B, S, D = x.shape
    T = B * S
    n_mem = mem.shape[1]
    l = 0
    x2 = x.reshape(T, D)

    half = DIFF_QKDIM // 2
    freq = ROPE_THETA ** (-jnp.arange(half, dtype=F32) / half)
    freq = jnp.tile(freq, LANES // half)[None, :]
    q_scale = math.log2(math.e) * DIFF_QKDIM ** -0.5
    qkg = jnp.stack([jnp.tile(q_norm_g[l], 2) * q_scale, jnp.tile(k_norm_g[l], 2)])
    score_bound = 1.01 * DIFF_QKDIM * q_scale * jnp.max(jnp.abs(q_norm_g[l])) * jnp.max(jnp.abs(k_norm_g[l]))
    lvec = jnp.stack([lambda_q1[l], lambda_k1[l], lambda_q2[l], lambda_k2[l]])

    qk, mid, log_a, sgr = _inproj(x2, g_attn[l][None], positions.reshape(T, 1), freq, qkg, w_in[l].T,
                                  gla_w_a2[l], gla_b_a[l][None], tm=1024)
    diffattn = functools.partial(_diffattn, lvec, qk, mid, diff_subln_g[l][None], B=B, S=S, tq=512)
    mix_d = lax.cond(score_bound <= SCORE_BOUND,
                     functools.partial(diffattn, bounded=True), functools.partial(diffattn, bounded=False))
    mix_g = _gla(mid, log_a, sgr, gla_out_g[l][None], B=B, S=S, blk=512)
    h1 = _outproj(mix_d, mix_g, w_out[l], x2, tm=1024, tn=512)

    hdim = D // CROSS_HEADS
    qc = _normproj(h1, g_cross[l][None], w_cq[l], cq_norm_g[l][None] * (hdim ** -0.5),
                   tm=1024, tn=hdim, n_norm=CROSS_HEADS, name="cq")
    kv = _normproj(mem.reshape(B * n_mem, D), g_mem[l][None], w_ckv[l], ck_norm_g[l][None],
                   tm=B * n_mem, tn=hdim, n_norm=CROSS_HEADS, name="ckv")
    h2 = _cross(qc, kv, w_co[l], h1, S=S, n_mem=n_mem, tm=1024, tn=512)

    w_rt = jnp.concatenate([w_router_grp[l].T, jnp.zeros((SUBLANES - N_GROUPS, D), F32), w_router_exp[l].T])
    b_r = jnp.concatenate([b_router_grp[l], jnp.zeros((SUBLANES - N_GROUPS,), F32), b_router_exp[l]])[:, None]
    eid, gate, xn = _router(h2, g_ffn[l][None], w_rt, b_r, tm=512)
    rank, cnt = _rank(eid, tm=512)

    tb = EXPERT_ROWS
    n_blocks = (T * TOP_K + N_EXPERTS * (tb - 1) + tb - 1) // tb
    counts = cnt[:, 0]
    pcounts = ((counts + tb - 1) // tb) * tb
    pends = jnp.cumsum(pcounts)
    pstarts = pends - pcounts
    nb = (pends[-1:] // tb).astype(I32)
    eids = jnp.arange(N_EXPERTS, dtype=I32)
    blk_start = jnp.arange(n_blocks, dtype=I32) * tb
    blk_e = jnp.minimum(jnp.sum(pends[None, :] <= blk_start[:, None], axis=1), N_EXPERTS - 1).astype(I32)
    blk_first = jnp.concatenate([jnp.ones((1,), I32), (blk_e[1:] != blk_e[:-1]).astype(I32)])
    used = jnp.where(counts > 0, eids, N_EXPERTS)
    next_used = jnp.concatenate([lax.cummin(used[::-1])[::-1][1:], jnp.full((1,), N_EXPERTS, I32)])
    next_used = jnp.where(next_used < N_EXPERTS, next_used, -1)

    def table_at(idx, table):
        return jnp.where(idx >= 0, jnp.sum(jnp.where(idx[:, None] == eids, table, 0), axis=1), -1).astype(I32)

    after_next = table_at(next_used, next_used)
    run_of = (jnp.cumsum((counts > 0).astype(I32)) - 1).astype(I32)
    blk_next = table_at(blk_e, after_next)
    blk_slot = table_at(blk_e, run_of % 2)
    run1 = table_at(blk_e[:1], next_used)
    pick = eid[:TOP_K]
    pstart_of = jnp.sum(jnp.where(pick[..., None] == eids, pstarts, 0), axis=-1)
    dest = (pstart_of + rank[:TOP_K]).astype(I32).reshape(-1)
    pad0 = (pstarts + counts).astype(I32)
    npad = (pcounts - counts).astype(I32)

    xb = _dispatch(dest, pad0, npad, nb, xn, tb=tb, n_blocks=n_blocks, tm=512)
    yb = _experts(xb, blk_e, blk_first, blk_slot, blk_next, run1, nb, w_gate[l], w_up[l], w_down[l],
                  tb=tb, n_blocks=n_blocks)
    out = _combine(dest, yb, h2, gate[:TOP_K].T, tm=512)
    return out.reshape(B, S, D)
```

```python
import functools
import math

import jax
import jax.numpy as jnp
from jax import lax
from jax.experimental import pallas as pl
from jax.experimental.pallas import tpu as pltpu

F32 = jnp.float32
BF16 = jnp.bfloat16
I32 = jnp.int32

LANES = 128
SUBLANES = 8

CHUNK = 64
ROPE_THETA = 10000.0
NORM_EPS = 1e-6
NEG_INF = -1e30
DIFF_HEADS = 8
DIFF_VDIM = 128
DIFF_QKDIM = 64
GLA_HEADS = 4
GLA_VDIM = 256
GLA_KDIM = 128
GLA_GATE_RANK = 16
GLA_TAU = 16.0
CROSS_HEADS = 4
N_GROUPS = 4
EXPERTS_PER_GROUP = 8
N_EXPERTS = N_GROUPS * EXPERTS_PER_GROUP
TOP_K = 2
LAM_INIT = 0.8 - 0.6 * math.exp(-0.3 * 0)

NT_DIMS = (((1,), (1,)), ((), ()))


def _cparams(semantics, vmem_mib):
    return pltpu.CompilerParams(dimension_semantics=semantics,
                                vmem_limit_bytes=vmem_mib * 1024 * 1024)


def _dot(a, b):
    return jnp.dot(a, b, preferred_element_type=F32)


def _dot_nt(a, b):
    return lax.dot_general(a, b, NT_DIMS, preferred_element_type=F32)


def _rms(x, g):
    ms = jnp.mean(x * x, axis=-1, keepdims=True)
    return x * lax.rsqrt(ms + NORM_EPS) * g


def _split_bf16(x):
    hi = x.astype(BF16)
    lo = (x - hi.astype(F32)).astype(BF16)
    return hi, lo


TN = 512
J_QK = 4
J_MID = 6
J_LR = J_QK + J_MID
J_GR = J_LR + 1
N_J = J_GR + 2


def _inproj_kernel(x_ref, g_ref, pos_ref, freq_ref, qkg_ref, w_ref, wgr_ref, wlr_ref, wa2_ref, ba_ref,
                   qk_ref, mid_ref, loga_ref, sgr_ref, n_scr, cos_scr, sin_scr, y_scr):
    j = pl.program_id(1)

    @pl.when(j == 0)
    def _():
        n_scr[...] = _rms(x_ref[...], g_ref[...]).astype(BF16)
        ang = pos_ref[...].astype(F32) * freq_ref[...]
        cos_scr[...] = jnp.cos(ang)
        sin_scr[...] = jnp.sin(ang)

    def qk_epilogue(jq):
        y_prev = y_scr.at[jq % 2]
        lane = lax.broadcasted_iota(I32, (1, LANES), 1)
        low_seg = lane < DIFF_QKDIM
        first_half = (lane % DIFF_QKDIM) < (DIFF_QKDIM // 2)
        gain = qkg_ref[jq // (J_QK // 2):jq // (J_QK // 2) + 1, :]
        cos = cos_scr[...]
        sin = sin_scr[...]
        for c in range(TN // LANES):
            yb = y_prev[:, c * LANES:(c + 1) * LANES]
            y2 = yb * yb
            s_lo = jnp.sum(jnp.where(low_seg, y2, 0.0), axis=-1, keepdims=True)
            s_hi = jnp.sum(jnp.where(low_seg, 0.0, y2), axis=-1, keepdims=True)
            ms = jnp.where(low_seg, s_lo, s_hi) * (1.0 / DIFF_QKDIM)
            yn = yb * lax.rsqrt(ms + NORM_EPS) * gain
            rot = jnp.where(first_half,
                            -pltpu.roll(yn, LANES - DIFF_QKDIM // 2, 1),
                            pltpu.roll(yn, DIFF_QKDIM // 2, 1))
            qk_ref[:, c * LANES:(c + 1) * LANES] = (yn * cos + rot * sin).astype(BF16)

    for jq in range(J_QK + 1):
        @pl.when(j == jq)
        def _():
            y = _dot_nt(n_scr[...], w_ref[...].astype(BF16))
            if jq > 0:
                qk_epilogue(jq - 1)
            if jq < J_QK:
                y_scr[jq % 2] = y
            else:
                mid_ref[...] = y.astype(BF16)

    @pl.when((j > J_QK) & (j < J_LR))
    def _():
        mid_ref[...] = _dot_nt(n_scr[...], w_ref[...].astype(BF16)).astype(BF16)

    @pl.when(j == J_LR)
    def _():
        lr = _dot_nt(n_scr[...], wlr_ref[...].astype(BF16))
        z = _dot(lr.astype(BF16), wa2_ref[...].astype(BF16)) + ba_ref[...]
        log_sig = jnp.minimum(z, 0.0) - jnp.log(1.0 + jnp.exp(-jnp.abs(z)))
        loga_ref[...] = log_sig * (1.0 / GLA_TAU)

    @pl.when(j >= J_GR)
    def _():
        y = _dot_nt(n_scr[...], wgr_ref[...].astype(BF16))
        sgr_ref[...] = (y / (1.0 + jnp.exp(-y))).astype(BF16)


def _inproj(x2, g_attn, pos2, freq, qkg, w_t, w_a2, b_a, *, tm):
    T, D = x2.shape
    n_mid = J_MID * TN
    n_gk = GLA_HEADS * GLA_KDIM
    lr0 = J_LR * TN
    gr0 = lr0 + GLA_GATE_RANK
    n_gr = w_t.shape[0] - gr0
    assert n_gr == 2 * TN and lr0 % GLA_GATE_RANK == 0
    return pl.pallas_call(
        _inproj_kernel,
        grid=(T // tm, N_J),
        in_specs=[
            pl.BlockSpec((tm, D), lambda i, j: (i, 0)),
            pl.BlockSpec((1, D), lambda i, j: (0, 0)),
            pl.BlockSpec((tm, 1), lambda i, j: (i, 0)),
            pl.BlockSpec((1, LANES), lambda i, j: (0, 0)),
            pl.BlockSpec((2, LANES), lambda i, j: (0, 0)),
            pl.BlockSpec((TN, D), lambda i, j: (jnp.minimum(j, J_LR - 1), 0)),
            pl.BlockSpec((pl.Element(TN), pl.Element(D)),
                         lambda i, j: (pl.multiple_of(gr0 + TN * jnp.clip(j - J_GR, 0, 1), SUBLANES), 0)),
            pl.BlockSpec((GLA_GATE_RANK, D), lambda i, j: (lr0 // GLA_GATE_RANK, 0)),
            pl.BlockSpec((GLA_GATE_RANK, n_gk), lambda i, j: (0, 0)),
            pl.BlockSpec((1, n_gk), lambda i, j: (0, 0)),
        ],
        out_specs=[
            pl.BlockSpec((tm, TN), lambda i, j: (i, jnp.clip(j - 1, 0, J_QK - 1))),
            pl.BlockSpec((tm, TN), lambda i, j: (i, jnp.clip(j - J_QK, 0, J_MID - 1))),
            pl.BlockSpec((tm, n_gk), lambda i, j: (i, 0)),
            pl.BlockSpec((tm, TN), lambda i, j: (i, jnp.clip(j - J_GR, 0, 1))),
        ],
        out_shape=[
            jax.ShapeDtypeStruct((T, J_QK * TN), BF16),
            jax.ShapeDtypeStruct((T, n_mid), BF16),
            jax.ShapeDtypeStruct((T, n_gk), F32),
            jax.ShapeDtypeStruct((T, n_gr), BF16),
        ],
        scratch_shapes=[
            pltpu.VMEM((tm, D), BF16),
            pltpu.VMEM((tm, LANES), F32),
            pltpu.VMEM((tm, LANES), F32),
            pltpu.VMEM((2, tm, TN), F32),
        ],
        compiler_params=_cparams(("parallel", "arbitrary"), 56),
        name="inproj",
    )(x2, g_attn, pos2, freq, qkg, w_t, w_t, w_t, w_a2, b_a)


SCORE_BOUND = 80.0


def _diffattn_kernel(lv_ref, q_ref, k_ref, v_ref, sg_ref, o_ref, vext, diag_mask, acc1, acc2, m1, m2,
                     *, tq, bounded):
    i = pl.program_id(2)

    @pl.when(i == 0)
    def _():
        S = v_ref.shape[0]
        lane_s = lax.broadcasted_iota(I32, (S, LANES), 1)
        vext[:, 0:DIFF_VDIM] = v_ref[...]
        vext[:, DIFF_VDIM:] = jnp.where(lane_s == 0, 1.0, 0.0).astype(BF16)
        row_chunk = lax.broadcasted_iota(I32, (tq, tq), 0) // CHUNK
        col_chunk = lax.broadcasted_iota(I32, (tq, tq), 1) // CHUNK
        diag_mask[...] = jnp.where(col_chunk <= row_chunk, 1.0, 0.0).astype(BF16)

    q = q_ref[...]
    lane = lax.broadcasted_iota(I32, (1, LANES), 1)
    zero = jnp.zeros_like(q)
    q1 = jnp.where(lane < DIFF_QKDIM, q, zero)
    q2 = jnp.where(lane < DIFF_QKDIM, zero, q)

    half = tq // 2
    comps = ((q1, acc1, m1), (q2, acc2, m2))

    def step(j):
        start = pl.multiple_of(j * tq, tq)
        k = k_ref[pl.ds(start, tq), :]
        v = vext[pl.ds(start, tq), :]
        for qc, acc, m in comps:
            s = _dot_nt(qc, k)
            if bounded:
                acc[...] += _dot(jnp.exp2(s).astype(BF16), v)
            else:
                m_old = m[...]
                m_new = jnp.maximum(m_old, jnp.max(s, axis=-1, keepdims=True))
                p = jnp.exp2(s - m_new)
                acc[...] = jnp.exp2(m_old - m_new) * acc[...] + _dot(p.astype(BF16), v)
                m[...] = m_new

    def diag_step():
        start = pl.multiple_of(i * tq, tq)
        for qc, acc, m in comps:
            for rows, n_keys in ((slice(0, half), half), (slice(half, tq), tq)):
                k = k_ref[pl.ds(start, n_keys), :]
                v = vext[pl.ds(start, n_keys), :]
                mask = diag_mask[rows, 0:n_keys]
                s = _dot_nt(qc[rows], k)
                if bounded:
                    acc[rows, :] = _dot(jnp.exp2(s).astype(BF16) * mask, v)
                else:
                    s = jnp.where(mask > 0, s, NEG_INF)
                    m_new = jnp.max(s, axis=-1, keepdims=True)
                    acc[rows, :] = _dot(jnp.exp2(s - m_new).astype(BF16), v)
                    m[rows, :] = m_new

    @pl.when(i % 2 == 1)
    def _():
        diag_step()
        step(i - 1)

    @pl.when(i % 2 == 0)
    def _():
        diag_step()

    n_quads = i // 4

    def body(jj, carry):
        for u in range(4):
            step(4 * jj + u)
        return carry

    lax.fori_loop(0, n_quads, body, 0)

    @pl.when((i // 2) % 2 == 1)
    def _():
        step(4 * n_quads)
        step(4 * n_quads + 1)

    lv = lv_ref[...]
    lam = (jnp.exp(jnp.sum(lv[0:1] * lv[1:2], axis=-1, keepdims=True))
           - jnp.exp(jnp.sum(lv[2:3] * lv[3:4], axis=-1, keepdims=True)) + LAM_INIT)
    a1 = acc1[...]
    a2 = acc2[...]
    o = (a1[:, :DIFF_VDIM] / a1[:, DIFF_VDIM:DIFF_VDIM + 1]
         - lam * (a2[:, :DIFF_VDIM] / a2[:, DIFF_VDIM:DIFF_VDIM + 1]))
    o_ref[...] = (_rms(o, sg_ref[...]) * (1.0 - LAM_INIT)).astype(BF16)


def _diffattn(lvec, qk, mid, subln_g, *, B, S, tq, bounded):
    T = B * S
    nq = S // tq
    kern = functools.partial(_diffattn_kernel, tq=tq, bounded=bounded)
    return pl.pallas_call(
        kern,
        grid=(B, DIFF_HEADS, nq),
        in_specs=[
            pl.BlockSpec((4, DIFF_QKDIM), lambda b, h, i: (0, 0)),
            pl.BlockSpec((tq, LANES), lambda b, h, i: (b * nq + i, h)),
            pl.BlockSpec((S, LANES), lambda b, h, i: (b, DIFF_HEADS + h)),
            pl.BlockSpec((S, LANES), lambda b, h, i: (b, h)),
            pl.BlockSpec((1, DIFF_VDIM), lambda b, h, i: (0, 0)),
        ],
        out_specs=pl.BlockSpec((tq, DIFF_VDIM), lambda b, h, i: (b * nq + i, h)),
        out_shape=jax.ShapeDtypeStruct((T, DIFF_HEADS * DIFF_VDIM), BF16),
        scratch_shapes=[
            pltpu.VMEM((S, 2 * DIFF_VDIM), BF16),
            pltpu.VMEM((tq, tq), BF16),
            pltpu.VMEM((tq, 2 * DIFF_VDIM), F32),
            pltpu.VMEM((tq, 2 * DIFF_VDIM), F32),
            pltpu.VMEM((tq, 1), F32),
            pltpu.VMEM((tq, 1), F32),
        ],
        compiler_params=_cparams(("parallel", "parallel", "arbitrary"), 32),
        name="diffattn_bounded" if bounded else "diffattn_online",
    )(lvec, qk, qk, mid, subln_g)


def _gla_kernel(q_ref, k_ref, v_ref, la_ref, sgr_ref, g_ref, tri_ref, ones_ref, o_ref, state, *, blk):
    @pl.when(pl.program_id(2) == 0)
    def _():
        state[...] = jnp.zeros(state.shape, F32)

    la_t = la_ref[...].T
    k_t = k_ref[...].astype(F32).T
    hi, lo = _split_bf16(la_t)
    tri = tri_ref[...]
    ones = ones_ref[...]
    cum_t = _dot(hi, tri) + _dot(lo, tri)
    tot_t = _dot(hi, ones) + _dot(lo, ones)
    kd_t = k_t * jnp.exp(tot_t - cum_t)

    n_chunks = blk // CHUNK
    lane = lax.broadcasted_iota(I32, (1, LANES), 1)
    d_states = []
    for ck in range(n_chunks):
        pair = slice((ck // 2) * LANES, (ck // 2 + 1) * LANES)
        in_chunk = (lane // CHUNK) == (ck % 2)
        kd = jnp.where(in_chunk, kd_t[:, pair], 0.0).astype(BF16)
        d_states.append(_dot(kd, v_ref[pair, :]))

    st = state[...]
    states = []
    for ck in range(n_chunks):
        decay = jnp.exp(tot_t[:, ck * CHUNK:ck * CHUNK + 1])
        st = decay * st + d_states[ck]
        states.append(st.astype(BF16))
    state[...] = st

    o = jnp.concatenate([_dot(q_ref[ck * CHUNK:(ck + 1) * CHUNK, :], states[ck]) for ck in range(n_chunks)],
                        axis=0) * (GLA_KDIM ** -0.5)
    o_ref[...] = (_rms(o, g_ref[...]) * sgr_ref[...].astype(F32)).astype(BF16)


def _gla(mid, log_a, sgr, out_g, *, B, S, blk):
    T = B * S
    ns = S // blk
    kern = functools.partial(_gla_kernel, blk=blk)
    q_col0 = (DIFF_HEADS * DIFF_VDIM) // GLA_KDIM
    k_col0 = q_col0 + GLA_HEADS
    v_col0 = (DIFF_HEADS * DIFF_VDIM + 2 * GLA_HEADS * GLA_KDIM) // GLA_VDIM
    r = jnp.arange(blk, dtype=I32)[:, None]
    c = jnp.arange(blk, dtype=I32)[None, :]
    same = (r // CHUNK) == (c // CHUNK)
    tri = (same & (r <= c)).astype(BF16)
    ones = same.astype(BF16)
    return pl.pallas_call(
        kern,
        grid=(B, GLA_HEADS, ns),
        in_specs=[
            pl.BlockSpec((blk, GLA_KDIM), lambda b, h, s: (b * ns + s, q_col0 + h)),
            pl.BlockSpec((blk, GLA_KDIM), lambda b, h, s: (b * ns + s, k_col0 + h)),
            pl.BlockSpec((blk, GLA_VDIM), lambda b, h, s: (b * ns + s, v_col0 + h)),
            pl.BlockSpec((blk, GLA_KDIM), lambda b, h, s: (b * ns + s, h)),
            pl.BlockSpec((blk, GLA_VDIM), lambda b, h, s: (b * ns + s, h)),
            pl.BlockSpec((1, GLA_VDIM), lambda b, h, s: (0, 0)),
            pl.BlockSpec((blk, blk), lambda b, h, s: (0, 0)),
            pl.BlockSpec((blk, blk), lambda b, h, s: (0, 0)),
        ],
        out_specs=pl.BlockSpec((blk, GLA_VDIM), lambda b, h, s: (b * ns + s, h)),
        out_shape=jax.ShapeDtypeStruct((T, GLA_HEADS * GLA_VDIM), BF16),
        scratch_shapes=[pltpu.VMEM((GLA_KDIM, GLA_VDIM), F32)],
        compiler_params=_cparams(("parallel", "parallel", "arbitrary"), 32),
        name="gla",
    )(mid, mid, mid, log_a, sgr, out_g, tri, ones)


def _resident_w_map(n_j):
    return lambda i, j: (0, jnp.where(i == 0, j, n_j - 1))


def _outproj_kernel(a_ref, b_ref, wa_ref, wb_ref, x_ref, o_ref, w_scr):
    j = pl.program_id(1)
    ka = a_ref.shape[1]

    @pl.when(pl.program_id(0) == 0)
    def _():
        w_scr[j, 0:ka, :] = wa_ref[...].astype(BF16)
        w_scr[j, ka:, :] = wb_ref[...].astype(BF16)

    acc = _dot(a_ref[...], w_scr[j, 0:ka, :]) + _dot(b_ref[...], w_scr[j, ka:, :])
    o_ref[...] = x_ref[...] + acc


def _outproj(a, b, w_out, x2, *, tm, tn):
    T, ka = a.shape
    kb = b.shape[1]
    assert ka == kb
    D = w_out.shape[1]
    n_j = D // tn
    return pl.pallas_call(
        _outproj_kernel,
        grid=(T // tm, n_j),
        in_specs=[
            pl.BlockSpec((tm, ka), lambda i, j: (i, 0)),
            pl.BlockSpec((tm, kb), lambda i, j: (i, 0)),
            pl.BlockSpec((ka, tn), _resident_w_map(n_j)),
            pl.BlockSpec((kb, tn), lambda i, j: (1, jnp.where(i == 0, j, n_j - 1))),
            pl.BlockSpec((tm, tn), lambda i, j: (i, j)),
        ],
        out_specs=pl.BlockSpec((tm, tn), lambda i, j: (i, j)),
        out_shape=jax.ShapeDtypeStruct((T, D), F32),
        scratch_shapes=[pltpu.VMEM((n_j, ka + kb, tn), BF16)],
        compiler_params=_cparams(("arbitrary", "arbitrary"), 48),
        name="outproj",
    )(a, b, w_out, w_out, x2)


def _normproj_kernel(x_ref, g_ref, w_ref, hg_ref, o_ref, n_scr, *w_scr, n_norm):
    j = pl.program_id(1)

    @pl.when(j == 0)
    def _():
        n_scr[...] = _rms(x_ref[...], g_ref[...]).astype(BF16)

    if w_scr:
        @pl.when(pl.program_id(0) == 0)
        def _():
            w_scr[0][j] = w_ref[...].astype(BF16)
        y = _dot(n_scr[...], w_scr[0][j])
    else:
        y = _dot(n_scr[...], w_ref[...].astype(BF16))

    @pl.when(j < n_norm)
    def _():
        o_ref[...] = _rms(y, hg_ref[...]).astype(BF16)

    @pl.when(j >= n_norm)
    def _():
        o_ref[...] = y.astype(BF16)


def _normproj(x2, g, w, head_g, *, tm, tn, n_norm, name):
    T, D = x2.shape
    N = w.shape[1]
    kern = functools.partial(_normproj_kernel, n_norm=n_norm)
    n_j = N // tn
    resident = T // tm > 1
    return pl.pallas_call(
        kern,
        grid=(T // tm, n_j),
        in_specs=[
            pl.BlockSpec((tm, D), lambda i, j: (i, 0)),
            pl.BlockSpec((1, D), lambda i, j: (0, 0)),
            pl.BlockSpec((D, tn), _resident_w_map(n_j) if resident else (lambda i, j: (0, j))),
            pl.BlockSpec((1, tn), lambda i, j: (0, 0)),
        ],
        out_specs=pl.BlockSpec((tm, tn), lambda i, j: (i, j)),
        out_shape=jax.ShapeDtypeStruct((T, N), BF16),
        scratch_shapes=[pltpu.VMEM((tm, D), BF16)] + ([pltpu.VMEM((n_j, D, tn), BF16)] if resident else []),
        compiler_params=_cparams(("arbitrary", "arbitrary"), 48),
        name=name,
    )(x2, g, w, head_g)


def _cross_kernel(q_ref, k_ref, v_ref, w_ref, h_ref, o_ref, att_scr, w_scr, *, hdim):
    j = pl.program_id(1)

    @pl.when(pl.program_id(0) == 0)
    def _():
        w_scr[j] = w_ref[...].astype(BF16)

    @pl.when(j == 0)
    def _():
        for hd in range(CROSS_HEADS):
            cols = slice(hd * hdim, (hd + 1) * hdim)
            s = lax.dot_general(q_ref[:, cols], k_ref[:, cols], NT_DIMS, preferred_element_type=F32)
            p = jnp.exp(s - jnp.max(s, axis=-1, keepdims=True))
            l = jnp.sum(p, axis=-1, keepdims=True)
            att_scr[:, cols] = (_dot(p.astype(BF16), v_ref[:, cols]) / l).astype(BF16)

    o_ref[...] = h_ref[...] + _dot(att_scr[...], w_scr[j])


def _cross(qc, kv, w_co, h1, *, S, n_mem, tm, tn):
    T, D = qc.shape
    per_b = S // tm
    kern = functools.partial(_cross_kernel, hdim=D // CROSS_HEADS)
    n_j = D // tn
    return pl.pallas_call(
        kern,
        grid=(T // tm, n_j),
        in_specs=[
            pl.BlockSpec((tm, D), lambda i, j: (i, 0)),
            pl.BlockSpec((n_mem, D), lambda i, j: (i // per_b, 0)),
            pl.BlockSpec((n_mem, D), lambda i, j: (i // per_b, 1)),
            pl.BlockSpec((D, tn), _resident_w_map(n_j)),
            pl.BlockSpec((tm, tn), lambda i, j: (i, j)),
        ],
        out_specs=pl.BlockSpec((tm, tn), lambda i, j: (i, j)),
        out_shape=jax.ShapeDtypeStruct((T, D), F32),
        scratch_shapes=[pltpu.VMEM((tm, D), BF16), pltpu.VMEM((n_j, D, tn), BF16)],
        compiler_params=_cparams(("arbitrary", "arbitrary"), 48),
        name="cross",
    )(qc, kv, kv, w_co, h1)


R_ROWS = SUBLANES + N_EXPERTS


def _pack_bf16_pairs(xb16):
    c = xb16.shape[1] // 2
    u = lax.bitcast_convert_type(xb16.astype(F32), jnp.uint32)
    return (u[:, :c] >> 16) | (u[:, c:] & jnp.uint32(0xFFFF0000))


def _store_row_tiles(ref, x):
    for g in range(SUBLANES):
        ref[:, g, :] = x[:, g * LANES:(g + 1) * LANES]


def _load_row_tiles(ref):
    return jnp.concatenate([ref[:, g, :] for g in range(SUBLANES)], axis=1)


def _unpack_bf16_pairs_f32(w):
    lo = lax.bitcast_convert_type(w << 16, F32)
    hi = lax.bitcast_convert_type(w & jnp.uint32(0xFFFF0000), F32)
    return lo, hi


def _unpack_bf16_pairs(w):
    lo, hi = _unpack_bf16_pairs_f32(w)
    return lo.astype(BF16), hi.astype(BF16)


def _router_kernel(h_ref, g_ref, wt_ref, b_ref, eid_ref, gate_ref, xn_ref):
    n = _rms(h_ref[...], g_ref[...])
    nh, nl = _split_bf16(n)
    _store_row_tiles(xn_ref, _pack_bf16_pairs(nh))
    wh, wl = _split_bf16(wt_ref[...])
    nt = functools.partial(lax.dot_general, dimension_numbers=NT_DIMS, preferred_element_type=F32)
    lg = nt(wh, nh) + nt(wh, nl) + nt(wl, nh) + b_ref[...]

    tm = lg.shape[1]
    row = lax.broadcasted_iota(I32, (SUBLANES, tm), 0)

    def first_argmax(v, vmax):
        return jnp.min(jnp.where(v == vmax, row, SUBLANES), axis=0, keepdims=True)

    gl = jnp.where(row < N_GROUPS, lg[0:SUBLANES], NEG_INF)
    gmax = jnp.max(gl, axis=0, keepdims=True)
    grp = first_argmax(gl, gmax)
    grp_w = 1.0 / jnp.sum(jnp.exp(gl - gmax), axis=0, keepdims=True)

    sel = jnp.zeros((SUBLANES, tm), F32)
    for gi in range(N_GROUPS):
        lo = SUBLANES + gi * EXPERTS_PER_GROUP
        sel = jnp.where(grp == gi, lg[lo:lo + EXPERTS_PER_GROUP], sel)
    e = jnp.exp(sel - jnp.max(sel, axis=0, keepdims=True))
    prob = e / jnp.sum(e, axis=0, keepdims=True)
    p1 = jnp.max(prob, axis=0, keepdims=True)
    i1 = first_argmax(prob, p1)
    rest = jnp.where(row == i1, -1.0, prob)
    p2 = jnp.max(rest, axis=0, keepdims=True)
    i2 = first_argmax(rest, p2)
    den = p1 + p2
    base = grp * EXPERTS_PER_GROUP
    eid_ref[...] = jnp.where(row == 0, base + i1, jnp.where(row == 1, base + i2, 0))
    gate_ref[...] = jnp.where(row == 0, grp_w * p1 / den, jnp.where(row == 1, grp_w * p2 / den, 0.0))


def _router(h2, g_ffn, w_rt, b_r, *, tm):
    T, D = h2.shape
    return pl.pallas_call(
        _router_kernel,
        grid=(T // tm,),
        in_specs=[
            pl.BlockSpec((tm, D), lambda i: (i, 0)),
            pl.BlockSpec((1, D), lambda i: (0, 0)),
            pl.BlockSpec((R_ROWS, D), lambda i: (0, 0)),
            pl.BlockSpec((R_ROWS, 1), lambda i: (0, 0)),
        ],
        out_specs=[
            pl.BlockSpec((SUBLANES, tm), lambda i: (0, i)),
            pl.BlockSpec((SUBLANES, tm), lambda i: (0, i)),
            pl.BlockSpec((tm, SUBLANES, LANES), lambda i: (i, 0, 0)),
        ],
        out_shape=[
            jax.ShapeDtypeStruct((SUBLANES, T), I32),
            jax.ShapeDtypeStruct((SUBLANES, T), F32),
            jax.ShapeDtypeStruct((T, SUBLANES, LANES), jnp.uint32),
        ],
        compiler_params=_cparams(("parallel",), 32),
        name="router",
    )(h2, g_ffn, w_rt, b_r)


def _rank_kernel(eid_ref, rank_ref, cnt_ref, carry):
    @pl.when(pl.program_id(0) == 0)
    def _():
        carry[...] = jnp.zeros(carry.shape, F32)

    tm = eid_ref.shape[1]
    e0 = eid_ref[0:1, :]
    e1 = eid_ref[1:2, :]
    erow = lax.broadcasted_iota(I32, (N_EXPERTS, tm), 0)
    hit = jnp.where((erow == e0) | (erow == e1), 1.0, 0.0)
    r = lax.broadcasted_iota(I32, (tm, tm), 0)
    c = lax.broadcasted_iota(I32, (tm, tm), 1)
    before = jnp.where(r < c, 1.0, 0.0).astype(BF16)
    pre = _dot(hit.astype(BF16), before) + carry[:, 0:1]
    rank0 = jnp.sum(jnp.where(erow == e0, pre, 0.0), axis=0, keepdims=True)
    rank1 = jnp.sum(jnp.where(erow == e1, pre, 0.0), axis=0, keepdims=True)
    row = lax.broadcasted_iota(I32, (SUBLANES, tm), 0)
    rank_ref[...] = jnp.where(row == 0, rank0, jnp.where(row == 1, rank1, 0.0)).astype(I32)
    total = carry[...] + jnp.sum(hit, axis=1, keepdims=True)
    carry[...] = total
    cnt_ref[...] = total.astype(I32)


def _rank(eid, *, tm):
    T = eid.shape[1]
    return pl.pallas_call(
        _rank_kernel,
        grid=(T // tm,),
        in_specs=[pl.BlockSpec((SUBLANES, tm), lambda i: (0, i))],
        out_specs=[
            pl.BlockSpec((SUBLANES, tm), lambda i: (0, i)),
            pl.BlockSpec((N_EXPERTS, LANES), lambda i: (0, 0)),
        ],
        out_shape=[
            jax.ShapeDtypeStruct((SUBLANES, T), I32),
            jax.ShapeDtypeStruct((N_EXPERTS, LANES), I32),
        ],
        scratch_shapes=[pltpu.VMEM((N_EXPERTS, LANES), F32)],
        compiler_params=_cparams(("arbitrary",), 32),
        name="rank",
    )(eid)


def _dispatch_kernel(dest_ref, pad0_ref, npad_ref, nb_ref, xn_ref, xb_hbm, zbuf, sem, psem,
                     *, n_tok, tb, n_blocks):
    i = pl.program_id(0)
    tm = xn_ref.shape[0]
    base = i * tm

    def row_body(r, carry):
        for k in range(TOP_K):
            pltpu.make_async_copy(xn_ref.at[r], xb_hbm.at[dest_ref[k * n_tok + base + r]], sem).start(priority=k)
        return carry
    lax.fori_loop(0, tm, row_body, 0, unroll=8)

    @pl.when(i == 0)
    def _():
        _dispatch_fill(pad0_ref, npad_ref, nb_ref, xb_hbm, zbuf, psem, tb=tb, n_blocks=n_blocks)

    for k in range(TOP_K):
        pltpu.make_async_copy(xn_ref, xb_hbm.at[pl.ds(0, tm)], sem).wait()


def _dispatch_fill(pad0_ref, npad_ref, nb_ref, xb_hbm, zbuf, psem, *, tb, n_blocks):
    zbuf[...] = jnp.zeros(zbuf.shape, zbuf.dtype)

    def pad_copy(e):
        n = npad_ref[e]
        return pltpu.make_async_copy(zbuf.at[pl.ds(0, n)], xb_hbm.at[pl.ds(pad0_ref[e], n)], psem.at[0])

    def tail_copy(blk):
        return pltpu.make_async_copy(zbuf, xb_hbm.at[pl.ds(pl.multiple_of(blk * tb, tb), tb)], psem.at[1])

    def for_each_pad(fn):
        def body(e, c):
            @pl.when(npad_ref[e] > 0)
            def _():
                fn(e)
            return c
        lax.fori_loop(0, N_EXPERTS, body, 0)

    def for_each_tail(fn):
        def body(b, c):
            fn(b)
            return c
        lax.fori_loop(nb_ref[0], n_blocks, body, 0)

    for_each_pad(lambda e: pad_copy(e).start())
    for_each_tail(lambda b: tail_copy(b).start())
    for_each_pad(lambda e: pad_copy(e).wait())
    for_each_tail(lambda b: tail_copy(b).wait())


def _dispatch(dest, pad0, npad, nb, xn, *, tb, n_blocks, tm):
    T = xn.shape[0]
    tile = xn.shape[1:]
    kern = functools.partial(_dispatch_kernel, n_tok=T, tb=tb, n_blocks=n_blocks)
    grid_spec = pltpu.PrefetchScalarGridSpec(
        num_scalar_prefetch=4,
        grid=(T // tm,),
        in_specs=[pl.BlockSpec((tm,) + tile, lambda i, *_: (i, 0, 0))],
        out_specs=pl.BlockSpec(memory_space=pl.ANY),
        scratch_shapes=[
            pltpu.VMEM((tb,) + tile, xn.dtype),
            pltpu.SemaphoreType.DMA(()),
            pltpu.SemaphoreType.DMA((2,)),
        ],
    )
    return pl.pallas_call(
        kern,
        grid_spec=grid_spec,
        out_shape=jax.ShapeDtypeStruct((n_blocks * tb,) + tile, xn.dtype),
        compiler_params=_cparams(("arbitrary",), 32),
        name="dispatch",
    )(dest, pad0, npad, nb, xn)


def _expert_kernel(be_ref, first_ref, slot_ref, nxt_ref, run1_ref, nb_ref, x_ref, wg_hbm, wu_hbm, wd_hbm, y_ref,
                   wg_f, wu_f, wd_f, wsem, wg_b, wu_b, wd_b):
    i = pl.program_id(0)
    nb = nb_ref[0]

    def weight_copies(e, slot):
        return (pltpu.make_async_copy(wg_hbm.at[e], wg_f.at[slot], wsem.at[3 * slot]),
                pltpu.make_async_copy(wu_hbm.at[e], wu_f.at[slot], wsem.at[3 * slot + 1]),
                pltpu.make_async_copy(wd_hbm.at[e], wd_f.at[slot], wsem.at[3 * slot + 2]))

    @pl.when(i == 0)
    def _():
        for cp in weight_copies(be_ref[0], 0):
            cp.start()

    @pl.when((i == 0) & (run1_ref[0] >= 0))
    def _():
        for cp in weight_copies(jnp.maximum(run1_ref[0], 0), 1):
            cp.start()

    first = (i < nb) & (first_ref[i] == 1)
    slot = slot_ref[i]

    @pl.when(first)
    def _():
        for cp in weight_copies(0, slot):
            cp.wait()
        wg_b[...] = wg_f[slot].astype(BF16)
        wu_b[...] = wu_f[slot].astype(BF16)
        wd_b[...] = wd_f[slot].astype(BF16)

    @pl.when(first & (nxt_ref[i] >= 0))
    def _():
        for cp in weight_copies(jnp.maximum(nxt_ref[i], 0), slot):
            cp.start()

    @pl.when(i < nb)
    def _():
        n_lo, n_hi = _unpack_bf16_pairs(_load_row_tiles(x_ref))
        half = n_lo.shape[1]
        a = _dot(n_lo, wg_b[0:half, :]) + _dot(n_hi, wg_b[half:, :])
        u = _dot(n_lo, wu_b[0:half, :]) + _dot(n_hi, wu_b[half:, :])
        hdn = (a / (1.0 + jnp.exp(-a))) * u
        y_ref[...] = _pack_bf16_pairs(_dot(hdn.astype(BF16), wd_b[...]).astype(BF16))

    @pl.when(i >= nb)
    def _():
        y_ref[...] = jnp.zeros(y_ref.shape, y_ref.dtype)


def _experts(xb, blk_e, blk_first, blk_slot, blk_next, run1, nb, w_gate, w_up, w_down, *, tb, n_blocks):
    D, De = w_gate.shape[1:]
    assert xb.shape[1:] == (SUBLANES, LANES) and D == 2 * SUBLANES * LANES

    def x_map(i, be, first, slot, nxt, r1, nbr):
        return (jnp.minimum(i, nbr[0] - 1), 0, 0)

    grid_spec = pltpu.PrefetchScalarGridSpec(
        num_scalar_prefetch=6,
        grid=(n_blocks,),
        in_specs=[
            pl.BlockSpec((tb, SUBLANES, LANES), x_map),
            pl.BlockSpec(memory_space=pl.ANY),
            pl.BlockSpec(memory_space=pl.ANY),
            pl.BlockSpec(memory_space=pl.ANY),
        ],
        out_specs=pl.BlockSpec((tb, D // 2), lambda i, *_: (i, 0)),
        scratch_shapes=[
            pltpu.VMEM((2, D, De), F32),
            pltpu.VMEM((2, D, De), F32),
            pltpu.VMEM((2, De, D), F32),
            pltpu.SemaphoreType.DMA((6,)),
            pltpu.VMEM((D, De), BF16),
            pltpu.VMEM((D, De), BF16),
            pltpu.VMEM((De, D), BF16),
        ],
    )
    return pl.pallas_call(
        _expert_kernel,
        grid_spec=grid_spec,
        out_shape=jax.ShapeDtypeStruct((n_blocks * tb, D // 2), jnp.uint32),
        compiler_params=_cparams(("arbitrary",), 52),
        name="experts",
    )(blk_e, blk_first, blk_slot, blk_next, run1, nb, xb, w_gate, w_up, w_down)


def _combine_kernel(dest_ref, y_hbm, h_ref, gate_ref, o_ref, ybuf, sem, *, tm, n_tok):
    i = pl.program_id(0)
    n = pl.num_programs(0)

    def row_copy(d, k, r, slot):
        return pltpu.make_async_copy(y_hbm.at[pl.ds(d, 1), :], ybuf.at[slot, k, pl.ds(r, 1), :], sem.at[slot])

    def start_gather(blk, slot):
        def body(r, carry):
            for k in range(TOP_K):
                row_copy(dest_ref[k * n_tok + blk * tm + r], k, r, slot).start(priority=k)
            return carry
        lax.fori_loop(0, tm, body, 0, unroll=8)

    def wait_gather(slot):
        for k in range(TOP_K):
            pltpu.make_async_copy(y_hbm.at[pl.ds(0, tm), :], ybuf.at[slot, k], sem.at[slot]).wait()

    @pl.when(i == 0)
    def _():
        start_gather(0, 0)

    @pl.when(i + 1 < n)
    def _():
        start_gather(i + 1, (i + 1) % 2)

    slot = i % 2
    wait_gather(slot)
    gt = gate_ref[...]
    half = h_ref.shape[1] // 2
    y0_lo, y0_hi = _unpack_bf16_pairs_f32(ybuf[slot, 0])
    y1_lo, y1_hi = _unpack_bf16_pairs_f32(ybuf[slot, 1])
    o_ref[:, 0:half] = h_ref[:, 0:half] + gt[:, 0:1] * y0_lo + gt[:, 1:2] * y1_lo
    o_ref[:, half:] = h_ref[:, half:] + gt[:, 0:1] * y0_hi + gt[:, 1:2] * y1_hi


def _combine(dest, yb, h2, gate, *, tm):
    T, D = h2.shape
    kern = functools.partial(_combine_kernel, tm=tm, n_tok=T)
    grid_spec = pltpu.PrefetchScalarGridSpec(
        num_scalar_prefetch=1,
        grid=(T // tm,),
        in_specs=[
            pl.BlockSpec(memory_space=pl.ANY),
            pl.BlockSpec((tm, D), lambda i, d: (i, 0)),
            pl.BlockSpec((tm, TOP_K), lambda i, d: (i, 0)),
        ],
        out_specs=pl.BlockSpec((tm, D), lambda i, d: (i, 0)),
        scratch_shapes=[
            pltpu.VMEM((2, TOP_K, tm, D // 2), jnp.uint32),
            pltpu.SemaphoreType.DMA((2,)),
        ],
    )
    return pl.pallas_call(
        kern,
        grid_spec=grid_spec,
        out_shape=jax.ShapeDtypeStruct((T, D), F32),
        compiler_params=_cparams(("arbitrary",), 40),
        name="combine",
    )(dest, yb, h2, gate)


EXPERT_ROWS = 256


def kernel(x, mem, positions, g_attn, w_in, q_norm_g, k_norm_g, lambda_q1, lambda_k1, lambda_q2, lambda_k2, diff_subln_g, gla_w_a2, gla_b_a, gla_out_g, w_out, g_cross, g_mem, w_cq, w_ckv, cq_norm_g, ck_norm_g, w_co, g_ffn, w_router_grp, b_router_grp, w_router_exp, b_router_exp, w_gate, w_up, w_down):
    B, S, D = x.shape
    T = B * S
    n_mem = mem.shape[1]
    l = 0
    x2 = x.reshape(T, D)

    half = DIFF_QKDIM // 2
    freq = ROPE_THETA ** (-jnp.arange(half, dtype=F32) / half)
    freq = jnp.tile(freq, LANES // half)[None, :]
    q_scale = math.log2(math.e) * DIFF_QKDIM ** -0.5
    qkg = jnp.stack([jnp.tile(q_norm_g[l], 2) * q_scale, jnp.tile(k_norm_g[l], 2)])
    score_bound = 1.01 * DIFF_QKDIM * q_scale * jnp.max(jnp.abs(q_norm_g[l])) * jnp.max(jnp.abs(k_norm_g[l]))
    lvec = jnp.stack([lambda_q1[l], lambda_k1[l], lambda_q2[l], lambda_k2[l]])

    qk, mid, log_a, sgr = _inproj(x2, g_attn[l][None], positions.reshape(T, 1), freq, qkg, w_in[l].T,
                                  gla_w_a2[l], gla_b_a[l][None], tm=1024)
    diffattn = functools.partial(_diffattn, lvec, qk, mid, diff_subln_g[l][None], B=B, S=S, tq=512)
    mix_d = lax.cond(score_bound <= SCORE_BOUND,
                     functools.partial(diffattn, bounded=True), functools.partial(diffattn, bounded=False))
    mix_g = _gla(mid, log_a, sgr, gla_out_g[l][None], B=B, S=S, blk=512)
    h1 = _outproj(mix_d, mix_g, w_out[l], x2, tm=1024, tn=512)

    hdim = D // CROSS_HEADS
    qc = _normproj(h1, g_cross[l][None], w_cq[l], cq_norm_g[l][None] * (hdim ** -0.5),
                   tm=1024, tn=hdim, n_norm=CROSS_HEADS, name="cq")
    kv = _normproj(mem.reshape(B * n_mem, D), g_mem[l][None], w_ckv[l], ck_norm_g[l][None],
                   tm=B * n_mem, tn=hdim, n_norm=CROSS_HEADS, name="ckv")
    h2 = _cross(qc, kv, w_co[l], h1, S=S, n_mem=n_mem, tm=1024, tn=512)

    w_rt = jnp.concatenate([w_router_grp[l].T, jnp.zeros((SUBLANES - N_GROUPS, D), F32), w_router_exp[l].T])
    b_r = jnp.concatenate([b_router_grp[l], jnp.zeros((SUBLANES - N_GROUPS,), F32), b_router_exp[l]])[:, None]
    eid, gate, xn = _router(h2, g_ffn[l][None], w_rt, b_r, tm=512)
    rank, cnt = _rank(eid, tm=512)

    tb = EXPERT_ROWS
    n_blocks = (T * TOP_K + N_EXPERTS * (tb - 1) + tb - 1) // tb
    counts = cnt[:, 0]
    pcounts = ((counts + tb - 1) // tb) * tb
    pends = jnp.cumsum(pcounts)
    pstarts = pends - pcounts
    nb = (pends[-1:] // tb).astype(I32)
    eids = jnp.arange(N_EXPERTS, dtype=I32)
    blk_start = jnp.arange(n_blocks, dtype=I32) * tb
    blk_e = jnp.minimum(jnp.sum(pends[None, :] <= blk_start[:, None], axis=1), N_EXPERTS - 1).astype(I32)
    blk_first = jnp.concatenate([jnp.ones((1,), I32), (blk_e[1:] != blk_e[:-1]).astype(I32)])
    used = jnp.where(counts > 0, eids, N_EXPERTS)
    next_used = jnp.concatenate([lax.cummin(used[::-1])[::-1][1:], jnp.full((1,), N_EXPERTS, I32)])
    next_used = jnp.where(next_used < N_EXPERTS, next_used, -1)

    def table_at(idx, table):
        return jnp.where(idx >= 0, jnp.sum(jnp.where(idx[:, None] == eids, table, 0), axis=1), -1).astype(I32)

    after_next = table_at(next_used, next_used)
    run_of = (jnp.cumsum((counts > 0).astype(I32)) - 1).astype(I32)
    blk_next = table_at(blk_e, after_next)
    blk_slot = table_at(blk_e, run_of % 2)
    run1 = table_at(blk_e[:1], next_used)
    pick = eid[:TOP_K]
    pstart_of = jnp.sum(jnp.where(pick[..., None] == eids, pstarts, 0), axis=-1)
    dest = (pstart_of + rank[:TOP_K]).astype(I32).reshape(-1)
    pad0 = (pstarts + counts).astype(I32)
    npad = (pcounts - counts).astype(I32)

    xb = _dispatch(dest, pad0, npad, nb, xn, tb=tb, n_blocks=n_blocks, tm=512)
    yb = _experts(xb, blk_e, blk_first, blk_slot, blk_next, run1, nb, w_gate[l], w_up[l], w_down[l],
                  tb=tb, n_blocks=n_blocks)
    out = _combine(dest, yb, h2, gate[:TOP_K].T, tm=512)
    return out.reshape(B, S, D)
```

```python
import functools
import math

import jax
import jax.numpy as jnp
from jax import lax
from jax.experimental import pallas as pl
from jax.experimental.pallas import tpu as pltpu

F32 = jnp.float32
BF16 = jnp.bfloat16
I32 = jnp.int32

LANES = 128
SUBLANES = 8

CHUNK = 64
ROPE_THETA = 10000.0
NORM_EPS = 1e-6
NEG_INF = -1e30
DIFF_HEADS = 8
DIFF_VDIM = 128
DIFF_QKDIM = 64
GLA_HEADS = 4
GLA_VDIM = 256
GLA_KDIM = 128
GLA_GATE_RANK = 16
GLA_TAU = 16.0
CROSS_HEADS = 4
N_GROUPS = 4
EXPERTS_PER_GROUP = 8
N_EXPERTS = N_GROUPS * EXPERTS_PER_GROUP
TOP_K = 2
LAM_INIT = 0.8 - 0.6 * math.exp(-0.3 * 0)

NT_DIMS = (((1,), (1,)), ((), ()))


def _cparams(semantics, vmem_mib):
    return pltpu.CompilerParams(dimension_semantics=semantics,
                                vmem_limit_bytes=vmem_mib * 1024 * 1024)


def _dot(a, b):
    return jnp.dot(a, b, preferred_element_type=F32)


def _dot_nt(a, b):
    return lax.dot_general(a, b, NT_DIMS, preferred_element_type=F32)


def _rms(x, g):
    ms = jnp.mean(x * x, axis=-1, keepdims=True)
    return x * lax.rsqrt(ms + NORM_EPS) * g


def _split_bf16(x):
    hi = x.astype(BF16)
    lo = (x - hi.astype(F32)).astype(BF16)
    return hi, lo


TN = 512
J_QK = 4
J_MID = 6
J_LR = J_QK + J_MID
J_GR = J_LR + 1
N_J = J_GR + 2


def _inproj_kernel(x_ref, g_ref, pos_ref, freq_ref, qkg_ref, w_ref, wgr_ref, wlr_ref, wa2_ref, ba_ref,
                   qk_ref, mid_ref, loga_ref, sgr_ref, n_scr, cos_scr, sin_scr, y_scr):
    j = pl.program_id(1)

    @pl.when(j == 0)
    def _():
        n_scr[...] = _rms(x_ref[...], g_ref[...]).astype(BF16)
        ang = pos_ref[...].astype(F32) * freq_ref[...]
        cos_scr[...] = jnp.cos(ang)
        sin_scr[...] = jnp.sin(ang)

    def qk_epilogue(jq):
        y_prev = y_scr.at[jq % 2]
        lane = lax.broadcasted_iota(I32, (1, LANES), 1)
        low_seg = lane < DIFF_QKDIM
        first_half = (lane % DIFF_QKDIM) < (DIFF_QKDIM // 2)
        gain = qkg_ref[jq // (J_QK // 2):jq // (J_QK // 2) + 1, :]
        cos = cos_scr[...]
        sin = sin_scr[...]
        for c in range(TN // LANES):
            yb = y_prev[:, c * LANES:(c + 1) * LANES]
            y2 = yb * yb
            s_lo = jnp.sum(jnp.where(low_seg, y2, 0.0), axis=-1, keepdims=True)
            s_hi = jnp.sum(jnp.where(low_seg, 0.0, y2), axis=-1, keepdims=True)
            ms = jnp.where(low_seg, s_lo, s_hi) * (1.0 / DIFF_QKDIM)
            yn = yb * lax.rsqrt(ms + NORM_EPS) * gain
            rot = jnp.where(first_half,
                            -pltpu.roll(yn, LANES - DIFF_QKDIM // 2, 1),
                            pltpu.roll(yn, DIFF_QKDIM // 2, 1))
            qk_ref[:, c * LANES:(c + 1) * LANES] = (yn * cos + rot * sin).astype(BF16)

    for jq in range(J_QK + 1):
        @pl.when(j == jq)
        def _():
            y = _dot_nt(n_scr[...], w_ref[...].astype(BF16))
            if jq > 0:
                qk_epilogue(jq - 1)
            if jq < J_QK:
                y_scr[jq % 2] = y
            else:
                mid_ref[...] = y.astype(BF16)

    @pl.when((j > J_QK) & (j < J_LR))
    def _():
        mid_ref[...] = _dot_nt(n_scr[...], w_ref[...].astype(BF16)).astype(BF16)

    @pl.when(j == J_LR)
    def _():
        lr = _dot_nt(n_scr[...], wlr_ref[...].astype(BF16))
        z = _dot(lr.astype(BF16), wa2_ref[...].astype(BF16)) + ba_ref[...]
        log_sig = jnp.minimum(z, 0.0) - jnp.log(1.0 + jnp.exp(-jnp.abs(z)))
        loga_ref[...] = log_sig * (1.0 / GLA_TAU)

    @pl.when(j >= J_GR)
    def _():
        y = _dot_nt(n_scr[...], wgr_ref[...].astype(BF16))
        sgr_ref[...] = (y / (1.0 + jnp.exp(-y))).astype(BF16)


def _inproj(x2, g_attn, pos2, freq, qkg, w_t, w_a2, b_a, *, tm):
    T, D = x2.shape
    n_mid = J_MID * TN
    n_gk = GLA_HEADS * GLA_KDIM
    lr0 = J_LR * TN
    gr0 = lr0 + GLA_GATE_RANK
    n_gr = w_t.shape[0] - gr0
    assert n_gr == 2 * TN and lr0 % GLA_GATE_RANK == 0
    return pl.pallas_call(
        _inproj_kernel,
        grid=(T // tm, N_J),
        in_specs=[
            pl.BlockSpec((tm, D), lambda i, j: (i, 0)),
            pl.BlockSpec((1, D), lambda i, j: (0, 0)),
            pl.BlockSpec((tm, 1), lambda i, j: (i, 0)),
            pl.BlockSpec((1, LANES), lambda i, j: (0, 0)),
            pl.BlockSpec((2, LANES), lambda i, j: (0, 0)),
            pl.BlockSpec((TN, D), lambda i, j: (jnp.minimum(j, J_LR - 1), 0)),
            pl.BlockSpec((pl.Element(TN), pl.Element(D)),
                         lambda i, j: (pl.multiple_of(gr0 + TN * jnp.clip(j - J_GR, 0, 1), SUBLANES), 0)),
            pl.BlockSpec((GLA_GATE_RANK, D), lambda i, j: (lr0 // GLA_GATE_RANK, 0)),
            pl.BlockSpec((GLA_GATE_RANK, n_gk), lambda i, j: (0, 0)),
            pl.BlockSpec((1, n_gk), lambda i, j: (0, 0)),
        ],
        out_specs=[
            pl.BlockSpec((tm, TN), lambda i, j: (i, jnp.clip(j - 1, 0, J_QK - 1))),
            pl.BlockSpec((tm, TN), lambda i, j: (i, jnp.clip(j - J_QK, 0, J_MID - 1))),
            pl.BlockSpec((tm, n_gk), lambda i, j: (i, 0)),
            pl.BlockSpec((tm, TN), lambda i, j: (i, jnp.clip(j - J_GR, 0, 1))),
        ],
        out_shape=[
            jax.ShapeDtypeStruct((T, J_QK * TN), BF16),
            jax.ShapeDtypeStruct((T, n_mid), BF16),
            jax.ShapeDtypeStruct((T, n_gk), F32),
            jax.ShapeDtypeStruct((T, n_gr), BF16),
        ],
        scratch_shapes=[
            pltpu.VMEM((tm, D), BF16),
            pltpu.VMEM((tm, LANES), F32),
            pltpu.VMEM((tm, LANES), F32),
            pltpu.VMEM((2, tm, TN), F32),
        ],
        compiler_params=_cparams(("parallel", "arbitrary"), 56),
        name="inproj",
    )(x2, g_attn, pos2, freq, qkg, w_t, w_t, w_t, w_a2, b_a)


SCORE_BOUND = 80.0


def _diffattn_kernel(ti_ref, tj_ref, lv_ref, q_ref, k_ref, v_ref, sg_ref, o_ref,
                     vext, diag_mask, acc1, acc2, m1, m2, *, tq, n_tiles, bounded):
    S = q_ref.shape[0]
    nq = S // tq
    half = tq // 2
    unroll = 4

    lane_s = lax.broadcasted_iota(I32, (S, LANES), 1)
    vext[:, 0:DIFF_VDIM] = v_ref[...]
    vext[:, DIFF_VDIM:] = jnp.where(lane_s == 0, 1.0, 0.0).astype(BF16)
    row_chunk = lax.broadcasted_iota(I32, (tq, tq), 0) // CHUNK
    col_chunk = lax.broadcasted_iota(I32, (tq, tq), 1) // CHUNK
    diag_mask[...] = jnp.where(col_chunk <= row_chunk, 1.0, 0.0).astype(BF16)

    lane = lax.broadcasted_iota(I32, (1, LANES), 1)

    def block(i):
        return pl.ds(pl.multiple_of(i * tq, tq), tq)

    def q_comps(i):
        q = q_ref[block(i), :]
        zero = jnp.zeros_like(q)
        return jnp.where(lane < DIFF_QKDIM, q, zero), jnp.where(lane < DIFF_QKDIM, zero, q)

    def diag_tile(i):
        q1, q2 = q_comps(i)
        for qc, acc, m in ((q1, acc1, m1), (q2, acc2, m2)):
            for lo, n_keys in ((0, half), (half, tq)):
                rows = pl.ds(pl.multiple_of(i * tq + lo, half), half)
                keys = pl.ds(pl.multiple_of(i * tq, tq), n_keys)
                mask = diag_mask[lo:lo + half, 0:n_keys]
                s = _dot_nt(qc[lo:lo + half], k_ref[keys, :])
                if bounded:
                    acc[rows, :] = _dot(jnp.exp2(s).astype(BF16) * mask, vext[keys, :])
                else:
                    s = jnp.where(mask > 0, s, NEG_INF)
                    m_new = jnp.max(s, axis=-1, keepdims=True)
                    acc[rows, :] = _dot(jnp.exp2(s - m_new).astype(BF16), vext[keys, :])
                    m[rows, :] = m_new

    def full_tile(i, j):
        q1, q2 = q_comps(i)
        rows = block(i)
        k = k_ref[block(j), :]
        v = vext[block(j), :]
        for qc, acc, m in ((q1, acc1, m1), (q2, acc2, m2)):
            s = _dot_nt(qc, k)
            if bounded:
                acc[rows, :] += _dot(jnp.exp2(s).astype(BF16), v)
            else:
                m_old = m[rows, :]
                m_new = jnp.maximum(m_old, jnp.max(s, axis=-1, keepdims=True))
                p = jnp.exp2(s - m_new)
                acc[rows, :] = jnp.exp2(m_old - m_new) * acc[rows, :] + _dot(p.astype(BF16), v)
                m[rows, :] = m_new

    def diag_body(t, carry):
        diag_tile(2 * t)
        diag_tile(2 * t + 1)
        return carry
    lax.fori_loop(0, nq // 2, diag_body, 0)

    def full_body(t, carry):
        for u in range(unroll):
            full_tile(ti_ref[unroll * t + u], tj_ref[unroll * t + u])
        return carry
    lax.fori_loop(0, n_tiles // unroll, full_body, 0)
    for t in range(n_tiles - n_tiles % unroll, n_tiles):
        full_tile(ti_ref[t], tj_ref[t])

    lv = lv_ref[...]
    lam = (jnp.exp(jnp.sum(lv[0:1] * lv[1:2], axis=-1, keepdims=True))
           - jnp.exp(jnp.sum(lv[2:3] * lv[3:4], axis=-1, keepdims=True)) + LAM_INIT)

    def out_body(i, carry):
        rows = block(i)
        a1 = acc1[rows, :]
        a2 = acc2[rows, :]
        o = (a1[:, :DIFF_VDIM] / a1[:, DIFF_VDIM:DIFF_VDIM + 1]
             - lam * (a2[:, :DIFF_VDIM] / a2[:, DIFF_VDIM:DIFF_VDIM + 1]))
        o_ref[rows, :] = (_rms(o, sg_ref[...]) * (1.0 - LAM_INIT)).astype(BF16)
        return carry
    lax.fori_loop(0, nq, out_body, 0)


def _diffattn(lvec, qk, mid, subln_g, *, B, S, tq, bounded):
    T = B * S
    nq = S // tq
    assert nq % 2 == 0
    tiles = [(i, j) for i in range(nq) for j in range(i)]
    ti = jnp.asarray([t[0] for t in tiles], I32)
    tj = jnp.asarray([t[1] for t in tiles], I32)
    kern = functools.partial(_diffattn_kernel, tq=tq, n_tiles=len(tiles), bounded=bounded)
    m_rows = SUBLANES if bounded else S
    grid_spec = pltpu.PrefetchScalarGridSpec(
        num_scalar_prefetch=2,
        grid=(B, DIFF_HEADS),
        in_specs=[
            pl.BlockSpec((4, DIFF_QKDIM), lambda b, h, *_: (0, 0)),
            pl.BlockSpec((S, LANES), lambda b, h, *_: (b, h)),
            pl.BlockSpec((S, LANES), lambda b, h, *_: (b, DIFF_HEADS + h)),
            pl.BlockSpec((S, LANES), lambda b, h, *_: (b, h)),
            pl.BlockSpec((1, DIFF_VDIM), lambda b, h, *_: (0, 0)),
        ],
        out_specs=pl.BlockSpec((S, DIFF_VDIM), lambda b, h, *_: (b, h)),
        scratch_shapes=[
            pltpu.VMEM((S, 2 * DIFF_VDIM), BF16),
            pltpu.VMEM((tq, tq), BF16),
            pltpu.VMEM((S, 2 * DIFF_VDIM), F32),
            pltpu.VMEM((S, 2 * DIFF_VDIM), F32),
            pltpu.VMEM((m_rows, 1), F32),
            pltpu.VMEM((m_rows, 1), F32),
        ],
    )
    return pl.pallas_call(
        kern,
        grid_spec=grid_spec,
        out_shape=jax.ShapeDtypeStruct((T, DIFF_HEADS * DIFF_VDIM), BF16),
        compiler_params=_cparams(("parallel", "parallel"), 40),
        name="diffattn_bounded" if bounded else "diffattn_online",
    )(ti, tj, lvec, qk, qk, mid, subln_g)


def _gla_kernel(q_ref, k_ref, v_ref, la_ref, sgr_ref, g_ref, tri_ref, ones_ref, o_ref, state, *, blk):
    @pl.when(pl.program_id(2) == 0)
    def _():
        state[...] = jnp.zeros(state.shape, F32)

    la_t = la_ref[...].T
    k_t = k_ref[...].astype(F32).T
    hi, lo = _split_bf16(la_t)
    tri = tri_ref[...]
    ones = ones_ref[...]
    cum_t = _dot(hi, tri) + _dot(lo, tri)
    tot_t = _dot(hi, ones) + _dot(lo, ones)
    kd_t = k_t * jnp.exp(tot_t - cum_t)

    n_chunks = blk // CHUNK
    lane = lax.broadcasted_iota(I32, (1, LANES), 1)
    d_states = []
    for ck in range(n_chunks):
        pair = slice((ck // 2) * LANES, (ck // 2 + 1) * LANES)
        in_chunk = (lane // CHUNK) == (ck % 2)
        kd = jnp.where(in_chunk, kd_t[:, pair], 0.0).astype(BF16)
        d_states.append(_dot(kd, v_ref[pair, :]))

    st = state[...]
    states = []
    for ck in range(n_chunks):
        decay = jnp.exp(tot_t[:, ck * CHUNK:ck * CHUNK + 1])
        st = decay * st + d_states[ck]
        states.append(st.astype(BF16))
    state[...] = st

    o = jnp.concatenate([_dot(q_ref[ck * CHUNK:(ck + 1) * CHUNK, :], states[ck]) for ck in range(n_chunks)],
                        axis=0) * (GLA_KDIM ** -0.5)
    o_ref[...] = (_rms(o, g_ref[...]) * sgr_ref[...].astype(F32)).astype(BF16)


def _gla(mid, log_a, sgr, out_g, *, B, S, blk):
    T = B * S
    ns = S // blk
    kern = functools.partial(_gla_kernel, blk=blk)
    q_col0 = (DIFF_HEADS * DIFF_VDIM) // GLA_KDIM
    k_col0 = q_col0 + GLA_HEADS
    v_col0 = (DIFF_HEADS * DIFF_VDIM + 2 * GLA_HEADS * GLA_KDIM) // GLA_VDIM
    r = jnp.arange(blk, dtype=I32)[:, None]
    c = jnp.arange(blk, dtype=I32)[None, :]
    same = (r // CHUNK) == (c // CHUNK)
    tri = (same & (r <= c)).astype(BF16)
    ones = same.astype(BF16)
    return pl.pallas_call(
        kern,
        grid=(B, GLA_HEADS, ns),
        in_specs=[
            pl.BlockSpec((blk, GLA_KDIM), lambda b, h, s: (b * ns + s, q_col0 + h)),
            pl.BlockSpec((blk, GLA_KDIM), lambda b, h, s: (b * ns + s, k_col0 + h)),
            pl.BlockSpec((blk, GLA_VDIM), lambda b, h, s: (b * ns + s, v_col0 + h)),
            pl.BlockSpec((blk, GLA_KDIM), lambda b, h, s: (b * ns + s, h)),
            pl.BlockSpec((blk, GLA_VDIM), lambda b, h, s: (b * ns + s, h)),
            pl.BlockSpec((1, GLA_VDIM), lambda b, h, s: (0, 0)),
            pl.BlockSpec((blk, blk), lambda b, h, s: (0, 0)),
            pl.BlockSpec((blk, blk), lambda b, h, s: (0, 0)),
        ],
        out_specs=pl.BlockSpec((blk, GLA_VDIM), lambda b, h, s: (b * ns + s, h)),
        out_shape=jax.ShapeDtypeStruct((T, GLA_HEADS * GLA_VDIM), BF16),
        scratch_shapes=[pltpu.VMEM((GLA_KDIM, GLA_VDIM), F32)],
        compiler_params=_cparams(("parallel", "parallel", "arbitrary"), 32),
        name="gla",
    )(mid, mid, mid, log_a, sgr, out_g, tri, ones)


def _resident_w_map(n_j):
    return lambda i, j: (0, jnp.where(i == 0, j, n_j - 1))


def _outproj_kernel(a_ref, b_ref, wa_ref, wb_ref, x_ref, o_ref, w_scr):
    j = pl.program_id(1)
    ka = a_ref.shape[1]

    @pl.when(pl.program_id(0) == 0)
    def _():
        w_scr[j, 0:ka, :] = wa_ref[...].astype(BF16)
        w_scr[j, ka:, :] = wb_ref[...].astype(BF16)

    acc = _dot(a_ref[...], w_scr[j, 0:ka, :]) + _dot(b_ref[...], w_scr[j, ka:, :])
    o_ref[...] = x_ref[...] + acc


def _outproj(a, b, w_out, x2, *, tm, tn):
    T, ka = a.shape
    kb = b.shape[1]
    assert ka == kb
    D = w_out.shape[1]
    n_j = D // tn
    return pl.pallas_call(
        _outproj_kernel,
        grid=(T // tm, n_j),
        in_specs=[
            pl.BlockSpec((tm, ka), lambda i, j: (i, 0)),
            pl.BlockSpec((tm, kb), lambda i, j: (i, 0)),
            pl.BlockSpec((ka, tn), _resident_w_map(n_j)),
            pl.BlockSpec((kb, tn), lambda i, j: (1, jnp.where(i == 0, j, n_j - 1))),
            pl.BlockSpec((tm, tn), lambda i, j: (i, j)),
        ],
        out_specs=pl.BlockSpec((tm, tn), lambda i, j: (i, j)),
        out_shape=jax.ShapeDtypeStruct((T, D), F32),
        scratch_shapes=[pltpu.VMEM((n_j, ka + kb, tn), BF16)],
        compiler_params=_cparams(("arbitrary", "arbitrary"), 48),
        name="outproj",
    )(a, b, w_out, w_out, x2)


def _normproj_kernel(x_ref, g_ref, w_ref, hg_ref, o_ref, n_scr, *w_scr, n_norm):
    j = pl.program_id(1)

    @pl.when(j == 0)
    def _():
        n_scr[...] = _rms(x_ref[...], g_ref[...]).astype(BF16)

    if w_scr:
        @pl.when(pl.program_id(0) == 0)
        def _():
            w_scr[0][j] = w_ref[...].astype(BF16)
        y = _dot(n_scr[...], w_scr[0][j])
    else:
        y = _dot(n_scr[...], w_ref[...].astype(BF16))

    @pl.when(j < n_norm)
    def _():
        o_ref[...] = _rms(y, hg_ref[...]).astype(BF16)

    @pl.when(j >= n_norm)
    def _():
        o_ref[...] = y.astype(BF16)


def _normproj(x2, g, w, head_g, *, tm, tn, n_norm, name):
    T, D = x2.shape
    N = w.shape[1]
    kern = functools.partial(_normproj_kernel, n_norm=n_norm)
    n_j = N // tn
    resident = T // tm > 1
    return pl.pallas_call(
        kern,
        grid=(T // tm, n_j),
        in_specs=[
            pl.BlockSpec((tm, D), lambda i, j: (i, 0)),
            pl.BlockSpec((1, D), lambda i, j: (0, 0)),
            pl.BlockSpec((D, tn), _resident_w_map(n_j) if resident else (lambda i, j: (0, j))),
            pl.BlockSpec((1, tn), lambda i, j: (0, 0)),
        ],
        out_specs=pl.BlockSpec((tm, tn), lambda i, j: (i, j)),
        out_shape=jax.ShapeDtypeStruct((T, N), BF16),
        scratch_shapes=[pltpu.VMEM((tm, D), BF16)] + ([pltpu.VMEM((n_j, D, tn), BF16)] if resident else []),
        compiler_params=_cparams(("arbitrary", "arbitrary"), 48),
        name=name,
    )(x2, g, w, head_g)


def _cross_kernel(q_ref, k_ref, v_ref, w_ref, h_ref, o_ref, att_scr, w_scr, *, hdim):
    j = pl.program_id(1)

    @pl.when(pl.program_id(0) == 0)
    def _():
        w_scr[j] = w_ref[...].astype(BF16)

    @pl.when(j == 0)
    def _():
        for hd in range(CROSS_HEADS):
            cols = slice(hd * hdim, (hd + 1) * hdim)
            s = lax.dot_general(q_ref[:, cols], k_ref[:, cols], NT_DIMS, preferred_element_type=F32)
            p = jnp.exp(s - jnp.max(s, axis=-1, keepdims=True))
            l = jnp.sum(p, axis=-1, keepdims=True)
            att_scr[:, cols] = (_dot(p.astype(BF16), v_ref[:, cols]) / l).astype(BF16)

    o_ref[...] = h_ref[...] + _dot(att_scr[...], w_scr[j])


def _cross(qc, kv, w_co, h1, *, S, n_mem, tm, tn):
    T, D = qc.shape
    per_b = S // tm
    kern = functools.partial(_cross_kernel, hdim=D // CROSS_HEADS)
    n_j = D // tn
    return pl.pallas_call(
        kern,
        grid=(T // tm, n_j),
        in_specs=[
            pl.BlockSpec((tm, D), lambda i, j: (i, 0)),
            pl.BlockSpec((n_mem, D), lambda i, j: (i // per_b, 0)),
            pl.BlockSpec((n_mem, D), lambda i, j: (i // per_b, 1)),
            pl.BlockSpec((D, tn), _resident_w_map(n_j)),
            pl.BlockSpec((tm, tn), lambda i, j: (i, j)),
        ],
        out_specs=pl.BlockSpec((tm, tn), lambda i, j: (i, j)),
        out_shape=jax.ShapeDtypeStruct((T, D), F32),
        scratch_shapes=[pltpu.VMEM((tm, D), BF16), pltpu.VMEM((n_j, D, tn), BF16)],
        compiler_params=_cparams(("arbitrary", "arbitrary"), 48),
        name="cross",
    )(qc, kv, kv, w_co, h1)


R_ROWS = SUBLANES + N_EXPERTS


def _pack_bf16_pairs(xb16):
    c = xb16.shape[1] // 2
    u = lax.bitcast_convert_type(xb16.astype(F32), jnp.uint32)
    return (u[:, :c] >> 16) | (u[:, c:] & jnp.uint32(0xFFFF0000))


def _store_row_tiles(ref, x):
    for g in range(SUBLANES):
        ref[:, g, :] = x[:, g * LANES:(g + 1) * LANES]


def _load_row_tiles(ref):
    return jnp.concatenate([ref[:, g, :] for g in range(SUBLANES)], axis=1)


def _unpack_bf16_pairs_f32(w):
    lo = lax.bitcast_convert_type(w << 16, F32)
    hi = lax.bitcast_convert_type(w & jnp.uint32(0xFFFF0000), F32)
    return lo, hi


def _unpack_bf16_pairs(w):
    lo, hi = _unpack_bf16_pairs_f32(w)
    return lo.astype(BF16), hi.astype(BF16)


def _router_kernel(h_ref, g_ref, wt_ref, b_ref, eid_ref, gate_ref, xn_ref):
    n = _rms(h_ref[...], g_ref[...])
    nh, nl = _split_bf16(n)
    _store_row_tiles(xn_ref, _pack_bf16_pairs(nh))
    wh, wl = _split_bf16(wt_ref[...])
    nt = functools.partial(lax.dot_general, dimension_numbers=NT_DIMS, preferred_element_type=F32)
    lg = nt(wh, nh) + nt(wh, nl) + nt(wl, nh) + b_ref[...]

    tm = lg.shape[1]
    row = lax.broadcasted_iota(I32, (SUBLANES, tm), 0)

    def first_argmax(v, vmax):
        return jnp.min(jnp.where(v == vmax, row, SUBLANES), axis=0, keepdims=True)

    gl = jnp.where(row < N_GROUPS, lg[0:SUBLANES], NEG_INF)
    gmax = jnp.max(gl, axis=0, keepdims=True)
    grp = first_argmax(gl, gmax)
    grp_w = 1.0 / jnp.sum(jnp.exp(gl - gmax), axis=0, keepdims=True)

    sel = jnp.zeros((SUBLANES, tm), F32)
    for gi in range(N_GROUPS):
        lo = SUBLANES + gi * EXPERTS_PER_GROUP
        sel = jnp.where(grp == gi, lg[lo:lo + EXPERTS_PER_GROUP], sel)
    e = jnp.exp(sel - jnp.max(sel, axis=0, keepdims=True))
    prob = e / jnp.sum(e, axis=0, keepdims=True)
    p1 = jnp.max(prob, axis=0, keepdims=True)
    i1 = first_argmax(prob, p1)
    rest = jnp.where(row == i1, -1.0, prob)
    p2 = jnp.max(rest, axis=0, keepdims=True)
    i2 = first_argmax(rest, p2)
    den = p1 + p2
    base = grp * EXPERTS_PER_GROUP
    eid_ref[...] = jnp.where(row == 0, base + i1, jnp.where(row == 1, base + i2, 0))
    gate_ref[...] = jnp.where(row == 0, grp_w * p1 / den, jnp.where(row == 1, grp_w * p2 / den, 0.0))


def _router(h2, g_ffn, w_rt, b_r, *, tm):
    T, D = h2.shape
    return pl.pallas_call(
        _router_kernel,
        grid=(T // tm,),
        in_specs=[
            pl.BlockSpec((tm, D), lambda i: (i, 0)),
            pl.BlockSpec((1, D), lambda i: (0, 0)),
            pl.BlockSpec((R_ROWS, D), lambda i: (0, 0)),
            pl.BlockSpec((R_ROWS, 1), lambda i: (0, 0)),
        ],
        out_specs=[
            pl.BlockSpec((SUBLANES, tm), lambda i: (0, i)),
            pl.BlockSpec((SUBLANES, tm), lambda i: (0, i)),
            pl.BlockSpec((tm, SUBLANES, LANES), lambda i: (i, 0, 0)),
        ],
        out_shape=[
            jax.ShapeDtypeStruct((SUBLANES, T), I32),
            jax.ShapeDtypeStruct((SUBLANES, T), F32),
            jax.ShapeDtypeStruct((T, SUBLANES, LANES), jnp.uint32),
        ],
        compiler_params=_cparams(("parallel",), 32),
        name="router",
    )(h2, g_ffn, w_rt, b_r)


def _rank_kernel(eid_ref, rank_ref, cnt_ref, carry):
    @pl.when(pl.program_id(0) == 0)
    def _():
        carry[...] = jnp.zeros(carry.shape, F32)

    tm = eid_ref.shape[1]
    e0 = eid_ref[0:1, :]
    e1 = eid_ref[1:2, :]
    erow = lax.broadcasted_iota(I32, (N_EXPERTS, tm), 0)
    hit = jnp.where((erow == e0) | (erow == e1), 1.0, 0.0)
    r = lax.broadcasted_iota(I32, (tm, tm), 0)
    c = lax.broadcasted_iota(I32, (tm, tm), 1)
    before = jnp.where(r < c, 1.0, 0.0).astype(BF16)
    pre = _dot(hit.astype(BF16), before) + carry[:, 0:1]
    rank0 = jnp.sum(jnp.where(erow == e0, pre, 0.0), axis=0, keepdims=True)
    rank1 = jnp.sum(jnp.where(erow == e1, pre, 0.0), axis=0, keepdims=True)
    row = lax.broadcasted_iota(I32, (SUBLANES, tm), 0)
    rank_ref[...] = jnp.where(row == 0, rank0, jnp.where(row == 1, rank1, 0.0)).astype(I32)
    total = carry[...] + jnp.sum(hit, axis=1, keepdims=True)
    carry[...] = total
    cnt_ref[...] = total.astype(I32)


def _rank(eid, *, tm):
    T = eid.shape[1]
    return pl.pallas_call(
        _rank_kernel,
        grid=(T // tm,),
        in_specs=[pl.BlockSpec((SUBLANES, tm), lambda i: (0, i))],
        out_specs=[
            pl.BlockSpec((SUBLANES, tm), lambda i: (0, i)),
            pl.BlockSpec((N_EXPERTS, LANES), lambda i: (0, 0)),
        ],
        out_shape=[
            jax.ShapeDtypeStruct((SUBLANES, T), I32),
            jax.ShapeDtypeStruct((N_EXPERTS, LANES), I32),
        ],
        scratch_shapes=[pltpu.VMEM((N_EXPERTS, LANES), F32)],
        compiler_params=_cparams(("arbitrary",), 32),
        name="rank",
    )(eid)


def _dispatch_kernel(dest_ref, pad0_ref, npad_ref, nb_ref, xn_ref, xb_hbm, zbuf, sem, psem,
                     *, n_tok, tb, n_blocks):
    i = pl.program_id(0)
    tm = xn_ref.shape[0]
    base = i * tm

    def row_body(r, carry):
        for k in range(TOP_K):
            pltpu.make_async_copy(xn_ref.at[r], xb_hbm.at[dest_ref[k * n_tok + base + r]], sem).start(priority=k)
        return carry
    lax.fori_loop(0, tm, row_body, 0, unroll=8)

    @pl.when(i == 0)
    def _():
        _dispatch_fill(pad0_ref, npad_ref, nb_ref, xb_hbm, zbuf, psem, tb=tb, n_blocks=n_blocks)

    for k in range(TOP_K):
        pltpu.make_async_copy(xn_ref, xb_hbm.at[pl.ds(0, tm)], sem).wait()


def _dispatch_fill(pad0_ref, npad_ref, nb_ref, xb_hbm, zbuf, psem, *, tb, n_blocks):
    zbuf[...] = jnp.zeros(zbuf.shape, zbuf.dtype)

    def pad_copy(e):
        n = npad_ref[e]
        return pltpu.make_async_copy(zbuf.at[pl.ds(0, n)], xb_hbm.at[pl.ds(pad0_ref[e], n)], psem.at[0])

    def tail_copy(blk):
        return pltpu.make_async_copy(zbuf, xb_hbm.at[pl.ds(pl.multiple_of(blk * tb, tb), tb)], psem.at[1])

    def for_each_pad(fn):
        def body(e, c):
            @pl.when(npad_ref[e] > 0)
            def _():
                fn(e)
            return c
        lax.fori_loop(0, N_EXPERTS, body, 0)

    def for_each_tail(fn):
        def body(b, c):
            fn(b)
            return c
        lax.fori_loop(nb_ref[0], n_blocks, body, 0)

    for_each_pad(lambda e: pad_copy(e).start())
    for_each_tail(lambda b: tail_copy(b).start())
    for_each_pad(lambda e: pad_copy(e).wait())
    for_each_tail(lambda b: tail_copy(b).wait())


def _dispatch(dest, pad0, npad, nb, xn, *, tb, n_blocks, tm):
    T = xn.shape[0]
    tile = xn.shape[1:]
    kern = functools.partial(_dispatch_kernel, n_tok=T, tb=tb, n_blocks=n_blocks)
    grid_spec = pltpu.PrefetchScalarGridSpec(
        num_scalar_prefetch=4,
        grid=(T // tm,),
        in_specs=[pl.BlockSpec((tm,) + tile, lambda i, *_: (i, 0, 0))],
        out_specs=pl.BlockSpec(memory_space=pl.ANY),
        scratch_shapes=[
            pltpu.VMEM((tb,) + tile, xn.dtype),
            pltpu.SemaphoreType.DMA(()),
            pltpu.SemaphoreType.DMA((2,)),
        ],
    )
    return pl.pallas_call(
        kern,
        grid_spec=grid_spec,
        out_shape=jax.ShapeDtypeStruct((n_blocks * tb,) + tile, xn.dtype),
        compiler_params=_cparams(("arbitrary",), 32),
        name="dispatch",
    )(dest, pad0, npad, nb, xn)


def _expert_kernel(be_ref, first_ref, slot_ref, nxt_ref, run1_ref, nb_ref, x_ref, wg_hbm, wu_hbm, wd_hbm, y_ref,
                   wg_f, wu_f, wd_f, wsem, wg_b, wu_b, wd_b):
    i = pl.program_id(0)
    nb = nb_ref[0]

    def weight_copies(e, slot):
        return (pltpu.make_async_copy(wg_hbm.at[e], wg_f.at[slot], wsem.at[3 * slot]),
                pltpu.make_async_copy(wu_hbm.at[e], wu_f.at[slot], wsem.at[3 * slot + 1]),
                pltpu.make_async_copy(wd_hbm.at[e], wd_f.at[slot], wsem.at[3 * slot + 2]))

    @pl.when(i == 0)
    def _():
        for cp in weight_copies(be_ref[0], 0):
            cp.start()

    @pl.when((i == 0) & (run1_ref[0] >= 0))
    def _():
        for cp in weight_copies(jnp.maximum(run1_ref[0], 0), 1):
            cp.start()

    first = (i < nb) & (first_ref[i] == 1)
    slot = slot_ref[i]

    @pl.when(first)
    def _():
        for cp in weight_copies(0, slot):
            cp.wait()
        wg_b[...] = wg_f[slot].astype(BF16)
        wu_b[...] = wu_f[slot].astype(BF16)
        wd_b[...] = wd_f[slot].astype(BF16)

    @pl.when(first & (nxt_ref[i] >= 0))
    def _():
        for cp in weight_copies(jnp.maximum(nxt_ref[i], 0), slot):
            cp.start()

    @pl.when(i < nb)
    def _():
        n_lo, n_hi = _unpack_bf16_pairs(_load_row_tiles(x_ref))
        half = n_lo.shape[1]
        a = _dot(n_lo, wg_b[0:half, :]) + _dot(n_hi, wg_b[half:, :])
        u = _dot(n_lo, wu_b[0:half, :]) + _dot(n_hi, wu_b[half:, :])
        hdn = (a / (1.0 + jnp.exp(-a))) * u
        y_ref[...] = _pack_bf16_pairs(_dot(hdn.astype(BF16), wd_b[...]).astype(BF16))

    @pl.when(i >= nb)
    def _():
        y_ref[...] = jnp.zeros(y_ref.shape, y_ref.dtype)


def _experts(xb, blk_e, blk_first, blk_slot, blk_next, run1, nb, w_gate, w_up, w_down, *, tb, n_blocks):
    D, De = w_gate.shape[1:]
    assert xb.shape[1:] == (SUBLANES, LANES) and D == 2 * SUBLANES * LANES

    def x_map(i, be, first, slot, nxt, r1, nbr):
        return (jnp.minimum(i, nbr[0] - 1), 0, 0)

    grid_spec = pltpu.PrefetchScalarGridSpec(
        num_scalar_prefetch=6,
        grid=(n_blocks,),
        in_specs=[
            pl.BlockSpec((tb, SUBLANES, LANES), x_map),
            pl.BlockSpec(memory_space=pl.ANY),
            pl.BlockSpec(memory_space=pl.ANY),
            pl.BlockSpec(memory_space=pl.ANY),
        ],
        out_specs=pl.BlockSpec((tb, D // 2), lambda i, *_: (i, 0)),
        scratch_shapes=[
            pltpu.VMEM((2, D, De), F32),
            pltpu.VMEM((2, D, De), F32),
            pltpu.VMEM((2, De, D), F32),
            pltpu.SemaphoreType.DMA((6,)),
            pltpu.VMEM((D, De), BF16),
            pltpu.VMEM((D, De), BF16),
            pltpu.VMEM((De, D), BF16),
        ],
    )
    return pl.pallas_call(
        _expert_kernel,
        grid_spec=grid_spec,
        out_shape=jax.ShapeDtypeStruct((n_blocks * tb, D // 2), jnp.uint32),
        compiler_params=_cparams(("arbitrary",), 52),
        name="experts",
    )(blk_e, blk_first, blk_slot, blk_next, run1, nb, xb, w_gate, w_up, w_down)


def _combine_kernel(dest_ref, y_hbm, h_ref, gate_ref, o_ref, ybuf, sem, *, tm, n_tok):
    i = pl.program_id(0)
    n = pl.num_programs(0)

    def row_copy(d, k, r, slot):
        return pltpu.make_async_copy(y_hbm.at[pl.ds(d, 1), :], ybuf.at[slot, k, pl.ds(r, 1), :], sem.at[slot])

    def start_gather(blk, slot):
        def body(r, carry):
            for k in range(TOP_K):
                row_copy(dest_ref[k * n_tok + blk * tm + r], k, r, slot).start(priority=k)
            return carry
        lax.fori_loop(0, tm, body, 0, unroll=8)

    def wait_gather(slot):
        for k in range(TOP_K):
            pltpu.make_async_copy(y_hbm.at[pl.ds(0, tm), :], ybuf.at[slot, k], sem.at[slot]).wait()

    @pl.when(i == 0)
    def _():
        start_gather(0, 0)

    @pl.when(i + 1 < n)
    def _():
        start_gather(i + 1, (i + 1) % 2)

    slot = i % 2
    wait_gather(slot)
    gt = gate_ref[...]
    half = h_ref.shape[1] // 2
    y0_lo, y0_hi = _unpack_bf16_pairs_f32(ybuf[slot, 0])
    y1_lo, y1_hi = _unpack_bf16_pairs_f32(ybuf[slot, 1])
    o_ref[:, 0:half] = h_ref[:, 0:half] + gt[:, 0:1] * y0_lo + gt[:, 1:2] * y1_lo
    o_ref[:, half:] = h_ref[:, half:] + gt[:, 0:1] * y0_hi + gt[:, 1:2] * y1_hi


def _combine(dest, yb, h2, gate, *, tm):
    T, D = h2.shape
    kern = functools.partial(_combine_kernel, tm=tm, n_tok=T)
    grid_spec = pltpu.PrefetchScalarGridSpec(
        num_scalar_prefetch=1,
        grid=(T // tm,),
        in_specs=[
            pl.BlockSpec(memory_space=pl.ANY),
            pl.BlockSpec((tm, D), lambda i, d: (i, 0)),
            pl.BlockSpec((tm, TOP_K), lambda i, d: (i, 0)),
        ],
        out_specs=pl.BlockSpec((tm, D), lambda i, d: (i, 0)),
        scratch_shapes=[
            pltpu.VMEM((2, TOP_K, tm, D // 2), jnp.uint32),
            pltpu.SemaphoreType.DMA((2,)),
        ],
    )
    return pl.pallas_call(
        kern,
        grid_spec=grid_spec,
        out_shape=jax.ShapeDtypeStruct((T, D), F32),
        compiler_params=_cparams(("arbitrary",), 40),
        name="combine",
    )(dest, yb, h2, gate)


EXPERT_ROWS = 256


def kernel(x, mem, positions, g_attn, w_in, q_norm_g, k_norm_g, lambda_q1, lambda_k1, lambda_q2, lambda_k2, diff_subln_g, gla_w_a2, gla_b_a, gla_out_g, w_out, g_cross, g_mem, w_cq, w_ckv, cq_norm_g, ck_norm_g, w_co, g_ffn, w_router_grp, b_router_grp, w_router_exp, b_router_exp, w_gate, w_up, w_down):
    B, S, D = x.shape
    T = B * S
    n_mem = mem.shape[1]
    l = 0
    x2 = x.reshape(T, D)

    half = DIFF_QKDIM // 2
    freq = ROPE_THETA ** (-jnp.arange(half, dtype=F32) / half)
    freq = jnp.tile(freq, LANES // half)[None, :]
    q_scale = math.log2(math.e) * DIFF_QKDIM ** -0.5
    qkg = jnp.stack([jnp.tile(q_norm_g[l], 2) * q_scale, jnp.tile(k_norm_g[l], 2)])
    score_bound = 1.01 * DIFF_QKDIM * q_scale * jnp.max(jnp.abs(q_norm_g[l])) * jnp.max(jnp.abs(k_norm_g[l]))
    lvec = jnp.stack([lambda_q1[l], lambda_k1[l], lambda_q2[l], lambda_k2[l]])

    qk, mid, log_a, sgr = _inproj(x2, g_attn[l][None], positions.reshape(T, 1), freq, qkg, w_in[l].T,
                                  gla_w_a2[l], gla_b_a[l][None], tm=1024)
    diffattn = functools.partial(_diffattn, lvec, qk, mid, diff_subln_g[l][None], B=B, S=S, tq=512)
    mix_d = lax.cond(score_bound <= SCORE_BOUND,
                     functools.partial(diffattn, bounded=True), functools.partial(diffattn, bounded=False))
    mix_g = _gla(mid, log_a, sgr, gla_out_g[l][None], B=B, S=S, blk=512)
    h1 = _outproj(mix_d, mix_g, w_out[l], x2, tm=1024, tn=512)

    hdim = D // CROSS_HEADS
    qc = _normproj(h1, g_cross[l][None], w_cq[l], cq_norm_g[l][None] * (hdim ** -0.5),
                   tm=1024, tn=hdim, n_norm=CROSS_HEADS, name="cq")
    kv = _normproj(mem.reshape(B * n_mem, D), g_mem[l][None], w_ckv[l], ck_norm_g[l][None],
                   tm=B * n_mem, tn=hdim, n_norm=CROSS_HEADS, name="ckv")
    h2 = _cross(qc, kv, w_co[l], h1, S=S, n_mem=n_mem, tm=1024, tn=512)

    w_rt = jnp.concatenate([w_router_grp[l].T, jnp.zeros((SUBLANES - N_GROUPS, D), F32), w_router_exp[l].T])
    b_r = jnp.concatenate([b_router_grp[l], jnp.zeros((SUBLANES - N_GROUPS,), F32), b_router_exp[l]])[:, None]
    eid, gate, xn = _router(h2, g_ffn[l][None], w_rt, b_r, tm=512)
    rank, cnt = _rank(eid, tm=512)

    tb = EXPERT_ROWS
    n_blocks = (T * TOP_K + N_EXPERTS * (tb - 1) + tb - 1) // tb
    counts = cnt[:, 0]
    pcounts = ((counts + tb - 1) // tb) * tb
    pends = jnp.cumsum(pcounts)
    pstarts = pends - pcounts
    nb = (pends[-1:] // tb).astype(I32)
    eids = jnp.arange(N_EXPERTS, dtype=I32)
    blk_start = jnp.arange(n_blocks, dtype=I32) * tb
    blk_e = jnp.minimum(jnp.sum(pends[None, :] <= blk_start[:, None], axis=1), N_EXPERTS - 1).astype(I32)
    blk_first = jnp.concatenate([jnp.ones((1,), I32), (blk_e[1:] != blk_e[:-1]).astype(I32)])
    used = jnp.where(counts > 0, eids, N_EXPERTS)
    next_used = jnp.concatenate([lax.cummin(used[::-1])[::-1][1:], jnp.full((1,), N_EXPERTS, I32)])
    next_used = jnp.where(next_used < N_EXPERTS, next_used, -1)

    def table_at(idx, table):
        return jnp.where(idx >= 0, jnp.sum(jnp.where(idx[:, None] == eids, table, 0), axis=1), -1).astype(I32)

    after_next = table_at(next_used, next_used)
    run_of = (jnp.cumsum((counts > 0).astype(I32)) - 1).astype(I32)
    blk_next = table_at(blk_e, after_next)
    blk_slot = table_at(blk_e, run_of % 2)
    run1 = table_at(blk_e[:1], next_used)
    pick = eid[:TOP_K]
    pstart_of = jnp.sum(jnp.where(pick[..., None] == eids, pstarts, 0), axis=-1)
    dest = (pstart_of + rank[:TOP_K]).astype(I32).reshape(-1)
    pad0 = (pstarts + counts).astype(I32)
    npad = (pcounts - counts).astype(I32)

    xb = _dispatch(dest, pad0, npad, nb, xn, tb=tb, n_blocks=n_blocks, tm=512)
    yb = _experts(xb, blk_e, blk_first, blk_slot, blk_next, run1, nb, w_gate[l], w_up[l], w_down[l],
                  tb=tb, n_blocks=n_blocks)
    out = _combine(dest, yb, h2, gate[:TOP_K].T, tm=512)
    return out.reshape(B, S, D)
```

```python
import functools
import math

import jax
import jax.numpy as jnp
from jax import lax
from jax.experimental import pallas as pl
from jax.experimental.pallas import tpu as pltpu

F32 = jnp.float32
BF16 = jnp.bfloat16
I32 = jnp.int32

LANES = 128
SUBLANES = 8

CHUNK = 64
ROPE_THETA = 10000.0
NORM_EPS = 1e-6
NEG_INF = -1e30
DIFF_HEADS = 8
DIFF_VDIM = 128
DIFF_QKDIM = 64
GLA_HEADS = 4
GLA_VDIM = 256
GLA_KDIM = 128
GLA_GATE_RANK = 16
GLA_TAU = 16.0
CROSS_HEADS = 4
N_GROUPS = 4
EXPERTS_PER_GROUP = 8
N_EXPERTS = N_GROUPS * EXPERTS_PER_GROUP
TOP_K = 2
LAM_INIT = 0.8 - 0.6 * math.exp(-0.3 * 0)

NT_DIMS = (((1,), (1,)), ((), ()))


def _cparams(semantics, vmem_mib):
    return pltpu.CompilerParams(dimension_semantics=semantics,
                                vmem_limit_bytes=vmem_mib * 1024 * 1024)


def _dot(a, b):
    return jnp.dot(a, b, preferred_element_type=F32)


def _dot_nt(a, b):
    return lax.dot_general(a, b, NT_DIMS, preferred_element_type=F32)


def _rms(x, g):
    ms = jnp.mean(x * x, axis=-1, keepdims=True)
    return x * lax.rsqrt(ms + NORM_EPS) * g


def _split_bf16(x):
    hi = x.astype(BF16)
    lo = (x - hi.astype(F32)).astype(BF16)
    return hi, lo


TN = 512
J_QK = 4
J_MID = 6
J_LR = J_QK + J_MID
J_GR = J_LR + 1
N_J = J_GR + 2


def _inproj_kernel(x_ref, g_ref, pos_ref, freq_ref, qkg_ref, w_ref, wgr_ref, wlr_ref, wa2_ref, ba_ref,
                   qk_ref, mid_ref, loga_ref, sgr_ref, n_scr, cos_scr, sin_scr, y_scr):
    j = pl.program_id(1)

    @pl.when(j == 0)
    def _():
        n_scr[...] = _rms(x_ref[...], g_ref[...]).astype(BF16)
        ang = pos_ref[...].astype(F32) * freq_ref[...]
        cos_scr[...] = jnp.cos(ang)
        sin_scr[...] = jnp.sin(ang)

    def qk_epilogue(jq):
        y_prev = y_scr.at[jq % 2]
        lane = lax.broadcasted_iota(I32, (1, LANES), 1)
        low_seg = lane < DIFF_QKDIM
        first_half = (lane % DIFF_QKDIM) < (DIFF_QKDIM // 2)
        gain = qkg_ref[jq // (J_QK // 2):jq // (J_QK // 2) + 1, :]
        cos = cos_scr[...]
        sin = sin_scr[...]
        for c in range(TN // LANES):
            yb = y_prev[:, c * LANES:(c + 1) * LANES]
            y2 = yb * yb
            s_lo = jnp.sum(jnp.where(low_seg, y2, 0.0), axis=-1, keepdims=True)
            s_hi = jnp.sum(jnp.where(low_seg, 0.0, y2), axis=-1, keepdims=True)
            ms = jnp.where(low_seg, s_lo, s_hi) * (1.0 / DIFF_QKDIM)
            yn = yb * lax.rsqrt(ms + NORM_EPS) * gain
            rot = jnp.where(first_half,
                            -pltpu.roll(yn, LANES - DIFF_QKDIM // 2, 1),
                            pltpu.roll(yn, DIFF_QKDIM // 2, 1))
            qk_ref[:, c * LANES:(c + 1) * LANES] = (yn * cos + rot * sin).astype(BF16)

    for jq in range(J_QK + 1):
        @pl.when(j == jq)
        def _():
            y = _dot_nt(n_scr[...], w_ref[...].astype(BF16))
            if jq > 0:
                qk_epilogue(jq - 1)
            if jq < J_QK:
                y_scr[jq % 2] = y
            else:
                mid_ref[...] = y.astype(BF16)

    @pl.when((j > J_QK) & (j < J_LR))
    def _():
        mid_ref[...] = _dot_nt(n_scr[...], w_ref[...].astype(BF16)).astype(BF16)

    @pl.when(j == J_LR)
    def _():
        lr = _dot_nt(n_scr[...], wlr_ref[...].astype(BF16))
        z = _dot(lr.astype(BF16), wa2_ref[...].astype(BF16)) + ba_ref[...]
        log_sig = jnp.minimum(z, 0.0) - jnp.log(1.0 + jnp.exp(-jnp.abs(z)))
        loga_ref[...] = log_sig * (1.0 / GLA_TAU)

    @pl.when(j >= J_GR)
    def _():
        y = _dot_nt(n_scr[...], wgr_ref[...].astype(BF16))
        sgr_ref[...] = (y / (1.0 + jnp.exp(-y))).astype(BF16)


def _inproj(x2, g_attn, pos2, freq, qkg, w_t, w_a2, b_a, *, tm):
    T, D = x2.shape
    n_mid = J_MID * TN
    n_gk = GLA_HEADS * GLA_KDIM
    lr0 = J_LR * TN
    gr0 = lr0 + GLA_GATE_RANK
    n_gr = w_t.shape[0] - gr0
    assert n_gr == 2 * TN and lr0 % GLA_GATE_RANK == 0
    return pl.pallas_call(
        _inproj_kernel,
        grid=(T // tm, N_J),
        in_specs=[
            pl.BlockSpec((tm, D), lambda i, j: (i, 0)),
            pl.BlockSpec((1, D), lambda i, j: (0, 0)),
            pl.BlockSpec((tm, 1), lambda i, j: (i, 0)),
            pl.BlockSpec((1, LANES), lambda i, j: (0, 0)),
            pl.BlockSpec((2, LANES), lambda i, j: (0, 0)),
            pl.BlockSpec((TN, D), lambda i, j: (jnp.minimum(j, J_LR - 1), 0)),
            pl.BlockSpec((pl.Element(TN), pl.Element(D)),
                         lambda i, j: (pl.multiple_of(gr0 + TN * jnp.clip(j - J_GR, 0, 1), SUBLANES), 0)),
            pl.BlockSpec((GLA_GATE_RANK, D), lambda i, j: (lr0 // GLA_GATE_RANK, 0)),
            pl.BlockSpec((GLA_GATE_RANK, n_gk), lambda i, j: (0, 0)),
            pl.BlockSpec((1, n_gk), lambda i, j: (0, 0)),
        ],
        out_specs=[
            pl.BlockSpec((tm, TN), lambda i, j: (i, jnp.clip(j - 1, 0, J_QK - 1))),
            pl.BlockSpec((tm, TN), lambda i, j: (i, jnp.clip(j - J_QK, 0, J_MID - 1))),
            pl.BlockSpec((tm, n_gk), lambda i, j: (i, 0)),
            pl.BlockSpec((tm, TN), lambda i, j: (i, jnp.clip(j - J_GR, 0, 1))),
        ],
        out_shape=[
            jax.ShapeDtypeStruct((T, J_QK * TN), BF16),
            jax.ShapeDtypeStruct((T, n_mid), BF16),
            jax.ShapeDtypeStruct((T, n_gk), F32),
            jax.ShapeDtypeStruct((T, n_gr), BF16),
        ],
        scratch_shapes=[
            pltpu.VMEM((tm, D), BF16),
            pltpu.VMEM((tm, LANES), F32),
            pltpu.VMEM((tm, LANES), F32),
            pltpu.VMEM((2, tm, TN), F32),
        ],
        compiler_params=_cparams(("parallel", "arbitrary"), 56),
        name="inproj",
    )(x2, g_attn, pos2, freq, qkg, w_t, w_t, w_t, w_a2, b_a)


SCORE_BOUND = 80.0


def _diffattn_kernel(ti_ref, tj_ref, lv_ref, q_ref, k_ref, v_ref, sg_ref, o_ref,
                     vext, diag_mask, acc1, acc2, m1, m2, *, tq, n_tiles, bounded):
    S = q_ref.shape[0]
    nq = S // tq
    half = tq // 2
    unroll = 7
    diag_unroll = 4 if nq % 4 == 0 else 2

    vext[:, 0:DIFF_VDIM] = v_ref[...]
    vext[:, DIFF_VDIM:] = jnp.ones((S, DIFF_VDIM), BF16)
    row_chunk = lax.broadcasted_iota(I32, (tq, tq), 0) // CHUNK
    col_chunk = lax.broadcasted_iota(I32, (tq, tq), 1) // CHUNK
    diag_mask[...] = jnp.where(col_chunk <= row_chunk, 1.0, 0.0).astype(BF16)

    lane = lax.broadcasted_iota(I32, (1, LANES), 1)

    def block(i):
        return pl.ds(pl.multiple_of(i * tq, tq), tq)

    def q_comps(i):
        q = q_ref[block(i), :]
        zero = jnp.zeros_like(q)
        return jnp.where(lane < DIFF_QKDIM, q, zero), jnp.where(lane < DIFF_QKDIM, zero, q)

    def diag_tile(i):
        q1, q2 = q_comps(i)
        for qc, acc, m in ((q1, acc1, m1), (q2, acc2, m2)):
            for lo, n_keys in ((0, half), (half, tq)):
                rows = pl.ds(pl.multiple_of(i * tq + lo, half), half)
                keys = pl.ds(pl.multiple_of(i * tq, tq), n_keys)
                mask = diag_mask[lo:lo + half, 0:n_keys]
                s = _dot_nt(qc[lo:lo + half], k_ref[keys, :])
                if bounded:
                    acc[rows, :] = _dot(jnp.exp2(s).astype(BF16) * mask, vext[keys, :])
                else:
                    s = jnp.where(mask > 0, s, NEG_INF)
                    m_new = jnp.max(s, axis=-1, keepdims=True)
                    acc[rows, :] = _dot(jnp.exp2(s - m_new).astype(BF16), vext[keys, :])
                    m[rows, :] = m_new

    def full_tile(i, j):
        q1, q2 = q_comps(i)
        rows = block(i)
        k = k_ref[block(j), :]
        v = vext[block(j), :]
        for qc, acc, m in ((q1, acc1, m1), (q2, acc2, m2)):
            s = _dot_nt(qc, k)
            if bounded:
                acc[rows, :] += _dot(jnp.exp2(s).astype(BF16), v)
            else:
                m_old = m[rows, :]
                m_new = jnp.maximum(m_old, jnp.max(s, axis=-1, keepdims=True))
                p = jnp.exp2(s - m_new)
                acc[rows, :] = jnp.exp2(m_old - m_new) * acc[rows, :] + _dot(p.astype(BF16), v)
                m[rows, :] = m_new

    def diag_body(t, carry):
        for u in range(diag_unroll):
            diag_tile(diag_unroll * t + u)
        return carry
    lax.fori_loop(0, nq // diag_unroll, diag_body, 0)

    def full_body(t, carry):
        for u in range(unroll):
            full_tile(ti_ref[unroll * t + u], tj_ref[unroll * t + u])
        return carry
    lax.fori_loop(0, n_tiles // unroll, full_body, 0)
    for t in range(n_tiles - n_tiles % unroll, n_tiles):
        full_tile(ti_ref[t], tj_ref[t])

    lv = lv_ref[...]
    lam = (jnp.exp(jnp.sum(lv[0:1] * lv[1:2], axis=-1, keepdims=True))
           - jnp.exp(jnp.sum(lv[2:3] * lv[3:4], axis=-1, keepdims=True)) + LAM_INIT)

    def out_body(i, carry):
        rows = block(i)
        a1 = acc1[rows, :]
        a2 = acc2[rows, :]
        o = a1[:, :DIFF_VDIM] / a1[:, DIFF_VDIM:] - lam * (a2[:, :DIFF_VDIM] / a2[:, DIFF_VDIM:])
        o_ref[rows, :] = (_rms(o, sg_ref[...]) * (1.0 - LAM_INIT)).astype(BF16)
        return carry
    lax.fori_loop(0, nq, out_body, 0)


def _diffattn(lvec, qk, mid, subln_g, *, B, S, tq, bounded):
    T = B * S
    nq = S // tq
    assert nq % 2 == 0
    tiles = [(i, j) for i in range(nq) for j in range(i)]
    ti = jnp.asarray([t[0] for t in tiles], I32)
    tj = jnp.asarray([t[1] for t in tiles], I32)
    kern = functools.partial(_diffattn_kernel, tq=tq, n_tiles=len(tiles), bounded=bounded)
    m_rows = SUBLANES if bounded else S
    grid_spec = pltpu.PrefetchScalarGridSpec(
        num_scalar_prefetch=2,
        grid=(B, DIFF_HEADS),
        in_specs=[
            pl.BlockSpec((4, DIFF_QKDIM), lambda b, h, *_: (0, 0)),
            pl.BlockSpec((S, LANES), lambda b, h, *_: (b, h)),
            pl.BlockSpec((S, LANES), lambda b, h, *_: (b, DIFF_HEADS + h)),
            pl.BlockSpec((S, LANES), lambda b, h, *_: (b, h)),
            pl.BlockSpec((1, DIFF_VDIM), lambda b, h, *_: (0, 0)),
        ],
        out_specs=pl.BlockSpec((S, DIFF_VDIM), lambda b, h, *_: (b, h)),
        scratch_shapes=[
            pltpu.VMEM((S, 2 * DIFF_VDIM), BF16),
            pltpu.VMEM((tq, tq), BF16),
            pltpu.VMEM((S, 2 * DIFF_VDIM), F32),
            pltpu.VMEM((S, 2 * DIFF_VDIM), F32),
            pltpu.VMEM((m_rows, 1), F32),
            pltpu.VMEM((m_rows, 1), F32),
        ],
    )
    return pl.pallas_call(
        kern,
        grid_spec=grid_spec,
        out_shape=jax.ShapeDtypeStruct((T, DIFF_HEADS * DIFF_VDIM), BF16),
        compiler_params=_cparams(("parallel", "parallel"), 40),
        name="diffattn_bounded" if bounded else "diffattn_online",
    )(ti, tj, lvec, qk, qk, mid, subln_g)


def _gla_kernel(q_ref, k_ref, v_ref, la_ref, sgr_ref, g_ref, tri_ref, ones_ref, o_ref, state, *, blk):
    @pl.when(pl.program_id(2) == 0)
    def _():
        state[...] = jnp.zeros(state.shape, F32)

    la_t = la_ref[...].T
    k_t = k_ref[...].astype(F32).T
    hi, lo = _split_bf16(la_t)
    tri = tri_ref[...]
    ones = ones_ref[...]
    cum_t = _dot(hi, tri) + _dot(lo, tri)
    tot_t = _dot(hi, ones) + _dot(lo, ones)
    kd_t = k_t * jnp.exp(tot_t - cum_t)

    n_chunks = blk // CHUNK
    lane = lax.broadcasted_iota(I32, (1, LANES), 1)
    d_states = []
    for ck in range(n_chunks):
        pair = slice((ck // 2) * LANES, (ck // 2 + 1) * LANES)
        in_chunk = (lane // CHUNK) == (ck % 2)
        kd = jnp.where(in_chunk, kd_t[:, pair], 0.0).astype(BF16)
        d_states.append(_dot(kd, v_ref[pair, :]))

    st = state[...]
    states = []
    for ck in range(n_chunks):
        decay = jnp.exp(tot_t[:, ck * CHUNK:ck * CHUNK + 1])
        st = decay * st + d_states[ck]
        states.append(st.astype(BF16))
    state[...] = st

    o = jnp.concatenate([_dot(q_ref[ck * CHUNK:(ck + 1) * CHUNK, :], states[ck]) for ck in range(n_chunks)],
                        axis=0) * (GLA_KDIM ** -0.5)
    o_ref[...] = (_rms(o, g_ref[...]) * sgr_ref[...].astype(F32)).astype(BF16)


def _gla(mid, log_a, sgr, out_g, *, B, S, blk):
    T = B * S
    ns = S // blk
    kern = functools.partial(_gla_kernel, blk=blk)
    q_col0 = (DIFF_HEADS * DIFF_VDIM) // GLA_KDIM
    k_col0 = q_col0 + GLA_HEADS
    v_col0 = (DIFF_HEADS * DIFF_VDIM + 2 * GLA_HEADS * GLA_KDIM) // GLA_VDIM
    r = jnp.arange(blk, dtype=I32)[:, None]
    c = jnp.arange(blk, dtype=I32)[None, :]
    same = (r // CHUNK) == (c // CHUNK)
    tri = (same & (r <= c)).astype(BF16)
    ones = same.astype(BF16)
    return pl.pallas_call(
        kern,
        grid=(B, GLA_HEADS, ns),
        in_specs=[
            pl.BlockSpec((blk, GLA_KDIM), lambda b, h, s: (b * ns + s, q_col0 + h)),
            pl.BlockSpec((blk, GLA_KDIM), lambda b, h, s: (b * ns + s, k_col0 + h)),
            pl.BlockSpec((blk, GLA_VDIM), lambda b, h, s: (b * ns + s, v_col0 + h)),
            pl.BlockSpec((blk, GLA_KDIM), lambda b, h, s: (b * ns + s, h)),
            pl.BlockSpec((blk, GLA_VDIM), lambda b, h, s: (b * ns + s, h)),
            pl.BlockSpec((1, GLA_VDIM), lambda b, h, s: (0, 0)),
            pl.BlockSpec((blk, blk), lambda b, h, s: (0, 0)),
            pl.BlockSpec((blk, blk), lambda b, h, s: (0, 0)),
        ],
        out_specs=pl.BlockSpec((blk, GLA_VDIM), lambda b, h, s: (b * ns + s, h)),
        out_shape=jax.ShapeDtypeStruct((T, GLA_HEADS * GLA_VDIM), BF16),
        scratch_shapes=[pltpu.VMEM((GLA_KDIM, GLA_VDIM), F32)],
        compiler_params=_cparams(("parallel", "parallel", "arbitrary"), 32),
        name="gla",
    )(mid, mid, mid, log_a, sgr, out_g, tri, ones)


def _resident_w_map(n_j):
    return lambda i, j: (0, jnp.where(i == 0, j, n_j - 1))


def _outproj_kernel(a_ref, b_ref, wa_ref, wb_ref, x_ref, o_ref, w_scr):
    j = pl.program_id(1)
    ka = a_ref.shape[1]

    @pl.when(pl.program_id(0) == 0)
    def _():
        w_scr[j, 0:ka, :] = wa_ref[...].astype(BF16)
        w_scr[j, ka:, :] = wb_ref[...].astype(BF16)

    acc = _dot(a_ref[...], w_scr[j, 0:ka, :]) + _dot(b_ref[...], w_scr[j, ka:, :])
    o_ref[...] = x_ref[...] + acc


def _outproj(a, b, w_out, x2, *, tm, tn):
    T, ka = a.shape
    kb = b.shape[1]
    assert ka == kb
    D = w_out.shape[1]
    n_j = D // tn
    return pl.pallas_call(
        _outproj_kernel,
        grid=(T // tm, n_j),
        in_specs=[
            pl.BlockSpec((tm, ka), lambda i, j: (i, 0)),
            pl.BlockSpec((tm, kb), lambda i, j: (i, 0)),
            pl.BlockSpec((ka, tn), _resident_w_map(n_j)),
            pl.BlockSpec((kb, tn), lambda i, j: (1, jnp.where(i == 0, j, n_j - 1))),
            pl.BlockSpec((tm, tn), lambda i, j: (i, j)),
        ],
        out_specs=pl.BlockSpec((tm, tn), lambda i, j: (i, j)),
        out_shape=jax.ShapeDtypeStruct((T, D), F32),
        scratch_shapes=[pltpu.VMEM((n_j, ka + kb, tn), BF16)],
        compiler_params=_cparams(("arbitrary", "arbitrary"), 48),
        name="outproj",
    )(a, b, w_out, w_out, x2)


def _normproj_kernel(x_ref, g_ref, w_ref, hg_ref, o_ref, n_scr, *w_scr, n_norm):
    j = pl.program_id(1)

    @pl.when(j == 0)
    def _():
        n_scr[...] = _rms(x_ref[...], g_ref[...]).astype(BF16)

    if w_scr:
        @pl.when(pl.program_id(0) == 0)
        def _():
            w_scr[0][j] = w_ref[...].astype(BF16)
        y = _dot(n_scr[...], w_scr[0][j])
    else:
        y = _dot(n_scr[...], w_ref[...].astype(BF16))

    @pl.when(j < n_norm)
    def _():
        o_ref[...] = _rms(y, hg_ref[...]).astype(BF16)

    @pl.when(j >= n_norm)
    def _():
        o_ref[...] = y.astype(BF16)


def _normproj(x2, g, w, head_g, *, tm, tn, n_norm, name):
    T, D = x2.shape
    N = w.shape[1]
    kern = functools.partial(_normproj_kernel, n_norm=n_norm)
    n_j = N // tn
    resident = T // tm > 1
    return pl.pallas_call(
        kern,
        grid=(T // tm, n_j),
        in_specs=[
            pl.BlockSpec((tm, D), lambda i, j: (i, 0)),
            pl.BlockSpec((1, D), lambda i, j: (0, 0)),
            pl.BlockSpec((D, tn), _resident_w_map(n_j) if resident else (lambda i, j: (0, j))),
            pl.BlockSpec((1, tn), lambda i, j: (0, 0)),
        ],
        out_specs=pl.BlockSpec((tm, tn), lambda i, j: (i, j)),
        out_shape=jax.ShapeDtypeStruct((T, N), BF16),
        scratch_shapes=[pltpu.VMEM((tm, D), BF16)] + ([pltpu.VMEM((n_j, D, tn), BF16)] if resident else []),
        compiler_params=_cparams(("arbitrary", "arbitrary"), 48),
        name=name,
    )(x2, g, w, head_g)


def _cross_kernel(q_ref, k_ref, v_ref, w_ref, h_ref, o_ref, att_scr, w_scr, *, hdim):
    j = pl.program_id(1)

    @pl.when(pl.program_id(0) == 0)
    def _():
        w_scr[j] = w_ref[...].astype(BF16)

    @pl.when(j == 0)
    def _():
        for hd in range(CROSS_HEADS):
            cols = slice(hd * hdim, (hd + 1) * hdim)
            s = lax.dot_general(q_ref[:, cols], k_ref[:, cols], NT_DIMS, preferred_element_type=F32)
            p = jnp.exp(s - jnp.max(s, axis=-1, keepdims=True))
            l = jnp.sum(p, axis=-1, keepdims=True)
            att_scr[:, cols] = (_dot(p.astype(BF16), v_ref[:, cols]) / l).astype(BF16)

    o_ref[...] = h_ref[...] + _dot(att_scr[...], w_scr[j])


def _cross(qc, kv, w_co, h1, *, S, n_mem, tm, tn):
    T, D = qc.shape
    per_b = S // tm
    kern = functools.partial(_cross_kernel, hdim=D // CROSS_HEADS)
    n_j = D // tn
    return pl.pallas_call(
        kern,
        grid=(T // tm, n_j),
        in_specs=[
            pl.BlockSpec((tm, D), lambda i, j: (i, 0)),
            pl.BlockSpec((n_mem, D), lambda i, j: (i // per_b, 0)),
            pl.BlockSpec((n_mem, D), lambda i, j: (i // per_b, 1)),
            pl.BlockSpec((D, tn), _resident_w_map(n_j)),
            pl.BlockSpec((tm, tn), lambda i, j: (i, j)),
        ],
        out_specs=pl.BlockSpec((tm, tn), lambda i, j: (i, j)),
        out_shape=jax.ShapeDtypeStruct((T, D), F32),
        scratch_shapes=[pltpu.VMEM((tm, D), BF16), pltpu.VMEM((n_j, D, tn), BF16)],
        compiler_params=_cparams(("arbitrary", "arbitrary"), 48),
        name="cross",
    )(qc, kv, kv, w_co, h1)


R_ROWS = SUBLANES + N_EXPERTS


def _pack_bf16_pairs(xb16):
    c = xb16.shape[1] // 2
    u = lax.bitcast_convert_type(xb16.astype(F32), jnp.uint32)
    return (u[:, :c] >> 16) | (u[:, c:] & jnp.uint32(0xFFFF0000))


def _store_row_tiles(ref, x):
    for g in range(SUBLANES):
        ref[:, g, :] = x[:, g * LANES:(g + 1) * LANES]


def _load_row_tiles(ref):
    return jnp.concatenate([ref[:, g, :] for g in range(SUBLANES)], axis=1)


def _unpack_bf16_pairs_f32(w):
    lo = lax.bitcast_convert_type(w << 16, F32)
    hi = lax.bitcast_convert_type(w & jnp.uint32(0xFFFF0000), F32)
    return lo, hi


def _unpack_bf16_pairs(w):
    lo, hi = _unpack_bf16_pairs_f32(w)
    return lo.astype(BF16), hi.astype(BF16)


def _router_kernel(h_ref, g_ref, wt_ref, b_ref, eid_ref, gate_ref, xn_ref):
    n = _rms(h_ref[...], g_ref[...])
    nh, nl = _split_bf16(n)
    _store_row_tiles(xn_ref, _pack_bf16_pairs(nh))
    wh, wl = _split_bf16(wt_ref[...])
    nt = functools.partial(lax.dot_general, dimension_numbers=NT_DIMS, preferred_element_type=F32)
    lg = nt(wh, nh) + nt(wh, nl) + nt(wl, nh) + b_ref[...]

    tm = lg.shape[1]
    row = lax.broadcasted_iota(I32, (SUBLANES, tm), 0)

    def first_argmax(v, vmax):
        return jnp.min(jnp.where(v == vmax, row, SUBLANES), axis=0, keepdims=True)

    gl = jnp.where(row < N_GROUPS, lg[0:SUBLANES], NEG_INF)
    gmax = jnp.max(gl, axis=0, keepdims=True)
    grp = first_argmax(gl, gmax)
    grp_w = 1.0 / jnp.sum(jnp.exp(gl - gmax), axis=0, keepdims=True)

    sel = jnp.zeros((SUBLANES, tm), F32)
    for gi in range(N_GROUPS):
        lo = SUBLANES + gi * EXPERTS_PER_GROUP
        sel = jnp.where(grp == gi, lg[lo:lo + EXPERTS_PER_GROUP], sel)
    e = jnp.exp(sel - jnp.max(sel, axis=0, keepdims=True))
    prob = e / jnp.sum(e, axis=0, keepdims=True)
    p1 = jnp.max(prob, axis=0, keepdims=True)
    i1 = first_argmax(prob, p1)
    rest = jnp.where(row == i1, -1.0, prob)
    p2 = jnp.max(rest, axis=0, keepdims=True)
    i2 = first_argmax(rest, p2)
    den = p1 + p2
    base = grp * EXPERTS_PER_GROUP
    eid_ref[...] = jnp.where(row == 0, base + i1, jnp.where(row == 1, base + i2, 0))
    gate_ref[...] = jnp.where(row == 0, grp_w * p1 / den, jnp.where(row == 1, grp_w * p2 / den, 0.0))


def _router(h2, g_ffn, w_rt, b_r, *, tm):
    T, D = h2.shape
    return pl.pallas_call(
        _router_kernel,
        grid=(T // tm,),
        in_specs=[
            pl.BlockSpec((tm, D), lambda i: (i, 0)),
            pl.BlockSpec((1, D), lambda i: (0, 0)),
            pl.BlockSpec((R_ROWS, D), lambda i: (0, 0)),
            pl.BlockSpec((R_ROWS, 1), lambda i: (0, 0)),
        ],
        out_specs=[
            pl.BlockSpec((SUBLANES, tm), lambda i: (0, i)),
            pl.BlockSpec((SUBLANES, tm), lambda i: (0, i)),
            pl.BlockSpec((tm, SUBLANES, LANES), lambda i: (i, 0, 0)),
        ],
        out_shape=[
            jax.ShapeDtypeStruct((SUBLANES, T), I32),
            jax.ShapeDtypeStruct((SUBLANES, T), F32),
            jax.ShapeDtypeStruct((T, SUBLANES, LANES), jnp.uint32),
        ],
        compiler_params=_cparams(("parallel",), 32),
        name="router",
    )(h2, g_ffn, w_rt, b_r)


def _rank_kernel(eid_ref, rank_ref, cnt_ref, carry):
    @pl.when(pl.program_id(0) == 0)
    def _():
        carry[...] = jnp.zeros(carry.shape, F32)

    tm = eid_ref.shape[1]
    e0 = eid_ref[0:1, :]
    e1 = eid_ref[1:2, :]
    erow = lax.broadcasted_iota(I32, (N_EXPERTS, tm), 0)
    hit = jnp.where((erow == e0) | (erow == e1), 1.0, 0.0)
    r = lax.broadcasted_iota(I32, (tm, tm), 0)
    c = lax.broadcasted_iota(I32, (tm, tm), 1)
    before = jnp.where(r < c, 1.0, 0.0).astype(BF16)
    pre = _dot(hit.astype(BF16), before) + carry[:, 0:1]
    rank0 = jnp.sum(jnp.where(erow == e0, pre, 0.0), axis=0, keepdims=True)
    rank1 = jnp.sum(jnp.where(erow == e1, pre, 0.0), axis=0, keepdims=True)
    row = lax.broadcasted_iota(I32, (SUBLANES, tm), 0)
    rank_ref[...] = jnp.where(row == 0, rank0, jnp.where(row == 1, rank1, 0.0)).astype(I32)
    total = carry[...] + jnp.sum(hit, axis=1, keepdims=True)
    carry[...] = total
    cnt_ref[...] = total.astype(I32)


def _rank(eid, *, tm):
    T = eid.shape[1]
    return pl.pallas_call(
        _rank_kernel,
        grid=(T // tm,),
        in_specs=[pl.BlockSpec((SUBLANES, tm), lambda i: (0, i))],
        out_specs=[
            pl.BlockSpec((SUBLANES, tm), lambda i: (0, i)),
            pl.BlockSpec((N_EXPERTS, LANES), lambda i: (0, 0)),
        ],
        out_shape=[
            jax.ShapeDtypeStruct((SUBLANES, T), I32),
            jax.ShapeDtypeStruct((N_EXPERTS, LANES), I32),
        ],
        scratch_shapes=[pltpu.VMEM((N_EXPERTS, LANES), F32)],
        compiler_params=_cparams(("arbitrary",), 32),
        name="rank",
    )(eid)


def _dispatch_kernel(dest_ref, pad0_ref, npad_ref, nb_ref, xn_ref, xb_hbm, zbuf, sem, psem,
                     *, n_tok, tb, n_blocks):
    i = pl.program_id(0)
    tm = xn_ref.shape[0]
    base = i * tm

    def row_body(r, carry):
        for k in range(TOP_K):
            pltpu.make_async_copy(xn_ref.at[r], xb_hbm.at[dest_ref[k * n_tok + base + r]], sem).start(priority=k)
        return carry
    lax.fori_loop(0, tm, row_body, 0, unroll=8)

    @pl.when(i == 0)
    def _():
        _dispatch_fill(pad0_ref, npad_ref, nb_ref, xb_hbm, zbuf, psem, tb=tb, n_blocks=n_blocks)

    for k in range(TOP_K):
        pltpu.make_async_copy(xn_ref, xb_hbm.at[pl.ds(0, tm)], sem).wait()


def _dispatch_fill(pad0_ref, npad_ref, nb_ref, xb_hbm, zbuf, psem, *, tb, n_blocks):
    zbuf[...] = jnp.zeros(zbuf.shape, zbuf.dtype)

    def pad_copy(e):
        n = npad_ref[e]
        return pltpu.make_async_copy(zbuf.at[pl.ds(0, n)], xb_hbm.at[pl.ds(pad0_ref[e], n)], psem.at[0])

    def tail_copy(blk):
        return pltpu.make_async_copy(zbuf, xb_hbm.at[pl.ds(pl.multiple_of(blk * tb, tb), tb)], psem.at[1])

    def for_each_pad(fn):
        def body(e, c):
            @pl.when(npad_ref[e] > 0)
            def _():
                fn(e)
            return c
        lax.fori_loop(0, N_EXPERTS, body, 0)

    def for_each_tail(fn):
        def body(b, c):
            fn(b)
            return c
        lax.fori_loop(nb_ref[0], n_blocks, body, 0)

    for_each_pad(lambda e: pad_copy(e).start())
    for_each_tail(lambda b: tail_copy(b).start())
    for_each_pad(lambda e: pad_copy(e).wait())
    for_each_tail(lambda b: tail_copy(b).wait())


def _dispatch(dest, pad0, npad, nb, xn, *, tb, n_blocks, tm):
    T = xn.shape[0]
    tile = xn.shape[1:]
    kern = functools.partial(_dispatch_kernel, n_tok=T, tb=tb, n_blocks=n_blocks)
    grid_spec = pltpu.PrefetchScalarGridSpec(
        num_scalar_prefetch=4,
        grid=(T // tm,),
        in_specs=[pl.BlockSpec((tm,) + tile, lambda i, *_: (i, 0, 0))],
        out_specs=pl.BlockSpec(memory_space=pl.ANY),
        scratch_shapes=[
            pltpu.VMEM((tb,) + tile, xn.dtype),
            pltpu.SemaphoreType.DMA(()),
            pltpu.SemaphoreType.DMA((2,)),
        ],
    )
    return pl.pallas_call(
        kern,
        grid_spec=grid_spec,
        out_shape=jax.ShapeDtypeStruct((n_blocks * tb,) + tile, xn.dtype),
        compiler_params=_cparams(("arbitrary",), 32),
        name="dispatch",
    )(dest, pad0, npad, nb, xn)


def _expert_kernel(be_ref, first_ref, slot_ref, nxt_ref, run1_ref, nb_ref, x_ref, wg_hbm, wu_hbm, wd_hbm, y_ref,
                   wg_f, wu_f, wd_f, wsem, wg_b, wu_b, wd_b):
    i = pl.program_id(0)
    nb = nb_ref[0]

    def weight_copies(e, slot):
        return (pltpu.make_async_copy(wg_hbm.at[e], wg_f.at[slot], wsem.at[3 * slot]),
                pltpu.make_async_copy(wu_hbm.at[e], wu_f.at[slot], wsem.at[3 * slot + 1]),
                pltpu.make_async_copy(wd_hbm.at[e], wd_f.at[slot], wsem.at[3 * slot + 2]))

    @pl.when(i == 0)
    def _():
        for cp in weight_copies(be_ref[0], 0):
            cp.start()

    @pl.when((i == 0) & (run1_ref[0] >= 0))
    def _():
        for cp in weight_copies(jnp.maximum(run1_ref[0], 0), 1):
            cp.start()

    first = (i < nb) & (first_ref[i] == 1)
    slot = slot_ref[i]

    @pl.when(first)
    def _():
        for cp in weight_copies(0, slot):
            cp.wait()
        wg_b[...] = wg_f[slot].astype(BF16)
        wu_b[...] = wu_f[slot].astype(BF16)
        wd_b[...] = wd_f[slot].astype(BF16)

    @pl.when(first & (nxt_ref[i] >= 0))
    def _():
        for cp in weight_copies(jnp.maximum(nxt_ref[i], 0), slot):
            cp.start()

    @pl.when(i < nb)
    def _():
        n_lo, n_hi = _unpack_bf16_pairs(_load_row_tiles(x_ref))
        half = n_lo.shape[1]
        a = _dot(n_lo, wg_b[0:half, :]) + _dot(n_hi, wg_b[half:, :])
        u = _dot(n_lo, wu_b[0:half, :]) + _dot(n_hi, wu_b[half:, :])
        hdn = (a / (1.0 + jnp.exp(-a))) * u
        y_ref[...] = _pack_bf16_pairs(_dot(hdn.astype(BF16), wd_b[...]).astype(BF16))

    @pl.when(i >= nb)
    def _():
        y_ref[...] = jnp.zeros(y_ref.shape, y_ref.dtype)


def _experts(xb, blk_e, blk_first, blk_slot, blk_next, run1, nb, w_gate, w_up, w_down, *, tb, n_blocks):
    D, De = w_gate.shape[1:]
    assert xb.shape[1:] == (SUBLANES, LANES) and D == 2 * SUBLANES * LANES

    def x_map(i, be, first, slot, nxt, r1, nbr):
        return (jnp.minimum(i, nbr[0] - 1), 0, 0)

    grid_spec = pltpu.PrefetchScalarGridSpec(
        num_scalar_prefetch=6,
        grid=(n_blocks,),
        in_specs=[
            pl.BlockSpec((tb, SUBLANES, LANES), x_map),
            pl.BlockSpec(memory_space=pl.ANY),
            pl.BlockSpec(memory_space=pl.ANY),
            pl.BlockSpec(memory_space=pl.ANY),
        ],
        out_specs=pl.BlockSpec((tb, D // 2), lambda i, *_: (i, 0)),
        scratch_shapes=[
            pltpu.VMEM((2, D, De), F32),
            pltpu.VMEM((2, D, De), F32),
            pltpu.VMEM((2, De, D), F32),
            pltpu.SemaphoreType.DMA((6,)),
            pltpu.VMEM((D, De), BF16),
            pltpu.VMEM((D, De), BF16),
            pltpu.VMEM((De, D), BF16),
        ],
    )
    return pl.pallas_call(
        _expert_kernel,
        grid_spec=grid_spec,
        out_shape=jax.ShapeDtypeStruct((n_blocks * tb, D // 2), jnp.uint32),
        compiler_params=_cparams(("arbitrary",), 52),
        name="experts",
    )(blk_e, blk_first, blk_slot, blk_next, run1, nb, xb, w_gate, w_up, w_down)


def _combine_kernel(dest_ref, y_hbm, h_ref, gate_ref, o_ref, ybuf, sem, *, tm, n_tok):
    i = pl.program_id(0)
    n = pl.num_programs(0)

    def row_copy(d, k, r, slot):
        return pltpu.make_async_copy(y_hbm.at[pl.ds(d, 1), :], ybuf.at[slot, k, pl.ds(r, 1), :], sem.at[slot])

    def start_gather(blk, slot):
        def body(r, carry):
            for k in range(TOP_K):
                row_copy(dest_ref[k * n_tok + blk * tm + r], k, r, slot).start(priority=k)
            return carry
        lax.fori_loop(0, tm, body, 0, unroll=8)

    def wait_gather(slot):
        for k in range(TOP_K):
            pltpu.make_async_copy(y_hbm.at[pl.ds(0, tm), :], ybuf.at[slot, k], sem.at[slot]).wait()

    @pl.when(i == 0)
    def _():
        start_gather(0, 0)

    @pl.when(i + 1 < n)
    def _():
        start_gather(i + 1, (i + 1) % 2)

    slot = i % 2
    wait_gather(slot)
    gt = gate_ref[...]
    half = h_ref.shape[1] // 2
    y0_lo, y0_hi = _unpack_bf16_pairs_f32(ybuf[slot, 0])
    y1_lo, y1_hi = _unpack_bf16_pairs_f32(ybuf[slot, 1])
    o_ref[:, 0:half] = h_ref[:, 0:half] + gt[:, 0:1] * y0_lo + gt[:, 1:2] * y1_lo
    o_ref[:, half:] = h_ref[:, half:] + gt[:, 0:1] * y0_hi + gt[:, 1:2] * y1_hi


def _combine(dest, yb, h2, gate, *, tm):
    T, D = h2.shape
    kern = functools.partial(_combine_kernel, tm=tm, n_tok=T)
    grid_spec = pltpu.PrefetchScalarGridSpec(
        num_scalar_prefetch=1,
        grid=(T // tm,),
        in_specs=[
            pl.BlockSpec(memory_space=pl.ANY),
            pl.BlockSpec((tm, D), lambda i, d: (i, 0)),
            pl.BlockSpec((tm, TOP_K), lambda i, d: (i, 0)),
        ],
        out_specs=pl.BlockSpec((tm, D), lambda i, d: (i, 0)),
        scratch_shapes=[
            pltpu.VMEM((2, TOP_K, tm, D // 2), jnp.uint32),
            pltpu.SemaphoreType.DMA((2,)),
        ],
    )
    return pl.pallas_call(
        kern,
        grid_spec=grid_spec,
        out_shape=jax.ShapeDtypeStruct((T, D), F32),
        compiler_params=_cparams(("arbitrary",), 40),
        name="combine",
    )(dest, yb, h2, gate)


EXPERT_ROWS = 256


def kernel(x, mem, positions, g_attn, w_in, q_norm_g, k_norm_g, lambda_q1, lambda_k1, lambda_q2, lambda_k2, diff_subln_g, gla_w_a2, gla_b_a, gla_out_g, w_out, g_cross, g_mem, w_cq, w_ckv, cq_norm_g, ck_norm_g, w_co, g_ffn, w_router_grp, b_router_grp, w_router_exp, b_router_exp, w_gate, w_up, w_down):
    B, S, D = x.shape
    T = B * S
    n_mem = mem.shape[1]
    l = 0
    x2 = x.reshape(T, D)

    half = DIFF_QKDIM // 2
    freq = ROPE_THETA ** (-jnp.arange(half, dtype=F32) / half)
    freq = jnp.tile(freq, LANES // half)[None, :]
    q_scale = math.log2(math.e) * DIFF_QKDIM ** -0.5
    qkg = jnp.stack([jnp.tile(q_norm_g[l], 2) * q_scale, jnp.tile(k_norm_g[l], 2)])
    score_bound = 1.01 * DIFF_QKDIM * q_scale * jnp.max(jnp.abs(q_norm_g[l])) * jnp.max(jnp.abs(k_norm_g[l]))
    lvec = jnp.stack([lambda_q1[l], lambda_k1[l], lambda_q2[l], lambda_k2[l]])

    qk, mid, log_a, sgr = _inproj(x2, g_attn[l][None], positions.reshape(T, 1), freq, qkg, w_in[l].T,
                                  gla_w_a2[l], gla_b_a[l][None], tm=1024)
    diffattn = functools.partial(_diffattn, lvec, qk, mid, diff_subln_g[l][None], B=B, S=S, tq=512)
    mix_d = lax.cond(score_bound <= SCORE_BOUND,
                     functools.partial(diffattn, bounded=True), functools.partial(diffattn, bounded=False))
    mix_g = _gla(mid, log_a, sgr, gla_out_g[l][None], B=B, S=S, blk=512)
    h1 = _outproj(mix_d, mix_g, w_out[l], x2, tm=1024, tn=512)

    hdim = D // CROSS_HEADS
    qc = _normproj(h1, g_cross[l][None], w_cq[l], cq_norm_g[l][None] * (hdim ** -0.5),
                   tm=1024, tn=hdim, n_norm=CROSS_HEADS, name="cq")
    kv = _normproj(mem.reshape(B * n_mem, D), g_mem[l][None], w_ckv[l], ck_norm_g[l][None],
                   tm=B * n_mem, tn=hdim, n_norm=CROSS_HEADS, name="ckv")
    h2 = _cross(qc, kv, w_co[l], h1, S=S, n_mem=n_mem, tm=1024, tn=512)

    w_rt = jnp.concatenate([w_router_grp[l].T, jnp.zeros((SUBLANES - N_GROUPS, D), F32), w_router_exp[l].T])
    b_r = jnp.concatenate([b_router_grp[l], jnp.zeros((SUBLANES - N_GROUPS,), F32), b_router_exp[l]])[:, None]
    eid, gate, xn = _router(h2, g_ffn[l][None], w_rt, b_r, tm=512)
    rank, cnt = _rank(eid, tm=512)

    tb = EXPERT_ROWS
    n_blocks = (T * TOP_K + N_EXPERTS * (tb - 1) + tb - 1) // tb
    counts = cnt[:, 0]
    pcounts = ((counts + tb - 1) // tb) * tb
    pends = jnp.cumsum(pcounts)
    pstarts = pends - pcounts
    nb = (pends[-1:] // tb).astype(I32)
    eids = jnp.arange(N_EXPERTS, dtype=I32)
    blk_start = jnp.arange(n_blocks, dtype=I32) * tb
    blk_e = jnp.minimum(jnp.sum(pends[None, :] <= blk_start[:, None], axis=1), N_EXPERTS - 1).astype(I32)
    blk_first = jnp.concatenate([jnp.ones((1,), I32), (blk_e[1:] != blk_e[:-1]).astype(I32)])
    used = jnp.where(counts > 0, eids, N_EXPERTS)
    next_used = jnp.concatenate([lax.cummin(used[::-1])[::-1][1:], jnp.full((1,), N_EXPERTS, I32)])
    next_used = jnp.where(next_used < N_EXPERTS, next_used, -1)

    def table_at(idx, table):
        return jnp.where(idx >= 0, jnp.sum(jnp.where(idx[:, None] == eids, table, 0), axis=1), -1).astype(I32)

    after_next = table_at(next_used, next_used)
    run_of = (jnp.cumsum((counts > 0).astype(I32)) - 1).astype(I32)
    blk_next = table_at(blk_e, after_next)
    blk_slot = table_at(blk_e, run_of % 2)
    run1 = table_at(blk_e[:1], next_used)
    pick = eid[:TOP_K]
    pstart_of = jnp.sum(jnp.where(pick[..., None] == eids, pstarts, 0), axis=-1)
    dest = (pstart_of + rank[:TOP_K]).astype(I32).reshape(-1)
    pad0 = (pstarts + counts).astype(I32)
    npad = (pcounts - counts).astype(I32)

    xb = _dispatch(dest, pad0, npad, nb, xn, tb=tb, n_blocks=n_blocks, tm=512)
    yb = _experts(xb, blk_e, blk_first, blk_slot, blk_next, run1, nb, w_gate[l], w_up[l], w_down[l],
                  tb=tb, n_blocks=n_blocks)
    out = _combine(dest, yb, h2, gate[:TOP_K].T, tm=512)
    return out.reshape(B, S, D)
```

```python
import functools
import math

import jax
import jax.numpy as jnp
from jax import lax
from jax.experimental import pallas as pl
from jax.experimental.pallas import tpu as pltpu

F32 = jnp.float32
BF16 = jnp.bfloat16
I32 = jnp.int32

LANES = 128
SUBLANES = 8

CHUNK = 64
ROPE_THETA = 10000.0
NORM_EPS = 1e-6
NEG_INF = -1e30
DIFF_HEADS = 8
DIFF_VDIM = 128
DIFF_QKDIM = 64
GLA_HEADS = 4
GLA_VDIM = 256
GLA_KDIM = 128
GLA_GATE_RANK = 16
GLA_TAU = 16.0
CROSS_HEADS = 4
N_GROUPS = 4
EXPERTS_PER_GROUP = 8
N_EXPERTS = N_GROUPS * EXPERTS_PER_GROUP
TOP_K = 2
LAM_INIT = 0.8 - 0.6 * math.exp(-0.3 * 0)

NT_DIMS = (((1,), (1,)), ((), ()))


def _cparams(semantics, vmem_mib):
    return pltpu.CompilerParams(dimension_semantics=semantics,
                                vmem_limit_bytes=vmem_mib * 1024 * 1024)


def _dot(a, b):
    return jnp.dot(a, b, preferred_element_type=F32)


def _dot_nt(a, b):
    return lax.dot_general(a, b, NT_DIMS, preferred_element_type=F32)


def _rms(x, g):
    ms = jnp.mean(x * x, axis=-1, keepdims=True)
    return x * lax.rsqrt(ms + NORM_EPS) * g


def _split_bf16(x):
    hi = x.astype(BF16)
    lo = (x - hi.astype(F32)).astype(BF16)
    return hi, lo


TN = 512
J_QK = 4
J_MID = 6
J_LR = J_QK + J_MID
J_GR = J_LR + 1
N_J = J_GR + 2


def _inproj_kernel(x_ref, g_ref, pos_ref, freq_ref, qkg_ref, w_ref, wgr_ref, wlr_ref, wa2_ref, ba_ref,
                   qk_ref, mid_ref, loga_ref, sgr_ref, n_scr, cos_scr, sin_scr, y_scr):
    j = pl.program_id(1)

    @pl.when(j == 0)
    def _():
        n_scr[...] = _rms(x_ref[...], g_ref[...]).astype(BF16)
        ang = pos_ref[...].astype(F32) * freq_ref[...]
        cos_scr[...] = jnp.cos(ang)
        sin_scr[...] = jnp.sin(ang)

    def qk_epilogue(jq):
        y_prev = y_scr.at[jq % 2]
        lane = lax.broadcasted_iota(I32, (1, LANES), 1)
        low_seg = lane < DIFF_QKDIM
        first_half = (lane % DIFF_QKDIM) < (DIFF_QKDIM // 2)
        gain = qkg_ref[jq // (J_QK // 2):jq // (J_QK // 2) + 1, :]
        cos = cos_scr[...]
        sin = sin_scr[...]
        for c in range(TN // LANES):
            yb = y_prev[:, c * LANES:(c + 1) * LANES]
            y2 = yb * yb
            s_lo = jnp.sum(jnp.where(low_seg, y2, 0.0), axis=-1, keepdims=True)
            s_hi = jnp.sum(jnp.where(low_seg, 0.0, y2), axis=-1, keepdims=True)
            ms = jnp.where(low_seg, s_lo, s_hi) * (1.0 / DIFF_QKDIM)
            yn = yb * lax.rsqrt(ms + NORM_EPS) * gain
            rot = jnp.where(first_half,
                            -pltpu.roll(yn, LANES - DIFF_QKDIM // 2, 1),
                            pltpu.roll(yn, DIFF_QKDIM // 2, 1))
            qk_ref[:, c * LANES:(c + 1) * LANES] = (yn * cos + rot * sin).astype(BF16)

    for jq in range(J_QK + 1):
        @pl.when(j == jq)
        def _():
            y = _dot_nt(n_scr[...], w_ref[...].astype(BF16))
            if jq > 0:
                qk_epilogue(jq - 1)
            if jq < J_QK:
                y_scr[jq % 2] = y
            else:
                mid_ref[...] = y.astype(BF16)

    @pl.when((j > J_QK) & (j < J_LR))
    def _():
        mid_ref[...] = _dot_nt(n_scr[...], w_ref[...].astype(BF16)).astype(BF16)

    @pl.when(j == J_LR)
    def _():
        lr = _dot_nt(n_scr[...], wlr_ref[...].astype(BF16))
        z = _dot(lr.astype(BF16), wa2_ref[...].astype(BF16)) + ba_ref[...]
        log_sig = jnp.minimum(z, 0.0) - jnp.log(1.0 + jnp.exp(-jnp.abs(z)))
        loga_ref[...] = log_sig * (1.0 / GLA_TAU)

    @pl.when(j >= J_GR)
    def _():
        y = _dot_nt(n_scr[...], wgr_ref[...].astype(BF16))
        sgr_ref[...] = (y / (1.0 + jnp.exp(-y))).astype(BF16)


def _inproj(x2, g_attn, pos2, freq, qkg, w_t, w_a2, b_a, *, tm):
    T, D = x2.shape
    n_mid = J_MID * TN
    n_gk = GLA_HEADS * GLA_KDIM
    lr0 = J_LR * TN
    gr0 = lr0 + GLA_GATE_RANK
    n_gr = w_t.shape[0] - gr0
    assert n_gr == 2 * TN and lr0 % GLA_GATE_RANK == 0
    return pl.pallas_call(
        _inproj_kernel,
        grid=(T // tm, N_J),
        in_specs=[
            pl.BlockSpec((tm, D), lambda i, j: (i, 0)),
            pl.BlockSpec((1, D), lambda i, j: (0, 0)),
            pl.BlockSpec((tm, 1), lambda i, j: (i, 0)),
            pl.BlockSpec((1, LANES), lambda i, j: (0, 0)),
            pl.BlockSpec((2, LANES), lambda i, j: (0, 0)),
            pl.BlockSpec((TN, D), lambda i, j: (jnp.minimum(j, J_LR - 1), 0)),
            pl.BlockSpec((pl.Element(TN), pl.Element(D)),
                         lambda i, j: (pl.multiple_of(gr0 + TN * jnp.clip(j - J_GR, 0, 1), SUBLANES), 0)),
            pl.BlockSpec((GLA_GATE_RANK, D), lambda i, j: (lr0 // GLA_GATE_RANK, 0)),
            pl.BlockSpec((GLA_GATE_RANK, n_gk), lambda i, j: (0, 0)),
            pl.BlockSpec((1, n_gk), lambda i, j: (0, 0)),
        ],
        out_specs=[
            pl.BlockSpec((tm, TN), lambda i, j: (i, jnp.clip(j - 1, 0, J_QK - 1))),
            pl.BlockSpec((tm, TN), lambda i, j: (i, jnp.clip(j - J_QK, 0, J_MID - 1))),
            pl.BlockSpec((tm, n_gk), lambda i, j: (i, 0)),
            pl.BlockSpec((tm, TN), lambda i, j: (i, jnp.clip(j - J_GR, 0, 1))),
        ],
        out_shape=[
            jax.ShapeDtypeStruct((T, J_QK * TN), BF16),
            jax.ShapeDtypeStruct((T, n_mid), BF16),
            jax.ShapeDtypeStruct((T, n_gk), F32),
            jax.ShapeDtypeStruct((T, n_gr), BF16),
        ],
        scratch_shapes=[
            pltpu.VMEM((tm, D), BF16),
            pltpu.VMEM((tm, LANES), F32),
            pltpu.VMEM((tm, LANES), F32),
            pltpu.VMEM((2, tm, TN), F32),
        ],
        compiler_params=_cparams(("parallel", "arbitrary"), 56),
        name="inproj",
    )(x2, g_attn, pos2, freq, qkg, w_t, w_t, w_t, w_a2, b_a)


SCORE_BOUND = 80.0


def _diffattn_kernel(ti_ref, tj_ref, lv_ref, q_ref, k_ref, v_ref, sg_ref, o_ref,
                     vext, diag_mask, acc1, acc2, m1, m2, *, tq, n_tiles, bounded):
    S = q_ref.shape[0]
    nq = S // tq
    half = tq // 2
    unroll = 7
    diag_unroll = 4 if nq % 4 == 0 else 2

    vext[:, 0:DIFF_VDIM] = v_ref[...]
    vext[:, DIFF_VDIM:] = jnp.ones((S, DIFF_VDIM), BF16)
    row_chunk = lax.broadcasted_iota(I32, (tq, tq), 0) // CHUNK
    col_chunk = lax.broadcasted_iota(I32, (tq, tq), 1) // CHUNK
    diag_mask[...] = jnp.where(col_chunk <= row_chunk, 1.0, 0.0).astype(BF16)

    lane = lax.broadcasted_iota(I32, (1, LANES), 1)

    def block(i):
        return pl.ds(pl.multiple_of(i * tq, tq), tq)

    def q_comps(i):
        q = q_ref[block(i), :]
        zero = jnp.zeros_like(q)
        return jnp.where(lane < DIFF_QKDIM, q, zero), jnp.where(lane < DIFF_QKDIM, zero, q)

    def diag_tile(i):
        q1, q2 = q_comps(i)
        for qc, acc, m in ((q1, acc1, m1), (q2, acc2, m2)):
            for lo, n_keys in ((0, half), (half, tq)):
                rows = pl.ds(pl.multiple_of(i * tq + lo, half), half)
                keys = pl.ds(pl.multiple_of(i * tq, tq), n_keys)
                mask = diag_mask[lo:lo + half, 0:n_keys]
                s = _dot_nt(qc[lo:lo + half], k_ref[keys, :])
                if bounded:
                    acc[rows, :] = _dot(jnp.exp2(s).astype(BF16) * mask, vext[keys, :])
                else:
                    s = jnp.where(mask > 0, s, NEG_INF)
                    m_new = jnp.max(s, axis=-1, keepdims=True)
                    acc[rows, :] = _dot(jnp.exp2(s - m_new).astype(BF16), vext[keys, :])
                    m[rows, :] = m_new

    def full_tile(i, j):
        q1, q2 = q_comps(i)
        rows = block(i)
        k = k_ref[block(j), :]
        v = vext[block(j), :]
        for qc, acc, m in ((q1, acc1, m1), (q2, acc2, m2)):
            s = _dot_nt(qc, k)
            if bounded:
                acc[rows, :] += _dot(jnp.exp2(s).astype(BF16), v)
            else:
                m_old = m[rows, :]
                m_new = jnp.maximum(m_old, jnp.max(s, axis=-1, keepdims=True))
                p = jnp.exp2(s - m_new)
                acc[rows, :] = jnp.exp2(m_old - m_new) * acc[rows, :] + _dot(p.astype(BF16), v)
                m[rows, :] = m_new

    def diag_body(t, carry):
        for u in range(diag_unroll):
            diag_tile(diag_unroll * t + u)
        return carry
    lax.fori_loop(0, nq // diag_unroll, diag_body, 0)

    def full_body(t, carry):
        for u in range(unroll):
            full_tile(ti_ref[unroll * t + u], tj_ref[unroll * t + u])
        return carry
    lax.fori_loop(0, n_tiles // unroll, full_body, 0)
    for t in range(n_tiles - n_tiles % unroll, n_tiles):
        full_tile(ti_ref[t], tj_ref[t])

    lv = lv_ref[...]
    lam = (jnp.exp(jnp.sum(lv[0:1] * lv[1:2], axis=-1, keepdims=True))
           - jnp.exp(jnp.sum(lv[2:3] * lv[3:4], axis=-1, keepdims=True)) + LAM_INIT)

    def out_body(i, carry):
        rows = block(i)
        a1 = acc1[rows, :]
        a2 = acc2[rows, :]
        o = a1[:, :DIFF_VDIM] / a1[:, DIFF_VDIM:] - lam * (a2[:, :DIFF_VDIM] / a2[:, DIFF_VDIM:])
        o_ref[rows, :] = (_rms(o, sg_ref[...]) * (1.0 - LAM_INIT)).astype(BF16)
        return carry
    lax.fori_loop(0, nq, out_body, 0)


def _diffattn(lvec, qk, mid, subln_g, *, B, S, tq, bounded):
    T = B * S
    nq = S // tq
    assert nq % 2 == 0
    tiles = [(i, j) for i in range(nq) for j in range(i)]
    ti = jnp.asarray([t[0] for t in tiles], I32)
    tj = jnp.asarray([t[1] for t in tiles], I32)
    kern = functools.partial(_diffattn_kernel, tq=tq, n_tiles=len(tiles), bounded=bounded)
    m_rows = SUBLANES if bounded else S
    grid_spec = pltpu.PrefetchScalarGridSpec(
        num_scalar_prefetch=2,
        grid=(B, DIFF_HEADS),
        in_specs=[
            pl.BlockSpec((4, DIFF_QKDIM), lambda b, h, *_: (0, 0)),
            pl.BlockSpec((S, LANES), lambda b, h, *_: (b, h)),
            pl.BlockSpec((S, LANES), lambda b, h, *_: (b, DIFF_HEADS + h)),
            pl.BlockSpec((S, LANES), lambda b, h, *_: (b, h)),
            pl.BlockSpec((1, DIFF_VDIM), lambda b, h, *_: (0, 0)),
        ],
        out_specs=pl.BlockSpec((S, DIFF_VDIM), lambda b, h, *_: (b, h)),
        scratch_shapes=[
            pltpu.VMEM((S, 2 * DIFF_VDIM), BF16),
            pltpu.VMEM((tq, tq), BF16),
            pltpu.VMEM((S, 2 * DIFF_VDIM), F32),
            pltpu.VMEM((S, 2 * DIFF_VDIM), F32),
            pltpu.VMEM((m_rows, 1), F32),
            pltpu.VMEM((m_rows, 1), F32),
        ],
    )
    return pl.pallas_call(
        kern,
        grid_spec=grid_spec,
        out_shape=jax.ShapeDtypeStruct((T, DIFF_HEADS * DIFF_VDIM), BF16),
        compiler_params=_cparams(("parallel", "parallel"), 40),
        name="diffattn_bounded" if bounded else "diffattn_online",
    )(ti, tj, lvec, qk, qk, mid, subln_g)


def _gla_kernel(q_ref, k_ref, v_ref, la_ref, sgr_ref, g_ref, tri_ref, ones_ref, o_ref, state, *, blk):
    @pl.when(pl.program_id(2) == 0)
    def _():
        state[...] = jnp.zeros(state.shape, F32)

    la_t = la_ref[...].T
    k_t = k_ref[...].astype(F32).T
    hi, lo = _split_bf16(la_t)
    tri = tri_ref[...]
    ones = ones_ref[...]
    cum_t = _dot(hi, tri) + _dot(lo, tri)
    tot_t = _dot(hi, ones) + _dot(lo, ones)
    kd_t = k_t * jnp.exp(tot_t - cum_t)

    n_chunks = blk // CHUNK
    lane = lax.broadcasted_iota(I32, (1, LANES), 1)
    d_states = []
    for ck in range(n_chunks):
        pair = slice((ck // 2) * LANES, (ck // 2 + 1) * LANES)
        in_chunk = (lane // CHUNK) == (ck % 2)
        kd = jnp.where(in_chunk, kd_t[:, pair], 0.0).astype(BF16)
        d_states.append(_dot(kd, v_ref[pair, :]))

    st = state[...]
    states = []
    for ck in range(n_chunks):
        decay = jnp.exp(tot_t[:, ck * CHUNK:ck * CHUNK + 1])
        st = decay * st + d_states[ck]
        states.append(st.astype(BF16))
    state[...] = st

    o = jnp.concatenate([_dot(q_ref[ck * CHUNK:(ck + 1) * CHUNK, :], states[ck]) for ck in range(n_chunks)],
                        axis=0) * (GLA_KDIM ** -0.5)
    o_ref[...] = (_rms(o, g_ref[...]) * sgr_ref[...].astype(F32)).astype(BF16)


def _gla(mid, log_a, sgr, out_g, *, B, S, blk):
    T = B * S
    ns = S // blk
    kern = functools.partial(_gla_kernel, blk=blk)
    q_col0 = (DIFF_HEADS * DIFF_VDIM) // GLA_KDIM
    k_col0 = q_col0 + GLA_HEADS
    v_col0 = (DIFF_HEADS * DIFF_VDIM + 2 * GLA_HEADS * GLA_KDIM) // GLA_VDIM
    r = jnp.arange(blk, dtype=I32)[:, None]
    c = jnp.arange(blk, dtype=I32)[None, :]
    same = (r // CHUNK) == (c // CHUNK)
    tri = (same & (r <= c)).astype(BF16)
    ones = same.astype(BF16)
    return pl.pallas_call(
        kern,
        grid=(B, GLA_HEADS, ns),
        in_specs=[
            pl.BlockSpec((blk, GLA_KDIM), lambda b, h, s: (b * ns + s, q_col0 + h)),
            pl.BlockSpec((blk, GLA_KDIM), lambda b, h, s: (b * ns + s, k_col0 + h)),
            pl.BlockSpec((blk, GLA_VDIM), lambda b, h, s: (b * ns + s, v_col0 + h)),
            pl.BlockSpec((blk, GLA_KDIM), lambda b, h, s: (b * ns + s, h)),
            pl.BlockSpec((blk, GLA_VDIM), lambda b, h, s: (b * ns + s, h)),
            pl.BlockSpec((1, GLA_VDIM), lambda b, h, s: (0, 0)),
            pl.BlockSpec((blk, blk), lambda b, h, s: (0, 0)),
            pl.BlockSpec((blk, blk), lambda b, h, s: (0, 0)),
        ],
        out_specs=pl.BlockSpec((blk, GLA_VDIM), lambda b, h, s: (b * ns + s, h)),
        out_shape=jax.ShapeDtypeStruct((T, GLA_HEADS * GLA_VDIM), BF16),
        scratch_shapes=[pltpu.VMEM((GLA_KDIM, GLA_VDIM), F32)],
        compiler_params=_cparams(("parallel", "parallel", "arbitrary"), 32),
        name="gla",
    )(mid, mid, mid, log_a, sgr, out_g, tri, ones)


def _resident_w_map(n_j):
    return lambda i, j: (0, jnp.where(i == 0, j, n_j - 1))


def _outproj_kernel(a_ref, b_ref, wa_ref, wb_ref, x_ref, o_ref, w_scr):
    j = pl.program_id(1)
    ka = a_ref.shape[1]

    @pl.when(pl.program_id(0) == 0)
    def _():
        w_scr[j, 0:ka, :] = wa_ref[...].astype(BF16)
        w_scr[j, ka:, :] = wb_ref[...].astype(BF16)

    acc = _dot(a_ref[...], w_scr[j, 0:ka, :]) + _dot(b_ref[...], w_scr[j, ka:, :])
    o_ref[...] = x_ref[...] + acc


def _outproj(a, b, w_out, x2, *, tm, tn):
    T, ka = a.shape
    kb = b.shape[1]
    assert ka == kb
    D = w_out.shape[1]
    n_j = D // tn
    return pl.pallas_call(
        _outproj_kernel,
        grid=(T // tm, n_j),
        in_specs=[
            pl.BlockSpec((tm, ka), lambda i, j: (i, 0)),
            pl.BlockSpec((tm, kb), lambda i, j: (i, 0)),
            pl.BlockSpec((ka, tn), _resident_w_map(n_j)),
            pl.BlockSpec((kb, tn), lambda i, j: (1, jnp.where(i == 0, j, n_j - 1))),
            pl.BlockSpec((tm, tn), lambda i, j: (i, j)),
        ],
        out_specs=pl.BlockSpec((tm, tn), lambda i, j: (i, j)),
        out_shape=jax.ShapeDtypeStruct((T, D), F32),
        scratch_shapes=[pltpu.VMEM((n_j, ka + kb, tn), BF16)],
        compiler_params=_cparams(("arbitrary", "arbitrary"), 48),
        name="outproj",
    )(a, b, w_out, w_out, x2)


def _normproj_kernel(x_ref, g_ref, w_ref, hg_ref, o_ref, n_scr, *w_scr, n_norm):
    j = pl.program_id(1)

    @pl.when(j == 0)
    def _():
        n_scr[...] = _rms(x_ref[...], g_ref[...]).astype(BF16)

    if w_scr:
        @pl.when(pl.program_id(0) == 0)
        def _():
            w_scr[0][j] = w_ref[...].astype(BF16)
        y = _dot(n_scr[...], w_scr[0][j])
    else:
        y = _dot(n_scr[...], w_ref[...].astype(BF16))

    @pl.when(j < n_norm)
    def _():
        o_ref[...] = _rms(y, hg_ref[...]).astype(BF16)

    @pl.when(j >= n_norm)
    def _():
        o_ref[...] = y.astype(BF16)


def _normproj(x2, g, w, head_g, *, tm, tn, n_norm, name):
    T, D = x2.shape
    N = w.shape[1]
    kern = functools.partial(_normproj_kernel, n_norm=n_norm)
    n_j = N // tn
    resident = T // tm > 1
    return pl.pallas_call(
        kern,
        grid=(T // tm, n_j),
        in_specs=[
            pl.BlockSpec((tm, D), lambda i, j: (i, 0)),
            pl.BlockSpec((1, D), lambda i, j: (0, 0)),
            pl.BlockSpec((D, tn), _resident_w_map(n_j) if resident else (lambda i, j: (0, j))),
            pl.BlockSpec((1, tn), lambda i, j: (0, 0)),
        ],
        out_specs=pl.BlockSpec((tm, tn), lambda i, j: (i, j)),
        out_shape=jax.ShapeDtypeStruct((T, N), BF16),
        scratch_shapes=[pltpu.VMEM((tm, D), BF16)] + ([pltpu.VMEM((n_j, D, tn), BF16)] if resident else []),
        compiler_params=_cparams(("arbitrary", "arbitrary"), 48),
        name=name,
    )(x2, g, w, head_g)


def _cross_kernel(q_ref, k_ref, v_ref, w_ref, h_ref, o_ref, att_scr, w_scr, *, hdim):
    j = pl.program_id(1)

    @pl.when(pl.program_id(0) == 0)
    def _():
        w_scr[j] = w_ref[...].astype(BF16)

    @pl.when(j == 0)
    def _():
        for hd in range(CROSS_HEADS):
            cols = slice(hd * hdim, (hd + 1) * hdim)
            s = lax.dot_general(q_ref[:, cols], k_ref[:, cols], NT_DIMS, preferred_element_type=F32)
            p = jnp.exp(s - jnp.max(s, axis=-1, keepdims=True))
            l = jnp.sum(p, axis=-1, keepdims=True)
            att_scr[:, cols] = (_dot(p.astype(BF16), v_ref[:, cols]) / l).astype(BF16)

    o_ref[...] = h_ref[...] + _dot(att_scr[...], w_scr[j])


def _cross(qc, kv, w_co, h1, *, S, n_mem, tm, tn):
    T, D = qc.shape
    per_b = S // tm
    kern = functools.partial(_cross_kernel, hdim=D // CROSS_HEADS)
    n_j = D // tn
    return pl.pallas_call(
        kern,
        grid=(T // tm, n_j),
        in_specs=[
            pl.BlockSpec((tm, D), lambda i, j: (i, 0)),
            pl.BlockSpec((n_mem, D), lambda i, j: (i // per_b, 0)),
            pl.BlockSpec((n_mem, D), lambda i, j: (i // per_b, 1)),
            pl.BlockSpec((D, tn), _resident_w_map(n_j)),
            pl.BlockSpec((tm, tn), lambda i, j: (i, j)),
        ],
        out_specs=pl.BlockSpec((tm, tn), lambda i, j: (i, j)),
        out_shape=jax.ShapeDtypeStruct((T, D), F32),
        scratch_shapes=[pltpu.VMEM((tm, D), BF16), pltpu.VMEM((n_j, D, tn), BF16)],
        compiler_params=_cparams(("arbitrary", "arbitrary"), 48),
        name="cross",
    )(qc, kv, kv, w_co, h1)


R_ROWS = SUBLANES + N_EXPERTS


def _pack_bf16_pairs(xb16):
    c = xb16.shape[1] // 2
    u = lax.bitcast_convert_type(xb16.astype(F32), jnp.uint32)
    return (u[:, :c] >> 16) | (u[:, c:] & jnp.uint32(0xFFFF0000))


def _store_row_tiles(ref, x):
    for g in range(SUBLANES):
        ref[:, g, :] = x[:, g * LANES:(g + 1) * LANES]


def _load_row_tiles(ref):
    return jnp.concatenate([ref[:, g, :] for g in range(SUBLANES)], axis=1)


def _unpack_bf16_pairs_f32(w):
    lo = lax.bitcast_convert_type(w << 16, F32)
    hi = lax.bitcast_convert_type(w & jnp.uint32(0xFFFF0000), F32)
    return lo, hi


def _unpack_bf16_pairs(w):
    lo, hi = _unpack_bf16_pairs_f32(w)
    return lo.astype(BF16), hi.astype(BF16)


def _router_kernel(h_ref, g_ref, wt_ref, b_ref, eid_ref, gate_ref, xn_ref):
    n = _rms(h_ref[...], g_ref[...])
    nh, nl = _split_bf16(n)
    _store_row_tiles(xn_ref, _pack_bf16_pairs(nh))
    wh, wl = _split_bf16(wt_ref[...])
    nt = functools.partial(lax.dot_general, dimension_numbers=NT_DIMS, preferred_element_type=F32)
    lg = nt(wh, nh) + nt(wh, nl) + nt(wl, nh) + b_ref[...]

    tm = lg.shape[1]
    row = lax.broadcasted_iota(I32, (SUBLANES, tm), 0)

    def first_argmax(v, vmax):
        return jnp.min(jnp.where(v == vmax, row, SUBLANES), axis=0, keepdims=True)

    gl = jnp.where(row < N_GROUPS, lg[0:SUBLANES], NEG_INF)
    gmax = jnp.max(gl, axis=0, keepdims=True)
    grp = first_argmax(gl, gmax)
    grp_w = 1.0 / jnp.sum(jnp.exp(gl - gmax), axis=0, keepdims=True)

    sel = jnp.zeros((SUBLANES, tm), F32)
    for gi in range(N_GROUPS):
        lo = SUBLANES + gi * EXPERTS_PER_GROUP
        sel = jnp.where(grp == gi, lg[lo:lo + EXPERTS_PER_GROUP], sel)
    e = jnp.exp(sel - jnp.max(sel, axis=0, keepdims=True))
    prob = e / jnp.sum(e, axis=0, keepdims=True)
    p1 = jnp.max(prob, axis=0, keepdims=True)
    i1 = first_argmax(prob, p1)
    rest = jnp.where(row == i1, -1.0, prob)
    p2 = jnp.max(rest, axis=0, keepdims=True)
    i2 = first_argmax(rest, p2)
    den = p1 + p2
    base = grp * EXPERTS_PER_GROUP
    eid_ref[...] = jnp.where(row == 0, base + i1, jnp.where(row == 1, base + i2, 0))
    gate_ref[...] = jnp.where(row == 0, grp_w * p1 / den, jnp.where(row == 1, grp_w * p2 / den, 0.0))


def _router(h2, g_ffn, w_rt, b_r, *, tm):
    T, D = h2.shape
    return pl.pallas_call(
        _router_kernel,
        grid=(T // tm,),
        in_specs=[
            pl.BlockSpec((tm, D), lambda i: (i, 0)),
            pl.BlockSpec((1, D), lambda i: (0, 0)),
            pl.BlockSpec((R_ROWS, D), lambda i: (0, 0)),
            pl.BlockSpec((R_ROWS, 1), lambda i: (0, 0)),
        ],
        out_specs=[
            pl.BlockSpec((SUBLANES, tm), lambda i: (0, i)),
            pl.BlockSpec((SUBLANES, tm), lambda i: (0, i)),
            pl.BlockSpec((tm, SUBLANES, LANES), lambda i: (i, 0, 0)),
        ],
        out_shape=[
            jax.ShapeDtypeStruct((SUBLANES, T), I32),
            jax.ShapeDtypeStruct((SUBLANES, T), F32),
            jax.ShapeDtypeStruct((T, SUBLANES, LANES), jnp.uint32),
        ],
        compiler_params=_cparams(("parallel",), 32),
        name="router",
    )(h2, g_ffn, w_rt, b_r)


def _rank_kernel(eid_ref, rank_ref, cnt_ref, carry):
    @pl.when(pl.program_id(0) == 0)
    def _():
        carry[...] = jnp.zeros(carry.shape, F32)

    tm = eid_ref.shape[1]
    e0 = eid_ref[0:1, :]
    e1 = eid_ref[1:2, :]
    erow = lax.broadcasted_iota(I32, (N_EXPERTS, tm), 0)
    hit = jnp.where((erow == e0) | (erow == e1), 1.0, 0.0)
    r = lax.broadcasted_iota(I32, (tm, tm), 0)
    c = lax.broadcasted_iota(I32, (tm, tm), 1)
    before = jnp.where(r < c, 1.0, 0.0).astype(BF16)
    pre = _dot(hit.astype(BF16), before) + carry[:, 0:1]
    rank0 = jnp.sum(jnp.where(erow == e0, pre, 0.0), axis=0, keepdims=True)
    rank1 = jnp.sum(jnp.where(erow == e1, pre, 0.0), axis=0, keepdims=True)
    row = lax.broadcasted_iota(I32, (SUBLANES, tm), 0)
    rank_ref[...] = jnp.where(row == 0, rank0, jnp.where(row == 1, rank1, 0.0)).astype(I32)
    total = carry[...] + jnp.sum(hit, axis=1, keepdims=True)
    carry[...] = total
    cnt_ref[...] = total.astype(I32)


def _rank(eid, *, tm):
    T = eid.shape[1]
    return pl.pallas_call(
        _rank_kernel,
        grid=(T // tm,),
        in_specs=[pl.BlockSpec((SUBLANES, tm), lambda i: (0, i))],
        out_specs=[
            pl.BlockSpec((SUBLANES, tm), lambda i: (0, i)),
            pl.BlockSpec((N_EXPERTS, LANES), lambda i: (0, 0)),
        ],
        out_shape=[
            jax.ShapeDtypeStruct((SUBLANES, T), I32),
            jax.ShapeDtypeStruct((N_EXPERTS, LANES), I32),
        ],
        scratch_shapes=[pltpu.VMEM((N_EXPERTS, LANES), F32)],
        compiler_params=_cparams(("arbitrary",), 32),
        name="rank",
    )(eid)


def _dispatch_kernel(dest_ref, pad0_ref, npad_ref, nb_ref, xn_ref, xb_hbm, zbuf, sem, psem,
                     *, n_tok, tb, n_blocks):
    i = pl.program_id(0)
    tm = xn_ref.shape[0]
    base = i * tm

    def row_body(r, carry):
        for k in range(TOP_K):
            pltpu.make_async_copy(xn_ref.at[r], xb_hbm.at[dest_ref[k * n_tok + base + r]], sem).start(priority=k)
        return carry
    lax.fori_loop(0, tm, row_body, 0, unroll=8)

    @pl.when(i == 0)
    def _():
        _dispatch_fill(pad0_ref, npad_ref, nb_ref, xb_hbm, zbuf, psem, tb=tb, n_blocks=n_blocks)

    for k in range(TOP_K):
        pltpu.make_async_copy(xn_ref, xb_hbm.at[pl.ds(0, tm)], sem).wait()


def _dispatch_fill(pad0_ref, npad_ref, nb_ref, xb_hbm, zbuf, psem, *, tb, n_blocks):
    zbuf[...] = jnp.zeros(zbuf.shape, zbuf.dtype)

    def pad_copy(e):
        n = npad_ref[e]
        return pltpu.make_async_copy(zbuf.at[pl.ds(0, n)], xb_hbm.at[pl.ds(pad0_ref[e], n)], psem.at[0])

    def tail_copy(blk):
        return pltpu.make_async_copy(zbuf, xb_hbm.at[pl.ds(pl.multiple_of(blk * tb, tb), tb)], psem.at[1])

    def for_each_pad(fn):
        def body(e, c):
            @pl.when(npad_ref[e] > 0)
            def _():
                fn(e)
            return c
        lax.fori_loop(0, N_EXPERTS, body, 0)

    def for_each_tail(fn):
        def body(b, c):
            fn(b)
            return c
        lax.fori_loop(nb_ref[0], n_blocks, body, 0)

    for_each_pad(lambda e: pad_copy(e).start())
    for_each_tail(lambda b: tail_copy(b).start())
    for_each_pad(lambda e: pad_copy(e).wait())
    for_each_tail(lambda b: tail_copy(b).wait())


def _dispatch(dest, pad0, npad, nb, xn, *, tb, n_blocks, tm):
    T = xn.shape[0]
    tile = xn.shape[1:]
    kern = functools.partial(_dispatch_kernel, n_tok=T, tb=tb, n_blocks=n_blocks)
    grid_spec = pltpu.PrefetchScalarGridSpec(
        num_scalar_prefetch=4,
        grid=(T // tm,),
        in_specs=[pl.BlockSpec((tm,) + tile, lambda i, *_: (i, 0, 0))],
        out_specs=pl.BlockSpec(memory_space=pl.ANY),
        scratch_shapes=[
            pltpu.VMEM((tb,) + tile, xn.dtype),
            pltpu.SemaphoreType.DMA(()),
            pltpu.SemaphoreType.DMA((2,)),
        ],
    )
    return pl.pallas_call(
        kern,
        grid_spec=grid_spec,
        out_shape=jax.ShapeDtypeStruct((n_blocks * tb,) + tile, xn.dtype),
        compiler_params=_cparams(("arbitrary",), 32),
        name="dispatch",
    )(dest, pad0, npad, nb, xn)


def _expert_kernel(be_ref, first_ref, slot_ref, nxt_ref, run1_ref, nb_ref,
                   xb_hbm, wg_hbm, wu_hbm, wd_hbm, y_ref,
                   wg_f, wu_f, wd_f, wsem, wg_b, wu_b, wd_b, raw0, raw1, xsem, xlo0, xhi0, xlo1, xhi1):
    xlo = (xlo0, xlo1)
    xhi = (xhi0, xhi1)
    i = pl.program_id(0)
    nb = nb_ref[0]

    def weight_copies(e, slot):
        return (pltpu.make_async_copy(wg_hbm.at[e], wg_f.at[slot], wsem.at[3 * slot]),
                pltpu.make_async_copy(wu_hbm.at[e], wu_f.at[slot], wsem.at[3 * slot + 1]),
                pltpu.make_async_copy(wd_hbm.at[e], wd_f.at[slot], wsem.at[3 * slot + 2]))

    @pl.when(i == 0)
    def _():
        for cp in weight_copies(be_ref[0], 0):
            cp.start()

    @pl.when((i == 0) & (run1_ref[0] >= 0))
    def _():
        for cp in weight_copies(jnp.maximum(run1_ref[0], 0), 1):
            cp.start()

    first = (i < nb) & (first_ref[i] == 1)
    slot = slot_ref[i]

    @pl.when(first)
    def _():
        for cp in weight_copies(0, slot):
            cp.wait()
        wg_b[...] = wg_f[slot].astype(BF16)
        wu_b[...] = wu_f[slot].astype(BF16)
        wd_b[...] = wd_f[slot].astype(BF16)

    @pl.when(first & (nxt_ref[i] >= 0))
    def _():
        for cp in weight_copies(jnp.maximum(nxt_ref[i], 0), slot):
            cp.start()

    raw = (raw0, raw1)
    tb = raw0.shape[0]

    def block_copies(b, p):
        row0 = pl.multiple_of(jnp.minimum(b, nb - 1) * tb, tb)
        return [pltpu.make_async_copy(xb_hbm.at[pl.ds(row0, tb), g, :],
                                      raw[p].at[:, pl.ds(g * LANES, LANES)], xsem.at[p])
                for g in range(SUBLANES)]

    def wait_block(p):
        pltpu.make_async_copy(raw[1 - p], raw[p], xsem.at[p]).wait()

    def unpack_into(p):
        xlo[p][...], xhi[p][...] = _unpack_bf16_pairs(raw[p][...])

    @pl.when(i == 0)
    def _():
        for cp in block_copies(0, 0) + block_copies(1, 1):
            cp.start()
        wait_block(0)
        unpack_into(0)

    for p in range(2):
        @pl.when((i + 1 < nb) & (i % 2 == p))
        def _():
            wait_block(1 - p)

        @pl.when((i + 2 < nb) & (i % 2 == p))
        def _():
            for cp in block_copies(i + 2, p):
                cp.start()

        @pl.when((i < nb) & (i % 2 == p))
        def _():
            unpack_into(1 - p)
            n_lo = xlo[p][...]
            n_hi = xhi[p][...]
            half = n_lo.shape[1]
            a = _dot(n_lo, wg_b[0:half, :]) + _dot(n_hi, wg_b[half:, :])
            u = _dot(n_lo, wu_b[0:half, :]) + _dot(n_hi, wu_b[half:, :])
            hdn = (a / (1.0 + jnp.exp(-a))) * u
            y_ref[...] = _pack_bf16_pairs(_dot(hdn.astype(BF16), wd_b[...]).astype(BF16))

    @pl.when(i >= nb)
    def _():
        y_ref[...] = jnp.zeros(y_ref.shape, y_ref.dtype)


def _experts(xb, blk_e, blk_first, blk_slot, blk_next, run1, nb, w_gate, w_up, w_down, *, tb, n_blocks):
    D, De = w_gate.shape[1:]
    assert xb.shape[1:] == (SUBLANES, LANES) and D == 2 * SUBLANES * LANES

    assert xb.shape[0] // tb >= 2

    grid_spec = pltpu.PrefetchScalarGridSpec(
        num_scalar_prefetch=6,
        grid=(n_blocks,),
        in_specs=[
            pl.BlockSpec(memory_space=pl.ANY),
            pl.BlockSpec(memory_space=pl.ANY),
            pl.BlockSpec(memory_space=pl.ANY),
            pl.BlockSpec(memory_space=pl.ANY),
        ],
        out_specs=pl.BlockSpec((tb, D // 2), lambda i, *_: (i, 0)),
        scratch_shapes=[
            pltpu.VMEM((2, D, De), F32),
            pltpu.VMEM((2, D, De), F32),
            pltpu.VMEM((2, De, D), F32),
            pltpu.SemaphoreType.DMA((6,)),
            pltpu.VMEM((D, De), BF16),
            pltpu.VMEM((D, De), BF16),
            pltpu.VMEM((De, D), BF16),
            pltpu.VMEM((tb, D // 2), jnp.uint32),
            pltpu.VMEM((tb, D // 2), jnp.uint32),
            pltpu.SemaphoreType.DMA((2,)),
        ] + [pltpu.VMEM((tb, D // 2), BF16)] * 4,
    )
    return pl.pallas_call(
        _expert_kernel,
        grid_spec=grid_spec,
        out_shape=jax.ShapeDtypeStruct((n_blocks * tb, D // 2), jnp.uint32),
        compiler_params=_cparams(("arbitrary",), 56),
        name="experts",
    )(blk_e, blk_first, blk_slot, blk_next, run1, nb, xb, w_gate, w_up, w_down)


def _combine_kernel(dest_ref, y_hbm, h_ref, gate_ref, o_ref, ybuf, sem, *, tm, n_tok):
    i = pl.program_id(0)
    n = pl.num_programs(0)

    def row_copy(d, k, r, slot):
        return pltpu.make_async_copy(y_hbm.at[pl.ds(d, 1), :], ybuf.at[slot, k, pl.ds(r, 1), :], sem.at[slot])

    def start_gather(blk, slot):
        def body(r, carry):
            for k in range(TOP_K):
                row_copy(dest_ref[k * n_tok + blk * tm + r], k, r, slot).start(priority=k)
            return carry
        lax.fori_loop(0, tm, body, 0, unroll=8)

    def wait_gather(slot):
        for k in range(TOP_K):
            pltpu.make_async_copy(y_hbm.at[pl.ds(0, tm), :], ybuf.at[slot, k], sem.at[slot]).wait()

    @pl.when(i == 0)
    def _():
        start_gather(0, 0)

    @pl.when(i + 1 < n)
    def _():
        start_gather(i + 1, (i + 1) % 2)

    slot = i % 2
    wait_gather(slot)
    gt = gate_ref[...]
    half = h_ref.shape[1] // 2
    y0_lo, y0_hi = _unpack_bf16_pairs_f32(ybuf[slot, 0])
    y1_lo, y1_hi = _unpack_bf16_pairs_f32(ybuf[slot, 1])
    o_ref[:, 0:half] = h_ref[:, 0:half] + gt[:, 0:1] * y0_lo + gt[:, 1:2] * y1_lo
    o_ref[:, half:] = h_ref[:, half:] + gt[:, 0:1] * y0_hi + gt[:, 1:2] * y1_hi


def _combine(dest, yb, h2, gate, *, tm):
    T, D = h2.shape
    kern = functools.partial(_combine_kernel, tm=tm, n_tok=T)
    grid_spec = pltpu.PrefetchScalarGridSpec(
        num_scalar_prefetch=1,
        grid=(T // tm,),
        in_specs=[
            pl.BlockSpec(memory_space=pl.ANY),
            pl.BlockSpec((tm, D), lambda i, d: (i, 0)),
            pl.BlockSpec((tm, TOP_K), lambda i, d: (i, 0)),
        ],
        out_specs=pl.BlockSpec((tm, D), lambda i, d: (i, 0)),
        scratch_shapes=[
            pltpu.VMEM((2, TOP_K, tm, D // 2), jnp.uint32),
            pltpu.SemaphoreType.DMA((2,)),
        ],
    )
    return pl.pallas_call(
        kern,
        grid_spec=grid_spec,
        out_shape=jax.ShapeDtypeStruct((T, D), F32),
        compiler_params=_cparams(("arbitrary",), 40),
        name="combine",
    )(dest, yb, h2, gate)


EXPERT_ROWS = 256


def kernel(x, mem, positions, g_attn, w_in, q_norm_g, k_norm_g, lambda_q1, lambda_k1, lambda_q2, lambda_k2, diff_subln_g, gla_w_a2, gla_b_a, gla_out_g, w_out, g_cross, g_mem, w_cq, w_ckv, cq_norm_g, ck_norm_g, w_co, g_ffn, w_router_grp, b_router_grp, w_router_exp, b_router_exp, w_gate, w_up, w_down):
    B, S, D = x.shape
    T = B * S
    n_mem = mem.shape[1]
    l = 0
    x2 = x.reshape(T, D)

    half = DIFF_QKDIM // 2
    freq = ROPE_THETA ** (-jnp.arange(half, dtype=F32) / half)
    freq = jnp.tile(freq, LANES // half)[None, :]
    q_scale = math.log2(math.e) * DIFF_QKDIM ** -0.5
    qkg = jnp.stack([jnp.tile(q_norm_g[l], 2) * q_scale, jnp.tile(k_norm_g[l], 2)])
    score_bound = 1.01 * DIFF_QKDIM * q_scale * jnp.max(jnp.abs(q_norm_g[l])) * jnp.max(jnp.abs(k_norm_g[l]))
    lvec = jnp.stack([lambda_q1[l], lambda_k1[l], lambda_q2[l], lambda_k2[l]])

    qk, mid, log_a, sgr = _inproj(x2, g_attn[l][None], positions.reshape(T, 1), freq, qkg, w_in[l].T,
                                  gla_w_a2[l], gla_b_a[l][None], tm=1024)
    diffattn = functools.partial(_diffattn, lvec, qk, mid, diff_subln_g[l][None], B=B, S=S, tq=512)
    mix_d = lax.cond(score_bound <= SCORE_BOUND,
                     functools.partial(diffattn, bounded=True), functools.partial(diffattn, bounded=False))
    mix_g = _gla(mid, log_a, sgr, gla_out_g[l][None], B=B, S=S, blk=512)
    h1 = _outproj(mix_d, mix_g, w_out[l], x2, tm=1024, tn=512)

    hdim = D // CROSS_HEADS
    qc = _normproj(h1, g_cross[l][None], w_cq[l], cq_norm_g[l][None] * (hdim ** -0.5),
                   tm=1024, tn=hdim, n_norm=CROSS_HEADS, name="cq")
    kv = _normproj(mem.reshape(B * n_mem, D), g_mem[l][None], w_ckv[l], ck_norm_g[l][None],
                   tm=B * n_mem, tn=hdim, n_norm=CROSS_HEADS, name="ckv")
    h2 = _cross(qc, kv, w_co[l], h1, S=S, n_mem=n_mem, tm=1024, tn=512)

    w_rt = jnp.concatenate([w_router_grp[l].T, jnp.zeros((SUBLANES - N_GROUPS, D), F32), w_router_exp[l].T])
    b_r = jnp.concatenate([b_router_grp[l], jnp.zeros((SUBLANES - N_GROUPS,), F32), b_router_exp[l]])[:, None]
    eid, gate, xn = _router(h2, g_ffn[l][None], w_rt, b_r, tm=512)
    rank, cnt = _rank(eid, tm=512)

    tb = EXPERT_ROWS
    n_blocks = (T * TOP_K + N_EXPERTS * (tb - 1) + tb - 1) // tb
    counts = cnt[:, 0]
    pcounts = ((counts + tb - 1) // tb) * tb
    pends = jnp.cumsum(pcounts)
    pstarts = pends - pcounts
    nb = (pends[-1:] // tb).astype(I32)
    eids = jnp.arange(N_EXPERTS, dtype=I32)
    blk_start = jnp.arange(n_blocks, dtype=I32) * tb
    blk_e = jnp.minimum(jnp.sum(pends[None, :] <= blk_start[:, None], axis=1), N_EXPERTS - 1).astype(I32)
    blk_first = jnp.concatenate([jnp.ones((1,), I32), (blk_e[1:] != blk_e[:-1]).astype(I32)])
    used = jnp.where(counts > 0, eids, N_EXPERTS)
    next_used = jnp.concatenate([lax.cummin(used[::-1])[::-1][1:], jnp.full((1,), N_EXPERTS, I32)])
    next_used = jnp.where(next_used < N_EXPERTS, next_used, -1)

    def table_at(idx, table):
        return jnp.where(idx >= 0, jnp.sum(jnp.where(idx[:, None] == eids, table, 0), axis=1), -1).astype(I32)

    after_next = table_at(next_used, next_used)
    run_of = (jnp.cumsum((counts > 0).astype(I32)) - 1).astype(I32)
    blk_next = table_at(blk_e, after_next)
    blk_slot = table_at(blk_e, run_of % 2)
    run1 = table_at(blk_e[:1], next_used)
    pick = eid[:TOP_K]
    pstart_of = jnp.sum(jnp.where(pick[..., None] == eids, pstarts, 0), axis=-1)
    dest = (pstart_of + rank[:TOP_K]).astype(I32).reshape(-1)
    pad0 = (pstarts + counts).astype(I32)
    npad = (pcounts - counts).astype(I32)

    xb = _dispatch(dest, pad0, npad, nb, xn, tb=tb, n_blocks=n_blocks, tm=512)
    yb = _experts(xb, blk_e, blk_first, blk_slot, blk_next, run1, nb, w_gate[l], w_up[l], w_down[l],
                  tb=tb, n_blocks=n_blocks)
    out = _combine(dest, yb, h2, gate[:TOP_K].T, tm=512)
    return out.reshape(B, S, D)
```

```python
import functools
import math

import jax
import jax.numpy as jnp
from jax import lax
from jax.experimental import pallas as pl
from jax.experimental.pallas import tpu as pltpu

F32 = jnp.float32
BF16 = jnp.bfloat16
I32 = jnp.int32

LANES = 128
SUBLANES = 8

CHUNK = 64
ROPE_THETA = 10000.0
NORM_EPS = 1e-6
NEG_INF = -1e30
DIFF_HEADS = 8
DIFF_VDIM = 128
DIFF_QKDIM = 64
GLA_HEADS = 4
GLA_VDIM = 256
GLA_KDIM = 128
GLA_GATE_RANK = 16
GLA_TAU = 16.0
CROSS_HEADS = 4
N_GROUPS = 4
EXPERTS_PER_GROUP = 8
N_EXPERTS = N_GROUPS * EXPERTS_PER_GROUP
TOP_K = 2
LAM_INIT = 0.8 - 0.6 * math.exp(-0.3 * 0)

NT_DIMS = (((1,), (1,)), ((), ()))


def _cparams(semantics, vmem_mib):
    return pltpu.CompilerParams(dimension_semantics=semantics,
                                vmem_limit_bytes=vmem_mib * 1024 * 1024)


def _dot(a, b):
    return jnp.dot(a, b, preferred_element_type=F32)


def _dot_nt(a, b):
    return lax.dot_general(a, b, NT_DIMS, preferred_element_type=F32)


def _rms(x, g):
    ms = jnp.mean(x * x, axis=-1, keepdims=True)
    return x * lax.rsqrt(ms + NORM_EPS) * g


def _split_bf16(x):
    hi = x.astype(BF16)
    lo = (x - hi.astype(F32)).astype(BF16)
    return hi, lo


TN = 512
J_QK = 4
J_MID = 6
J_LR = J_QK + J_MID
J_GR = J_LR + 1
N_J = J_GR + 2


def _inproj_kernel(x_ref, g_ref, pos_ref, freq_ref, qkg_ref, w_ref, wgr_ref, wlr_ref, wa2_ref, ba_ref,
                   qk_ref, mid_ref, loga_ref, sgr_ref, n_scr, cos_scr, sin_scr, y_scr):
    j = pl.program_id(1)

    @pl.when(j == 0)
    def _():
        n_scr[...] = _rms(x_ref[...], g_ref[...]).astype(BF16)
        ang = pos_ref[...].astype(F32) * freq_ref[...]
        cos_scr[...] = jnp.cos(ang)
        sin_scr[...] = jnp.sin(ang)

    def qk_epilogue(jq):
        y_prev = y_scr.at[jq % 2]
        lane = lax.broadcasted_iota(I32, (1, LANES), 1)
        low_seg = lane < DIFF_QKDIM
        first_half = (lane % DIFF_QKDIM) < (DIFF_QKDIM // 2)
        gain = qkg_ref[jq // (J_QK // 2):jq // (J_QK // 2) + 1, :]
        cos = cos_scr[...]
        sin = sin_scr[...]
        for c in range(TN // LANES):
            yb = y_prev[:, c * LANES:(c + 1) * LANES]
            y2 = yb * yb
            s_lo = jnp.sum(jnp.where(low_seg, y2, 0.0), axis=-1, keepdims=True)
            s_hi = jnp.sum(jnp.where(low_seg, 0.0, y2), axis=-1, keepdims=True)
            ms = jnp.where(low_seg, s_lo, s_hi) * (1.0 / DIFF_QKDIM)
            yn = yb * lax.rsqrt(ms + NORM_EPS) * gain
            rot = jnp.where(first_half,
                            -pltpu.roll(yn, LANES - DIFF_QKDIM // 2, 1),
                            pltpu.roll(yn, DIFF_QKDIM // 2, 1))
            qk_ref[:, c * LANES:(c + 1) * LANES] = (yn * cos + rot * sin).astype(BF16)

    for jq in range(J_QK + 1):
        @pl.when(j == jq)
        def _():
            y = _dot_nt(n_scr[...], w_ref[...].astype(BF16))
            if jq > 0:
                qk_epilogue(jq - 1)
            if jq < J_QK:
                y_scr[jq % 2] = y
            else:
                mid_ref[...] = y.astype(BF16)

    @pl.when((j > J_QK) & (j < J_LR))
    def _():
        mid_ref[...] = _dot_nt(n_scr[...], w_ref[...].astype(BF16)).astype(BF16)

    @pl.when(j == J_LR)
    def _():
        lr = _dot_nt(n_scr[...], wlr_ref[...].astype(BF16))
        z = _dot(lr.astype(BF16), wa2_ref[...].astype(BF16)) + ba_ref[...]
        log_sig = jnp.minimum(z, 0.0) - jnp.log(1.0 + jnp.exp(-jnp.abs(z)))
        loga_ref[...] = log_sig * (1.0 / GLA_TAU)

    @pl.when(j >= J_GR)
    def _():
        y = _dot_nt(n_scr[...], wgr_ref[...].astype(BF16))
        sgr_ref[...] = (y / (1.0 + jnp.exp(-y))).astype(BF16)


def _inproj(x2, g_attn, pos2, freq, qkg, w_t, w_a2, b_a, *, tm):
    T, D = x2.shape
    n_mid = J_MID * TN
    n_gk = GLA_HEADS * GLA_KDIM
    lr0 = J_LR * TN
    gr0 = lr0 + GLA_GATE_RANK
    n_gr = w_t.shape[0] - gr0
    assert n_gr == 2 * TN and lr0 % GLA_GATE_RANK == 0
    return pl.pallas_call(
        _inproj_kernel,
        grid=(T // tm, N_J),
        in_specs=[
            pl.BlockSpec((tm, D), lambda i, j: (i, 0)),
            pl.BlockSpec((1, D), lambda i, j: (0, 0)),
            pl.BlockSpec((tm, 1), lambda i, j: (i, 0)),
            pl.BlockSpec((1, LANES), lambda i, j: (0, 0)),
            pl.BlockSpec((2, LANES), lambda i, j: (0, 0)),
            pl.BlockSpec((TN, D), lambda i, j: (jnp.minimum(j, J_LR - 1), 0)),
            pl.BlockSpec((pl.Element(TN), pl.Element(D)),
                         lambda i, j: (pl.multiple_of(gr0 + TN * jnp.clip(j - J_GR, 0, 1), SUBLANES), 0)),
            pl.BlockSpec((GLA_GATE_RANK, D), lambda i, j: (lr0 // GLA_GATE_RANK, 0)),
            pl.BlockSpec((GLA_GATE_RANK, n_gk), lambda i, j: (0, 0)),
            pl.BlockSpec((1, n_gk), lambda i, j: (0, 0)),
        ],
        out_specs=[
            pl.BlockSpec((tm, TN), lambda i, j: (i, jnp.clip(j - 1, 0, J_QK - 1))),
            pl.BlockSpec((tm, TN), lambda i, j: (i, jnp.clip(j - J_QK, 0, J_MID - 1))),
            pl.BlockSpec((tm, n_gk), lambda i, j: (i, 0)),
            pl.BlockSpec((tm, TN), lambda i, j: (i, jnp.clip(j - J_GR, 0, 1))),
        ],
        out_shape=[
            jax.ShapeDtypeStruct((T, J_QK * TN), BF16),
            jax.ShapeDtypeStruct((T, n_mid), BF16),
            jax.ShapeDtypeStruct((T, n_gk), F32),
            jax.ShapeDtypeStruct((T, n_gr), BF16),
        ],
        scratch_shapes=[
            pltpu.VMEM((tm, D), BF16),
            pltpu.VMEM((tm, LANES), F32),
            pltpu.VMEM((tm, LANES), F32),
            pltpu.VMEM((2, tm, TN), F32),
        ],
        compiler_params=_cparams(("parallel", "arbitrary"), 56),
        name="inproj",
    )(x2, g_attn, pos2, freq, qkg, w_t, w_t, w_t, w_a2, b_a)


SCORE_BOUND = 80.0


def _diffattn_kernel(ti_ref, tj_ref, lv_ref, q_ref, k_ref, v_ref, sg_ref, o_ref,
                     vext, diag_mask, acc1, acc2, m1, m2, *, tq, n_tiles, bounded):
    S = q_ref.shape[0]
    nq = S // tq
    half = tq // 2
    unroll = 7
    diag_unroll = 4 if nq % 4 == 0 else 2

    vext[:, 0:DIFF_VDIM] = v_ref[...]
    vext[:, DIFF_VDIM:] = jnp.ones((S, DIFF_VDIM), BF16)
    row_chunk = lax.broadcasted_iota(I32, (tq, tq), 0) // CHUNK
    col_chunk = lax.broadcasted_iota(I32, (tq, tq), 1) // CHUNK
    diag_mask[...] = jnp.where(col_chunk <= row_chunk, 1.0, 0.0).astype(BF16)

    lane = lax.broadcasted_iota(I32, (1, LANES), 1)

    def block(i):
        return pl.ds(pl.multiple_of(i * tq, tq), tq)

    def q_comps(i):
        q = q_ref[block(i), :]
        zero = jnp.zeros_like(q)
        return jnp.where(lane < DIFF_QKDIM, q, zero), jnp.where(lane < DIFF_QKDIM, zero, q)

    def diag_tile(i):
        q1, q2 = q_comps(i)
        for qc, acc, m in ((q1, acc1, m1), (q2, acc2, m2)):
            for lo, n_keys in ((0, half), (half, tq)):
                rows = pl.ds(pl.multiple_of(i * tq + lo, half), half)
                keys = pl.ds(pl.multiple_of(i * tq, tq), n_keys)
                mask = diag_mask[lo:lo + half, 0:n_keys]
                s = _dot_nt(qc[lo:lo + half], k_ref[keys, :])
                if bounded:
                    acc[rows, :] = _dot(jnp.exp2(s).astype(BF16) * mask, vext[keys, :])
                else:
                    s = jnp.where(mask > 0, s, NEG_INF)
                    m_new = jnp.max(s, axis=-1, keepdims=True)
                    acc[rows, :] = _dot(jnp.exp2(s - m_new).astype(BF16), vext[keys, :])
                    m[rows, :] = m_new

    def full_tile(i, j):
        q1, q2 = q_comps(i)
        rows = block(i)
        k = k_ref[block(j), :]
        v = vext[block(j), :]
        for qc, acc, m in ((q1, acc1, m1), (q2, acc2, m2)):
            s = _dot_nt(qc, k)
            if bounded:
                acc[rows, :] += _dot(jnp.exp2(s).astype(BF16), v)
            else:
                m_old = m[rows, :]
                m_new = jnp.maximum(m_old, jnp.max(s, axis=-1, keepdims=True))
                p = jnp.exp2(s - m_new)
                acc[rows, :] = jnp.exp2(m_old - m_new) * acc[rows, :] + _dot(p.astype(BF16), v)
                m[rows, :] = m_new

    def diag_body(t, carry):
        for u in range(diag_unroll):
            diag_tile(diag_unroll * t + u)
        return carry
    lax.fori_loop(0, nq // diag_unroll, diag_body, 0)

    def full_body(t, carry):
        for u in range(unroll):
            full_tile(ti_ref[unroll * t + u], tj_ref[unroll * t + u])
        return carry
    lax.fori_loop(0, n_tiles // unroll, full_body, 0)
    for t in range(n_tiles - n_tiles % unroll, n_tiles):
        full_tile(ti_ref[t], tj_ref[t])

    lv = lv_ref[...]
    lam = (jnp.exp(jnp.sum(lv[0:1] * lv[1:2], axis=-1, keepdims=True))
           - jnp.exp(jnp.sum(lv[2:3] * lv[3:4], axis=-1, keepdims=True)) + LAM_INIT)

    def out_body(i, carry):
        rows = block(i)
        a1 = acc1[rows, :]
        a2 = acc2[rows, :]
        o = a1[:, :DIFF_VDIM] / a1[:, DIFF_VDIM:] - lam * (a2[:, :DIFF_VDIM] / a2[:, DIFF_VDIM:])
        o_ref[rows, :] = (_rms(o, sg_ref[...]) * (1.0 - LAM_INIT)).astype(BF16)
        return carry
    lax.fori_loop(0, nq, out_body, 0)


def _diffattn(lvec, qk, mid, subln_g, *, B, S, tq, bounded):
    T = B * S
    nq = S // tq
    assert nq % 2 == 0
    tiles = [(i, j) for i in range(nq) for j in range(i)]
    ti = jnp.asarray([t[0] for t in tiles], I32)
    tj = jnp.asarray([t[1] for t in tiles], I32)
    kern = functools.partial(_diffattn_kernel, tq=tq, n_tiles=len(tiles), bounded=bounded)
    m_rows = SUBLANES if bounded else S
    grid_spec = pltpu.PrefetchScalarGridSpec(
        num_scalar_prefetch=2,
        grid=(B, DIFF_HEADS),
        in_specs=[
            pl.BlockSpec((4, DIFF_QKDIM), lambda b, h, *_: (0, 0)),
            pl.BlockSpec((S, LANES), lambda b, h, *_: (b, h)),
            pl.BlockSpec((S, LANES), lambda b, h, *_: (b, DIFF_HEADS + h)),
            pl.BlockSpec((S, LANES), lambda b, h, *_: (b, h)),
            pl.BlockSpec((1, DIFF_VDIM), lambda b, h, *_: (0, 0)),
        ],
        out_specs=pl.BlockSpec((S, DIFF_VDIM), lambda b, h, *_: (b, h)),
        scratch_shapes=[
            pltpu.VMEM((S, 2 * DIFF_VDIM), BF16),
            pltpu.VMEM((tq, tq), BF16),
            pltpu.VMEM((S, 2 * DIFF_VDIM), F32),
            pltpu.VMEM((S, 2 * DIFF_VDIM), F32),
            pltpu.VMEM((m_rows, 1), F32),
            pltpu.VMEM((m_rows, 1), F32),
        ],
    )
    return pl.pallas_call(
        kern,
        grid_spec=grid_spec,
        out_shape=jax.ShapeDtypeStruct((T, DIFF_HEADS * DIFF_VDIM), BF16),
        compiler_params=_cparams(("parallel", "parallel"), 40),
        name="diffattn_bounded" if bounded else "diffattn_online",
    )(ti, tj, lvec, qk, qk, mid, subln_g)


def _gla_kernel(q_ref, k_ref, v_ref, la_ref, sgr_ref, g_ref, tri_ref, ones_ref, o_ref, state, *, blk):
    @pl.when(pl.program_id(2) == 0)
    def _():
        state[...] = jnp.zeros(state.shape, F32)

    la_t = la_ref[...].T
    k_t = k_ref[...].astype(F32).T
    hi, lo = _split_bf16(la_t)
    tri = tri_ref[...]
    ones = ones_ref[...]
    cum_t = _dot(hi, tri) + _dot(lo, tri)
    tot_t = _dot(hi, ones) + _dot(lo, ones)
    kd_t = k_t * jnp.exp(tot_t - cum_t)

    n_chunks = blk // CHUNK
    lane = lax.broadcasted_iota(I32, (1, LANES), 1)
    d_states = []
    for ck in range(n_chunks):
        pair = slice((ck // 2) * LANES, (ck // 2 + 1) * LANES)
        in_chunk = (lane // CHUNK) == (ck % 2)
        kd = jnp.where(in_chunk, kd_t[:, pair], 0.0).astype(BF16)
        d_states.append(_dot(kd, v_ref[pair, :]))

    st = state[...]
    states = []
    for ck in range(n_chunks):
        decay = jnp.exp(tot_t[:, ck * CHUNK:ck * CHUNK + 1])
        st = decay * st + d_states[ck]
        states.append(st.astype(BF16))
    state[...] = st

    o = jnp.concatenate([_dot(q_ref[ck * CHUNK:(ck + 1) * CHUNK, :], states[ck]) for ck in range(n_chunks)],
                        axis=0) * (GLA_KDIM ** -0.5)
    o_ref[...] = (_rms(o, g_ref[...]) * sgr_ref[...].astype(F32)).astype(BF16)


def _gla(mid, log_a, sgr, out_g, *, B, S, blk):
    T = B * S
    ns = S // blk
    kern = functools.partial(_gla_kernel, blk=blk)
    q_col0 = (DIFF_HEADS * DIFF_VDIM) // GLA_KDIM
    k_col0 = q_col0 + GLA_HEADS
    v_col0 = (DIFF_HEADS * DIFF_VDIM + 2 * GLA_HEADS * GLA_KDIM) // GLA_VDIM
    r = jnp.arange(blk, dtype=I32)[:, None]
    c = jnp.arange(blk, dtype=I32)[None, :]
    same = (r // CHUNK) == (c // CHUNK)
    tri = (same & (r <= c)).astype(BF16)
    ones = same.astype(BF16)
    return pl.pallas_call(
        kern,
        grid=(B, GLA_HEADS, ns),
        in_specs=[
            pl.BlockSpec((blk, GLA_KDIM), lambda b, h, s: (b * ns + s, q_col0 + h)),
            pl.BlockSpec((blk, GLA_KDIM), lambda b, h, s: (b * ns + s, k_col0 + h)),
            pl.BlockSpec((blk, GLA_VDIM), lambda b, h, s: (b * ns + s, v_col0 + h)),
            pl.BlockSpec((blk, GLA_KDIM), lambda b, h, s: (b * ns + s, h)),
            pl.BlockSpec((blk, GLA_VDIM), lambda b, h, s: (b * ns + s, h)),
            pl.BlockSpec((1, GLA_VDIM), lambda b, h, s: (0, 0)),
            pl.BlockSpec((blk, blk), lambda b, h, s: (0, 0)),
            pl.BlockSpec((blk, blk), lambda b, h, s: (0, 0)),
        ],
        out_specs=pl.BlockSpec((blk, GLA_VDIM), lambda b, h, s: (b * ns + s, h)),
        out_shape=jax.ShapeDtypeStruct((T, GLA_HEADS * GLA_VDIM), BF16),
        scratch_shapes=[pltpu.VMEM((GLA_KDIM, GLA_VDIM), F32)],
        compiler_params=_cparams(("parallel", "parallel", "arbitrary"), 32),
        name="gla",
    )(mid, mid, mid, log_a, sgr, out_g, tri, ones)


def _resident_w_map(n_j):
    return lambda i, j: (0, jnp.where(i == 0, j, n_j - 1))


def _outproj_kernel(a_ref, b_ref, wa_ref, wb_ref, x_ref, o_ref, w_scr):
    j = pl.program_id(1)
    ka = a_ref.shape[1]

    @pl.when(pl.program_id(0) == 0)
    def _():
        w_scr[j, 0:ka, :] = wa_ref[...].astype(BF16)
        w_scr[j, ka:, :] = wb_ref[...].astype(BF16)

    acc = _dot(a_ref[...], w_scr[j, 0:ka, :]) + _dot(b_ref[...], w_scr[j, ka:, :])
    o_ref[...] = x_ref[...] + acc


def _outproj(a, b, w_out, x2, *, tm, tn):
    T, ka = a.shape
    kb = b.shape[1]
    assert ka == kb
    D = w_out.shape[1]
    n_j = D // tn
    return pl.pallas_call(
        _outproj_kernel,
        grid=(T // tm, n_j),
        in_specs=[
            pl.BlockSpec((tm, ka), lambda i, j: (i, 0)),
            pl.BlockSpec((tm, kb), lambda i, j: (i, 0)),
            pl.BlockSpec((ka, tn), _resident_w_map(n_j)),
            pl.BlockSpec((kb, tn), lambda i, j: (1, jnp.where(i == 0, j, n_j - 1))),
            pl.BlockSpec((tm, tn), lambda i, j: (i, j)),
        ],
        out_specs=pl.BlockSpec((tm, tn), lambda i, j: (i, j)),
        out_shape=jax.ShapeDtypeStruct((T, D), F32),
        scratch_shapes=[pltpu.VMEM((n_j, ka + kb, tn), BF16)],
        compiler_params=_cparams(("arbitrary", "arbitrary"), 48),
        name="outproj",
    )(a, b, w_out, w_out, x2)


def _normproj_kernel(x_ref, g_ref, w_ref, hg_ref, o_ref, n_scr, *w_scr, n_norm):
    j = pl.program_id(1)

    @pl.when(j == 0)
    def _():
        n_scr[...] = _rms(x_ref[...], g_ref[...]).astype(BF16)

    if w_scr:
        @pl.when(pl.program_id(0) == 0)
        def _():
            w_scr[0][j] = w_ref[...].astype(BF16)
        y = _dot(n_scr[...], w_scr[0][j])
    else:
        y = _dot(n_scr[...], w_ref[...].astype(BF16))

    @pl.when(j < n_norm)
    def _():
        o_ref[...] = _rms(y, hg_ref[...]).astype(BF16)

    @pl.when(j >= n_norm)
    def _():
        o_ref[...] = y.astype(BF16)


def _normproj(x2, g, w, head_g, *, tm, tn, n_norm, name):
    T, D = x2.shape
    N = w.shape[1]
    kern = functools.partial(_normproj_kernel, n_norm=n_norm)
    n_j = N // tn
    resident = T // tm > 1
    return pl.pallas_call(
        kern,
        grid=(T // tm, n_j),
        in_specs=[
            pl.BlockSpec((tm, D), lambda i, j: (i, 0)),
            pl.BlockSpec((1, D), lambda i, j: (0, 0)),
            pl.BlockSpec((D, tn), _resident_w_map(n_j) if resident else (lambda i, j: (0, j))),
            pl.BlockSpec((1, tn), lambda i, j: (0, 0)),
        ],
        out_specs=pl.BlockSpec((tm, tn), lambda i, j: (i, j)),
        out_shape=jax.ShapeDtypeStruct((T, N), BF16),
        scratch_shapes=[pltpu.VMEM((tm, D), BF16)] + ([pltpu.VMEM((n_j, D, tn), BF16)] if resident else []),
        compiler_params=_cparams(("arbitrary", "arbitrary"), 48),
        name=name,
    )(x2, g, w, head_g)


def _cross_kernel(q_ref, k_ref, v_ref, w_ref, h_ref, o_ref, att_scr, w_scr, *, hdim):
    j = pl.program_id(1)

    @pl.when(pl.program_id(0) == 0)
    def _():
        w_scr[j] = w_ref[...].astype(BF16)

    @pl.when(j == 0)
    def _():
        for hd in range(CROSS_HEADS):
            cols = slice(hd * hdim, (hd + 1) * hdim)
            s = lax.dot_general(q_ref[:, cols], k_ref[:, cols], NT_DIMS, preferred_element_type=F32)
            p = jnp.exp(s - jnp.max(s, axis=-1, keepdims=True))
            l = jnp.sum(p, axis=-1, keepdims=True)
            att_scr[:, cols] = (_dot(p.astype(BF16), v_ref[:, cols]) / l).astype(BF16)

    o_ref[...] = h_ref[...] + _dot(att_scr[...], w_scr[j])


def _cross(qc, kv, w_co, h1, *, S, n_mem, tm, tn):
    T, D = qc.shape
    per_b = S // tm
    kern = functools.partial(_cross_kernel, hdim=D // CROSS_HEADS)
    n_j = D // tn
    return pl.pallas_call(
        kern,
        grid=(T // tm, n_j),
        in_specs=[
            pl.BlockSpec((tm, D), lambda i, j: (i, 0)),
            pl.BlockSpec((n_mem, D), lambda i, j: (i // per_b, 0)),
            pl.BlockSpec((n_mem, D), lambda i, j: (i // per_b, 1)),
            pl.BlockSpec((D, tn), _resident_w_map(n_j)),
            pl.BlockSpec((tm, tn), lambda i, j: (i, j)),
        ],
        out_specs=pl.BlockSpec((tm, tn), lambda i, j: (i, j)),
        out_shape=jax.ShapeDtypeStruct((T, D), F32),
        scratch_shapes=[pltpu.VMEM((tm, D), BF16), pltpu.VMEM((n_j, D, tn), BF16)],
        compiler_params=_cparams(("arbitrary", "arbitrary"), 48),
        name="cross",
    )(qc, kv, kv, w_co, h1)


R_ROWS = SUBLANES + N_EXPERTS


def _pack_bf16_pairs(xb16):
    c = xb16.shape[1] // 2
    u = lax.bitcast_convert_type(xb16.astype(F32), jnp.uint32)
    return (u[:, :c] >> 16) | (u[:, c:] & jnp.uint32(0xFFFF0000))


def _store_row_tiles(ref, x):
    for g in range(SUBLANES):
        ref[:, g, :] = x[:, g * LANES:(g + 1) * LANES]


def _load_row_tiles(ref):
    return jnp.concatenate([ref[:, g, :] for g in range(SUBLANES)], axis=1)


def _unpack_bf16_pairs_f32(w):
    lo = lax.bitcast_convert_type(w << 16, F32)
    hi = lax.bitcast_convert_type(w & jnp.uint32(0xFFFF0000), F32)
    return lo, hi


def _unpack_bf16_pairs(w):
    lo, hi = _unpack_bf16_pairs_f32(w)
    return lo.astype(BF16), hi.astype(BF16)


def _router_kernel(h_ref, g_ref, wt_ref, b_ref, eid_ref, gate_ref, xn_ref):
    n = _rms(h_ref[...], g_ref[...])
    nh, nl = _split_bf16(n)
    _store_row_tiles(xn_ref, _pack_bf16_pairs(nh))
    wh, wl = _split_bf16(wt_ref[...])
    nt = functools.partial(lax.dot_general, dimension_numbers=NT_DIMS, preferred_element_type=F32)
    lg = nt(wh, nh) + nt(wh, nl) + nt(wl, nh) + b_ref[...]

    tm = lg.shape[1]
    row = lax.broadcasted_iota(I32, (SUBLANES, tm), 0)

    def first_argmax(v, vmax):
        return jnp.min(jnp.where(v == vmax, row, SUBLANES), axis=0, keepdims=True)

    gl = jnp.where(row < N_GROUPS, lg[0:SUBLANES], NEG_INF)
    gmax = jnp.max(gl, axis=0, keepdims=True)
    grp = first_argmax(gl, gmax)
    grp_w = 1.0 / jnp.sum(jnp.exp(gl - gmax), axis=0, keepdims=True)

    sel = jnp.zeros((SUBLANES, tm), F32)
    for gi in range(N_GROUPS):
        lo = SUBLANES + gi * EXPERTS_PER_GROUP
        sel = jnp.where(grp == gi, lg[lo:lo + EXPERTS_PER_GROUP], sel)
    e = jnp.exp(sel - jnp.max(sel, axis=0, keepdims=True))
    prob = e / jnp.sum(e, axis=0, keepdims=True)
    p1 = jnp.max(prob, axis=0, keepdims=True)
    i1 = first_argmax(prob, p1)
    rest = jnp.where(row == i1, -1.0, prob)
    p2 = jnp.max(rest, axis=0, keepdims=True)
    i2 = first_argmax(rest, p2)
    den = p1 + p2
    base = grp * EXPERTS_PER_GROUP
    eid_ref[...] = jnp.where(row == 0, base + i1, jnp.where(row == 1, base + i2, 0))
    gate_ref[...] = jnp.where(row == 0, grp_w * p1 / den, jnp.where(row == 1, grp_w * p2 / den, 0.0))


def _router(h2, g_ffn, w_rt, b_r, *, tm):
    T, D = h2.shape
    return pl.pallas_call(
        _router_kernel,
        grid=(T // tm,),
        in_specs=[
            pl.BlockSpec((tm, D), lambda i: (i, 0)),
            pl.BlockSpec((1, D), lambda i: (0, 0)),
            pl.BlockSpec((R_ROWS, D), lambda i: (0, 0)),
            pl.BlockSpec((R_ROWS, 1), lambda i: (0, 0)),
        ],
        out_specs=[
            pl.BlockSpec((SUBLANES, tm), lambda i: (0, i)),
            pl.BlockSpec((SUBLANES, tm), lambda i: (0, i)),
            pl.BlockSpec((tm, SUBLANES, LANES), lambda i: (i, 0, 0)),
        ],
        out_shape=[
            jax.ShapeDtypeStruct((SUBLANES, T), I32),
            jax.ShapeDtypeStruct((SUBLANES, T), F32),
            jax.ShapeDtypeStruct((T, SUBLANES, LANES), jnp.uint32),
        ],
        compiler_params=_cparams(("parallel",), 32),
        name="router",
    )(h2, g_ffn, w_rt, b_r)


def _rank_kernel(eid_ref, rank_ref, cnt_ref, carry):
    @pl.when(pl.program_id(0) == 0)
    def _():
        carry[...] = jnp.zeros(carry.shape, F32)

    tm = eid_ref.shape[1]
    e0 = eid_ref[0:1, :]
    e1 = eid_ref[1:2, :]
    erow = lax.broadcasted_iota(I32, (N_EXPERTS, tm), 0)
    hit = jnp.where((erow == e0) | (erow == e1), 1.0, 0.0)
    r = lax.broadcasted_iota(I32, (tm, tm), 0)
    c = lax.broadcasted_iota(I32, (tm, tm), 1)
    before = jnp.where(r < c, 1.0, 0.0).astype(BF16)
    pre = _dot(hit.astype(BF16), before) + carry[:, 0:1]
    rank0 = jnp.sum(jnp.where(erow == e0, pre, 0.0), axis=0, keepdims=True)
    rank1 = jnp.sum(jnp.where(erow == e1, pre, 0.0), axis=0, keepdims=True)
    row = lax.broadcasted_iota(I32, (SUBLANES, tm), 0)
    rank_ref[...] = jnp.where(row == 0, rank0, jnp.where(row == 1, rank1, 0.0)).astype(I32)
    total = carry[...] + jnp.sum(hit, axis=1, keepdims=True)
    carry[...] = total
    cnt_ref[...] = total.astype(I32)


def _rank(eid, *, tm):
    T = eid.shape[1]
    return pl.pallas_call(
        _rank_kernel,
        grid=(T // tm,),
        in_specs=[pl.BlockSpec((SUBLANES, tm), lambda i: (0, i))],
        out_specs=[
            pl.BlockSpec((SUBLANES, tm), lambda i: (0, i)),
            pl.BlockSpec((N_EXPERTS, LANES), lambda i: (0, 0)),
        ],
        out_shape=[
            jax.ShapeDtypeStruct((SUBLANES, T), I32),
            jax.ShapeDtypeStruct((N_EXPERTS, LANES), I32),
        ],
        scratch_shapes=[pltpu.VMEM((N_EXPERTS, LANES), F32)],
        compiler_params=_cparams(("arbitrary",), 32),
        name="rank",
    )(eid)


def _dispatch_kernel(dest_ref, pad0_ref, npad_ref, nb_ref, xn_ref, xb_hbm, zbuf, sem, psem,
                     *, n_tok, tb, n_blocks):
    i = pl.program_id(0)
    tm = xn_ref.shape[0]
    base = i * tm

    def row_body(r, carry):
        for k in range(TOP_K):
            pltpu.make_async_copy(xn_ref.at[r], xb_hbm.at[dest_ref[k * n_tok + base + r]], sem).start(priority=k)
        return carry
    lax.fori_loop(0, tm, row_body, 0, unroll=8)

    @pl.when(i == 0)
    def _():
        _dispatch_fill(pad0_ref, npad_ref, nb_ref, xb_hbm, zbuf, psem, tb=tb, n_blocks=n_blocks)

    for k in range(TOP_K):
        pltpu.make_async_copy(xn_ref, xb_hbm.at[pl.ds(0, tm)], sem).wait()


def _dispatch_fill(pad0_ref, npad_ref, nb_ref, xb_hbm, zbuf, psem, *, tb, n_blocks):
    zbuf[...] = jnp.zeros(zbuf.shape, zbuf.dtype)

    def pad_copy(e):
        n = npad_ref[e]
        return pltpu.make_async_copy(zbuf.at[pl.ds(0, n)], xb_hbm.at[pl.ds(pad0_ref[e], n)], psem.at[0])

    def tail_copy(blk):
        return pltpu.make_async_copy(zbuf, xb_hbm.at[pl.ds(pl.multiple_of(blk * tb, tb), tb)], psem.at[1])

    def for_each_pad(fn):
        def body(e, c):
            @pl.when(npad_ref[e] > 0)
            def _():
                fn(e)
            return c
        lax.fori_loop(0, N_EXPERTS, body, 0)

    def for_each_tail(fn):
        def body(b, c):
            fn(b)
            return c
        lax.fori_loop(nb_ref[0], n_blocks, body, 0)

    for_each_pad(lambda e: pad_copy(e).start())
    for_each_tail(lambda b: tail_copy(b).start())
    for_each_pad(lambda e: pad_copy(e).wait())
    for_each_tail(lambda b: tail_copy(b).wait())


def _dispatch(dest, pad0, npad, nb, xn, *, tb, n_blocks, tm):
    T = xn.shape[0]
    tile = xn.shape[1:]
    kern = functools.partial(_dispatch_kernel, n_tok=T, tb=tb, n_blocks=n_blocks)
    grid_spec = pltpu.PrefetchScalarGridSpec(
        num_scalar_prefetch=4,
        grid=(T // tm,),
        in_specs=[pl.BlockSpec((tm,) + tile, lambda i, *_: (i, 0, 0))],
        out_specs=pl.BlockSpec(memory_space=pl.ANY),
        scratch_shapes=[
            pltpu.VMEM((tb,) + tile, xn.dtype),
            pltpu.SemaphoreType.DMA(()),
            pltpu.SemaphoreType.DMA((2,)),
        ],
    )
    return pl.pallas_call(
        kern,
        grid_spec=grid_spec,
        out_shape=jax.ShapeDtypeStruct((n_blocks * tb,) + tile, xn.dtype),
        compiler_params=_cparams(("arbitrary",), 32),
        name="dispatch",
    )(dest, pad0, npad, nb, xn)


def _expert_kernel(be_ref, first_ref, slot_ref, nxt_ref, run1_ref, nb_ref,
                   x0_ref, xnext_ref, wg_hbm, wu_hbm, wd_hbm, y_ref,
                   wg_f, wu_f, wd_f, wsem, wg_b, wu_b, wd_b, xlo0, xhi0, xlo1, xhi1):
    xlo = (xlo0, xlo1)
    xhi = (xhi0, xhi1)
    i = pl.program_id(0)
    nb = nb_ref[0]

    def weight_copies(e, slot):
        return (pltpu.make_async_copy(wg_hbm.at[e], wg_f.at[slot], wsem.at[3 * slot]),
                pltpu.make_async_copy(wu_hbm.at[e], wu_f.at[slot], wsem.at[3 * slot + 1]),
                pltpu.make_async_copy(wd_hbm.at[e], wd_f.at[slot], wsem.at[3 * slot + 2]))

    @pl.when(i == 0)
    def _():
        for cp in weight_copies(be_ref[0], 0):
            cp.start()

    @pl.when((i == 0) & (run1_ref[0] >= 0))
    def _():
        for cp in weight_copies(jnp.maximum(run1_ref[0], 0), 1):
            cp.start()

    first = (i < nb) & (first_ref[i] == 1)
    slot = slot_ref[i]

    @pl.when(first)
    def _():
        for cp in weight_copies(0, slot):
            cp.wait()
        wg_b[...] = wg_f[slot].astype(BF16)
        wu_b[...] = wu_f[slot].astype(BF16)
        wd_b[...] = wd_f[slot].astype(BF16)

    @pl.when(first & (nxt_ref[i] >= 0))
    def _():
        for cp in weight_copies(jnp.maximum(nxt_ref[i], 0), slot):
            cp.start()

    def unpack_into(src_ref, p):
        xlo[p][...], xhi[p][...] = _unpack_bf16_pairs(_load_row_tiles(src_ref))

    @pl.when(i == 0)
    def _():
        unpack_into(x0_ref, 0)

    for p in range(2):
        @pl.when((i < nb) & (i % 2 == p))
        def _():
            unpack_into(xnext_ref, 1 - p)
            n_lo = xlo[p][...]
            n_hi = xhi[p][...]
            half = n_lo.shape[1]
            a = _dot(n_lo, wg_b[0:half, :]) + _dot(n_hi, wg_b[half:, :])
            u = _dot(n_lo, wu_b[0:half, :]) + _dot(n_hi, wu_b[half:, :])
            hdn = (a / (1.0 + jnp.exp(-a))) * u
            y_ref[...] = _pack_bf16_pairs(_dot(hdn.astype(BF16), wd_b[...]).astype(BF16))

    @pl.when(i >= nb)
    def _():
        y_ref[...] = jnp.zeros(y_ref.shape, y_ref.dtype)


def _experts(xb, blk_e, blk_first, blk_slot, blk_next, run1, nb, w_gate, w_up, w_down, *, tb, n_blocks):
    D, De = w_gate.shape[1:]
    assert xb.shape[1:] == (SUBLANES, LANES) and D == 2 * SUBLANES * LANES

    def next_x_map(i, be, first, slot, nxt, r1, nbr):
        return (jnp.minimum(i + 1, nbr[0] - 1), 0, 0)

    grid_spec = pltpu.PrefetchScalarGridSpec(
        num_scalar_prefetch=6,
        grid=(n_blocks,),
        in_specs=[
            pl.BlockSpec((tb, SUBLANES, LANES), lambda i, *_: (0, 0, 0)),
            pl.BlockSpec((tb, SUBLANES, LANES), next_x_map),
            pl.BlockSpec(memory_space=pl.ANY),
            pl.BlockSpec(memory_space=pl.ANY),
            pl.BlockSpec(memory_space=pl.ANY),
        ],
        out_specs=pl.BlockSpec((tb, D // 2), lambda i, *_: (i, 0)),
        scratch_shapes=[
            pltpu.VMEM((2, D, De), F32),
            pltpu.VMEM((2, D, De), F32),
            pltpu.VMEM((2, De, D), F32),
            pltpu.SemaphoreType.DMA((6,)),
            pltpu.VMEM((D, De), BF16),
            pltpu.VMEM((D, De), BF16),
            pltpu.VMEM((De, D), BF16),
        ] + [pltpu.VMEM((tb, D // 2), BF16)] * 4,
    )
    return pl.pallas_call(
        _expert_kernel,
        grid_spec=grid_spec,
        out_shape=jax.ShapeDtypeStruct((n_blocks * tb, D // 2), jnp.uint32),
        compiler_params=_cparams(("arbitrary",), 56),
        name="experts",
    )(blk_e, blk_first, blk_slot, blk_next, run1, nb, xb, xb, w_gate, w_up, w_down)


def _combine_kernel(dest_ref, y_hbm, h_ref, gate_ref, o_ref, ybuf, sem, *, tm, n_tok):
    i = pl.program_id(0)
    n = pl.num_programs(0)

    def row_copy(d, k, r, slot):
        return pltpu.make_async_copy(y_hbm.at[pl.ds(d, 1), :], ybuf.at[slot, k, pl.ds(r, 1), :], sem.at[slot])

    def start_gather(blk, slot):
        def body(r, carry):
            for k in range(TOP_K):
                row_copy(dest_ref[k * n_tok + blk * tm + r], k, r, slot).start(priority=k)
            return carry
        lax.fori_loop(0, tm, body, 0, unroll=8)

    def wait_gather(slot):
        for k in range(TOP_K):
            pltpu.make_async_copy(y_hbm.at[pl.ds(0, tm), :], ybuf.at[slot, k], sem.at[slot]).wait()

    @pl.when(i == 0)
    def _():
        start_gather(0, 0)

    @pl.when(i + 1 < n)
    def _():
        start_gather(i + 1, (i + 1) % 2)

    slot = i % 2
    wait_gather(slot)
    gt = gate_ref[...]
    half = h_ref.shape[1] // 2
    y0_lo, y0_hi = _unpack_bf16_pairs_f32(ybuf[slot, 0])
    y1_lo, y1_hi = _unpack_bf16_pairs_f32(ybuf[slot, 1])
    o_ref[:, 0:half] = h_ref[:, 0:half] + gt[:, 0:1] * y0_lo + gt[:, 1:2] * y1_lo
    o_ref[:, half:] = h_ref[:, half:] + gt[:, 0:1] * y0_hi + gt[:, 1:2] * y1_hi


def _combine(dest, yb, h2, gate, *, tm):
    T, D = h2.shape
    kern = functools.partial(_combine_kernel, tm=tm, n_tok=T)
    grid_spec = pltpu.PrefetchScalarGridSpec(
        num_scalar_prefetch=1,
        grid=(T // tm,),
        in_specs=[
            pl.BlockSpec(memory_space=pl.ANY),
            pl.BlockSpec((tm, D), lambda i, d: (i, 0)),
            pl.BlockSpec((tm, TOP_K), lambda i, d: (i, 0)),
        ],
        out_specs=pl.BlockSpec((tm, D), lambda i, d: (i, 0)),
        scratch_shapes=[
            pltpu.VMEM((2, TOP_K, tm, D // 2), jnp.uint32),
            pltpu.SemaphoreType.DMA((2,)),
        ],
    )
    return pl.pallas_call(
        kern,
        grid_spec=grid_spec,
        out_shape=jax.ShapeDtypeStruct((T, D), F32),
        compiler_params=_cparams(("arbitrary",), 40),
        name="combine",
    )(dest, yb, h2, gate)


EXPERT_ROWS = 256


def kernel(x, mem, positions, g_attn, w_in, q_norm_g, k_norm_g, lambda_q1, lambda_k1, lambda_q2, lambda_k2, diff_subln_g, gla_w_a2, gla_b_a, gla_out_g, w_out, g_cross, g_mem, w_cq, w_ckv, cq_norm_g, ck_norm_g, w_co, g_ffn, w_router_grp, b_router_grp, w_router_exp, b_router_exp, w_gate, w_up, w_down):
    B, S, D = x.shape
    T = B * S
    n_mem = mem.shape[1]
    l = 0
    x2 = x.reshape(T, D)

    half = DIFF_QKDIM // 2
    freq = ROPE_THETA ** (-jnp.arange(half, dtype=F32) / half)
    freq = jnp.tile(freq, LANES // half)[None, :]
    q_scale = math.log2(math.e) * DIFF_QKDIM ** -0.5
    qkg = jnp.stack([jnp.tile(q_norm_g[l], 2) * q_scale, jnp.tile(k_norm_g[l], 2)])
    score_bound = 1.01 * DIFF_QKDIM * q_scale * jnp.max(jnp.abs(q_norm_g[l])) * jnp.max(jnp.abs(k_norm_g[l]))
    lvec = jnp.stack([lambda_q1[l], lambda_k1[l], lambda_q2[l], lambda_k2[l]])

    qk, mid, log_a, sgr = _inproj(x2, g_attn[l][None], positions.reshape(T, 1), freq, qkg, w_in[l].T,
                                  gla_w_a2[l], gla_b_a[l][None], tm=1024)
    diffattn = functools.partial(_diffattn, lvec, qk, mid, diff_subln_g[l][None], B=B, S=S, tq=512)
    mix_d = lax.cond(score_bound <= SCORE_BOUND,
                     functools.partial(diffattn, bounded=True), functools.partial(diffattn, bounded=False))
    mix_g = _gla(mid, log_a, sgr, gla_out_g[l][None], B=B, S=S, blk=512)
    h1 = _outproj(mix_d, mix_g, w_out[l], x2, tm=1024, tn=512)

    hdim = D // CROSS_HEADS
    qc = _normproj(h1, g_cross[l][None], w_cq[l], cq_norm_g[l][None] * (hdim ** -0.5),
                   tm=1024, tn=hdim, n_norm=CROSS_HEADS, name="cq")
    kv = _normproj(mem.reshape(B * n_mem, D), g_mem[l][None], w_ckv[l], ck_norm_g[l][None],
                   tm=B * n_mem, tn=hdim, n_norm=CROSS_HEADS, name="ckv")
    h2 = _cross(qc, kv, w_co[l], h1, S=S, n_mem=n_mem, tm=1024, tn=512)

    w_rt = jnp.concatenate([w_router_grp[l].T, jnp.zeros((SUBLANES - N_GROUPS, D), F32), w_router_exp[l].T])
    b_r = jnp.concatenate([b_router_grp[l], jnp.zeros((SUBLANES - N_GROUPS,), F32), b_router_exp[l]])[:, None]
    eid, gate, xn = _router(h2, g_ffn[l][None], w_rt, b_r, tm=512)
    rank, cnt = _rank(eid, tm=512)

    tb = EXPERT_ROWS
    n_blocks = (T * TOP_K + N_EXPERTS * (tb - 1) + tb - 1) // tb
    counts = cnt[:, 0]
    pcounts = ((counts + tb - 1) // tb) * tb
    pends = jnp.cumsum(pcounts)
    pstarts = pends - pcounts
    nb = (pends[-1:] // tb).astype(I32)
    eids = jnp.arange(N_EXPERTS, dtype=I32)
    blk_start = jnp.arange(n_blocks, dtype=I32) * tb
    blk_e = jnp.minimum(jnp.sum(pends[None, :] <= blk_start[:, None], axis=1), N_EXPERTS - 1).astype(I32)
    blk_first = jnp.concatenate([jnp.ones((1,), I32), (blk_e[1:] != blk_e[:-1]).astype(I32)])
    used = jnp.where(counts > 0, eids, N_EXPERTS)
    next_used = jnp.concatenate([lax.cummin(used[::-1])[::-1][1:], jnp.full((1,), N_EXPERTS, I32)])
    next_used = jnp.where(next_used < N_EXPERTS, next_used, -1)

    def table_at(idx, table):
        return jnp.where(idx >= 0, jnp.sum(jnp.where(idx[:, None] == eids, table, 0), axis=1), -1).astype(I32)

    after_next = table_at(next_used, next_used)
    run_of = (jnp.cumsum((counts > 0).astype(I32)) - 1).astype(I32)
    blk_next = table_at(blk_e, after_next)
    blk_slot = table_at(blk_e, run_of % 2)
    run1 = table_at(blk_e[:1], next_used)
    pick = eid[:TOP_K]
    pstart_of = jnp.sum(jnp.where(pick[..., None] == eids, pstarts, 0), axis=-1)
    dest = (pstart_of + rank[:TOP_K]).astype(I32).reshape(-1)
    pad0 = (pstarts + counts).astype(I32)
    npad = (pcounts - counts).astype(I32)

    xb = _dispatch(dest, pad0, npad, nb, xn, tb=tb, n_blocks=n_blocks, tm=512)
    yb = _experts(xb, blk_e, blk_first, blk_slot, blk_next, run1, nb, w_gate[l], w_up[l], w_down[l],
                  tb=tb, n_blocks=n_blocks)
    out = _combine(dest, yb, h2, gate[:TOP_K].T, tm=512)
    return out.reshape(B, S, D)
```

```python
import functools
import math
from typing import NamedTuple

import jax
import jax.numpy as jnp
from jax import lax
from jax.experimental import pallas as pl
from jax.experimental.pallas import tpu as pltpu

F32 = jnp.float32
BF16 = jnp.bfloat16
I32 = jnp.int32

LANES = 128
SUBLANES = 8

CHUNK = 64
ROPE_THETA = 10000.0
NORM_EPS = 1e-6
NEG_INF = -1e30
DIFF_HEADS = 8
DIFF_VDIM = 128
DIFF_QKDIM = 64
GLA_HEADS = 4
GLA_VDIM = 256
GLA_KDIM = 128
GLA_GATE_RANK = 16
GLA_TAU = 16.0
CROSS_HEADS = 4
N_GROUPS = 4
EXPERTS_PER_GROUP = 8
N_EXPERTS = N_GROUPS * EXPERTS_PER_GROUP
TOP_K = 2
LAM_INIT = 0.8 - 0.6 * math.exp(-0.3 * 0)

NT_DIMS = (((1,), (1,)), ((), ()))


def _cparams(semantics, vmem_mib):
    return pltpu.CompilerParams(dimension_semantics=semantics,
                                vmem_limit_bytes=vmem_mib * 1024 * 1024)


def _dot(a, b):
    return jnp.dot(a, b, preferred_element_type=F32)


def _dot_nt(a, b):
    return lax.dot_general(a, b, NT_DIMS, preferred_element_type=F32)


def _rms(x, g):
    ms = jnp.mean(x * x, axis=-1, keepdims=True)
    return x * lax.rsqrt(ms + NORM_EPS) * g


def _split_bf16(x):
    hi = x.astype(BF16)
    lo = (x - hi.astype(F32)).astype(BF16)
    return hi, lo


TN = 512
J_QK = 4
J_MID = 6
J_LR = J_QK + J_MID
J_GR = J_LR + 1
N_J = J_GR + 2


def _inproj_kernel(x_ref, g_ref, pos_ref, freq_ref, qkg_ref, w_ref, wgr_ref, wlr_ref, wa2_ref, ba_ref,
                   qk_ref, mid_ref, loga_ref, sgr_ref, n_scr, cos_scr, sin_scr, y_scr):
    j = pl.program_id(1)

    @pl.when(j == 0)
    def _():
        n_scr[...] = _rms(x_ref[...], g_ref[...]).astype(BF16)
        ang = pos_ref[...].astype(F32) * freq_ref[...]
        cos_scr[...] = jnp.cos(ang)
        sin_scr[...] = jnp.sin(ang)

    def qk_epilogue(jq):
        y_prev = y_scr.at[jq % 2]
        lane = lax.broadcasted_iota(I32, (1, LANES), 1)
        low_seg = lane < DIFF_QKDIM
        first_half = (lane % DIFF_QKDIM) < (DIFF_QKDIM // 2)
        gain = qkg_ref[jq // (J_QK // 2):jq // (J_QK // 2) + 1, :]
        cos = cos_scr[...]
        sin = sin_scr[...]
        for c in range(TN // LANES):
            yb = y_prev[:, c * LANES:(c + 1) * LANES]
            y2 = yb * yb
            s_lo = jnp.sum(jnp.where(low_seg, y2, 0.0), axis=-1, keepdims=True)
            s_hi = jnp.sum(jnp.where(low_seg, 0.0, y2), axis=-1, keepdims=True)
            ms = jnp.where(low_seg, s_lo, s_hi) * (1.0 / DIFF_QKDIM)
            yn = yb * lax.rsqrt(ms + NORM_EPS) * gain
            rot = jnp.where(first_half,
                            -pltpu.roll(yn, LANES - DIFF_QKDIM // 2, 1),
                            pltpu.roll(yn, DIFF_QKDIM // 2, 1))
            qk_ref[:, c * LANES:(c + 1) * LANES] = (yn * cos + rot * sin).astype(BF16)

    for jq in range(J_QK + 1):
        @pl.when(j == jq)
        def _():
            y = _dot_nt(n_scr[...], w_ref[...].astype(BF16))
            if jq > 0:
                qk_epilogue(jq - 1)
            if jq < J_QK:
                y_scr[jq % 2] = y
            else:
                mid_ref[...] = y.astype(BF16)

    @pl.when((j > J_QK) & (j < J_LR))
    def _():
        mid_ref[...] = _dot_nt(n_scr[...], w_ref[...].astype(BF16)).astype(BF16)

    @pl.when(j == J_LR)
    def _():
        lr = _dot_nt(n_scr[...], wlr_ref[...].astype(BF16))
        z = _dot(lr.astype(BF16), wa2_ref[...].astype(BF16)) + ba_ref[...]
        log_sig = jnp.minimum(z, 0.0) - jnp.log(1.0 + jnp.exp(-jnp.abs(z)))
        loga_ref[...] = log_sig * (1.0 / GLA_TAU)

    @pl.when(j >= J_GR)
    def _():
        y = _dot_nt(n_scr[...], wgr_ref[...].astype(BF16))
        sgr_ref[...] = (y / (1.0 + jnp.exp(-y))).astype(BF16)


def _inproj(x2, g_attn, pos2, freq, qkg, w_t, w_a2, b_a, *, tm):
    T, D = x2.shape
    n_mid = J_MID * TN
    n_gk = GLA_HEADS * GLA_KDIM
    lr0 = J_LR * TN
    gr0 = lr0 + GLA_GATE_RANK
    n_gr = w_t.shape[0] - gr0
    assert n_gr == 2 * TN and lr0 % GLA_GATE_RANK == 0
    return pl.pallas_call(
        _inproj_kernel,
        grid=(T // tm, N_J),
        in_specs=[
            pl.BlockSpec((tm, D), lambda i, j: (i, 0)),
            pl.BlockSpec((1, D), lambda i, j: (0, 0)),
            pl.BlockSpec((tm, 1), lambda i, j: (i, 0)),
            pl.BlockSpec((1, LANES), lambda i, j: (0, 0)),
            pl.BlockSpec((2, LANES), lambda i, j: (0, 0)),
            pl.BlockSpec((TN, D), lambda i, j: (jnp.minimum(j, J_LR - 1), 0)),
            pl.BlockSpec((pl.Element(TN), pl.Element(D)),
                         lambda i, j: (pl.multiple_of(gr0 + TN * jnp.clip(j - J_GR, 0, 1), SUBLANES), 0)),
            pl.BlockSpec((GLA_GATE_RANK, D), lambda i, j: (lr0 // GLA_GATE_RANK, 0)),
            pl.BlockSpec((GLA_GATE_RANK, n_gk), lambda i, j: (0, 0)),
            pl.BlockSpec((1, n_gk), lambda i, j: (0, 0)),
        ],
        out_specs=[
            pl.BlockSpec((tm, TN), lambda i, j: (i, jnp.clip(j - 1, 0, J_QK - 1))),
            pl.BlockSpec((tm, TN), lambda i, j: (i, jnp.clip(j - J_QK, 0, J_MID - 1))),
            pl.BlockSpec((tm, n_gk), lambda i, j: (i, 0)),
            pl.BlockSpec((tm, TN), lambda i, j: (i, jnp.clip(j - J_GR, 0, 1))),
        ],
        out_shape=[
            jax.ShapeDtypeStruct((T, J_QK * TN), BF16),
            jax.ShapeDtypeStruct((T, n_mid), BF16),
            jax.ShapeDtypeStruct((T, n_gk), F32),
            jax.ShapeDtypeStruct((T, n_gr), BF16),
        ],
        scratch_shapes=[
            pltpu.VMEM((tm, D), BF16),
            pltpu.VMEM((tm, LANES), F32),
            pltpu.VMEM((tm, LANES), F32),
            pltpu.VMEM((2, tm, TN), F32),
        ],
        compiler_params=_cparams(("parallel", "arbitrary"), 56),
        name="inproj",
    )(x2, g_attn, pos2, freq, qkg, w_t, w_t, w_t, w_a2, b_a)


SCORE_BOUND = 80.0


def _diffattn_kernel(ti_ref, tj_ref, lv_ref, q_ref, k_ref, v_ref, sg_ref, o_ref,
                     vext, diag_mask, acc1, acc2, m1, m2, *, tq, n_tiles, bounded):
    S = q_ref.shape[0]
    nq = S // tq
    half = tq // 2
    unroll = 7
    diag_unroll = 4 if nq % 4 == 0 else 2

    vext[:, 0:DIFF_VDIM] = v_ref[...]
    vext[:, DIFF_VDIM:] = jnp.ones((S, DIFF_VDIM), BF16)
    row_chunk = lax.broadcasted_iota(I32, (tq, tq), 0) // CHUNK
    col_chunk = lax.broadcasted_iota(I32, (tq, tq), 1) // CHUNK
    diag_mask[...] = jnp.where(col_chunk <= row_chunk, 1.0, 0.0).astype(BF16)

    lane = lax.broadcasted_iota(I32, (1, LANES), 1)

    def block(i):
        return pl.ds(pl.multiple_of(i * tq, tq), tq)

    def q_comps(i):
        q = q_ref[block(i), :]
        zero = jnp.zeros_like(q)
        return jnp.where(lane < DIFF_QKDIM, q, zero), jnp.where(lane < DIFF_QKDIM, zero, q)

    def diag_tile(i):
        q1, q2 = q_comps(i)
        for qc, acc, m in ((q1, acc1, m1), (q2, acc2, m2)):
            for lo, n_keys in ((0, half), (half, tq)):
                rows = pl.ds(pl.multiple_of(i * tq + lo, half), half)
                keys = pl.ds(pl.multiple_of(i * tq, tq), n_keys)
                mask = diag_mask[lo:lo + half, 0:n_keys]
                s = _dot_nt(qc[lo:lo + half], k_ref[keys, :])
                if bounded:
                    acc[rows, :] = _dot(jnp.exp2(s).astype(BF16) * mask, vext[keys, :])
                else:
                    s = jnp.where(mask > 0, s, NEG_INF)
                    m_new = jnp.max(s, axis=-1, keepdims=True)
                    acc[rows, :] = _dot(jnp.exp2(s - m_new).astype(BF16), vext[keys, :])
                    m[rows, :] = m_new

    def full_tile(i, j):
        q1, q2 = q_comps(i)
        rows = block(i)
        k = k_ref[block(j), :]
        v = vext[block(j), :]
        for qc, acc, m in ((q1, acc1, m1), (q2, acc2, m2)):
            s = _dot_nt(qc, k)
            if bounded:
                acc[rows, :] += _dot(jnp.exp2(s).astype(BF16), v)
            else:
                m_old = m[rows, :]
                m_new = jnp.maximum(m_old, jnp.max(s, axis=-1, keepdims=True))
                p = jnp.exp2(s - m_new)
                acc[rows, :] = jnp.exp2(m_old - m_new) * acc[rows, :] + _dot(p.astype(BF16), v)
                m[rows, :] = m_new

    def diag_body(t, carry):
        for u in range(diag_unroll):
            diag_tile(diag_unroll * t + u)
        return carry
    lax.fori_loop(0, nq // diag_unroll, diag_body, 0)

    def full_body(t, carry):
        for u in range(unroll):
            full_tile(ti_ref[unroll * t + u], tj_ref[unroll * t + u])
        return carry
    lax.fori_loop(0, n_tiles // unroll, full_body, 0)
    for t in range(n_tiles - n_tiles % unroll, n_tiles):
        full_tile(ti_ref[t], tj_ref[t])

    lv = lv_ref[...]
    lam = (jnp.exp(jnp.sum(lv[0:1] * lv[1:2], axis=-1, keepdims=True))
           - jnp.exp(jnp.sum(lv[2:3] * lv[3:4], axis=-1, keepdims=True)) + LAM_INIT)

    def out_body(i, carry):
        rows = block(i)
        a1 = acc1[rows, :]
        a2 = acc2[rows, :]
        o = a1[:, :DIFF_VDIM] / a1[:, DIFF_VDIM:] - lam * (a2[:, :DIFF_VDIM] / a2[:, DIFF_VDIM:])
        o_ref[rows, :] = (_rms(o, sg_ref[...]) * (1.0 - LAM_INIT)).astype(BF16)
        return carry
    lax.fori_loop(0, nq, out_body, 0)


def _diffattn(lvec, qk, mid, subln_g, *, B, S, tq, bounded):
    T = B * S
    nq = S // tq
    assert nq % 2 == 0
    tiles = [(i, j) for i in range(nq) for j in range(i)]
    ti = jnp.asarray([t[0] for t in tiles], I32)
    tj = jnp.asarray([t[1] for t in tiles], I32)
    kern = functools.partial(_diffattn_kernel, tq=tq, n_tiles=len(tiles), bounded=bounded)
    m_rows = SUBLANES if bounded else S
    grid_spec = pltpu.PrefetchScalarGridSpec(
        num_scalar_prefetch=2,
        grid=(B, DIFF_HEADS),
        in_specs=[
            pl.BlockSpec((4, DIFF_QKDIM), lambda b, h, *_: (0, 0)),
            pl.BlockSpec((S, LANES), lambda b, h, *_: (b, h)),
            pl.BlockSpec((S, LANES), lambda b, h, *_: (b, DIFF_HEADS + h)),
            pl.BlockSpec((S, LANES), lambda b, h, *_: (b, h)),
            pl.BlockSpec((1, DIFF_VDIM), lambda b, h, *_: (0, 0)),
        ],
        out_specs=pl.BlockSpec((S, DIFF_VDIM), lambda b, h, *_: (b, h)),
        scratch_shapes=[
            pltpu.VMEM((S, 2 * DIFF_VDIM), BF16),
            pltpu.VMEM((tq, tq), BF16),
            pltpu.VMEM((S, 2 * DIFF_VDIM), F32),
            pltpu.VMEM((S, 2 * DIFF_VDIM), F32),
            pltpu.VMEM((m_rows, 1), F32),
            pltpu.VMEM((m_rows, 1), F32),
        ],
    )
    return pl.pallas_call(
        kern,
        grid_spec=grid_spec,
        out_shape=jax.ShapeDtypeStruct((T, DIFF_HEADS * DIFF_VDIM), BF16),
        compiler_params=_cparams(("parallel", "parallel"), 40),
        name="diffattn_bounded" if bounded else "diffattn_online",
    )(ti, tj, lvec, qk, qk, mid, subln_g)


def _gla_kernel(q_ref, k_ref, v_ref, la_ref, sgr_ref, g_ref, tri_ref, ones_ref, o_ref, state, *, blk):
    @pl.when(pl.program_id(2) == 0)
    def _():
        state[...] = jnp.zeros(state.shape, F32)

    la_t = la_ref[...].T
    k_t = k_ref[...].astype(F32).T
    hi, lo = _split_bf16(la_t)
    tri = tri_ref[...]
    ones = ones_ref[...]
    cum_t = _dot(hi, tri) + _dot(lo, tri)
    tot_t = _dot(hi, ones) + _dot(lo, ones)
    kd_t = k_t * jnp.exp(tot_t - cum_t)

    n_chunks = blk // CHUNK
    lane = lax.broadcasted_iota(I32, (1, LANES), 1)
    d_states = []
    for ck in range(n_chunks):
        pair = slice((ck // 2) * LANES, (ck // 2 + 1) * LANES)
        in_chunk = (lane // CHUNK) == (ck % 2)
        kd = jnp.where(in_chunk, kd_t[:, pair], 0.0).astype(BF16)
        d_states.append(_dot(kd, v_ref[pair, :]))

    st = state[...]
    states = []
    for ck in range(n_chunks):
        decay = jnp.exp(tot_t[:, ck * CHUNK:ck * CHUNK + 1])
        st = decay * st + d_states[ck]
        states.append(st.astype(BF16))
    state[...] = st

    o = jnp.concatenate([_dot(q_ref[ck * CHUNK:(ck + 1) * CHUNK, :], states[ck]) for ck in range(n_chunks)],
                        axis=0) * (GLA_KDIM ** -0.5)
    o_ref[...] = (_rms(o, g_ref[...]) * sgr_ref[...].astype(F32)).astype(BF16)


def _gla(mid, log_a, sgr, out_g, *, B, S, blk):
    T = B * S
    ns = S // blk
    kern = functools.partial(_gla_kernel, blk=blk)
    q_col0 = (DIFF_HEADS * DIFF_VDIM) // GLA_KDIM
    k_col0 = q_col0 + GLA_HEADS
    v_col0 = (DIFF_HEADS * DIFF_VDIM + 2 * GLA_HEADS * GLA_KDIM) // GLA_VDIM
    r = jnp.arange(blk, dtype=I32)[:, None]
    c = jnp.arange(blk, dtype=I32)[None, :]
    same = (r // CHUNK) == (c // CHUNK)
    tri = (same & (r <= c)).astype(BF16)
    ones = same.astype(BF16)
    return pl.pallas_call(
        kern,
        grid=(B, GLA_HEADS, ns),
        in_specs=[
            pl.BlockSpec((blk, GLA_KDIM), lambda b, h, s: (b * ns + s, q_col0 + h)),
            pl.BlockSpec((blk, GLA_KDIM), lambda b, h, s: (b * ns + s, k_col0 + h)),
            pl.BlockSpec((blk, GLA_VDIM), lambda b, h, s: (b * ns + s, v_col0 + h)),
            pl.BlockSpec((blk, GLA_KDIM), lambda b, h, s: (b * ns + s, h)),
            pl.BlockSpec((blk, GLA_VDIM), lambda b, h, s: (b * ns + s, h)),
            pl.BlockSpec((1, GLA_VDIM), lambda b, h, s: (0, 0)),
            pl.BlockSpec((blk, blk), lambda b, h, s: (0, 0)),
            pl.BlockSpec((blk, blk), lambda b, h, s: (0, 0)),
        ],
        out_specs=pl.BlockSpec((blk, GLA_VDIM), lambda b, h, s: (b * ns + s, h)),
        out_shape=jax.ShapeDtypeStruct((T, GLA_HEADS * GLA_VDIM), BF16),
        scratch_shapes=[pltpu.VMEM((GLA_KDIM, GLA_VDIM), F32)],
        compiler_params=_cparams(("parallel", "parallel", "arbitrary"), 32),
        name="gla",
    )(mid, mid, mid, log_a, sgr, out_g, tri, ones)


def _resident_w_map(n_j):
    return lambda i, j: (0, jnp.where(i == 0, j, n_j - 1))


def _outproj_kernel(a_ref, b_ref, wa_ref, wb_ref, x_ref, o_ref, w_scr):
    j = pl.program_id(1)
    ka = a_ref.shape[1]

    @pl.when(pl.program_id(0) == 0)
    def _():
        w_scr[j, 0:ka, :] = wa_ref[...].astype(BF16)
        w_scr[j, ka:, :] = wb_ref[...].astype(BF16)

    acc = _dot(a_ref[...], w_scr[j, 0:ka, :]) + _dot(b_ref[...], w_scr[j, ka:, :])
    o_ref[...] = x_ref[...] + acc


def _outproj(a, b, w_out, x2, *, tm, tn):
    T, ka = a.shape
    kb = b.shape[1]
    assert ka == kb
    D = w_out.shape[1]
    n_j = D // tn
    return pl.pallas_call(
        _outproj_kernel,
        grid=(T // tm, n_j),
        in_specs=[
            pl.BlockSpec((tm, ka), lambda i, j: (i, 0)),
            pl.BlockSpec((tm, kb), lambda i, j: (i, 0)),
            pl.BlockSpec((ka, tn), _resident_w_map(n_j)),
            pl.BlockSpec((kb, tn), lambda i, j: (1, jnp.where(i == 0, j, n_j - 1))),
            pl.BlockSpec((tm, tn), lambda i, j: (i, j)),
        ],
        out_specs=pl.BlockSpec((tm, tn), lambda i, j: (i, j)),
        out_shape=jax.ShapeDtypeStruct((T, D), F32),
        scratch_shapes=[pltpu.VMEM((n_j, ka + kb, tn), BF16)],
        compiler_params=_cparams(("arbitrary", "arbitrary"), 48),
        name="outproj",
    )(a, b, w_out, w_out, x2)


def _normproj_kernel(x_ref, g_ref, w_ref, hg_ref, o_ref, n_scr, *w_scr, n_norm):
    j = pl.program_id(1)

    @pl.when(j == 0)
    def _():
        n_scr[...] = _rms(x_ref[...], g_ref[...]).astype(BF16)

    if w_scr:
        @pl.when(pl.program_id(0) == 0)
        def _():
            w_scr[0][j] = w_ref[...].astype(BF16)
        y = _dot(n_scr[...], w_scr[0][j])
    else:
        y = _dot(n_scr[...], w_ref[...].astype(BF16))

    @pl.when(j < n_norm)
    def _():
        o_ref[...] = _rms(y, hg_ref[...]).astype(BF16)

    @pl.when(j >= n_norm)
    def _():
        o_ref[...] = y.astype(BF16)


def _normproj(x2, g, w, head_g, *, tm, tn, n_norm, name):
    T, D = x2.shape
    N = w.shape[1]
    kern = functools.partial(_normproj_kernel, n_norm=n_norm)
    n_j = N // tn
    resident = T // tm > 1
    return pl.pallas_call(
        kern,
        grid=(T // tm, n_j),
        in_specs=[
            pl.BlockSpec((tm, D), lambda i, j: (i, 0)),
            pl.BlockSpec((1, D), lambda i, j: (0, 0)),
            pl.BlockSpec((D, tn), _resident_w_map(n_j) if resident else (lambda i, j: (0, j))),
            pl.BlockSpec((1, tn), lambda i, j: (0, 0)),
        ],
        out_specs=pl.BlockSpec((tm, tn), lambda i, j: (i, j)),
        out_shape=jax.ShapeDtypeStruct((T, N), BF16),
        scratch_shapes=[pltpu.VMEM((tm, D), BF16)] + ([pltpu.VMEM((n_j, D, tn), BF16)] if resident else []),
        compiler_params=_cparams(("arbitrary", "arbitrary"), 48),
        name=name,
    )(x2, g, w, head_g)


def _cross_kernel(q_ref, k_ref, v_ref, w_ref, h_ref, o_ref, att_scr, w_scr, *, hdim):
    j = pl.program_id(1)

    @pl.when(pl.program_id(0) == 0)
    def _():
        w_scr[j] = w_ref[...].astype(BF16)

    @pl.when(j == 0)
    def _():
        for hd in range(CROSS_HEADS):
            cols = slice(hd * hdim, (hd + 1) * hdim)
            s = lax.dot_general(q_ref[:, cols], k_ref[:, cols], NT_DIMS, preferred_element_type=F32)
            p = jnp.exp(s - jnp.max(s, axis=-1, keepdims=True))
            l = jnp.sum(p, axis=-1, keepdims=True)
            att_scr[:, cols] = (_dot(p.astype(BF16), v_ref[:, cols]) / l).astype(BF16)

    o_ref[...] = h_ref[...] + _dot(att_scr[...], w_scr[j])


def _cross(qc, kv, w_co, h1, *, S, n_mem, tm, tn):
    T, D = qc.shape
    per_b = S // tm
    kern = functools.partial(_cross_kernel, hdim=D // CROSS_HEADS)
    n_j = D // tn
    return pl.pallas_call(
        kern,
        grid=(T // tm, n_j),
        in_specs=[
            pl.BlockSpec((tm, D), lambda i, j: (i, 0)),
            pl.BlockSpec((n_mem, D), lambda i, j: (i // per_b, 0)),
            pl.BlockSpec((n_mem, D), lambda i, j: (i // per_b, 1)),
            pl.BlockSpec((D, tn), _resident_w_map(n_j)),
            pl.BlockSpec((tm, tn), lambda i, j: (i, j)),
        ],
        out_specs=pl.BlockSpec((tm, tn), lambda i, j: (i, j)),
        out_shape=jax.ShapeDtypeStruct((T, D), F32),
        scratch_shapes=[pltpu.VMEM((tm, D), BF16), pltpu.VMEM((n_j, D, tn), BF16)],
        compiler_params=_cparams(("arbitrary", "arbitrary"), 48),
        name="cross",
    )(qc, kv, kv, w_co, h1)


R_ROWS = SUBLANES + N_EXPERTS


def _pack_bf16_pairs(xb16):
    c = xb16.shape[1] // 2
    u = lax.bitcast_convert_type(xb16.astype(F32), jnp.uint32)
    return (u[:, :c] >> 16) | (u[:, c:] & jnp.uint32(0xFFFF0000))


def _store_row_tiles(ref, x):
    for g in range(SUBLANES):
        ref[:, g, :] = x[:, g * LANES:(g + 1) * LANES]


def _load_row_tiles(ref):
    return jnp.concatenate([ref[:, g, :] for g in range(SUBLANES)], axis=1)


def _unpack_bf16_pairs_f32(w):
    lo = lax.bitcast_convert_type(w << 16, F32)
    hi = lax.bitcast_convert_type(w & jnp.uint32(0xFFFF0000), F32)
    return lo, hi


def _unpack_bf16_pairs(w):
    lo, hi = _unpack_bf16_pairs_f32(w)
    return lo.astype(BF16), hi.astype(BF16)


def _router_kernel(h_ref, g_ref, wt_ref, b_ref, eid_ref, gate_ref, xn_ref):
    n = _rms(h_ref[...], g_ref[...])
    nh, nl = _split_bf16(n)
    _store_row_tiles(xn_ref, _pack_bf16_pairs(nh))
    wh, wl = _split_bf16(wt_ref[...])
    nt = functools.partial(lax.dot_general, dimension_numbers=NT_DIMS, preferred_element_type=F32)
    lg = nt(wh, nh) + nt(wh, nl) + nt(wl, nh) + b_ref[...]

    tm = lg.shape[1]
    row = lax.broadcasted_iota(I32, (SUBLANES, tm), 0)

    def first_argmax(v, vmax):
        return jnp.min(jnp.where(v == vmax, row, SUBLANES), axis=0, keepdims=True)

    gl = jnp.where(row < N_GROUPS, lg[0:SUBLANES], NEG_INF)
    gmax = jnp.max(gl, axis=0, keepdims=True)
    grp = first_argmax(gl, gmax)
    grp_w = 1.0 / jnp.sum(jnp.exp(gl - gmax), axis=0, keepdims=True)

    sel = jnp.zeros((SUBLANES, tm), F32)
    for gi in range(N_GROUPS):
        lo = SUBLANES + gi * EXPERTS_PER_GROUP
        sel = jnp.where(grp == gi, lg[lo:lo + EXPERTS_PER_GROUP], sel)
    e = jnp.exp(sel - jnp.max(sel, axis=0, keepdims=True))
    prob = e / jnp.sum(e, axis=0, keepdims=True)
    p1 = jnp.max(prob, axis=0, keepdims=True)
    i1 = first_argmax(prob, p1)
    rest = jnp.where(row == i1, -1.0, prob)
    p2 = jnp.max(rest, axis=0, keepdims=True)
    i2 = first_argmax(rest, p2)
    den = p1 + p2
    base = grp * EXPERTS_PER_GROUP
    eid_ref[...] = jnp.where(row == 0, base + i1, jnp.where(row == 1, base + i2, 0))
    gate_ref[...] = jnp.where(row == 0, grp_w * p1 / den, jnp.where(row == 1, grp_w * p2 / den, 0.0))


def _router(h2, g_ffn, w_rt, b_r, *, tm):
    T, D = h2.shape
    return pl.pallas_call(
        _router_kernel,
        grid=(T // tm,),
        in_specs=[
            pl.BlockSpec((tm, D), lambda i: (i, 0)),
            pl.BlockSpec((1, D), lambda i: (0, 0)),
            pl.BlockSpec((R_ROWS, D), lambda i: (0, 0)),
            pl.BlockSpec((R_ROWS, 1), lambda i: (0, 0)),
        ],
        out_specs=[
            pl.BlockSpec((SUBLANES, tm), lambda i: (0, i)),
            pl.BlockSpec((SUBLANES, tm), lambda i: (0, i)),
            pl.BlockSpec((tm, SUBLANES, LANES), lambda i: (i, 0, 0)),
        ],
        out_shape=[
            jax.ShapeDtypeStruct((SUBLANES, T), I32),
            jax.ShapeDtypeStruct((SUBLANES, T), F32),
            jax.ShapeDtypeStruct((T, SUBLANES, LANES), jnp.uint32),
        ],
        compiler_params=_cparams(("parallel",), 32),
        name="router",
    )(h2, g_ffn, w_rt, b_r)


def _rank_kernel(eid_ref, rank_ref, cnt_ref, carry):
    @pl.when(pl.program_id(0) == 0)
    def _():
        carry[...] = jnp.zeros(carry.shape, F32)

    tm = eid_ref.shape[1]
    e0 = eid_ref[0:1, :]
    e1 = eid_ref[1:2, :]
    erow = lax.broadcasted_iota(I32, (N_EXPERTS, tm), 0)
    hit = jnp.where((erow == e0) | (erow == e1), 1.0, 0.0)
    r = lax.broadcasted_iota(I32, (tm, tm), 0)
    c = lax.broadcasted_iota(I32, (tm, tm), 1)
    before = jnp.where(r < c, 1.0, 0.0).astype(BF16)
    pre = _dot(hit.astype(BF16), before) + carry[:, 0:1]
    rank0 = jnp.sum(jnp.where(erow == e0, pre, 0.0), axis=0, keepdims=True)
    rank1 = jnp.sum(jnp.where(erow == e1, pre, 0.0), axis=0, keepdims=True)
    row = lax.broadcasted_iota(I32, (SUBLANES, tm), 0)
    rank_ref[...] = jnp.where(row == 0, rank0, jnp.where(row == 1, rank1, 0.0)).astype(I32)
    total = carry[...] + jnp.sum(hit, axis=1, keepdims=True)
    carry[...] = total
    cnt_ref[...] = total.astype(I32)


def _rank(eid, *, tm):
    T = eid.shape[1]
    return pl.pallas_call(
        _rank_kernel,
        grid=(T // tm,),
        in_specs=[pl.BlockSpec((SUBLANES, tm), lambda i: (0, i))],
        out_specs=[
            pl.BlockSpec((SUBLANES, tm), lambda i: (0, i)),
            pl.BlockSpec((N_EXPERTS, LANES), lambda i: (0, 0)),
        ],
        out_shape=[
            jax.ShapeDtypeStruct((SUBLANES, T), I32),
            jax.ShapeDtypeStruct((N_EXPERTS, LANES), I32),
        ],
        scratch_shapes=[pltpu.VMEM((N_EXPERTS, LANES), F32)],
        compiler_params=_cparams(("arbitrary",), 32),
        name="rank",
    )(eid)


def _dispatch_kernel(dest_ref, pad0_ref, npad_ref, nb_ref, xn_ref, xb_hbm, zbuf, sem, psem,
                     *, n_tok, tb, n_blocks):
    i = pl.program_id(0)
    tm = xn_ref.shape[0]
    base = i * tm

    def row_body(r, carry):
        for k in range(TOP_K):
            pltpu.make_async_copy(xn_ref.at[r], xb_hbm.at[dest_ref[k * n_tok + base + r]], sem).start(priority=k)
        return carry
    lax.fori_loop(0, tm, row_body, 0, unroll=8)

    @pl.when(i == 0)
    def _():
        _dispatch_fill(pad0_ref, npad_ref, nb_ref, xb_hbm, zbuf, psem, tb=tb, n_blocks=n_blocks)

    for k in range(TOP_K):
        pltpu.make_async_copy(xn_ref, xb_hbm.at[pl.ds(0, tm)], sem).wait()


def _dispatch_fill(pad0_ref, npad_ref, nb_ref, xb_hbm, zbuf, psem, *, tb, n_blocks):
    zbuf[...] = jnp.zeros(zbuf.shape, zbuf.dtype)

    def pad_copy(e):
        n = npad_ref[e]
        return pltpu.make_async_copy(zbuf.at[pl.ds(0, n)], xb_hbm.at[pl.ds(pad0_ref[e], n)], psem.at[0])

    def tail_copy(blk):
        return pltpu.make_async_copy(zbuf, xb_hbm.at[pl.ds(pl.multiple_of(blk * tb, tb), tb)], psem.at[1])

    def for_each_pad(fn):
        def body(e, c):
            @pl.when(npad_ref[e] > 0)
            def _():
                fn(e)
            return c
        lax.fori_loop(0, N_EXPERTS, body, 0)

    def for_each_tail(fn):
        def body(b, c):
            fn(b)
            return c
        lax.fori_loop(nb_ref[0], n_blocks, body, 0)

    for_each_pad(lambda e: pad_copy(e).start())
    for_each_tail(lambda b: tail_copy(b).start())
    for_each_pad(lambda e: pad_copy(e).wait())
    for_each_tail(lambda b: tail_copy(b).wait())


def _dispatch(dest, pad0, npad, nb, xn, *, tb, n_blocks, tm):
    T = xn.shape[0]
    tile = xn.shape[1:]
    kern = functools.partial(_dispatch_kernel, n_tok=T, tb=tb, n_blocks=n_blocks)
    grid_spec = pltpu.PrefetchScalarGridSpec(
        num_scalar_prefetch=4,
        grid=(T // tm,),
        in_specs=[pl.BlockSpec((tm,) + tile, lambda i, *_: (i, 0, 0))],
        out_specs=pl.BlockSpec(memory_space=pl.ANY),
        scratch_shapes=[
            pltpu.VMEM((tb,) + tile, xn.dtype),
            pltpu.SemaphoreType.DMA(()),
            pltpu.SemaphoreType.DMA((2,)),
        ],
    )
    return pl.pallas_call(
        kern,
        grid_spec=grid_spec,
        out_shape=jax.ShapeDtypeStruct((n_blocks * tb,) + tile, xn.dtype),
        compiler_params=_cparams(("arbitrary",), 32),
        name="dispatch",
    )(dest, pad0, npad, nb, xn)


def _expert_kernel(be_ref, first_ref, slot_ref, nxt_ref, run1_ref, nb_ref, x_ref, wg_hbm, wu_hbm, wd_hbm, y_ref,
                   wg_f, wu_f, wd_f, wsem, wg_b, wu_b, wd_b):
    i = pl.program_id(0)
    nb = nb_ref[0]

    def weight_copies(e, slot):
        return (pltpu.make_async_copy(wg_hbm.at[e], wg_f.at[slot], wsem.at[3 * slot]),
                pltpu.make_async_copy(wu_hbm.at[e], wu_f.at[slot], wsem.at[3 * slot + 1]),
                pltpu.make_async_copy(wd_hbm.at[e], wd_f.at[slot], wsem.at[3 * slot + 2]))

    @pl.when(i == 0)
    def _():
        for cp in weight_copies(be_ref[0], 0):
            cp.start()

    @pl.when((i == 0) & (run1_ref[0] >= 0))
    def _():
        for cp in weight_copies(jnp.maximum(run1_ref[0], 0), 1):
            cp.start()

    first = (i < nb) & (first_ref[i] == 1)
    slot = slot_ref[i]

    @pl.when(first)
    def _():
        for cp in weight_copies(0, slot):
            cp.wait()
        wg_b[...] = wg_f[slot].astype(BF16)
        wu_b[...] = wu_f[slot].astype(BF16)
        wd_b[...] = wd_f[slot].astype(BF16)

    @pl.when(first & (nxt_ref[i] >= 0))
    def _():
        for cp in weight_copies(jnp.maximum(nxt_ref[i], 0), slot):
            cp.start()

    @pl.when(i < nb)
    def _():
        n_lo, n_hi = _unpack_bf16_pairs(_load_row_tiles(x_ref))
        half = n_lo.shape[1]
        a = _dot(n_lo, wg_b[0:half, :]) + _dot(n_hi, wg_b[half:, :])
        u = _dot(n_lo, wu_b[0:half, :]) + _dot(n_hi, wu_b[half:, :])
        hdn = (a / (1.0 + jnp.exp(-a))) * u
        y_ref[...] = _pack_bf16_pairs(_dot(hdn.astype(BF16), wd_b[...]).astype(BF16))

    @pl.when(i >= nb)
    def _():
        y_ref[...] = jnp.zeros(y_ref.shape, y_ref.dtype)


def _experts(xb, blk_e, blk_first, blk_slot, blk_next, run1, nb, w_gate, w_up, w_down, *, tb, n_blocks):
    D, De = w_gate.shape[1:]
    assert xb.shape[1:] == (SUBLANES, LANES) and D == 2 * SUBLANES * LANES

    def x_map(i, be, first, slot, nxt, r1, nbr):
        return (jnp.minimum(i, nbr[0] - 1), 0, 0)

    grid_spec = pltpu.PrefetchScalarGridSpec(
        num_scalar_prefetch=6,
        grid=(n_blocks,),
        in_specs=[
            pl.BlockSpec((tb, SUBLANES, LANES), x_map),
            pl.BlockSpec(memory_space=pl.ANY),
            pl.BlockSpec(memory_space=pl.ANY),
            pl.BlockSpec(memory_space=pl.ANY),
        ],
        out_specs=pl.BlockSpec((tb, D // 2), lambda i, *_: (i, 0)),
        scratch_shapes=[
            pltpu.VMEM((2, D, De), F32),
            pltpu.VMEM((2, D, De), F32),
            pltpu.VMEM((2, De, D), F32),
            pltpu.SemaphoreType.DMA((6,)),
            pltpu.VMEM((D, De), BF16),
            pltpu.VMEM((D, De), BF16),
            pltpu.VMEM((De, D), BF16),
        ],
    )
    return pl.pallas_call(
        _expert_kernel,
        grid_spec=grid_spec,
        out_shape=jax.ShapeDtypeStruct((n_blocks * tb, D // 2), jnp.uint32),
        compiler_params=_cparams(("arbitrary",), 52),
        name="experts",
    )(blk_e, blk_first, blk_slot, blk_next, run1, nb, xb, w_gate, w_up, w_down)


def _combine_kernel(dest_ref, y_hbm, h_ref, gate_ref, o_ref, ybuf, sem, *, tm, n_tok):
    i = pl.program_id(0)
    n = pl.num_programs(0)

    def row_copy(d, k, r, slot):
        return pltpu.make_async_copy(y_hbm.at[pl.ds(d, 1), :], ybuf.at[slot, k, pl.ds(r, 1), :], sem.at[slot])

    def start_gather(blk, slot):
        def body(r, carry):
            for k in range(TOP_K):
                row_copy(dest_ref[k * n_tok + blk * tm + r], k, r, slot).start(priority=k)
            return carry
        lax.fori_loop(0, tm, body, 0, unroll=8)

    def wait_gather(slot):
        for k in range(TOP_K):
            pltpu.make_async_copy(y_hbm.at[pl.ds(0, tm), :], ybuf.at[slot, k], sem.at[slot]).wait()

    @pl.when(i == 0)
    def _():
        start_gather(0, 0)

    @pl.when(i + 1 < n)
    def _():
        start_gather(i + 1, (i + 1) % 2)

    slot = i % 2
    wait_gather(slot)
    gt = gate_ref[...]
    half = h_ref.shape[1] // 2
    y0_lo, y0_hi = _unpack_bf16_pairs_f32(ybuf[slot, 0])
    y1_lo, y1_hi = _unpack_bf16_pairs_f32(ybuf[slot, 1])
    o_ref[:, 0:half] = h_ref[:, 0:half] + gt[:, 0:1] * y0_lo + gt[:, 1:2] * y1_lo
    o_ref[:, half:] = h_ref[:, half:] + gt[:, 0:1] * y0_hi + gt[:, 1:2] * y1_hi


def _combine(dest, yb, h2, gate, *, tm):
    T, D = h2.shape
    kern = functools.partial(_combine_kernel, tm=tm, n_tok=T)
    grid_spec = pltpu.PrefetchScalarGridSpec(
        num_scalar_prefetch=1,
        grid=(T // tm,),
        in_specs=[
            pl.BlockSpec(memory_space=pl.ANY),
            pl.BlockSpec((tm, D), lambda i, d: (i, 0)),
            pl.BlockSpec((tm, TOP_K), lambda i, d: (i, 0)),
        ],
        out_specs=pl.BlockSpec((tm, D), lambda i, d: (i, 0)),
        scratch_shapes=[
            pltpu.VMEM((2, TOP_K, tm, D // 2), jnp.uint32),
            pltpu.SemaphoreType.DMA((2,)),
        ],
    )
    return pl.pallas_call(
        kern,
        grid_spec=grid_spec,
        out_shape=jax.ShapeDtypeStruct((T, D), F32),
        compiler_params=_cparams(("arbitrary",), 40),
        name="combine",
    )(dest, yb, h2, gate)


class _Tiles(NamedTuple):
    proj_rows: int = 1024
    proj_cols: int = 512
    attn_rows: int = 512
    gla_rows: int = 512
    token_rows: int = 512
    expert_rows: int = 256


TILES = _Tiles()


def kernel(x, mem, positions, g_attn, w_in, q_norm_g, k_norm_g, lambda_q1, lambda_k1, lambda_q2, lambda_k2, diff_subln_g, gla_w_a2, gla_b_a, gla_out_g, w_out, g_cross, g_mem, w_cq, w_ckv, cq_norm_g, ck_norm_g, w_co, g_ffn, w_router_grp, b_router_grp, w_router_exp, b_router_exp, w_gate, w_up, w_down):
    B, S, D = x.shape
    T = B * S
    n_mem = mem.shape[1]
    l = 0
    x2 = x.reshape(T, D)

    half = DIFF_QKDIM // 2
    freq = ROPE_THETA ** (-jnp.arange(half, dtype=F32) / half)
    freq = jnp.tile(freq, LANES // half)[None, :]
    q_scale = math.log2(math.e) * DIFF_QKDIM ** -0.5
    qkg = jnp.stack([jnp.tile(q_norm_g[l], 2) * q_scale, jnp.tile(k_norm_g[l], 2)])
    score_bound = 1.01 * DIFF_QKDIM * q_scale * jnp.max(jnp.abs(q_norm_g[l])) * jnp.max(jnp.abs(k_norm_g[l]))
    lvec = jnp.stack([lambda_q1[l], lambda_k1[l], lambda_q2[l], lambda_k2[l]])

    qk, mid, log_a, sgr = _inproj(x2, g_attn[l][None], positions.reshape(T, 1), freq, qkg, w_in[l].T,
                                  gla_w_a2[l], gla_b_a[l][None], tm=TILES.proj_rows)
    diffattn = functools.partial(_diffattn, lvec, qk, mid, diff_subln_g[l][None], B=B, S=S, tq=TILES.attn_rows)
    mix_d = lax.cond(score_bound <= SCORE_BOUND,
                     functools.partial(diffattn, bounded=True), functools.partial(diffattn, bounded=False))
    mix_g = _gla(mid, log_a, sgr, gla_out_g[l][None], B=B, S=S, blk=TILES.gla_rows)
    h1 = _outproj(mix_d, mix_g, w_out[l], x2, tm=TILES.proj_rows, tn=TILES.proj_cols)

    hdim = D // CROSS_HEADS
    qc = _normproj(h1, g_cross[l][None], w_cq[l], cq_norm_g[l][None] * (hdim ** -0.5),
                   tm=TILES.proj_rows, tn=hdim, n_norm=CROSS_HEADS, name="cq")
    kv = _normproj(mem.reshape(B * n_mem, D), g_mem[l][None], w_ckv[l], ck_norm_g[l][None],
                   tm=B * n_mem, tn=hdim, n_norm=CROSS_HEADS, name="ckv")
    h2 = _cross(qc, kv, w_co[l], h1, S=S, n_mem=n_mem, tm=TILES.proj_rows, tn=TILES.proj_cols)

    w_rt = jnp.concatenate([w_router_grp[l].T, jnp.zeros((SUBLANES - N_GROUPS, D), F32), w_router_exp[l].T])
    b_r = jnp.concatenate([b_router_grp[l], jnp.zeros((SUBLANES - N_GROUPS,), F32), b_router_exp[l]])[:, None]
    eid, gate, xn = _router(h2, g_ffn[l][None], w_rt, b_r, tm=TILES.token_rows)
    rank, cnt = _rank(eid, tm=TILES.token_rows)

    assert TOP_K == 2
    tb = TILES.expert_rows
    n_blocks = (T * TOP_K + N_EXPERTS * (tb - 1) + tb - 1) // tb
    counts = cnt[:, 0]
    pcounts = ((counts + tb - 1) // tb) * tb
    pends = jnp.cumsum(pcounts)
    pstarts = pends - pcounts
    nb = (pends[-1:] // tb).astype(I32)
    eids = jnp.arange(N_EXPERTS, dtype=I32)
    blk_start = jnp.arange(n_blocks, dtype=I32) * tb
    blk_e = jnp.minimum(jnp.sum(pends[None, :] <= blk_start[:, None], axis=1), N_EXPERTS - 1).astype(I32)
    blk_first = jnp.concatenate([jnp.ones((1,), I32), (blk_e[1:] != blk_e[:-1]).astype(I32)])
    used = jnp.where(counts > 0, eids, N_EXPERTS)
    next_used = jnp.concatenate([lax.cummin(used[::-1])[::-1][1:], jnp.full((1,), N_EXPERTS, I32)])
    next_used = jnp.where(next_used < N_EXPERTS, next_used, -1)

    def table_at(idx, table):
        return jnp.where(idx >= 0, jnp.sum(jnp.where(idx[:, None] == eids, table, 0), axis=1), -1).astype(I32)

    after_next = table_at(next_used, next_used)
    run_of = (jnp.cumsum((counts > 0).astype(I32)) - 1).astype(I32)
    blk_next = table_at(blk_e, after_next)
    blk_slot = table_at(blk_e, run_of % 2)
    run1 = table_at(blk_e[:1], next_used)
    pick = eid[:TOP_K]
    pstart_of = jnp.sum(jnp.where(pick[..., None] == eids, pstarts, 0), axis=-1)
    dest = (pstart_of + rank[:TOP_K]).astype(I32).reshape(-1)
    pad0 = (pstarts + counts).astype(I32)
    npad = (pcounts - counts).astype(I32)

    xb = _dispatch(dest, pad0, npad, nb, xn, tb=tb, n_blocks=n_blocks, tm=TILES.token_rows)
    yb = _experts(xb, blk_e, blk_first, blk_slot, blk_next, run1, nb, w_gate[l], w_up[l], w_down[l],
                  tb=tb, n_blocks=n_blocks)
    out = _combine(dest, yb, h2, gate[:TOP_K].T, tm=TILES.token_rows)
    return out.reshape(B, S, D)
```

```python
import functools
import math
from typing import NamedTuple

import jax
import jax.numpy as jnp
from jax import lax
from jax.experimental import pallas as pl
from jax.experimental.pallas import tpu as pltpu

F32 = jnp.float32
BF16 = jnp.bfloat16
I32 = jnp.int32

LANES = 128
SUBLANES = 8

CHUNK = 64
ROPE_THETA = 10000.0
NORM_EPS = 1e-6
NEG_INF = -1e30
DIFF_HEADS = 8
DIFF_VDIM = 128
DIFF_QKDIM = 64
GLA_HEADS = 4
GLA_VDIM = 256
GLA_KDIM = 128
GLA_GATE_RANK = 16
GLA_TAU = 16.0
CROSS_HEADS = 4
N_GROUPS = 4
EXPERTS_PER_GROUP = 8
N_EXPERTS = N_GROUPS * EXPERTS_PER_GROUP
TOP_K = 2
LAM_INIT = 0.8 - 0.6 * math.exp(-0.3 * 0)

NT_DIMS = (((1,), (1,)), ((), ()))


def _cparams(semantics, vmem_mib):
    return pltpu.CompilerParams(dimension_semantics=semantics,
                                vmem_limit_bytes=vmem_mib * 1024 * 1024)


def _dot(a, b):
    return jnp.dot(a, b, preferred_element_type=F32)


def _dot_nt(a, b):
    return lax.dot_general(a, b, NT_DIMS, preferred_element_type=F32)


def _rms(x, g):
    ms = jnp.mean(x * x, axis=-1, keepdims=True)
    return x * lax.rsqrt(ms + NORM_EPS) * g


def _split_bf16(x):
    hi = x.astype(BF16)
    lo = (x - hi.astype(F32)).astype(BF16)
    return hi, lo


TN = 512
J_QK = 4
J_MID = 6
J_LR = J_QK + J_MID
J_GR = J_LR + 1
N_J = J_GR + 2


def _inproj_kernel(x_ref, g_ref, pos_ref, freq_ref, qkg_ref, w_ref, wgr_ref, wlr_ref, wa2_ref, ba_ref,
                   qk_ref, mid_ref, loga_ref, sgr_ref, n_scr, cos_scr, sin_scr, y_scr):
    j = pl.program_id(1)

    @pl.when(j == 0)
    def _():
        n_scr[...] = _rms(x_ref[...], g_ref[...]).astype(BF16)
        ang = pos_ref[...].astype(F32) * freq_ref[...]
        cos_scr[...] = jnp.cos(ang)
        sin_scr[...] = jnp.sin(ang)

    def qk_epilogue(jq):
        y_prev = y_scr.at[jq % 2]
        lane = lax.broadcasted_iota(I32, (1, LANES), 1)
        low_seg = lane < DIFF_QKDIM
        first_half = (lane % DIFF_QKDIM) < (DIFF_QKDIM // 2)
        gain = qkg_ref[jq // (J_QK // 2):jq // (J_QK // 2) + 1, :]
        cos = cos_scr[...]
        sin = sin_scr[...]
        for c in range(TN // LANES):
            yb = y_prev[:, c * LANES:(c + 1) * LANES]
            y2 = yb * yb
            s_lo = jnp.sum(jnp.where(low_seg, y2, 0.0), axis=-1, keepdims=True)
            s_hi = jnp.sum(jnp.where(low_seg, 0.0, y2), axis=-1, keepdims=True)
            ms = jnp.where(low_seg, s_lo, s_hi) * (1.0 / DIFF_QKDIM)
            yn = yb * lax.rsqrt(ms + NORM_EPS) * gain
            rot = jnp.where(first_half,
                            -pltpu.roll(yn, LANES - DIFF_QKDIM // 2, 1),
                            pltpu.roll(yn, DIFF_QKDIM // 2, 1))
            qk_ref[:, c * LANES:(c + 1) * LANES] = (yn * cos + rot * sin).astype(BF16)

    for jq in range(J_QK + 1):
        @pl.when(j == jq)
        def _():
            y = _dot_nt(n_scr[...], w_ref[...].astype(BF16))
            if jq > 0:
                qk_epilogue(jq - 1)
            if jq < J_QK:
                y_scr[jq % 2] = y
            else:
                mid_ref[...] = y.astype(BF16)

    @pl.when((j > J_QK) & (j < J_LR))
    def _():
        mid_ref[...] = _dot_nt(n_scr[...], w_ref[...].astype(BF16)).astype(BF16)

    @pl.when(j == J_LR)
    def _():
        lr = _dot_nt(n_scr[...], wlr_ref[...].astype(BF16))
        z = _dot(lr.astype(BF16), wa2_ref[...].astype(BF16)) + ba_ref[...]
        log_sig = jnp.minimum(z, 0.0) - jnp.log(1.0 + jnp.exp(-jnp.abs(z)))
        loga_ref[...] = log_sig * (1.0 / GLA_TAU)

    @pl.when(j >= J_GR)
    def _():
        y = _dot_nt(n_scr[...], wgr_ref[...].astype(BF16))
        sgr_ref[...] = (y / (1.0 + jnp.exp(-y))).astype(BF16)


def _inproj(x2, g_attn, pos2, freq, qkg, w_t, w_a2, b_a, *, tm):
    T, D = x2.shape
    n_mid = J_MID * TN
    n_gk = GLA_HEADS * GLA_KDIM
    lr0 = J_LR * TN
    gr0 = lr0 + GLA_GATE_RANK
    n_gr = w_t.shape[0] - gr0
    assert n_gr == 2 * TN and lr0 % GLA_GATE_RANK == 0
    return pl.pallas_call(
        _inproj_kernel,
        grid=(T // tm, N_J),
        in_specs=[
            pl.BlockSpec((tm, D), lambda i, j: (i, 0)),
            pl.BlockSpec((1, D), lambda i, j: (0, 0)),
            pl.BlockSpec((tm, 1), lambda i, j: (i, 0)),
            pl.BlockSpec((1, LANES), lambda i, j: (0, 0)),
            pl.BlockSpec((2, LANES), lambda i, j: (0, 0)),
            pl.BlockSpec((TN, D), lambda i, j: (jnp.minimum(j, J_LR - 1), 0)),
            pl.BlockSpec((pl.Element(TN), pl.Element(D)),
                         lambda i, j: (pl.multiple_of(gr0 + TN * jnp.clip(j - J_GR, 0, 1), SUBLANES), 0)),
            pl.BlockSpec((GLA_GATE_RANK, D), lambda i, j: (lr0 // GLA_GATE_RANK, 0)),
            pl.BlockSpec((GLA_GATE_RANK, n_gk), lambda i, j: (0, 0)),
            pl.BlockSpec((1, n_gk), lambda i, j: (0, 0)),
        ],
        out_specs=[
            pl.BlockSpec((tm, TN), lambda i, j: (i, jnp.clip(j - 1, 0, J_QK - 1))),
            pl.BlockSpec((tm, TN), lambda i, j: (i, jnp.clip(j - J_QK, 0, J_MID - 1))),
            pl.BlockSpec((tm, n_gk), lambda i, j: (i, 0)),
            pl.BlockSpec((tm, TN), lambda i, j: (i, jnp.clip(j - J_GR, 0, 1))),
        ],
        out_shape=[
            jax.ShapeDtypeStruct((T, J_QK * TN), BF16),
            jax.ShapeDtypeStruct((T, n_mid), BF16),
            jax.ShapeDtypeStruct((T, n_gk), F32),
            jax.ShapeDtypeStruct((T, n_gr), BF16),
        ],
        scratch_shapes=[
            pltpu.VMEM((tm, D), BF16),
            pltpu.VMEM((tm, LANES), F32),
            pltpu.VMEM((tm, LANES), F32),
            pltpu.VMEM((2, tm, TN), F32),
        ],
        compiler_params=_cparams(("parallel", "arbitrary"), 56),
        name="inproj",
    )(x2, g_attn, pos2, freq, qkg, w_t, w_t, w_t, w_a2, b_a)


SCORE_BOUND = 80.0


def _diffattn_kernel(ti_ref, tj_ref, lv_ref, q_ref, k_ref, v_ref, sg_ref, o_ref,
                     vext, diag_mask, acc1, acc2, m1, m2, *, tq, n_tiles, bounded):
    S = q_ref.shape[0]
    nq = S // tq
    half = tq // 2
    unroll = 7
    diag_unroll = 4 if nq % 4 == 0 else 2

    vext[:, 0:DIFF_VDIM] = v_ref[...]
    vext[:, DIFF_VDIM:] = jnp.ones((S, DIFF_VDIM), BF16)
    row_chunk = lax.broadcasted_iota(I32, (tq, tq), 0) // CHUNK
    col_chunk = lax.broadcasted_iota(I32, (tq, tq), 1) // CHUNK
    diag_mask[...] = jnp.where(col_chunk <= row_chunk, 1.0, 0.0).astype(BF16)

    lane = lax.broadcasted_iota(I32, (1, LANES), 1)

    def block(i):
        return pl.ds(pl.multiple_of(i * tq, tq), tq)

    def q_comps(i):
        q = q_ref[block(i), :]
        zero = jnp.zeros_like(q)
        return jnp.where(lane < DIFF_QKDIM, q, zero), jnp.where(lane < DIFF_QKDIM, zero, q)

    def diag_tile(i):
        q1, q2 = q_comps(i)
        for qc, acc, m in ((q1, acc1, m1), (q2, acc2, m2)):
            for lo, n_keys in ((0, half), (half, tq)):
                rows = pl.ds(pl.multiple_of(i * tq + lo, half), half)
                keys = pl.ds(pl.multiple_of(i * tq, tq), n_keys)
                mask = diag_mask[lo:lo + half, 0:n_keys]
                s = _dot_nt(qc[lo:lo + half], k_ref[keys, :])
                if bounded:
                    acc[rows, :] = _dot(jnp.exp2(s).astype(BF16) * mask, vext[keys, :])
                else:
                    s = jnp.where(mask > 0, s, NEG_INF)
                    m_new = jnp.max(s, axis=-1, keepdims=True)
                    acc[rows, :] = _dot(jnp.exp2(s - m_new).astype(BF16), vext[keys, :])
                    m[rows, :] = m_new

    def full_tile(i, j):
        q1, q2 = q_comps(i)
        rows = block(i)
        k = k_ref[block(j), :]
        v = vext[block(j), :]
        for qc, acc, m in ((q1, acc1, m1), (q2, acc2, m2)):
            s = _dot_nt(qc, k)
            if bounded:
                acc[rows, :] += _dot(jnp.exp2(s).astype(BF16), v)
            else:
                m_old = m[rows, :]
                m_new = jnp.maximum(m_old, jnp.max(s, axis=-1, keepdims=True))
                p = jnp.exp2(s - m_new)
                acc[rows, :] = jnp.exp2(m_old - m_new) * acc[rows, :] + _dot(p.astype(BF16), v)
                m[rows, :] = m_new

    def diag_body(t, carry):
        for u in range(diag_unroll):
            diag_tile(diag_unroll * t + u)
        return carry
    lax.fori_loop(0, nq // diag_unroll, diag_body, 0)

    def full_body(t, carry):
        for u in range(unroll):
            full_tile(ti_ref[unroll * t + u], tj_ref[unroll * t + u])
        return carry
    lax.fori_loop(0, n_tiles // unroll, full_body, 0)
    for t in range(n_tiles - n_tiles % unroll, n_tiles):
        full_tile(ti_ref[t], tj_ref[t])

    lv = lv_ref[...]
    lam = (jnp.exp(jnp.sum(lv[0:1] * lv[1:2], axis=-1, keepdims=True))
           - jnp.exp(jnp.sum(lv[2:3] * lv[3:4], axis=-1, keepdims=True)) + LAM_INIT)

    def out_body(i, carry):
        rows = block(i)
        a1 = acc1[rows, :]
        a2 = acc2[rows, :]
        o = a1[:, :DIFF_VDIM] / a1[:, DIFF_VDIM:] - lam * (a2[:, :DIFF_VDIM] / a2[:, DIFF_VDIM:])
        o_ref[rows, :] = (_rms(o, sg_ref[...]) * (1.0 - LAM_INIT)).astype(BF16)
        return carry
    lax.fori_loop(0, nq, out_body, 0)


def _diffattn(lvec, qk, mid, subln_g, *, B, S, tq, bounded):
    T = B * S
    nq = S // tq
    assert nq % 2 == 0
    tiles = [(i, j) for i in range(nq) for j in range(i)]
    ti = jnp.asarray([t[0] for t in tiles], I32)
    tj = jnp.asarray([t[1] for t in tiles], I32)
    kern = functools.partial(_diffattn_kernel, tq=tq, n_tiles=len(tiles), bounded=bounded)
    m_rows = SUBLANES if bounded else S
    grid_spec = pltpu.PrefetchScalarGridSpec(
        num_scalar_prefetch=2,
        grid=(B, DIFF_HEADS),
        in_specs=[
            pl.BlockSpec((4, DIFF_QKDIM), lambda b, h, *_: (0, 0)),
            pl.BlockSpec((S, LANES), lambda b, h, *_: (b, h)),
            pl.BlockSpec((S, LANES), lambda b, h, *_: (b, DIFF_HEADS + h)),
            pl.BlockSpec((S, LANES), lambda b, h, *_: (b, h)),
            pl.BlockSpec((1, DIFF_VDIM), lambda b, h, *_: (0, 0)),
        ],
        out_specs=pl.BlockSpec((S, DIFF_VDIM), lambda b, h, *_: (b, h)),
        scratch_shapes=[
            pltpu.VMEM((S, 2 * DIFF_VDIM), BF16),
            pltpu.VMEM((tq, tq), BF16),
            pltpu.VMEM((S, 2 * DIFF_VDIM), F32),
            pltpu.VMEM((S, 2 * DIFF_VDIM), F32),
            pltpu.VMEM((m_rows, 1), F32),
            pltpu.VMEM((m_rows, 1), F32),
        ],
    )
    return pl.pallas_call(
        kern,
        grid_spec=grid_spec,
        out_shape=jax.ShapeDtypeStruct((T, DIFF_HEADS * DIFF_VDIM), BF16),
        compiler_params=_cparams(("parallel", "parallel"), 40),
        name="diffattn_bounded" if bounded else "diffattn_online",
    )(ti, tj, lvec, qk, qk, mid, subln_g)


def _gla_kernel(q_ref, k_ref, v_ref, la_ref, sgr_ref, g_ref, scan_ref, o_ref, state, *, blk):
    @pl.when(pl.program_id(2) == 0)
    def _():
        state[...] = jnp.zeros(state.shape, F32)

    la_t = la_ref[...].T
    k_t = k_ref[...].astype(F32).T
    hi, lo = _split_bf16(la_t)
    scan = scan_ref[...]
    cums, tots = [], []
    for sb in range(blk // LANES):
        slab = slice(sb * LANES, (sb + 1) * LANES)
        r = _dot(hi[:, slab], scan) + _dot(lo[:, slab], scan)
        cums.append(r[:, :LANES])
        tots.append(r[:, LANES:])
    cum_t = jnp.concatenate(cums, axis=1)
    tot_t = jnp.concatenate(tots, axis=1)
    kd_t = k_t * jnp.exp(tot_t - cum_t)

    n_chunks = blk // CHUNK
    lane = lax.broadcasted_iota(I32, (1, LANES), 1)
    d_states = []
    for ck in range(n_chunks):
        pair = slice((ck // 2) * LANES, (ck // 2 + 1) * LANES)
        in_chunk = (lane // CHUNK) == (ck % 2)
        kd = jnp.where(in_chunk, kd_t[:, pair], 0.0).astype(BF16)
        d_states.append(_dot(kd, v_ref[pair, :]))

    st = state[...]
    states = []
    for ck in range(n_chunks):
        decay = jnp.exp(tot_t[:, ck * CHUNK:ck * CHUNK + 1])
        st = decay * st + d_states[ck]
        states.append(st.astype(BF16))
    state[...] = st

    o = jnp.concatenate([_dot(q_ref[ck * CHUNK:(ck + 1) * CHUNK, :], states[ck]) for ck in range(n_chunks)],
                        axis=0) * (GLA_KDIM ** -0.5)
    o_ref[...] = (_rms(o, g_ref[...]) * sgr_ref[...].astype(F32)).astype(BF16)


def _gla(mid, log_a, sgr, out_g, *, B, S, blk):
    T = B * S
    ns = S // blk
    kern = functools.partial(_gla_kernel, blk=blk)
    q_col0 = (DIFF_HEADS * DIFF_VDIM) // GLA_KDIM
    k_col0 = q_col0 + GLA_HEADS
    v_col0 = (DIFF_HEADS * DIFF_VDIM + 2 * GLA_HEADS * GLA_KDIM) // GLA_VDIM
    assert LANES % CHUNK == 0 and blk % LANES == 0
    r = jnp.arange(LANES, dtype=I32)[:, None]
    c = jnp.arange(LANES, dtype=I32)[None, :]
    same = (r // CHUNK) == (c // CHUNK)
    scan = jnp.concatenate([(same & (r <= c)).astype(BF16), same.astype(BF16)], axis=1)
    return pl.pallas_call(
        kern,
        grid=(B, GLA_HEADS, ns),
        in_specs=[
            pl.BlockSpec((blk, GLA_KDIM), lambda b, h, s: (b * ns + s, q_col0 + h)),
            pl.BlockSpec((blk, GLA_KDIM), lambda b, h, s: (b * ns + s, k_col0 + h)),
            pl.BlockSpec((blk, GLA_VDIM), lambda b, h, s: (b * ns + s, v_col0 + h)),
            pl.BlockSpec((blk, GLA_KDIM), lambda b, h, s: (b * ns + s, h)),
            pl.BlockSpec((blk, GLA_VDIM), lambda b, h, s: (b * ns + s, h)),
            pl.BlockSpec((1, GLA_VDIM), lambda b, h, s: (0, 0)),
            pl.BlockSpec((LANES, 2 * LANES), lambda b, h, s: (0, 0)),
        ],
        out_specs=pl.BlockSpec((blk, GLA_VDIM), lambda b, h, s: (b * ns + s, h)),
        out_shape=jax.ShapeDtypeStruct((T, GLA_HEADS * GLA_VDIM), BF16),
        scratch_shapes=[pltpu.VMEM((GLA_KDIM, GLA_VDIM), F32)],
        compiler_params=_cparams(("parallel", "parallel", "arbitrary"), 32),
        name="gla",
    )(mid, mid, mid, log_a, sgr, out_g, scan)


def _resident_w_map(n_j):
    return lambda i, j: (0, jnp.where(i == 0, j, n_j - 1))


def _outproj_kernel(a_ref, b_ref, wa_ref, wb_ref, x_ref, o_ref, w_scr):
    j = pl.program_id(1)
    ka = a_ref.shape[1]

    @pl.when(pl.program_id(0) == 0)
    def _():
        w_scr[j, 0:ka, :] = wa_ref[...].astype(BF16)
        w_scr[j, ka:, :] = wb_ref[...].astype(BF16)

    acc = _dot(a_ref[...], w_scr[j, 0:ka, :]) + _dot(b_ref[...], w_scr[j, ka:, :])
    o_ref[...] = x_ref[...] + acc


def _outproj(a, b, w_out, x2, *, tm, tn):
    T, ka = a.shape
    kb = b.shape[1]
    assert ka == kb
    D = w_out.shape[1]
    n_j = D // tn
    return pl.pallas_call(
        _outproj_kernel,
        grid=(T // tm, n_j),
        in_specs=[
            pl.BlockSpec((tm, ka), lambda i, j: (i, 0)),
            pl.BlockSpec((tm, kb), lambda i, j: (i, 0)),
            pl.BlockSpec((ka, tn), _resident_w_map(n_j)),
            pl.BlockSpec((kb, tn), lambda i, j: (1, jnp.where(i == 0, j, n_j - 1))),
            pl.BlockSpec((tm, tn), lambda i, j: (i, j)),
        ],
        out_specs=pl.BlockSpec((tm, tn), lambda i, j: (i, j)),
        out_shape=jax.ShapeDtypeStruct((T, D), F32),
        scratch_shapes=[pltpu.VMEM((n_j, ka + kb, tn), BF16)],
        compiler_params=_cparams(("arbitrary", "arbitrary"), 48),
        name="outproj",
    )(a, b, w_out, w_out, x2)


def _normproj_kernel(x_ref, g_ref, w_ref, hg_ref, o_ref, n_scr, *w_scr, n_norm):
    j = pl.program_id(1)

    @pl.when(j == 0)
    def _():
        n_scr[...] = _rms(x_ref[...], g_ref[...]).astype(BF16)

    if w_scr:
        @pl.when(pl.program_id(0) == 0)
        def _():
            w_scr[0][j] = w_ref[...].astype(BF16)
        y = _dot(n_scr[...], w_scr[0][j])
    else:
        y = _dot(n_scr[...], w_ref[...].astype(BF16))

    @pl.when(j < n_norm)
    def _():
        o_ref[...] = _rms(y, hg_ref[...]).astype(BF16)

    @pl.when(j >= n_norm)
    def _():
        o_ref[...] = y.astype(BF16)


def _normproj(x2, g, w, head_g, *, tm, tn, n_norm, name):
    T, D = x2.shape
    N = w.shape[1]
    kern = functools.partial(_normproj_kernel, n_norm=n_norm)
    n_j = N // tn
    resident = T // tm > 1
    return pl.pallas_call(
        kern,
        grid=(T // tm, n_j),
        in_specs=[
            pl.BlockSpec((tm, D), lambda i, j: (i, 0)),
            pl.BlockSpec((1, D), lambda i, j: (0, 0)),
            pl.BlockSpec((D, tn), _resident_w_map(n_j) if resident else (lambda i, j: (0, j))),
            pl.BlockSpec((1, tn), lambda i, j: (0, 0)),
        ],
        out_specs=pl.BlockSpec((tm, tn), lambda i, j: (i, j)),
        out_shape=jax.ShapeDtypeStruct((T, N), BF16),
        scratch_shapes=[pltpu.VMEM((tm, D), BF16)] + ([pltpu.VMEM((n_j, D, tn), BF16)] if resident else []),
        compiler_params=_cparams(("arbitrary", "arbitrary"), 48),
        name=name,
    )(x2, g, w, head_g)


def _cross_kernel(q_ref, k_ref, v_ref, w_ref, h_ref, o_ref, att_scr, w_scr, *, hdim):
    j = pl.program_id(1)

    @pl.when(pl.program_id(0) == 0)
    def _():
        w_scr[j] = w_ref[...].astype(BF16)

    @pl.when(j == 0)
    def _():
        for hd in range(CROSS_HEADS):
            cols = slice(hd * hdim, (hd + 1) * hdim)
            s = lax.dot_general(q_ref[:, cols], k_ref[:, cols], NT_DIMS, preferred_element_type=F32)
            p = jnp.exp(s - jnp.max(s, axis=-1, keepdims=True))
            l = jnp.sum(p, axis=-1, keepdims=True)
            att_scr[:, cols] = (_dot(p.astype(BF16), v_ref[:, cols]) / l).astype(BF16)

    o_ref[...] = h_ref[...] + _dot(att_scr[...], w_scr[j])


def _cross(qc, kv, w_co, h1, *, S, n_mem, tm, tn):
    T, D = qc.shape
    per_b = S // tm
    kern = functools.partial(_cross_kernel, hdim=D // CROSS_HEADS)
    n_j = D // tn
    return pl.pallas_call(
        kern,
        grid=(T // tm, n_j),
        in_specs=[
            pl.BlockSpec((tm, D), lambda i, j: (i, 0)),
            pl.BlockSpec((n_mem, D), lambda i, j: (i // per_b, 0)),
            pl.BlockSpec((n_mem, D), lambda i, j: (i // per_b, 1)),
            pl.BlockSpec((D, tn), _resident_w_map(n_j)),
            pl.BlockSpec((tm, tn), lambda i, j: (i, j)),
        ],
        out_specs=pl.BlockSpec((tm, tn), lambda i, j: (i, j)),
        out_shape=jax.ShapeDtypeStruct((T, D), F32),
        scratch_shapes=[pltpu.VMEM((tm, D), BF16), pltpu.VMEM((n_j, D, tn), BF16)],
        compiler_params=_cparams(("arbitrary", "arbitrary"), 48),
        name="cross",
    )(qc, kv, kv, w_co, h1)


R_ROWS = SUBLANES + N_EXPERTS


def _pack_bf16_pairs(xb16):
    c = xb16.shape[1] // 2
    u = lax.bitcast_convert_type(xb16.astype(F32), jnp.uint32)
    return (u[:, :c] >> 16) | (u[:, c:] & jnp.uint32(0xFFFF0000))


def _store_row_tiles(ref, x):
    for g in range(SUBLANES):
        ref[:, g, :] = x[:, g * LANES:(g + 1) * LANES]


def _load_row_tiles(ref):
    return jnp.concatenate([ref[:, g, :] for g in range(SUBLANES)], axis=1)


def _unpack_bf16_pairs_f32(w):
    lo = lax.bitcast_convert_type(w << 16, F32)
    hi = lax.bitcast_convert_type(w & jnp.uint32(0xFFFF0000), F32)
    return lo, hi


def _unpack_bf16_pairs(w):
    lo, hi = _unpack_bf16_pairs_f32(w)
    return lo.astype(BF16), hi.astype(BF16)


def _router_kernel(h_ref, g_ref, wt_ref, b_ref, eid_ref, gate_ref, xn_ref):
    n = _rms(h_ref[...], g_ref[...])
    nh, nl = _split_bf16(n)
    _store_row_tiles(xn_ref, _pack_bf16_pairs(nh))
    wh, wl = _split_bf16(wt_ref[...])
    nt = functools.partial(lax.dot_general, dimension_numbers=NT_DIMS, preferred_element_type=F32)
    lg = nt(wh, nh) + nt(wh, nl) + nt(wl, nh) + b_ref[...]

    tm = lg.shape[1]
    row = lax.broadcasted_iota(I32, (SUBLANES, tm), 0)

    def first_argmax(v, vmax):
        return jnp.min(jnp.where(v == vmax, row, SUBLANES), axis=0, keepdims=True)

    gl = jnp.where(row < N_GROUPS, lg[0:SUBLANES], NEG_INF)
    gmax = jnp.max(gl, axis=0, keepdims=True)
    grp = first_argmax(gl, gmax)
    grp_w = 1.0 / jnp.sum(jnp.exp(gl - gmax), axis=0, keepdims=True)

    sel = jnp.zeros((SUBLANES, tm), F32)
    for gi in range(N_GROUPS):
        lo = SUBLANES + gi * EXPERTS_PER_GROUP
        sel = jnp.where(grp == gi, lg[lo:lo + EXPERTS_PER_GROUP], sel)
    e = jnp.exp(sel - jnp.max(sel, axis=0, keepdims=True))
    prob = e / jnp.sum(e, axis=0, keepdims=True)
    p1 = jnp.max(prob, axis=0, keepdims=True)
    i1 = first_argmax(prob, p1)
    rest = jnp.where(row == i1, -1.0, prob)
    p2 = jnp.max(rest, axis=0, keepdims=True)
    i2 = first_argmax(rest, p2)
    den = p1 + p2
    base = grp * EXPERTS_PER_GROUP
    eid_ref[...] = jnp.where(row == 0, base + i1, jnp.where(row == 1, base + i2, 0))
    gate_ref[...] = jnp.where(row == 0, grp_w * p1 / den, jnp.where(row == 1, grp_w * p2 / den, 0.0))


def _router(h2, g_ffn, w_rt, b_r, *, tm):
    T, D = h2.shape
    return pl.pallas_call(
        _router_kernel,
        grid=(T // tm,),
        in_specs=[
            pl.BlockSpec((tm, D), lambda i: (i, 0)),
            pl.BlockSpec((1, D), lambda i: (0, 0)),
            pl.BlockSpec((R_ROWS, D), lambda i: (0, 0)),
            pl.BlockSpec((R_ROWS, 1), lambda i: (0, 0)),
        ],
        out_specs=[
            pl.BlockSpec((SUBLANES, tm), lambda i: (0, i)),
            pl.BlockSpec((SUBLANES, tm), lambda i: (0, i)),
            pl.BlockSpec((tm, SUBLANES, LANES), lambda i: (i, 0, 0)),
        ],
        out_shape=[
            jax.ShapeDtypeStruct((SUBLANES, T), I32),
            jax.ShapeDtypeStruct((SUBLANES, T), F32),
            jax.ShapeDtypeStruct((T, SUBLANES, LANES), jnp.uint32),
        ],
        compiler_params=_cparams(("parallel",), 32),
        name="router",
    )(h2, g_ffn, w_rt, b_r)


def _rank_kernel(eid_ref, rank_ref, cnt_ref, carry):
    @pl.when(pl.program_id(0) == 0)
    def _():
        carry[...] = jnp.zeros(carry.shape, F32)

    tm = eid_ref.shape[1]
    e0 = eid_ref[0:1, :]
    e1 = eid_ref[1:2, :]
    erow = lax.broadcasted_iota(I32, (N_EXPERTS, tm), 0)
    hit = jnp.where((erow == e0) | (erow == e1), 1.0, 0.0)
    r = lax.broadcasted_iota(I32, (tm, tm), 0)
    c = lax.broadcasted_iota(I32, (tm, tm), 1)
    before = jnp.where(r < c, 1.0, 0.0).astype(BF16)
    pre = _dot(hit.astype(BF16), before) + carry[:, 0:1]
    rank0 = jnp.sum(jnp.where(erow == e0, pre, 0.0), axis=0, keepdims=True)
    rank1 = jnp.sum(jnp.where(erow == e1, pre, 0.0), axis=0, keepdims=True)
    row = lax.broadcasted_iota(I32, (SUBLANES, tm), 0)
    rank_ref[...] = jnp.where(row == 0, rank0, jnp.where(row == 1, rank1, 0.0)).astype(I32)
    total = carry[...] + jnp.sum(hit, axis=1, keepdims=True)
    carry[...] = total
    cnt_ref[...] = total.astype(I32)


def _rank(eid, *, tm):
    T = eid.shape[1]
    return pl.pallas_call(
        _rank_kernel,
        grid=(T // tm,),
        in_specs=[pl.BlockSpec((SUBLANES, tm), lambda i: (0, i))],
        out_specs=[
            pl.BlockSpec((SUBLANES, tm), lambda i: (0, i)),
            pl.BlockSpec((N_EXPERTS, LANES), lambda i: (0, 0)),
        ],
        out_shape=[
            jax.ShapeDtypeStruct((SUBLANES, T), I32),
            jax.ShapeDtypeStruct((N_EXPERTS, LANES), I32),
        ],
        scratch_shapes=[pltpu.VMEM((N_EXPERTS, LANES), F32)],
        compiler_params=_cparams(("arbitrary",), 32),
        name="rank",
    )(eid)


def _dispatch_kernel(dest_ref, pad0_ref, npad_ref, nb_ref, xn_ref, xb_hbm, zbuf, sem, psem,
                     *, n_tok, tb, n_blocks):
    i = pl.program_id(0)
    tm = xn_ref.shape[0]
    base = i * tm

    def row_body(r, carry):
        for k in range(TOP_K):
            pltpu.make_async_copy(xn_ref.at[r], xb_hbm.at[dest_ref[k * n_tok + base + r]], sem).start(priority=k)
        return carry
    lax.fori_loop(0, tm, row_body, 0, unroll=8)

    @pl.when(i == 0)
    def _():
        _dispatch_fill(pad0_ref, npad_ref, nb_ref, xb_hbm, zbuf, psem, tb=tb, n_blocks=n_blocks)

    for k in range(TOP_K):
        pltpu.make_async_copy(xn_ref, xb_hbm.at[pl.ds(0, tm)], sem).wait()


def _dispatch_fill(pad0_ref, npad_ref, nb_ref, xb_hbm, zbuf, psem, *, tb, n_blocks):
    zbuf[...] = jnp.zeros(zbuf.shape, zbuf.dtype)

    def pad_copy(e):
        n = npad_ref[e]
        return pltpu.make_async_copy(zbuf.at[pl.ds(0, n)], xb_hbm.at[pl.ds(pad0_ref[e], n)], psem.at[0])

    def tail_copy(blk):
        return pltpu.make_async_copy(zbuf, xb_hbm.at[pl.ds(pl.multiple_of(blk * tb, tb), tb)], psem.at[1])

    def for_each_pad(fn):
        def body(e, c):
            @pl.when(npad_ref[e] > 0)
            def _():
                fn(e)
            return c
        lax.fori_loop(0, N_EXPERTS, body, 0)

    def for_each_tail(fn):
        def body(b, c):
            fn(b)
            return c
        lax.fori_loop(nb_ref[0], n_blocks, body, 0)

    for_each_pad(lambda e: pad_copy(e).start())
    for_each_tail(lambda b: tail_copy(b).start())
    for_each_pad(lambda e: pad_copy(e).wait())
    for_each_tail(lambda b: tail_copy(b).wait())


def _dispatch(dest, pad0, npad, nb, xn, *, tb, n_blocks, tm):
    T = xn.shape[0]
    tile = xn.shape[1:]
    kern = functools.partial(_dispatch_kernel, n_tok=T, tb=tb, n_blocks=n_blocks)
    grid_spec = pltpu.PrefetchScalarGridSpec(
        num_scalar_prefetch=4,
        grid=(T // tm,),
        in_specs=[pl.BlockSpec((tm,) + tile, lambda i, *_: (i, 0, 0))],
        out_specs=pl.BlockSpec(memory_space=pl.ANY),
        scratch_shapes=[
            pltpu.VMEM((tb,) + tile, xn.dtype),
            pltpu.SemaphoreType.DMA(()),
            pltpu.SemaphoreType.DMA((2,)),
        ],
    )
    return pl.pallas_call(
        kern,
        grid_spec=grid_spec,
        out_shape=jax.ShapeDtypeStruct((n_blocks * tb,) + tile, xn.dtype),
        compiler_params=_cparams(("arbitrary",), 32),
        name="dispatch",
    )(dest, pad0, npad, nb, xn)


def _expert_kernel(be_ref, first_ref, slot_ref, nxt_ref, run1_ref, nb_ref, x_ref, wg_hbm, wu_hbm, wd_hbm, y_ref,
                   wg_f, wu_f, wd_f, wsem, wg_b, wu_b, wd_b):
    i = pl.program_id(0)
    nb = nb_ref[0]

    def weight_copies(e, slot):
        return (pltpu.make_async_copy(wg_hbm.at[e], wg_f.at[slot], wsem.at[3 * slot]),
                pltpu.make_async_copy(wu_hbm.at[e], wu_f.at[slot], wsem.at[3 * slot + 1]),
                pltpu.make_async_copy(wd_hbm.at[e], wd_f.at[slot], wsem.at[3 * slot + 2]))

    @pl.when(i == 0)
    def _():
        for cp in weight_copies(be_ref[0], 0):
            cp.start()

    @pl.when((i == 0) & (run1_ref[0] >= 0))
    def _():
        for cp in weight_copies(jnp.maximum(run1_ref[0], 0), 1):
            cp.start()

    first = (i < nb) & (first_ref[i] == 1)
    slot = slot_ref[i]

    @pl.when(first)
    def _():
        for cp in weight_copies(0, slot):
            cp.wait()
        wg_b[...] = wg_f[slot].astype(BF16)
        wu_b[...] = wu_f[slot].astype(BF16)
        wd_b[...] = wd_f[slot].astype(BF16)

    @pl.when(first & (nxt_ref[i] >= 0))
    def _():
        for cp in weight_copies(jnp.maximum(nxt_ref[i], 0), slot):
            cp.start()

    @pl.when(i < nb)
    def _():
        n_lo, n_hi = _unpack_bf16_pairs(_load_row_tiles(x_ref))
        half = n_lo.shape[1]
        a = _dot(n_lo, wg_b[0:half, :]) + _dot(n_hi, wg_b[half:, :])
        u = _dot(n_lo, wu_b[0:half, :]) + _dot(n_hi, wu_b[half:, :])
        hdn = (a / (1.0 + jnp.exp(-a))) * u
        y_ref[...] = _pack_bf16_pairs(_dot(hdn.astype(BF16), wd_b[...]).astype(BF16))

    @pl.when(i >= nb)
    def _():
        y_ref[...] = jnp.zeros(y_ref.shape, y_ref.dtype)


def _experts(xb, blk_e, blk_first, blk_slot, blk_next, run1, nb, w_gate, w_up, w_down, *, tb, n_blocks):
    D, De = w_gate.shape[1:]
    assert xb.shape[1:] == (SUBLANES, LANES) and D == 2 * SUBLANES * LANES

    def x_map(i, be, first, slot, nxt, r1, nbr):
        return (jnp.minimum(i, nbr[0] - 1), 0, 0)

    grid_spec = pltpu.PrefetchScalarGridSpec(
        num_scalar_prefetch=6,
        grid=(n_blocks,),
        in_specs=[
            pl.BlockSpec((tb, SUBLANES, LANES), x_map),
            pl.BlockSpec(memory_space=pl.ANY),
            pl.BlockSpec(memory_space=pl.ANY),
            pl.BlockSpec(memory_space=pl.ANY),
        ],
        out_specs=pl.BlockSpec((tb, D // 2), lambda i, *_: (i, 0)),
        scratch_shapes=[
            pltpu.VMEM((2, D, De), F32),
            pltpu.VMEM((2, D, De), F32),
            pltpu.VMEM((2, De, D), F32),
            pltpu.SemaphoreType.DMA((6,)),
            pltpu.VMEM((D, De), BF16),
            pltpu.VMEM((D, De), BF16),
            pltpu.VMEM((De, D), BF16),
        ],
    )
    return pl.pallas_call(
        _expert_kernel,
        grid_spec=grid_spec,
        out_shape=jax.ShapeDtypeStruct((n_blocks * tb, D // 2), jnp.uint32),
        compiler_params=_cparams(("arbitrary",), 52),
        name="experts",
    )(blk_e, blk_first, blk_slot, blk_next, run1, nb, xb, w_gate, w_up, w_down)


def _combine_kernel(dest_ref, y_hbm, h_ref, gate_ref, o_ref, ybuf, sem, *, tm, n_tok):
    i = pl.program_id(0)
    n = pl.num_programs(0)

    def row_copy(d, k, r, slot):
        return pltpu.make_async_copy(y_hbm.at[pl.ds(d, 1), :], ybuf.at[slot, k, pl.ds(r, 1), :], sem.at[slot])

    def start_gather(blk, slot):
        def body(r, carry):
            for k in range(TOP_K):
                row_copy(dest_ref[k * n_tok + blk * tm + r], k, r, slot).start(priority=k)
            return carry
        lax.fori_loop(0, tm, body, 0, unroll=8)

    def wait_gather(slot):
        for k in range(TOP_K):
            pltpu.make_async_copy(y_hbm.at[pl.ds(0, tm), :], ybuf.at[slot, k], sem.at[slot]).wait()

    @pl.when(i == 0)
    def _():
        start_gather(0, 0)

    @pl.when(i + 1 < n)
    def _():
        start_gather(i + 1, (i + 1) % 2)

    slot = i % 2
    wait_gather(slot)
    gt = gate_ref[...]
    half = h_ref.shape[1] // 2
    y0_lo, y0_hi = _unpack_bf16_pairs_f32(ybuf[slot, 0])
    y1_lo, y1_hi = _unpack_bf16_pairs_f32(ybuf[slot, 1])
    o_ref[:, 0:half] = h_ref[:, 0:half] + gt[:, 0:1] * y0_lo + gt[:, 1:2] * y1_lo
    o_ref[:, half:] = h_ref[:, half:] + gt[:, 0:1] * y0_hi + gt[:, 1:2] * y1_hi


def _combine(dest, yb, h2, gate, *, tm):
    T, D = h2.shape
    kern = functools.partial(_combine_kernel, tm=tm, n_tok=T)
    grid_spec = pltpu.PrefetchScalarGridSpec(
        num_scalar_prefetch=1,
        grid=(T // tm,),
        in_specs=[
            pl.BlockSpec(memory_space=pl.ANY),
            pl.BlockSpec((tm, D), lambda i, d: (i, 0)),
            pl.BlockSpec((tm, TOP_K), lambda i, d: (i, 0)),
        ],
        out_specs=pl.BlockSpec((tm, D), lambda i, d: (i, 0)),
        scratch_shapes=[
            pltpu.VMEM((2, TOP_K, tm, D // 2), jnp.uint32),
            pltpu.SemaphoreType.DMA((2,)),
        ],
    )
    return pl.pallas_call(
        kern,
        grid_spec=grid_spec,
        out_shape=jax.ShapeDtypeStruct((T, D), F32),
        compiler_params=_cparams(("arbitrary",), 40),
        name="combine",
    )(dest, yb, h2, gate)


class _Tiles(NamedTuple):
    proj_rows: int = 1024
    proj_cols: int = 512
    attn_rows: int = 512
    gla_rows: int = 512
    token_rows: int = 512
    expert_rows: int = 256


TILES = _Tiles()


def kernel(x, mem, positions, g_attn, w_in, q_norm_g, k_norm_g, lambda_q1, lambda_k1, lambda_q2, lambda_k2, diff_subln_g, gla_w_a2, gla_b_a, gla_out_g, w_out, g_cross, g_mem, w_cq, w_ckv, cq_norm_g, ck_norm_g, w_co, g_ffn, w_router_grp, b_router_grp, w_router_exp, b_router_exp, w_gate, w_up, w_down):
    B, S, D = x.shape
    T = B * S
    n_mem = mem.shape[1]
    l = 0
    x2 = x.reshape(T, D)

    half = DIFF_QKDIM // 2
    freq = ROPE_THETA ** (-jnp.arange(half, dtype=F32) / half)
    freq = jnp.tile(freq, LANES // half)[None, :]
    q_scale = math.log2(math.e) * DIFF_QKDIM ** -0.5
    qkg = jnp.stack([jnp.tile(q_norm_g[l], 2) * q_scale, jnp.tile(k_norm_g[l], 2)])
    score_bound = 1.01 * DIFF_QKDIM * q_scale * jnp.max(jnp.abs(q_norm_g[l])) * jnp.max(jnp.abs(k_norm_g[l]))
    lvec = jnp.stack([lambda_q1[l], lambda_k1[l], lambda_q2[l], lambda_k2[l]])

    qk, mid, log_a, sgr = _inproj(x2, g_attn[l][None], positions.reshape(T, 1), freq, qkg, w_in[l].T,
                                  gla_w_a2[l], gla_b_a[l][None], tm=TILES.proj_rows)
    diffattn = functools.partial(_diffattn, lvec, qk, mid, diff_subln_g[l][None], B=B, S=S, tq=TILES.attn_rows)
    mix_d = lax.cond(score_bound <= SCORE_BOUND,
                     functools.partial(diffattn, bounded=True), functools.partial(diffattn, bounded=False))
    mix_g = _gla(mid, log_a, sgr, gla_out_g[l][None], B=B, S=S, blk=TILES.gla_rows)
    h1 = _outproj(mix_d, mix_g, w_out[l], x2, tm=TILES.proj_rows, tn=TILES.proj_cols)

    hdim = D // CROSS_HEADS
    qc = _normproj(h1, g_cross[l][None], w_cq[l], cq_norm_g[l][None] * (hdim ** -0.5),
                   tm=TILES.proj_rows, tn=hdim, n_norm=CROSS_HEADS, name="cq")
    kv = _normproj(mem.reshape(B * n_mem, D), g_mem[l][None], w_ckv[l], ck_norm_g[l][None],
                   tm=B * n_mem, tn=hdim, n_norm=CROSS_HEADS, name="ckv")
    h2 = _cross(qc, kv, w_co[l], h1, S=S, n_mem=n_mem, tm=TILES.proj_rows, tn=TILES.proj_cols)

    w_rt = jnp.concatenate([w_router_grp[l].T, jnp.zeros((SUBLANES - N_GROUPS, D), F32), w_router_exp[l].T])
    b_r = jnp.concatenate([b_router_grp[l], jnp.zeros((SUBLANES - N_GROUPS,), F32), b_router_exp[l]])[:, None]
    eid, gate, xn = _router(h2, g_ffn[l][None], w_rt, b_r, tm=TILES.token_rows)
    rank, cnt = _rank(eid, tm=TILES.token_rows)

    assert TOP_K == 2
    tb = TILES.expert_rows
    n_blocks = (T * TOP_K + N_EXPERTS * (tb - 1) + tb - 1) // tb
    counts = cnt[:, 0]
    pcounts = ((counts + tb - 1) // tb) * tb
    pends = jnp.cumsum(pcounts)
    pstarts = pends - pcounts
    nb = (pends[-1:] // tb).astype(I32)
    eids = jnp.arange(N_EXPERTS, dtype=I32)
    blk_start = jnp.arange(n_blocks, dtype=I32) * tb
    blk_e = jnp.minimum(jnp.sum(pends[None, :] <= blk_start[:, None], axis=1), N_EXPERTS - 1).astype(I32)
    blk_first = jnp.concatenate([jnp.ones((1,), I32), (blk_e[1:] != blk_e[:-1]).astype(I32)])
    used = jnp.where(counts > 0, eids, N_EXPERTS)
    next_used = jnp.concatenate([lax.cummin(used[::-1])[::-1][1:], jnp.full((1,), N_EXPERTS, I32)])
    next_used = jnp.where(next_used < N_EXPERTS, next_used, -1)

    def table_at(idx, table):
        return jnp.where(idx >= 0, jnp.sum(jnp.where(idx[:, None] == eids, table, 0), axis=1), -1).astype(I32)

    after_next = table_at(next_used, next_used)
    run_of = (jnp.cumsum((counts > 0).astype(I32)) - 1).astype(I32)
    blk_next = table_at(blk_e, after_next)
    blk_slot = table_at(blk_e, run_of % 2)
    run1 = table_at(blk_e[:1], next_used)
    pick = eid[:TOP_K]
    pstart_of = jnp.sum(jnp.where(pick[..., None] == eids, pstarts, 0), axis=-1)
    dest = (pstart_of + rank[:TOP_K]).astype(I32).reshape(-1)
    pad0 = (pstarts + counts).astype(I32)
    npad = (pcounts - counts).astype(I32)

    xb = _dispatch(dest, pad0, npad, nb, xn, tb=tb, n_blocks=n_blocks, tm=TILES.token_rows)
    yb = _experts(xb, blk_e, blk_first, blk_slot, blk_next, run1, nb, w_gate[l], w_up[l], w_down[l],
                  tb=tb, n_blocks=n_blocks)
    out = _combine(dest, yb, h2, gate[:TOP_K].T, tm=TILES.token_rows)
    return out.reshape(B, S, D)
```

```python
import functools
import math
from typing import NamedTuple

import jax
import jax.numpy as jnp
from jax import lax
from jax.experimental import pallas as pl
from jax.experimental.pallas import tpu as pltpu

F32 = jnp.float32
BF16 = jnp.bfloat16
I32 = jnp.int32

LANES = 128
SUBLANES = 8

CHUNK = 64
ROPE_THETA = 10000.0
NORM_EPS = 1e-6
NEG_INF = -1e30
DIFF_HEADS = 8
DIFF_VDIM = 128
DIFF_QKDIM = 64
GLA_HEADS = 4
GLA_VDIM = 256
GLA_KDIM = 128
GLA_GATE_RANK = 16
GLA_TAU = 16.0
CROSS_HEADS = 4
N_GROUPS = 4
EXPERTS_PER_GROUP = 8
N_EXPERTS = N_GROUPS * EXPERTS_PER_GROUP
TOP_K = 2
LAM_INIT = 0.8 - 0.6 * math.exp(-0.3 * 0)

NT_DIMS = (((1,), (1,)), ((), ()))


def _cparams(semantics, vmem_mib):
    return pltpu.CompilerParams(dimension_semantics=semantics,
                                vmem_limit_bytes=vmem_mib * 1024 * 1024)


def _dot(a, b):
    return jnp.dot(a, b, preferred_element_type=F32)


def _dot_nt(a, b):
    return lax.dot_general(a, b, NT_DIMS, preferred_element_type=F32)


def _rms(x, g):
    ms = jnp.mean(x * x, axis=-1, keepdims=True)
    return x * lax.rsqrt(ms + NORM_EPS) * g


def _split_bf16(x):
    hi = x.astype(BF16)
    lo = (x - hi.astype(F32)).astype(BF16)
    return hi, lo


TN = 512
J_QK = 4
J_MID = 6
J_LR = J_QK + J_MID
J_GR = J_LR + 1
N_J = J_GR + 2


def _inproj_kernel(x_ref, g_ref, pos_ref, freq_ref, qkg_ref, w_ref, wgr_ref, wlr_ref, wa2_ref, ba_ref,
                   qk_ref, mid_ref, loga_ref, sgr_ref, n_scr, cos_scr, sin_scr, y_scr):
    j = pl.program_id(1)

    @pl.when(j == 0)
    def _():
        n_scr[...] = _rms(x_ref[...], g_ref[...]).astype(BF16)
        ang = pos_ref[...].astype(F32) * freq_ref[...]
        cos_scr[...] = jnp.cos(ang)
        sin_scr[...] = jnp.sin(ang)

    def qk_epilogue(jq):
        y_prev = y_scr.at[jq % 2]
        lane = lax.broadcasted_iota(I32, (1, LANES), 1)
        low_seg = lane < DIFF_QKDIM
        first_half = (lane % DIFF_QKDIM) < (DIFF_QKDIM // 2)
        gain = qkg_ref[jq // (J_QK // 2):jq // (J_QK // 2) + 1, :]
        cos = cos_scr[...]
        sin = sin_scr[...]
        for c in range(TN // LANES):
            yb = y_prev[:, c * LANES:(c + 1) * LANES]
            y2 = yb * yb
            s_lo = jnp.sum(jnp.where(low_seg, y2, 0.0), axis=-1, keepdims=True)
            s_hi = jnp.sum(jnp.where(low_seg, 0.0, y2), axis=-1, keepdims=True)
            ms = jnp.where(low_seg, s_lo, s_hi) * (1.0 / DIFF_QKDIM)
            yn = yb * lax.rsqrt(ms + NORM_EPS) * gain
            rot = jnp.where(first_half,
                            -pltpu.roll(yn, LANES - DIFF_QKDIM // 2, 1),
                            pltpu.roll(yn, DIFF_QKDIM // 2, 1))
            qk_ref[:, c * LANES:(c + 1) * LANES] = (yn * cos + rot * sin).astype(BF16)

    for jq in range(J_QK + 1):
        @pl.when(j == jq)
        def _():
            y = _dot_nt(n_scr[...], w_ref[...].astype(BF16))
            if jq > 0:
                qk_epilogue(jq - 1)
            if jq < J_QK:
                y_scr[jq % 2] = y
            else:
                mid_ref[...] = y.astype(BF16)

    @pl.when((j > J_QK) & (j < J_LR))
    def _():
        mid_ref[...] = _dot_nt(n_scr[...], w_ref[...].astype(BF16)).astype(BF16)

    @pl.when(j == J_LR)
    def _():
        lr = _dot_nt(n_scr[...], wlr_ref[...].astype(BF16))
        z = _dot(lr.astype(BF16), wa2_ref[...].astype(BF16)) + ba_ref[...]
        log_sig = jnp.minimum(z, 0.0) - jnp.log(1.0 + jnp.exp(-jnp.abs(z)))
        loga_ref[...] = log_sig * (1.0 / GLA_TAU)

    @pl.when(j >= J_GR)
    def _():
        y = _dot_nt(n_scr[...], wgr_ref[...].astype(BF16))
        sgr_ref[...] = (y / (1.0 + jnp.exp(-y))).astype(BF16)


def _inproj(x2, g_attn, pos2, freq, qkg, w_t, w_a2, b_a, *, tm):
    T, D = x2.shape
    n_mid = J_MID * TN
    n_gk = GLA_HEADS * GLA_KDIM
    lr0 = J_LR * TN
    gr0 = lr0 + GLA_GATE_RANK
    n_gr = w_t.shape[0] - gr0
    assert n_gr == 2 * TN and lr0 % GLA_GATE_RANK == 0
    return pl.pallas_call(
        _inproj_kernel,
        grid=(T // tm, N_J),
        in_specs=[
            pl.BlockSpec((tm, D), lambda i, j: (i, 0)),
            pl.BlockSpec((1, D), lambda i, j: (0, 0)),
            pl.BlockSpec((tm, 1), lambda i, j: (i, 0)),
            pl.BlockSpec((1, LANES), lambda i, j: (0, 0)),
            pl.BlockSpec((2, LANES), lambda i, j: (0, 0)),
            pl.BlockSpec((TN, D), lambda i, j: (jnp.minimum(j, J_LR - 1), 0)),
            pl.BlockSpec((pl.Element(TN), pl.Element(D)),
                         lambda i, j: (pl.multiple_of(gr0 + TN * jnp.clip(j - J_GR, 0, 1), SUBLANES), 0)),
            pl.BlockSpec((GLA_GATE_RANK, D), lambda i, j: (lr0 // GLA_GATE_RANK, 0)),
            pl.BlockSpec((GLA_GATE_RANK, n_gk), lambda i, j: (0, 0)),
            pl.BlockSpec((1, n_gk), lambda i, j: (0, 0)),
        ],
        out_specs=[
            pl.BlockSpec((tm, TN), lambda i, j: (i, jnp.clip(j - 1, 0, J_QK - 1))),
            pl.BlockSpec((tm, TN), lambda i, j: (i, jnp.clip(j - J_QK, 0, J_MID - 1))),
            pl.BlockSpec((tm, n_gk), lambda i, j: (i, 0)),
            pl.BlockSpec((tm, TN), lambda i, j: (i, jnp.clip(j - J_GR, 0, 1))),
        ],
        out_shape=[
            jax.ShapeDtypeStruct((T, J_QK * TN), BF16),
            jax.ShapeDtypeStruct((T, n_mid), BF16),
            jax.ShapeDtypeStruct((T, n_gk), F32),
            jax.ShapeDtypeStruct((T, n_gr), BF16),
        ],
        scratch_shapes=[
            pltpu.VMEM((tm, D), BF16),
            pltpu.VMEM((tm, LANES), F32),
            pltpu.VMEM((tm, LANES), F32),
            pltpu.VMEM((2, tm, TN), F32),
        ],
        compiler_params=_cparams(("parallel", "arbitrary"), 56),
        name="inproj",
    )(x2, g_attn, pos2, freq, qkg, w_t, w_t, w_t, w_a2, b_a)


SCORE_BOUND = 80.0


def _diffattn_kernel(ti_ref, tj_ref, lv_ref, q_ref, k_ref, v_ref, sg_ref, o_ref,
                     vext, diag_mask, acc1, acc2, m1, m2, *, tq, n_tiles, bounded):
    S = q_ref.shape[0]
    nq = S // tq
    half = tq // 2
    unroll = 7
    diag_unroll = 4 if nq % 4 == 0 else 2

    vext[:, 0:DIFF_VDIM] = v_ref[...]
    vext[:, DIFF_VDIM:] = jnp.ones((S, DIFF_VDIM), BF16)
    row_chunk = lax.broadcasted_iota(I32, (tq, tq), 0) // CHUNK
    col_chunk = lax.broadcasted_iota(I32, (tq, tq), 1) // CHUNK
    diag_mask[...] = jnp.where(col_chunk <= row_chunk, 1.0, 0.0).astype(BF16)

    lane = lax.broadcasted_iota(I32, (1, LANES), 1)

    def block(i):
        return pl.ds(pl.multiple_of(i * tq, tq), tq)

    def q_comps(i):
        q = q_ref[block(i), :]
        zero = jnp.zeros_like(q)
        return jnp.where(lane < DIFF_QKDIM, q, zero), jnp.where(lane < DIFF_QKDIM, zero, q)

    def diag_tile(i):
        q1, q2 = q_comps(i)
        for qc, acc, m in ((q1, acc1, m1), (q2, acc2, m2)):
            for lo, n_keys in ((0, half), (half, tq)):
                rows = pl.ds(pl.multiple_of(i * tq + lo, half), half)
                keys = pl.ds(pl.multiple_of(i * tq, tq), n_keys)
                mask = diag_mask[lo:lo + half, 0:n_keys]
                s = _dot_nt(qc[lo:lo + half], k_ref[keys, :])
                if bounded:
                    acc[rows, :] = _dot(jnp.exp2(s).astype(BF16) * mask, vext[keys, :])
                else:
                    s = jnp.where(mask > 0, s, NEG_INF)
                    m_new = jnp.max(s, axis=-1, keepdims=True)
                    acc[rows, :] = _dot(jnp.exp2(s - m_new).astype(BF16), vext[keys, :])
                    m[rows, :] = m_new

    def full_tile(i, j):
        q1, q2 = q_comps(i)
        rows = block(i)
        k = k_ref[block(j), :]
        v = vext[block(j), :]
        for qc, acc, m in ((q1, acc1, m1), (q2, acc2, m2)):
            s = _dot_nt(qc, k)
            if bounded:
                acc[rows, :] += _dot(jnp.exp2(s).astype(BF16), v)
            else:
                m_old = m[rows, :]
                m_new = jnp.maximum(m_old, jnp.max(s, axis=-1, keepdims=True))
                p = jnp.exp2(s - m_new)
                acc[rows, :] = jnp.exp2(m_old - m_new) * acc[rows, :] + _dot(p.astype(BF16), v)
                m[rows, :] = m_new

    def diag_body(t, carry):
        for u in range(diag_unroll):
            diag_tile(diag_unroll * t + u)
        return carry
    lax.fori_loop(0, nq // diag_unroll, diag_body, 0)

    def full_body(t, carry):
        for u in range(unroll):
            full_tile(ti_ref[unroll * t + u], tj_ref[unroll * t + u])
        return carry
    lax.fori_loop(0, n_tiles // unroll, full_body, 0)
    for t in range(n_tiles - n_tiles % unroll, n_tiles):
        full_tile(ti_ref[t], tj_ref[t])

    lv = lv_ref[...]
    lam = (jnp.exp(jnp.sum(lv[0:1] * lv[1:2], axis=-1, keepdims=True))
           - jnp.exp(jnp.sum(lv[2:3] * lv[3:4], axis=-1, keepdims=True)) + LAM_INIT)

    def out_body(i, carry):
        rows = block(i)
        a1 = acc1[rows, :]
        a2 = acc2[rows, :]
        o = a1[:, :DIFF_VDIM] / a1[:, DIFF_VDIM:] - lam * (a2[:, :DIFF_VDIM] / a2[:, DIFF_VDIM:])
        o_ref[rows, :] = (_rms(o, sg_ref[...]) * (1.0 - LAM_INIT)).astype(BF16)
        return carry
    lax.fori_loop(0, nq, out_body, 0)


def _diffattn(lvec, qk, mid, subln_g, *, B, S, tq, bounded):
    T = B * S
    nq = S // tq
    assert nq % 2 == 0
    tiles = [(i, j) for i in range(nq) for j in range(i)]
    ti = jnp.asarray([t[0] for t in tiles], I32)
    tj = jnp.asarray([t[1] for t in tiles], I32)
    kern = functools.partial(_diffattn_kernel, tq=tq, n_tiles=len(tiles), bounded=bounded)
    m_rows = SUBLANES if bounded else S
    grid_spec = pltpu.PrefetchScalarGridSpec(
        num_scalar_prefetch=2,
        grid=(B, DIFF_HEADS),
        in_specs=[
            pl.BlockSpec((4, DIFF_QKDIM), lambda b, h, *_: (0, 0)),
            pl.BlockSpec((S, LANES), lambda b, h, *_: (b, h)),
            pl.BlockSpec((S, LANES), lambda b, h, *_: (b, DIFF_HEADS + h)),
            pl.BlockSpec((S, LANES), lambda b, h, *_: (b, h)),
            pl.BlockSpec((1, DIFF_VDIM), lambda b, h, *_: (0, 0)),
        ],
        out_specs=pl.BlockSpec((S, DIFF_VDIM), lambda b, h, *_: (b, h)),
        scratch_shapes=[
            pltpu.VMEM((S, 2 * DIFF_VDIM), BF16),
            pltpu.VMEM((tq, tq), BF16),
            pltpu.VMEM((S, 2 * DIFF_VDIM), F32),
            pltpu.VMEM((S, 2 * DIFF_VDIM), F32),
            pltpu.VMEM((m_rows, 1), F32),
            pltpu.VMEM((m_rows, 1), F32),
        ],
    )
    return pl.pallas_call(
        kern,
        grid_spec=grid_spec,
        out_shape=jax.ShapeDtypeStruct((T, DIFF_HEADS * DIFF_VDIM), BF16),
        compiler_params=_cparams(("parallel", "parallel"), 40),
        name="diffattn_bounded" if bounded else "diffattn_online",
    )(ti, tj, lvec, qk, qk, mid, subln_g)


def _gla_kernel(q_ref, k_ref, v_ref, la_ref, sgr_ref, g_ref, scan_ref, o_ref, state, *, blk):
    @pl.when(pl.program_id(2) == 0)
    def _():
        state[...] = jnp.zeros(state.shape, F32)

    la_t = la_ref[...].T
    k_t = k_ref[...].astype(F32).T
    hi, lo = _split_bf16(la_t)
    scan = scan_ref[...]
    cums, tots = [], []
    for sb in range(blk // LANES):
        slab = slice(sb * LANES, (sb + 1) * LANES)
        r = _dot(hi[:, slab], scan) + _dot(lo[:, slab], scan)
        cums.append(r[:, :LANES])
        tots.append(r[:, LANES:])
    cum_t = jnp.concatenate(cums, axis=1)
    tot_t = jnp.concatenate(tots, axis=1)
    kd_t = k_t * jnp.exp(tot_t - cum_t)

    n_chunks = blk // CHUNK
    lane = lax.broadcasted_iota(I32, (1, LANES), 1)
    d_states = []
    for ck in range(n_chunks):
        pair = slice((ck // 2) * LANES, (ck // 2 + 1) * LANES)
        in_chunk = (lane // CHUNK) == (ck % 2)
        kd = jnp.where(in_chunk, kd_t[:, pair], 0.0).astype(BF16)
        d_states.append(_dot(kd, v_ref[pair, :]))

    st = state[...]
    states = []
    for ck in range(n_chunks):
        decay = jnp.exp(tot_t[:, ck * CHUNK:ck * CHUNK + 1])
        st = decay * st + d_states[ck]
        states.append(st.astype(BF16))
    state[...] = st

    o = jnp.concatenate([_dot(q_ref[ck * CHUNK:(ck + 1) * CHUNK, :], states[ck]) for ck in range(n_chunks)],
                        axis=0) * (GLA_KDIM ** -0.5)
    o_ref[...] = (_rms(o, g_ref[...]) * sgr_ref[...].astype(F32)).astype(BF16)


def _gla(mid, log_a, sgr, out_g, *, B, S, blk):
    T = B * S
    ns = S // blk
    kern = functools.partial(_gla_kernel, blk=blk)
    q_col0 = (DIFF_HEADS * DIFF_VDIM) // GLA_KDIM
    k_col0 = q_col0 + GLA_HEADS
    v_col0 = (DIFF_HEADS * DIFF_VDIM + 2 * GLA_HEADS * GLA_KDIM) // GLA_VDIM
    assert LANES % CHUNK == 0 and blk % LANES == 0
    r = jnp.arange(LANES, dtype=I32)[:, None]
    c = jnp.arange(LANES, dtype=I32)[None, :]
    same = (r // CHUNK) == (c // CHUNK)
    scan = jnp.concatenate([(same & (r <= c)).astype(BF16), same.astype(BF16)], axis=1)
    return pl.pallas_call(
        kern,
        grid=(B, GLA_HEADS, ns),
        in_specs=[
            pl.BlockSpec((blk, GLA_KDIM), lambda b, h, s: (b * ns + s, q_col0 + h)),
            pl.BlockSpec((blk, GLA_KDIM), lambda b, h, s: (b * ns + s, k_col0 + h)),
            pl.BlockSpec((blk, GLA_VDIM), lambda b, h, s: (b * ns + s, v_col0 + h)),
            pl.BlockSpec((blk, GLA_KDIM), lambda b, h, s: (b * ns + s, h)),
            pl.BlockSpec((blk, GLA_VDIM), lambda b, h, s: (b * ns + s, h)),
            pl.BlockSpec((1, GLA_VDIM), lambda b, h, s: (0, 0)),
            pl.BlockSpec((LANES, 2 * LANES), lambda b, h, s: (0, 0)),
        ],
        out_specs=pl.BlockSpec((blk, GLA_VDIM), lambda b, h, s: (b * ns + s, h)),
        out_shape=jax.ShapeDtypeStruct((T, GLA_HEADS * GLA_VDIM), BF16),
        scratch_shapes=[pltpu.VMEM((GLA_KDIM, GLA_VDIM), F32)],
        compiler_params=_cparams(("parallel", "parallel", "arbitrary"), 32),
        name="gla",
    )(mid, mid, mid, log_a, sgr, out_g, scan)


def _resident_w_map(n_j):
    return lambda i, j: (0, jnp.where(i == 0, j, n_j - 1))


def _outproj_kernel(a_ref, b_ref, wa_ref, wb_ref, x_ref, o_ref, w_scr):
    j = pl.program_id(1)
    ka = a_ref.shape[1]

    @pl.when(pl.program_id(0) == 0)
    def _():
        w_scr[j, 0:ka, :] = wa_ref[...].astype(BF16)
        w_scr[j, ka:, :] = wb_ref[...].astype(BF16)

    acc = _dot(a_ref[...], w_scr[j, 0:ka, :]) + _dot(b_ref[...], w_scr[j, ka:, :])
    o_ref[...] = x_ref[...] + acc


def _outproj(a, b, w_out, x2, *, tm, tn):
    T, ka = a.shape
    kb = b.shape[1]
    assert ka == kb
    D = w_out.shape[1]
    n_j = D // tn
    return pl.pallas_call(
        _outproj_kernel,
        grid=(T // tm, n_j),
        in_specs=[
            pl.BlockSpec((tm, ka), lambda i, j: (i, 0)),
            pl.BlockSpec((tm, kb), lambda i, j: (i, 0)),
            pl.BlockSpec((ka, tn), _resident_w_map(n_j)),
            pl.BlockSpec((kb, tn), lambda i, j: (1, jnp.where(i == 0, j, n_j - 1))),
            pl.BlockSpec((tm, tn), lambda i, j: (i, j)),
        ],
        out_specs=pl.BlockSpec((tm, tn), lambda i, j: (i, j)),
        out_shape=jax.ShapeDtypeStruct((T, D), F32),
        scratch_shapes=[pltpu.VMEM((n_j, ka + kb, tn), BF16)],
        compiler_params=_cparams(("arbitrary", "arbitrary"), 48),
        name="outproj",
    )(a, b, w_out, w_out, x2)


def _normproj_kernel(x_ref, g_ref, w_ref, hg_ref, o_ref, n_scr, *w_scr, n_norm):
    j = pl.program_id(1)

    @pl.when(j == 0)
    def _():
        n_scr[...] = _rms(x_ref[...], g_ref[...]).astype(BF16)

    if w_scr:
        @pl.when(pl.program_id(0) == 0)
        def _():
            w_scr[0][j] = w_ref[...].astype(BF16)
        y = _dot(n_scr[...], w_scr[0][j])
    else:
        y = _dot(n_scr[...], w_ref[...].astype(BF16))

    @pl.when(j < n_norm)
    def _():
        o_ref[...] = _rms(y, hg_ref[...]).astype(BF16)

    @pl.when(j >= n_norm)
    def _():
        o_ref[...] = y.astype(BF16)


def _normproj(x2, g, w, head_g, *, tm, tn, n_norm, name):
    T, D = x2.shape
    N = w.shape[1]
    kern = functools.partial(_normproj_kernel, n_norm=n_norm)
    n_j = N // tn
    resident = T // tm > 1
    return pl.pallas_call(
        kern,
        grid=(T // tm, n_j),
        in_specs=[
            pl.BlockSpec((tm, D), lambda i, j: (i, 0)),
            pl.BlockSpec((1, D), lambda i, j: (0, 0)),
            pl.BlockSpec((D, tn), _resident_w_map(n_j) if resident else (lambda i, j: (0, j))),
            pl.BlockSpec((1, tn), lambda i, j: (0, 0)),
        ],
        out_specs=pl.BlockSpec((tm, tn), lambda i, j: (i, j)),
        out_shape=jax.ShapeDtypeStruct((T, N), BF16),
        scratch_shapes=[pltpu.VMEM((tm, D), BF16)] + ([pltpu.VMEM((n_j, D, tn), BF16)] if resident else []),
        compiler_params=_cparams(("arbitrary", "arbitrary"), 48),
        name=name,
    )(x2, g, w, head_g)


def _cross_kernel(q_ref, k_ref, v_ref, w_ref, h_ref, o_ref, att_scr, w_scr, *, hdim):
    j = pl.program_id(1)

    @pl.when(pl.program_id(0) == 0)
    def _():
        w_scr[j] = w_ref[...].astype(BF16)

    @pl.when(j == 0)
    def _():
        for hd in range(CROSS_HEADS):
            cols = slice(hd * hdim, (hd + 1) * hdim)
            s = lax.dot_general(q_ref[:, cols], k_ref[:, cols], NT_DIMS, preferred_element_type=F32)
            p = jnp.exp(s - jnp.max(s, axis=-1, keepdims=True))
            l = jnp.sum(p, axis=-1, keepdims=True)
            att_scr[:, cols] = (_dot(p.astype(BF16), v_ref[:, cols]) / l).astype(BF16)

    o_ref[...] = h_ref[...] + _dot(att_scr[...], w_scr[j])


def _cross(qc, kv, w_co, h1, *, S, n_mem, tm, tn):
    T, D = qc.shape
    per_b = S // tm
    kern = functools.partial(_cross_kernel, hdim=D // CROSS_HEADS)
    n_j = D // tn
    return pl.pallas_call(
        kern,
        grid=(T // tm, n_j),
        in_specs=[
            pl.BlockSpec((tm, D), lambda i, j: (i, 0)),
            pl.BlockSpec((n_mem, D), lambda i, j: (i // per_b, 0)),
            pl.BlockSpec((n_mem, D), lambda i, j: (i // per_b, 1)),
            pl.BlockSpec((D, tn), _resident_w_map(n_j)),
            pl.BlockSpec((tm, tn), lambda i, j: (i, j)),
        ],
        out_specs=pl.BlockSpec((tm, tn), lambda i, j: (i, j)),
        out_shape=jax.ShapeDtypeStruct((T, D), F32),
        scratch_shapes=[pltpu.VMEM((tm, D), BF16), pltpu.VMEM((n_j, D, tn), BF16)],
        compiler_params=_cparams(("arbitrary", "arbitrary"), 48),
        name="cross",
    )(qc, kv, kv, w_co, h1)


R_ROWS = SUBLANES + N_EXPERTS


def _pack_bf16_pairs(xb16):
    c = xb16.shape[1] // 2
    u = lax.bitcast_convert_type(xb16.astype(F32), jnp.uint32)
    return (u[:, :c] >> 16) | (u[:, c:] & jnp.uint32(0xFFFF0000))


def _store_row_tiles(ref, x):
    for g in range(SUBLANES):
        ref[:, g, :] = x[:, g * LANES:(g + 1) * LANES]


def _load_row_tiles(ref):
    return jnp.concatenate([ref[:, g, :] for g in range(SUBLANES)], axis=1)


def _unpack_bf16_pairs_f32(w):
    lo = lax.bitcast_convert_type(w << 16, F32)
    hi = lax.bitcast_convert_type(w & jnp.uint32(0xFFFF0000), F32)
    return lo, hi


def _unpack_bf16_pairs(w):
    lo, hi = _unpack_bf16_pairs_f32(w)
    return lo.astype(BF16), hi.astype(BF16)


def _router_kernel(h_ref, g_ref, wt_ref, b_ref, eid_ref, gate_ref, xn_ref):
    n = _rms(h_ref[...], g_ref[...])
    nh, nl = _split_bf16(n)
    _store_row_tiles(xn_ref, _pack_bf16_pairs(nh))
    wh, wl = _split_bf16(wt_ref[...])
    nt = functools.partial(lax.dot_general, dimension_numbers=NT_DIMS, preferred_element_type=F32)
    lg = nt(wh, nh) + nt(wh, nl) + nt(wl, nh) + b_ref[...]

    tm = lg.shape[1]
    row = lax.broadcasted_iota(I32, (SUBLANES, tm), 0)

    def first_argmax(v, vmax):
        return jnp.min(jnp.where(v == vmax, row, SUBLANES), axis=0, keepdims=True)

    gl = jnp.where(row < N_GROUPS, lg[0:SUBLANES], NEG_INF)
    gmax = jnp.max(gl, axis=0, keepdims=True)
    grp = first_argmax(gl, gmax)
    grp_w = 1.0 / jnp.sum(jnp.exp(gl - gmax), axis=0, keepdims=True)

    sel = jnp.zeros((SUBLANES, tm), F32)
    for gi in range(N_GROUPS):
        lo = SUBLANES + gi * EXPERTS_PER_GROUP
        sel = jnp.where(grp == gi, lg[lo:lo + EXPERTS_PER_GROUP], sel)
    e = jnp.exp(sel - jnp.max(sel, axis=0, keepdims=True))
    prob = e / jnp.sum(e, axis=0, keepdims=True)
    p1 = jnp.max(prob, axis=0, keepdims=True)
    i1 = first_argmax(prob, p1)
    rest = jnp.where(row == i1, -1.0, prob)
    p2 = jnp.max(rest, axis=0, keepdims=True)
    i2 = first_argmax(rest, p2)
    den = p1 + p2
    base = grp * EXPERTS_PER_GROUP
    eid_ref[...] = jnp.where(row == 0, base + i1, jnp.where(row == 1, base + i2, 0))
    gate_ref[...] = jnp.where(row == 0, grp_w * p1 / den, jnp.where(row == 1, grp_w * p2 / den, 0.0))


def _router(h2, g_ffn, w_rt, b_r, *, tm):
    T, D = h2.shape
    return pl.pallas_call(
        _router_kernel,
        grid=(T // tm,),
        in_specs=[
            pl.BlockSpec((tm, D), lambda i: (i, 0)),
            pl.BlockSpec((1, D), lambda i: (0, 0)),
            pl.BlockSpec((R_ROWS, D), lambda i: (0, 0)),
            pl.BlockSpec((R_ROWS, 1), lambda i: (0, 0)),
        ],
        out_specs=[
            pl.BlockSpec((SUBLANES, tm), lambda i: (0, i)),
            pl.BlockSpec((SUBLANES, tm), lambda i: (0, i)),
            pl.BlockSpec((tm, SUBLANES, LANES), lambda i: (i, 0, 0)),
        ],
        out_shape=[
            jax.ShapeDtypeStruct((SUBLANES, T), I32),
            jax.ShapeDtypeStruct((SUBLANES, T), F32),
            jax.ShapeDtypeStruct((T, SUBLANES, LANES), jnp.uint32),
        ],
        compiler_params=_cparams(("parallel",), 32),
        name="router",
    )(h2, g_ffn, w_rt, b_r)


def _rank_kernel(eid_ref, rank_ref, cnt_ref, carry):
    @pl.when(pl.program_id(0) == 0)
    def _():
        carry[...] = jnp.zeros(carry.shape, F32)

    tm = eid_ref.shape[1]
    e0 = eid_ref[0:1, :]
    e1 = eid_ref[1:2, :]
    erow = lax.broadcasted_iota(I32, (N_EXPERTS, tm), 0)
    hit = jnp.where((erow == e0) | (erow == e1), 1.0, 0.0)
    r = lax.broadcasted_iota(I32, (tm, tm), 0)
    c = lax.broadcasted_iota(I32, (tm, tm), 1)
    before = jnp.where(r < c, 1.0, 0.0).astype(BF16)
    pre = _dot(hit.astype(BF16), before) + carry[:, 0:1]
    rank0 = jnp.sum(jnp.where(erow == e0, pre, 0.0), axis=0, keepdims=True)
    rank1 = jnp.sum(jnp.where(erow == e1, pre, 0.0), axis=0, keepdims=True)
    row = lax.broadcasted_iota(I32, (SUBLANES, tm), 0)
    rank_ref[...] = jnp.where(row == 0, rank0, jnp.where(row == 1, rank1, 0.0)).astype(I32)
    total = carry[...] + jnp.sum(hit, axis=1, keepdims=True)
    carry[...] = total
    cnt_ref[...] = total.astype(I32)


def _rank(eid, *, tm):
    T = eid.shape[1]
    return pl.pallas_call(
        _rank_kernel,
        grid=(T // tm,),
        in_specs=[pl.BlockSpec((SUBLANES, tm), lambda i: (0, i))],
        out_specs=[
            pl.BlockSpec((SUBLANES, tm), lambda i: (0, i)),
            pl.BlockSpec((N_EXPERTS, LANES), lambda i: (0, 0)),
        ],
        out_shape=[
            jax.ShapeDtypeStruct((SUBLANES, T), I32),
            jax.ShapeDtypeStruct((N_EXPERTS, LANES), I32),
        ],
        scratch_shapes=[pltpu.VMEM((N_EXPERTS, LANES), F32)],
        compiler_params=_cparams(("arbitrary",), 32),
        name="rank",
    )(eid)


def _dispatch_kernel(dest_ref, pad0_ref, npad_ref, nb_ref, xn_ref, xb_hbm, zbuf, sem, psem,
                     *, n_tok, tb, n_blocks):
    i = pl.program_id(0)
    tm = xn_ref.shape[0]
    base = i * tm

    def row_body(r, carry):
        for k in range(TOP_K):
            pltpu.make_async_copy(xn_ref.at[r], xb_hbm.at[dest_ref[k * n_tok + base + r]], sem).start(priority=k)
        return carry
    lax.fori_loop(0, tm, row_body, 0, unroll=8)

    @pl.when(i == 0)
    def _():
        _dispatch_fill(pad0_ref, npad_ref, nb_ref, xb_hbm, zbuf, psem, tb=tb, n_blocks=n_blocks)

    for k in range(TOP_K):
        pltpu.make_async_copy(xn_ref, xb_hbm.at[pl.ds(0, tm)], sem).wait()


def _dispatch_fill(pad0_ref, npad_ref, nb_ref, xb_hbm, zbuf, psem, *, tb, n_blocks):
    zbuf[...] = jnp.zeros(zbuf.shape, zbuf.dtype)

    def pad_copy(e):
        n = npad_ref[e]
        return pltpu.make_async_copy(zbuf.at[pl.ds(0, n)], xb_hbm.at[pl.ds(pad0_ref[e], n)], psem.at[0])

    def tail_copy(blk):
        return pltpu.make_async_copy(zbuf, xb_hbm.at[pl.ds(pl.multiple_of(blk * tb, tb), tb)], psem.at[1])

    def for_each_pad(fn):
        def body(e, c):
            @pl.when(npad_ref[e] > 0)
            def _():
                fn(e)
            return c
        lax.fori_loop(0, N_EXPERTS, body, 0)

    def for_each_tail(fn):
        def body(b, c):
            fn(b)
            return c
        lax.fori_loop(nb_ref[0], n_blocks, body, 0)

    for_each_pad(lambda e: pad_copy(e).start())
    for_each_tail(lambda b: tail_copy(b).start())
    for_each_pad(lambda e: pad_copy(e).wait())
    for_each_tail(lambda b: tail_copy(b).wait())


def _dispatch(dest, pad0, npad, nb, xn, *, tb, n_blocks, tm):
    T = xn.shape[0]
    tile = xn.shape[1:]
    kern = functools.partial(_dispatch_kernel, n_tok=T, tb=tb, n_blocks=n_blocks)
    grid_spec = pltpu.PrefetchScalarGridSpec(
        num_scalar_prefetch=4,
        grid=(T // tm,),
        in_specs=[pl.BlockSpec((tm,) + tile, lambda i, *_: (i, 0, 0))],
        out_specs=pl.BlockSpec(memory_space=pl.ANY),
        scratch_shapes=[
            pltpu.VMEM((tb,) + tile, xn.dtype),
            pltpu.SemaphoreType.DMA(()),
            pltpu.SemaphoreType.DMA((2,)),
        ],
    )
    return pl.pallas_call(
        kern,
        grid_spec=grid_spec,
        out_shape=jax.ShapeDtypeStruct((n_blocks * tb,) + tile, xn.dtype),
        compiler_params=_cparams(("arbitrary",), 32),
        name="dispatch",
    )(dest, pad0, npad, nb, xn)


def _expert_kernel(be_ref, first_ref, slot_ref, nxt_ref, run1_ref, nb_ref, x_ref, wg_hbm, wu_hbm, wd_hbm, y_ref,
                   wg_f, wu_f, wd_f, wsem, wg_b, wu_b, wd_b):
    i = pl.program_id(0)
    nb = nb_ref[0]

    def weight_copies(e, slot):
        return (pltpu.make_async_copy(wg_hbm.at[e], wg_f.at[slot], wsem.at[3 * slot]),
                pltpu.make_async_copy(wu_hbm.at[e], wu_f.at[slot], wsem.at[3 * slot + 1]),
                pltpu.make_async_copy(wd_hbm.at[e], wd_f.at[slot], wsem.at[3 * slot + 2]))

    @pl.when(i == 0)
    def _():
        for cp in weight_copies(be_ref[0], 0):
            cp.start()

    @pl.when((i == 0) & (run1_ref[0] >= 0))
    def _():
        for cp in weight_copies(jnp.maximum(run1_ref[0], 0), 1):
            cp.start()

    first = (i < nb) & (first_ref[i] == 1)
    slot = slot_ref[i]

    @pl.when(first)
    def _():
        for cp in weight_copies(0, slot):
            cp.wait()
        wg_b[...] = wg_f[slot].astype(BF16)
        wu_b[...] = wu_f[slot].astype(BF16)
        wd_b[...] = wd_f[slot].astype(BF16)

    @pl.when(first & (nxt_ref[i] >= 0))
    def _():
        for cp in weight_copies(jnp.maximum(nxt_ref[i], 0), slot):
            cp.start()

    @pl.when(i < nb)
    def _():
        n_lo, n_hi = _unpack_bf16_pairs(_load_row_tiles(x_ref))
        half = n_lo.shape[1]
        a = _dot(n_lo, wg_b[0:half, :]) + _dot(n_hi, wg_b[half:, :])
        u = _dot(n_lo, wu_b[0:half, :]) + _dot(n_hi, wu_b[half:, :])
        hdn = (a / (1.0 + jnp.exp(-a))) * u
        y_ref[...] = _pack_bf16_pairs(_dot(hdn.astype(BF16), wd_b[...]).astype(BF16))

    @pl.when(i >= nb)
    def _():
        y_ref[...] = jnp.zeros(y_ref.shape, y_ref.dtype)


def _experts(xb, blk_e, blk_first, blk_slot, blk_next, run1, nb, w_gate, w_up, w_down, *, tb, n_blocks):
    D, De = w_gate.shape[1:]
    assert xb.shape[1:] == (SUBLANES, LANES) and D == 2 * SUBLANES * LANES

    def x_map(i, be, first, slot, nxt, r1, nbr):
        return (jnp.minimum(i, nbr[0] - 1), 0, 0)

    grid_spec = pltpu.PrefetchScalarGridSpec(
        num_scalar_prefetch=6,
        grid=(n_blocks,),
        in_specs=[
            pl.BlockSpec((tb, SUBLANES, LANES), x_map),
            pl.BlockSpec(memory_space=pl.ANY),
            pl.BlockSpec(memory_space=pl.ANY),
            pl.BlockSpec(memory_space=pl.ANY),
        ],
        out_specs=pl.BlockSpec((tb, D // 2), lambda i, *_: (i, 0)),
        scratch_shapes=[
            pltpu.VMEM((2, D, De), F32),
            pltpu.VMEM((2, D, De), F32),
            pltpu.VMEM((2, De, D), F32),
            pltpu.SemaphoreType.DMA((6,)),
            pltpu.VMEM((D, De), BF16),
            pltpu.VMEM((D, De), BF16),
            pltpu.VMEM((De, D), BF16),
        ],
    )
    return pl.pallas_call(
        _expert_kernel,
        grid_spec=grid_spec,
        out_shape=jax.ShapeDtypeStruct((n_blocks * tb, D // 2), jnp.uint32),
        compiler_params=_cparams(("arbitrary",), 52),
        name="experts",
    )(blk_e, blk_first, blk_slot, blk_next, run1, nb, xb, w_gate, w_up, w_down)


def _combine_kernel(dest_ref, y_hbm, h_ref, gate_ref, o_ref, ybuf, sem, *, tm, n_tok):
    i = pl.program_id(0)
    n = pl.num_programs(0)

    def row_copy(d, k, r, slot):
        return pltpu.make_async_copy(y_hbm.at[pl.ds(d, 1), :], ybuf.at[slot, k, pl.ds(r, 1), :], sem.at[slot])

    def start_gather(blk, slot):
        def body(r, carry):
            for k in range(TOP_K):
                row_copy(dest_ref[k * n_tok + blk * tm + r], k, r, slot).start(priority=k)
            return carry
        lax.fori_loop(0, tm, body, 0, unroll=8)

    def wait_gather(slot):
        for k in range(TOP_K):
            pltpu.make_async_copy(y_hbm.at[pl.ds(0, tm), :], ybuf.at[slot, k], sem.at[slot]).wait()

    @pl.when(i == 0)
    def _():
        start_gather(0, 0)

    @pl.when(i + 1 < n)
    def _():
        start_gather(i + 1, (i + 1) % 2)

    slot = i % 2
    wait_gather(slot)
    gt = gate_ref[...]
    half = h_ref.shape[1] // 2
    y0_lo, y0_hi = _unpack_bf16_pairs_f32(ybuf[slot, 0])
    y1_lo, y1_hi = _unpack_bf16_pairs_f32(ybuf[slot, 1])
    o_ref[:, 0:half] = h_ref[:, 0:half] + gt[:, 0:1] * y0_lo + gt[:, 1:2] * y1_lo
    o_ref[:, half:] = h_ref[:, half:] + gt[:, 0:1] * y0_hi + gt[:, 1:2] * y1_hi


def _combine(dest, yb, h2, gate, *, tm):
    T, D = h2.shape
    kern = functools.partial(_combine_kernel, tm=tm, n_tok=T)
    grid_spec = pltpu.PrefetchScalarGridSpec(
        num_scalar_prefetch=1,
        grid=(T // tm,),
        in_specs=[
            pl.BlockSpec(memory_space=pl.ANY),
            pl.BlockSpec((tm, D), lambda i, d: (i, 0)),
            pl.BlockSpec((tm, TOP_K), lambda i, d: (i, 0)),
        ],
        out_specs=pl.BlockSpec((tm, D), lambda i, d: (i, 0)),
        scratch_shapes=[
            pltpu.VMEM((2, TOP_K, tm, D // 2), jnp.uint32),
            pltpu.SemaphoreType.DMA((2,)),
        ],
    )
    return pl.pallas_call(
        kern,
        grid_spec=grid_spec,
        out_shape=jax.ShapeDtypeStruct((T, D), F32),
        compiler_params=_cparams(("arbitrary",), 40),
        name="combine",
    )(dest, yb, h2, gate)


class _Tiles(NamedTuple):
    proj_rows: int = 1024
    proj_cols: int = 512
    attn_rows: int = 512
    gla_rows: int = 2048
    token_rows: int = 512
    expert_rows: int = 256


TILES = _Tiles()


def kernel(x, mem, positions, g_attn, w_in, q_norm_g, k_norm_g, lambda_q1, lambda_k1, lambda_q2, lambda_k2, diff_subln_g, gla_w_a2, gla_b_a, gla_out_g, w_out, g_cross, g_mem, w_cq, w_ckv, cq_norm_g, ck_norm_g, w_co, g_ffn, w_router_grp, b_router_grp, w_router_exp, b_router_exp, w_gate, w_up, w_down):
    B, S, D = x.shape
    T = B * S
    n_mem = mem.shape[1]
    l = 0
    x2 = x.reshape(T, D)

    half = DIFF_QKDIM // 2
    freq = ROPE_THETA ** (-jnp.arange(half, dtype=F32) / half)
    freq = jnp.tile(freq, LANES // half)[None, :]
    q_scale = math.log2(math.e) * DIFF_QKDIM ** -0.5
    qkg = jnp.stack([jnp.tile(q_norm_g[l], 2) * q_scale, jnp.tile(k_norm_g[l], 2)])
    score_bound = 1.01 * DIFF_QKDIM * q_scale * jnp.max(jnp.abs(q_norm_g[l])) * jnp.max(jnp.abs(k_norm_g[l]))
    lvec = jnp.stack([lambda_q1[l], lambda_k1[l], lambda_q2[l], lambda_k2[l]])

    qk, mid, log_a, sgr = _inproj(x2, g_attn[l][None], positions.reshape(T, 1), freq, qkg, w_in[l].T,
                                  gla_w_a2[l], gla_b_a[l][None], tm=TILES.proj_rows)
    diffattn = functools.partial(_diffattn, lvec, qk, mid, diff_subln_g[l][None], B=B, S=S, tq=TILES.attn_rows)
    mix_d = lax.cond(score_bound <= SCORE_BOUND,
                     functools.partial(diffattn, bounded=True), functools.partial(diffattn, bounded=False))
    mix_g = _gla(mid, log_a, sgr, gla_out_g[l][None], B=B, S=S, blk=TILES.gla_rows)
    h1 = _outproj(mix_d, mix_g, w_out[l], x2, tm=TILES.proj_rows, tn=TILES.proj_cols)

    hdim = D // CROSS_HEADS
    qc = _normproj(h1, g_cross[l][None], w_cq[l], cq_norm_g[l][None] * (hdim ** -0.5),
                   tm=TILES.proj_rows, tn=hdim, n_norm=CROSS_HEADS, name="cq")
    kv = _normproj(mem.reshape(B * n_mem, D), g_mem[l][None], w_ckv[l], ck_norm_g[l][None],
                   tm=B * n_mem, tn=hdim, n_norm=CROSS_HEADS, name="ckv")
    h2 = _cross(qc, kv, w_co[l], h1, S=S, n_mem=n_mem, tm=TILES.proj_rows, tn=TILES.proj_cols)

    w_rt = jnp.concatenate([w_router_grp[l].T, jnp.zeros((SUBLANES - N_GROUPS, D), F32), w_router_exp[l].T])
    b_r = jnp.concatenate([b_router_grp[l], jnp.zeros((SUBLANES - N_GROUPS,), F32), b_router_exp[l]])[:, None]
    eid, gate, xn = _router(h2, g_ffn[l][None], w_rt, b_r, tm=TILES.token_rows)
    rank, cnt = _rank(eid, tm=TILES.token_rows)

    assert TOP_K == 2
    tb = TILES.expert_rows
    n_blocks = (T * TOP_K + N_EXPERTS * (tb - 1) + tb - 1) // tb
    counts = cnt[:, 0]
    pcounts = ((counts + tb - 1) // tb) * tb
    pends = jnp.cumsum(pcounts)
    pstarts = pends - pcounts
    nb = (pends[-1:] // tb).astype(I32)
    eids = jnp.arange(N_EXPERTS, dtype=I32)
    blk_start = jnp.arange(n_blocks, dtype=I32) * tb
    blk_e = jnp.minimum(jnp.sum(pends[None, :] <= blk_start[:, None], axis=1), N_EXPERTS - 1).astype(I32)
    blk_first = jnp.concatenate([jnp.ones((1,), I32), (blk_e[1:] != blk_e[:-1]).astype(I32)])
    used = jnp.where(counts > 0, eids, N_EXPERTS)
    next_used = jnp.concatenate([lax.cummin(used[::-1])[::-1][1:], jnp.full((1,), N_EXPERTS, I32)])
    next_used = jnp.where(next_used < N_EXPERTS, next_used, -1)

    def table_at(idx, table):
        return jnp.where(idx >= 0, jnp.sum(jnp.where(idx[:, None] == eids, table, 0), axis=1), -1).astype(I32)

    after_next = table_at(next_used, next_used)
    run_of = (jnp.cumsum((counts > 0).astype(I32)) - 1).astype(I32)
    blk_next = table_at(blk_e, after_next)
    blk_slot = table_at(blk_e, run_of % 2)
    run1 = table_at(blk_e[:1], next_used)
    pick = eid[:TOP_K]
    pstart_of = jnp.sum(jnp.where(pick[..., None] == eids, pstarts, 0), axis=-1)
    dest = (pstart_of + rank[:TOP_K]).astype(I32).reshape(-1)
    pad0 = (pstarts + counts).astype(I32)
    npad = (pcounts - counts).astype(I32)

    xb = _dispatch(dest, pad0, npad, nb, xn, tb=tb, n_blocks=n_blocks, tm=TILES.token_rows)
    yb = _experts(xb, blk_e, blk_first, blk_slot, blk_next, run1, nb, w_gate[l], w_up[l], w_down[l],
                  tb=tb, n_blocks=n_blocks)
    out = _combine(dest, yb, h2, gate[:TOP_K].T, tm=TILES.token_rows)
    return out.reshape(B, S, D)
```

```python
import functools
import math
from typing import NamedTuple

import jax
import jax.numpy as jnp
from jax import lax
from jax.experimental import pallas as pl
from jax.experimental.pallas import tpu as pltpu

F32 = jnp.float32
BF16 = jnp.bfloat16
I32 = jnp.int32

LANES = 128
SUBLANES = 8

CHUNK = 64
ROPE_THETA = 10000.0
NORM_EPS = 1e-6
NEG_INF = -1e30
DIFF_HEADS = 8
DIFF_VDIM = 128
DIFF_QKDIM = 64
GLA_HEADS = 4
GLA_VDIM = 256
GLA_KDIM = 128
GLA_GATE_RANK = 16
GLA_TAU = 16.0
CROSS_HEADS = 4
N_GROUPS = 4
EXPERTS_PER_GROUP = 8
N_EXPERTS = N_GROUPS * EXPERTS_PER_GROUP
TOP_K = 2
LAM_INIT = 0.8 - 0.6 * math.exp(-0.3 * 0)

NT_DIMS = (((1,), (1,)), ((), ()))


def _cparams(semantics, vmem_mib):
    return pltpu.CompilerParams(dimension_semantics=semantics,
                                vmem_limit_bytes=vmem_mib * 1024 * 1024)


def _dot(a, b):
    return jnp.dot(a, b, preferred_element_type=F32)


def _dot_nt(a, b):
    return lax.dot_general(a, b, NT_DIMS, preferred_element_type=F32)


def _rms(x, g):
    ms = jnp.mean(x * x, axis=-1, keepdims=True)
    return x * lax.rsqrt(ms + NORM_EPS) * g


def _split_bf16(x):
    hi = x.astype(BF16)
    lo = (x - hi.astype(F32)).astype(BF16)
    return hi, lo


TN = 512
J_QK = 4
J_MID = 6
J_LR = J_QK + J_MID
J_GR = J_LR + 1
N_J = J_GR + 2


def _inproj_kernel(x_ref, g_ref, pos_ref, freq_ref, qkg_ref, w_ref, wgr_ref, wlr_ref, wa2_ref, ba_ref,
                   qk_ref, mid_ref, loga_ref, sgr_ref, n_scr, cos_scr, sin_scr, y_scr):
    j = pl.program_id(1)

    @pl.when(j == 0)
    def _():
        n_scr[...] = _rms(x_ref[...], g_ref[...]).astype(BF16)
        ang = pos_ref[...].astype(F32) * freq_ref[...]
        cos_scr[...] = jnp.cos(ang)
        sin_scr[...] = jnp.sin(ang)

    def qk_epilogue(jq):
        y_prev = y_scr.at[jq % 2]
        lane = lax.broadcasted_iota(I32, (1, LANES), 1)
        low_seg = lane < DIFF_QKDIM
        first_half = (lane % DIFF_QKDIM) < (DIFF_QKDIM // 2)
        gain = qkg_ref[jq // (J_QK // 2):jq // (J_QK // 2) + 1, :]
        cos = cos_scr[...]
        sin = sin_scr[...]
        for c in range(TN // LANES):
            yb = y_prev[:, c * LANES:(c + 1) * LANES]
            y2 = yb * yb
            s_lo = jnp.sum(jnp.where(low_seg, y2, 0.0), axis=-1, keepdims=True)
            s_hi = jnp.sum(jnp.where(low_seg, 0.0, y2), axis=-1, keepdims=True)
            ms = jnp.where(low_seg, s_lo, s_hi) * (1.0 / DIFF_QKDIM)
            yn = yb * lax.rsqrt(ms + NORM_EPS) * gain
            rot = jnp.where(first_half,
                            -pltpu.roll(yn, LANES - DIFF_QKDIM // 2, 1),
                            pltpu.roll(yn, DIFF_QKDIM // 2, 1))
            qk_ref[:, c * LANES:(c + 1) * LANES] = (yn * cos + rot * sin).astype(BF16)

    for jq in range(J_QK + 1):
        @pl.when(j == jq)
        def _():
            y = _dot_nt(n_scr[...], w_ref[...].astype(BF16))
            if jq > 0:
                qk_epilogue(jq - 1)
            if jq < J_QK:
                y_scr[jq % 2] = y
            else:
                mid_ref[...] = y.astype(BF16)

    @pl.when((j > J_QK) & (j < J_LR))
    def _():
        mid_ref[...] = _dot_nt(n_scr[...], w_ref[...].astype(BF16)).astype(BF16)

    @pl.when(j == J_LR)
    def _():
        lr = _dot_nt(n_scr[...], wlr_ref[...].astype(BF16))
        z = _dot(lr.astype(BF16), wa2_ref[...].astype(BF16)) + ba_ref[...]
        log_sig = jnp.minimum(z, 0.0) - jnp.log(1.0 + jnp.exp(-jnp.abs(z)))
        loga_ref[...] = log_sig * (1.0 / GLA_TAU)

    @pl.when(j >= J_GR)
    def _():
        y = _dot_nt(n_scr[...], wgr_ref[...].astype(BF16))
        sgr_ref[...] = (y / (1.0 + jnp.exp(-y))).astype(BF16)


def _to_bf16_kernel(x_ref, o_ref):
    o_ref[...] = x_ref[...].astype(BF16)


def _to_bf16(x, *, rows):
    n, d = x.shape
    assert n % rows == 0
    return pl.pallas_call(
        _to_bf16_kernel,
        grid=(n // rows,),
        in_specs=[pl.BlockSpec((rows, d), lambda i: (i, 0))],
        out_specs=pl.BlockSpec((rows, d), lambda i: (i, 0)),
        out_shape=jax.ShapeDtypeStruct((n, d), BF16),
        compiler_params=_cparams(("parallel",), 32),
        name="to_bf16",
    )(x)


def _inproj(x2, g_attn, pos2, freq, qkg, w_t, w_a2, b_a, *, tm):
    T, D = x2.shape
    n_mid = J_MID * TN
    n_gk = GLA_HEADS * GLA_KDIM
    lr0 = J_LR * TN
    gr0 = lr0 + GLA_GATE_RANK
    n_gr = w_t.shape[0] - gr0
    assert n_gr == 2 * TN and lr0 % GLA_GATE_RANK == 0
    return pl.pallas_call(
        _inproj_kernel,
        grid=(T // tm, N_J),
        in_specs=[
            pl.BlockSpec((tm, D), lambda i, j: (i, 0)),
            pl.BlockSpec((1, D), lambda i, j: (0, 0)),
            pl.BlockSpec((tm, 1), lambda i, j: (i, 0)),
            pl.BlockSpec((1, LANES), lambda i, j: (0, 0)),
            pl.BlockSpec((2, LANES), lambda i, j: (0, 0)),
            pl.BlockSpec((TN, D), lambda i, j: (jnp.minimum(j, J_LR - 1), 0)),
            pl.BlockSpec((pl.Element(TN), pl.Element(D)),
                         lambda i, j: (pl.multiple_of(gr0 + TN * jnp.clip(j - J_GR, 0, 1), 2 * SUBLANES), 0)),
            pl.BlockSpec((GLA_GATE_RANK, D), lambda i, j: (lr0 // GLA_GATE_RANK, 0)),
            pl.BlockSpec((GLA_GATE_RANK, n_gk), lambda i, j: (0, 0)),
            pl.BlockSpec((1, n_gk), lambda i, j: (0, 0)),
        ],
        out_specs=[
            pl.BlockSpec((tm, TN), lambda i, j: (i, jnp.clip(j - 1, 0, J_QK - 1))),
            pl.BlockSpec((tm, TN), lambda i, j: (i, jnp.clip(j - J_QK, 0, J_MID - 1))),
            pl.BlockSpec((tm, n_gk), lambda i, j: (i, 0)),
            pl.BlockSpec((tm, TN), lambda i, j: (i, jnp.clip(j - J_GR, 0, 1))),
        ],
        out_shape=[
            jax.ShapeDtypeStruct((T, J_QK * TN), BF16),
            jax.ShapeDtypeStruct((T, n_mid), BF16),
            jax.ShapeDtypeStruct((T, n_gk), F32),
            jax.ShapeDtypeStruct((T, n_gr), BF16),
        ],
        scratch_shapes=[
            pltpu.VMEM((tm, D), BF16),
            pltpu.VMEM((tm, LANES), F32),
            pltpu.VMEM((tm, LANES), F32),
            pltpu.VMEM((2, tm, TN), F32),
        ],
        compiler_params=_cparams(("parallel", "arbitrary"), 56),
        name="inproj",
    )(x2, g_attn, pos2, freq, qkg, w_t, w_t, w_t, w_a2, b_a)


SCORE_BOUND = 80.0


def _diffattn_kernel(ti_ref, tj_ref, lv_ref, q_ref, k_ref, v_ref, sg_ref, o_ref,
                     vext, diag_mask, acc1, acc2, m1, m2, *, tq, n_tiles, bounded):
    S = q_ref.shape[0]
    nq = S // tq
    half = tq // 2
    unroll = 7
    diag_unroll = 4 if nq % 4 == 0 else 2

    vext[:, 0:DIFF_VDIM] = v_ref[...]
    vext[:, DIFF_VDIM:] = jnp.ones((S, DIFF_VDIM), BF16)
    row_chunk = lax.broadcasted_iota(I32, (tq, tq), 0) // CHUNK
    col_chunk = lax.broadcasted_iota(I32, (tq, tq), 1) // CHUNK
    diag_mask[...] = jnp.where(col_chunk <= row_chunk, 1.0, 0.0).astype(BF16)

    lane = lax.broadcasted_iota(I32, (1, LANES), 1)

    def block(i):
        return pl.ds(pl.multiple_of(i * tq, tq), tq)

    def q_comps(i):
        q = q_ref[block(i), :]
        zero = jnp.zeros_like(q)
        return jnp.where(lane < DIFF_QKDIM, q, zero), jnp.where(lane < DIFF_QKDIM, zero, q)

    def diag_tile(i):
        q1, q2 = q_comps(i)
        for qc, acc, m in ((q1, acc1, m1), (q2, acc2, m2)):
            for lo, n_keys in ((0, half), (half, tq)):
                rows = pl.ds(pl.multiple_of(i * tq + lo, half), half)
                keys = pl.ds(pl.multiple_of(i * tq, tq), n_keys)
                mask = diag_mask[lo:lo + half, 0:n_keys]
                s = _dot_nt(qc[lo:lo + half], k_ref[keys, :])
                if bounded:
                    acc[rows, :] = _dot(jnp.exp2(s).astype(BF16) * mask, vext[keys, :])
                else:
                    s = jnp.where(mask > 0, s, NEG_INF)
                    m_new = jnp.max(s, axis=-1, keepdims=True)
                    acc[rows, :] = _dot(jnp.exp2(s - m_new).astype(BF16), vext[keys, :])
                    m[rows, :] = m_new

    def full_tile(i, j):
        q1, q2 = q_comps(i)
        rows = block(i)
        k = k_ref[block(j), :]
        v = vext[block(j), :]
        for qc, acc, m in ((q1, acc1, m1), (q2, acc2, m2)):
            s = _dot_nt(qc, k)
            if bounded:
                acc[rows, :] += _dot(jnp.exp2(s).astype(BF16), v)
            else:
                m_old = m[rows, :]
                m_new = jnp.maximum(m_old, jnp.max(s, axis=-1, keepdims=True))
                p = jnp.exp2(s - m_new)
                acc[rows, :] = jnp.exp2(m_old - m_new) * acc[rows, :] + _dot(p.astype(BF16), v)
                m[rows, :] = m_new

    def diag_body(t, carry):
        for u in range(diag_unroll):
            diag_tile(diag_unroll * t + u)
        return carry
    lax.fori_loop(0, nq // diag_unroll, diag_body, 0)

    def full_body(t, carry):
        for u in range(unroll):
            full_tile(ti_ref[unroll * t + u], tj_ref[unroll * t + u])
        return carry
    lax.fori_loop(0, n_tiles // unroll, full_body, 0)
    for t in range(n_tiles - n_tiles % unroll, n_tiles):
        full_tile(ti_ref[t], tj_ref[t])

    lv = lv_ref[...]
    lam = (jnp.exp(jnp.sum(lv[0:1] * lv[1:2], axis=-1, keepdims=True))
           - jnp.exp(jnp.sum(lv[2:3] * lv[3:4], axis=-1, keepdims=True)) + LAM_INIT)

    def out_body(i, carry):
        rows = block(i)
        a1 = acc1[rows, :]
        a2 = acc2[rows, :]
        o = a1[:, :DIFF_VDIM] / a1[:, DIFF_VDIM:] - lam * (a2[:, :DIFF_VDIM] / a2[:, DIFF_VDIM:])
        o_ref[rows, :] = (_rms(o, sg_ref[...]) * (1.0 - LAM_INIT)).astype(BF16)
        return carry
    lax.fori_loop(0, nq, out_body, 0)


def _diffattn(lvec, qk, mid, subln_g, *, B, S, tq, bounded):
    T = B * S
    nq = S // tq
    assert nq % 2 == 0
    tiles = [(i, j) for i in range(nq) for j in range(i)]
    ti = jnp.asarray([t[0] for t in tiles], I32)
    tj = jnp.asarray([t[1] for t in tiles], I32)
    kern = functools.partial(_diffattn_kernel, tq=tq, n_tiles=len(tiles), bounded=bounded)
    m_rows = SUBLANES if bounded else S
    grid_spec = pltpu.PrefetchScalarGridSpec(
        num_scalar_prefetch=2,
        grid=(B, DIFF_HEADS),
        in_specs=[
            pl.BlockSpec((4, DIFF_QKDIM), lambda b, h, *_: (0, 0)),
            pl.BlockSpec((S, LANES), lambda b, h, *_: (b, h)),
            pl.BlockSpec((S, LANES), lambda b, h, *_: (b, DIFF_HEADS + h)),
            pl.BlockSpec((S, LANES), lambda b, h, *_: (b, h)),
            pl.BlockSpec((1, DIFF_VDIM), lambda b, h, *_: (0, 0)),
        ],
        out_specs=pl.BlockSpec((S, DIFF_VDIM), lambda b, h, *_: (b, h)),
        scratch_shapes=[
            pltpu.VMEM((S, 2 * DIFF_VDIM), BF16),
            pltpu.VMEM((tq, tq), BF16),
            pltpu.VMEM((S, 2 * DIFF_VDIM), F32),
            pltpu.VMEM((S, 2 * DIFF_VDIM), F32),
            pltpu.VMEM((m_rows, 1), F32),
            pltpu.VMEM((m_rows, 1), F32),
        ],
    )
    return pl.pallas_call(
        kern,
        grid_spec=grid_spec,
        out_shape=jax.ShapeDtypeStruct((T, DIFF_HEADS * DIFF_VDIM), BF16),
        compiler_params=_cparams(("parallel", "parallel"), 40),
        name="diffattn_bounded" if bounded else "diffattn_online",
    )(ti, tj, lvec, qk, qk, mid, subln_g)


def _gla_kernel(q_ref, k_ref, v_ref, la_ref, sgr_ref, g_ref, scan_ref, o_ref, state, *, blk):
    @pl.when(pl.program_id(2) == 0)
    def _():
        state[...] = jnp.zeros(state.shape, F32)

    la_t = la_ref[...].T
    k_t = k_ref[...].astype(F32).T
    hi, lo = _split_bf16(la_t)
    scan = scan_ref[...]
    cums, tots = [], []
    for sb in range(blk // LANES):
        slab = slice(sb * LANES, (sb + 1) * LANES)
        r = _dot(hi[:, slab], scan) + _dot(lo[:, slab], scan)
        cums.append(r[:, :LANES])
        tots.append(r[:, LANES:])
    cum_t = jnp.concatenate(cums, axis=1)
    tot_t = jnp.concatenate(tots, axis=1)
    kd_t = k_t * jnp.exp(tot_t - cum_t)

    n_chunks = blk // CHUNK
    lane = lax.broadcasted_iota(I32, (1, LANES), 1)
    d_states = []
    for ck in range(n_chunks):
        pair = slice((ck // 2) * LANES, (ck // 2 + 1) * LANES)
        in_chunk = (lane // CHUNK) == (ck % 2)
        kd = jnp.where(in_chunk, kd_t[:, pair], 0.0).astype(BF16)
        d_states.append(_dot(kd, v_ref[pair, :]))

    st = state[...]
    states = []
    for ck in range(n_chunks):
        decay = jnp.exp(tot_t[:, ck * CHUNK:ck * CHUNK + 1])
        st = decay * st + d_states[ck]
        states.append(st.astype(BF16))
    state[...] = st

    o = jnp.concatenate([_dot(q_ref[ck * CHUNK:(ck + 1) * CHUNK, :], states[ck]) for ck in range(n_chunks)],
                        axis=0) * (GLA_KDIM ** -0.5)
    o_ref[...] = (_rms(o, g_ref[...]) * sgr_ref[...].astype(F32)).astype(BF16)


def _gla(mid, log_a, sgr, out_g, *, B, S, blk):
    T = B * S
    ns = S // blk
    kern = functools.partial(_gla_kernel, blk=blk)
    q_col0 = (DIFF_HEADS * DIFF_VDIM) // GLA_KDIM
    k_col0 = q_col0 + GLA_HEADS
    v_col0 = (DIFF_HEADS * DIFF_VDIM + 2 * GLA_HEADS * GLA_KDIM) // GLA_VDIM
    assert LANES % CHUNK == 0 and blk % LANES == 0
    r = jnp.arange(LANES, dtype=I32)[:, None]
    c = jnp.arange(LANES, dtype=I32)[None, :]
    same = (r // CHUNK) == (c // CHUNK)
    scan = jnp.concatenate([(same & (r <= c)).astype(BF16), same.astype(BF16)], axis=1)
    return pl.pallas_call(
        kern,
        grid=(B, GLA_HEADS, ns),
        in_specs=[
            pl.BlockSpec((blk, GLA_KDIM), lambda b, h, s: (b * ns + s, q_col0 + h)),
            pl.BlockSpec((blk, GLA_KDIM), lambda b, h, s: (b * ns + s, k_col0 + h)),
            pl.BlockSpec((blk, GLA_VDIM), lambda b, h, s: (b * ns + s, v_col0 + h)),
            pl.BlockSpec((blk, GLA_KDIM), lambda b, h, s: (b * ns + s, h)),
            pl.BlockSpec((blk, GLA_VDIM), lambda b, h, s: (b * ns + s, h)),
            pl.BlockSpec((1, GLA_VDIM), lambda b, h, s: (0, 0)),
            pl.BlockSpec((LANES, 2 * LANES), lambda b, h, s: (0, 0)),
        ],
        out_specs=pl.BlockSpec((blk, GLA_VDIM), lambda b, h, s: (b * ns + s, h)),
        out_shape=jax.ShapeDtypeStruct((T, GLA_HEADS * GLA_VDIM), BF16),
        scratch_shapes=[pltpu.VMEM((GLA_KDIM, GLA_VDIM), F32)],
        compiler_params=_cparams(("parallel", "parallel", "arbitrary"), 32),
        name="gla",
    )(mid, mid, mid, log_a, sgr, out_g, scan)


def _resident_w_map(n_j):
    return lambda i, j: (0, jnp.where(i == 0, j, n_j - 1))


def _outproj_kernel(a_ref, b_ref, wa_ref, wb_ref, x_ref, o_ref, w_scr):
    j = pl.program_id(1)
    ka = a_ref.shape[1]

    @pl.when(pl.program_id(0) == 0)
    def _():
        w_scr[j, 0:ka, :] = wa_ref[...].astype(BF16)
        w_scr[j, ka:, :] = wb_ref[...].astype(BF16)

    acc = _dot(a_ref[...], w_scr[j, 0:ka, :]) + _dot(b_ref[...], w_scr[j, ka:, :])
    o_ref[...] = x_ref[...] + acc


def _outproj(a, b, w_out, x2, *, tm, tn):
    T, ka = a.shape
    kb = b.shape[1]
    assert ka == kb
    D = w_out.shape[1]
    n_j = D // tn
    return pl.pallas_call(
        _outproj_kernel,
        grid=(T // tm, n_j),
        in_specs=[
            pl.BlockSpec((tm, ka), lambda i, j: (i, 0)),
            pl.BlockSpec((tm, kb), lambda i, j: (i, 0)),
            pl.BlockSpec((ka, tn), _resident_w_map(n_j)),
            pl.BlockSpec((kb, tn), lambda i, j: (1, jnp.where(i == 0, j, n_j - 1))),
            pl.BlockSpec((tm, tn), lambda i, j: (i, j)),
        ],
        out_specs=pl.BlockSpec((tm, tn), lambda i, j: (i, j)),
        out_shape=jax.ShapeDtypeStruct((T, D), F32),
        scratch_shapes=[pltpu.VMEM((n_j, ka + kb, tn), BF16)],
        compiler_params=_cparams(("arbitrary", "arbitrary"), 48),
        name="outproj",
    )(a, b, w_out, w_out, x2)


def _normproj_kernel(x_ref, g_ref, w_ref, hg_ref, o_ref, n_scr, *w_scr, n_norm):
    j = pl.program_id(1)

    @pl.when(j == 0)
    def _():
        n_scr[...] = _rms(x_ref[...], g_ref[...]).astype(BF16)

    if w_scr:
        @pl.when(pl.program_id(0) == 0)
        def _():
            w_scr[0][j] = w_ref[...].astype(BF16)
        y = _dot(n_scr[...], w_scr[0][j])
    else:
        y = _dot(n_scr[...], w_ref[...].astype(BF16))

    @pl.when(j < n_norm)
    def _():
        o_ref[...] = _rms(y, hg_ref[...]).astype(BF16)

    @pl.when(j >= n_norm)
    def _():
        o_ref[...] = y.astype(BF16)


def _normproj(x2, g, w, head_g, *, tm, tn, n_norm, name):
    T, D = x2.shape
    N = w.shape[1]
    kern = functools.partial(_normproj_kernel, n_norm=n_norm)
    n_j = N // tn
    resident = T // tm > 1
    return pl.pallas_call(
        kern,
        grid=(T // tm, n_j),
        in_specs=[
            pl.BlockSpec((tm, D), lambda i, j: (i, 0)),
            pl.BlockSpec((1, D), lambda i, j: (0, 0)),
            pl.BlockSpec((D, tn), _resident_w_map(n_j) if resident else (lambda i, j: (0, j))),
            pl.BlockSpec((1, tn), lambda i, j: (0, 0)),
        ],
        out_specs=pl.BlockSpec((tm, tn), lambda i, j: (i, j)),
        out_shape=jax.ShapeDtypeStruct((T, N), BF16),
        scratch_shapes=[pltpu.VMEM((tm, D), BF16)] + ([pltpu.VMEM((n_j, D, tn), BF16)] if resident else []),
        compiler_params=_cparams(("arbitrary", "arbitrary"), 48),
        name=name,
    )(x2, g, w, head_g)


def _cross_kernel(q_ref, k_ref, v_ref, w_ref, h_ref, o_ref, att_scr, w_scr, *, hdim):
    j = pl.program_id(1)

    @pl.when(pl.program_id(0) == 0)
    def _():
        w_scr[j] = w_ref[...].astype(BF16)

    @pl.when(j == 0)
    def _():
        for hd in range(CROSS_HEADS):
            cols = slice(hd * hdim, (hd + 1) * hdim)
            s = lax.dot_general(q_ref[:, cols], k_ref[:, cols], NT_DIMS, preferred_element_type=F32)
            p = jnp.exp(s - jnp.max(s, axis=-1, keepdims=True))
            l = jnp.sum(p, axis=-1, keepdims=True)
            att_scr[:, cols] = (_dot(p.astype(BF16), v_ref[:, cols]) / l).astype(BF16)

    o_ref[...] = h_ref[...] + _dot(att_scr[...], w_scr[j])


def _cross(qc, kv, w_co, h1, *, S, n_mem, tm, tn):
    T, D = qc.shape
    per_b = S // tm
    kern = functools.partial(_cross_kernel, hdim=D // CROSS_HEADS)
    n_j = D // tn
    return pl.pallas_call(
        kern,
        grid=(T // tm, n_j),
        in_specs=[
            pl.BlockSpec((tm, D), lambda i, j: (i, 0)),
            pl.BlockSpec((n_mem, D), lambda i, j: (i // per_b, 0)),
            pl.BlockSpec((n_mem, D), lambda i, j: (i // per_b, 1)),
            pl.BlockSpec((D, tn), _resident_w_map(n_j)),
            pl.BlockSpec((tm, tn), lambda i, j: (i, j)),
        ],
        out_specs=pl.BlockSpec((tm, tn), lambda i, j: (i, j)),
        out_shape=jax.ShapeDtypeStruct((T, D), F32),
        scratch_shapes=[pltpu.VMEM((tm, D), BF16), pltpu.VMEM((n_j, D, tn), BF16)],
        compiler_params=_cparams(("arbitrary", "arbitrary"), 48),
        name="cross",
    )(qc, kv, kv, w_co, h1)


R_ROWS = SUBLANES + N_EXPERTS


def _pack_bf16_pairs(xb16):
    c = xb16.shape[1] // 2
    u = lax.bitcast_convert_type(xb16.astype(F32), jnp.uint32)
    return (u[:, :c] >> 16) | (u[:, c:] & jnp.uint32(0xFFFF0000))


def _store_row_tiles(ref, x):
    for g in range(SUBLANES):
        ref[:, g, :] = x[:, g * LANES:(g + 1) * LANES]


def _load_row_tiles(ref):
    return jnp.concatenate([ref[:, g, :] for g in range(SUBLANES)], axis=1)


def _unpack_bf16_pairs_f32(w):
    lo = lax.bitcast_convert_type(w << 16, F32)
    hi = lax.bitcast_convert_type(w & jnp.uint32(0xFFFF0000), F32)
    return lo, hi


def _unpack_bf16_pairs(w):
    lo, hi = _unpack_bf16_pairs_f32(w)
    return lo.astype(BF16), hi.astype(BF16)


def _router_kernel(h_ref, g_ref, wt_ref, b_ref, eid_ref, gate_ref, xn_ref):
    n = _rms(h_ref[...], g_ref[...])
    nh, nl = _split_bf16(n)
    _store_row_tiles(xn_ref, _pack_bf16_pairs(nh))
    wh, wl = _split_bf16(wt_ref[...])
    nt = functools.partial(lax.dot_general, dimension_numbers=NT_DIMS, preferred_element_type=F32)
    lg = nt(wh, nh) + nt(wh, nl) + nt(wl, nh) + b_ref[...]

    tm = lg.shape[1]
    row = lax.broadcasted_iota(I32, (SUBLANES, tm), 0)

    def first_argmax(v, vmax):
        return jnp.min(jnp.where(v == vmax, row, SUBLANES), axis=0, keepdims=True)

    gl = jnp.where(row < N_GROUPS, lg[0:SUBLANES], NEG_INF)
    gmax = jnp.max(gl, axis=0, keepdims=True)
    grp = first_argmax(gl, gmax)
    grp_w = 1.0 / jnp.sum(jnp.exp(gl - gmax), axis=0, keepdims=True)

    sel = jnp.zeros((SUBLANES, tm), F32)
    for gi in range(N_GROUPS):
        lo = SUBLANES + gi * EXPERTS_PER_GROUP
        sel = jnp.where(grp == gi, lg[lo:lo + EXPERTS_PER_GROUP], sel)
    e = jnp.exp(sel - jnp.max(sel, axis=0, keepdims=True))
    prob = e / jnp.sum(e, axis=0, keepdims=True)
    p1 = jnp.max(prob, axis=0, keepdims=True)
    i1 = first_argmax(prob, p1)
    rest = jnp.where(row == i1, -1.0, prob)
    p2 = jnp.max(rest, axis=0, keepdims=True)
    i2 = first_argmax(rest, p2)
    den = p1 + p2
    base = grp * EXPERTS_PER_GROUP
    eid_ref[...] = jnp.where(row == 0, base + i1, jnp.where(row == 1, base + i2, 0))
    gate_ref[...] = jnp.where(row == 0, grp_w * p1 / den, jnp.where(row == 1, grp_w * p2 / den, 0.0))


def _router(h2, g_ffn, w_rt, b_r, *, tm):
    T, D = h2.shape
    return pl.pallas_call(
        _router_kernel,
        grid=(T // tm,),
        in_specs=[
            pl.BlockSpec((tm, D), lambda i: (i, 0)),
            pl.BlockSpec((1, D), lambda i: (0, 0)),
            pl.BlockSpec((R_ROWS, D), lambda i: (0, 0)),
            pl.BlockSpec((R_ROWS, 1), lambda i: (0, 0)),
        ],
        out_specs=[
            pl.BlockSpec((SUBLANES, tm), lambda i: (0, i)),
            pl.BlockSpec((SUBLANES, tm), lambda i: (0, i)),
            pl.BlockSpec((tm, SUBLANES, LANES), lambda i: (i, 0, 0)),
        ],
        out_shape=[
            jax.ShapeDtypeStruct((SUBLANES, T), I32),
            jax.ShapeDtypeStruct((SUBLANES, T), F32),
            jax.ShapeDtypeStruct((T, SUBLANES, LANES), jnp.uint32),
        ],
        compiler_params=_cparams(("parallel",), 32),
        name="router",
    )(h2, g_ffn, w_rt, b_r)


def _rank_kernel(eid_ref, rank_ref, cnt_ref, carry):
    @pl.when(pl.program_id(0) == 0)
    def _():
        carry[...] = jnp.zeros(carry.shape, F32)

    tm = eid_ref.shape[1]
    e0 = eid_ref[0:1, :]
    e1 = eid_ref[1:2, :]
    erow = lax.broadcasted_iota(I32, (N_EXPERTS, tm), 0)
    hit = jnp.where((erow == e0) | (erow == e1), 1.0, 0.0)
    r = lax.broadcasted_iota(I32, (tm, tm), 0)
    c = lax.broadcasted_iota(I32, (tm, tm), 1)
    before = jnp.where(r < c, 1.0, 0.0).astype(BF16)
    pre = _dot(hit.astype(BF16), before) + carry[:, 0:1]
    rank0 = jnp.sum(jnp.where(erow == e0, pre, 0.0), axis=0, keepdims=True)
    rank1 = jnp.sum(jnp.where(erow == e1, pre, 0.0), axis=0, keepdims=True)
    row = lax.broadcasted_iota(I32, (SUBLANES, tm), 0)
    rank_ref[...] = jnp.where(row == 0, rank0, jnp.where(row == 1, rank1, 0.0)).astype(I32)
    total = carry[...] + jnp.sum(hit, axis=1, keepdims=True)
    carry[...] = total
    cnt_ref[...] = total.astype(I32)


def _rank(eid, *, tm):
    T = eid.shape[1]
    return pl.pallas_call(
        _rank_kernel,
        grid=(T // tm,),
        in_specs=[pl.BlockSpec((SUBLANES, tm), lambda i: (0, i))],
        out_specs=[
            pl.BlockSpec((SUBLANES, tm), lambda i: (0, i)),
            pl.BlockSpec((N_EXPERTS, LANES), lambda i: (0, 0)),
        ],
        out_shape=[
            jax.ShapeDtypeStruct((SUBLANES, T), I32),
            jax.ShapeDtypeStruct((N_EXPERTS, LANES), I32),
        ],
        scratch_shapes=[pltpu.VMEM((N_EXPERTS, LANES), F32)],
        compiler_params=_cparams(("arbitrary",), 32),
        name="rank",
    )(eid)


def _dispatch_kernel(dest_ref, pad0_ref, npad_ref, nb_ref, xn_ref, xb_hbm, zbuf, sem, psem,
                     *, n_tok, tb, n_blocks):
    i = pl.program_id(0)
    tm = xn_ref.shape[0]
    base = i * tm

    def row_body(r, carry):
        for k in range(TOP_K):
            pltpu.make_async_copy(xn_ref.at[r], xb_hbm.at[dest_ref[k * n_tok + base + r]], sem).start(priority=k)
        return carry
    lax.fori_loop(0, tm, row_body, 0, unroll=8)

    @pl.when(i == 0)
    def _():
        _dispatch_fill(pad0_ref, npad_ref, nb_ref, xb_hbm, zbuf, psem, tb=tb, n_blocks=n_blocks)

    for k in range(TOP_K):
        pltpu.make_async_copy(xn_ref, xb_hbm.at[pl.ds(0, tm)], sem).wait()


def _dispatch_fill(pad0_ref, npad_ref, nb_ref, xb_hbm, zbuf, psem, *, tb, n_blocks):
    zbuf[...] = jnp.zeros(zbuf.shape, zbuf.dtype)

    def pad_copy(e):
        n = npad_ref[e]
        return pltpu.make_async_copy(zbuf.at[pl.ds(0, n)], xb_hbm.at[pl.ds(pad0_ref[e], n)], psem.at[0])

    def tail_copy(blk):
        return pltpu.make_async_copy(zbuf, xb_hbm.at[pl.ds(pl.multiple_of(blk * tb, tb), tb)], psem.at[1])

    def for_each_pad(fn):
        def body(e, c):
            @pl.when(npad_ref[e] > 0)
            def _():
                fn(e)
            return c
        lax.fori_loop(0, N_EXPERTS, body, 0)

    def for_each_tail(fn):
        def body(b, c):
            fn(b)
            return c
        lax.fori_loop(nb_ref[0], n_blocks, body, 0)

    for_each_pad(lambda e: pad_copy(e).start())
    for_each_tail(lambda b: tail_copy(b).start())
    for_each_pad(lambda e: pad_copy(e).wait())
    for_each_tail(lambda b: tail_copy(b).wait())


def _dispatch(dest, pad0, npad, nb, xn, *, tb, n_blocks, tm):
    T = xn.shape[0]
    tile = xn.shape[1:]
    kern = functools.partial(_dispatch_kernel, n_tok=T, tb=tb, n_blocks=n_blocks)
    grid_spec = pltpu.PrefetchScalarGridSpec(
        num_scalar_prefetch=4,
        grid=(T // tm,),
        in_specs=[pl.BlockSpec((tm,) + tile, lambda i, *_: (i, 0, 0))],
        out_specs=pl.BlockSpec(memory_space=pl.ANY),
        scratch_shapes=[
            pltpu.VMEM((tb,) + tile, xn.dtype),
            pltpu.SemaphoreType.DMA(()),
            pltpu.SemaphoreType.DMA((2,)),
        ],
    )
    return pl.pallas_call(
        kern,
        grid_spec=grid_spec,
        out_shape=jax.ShapeDtypeStruct((n_blocks * tb,) + tile, xn.dtype),
        compiler_params=_cparams(("arbitrary",), 32),
        name="dispatch",
    )(dest, pad0, npad, nb, xn)


def _expert_kernel(be_ref, first_ref, slot_ref, nxt_ref, run1_ref, nb_ref, x_ref, wg_hbm, wu_hbm, wd_hbm, y_ref,
                   wg_f, wu_f, wd_f, wsem, wg_b, wu_b, wd_b):
    i = pl.program_id(0)
    nb = nb_ref[0]

    def weight_copies(e, slot):
        return (pltpu.make_async_copy(wg_hbm.at[e], wg_f.at[slot], wsem.at[3 * slot]),
                pltpu.make_async_copy(wu_hbm.at[e], wu_f.at[slot], wsem.at[3 * slot + 1]),
                pltpu.make_async_copy(wd_hbm.at[e], wd_f.at[slot], wsem.at[3 * slot + 2]))

    @pl.when(i == 0)
    def _():
        for cp in weight_copies(be_ref[0], 0):
            cp.start()

    @pl.when((i == 0) & (run1_ref[0] >= 0))
    def _():
        for cp in weight_copies(jnp.maximum(run1_ref[0], 0), 1):
            cp.start()

    first = (i < nb) & (first_ref[i] == 1)
    slot = slot_ref[i]

    @pl.when(first)
    def _():
        for cp in weight_copies(0, slot):
            cp.wait()
        wg_b[...] = wg_f[slot].astype(BF16)
        wu_b[...] = wu_f[slot].astype(BF16)
        wd_b[...] = wd_f[slot].astype(BF16)

    @pl.when(first & (nxt_ref[i] >= 0))
    def _():
        for cp in weight_copies(jnp.maximum(nxt_ref[i], 0), slot):
            cp.start()

    @pl.when(i < nb)
    def _():
        n_lo, n_hi = _unpack_bf16_pairs(_load_row_tiles(x_ref))
        half = n_lo.shape[1]
        a = _dot(n_lo, wg_b[0:half, :]) + _dot(n_hi, wg_b[half:, :])
        u = _dot(n_lo, wu_b[0:half, :]) + _dot(n_hi, wu_b[half:, :])
        hdn = (a / (1.0 + jnp.exp(-a))) * u
        y_ref[...] = _pack_bf16_pairs(_dot(hdn.astype(BF16), wd_b[...]).astype(BF16))

    @pl.when(i >= nb)
    def _():
        y_ref[...] = jnp.zeros(y_ref.shape, y_ref.dtype)


def _experts(xb, blk_e, blk_first, blk_slot, blk_next, run1, nb, w_gate, w_up, w_down, *, tb, n_blocks):
    D, De = w_gate.shape[1:]
    assert xb.shape[1:] == (SUBLANES, LANES) and D == 2 * SUBLANES * LANES

    def x_map(i, be, first, slot, nxt, r1, nbr):
        return (jnp.minimum(i, nbr[0] - 1), 0, 0)

    grid_spec = pltpu.PrefetchScalarGridSpec(
        num_scalar_prefetch=6,
        grid=(n_blocks,),
        in_specs=[
            pl.BlockSpec((tb, SUBLANES, LANES), x_map),
            pl.BlockSpec(memory_space=pl.ANY),
            pl.BlockSpec(memory_space=pl.ANY),
            pl.BlockSpec(memory_space=pl.ANY),
        ],
        out_specs=pl.BlockSpec((tb, D // 2), lambda i, *_: (i, 0)),
        scratch_shapes=[
            pltpu.VMEM((2, D, De), F32),
            pltpu.VMEM((2, D, De), F32),
            pltpu.VMEM((2, De, D), F32),
            pltpu.SemaphoreType.DMA((6,)),
            pltpu.VMEM((D, De), BF16),
            pltpu.VMEM((D, De), BF16),
            pltpu.VMEM((De, D), BF16),
        ],
    )
    return pl.pallas_call(
        _expert_kernel,
        grid_spec=grid_spec,
        out_shape=jax.ShapeDtypeStruct((n_blocks * tb, D // 2), jnp.uint32),
        compiler_params=_cparams(("arbitrary",), 52),
        name="experts",
    )(blk_e, blk_first, blk_slot, blk_next, run1, nb, xb, w_gate, w_up, w_down)


def _combine_kernel(dest_ref, y_hbm, h_ref, gate_ref, o_ref, ybuf, sem, *, tm, n_tok):
    i = pl.program_id(0)
    n = pl.num_programs(0)

    def row_copy(d, k, r, slot):
        return pltpu.make_async_copy(y_hbm.at[pl.ds(d, 1), :], ybuf.at[slot, k, pl.ds(r, 1), :], sem.at[slot])

    def start_gather(blk, slot):
        def body(r, carry):
            for k in range(TOP_K):
                row_copy(dest_ref[k * n_tok + blk * tm + r], k, r, slot).start(priority=k)
            return carry
        lax.fori_loop(0, tm, body, 0, unroll=8)

    def wait_gather(slot):
        for k in range(TOP_K):
            pltpu.make_async_copy(y_hbm.at[pl.ds(0, tm), :], ybuf.at[slot, k], sem.at[slot]).wait()

    @pl.when(i == 0)
    def _():
        start_gather(0, 0)

    @pl.when(i + 1 < n)
    def _():
        start_gather(i + 1, (i + 1) % 2)

    slot = i % 2
    wait_gather(slot)
    gt = gate_ref[...]
    half = h_ref.shape[1] // 2
    y0_lo, y0_hi = _unpack_bf16_pairs_f32(ybuf[slot, 0])
    y1_lo, y1_hi = _unpack_bf16_pairs_f32(ybuf[slot, 1])
    o_ref[:, 0:half] = h_ref[:, 0:half] + gt[:, 0:1] * y0_lo + gt[:, 1:2] * y1_lo
    o_ref[:, half:] = h_ref[:, half:] + gt[:, 0:1] * y0_hi + gt[:, 1:2] * y1_hi


def _combine(dest, yb, h2, gate, *, tm):
    T, D = h2.shape
    kern = functools.partial(_combine_kernel, tm=tm, n_tok=T)
    grid_spec = pltpu.PrefetchScalarGridSpec(
        num_scalar_prefetch=1,
        grid=(T // tm,),
        in_specs=[
            pl.BlockSpec(memory_space=pl.ANY),
            pl.BlockSpec((tm, D), lambda i, d: (i, 0)),
            pl.BlockSpec((tm, TOP_K), lambda i, d: (i, 0)),
        ],
        out_specs=pl.BlockSpec((tm, D), lambda i, d: (i, 0)),
        scratch_shapes=[
            pltpu.VMEM((2, TOP_K, tm, D // 2), jnp.uint32),
            pltpu.SemaphoreType.DMA((2,)),
        ],
    )
    return pl.pallas_call(
        kern,
        grid_spec=grid_spec,
        out_shape=jax.ShapeDtypeStruct((T, D), F32),
        compiler_params=_cparams(("arbitrary",), 40),
        name="combine",
    )(dest, yb, h2, gate)


class _Tiles(NamedTuple):
    proj_rows: int = 1024
    proj_cols: int = 512
    attn_rows: int = 512
    gla_rows: int = 2048
    token_rows: int = 512
    expert_rows: int = 256
    cast_rows: int = 560


TILES = _Tiles()


def kernel(x, mem, positions, g_attn, w_in, q_norm_g, k_norm_g, lambda_q1, lambda_k1, lambda_q2, lambda_k2, diff_subln_g, gla_w_a2, gla_b_a, gla_out_g, w_out, g_cross, g_mem, w_cq, w_ckv, cq_norm_g, ck_norm_g, w_co, g_ffn, w_router_grp, b_router_grp, w_router_exp, b_router_exp, w_gate, w_up, w_down):
    B, S, D = x.shape
    T = B * S
    n_mem = mem.shape[1]
    l = 0
    x2 = x.reshape(T, D)

    half = DIFF_QKDIM // 2
    freq = ROPE_THETA ** (-jnp.arange(half, dtype=F32) / half)
    freq = jnp.tile(freq, LANES // half)[None, :]
    q_scale = math.log2(math.e) * DIFF_QKDIM ** -0.5
    qkg = jnp.stack([jnp.tile(q_norm_g[l], 2) * q_scale, jnp.tile(k_norm_g[l], 2)])
    score_bound = 1.01 * DIFF_QKDIM * q_scale * jnp.max(jnp.abs(q_norm_g[l])) * jnp.max(jnp.abs(k_norm_g[l]))
    lvec = jnp.stack([lambda_q1[l], lambda_k1[l], lambda_q2[l], lambda_k2[l]])

    w_in_t = _to_bf16(w_in[l].T, rows=TILES.cast_rows)
    qk, mid, log_a, sgr = _inproj(x2, g_attn[l][None], positions.reshape(T, 1), freq, qkg, w_in_t,
                                  gla_w_a2[l], gla_b_a[l][None], tm=TILES.proj_rows)
    diffattn = functools.partial(_diffattn, lvec, qk, mid, diff_subln_g[l][None], B=B, S=S, tq=TILES.attn_rows)
    mix_d = lax.cond(score_bound <= SCORE_BOUND,
                     functools.partial(diffattn, bounded=True), functools.partial(diffattn, bounded=False))
    mix_g = _gla(mid, log_a, sgr, gla_out_g[l][None], B=B, S=S, blk=TILES.gla_rows)
    h1 = _outproj(mix_d, mix_g, w_out[l], x2, tm=TILES.proj_rows, tn=TILES.proj_cols)

    hdim = D // CROSS_HEADS
    qc = _normproj(h1, g_cross[l][None], w_cq[l], cq_norm_g[l][None] * (hdim ** -0.5),
                   tm=TILES.proj_rows, tn=hdim, n_norm=CROSS_HEADS, name="cq")
    kv = _normproj(mem.reshape(B * n_mem, D), g_mem[l][None], w_ckv[l], ck_norm_g[l][None],
                   tm=B * n_mem, tn=hdim, n_norm=CROSS_HEADS, name="ckv")
    h2 = _cross(qc, kv, w_co[l], h1, S=S, n_mem=n_mem, tm=TILES.proj_rows, tn=TILES.proj_cols)

    w_rt = jnp.concatenate([w_router_grp[l].T, jnp.zeros((SUBLANES - N_GROUPS, D), F32), w_router_exp[l].T])
    b_r = jnp.concatenate([b_router_grp[l], jnp.zeros((SUBLANES - N_GROUPS,), F32), b_router_exp[l]])[:, None]
    eid, gate, xn = _router(h2, g_ffn[l][None], w_rt, b_r, tm=TILES.token_rows)
    rank, cnt = _rank(eid, tm=TILES.token_rows)

    assert TOP_K == 2
    tb = TILES.expert_rows
    n_blocks = (T * TOP_K + N_EXPERTS * (tb - 1) + tb - 1) // tb
    counts = cnt[:, 0]
    pcounts = ((counts + tb - 1) // tb) * tb
    pends = jnp.cumsum(pcounts)
    pstarts = pends - pcounts
    nb = (pends[-1:] // tb).astype(I32)
    eids = jnp.arange(N_EXPERTS, dtype=I32)
    blk_start = jnp.arange(n_blocks, dtype=I32) * tb
    blk_e = jnp.minimum(jnp.sum(pends[None, :] <= blk_start[:, None], axis=1), N_EXPERTS - 1).astype(I32)
    blk_first = jnp.concatenate([jnp.ones((1,), I32), (blk_e[1:] != blk_e[:-1]).astype(I32)])
    used = jnp.where(counts > 0, eids, N_EXPERTS)
    next_used = jnp.concatenate([lax.cummin(used[::-1])[::-1][1:], jnp.full((1,), N_EXPERTS, I32)])
    next_used = jnp.where(next_used < N_EXPERTS, next_used, -1)

    def table_at(idx, table):
        return jnp.where(idx >= 0, jnp.sum(jnp.where(idx[:, None] == eids, table, 0), axis=1), -1).astype(I32)

    after_next = table_at(next_used, next_used)
    run_of = (jnp.cumsum((counts > 0).astype(I32)) - 1).astype(I32)
    blk_next = table_at(blk_e, after_next)
    blk_slot = table_at(blk_e, run_of % 2)
    run1 = table_at(blk_e[:1], next_used)
    pick = eid[:TOP_K]
    pstart_of = jnp.sum(jnp.where(pick[..., None] == eids, pstarts, 0), axis=-1)
    dest = (pstart_of + rank[:TOP_K]).astype(I32).reshape(-1)
    pad0 = (pstarts + counts).astype(I32)
    npad = (pcounts - counts).astype(I32)

    xb = _dispatch(dest, pad0, npad, nb, xn, tb=tb, n_blocks=n_blocks, tm=TILES.token_rows)
    yb = _experts(xb, blk_e, blk_first, blk_slot, blk_next, run1, nb, w_gate[l], w_up[l], w_down[l],
                  tb=tb, n_blocks=n_blocks)
    out = _combine(dest, yb, h2, gate[:TOP_K].T, tm=TILES.token_rows)
    return out.reshape(B, S, D)
```

```python
import functools
import math
from typing import NamedTuple

import jax
import jax.numpy as jnp
from jax import lax
from jax.experimental import pallas as pl
from jax.experimental.pallas import tpu as pltpu

F32 = jnp.float32
BF16 = jnp.bfloat16
I32 = jnp.int32

LANES = 128
SUBLANES = 8

CHUNK = 64
ROPE_THETA = 10000.0
NORM_EPS = 1e-6
NEG_INF = -1e30
DIFF_HEADS = 8
DIFF_VDIM = 128
DIFF_QKDIM = 64
GLA_HEADS = 4
GLA_VDIM = 256
GLA_KDIM = 128
GLA_GATE_RANK = 16
GLA_TAU = 16.0
CROSS_HEADS = 4
N_GROUPS = 4
EXPERTS_PER_GROUP = 8
N_EXPERTS = N_GROUPS * EXPERTS_PER_GROUP
TOP_K = 2
LAM_INIT = 0.8 - 0.6 * math.exp(-0.3 * 0)

NT_DIMS = (((1,), (1,)), ((), ()))


def _cparams(semantics, vmem_mib):
    return pltpu.CompilerParams(dimension_semantics=semantics,
                                vmem_limit_bytes=vmem_mib * 1024 * 1024)


def _dot(a, b):
    return jnp.dot(a, b, preferred_element_type=F32)


def _dot_nt(a, b):
    return lax.dot_general(a, b, NT_DIMS, preferred_element_type=F32)


def _rms(x, g):
    ms = jnp.mean(x * x, axis=-1, keepdims=True)
    return x * lax.rsqrt(ms + NORM_EPS) * g


def _split_bf16(x):
    hi = x.astype(BF16)
    lo = (x - hi.astype(F32)).astype(BF16)
    return hi, lo


TN = 512
J_QK = 4
J_MID = 6
J_LR = J_QK + J_MID
J_GR = J_LR + 1
N_J = J_GR + 2


def _inproj_kernel(x_ref, g_ref, pos_ref, freq_ref, qkg_ref, w_ref, wgr_ref, wlr_ref, wa2_ref, ba_ref,
                   qk_ref, mid_ref, loga_ref, sgr_ref, n_scr, cos_scr, sin_scr, y_scr):
    j = pl.program_id(1)

    @pl.when(j == 0)
    def _():
        n_scr[...] = _rms(x_ref[...], g_ref[...]).astype(BF16)
        ang = pos_ref[...].astype(F32) * freq_ref[...]
        cos_scr[...] = jnp.cos(ang)
        sin_scr[...] = jnp.sin(ang)

    def qk_epilogue(jq):
        y_prev = y_scr.at[jq % 2]
        lane = lax.broadcasted_iota(I32, (1, LANES), 1)
        low_seg = lane < DIFF_QKDIM
        first_half = (lane % DIFF_QKDIM) < (DIFF_QKDIM // 2)
        gain = qkg_ref[jq // (J_QK // 2):jq // (J_QK // 2) + 1, :]
        cos = cos_scr[...]
        sin = sin_scr[...]
        for c in range(TN // LANES):
            yb = y_prev[:, c * LANES:(c + 1) * LANES]
            y2 = yb * yb
            s_lo = jnp.sum(jnp.where(low_seg, y2, 0.0), axis=-1, keepdims=True)
            s_hi = jnp.sum(jnp.where(low_seg, 0.0, y2), axis=-1, keepdims=True)
            ms = jnp.where(low_seg, s_lo, s_hi) * (1.0 / DIFF_QKDIM)
            yn = yb * lax.rsqrt(ms + NORM_EPS) * gain
            rot = jnp.where(first_half,
                            -pltpu.roll(yn, LANES - DIFF_QKDIM // 2, 1),
                            pltpu.roll(yn, DIFF_QKDIM // 2, 1))
            qk_ref[:, c * LANES:(c + 1) * LANES] = (yn * cos + rot * sin).astype(BF16)

    for jq in range(J_QK + 1):
        @pl.when(j == jq)
        def _():
            y = _dot_nt(n_scr[...], w_ref[...].astype(BF16))
            if jq > 0:
                qk_epilogue(jq - 1)
            if jq < J_QK:
                y_scr[jq % 2] = y
            else:
                mid_ref[...] = y.astype(BF16)

    @pl.when((j > J_QK) & (j < J_LR))
    def _():
        mid_ref[...] = _dot_nt(n_scr[...], w_ref[...].astype(BF16)).astype(BF16)

    @pl.when(j == J_LR)
    def _():
        lr = _dot_nt(n_scr[...], wlr_ref[...].astype(BF16))
        z = _dot(lr.astype(BF16), wa2_ref[...].astype(BF16)) + ba_ref[...]
        log_sig = jnp.minimum(z, 0.0) - jnp.log(1.0 + jnp.exp(-jnp.abs(z)))
        loga_ref[...] = log_sig * (1.0 / GLA_TAU)

    @pl.when(j >= J_GR)
    def _():
        y = _dot_nt(n_scr[...], wgr_ref[...].astype(BF16))
        sgr_ref[...] = (y / (1.0 + jnp.exp(-y))).astype(BF16)


def _inproj(x2, g_attn, pos2, freq, qkg, w_t, w_a2, b_a, *, tm):
    T, D = x2.shape
    n_mid = J_MID * TN
    n_gk = GLA_HEADS * GLA_KDIM
    lr0 = J_LR * TN
    gr0 = lr0 + GLA_GATE_RANK
    n_gr = w_t.shape[0] - gr0
    assert n_gr == 2 * TN and lr0 % GLA_GATE_RANK == 0
    return pl.pallas_call(
        _inproj_kernel,
        grid=(T // tm, N_J),
        in_specs=[
            pl.BlockSpec((tm, D), lambda i, j: (i, 0)),
            pl.BlockSpec((1, D), lambda i, j: (0, 0)),
            pl.BlockSpec((tm, 1), lambda i, j: (i, 0)),
            pl.BlockSpec((1, LANES), lambda i, j: (0, 0)),
            pl.BlockSpec((2, LANES), lambda i, j: (0, 0)),
            pl.BlockSpec((TN, D), lambda i, j: (jnp.minimum(j, J_LR - 1), 0)),
            pl.BlockSpec((pl.Element(TN), pl.Element(D)),
                         lambda i, j: (pl.multiple_of(gr0 + TN * jnp.clip(j - J_GR, 0, 1), SUBLANES), 0)),
            pl.BlockSpec((GLA_GATE_RANK, D), lambda i, j: (lr0 // GLA_GATE_RANK, 0)),
            pl.BlockSpec((GLA_GATE_RANK, n_gk), lambda i, j: (0, 0)),
            pl.BlockSpec((1, n_gk), lambda i, j: (0, 0)),
        ],
        out_specs=[
            pl.BlockSpec((tm, TN), lambda i, j: (i, jnp.clip(j - 1, 0, J_QK - 1))),
            pl.BlockSpec((tm, TN), lambda i, j: (i, jnp.clip(j - J_QK, 0, J_MID - 1))),
            pl.BlockSpec((tm, n_gk), lambda i, j: (i, 0)),
            pl.BlockSpec((tm, TN), lambda i, j: (i, jnp.clip(j - J_GR, 0, 1))),
        ],
        out_shape=[
            jax.ShapeDtypeStruct((T, J_QK * TN), BF16),
            jax.ShapeDtypeStruct((T, n_mid), BF16),
            jax.ShapeDtypeStruct((T, n_gk), F32),
            jax.ShapeDtypeStruct((T, n_gr), BF16),
        ],
        scratch_shapes=[
            pltpu.VMEM((tm, D), BF16),
            pltpu.VMEM((tm, LANES), F32),
            pltpu.VMEM((tm, LANES), F32),
            pltpu.VMEM((2, tm, TN), F32),
        ],
        compiler_params=_cparams(("parallel", "arbitrary"), 56),
        name="inproj",
    )(x2, g_attn, pos2, freq, qkg, w_t, w_t, w_t, w_a2, b_a)


SCORE_BOUND = 80.0


def _diffattn_kernel(ti_ref, tj_ref, lv_ref, q_ref, k_ref, v_ref, sg_ref, o_ref,
                     vext, diag_mask, acc1, acc2, m1, m2, *, tq, n_tiles, bounded):
    S = q_ref.shape[0]
    nq = S // tq
    half = tq // 2
    unroll = 7
    diag_unroll = 4 if nq % 4 == 0 else 2

    vext[:, 0:DIFF_VDIM] = v_ref[...]
    vext[:, DIFF_VDIM:] = jnp.ones((S, DIFF_VDIM), BF16)
    row_chunk = lax.broadcasted_iota(I32, (tq, tq), 0) // CHUNK
    col_chunk = lax.broadcasted_iota(I32, (tq, tq), 1) // CHUNK
    diag_mask[...] = jnp.where(col_chunk <= row_chunk, 1.0, 0.0).astype(BF16)

    lane = lax.broadcasted_iota(I32, (1, LANES), 1)

    def block(i):
        return pl.ds(pl.multiple_of(i * tq, tq), tq)

    def q_comps(i):
        q = q_ref[block(i), :]
        zero = jnp.zeros_like(q)
        return jnp.where(lane < DIFF_QKDIM, q, zero), jnp.where(lane < DIFF_QKDIM, zero, q)

    def diag_tile(i):
        q1, q2 = q_comps(i)
        for qc, acc, m in ((q1, acc1, m1), (q2, acc2, m2)):
            for lo, n_keys in ((0, half), (half, tq)):
                rows = pl.ds(pl.multiple_of(i * tq + lo, half), half)
                keys = pl.ds(pl.multiple_of(i * tq, tq), n_keys)
                mask = diag_mask[lo:lo + half, 0:n_keys]
                s = _dot_nt(qc[lo:lo + half], k_ref[keys, :])
                if bounded:
                    acc[rows, :] = _dot(jnp.exp2(s).astype(BF16) * mask, vext[keys, :])
                else:
                    s = jnp.where(mask > 0, s, NEG_INF)
                    m_new = jnp.max(s, axis=-1, keepdims=True)
                    acc[rows, :] = _dot(jnp.exp2(s - m_new).astype(BF16), vext[keys, :])
                    m[rows, :] = m_new

    def full_tile(i, j):
        q1, q2 = q_comps(i)
        rows = block(i)
        k = k_ref[block(j), :]
        v = vext[block(j), :]
        for qc, acc, m in ((q1, acc1, m1), (q2, acc2, m2)):
            s = _dot_nt(qc, k)
            if bounded:
                acc[rows, :] += _dot(jnp.exp2(s).astype(BF16), v)
            else:
                m_old = m[rows, :]
                m_new = jnp.maximum(m_old, jnp.max(s, axis=-1, keepdims=True))
                p = jnp.exp2(s - m_new)
                acc[rows, :] = jnp.exp2(m_old - m_new) * acc[rows, :] + _dot(p.astype(BF16), v)
                m[rows, :] = m_new

    def diag_body(t, carry):
        for u in range(diag_unroll):
            diag_tile(diag_unroll * t + u)
        return carry
    lax.fori_loop(0, nq // diag_unroll, diag_body, 0)

    def full_body(t, carry):
        for u in range(unroll):
            full_tile(ti_ref[unroll * t + u], tj_ref[unroll * t + u])
        return carry
    lax.fori_loop(0, n_tiles // unroll, full_body, 0)
    for t in range(n_tiles - n_tiles % unroll, n_tiles):
        full_tile(ti_ref[t], tj_ref[t])

    lv = lv_ref[...]
    lam = (jnp.exp(jnp.sum(lv[0:1] * lv[1:2], axis=-1, keepdims=True))
           - jnp.exp(jnp.sum(lv[2:3] * lv[3:4], axis=-1, keepdims=True)) + LAM_INIT)

    def out_body(i, carry):
        rows = block(i)
        a1 = acc1[rows, :]
        a2 = acc2[rows, :]
        o = a1[:, :DIFF_VDIM] / a1[:, DIFF_VDIM:] - lam * (a2[:, :DIFF_VDIM] / a2[:, DIFF_VDIM:])
        o_ref[rows, :] = (_rms(o, sg_ref[...]) * (1.0 - LAM_INIT)).astype(BF16)
        return carry
    lax.fori_loop(0, nq, out_body, 0)


def _diffattn(lvec, qk, mid, subln_g, *, B, S, tq, bounded):
    T = B * S
    nq = S // tq
    assert nq % 2 == 0
    tiles = [(i, j) for i in range(nq) for j in range(i)]
    ti = jnp.asarray([t[0] for t in tiles], I32)
    tj = jnp.asarray([t[1] for t in tiles], I32)
    kern = functools.partial(_diffattn_kernel, tq=tq, n_tiles=len(tiles), bounded=bounded)
    m_rows = SUBLANES if bounded else S
    grid_spec = pltpu.PrefetchScalarGridSpec(
        num_scalar_prefetch=2,
        grid=(B, DIFF_HEADS),
        in_specs=[
            pl.BlockSpec((4, DIFF_QKDIM), lambda b, h, *_: (0, 0)),
            pl.BlockSpec((S, LANES), lambda b, h, *_: (b, h)),
            pl.BlockSpec((S, LANES), lambda b, h, *_: (b, DIFF_HEADS + h)),
            pl.BlockSpec((S, LANES), lambda b, h, *_: (b, h)),
            pl.BlockSpec((1, DIFF_VDIM), lambda b, h, *_: (0, 0)),
        ],
        out_specs=pl.BlockSpec((S, DIFF_VDIM), lambda b, h, *_: (b, h)),
        scratch_shapes=[
            pltpu.VMEM((S, 2 * DIFF_VDIM), BF16),
            pltpu.VMEM((tq, tq), BF16),
            pltpu.VMEM((S, 2 * DIFF_VDIM), F32),
            pltpu.VMEM((S, 2 * DIFF_VDIM), F32),
            pltpu.VMEM((m_rows, 1), F32),
            pltpu.VMEM((m_rows, 1), F32),
        ],
    )
    return pl.pallas_call(
        kern,
        grid_spec=grid_spec,
        out_shape=jax.ShapeDtypeStruct((T, DIFF_HEADS * DIFF_VDIM), BF16),
        compiler_params=_cparams(("parallel", "parallel"), 40),
        name="diffattn_bounded" if bounded else "diffattn_online",
    )(ti, tj, lvec, qk, qk, mid, subln_g)


def _gla_kernel(q_ref, k_ref, v_ref, la_ref, sgr_ref, g_ref, scan_ref, o_ref, state, *, blk):
    @pl.when(pl.program_id(2) == 0)
    def _():
        state[...] = jnp.zeros(state.shape, F32)

    la_t = la_ref[...].T
    k_t = k_ref[...].astype(F32).T
    hi, lo = _split_bf16(la_t)
    scan = scan_ref[...]
    cums, tots = [], []
    for sb in range(blk // LANES):
        slab = slice(sb * LANES, (sb + 1) * LANES)
        r = _dot(hi[:, slab], scan) + _dot(lo[:, slab], scan)
        cums.append(r[:, :LANES])
        tots.append(r[:, LANES:])
    cum_t = jnp.concatenate(cums, axis=1)
    tot_t = jnp.concatenate(tots, axis=1)
    kd_t = k_t * jnp.exp(tot_t - cum_t)

    n_chunks = blk // CHUNK
    lane = lax.broadcasted_iota(I32, (1, LANES), 1)
    d_states = []
    for ck in range(n_chunks):
        pair = slice((ck // 2) * LANES, (ck // 2 + 1) * LANES)
        in_chunk = (lane // CHUNK) == (ck % 2)
        kd = jnp.where(in_chunk, kd_t[:, pair], 0.0).astype(BF16)
        d_states.append(_dot(kd, v_ref[pair, :]))

    st = state[...]
    states = []
    for ck in range(n_chunks):
        decay = jnp.exp(tot_t[:, ck * CHUNK:ck * CHUNK + 1])
        st = decay * st + d_states[ck]
        states.append(st.astype(BF16))
    state[...] = st

    o = jnp.concatenate([_dot(q_ref[ck * CHUNK:(ck + 1) * CHUNK, :], states[ck]) for ck in range(n_chunks)],
                        axis=0) * (GLA_KDIM ** -0.5)
    o_ref[...] = (_rms(o, g_ref[...]) * sgr_ref[...].astype(F32)).astype(BF16)


def _gla(mid, log_a, sgr, out_g, *, B, S, blk):
    T = B * S
    ns = S // blk
    kern = functools.partial(_gla_kernel, blk=blk)
    q_col0 = (DIFF_HEADS * DIFF_VDIM) // GLA_KDIM
    k_col0 = q_col0 + GLA_HEADS
    v_col0 = (DIFF_HEADS * DIFF_VDIM + 2 * GLA_HEADS * GLA_KDIM) // GLA_VDIM
    assert LANES % CHUNK == 0 and blk % LANES == 0
    r = jnp.arange(LANES, dtype=I32)[:, None]
    c = jnp.arange(LANES, dtype=I32)[None, :]
    same = (r // CHUNK) == (c // CHUNK)
    scan = jnp.concatenate([(same & (r <= c)).astype(BF16), same.astype(BF16)], axis=1)
    return pl.pallas_call(
        kern,
        grid=(B, GLA_HEADS, ns),
        in_specs=[
            pl.BlockSpec((blk, GLA_KDIM), lambda b, h, s: (b * ns + s, q_col0 + h)),
            pl.BlockSpec((blk, GLA_KDIM), lambda b, h, s: (b * ns + s, k_col0 + h)),
            pl.BlockSpec((blk, GLA_VDIM), lambda b, h, s: (b * ns + s, v_col0 + h)),
            pl.BlockSpec((blk, GLA_KDIM), lambda b, h, s: (b * ns + s, h)),
            pl.BlockSpec((blk, GLA_VDIM), lambda b, h, s: (b * ns + s, h)),
            pl.BlockSpec((1, GLA_VDIM), lambda b, h, s: (0, 0)),
            pl.BlockSpec((LANES, 2 * LANES), lambda b, h, s: (0, 0)),
        ],
        out_specs=pl.BlockSpec((blk, GLA_VDIM), lambda b, h, s: (b * ns + s, h)),
        out_shape=jax.ShapeDtypeStruct((T, GLA_HEADS * GLA_VDIM), BF16),
        scratch_shapes=[pltpu.VMEM((GLA_KDIM, GLA_VDIM), F32)],
        compiler_params=_cparams(("parallel", "parallel", "arbitrary"), 32),
        name="gla",
    )(mid, mid, mid, log_a, sgr, out_g, scan)


def _resident_w_map(n_j):
    return lambda i, j: (0, jnp.where(i == 0, j, n_j - 1))


def _outproj_kernel(a_ref, b_ref, wa_ref, wb_ref, x_ref, o_ref, w_scr):
    j = pl.program_id(1)
    ka = a_ref.shape[1]

    @pl.when(pl.program_id(0) == 0)
    def _():
        w_scr[j, 0:ka, :] = wa_ref[...].astype(BF16)
        w_scr[j, ka:, :] = wb_ref[...].astype(BF16)

    acc = _dot(a_ref[...], w_scr[j, 0:ka, :]) + _dot(b_ref[...], w_scr[j, ka:, :])
    o_ref[...] = x_ref[...] + acc


def _outproj(a, b, w_out, x2, *, tm, tn):
    T, ka = a.shape
    kb = b.shape[1]
    assert ka == kb
    D = w_out.shape[1]
    n_j = D // tn
    return pl.pallas_call(
        _outproj_kernel,
        grid=(T // tm, n_j),
        in_specs=[
            pl.BlockSpec((tm, ka), lambda i, j: (i, 0)),
            pl.BlockSpec((tm, kb), lambda i, j: (i, 0)),
            pl.BlockSpec((ka, tn), _resident_w_map(n_j)),
            pl.BlockSpec((kb, tn), lambda i, j: (1, jnp.where(i == 0, j, n_j - 1))),
            pl.BlockSpec((tm, tn), lambda i, j: (i, j)),
        ],
        out_specs=pl.BlockSpec((tm, tn), lambda i, j: (i, j)),
        out_shape=jax.ShapeDtypeStruct((T, D), F32),
        scratch_shapes=[pltpu.VMEM((n_j, ka + kb, tn), BF16)],
        compiler_params=_cparams(("arbitrary", "arbitrary"), 48),
        name="outproj",
    )(a, b, w_out, w_out, x2)


def _normproj_kernel(x_ref, g_ref, w_ref, hg_ref, o_ref, n_scr, *w_scr, n_norm):
    j = pl.program_id(1)

    @pl.when(j == 0)
    def _():
        n_scr[...] = _rms(x_ref[...], g_ref[...]).astype(BF16)

    if w_scr:
        @pl.when(pl.program_id(0) == 0)
        def _():
            w_scr[0][j] = w_ref[...].astype(BF16)
        y = _dot(n_scr[...], w_scr[0][j])
    else:
        y = _dot(n_scr[...], w_ref[...].astype(BF16))

    @pl.when(j < n_norm)
    def _():
        o_ref[...] = _rms(y, hg_ref[...]).astype(BF16)

    @pl.when(j >= n_norm)
    def _():
        o_ref[...] = y.astype(BF16)


def _normproj(x2, g, w, head_g, *, tm, tn, n_norm, name):
    T, D = x2.shape
    N = w.shape[1]
    kern = functools.partial(_normproj_kernel, n_norm=n_norm)
    n_j = N // tn
    resident = T // tm > 1
    return pl.pallas_call(
        kern,
        grid=(T // tm, n_j),
        in_specs=[
            pl.BlockSpec((tm, D), lambda i, j: (i, 0)),
            pl.BlockSpec((1, D), lambda i, j: (0, 0)),
            pl.BlockSpec((D, tn), _resident_w_map(n_j) if resident else (lambda i, j: (0, j))),
            pl.BlockSpec((1, tn), lambda i, j: (0, 0)),
        ],
        out_specs=pl.BlockSpec((tm, tn), lambda i, j: (i, j)),
        out_shape=jax.ShapeDtypeStruct((T, N), BF16),
        scratch_shapes=[pltpu.VMEM((tm, D), BF16)] + ([pltpu.VMEM((n_j, D, tn), BF16)] if resident else []),
        compiler_params=_cparams(("arbitrary", "arbitrary"), 48),
        name=name,
    )(x2, g, w, head_g)


def _cross_kernel(q_ref, k_ref, v_ref, w_ref, h_ref, o_ref, att_scr, w_scr, *, hdim):
    j = pl.program_id(1)

    @pl.when(pl.program_id(0) == 0)
    def _():
        w_scr[j] = w_ref[...].astype(BF16)

    @pl.when(j == 0)
    def _():
        for hd in range(CROSS_HEADS):
            cols = slice(hd * hdim, (hd + 1) * hdim)
            s = lax.dot_general(q_ref[:, cols], k_ref[:, cols], NT_DIMS, preferred_element_type=F32)
            p = jnp.exp(s - jnp.max(s, axis=-1, keepdims=True))
            l = jnp.sum(p, axis=-1, keepdims=True)
            att_scr[:, cols] = (_dot(p.astype(BF16), v_ref[:, cols]) / l).astype(BF16)

    o_ref[...] = h_ref[...] + _dot(att_scr[...], w_scr[j])


def _cross(qc, kv, w_co, h1, *, S, n_mem, tm, tn):
    T, D = qc.shape
    per_b = S // tm
    kern = functools.partial(_cross_kernel, hdim=D // CROSS_HEADS)
    n_j = D // tn
    return pl.pallas_call(
        kern,
        grid=(T // tm, n_j),
        in_specs=[
            pl.BlockSpec((tm, D), lambda i, j: (i, 0)),
            pl.BlockSpec((n_mem, D), lambda i, j: (i // per_b, 0)),
            pl.BlockSpec((n_mem, D), lambda i, j: (i // per_b, 1)),
            pl.BlockSpec((D, tn), _resident_w_map(n_j)),
            pl.BlockSpec((tm, tn), lambda i, j: (i, j)),
        ],
        out_specs=pl.BlockSpec((tm, tn), lambda i, j: (i, j)),
        out_shape=jax.ShapeDtypeStruct((T, D), F32),
        scratch_shapes=[pltpu.VMEM((tm, D), BF16), pltpu.VMEM((n_j, D, tn), BF16)],
        compiler_params=_cparams(("arbitrary", "arbitrary"), 48),
        name="cross",
    )(qc, kv, kv, w_co, h1)


R_ROWS = SUBLANES + N_EXPERTS


def _pack_bf16_pairs(xb16):
    c = xb16.shape[1] // 2
    u = lax.bitcast_convert_type(xb16.astype(F32), jnp.uint32)
    return (u[:, :c] >> 16) | (u[:, c:] & jnp.uint32(0xFFFF0000))


def _store_row_tiles(ref, x):
    for g in range(SUBLANES):
        ref[:, g, :] = x[:, g * LANES:(g + 1) * LANES]


def _load_row_tiles(ref):
    return jnp.concatenate([ref[:, g, :] for g in range(SUBLANES)], axis=1)


def _unpack_bf16_pairs_f32(w):
    lo = lax.bitcast_convert_type(w << 16, F32)
    hi = lax.bitcast_convert_type(w & jnp.uint32(0xFFFF0000), F32)
    return lo, hi


def _unpack_bf16_pairs(w):
    lo, hi = _unpack_bf16_pairs_f32(w)
    return lo.astype(BF16), hi.astype(BF16)


def _router_kernel(h_ref, g_ref, wt_ref, b_ref, eid_ref, gate_ref, xn_ref):
    n = _rms(h_ref[...], g_ref[...])
    nh, nl = _split_bf16(n)
    _store_row_tiles(xn_ref, _pack_bf16_pairs(nh))
    wh, wl = _split_bf16(wt_ref[...])
    nt = functools.partial(lax.dot_general, dimension_numbers=NT_DIMS, preferred_element_type=F32)
    lg = nt(wh, nh) + nt(wh, nl) + nt(wl, nh) + b_ref[...]

    tm = lg.shape[1]
    row = lax.broadcasted_iota(I32, (SUBLANES, tm), 0)

    def first_argmax(v, vmax):
        return jnp.min(jnp.where(v == vmax, row, SUBLANES), axis=0, keepdims=True)

    gl = jnp.where(row < N_GROUPS, lg[0:SUBLANES], NEG_INF)
    gmax = jnp.max(gl, axis=0, keepdims=True)
    grp = first_argmax(gl, gmax)
    grp_w = 1.0 / jnp.sum(jnp.exp(gl - gmax), axis=0, keepdims=True)

    sel = jnp.zeros((SUBLANES, tm), F32)
    for gi in range(N_GROUPS):
        lo = SUBLANES + gi * EXPERTS_PER_GROUP
        sel = jnp.where(grp == gi, lg[lo:lo + EXPERTS_PER_GROUP], sel)
    e = jnp.exp(sel - jnp.max(sel, axis=0, keepdims=True))
    prob = e / jnp.sum(e, axis=0, keepdims=True)
    p1 = jnp.max(prob, axis=0, keepdims=True)
    i1 = first_argmax(prob, p1)
    rest = jnp.where(row == i1, -1.0, prob)
    p2 = jnp.max(rest, axis=0, keepdims=True)
    i2 = first_argmax(rest, p2)
    den = p1 + p2
    base = grp * EXPERTS_PER_GROUP
    eid_ref[...] = jnp.where(row == 0, base + i1, jnp.where(row == 1, base + i2, 0))
    gate_ref[...] = jnp.where(row == 0, grp_w * p1 / den, jnp.where(row == 1, grp_w * p2 / den, 0.0))


def _router(h2, g_ffn, w_rt, b_r, *, tm):
    T, D = h2.shape
    return pl.pallas_call(
        _router_kernel,
        grid=(T // tm,),
        in_specs=[
            pl.BlockSpec((tm, D), lambda i: (i, 0)),
            pl.BlockSpec((1, D), lambda i: (0, 0)),
            pl.BlockSpec((R_ROWS, D), lambda i: (0, 0)),
            pl.BlockSpec((R_ROWS, 1), lambda i: (0, 0)),
        ],
        out_specs=[
            pl.BlockSpec((SUBLANES, tm), lambda i: (0, i)),
            pl.BlockSpec((SUBLANES, tm), lambda i: (0, i)),
            pl.BlockSpec((tm, SUBLANES, LANES), lambda i: (i, 0, 0)),
        ],
        out_shape=[
            jax.ShapeDtypeStruct((SUBLANES, T), I32),
            jax.ShapeDtypeStruct((SUBLANES, T), F32),
            jax.ShapeDtypeStruct((T, SUBLANES, LANES), jnp.uint32),
        ],
        compiler_params=_cparams(("parallel",), 32),
        name="router",
    )(h2, g_ffn, w_rt, b_r)


def _rank_kernel(eid_ref, rank_ref, cnt_ref, carry):
    @pl.when(pl.program_id(0) == 0)
    def _():
        carry[...] = jnp.zeros(carry.shape, F32)

    tm = eid_ref.shape[1]
    e0 = eid_ref[0:1, :]
    e1 = eid_ref[1:2, :]
    erow = lax.broadcasted_iota(I32, (N_EXPERTS, tm), 0)
    hit = jnp.where((erow == e0) | (erow == e1), 1.0, 0.0)
    r = lax.broadcasted_iota(I32, (tm, tm), 0)
    c = lax.broadcasted_iota(I32, (tm, tm), 1)
    before = jnp.where(r < c, 1.0, 0.0).astype(BF16)
    pre = _dot(hit.astype(BF16), before) + carry[:, 0:1]
    rank0 = jnp.sum(jnp.where(erow == e0, pre, 0.0), axis=0, keepdims=True)
    rank1 = jnp.sum(jnp.where(erow == e1, pre, 0.0), axis=0, keepdims=True)
    row = lax.broadcasted_iota(I32, (SUBLANES, tm), 0)
    rank_ref[...] = jnp.where(row == 0, rank0, jnp.where(row == 1, rank1, 0.0)).astype(I32)
    total = carry[...] + jnp.sum(hit, axis=1, keepdims=True)
    carry[...] = total
    cnt_ref[...] = total.astype(I32)


def _rank(eid, *, tm):
    T = eid.shape[1]
    return pl.pallas_call(
        _rank_kernel,
        grid=(T // tm,),
        in_specs=[pl.BlockSpec((SUBLANES, tm), lambda i: (0, i))],
        out_specs=[
            pl.BlockSpec((SUBLANES, tm), lambda i: (0, i)),
            pl.BlockSpec((N_EXPERTS, LANES), lambda i: (0, 0)),
        ],
        out_shape=[
            jax.ShapeDtypeStruct((SUBLANES, T), I32),
            jax.ShapeDtypeStruct((N_EXPERTS, LANES), I32),
        ],
        scratch_shapes=[pltpu.VMEM((N_EXPERTS, LANES), F32)],
        compiler_params=_cparams(("arbitrary",), 32),
        name="rank",
    )(eid)


def _dispatch_kernel(dest_ref, pad0_ref, npad_ref, nb_ref, xn_ref, xb_hbm, zbuf, sem, psem,
                     *, n_tok, tb, n_blocks):
    i = pl.program_id(0)
    tm = xn_ref.shape[0]
    base = i * tm

    def row_body(r, carry):
        for k in range(TOP_K):
            pltpu.make_async_copy(xn_ref.at[r], xb_hbm.at[dest_ref[k * n_tok + base + r]], sem).start(priority=k)
        return carry
    lax.fori_loop(0, tm, row_body, 0, unroll=8)

    @pl.when(i == 0)
    def _():
        _dispatch_fill(pad0_ref, npad_ref, nb_ref, xb_hbm, zbuf, psem, tb=tb, n_blocks=n_blocks)

    for k in range(TOP_K):
        pltpu.make_async_copy(xn_ref, xb_hbm.at[pl.ds(0, tm)], sem).wait()


def _dispatch_fill(pad0_ref, npad_ref, nb_ref, xb_hbm, zbuf, psem, *, tb, n_blocks):
    zbuf[...] = jnp.zeros(zbuf.shape, zbuf.dtype)

    def pad_copy(e):
        n = npad_ref[e]
        return pltpu.make_async_copy(zbuf.at[pl.ds(0, n)], xb_hbm.at[pl.ds(pad0_ref[e], n)], psem.at[0])

    def tail_copy(blk):
        return pltpu.make_async_copy(zbuf, xb_hbm.at[pl.ds(pl.multiple_of(blk * tb, tb), tb)], psem.at[1])

    def for_each_pad(fn):
        def body(e, c):
            @pl.when(npad_ref[e] > 0)
            def _():
                fn(e)
            return c
        lax.fori_loop(0, N_EXPERTS, body, 0)

    def for_each_tail(fn):
        def body(b, c):
            fn(b)
            return c
        lax.fori_loop(nb_ref[0], n_blocks, body, 0)

    for_each_pad(lambda e: pad_copy(e).start())
    for_each_tail(lambda b: tail_copy(b).start())
    for_each_pad(lambda e: pad_copy(e).wait())
    for_each_tail(lambda b: tail_copy(b).wait())


def _dispatch(dest, pad0, npad, nb, xn, *, tb, n_blocks, tm):
    T = xn.shape[0]
    tile = xn.shape[1:]
    kern = functools.partial(_dispatch_kernel, n_tok=T, tb=tb, n_blocks=n_blocks)
    grid_spec = pltpu.PrefetchScalarGridSpec(
        num_scalar_prefetch=4,
        grid=(T // tm,),
        in_specs=[pl.BlockSpec((tm,) + tile, lambda i, *_: (i, 0, 0))],
        out_specs=pl.BlockSpec(memory_space=pl.ANY),
        scratch_shapes=[
            pltpu.VMEM((tb,) + tile, xn.dtype),
            pltpu.SemaphoreType.DMA(()),
            pltpu.SemaphoreType.DMA((2,)),
        ],
    )
    return pl.pallas_call(
        kern,
        grid_spec=grid_spec,
        out_shape=jax.ShapeDtypeStruct((n_blocks * tb,) + tile, xn.dtype),
        compiler_params=_cparams(("arbitrary",), 32),
        name="dispatch",
    )(dest, pad0, npad, nb, xn)


def _expert_kernel(be_ref, first_ref, slot_ref, nxt_ref, run1_ref, nb_ref, x_ref, wg_hbm, wu_hbm, wd_hbm, y_ref,
                   wg_f, wu_f, wd_f, wsem, wg_b, wu_b, wd_b):
    i = pl.program_id(0)
    nb = nb_ref[0]

    def weight_copies(e, slot):
        return (pltpu.make_async_copy(wg_hbm.at[e], wg_f.at[slot], wsem.at[3 * slot]),
                pltpu.make_async_copy(wu_hbm.at[e], wu_f.at[slot], wsem.at[3 * slot + 1]),
                pltpu.make_async_copy(wd_hbm.at[e], wd_f.at[slot], wsem.at[3 * slot + 2]))

    @pl.when(i == 0)
    def _():
        for cp in weight_copies(be_ref[0], 0):
            cp.start()

    @pl.when((i == 0) & (run1_ref[0] >= 0))
    def _():
        for cp in weight_copies(jnp.maximum(run1_ref[0], 0), 1):
            cp.start()

    first = (i < nb) & (first_ref[i] == 1)
    slot = slot_ref[i]

    @pl.when(first)
    def _():
        for cp in weight_copies(0, slot):
            cp.wait()
        wg_b[...] = wg_f[slot].astype(BF16)
        wu_b[...] = wu_f[slot].astype(BF16)
        wd_b[...] = wd_f[slot].astype(BF16)

    @pl.when(first & (nxt_ref[i] >= 0))
    def _():
        for cp in weight_copies(jnp.maximum(nxt_ref[i], 0), slot):
            cp.start()

    @pl.when(i < nb)
    def _():
        n_lo, n_hi = _unpack_bf16_pairs(_load_row_tiles(x_ref))
        half = n_lo.shape[1]
        a = _dot(n_lo, wg_b[0:half, :]) + _dot(n_hi, wg_b[half:, :])
        u = _dot(n_lo, wu_b[0:half, :]) + _dot(n_hi, wu_b[half:, :])
        hdn = (a / (1.0 + jnp.exp(-a))) * u
        y_ref[...] = _pack_bf16_pairs(_dot(hdn.astype(BF16), wd_b[...]).astype(BF16))

    @pl.when(i >= nb)
    def _():
        y_ref[...] = jnp.zeros(y_ref.shape, y_ref.dtype)


def _experts(xb, blk_e, blk_first, blk_slot, blk_next, run1, nb, w_gate, w_up, w_down, *, tb, n_blocks):
    D, De = w_gate.shape[1:]
    assert xb.shape[1:] == (SUBLANES, LANES) and D == 2 * SUBLANES * LANES

    def x_map(i, be, first, slot, nxt, r1, nbr):
        return (jnp.minimum(i, nbr[0] - 1), 0, 0)

    grid_spec = pltpu.PrefetchScalarGridSpec(
        num_scalar_prefetch=6,
        grid=(n_blocks,),
        in_specs=[
            pl.BlockSpec((tb, SUBLANES, LANES), x_map),
            pl.BlockSpec(memory_space=pl.ANY),
            pl.BlockSpec(memory_space=pl.ANY),
            pl.BlockSpec(memory_space=pl.ANY),
        ],
        out_specs=pl.BlockSpec((tb, D // 2), lambda i, *_: (i, 0)),
        scratch_shapes=[
            pltpu.VMEM((2, D, De), F32),
            pltpu.VMEM((2, D, De), F32),
            pltpu.VMEM((2, De, D), F32),
            pltpu.SemaphoreType.DMA((6,)),
            pltpu.VMEM((D, De), BF16),
            pltpu.VMEM((D, De), BF16),
            pltpu.VMEM((De, D), BF16),
        ],
    )
    return pl.pallas_call(
        _expert_kernel,
        grid_spec=grid_spec,
        out_shape=jax.ShapeDtypeStruct((n_blocks * tb, D // 2), jnp.uint32),
        compiler_params=_cparams(("arbitrary",), 52),
        name="experts",
    )(blk_e, blk_first, blk_slot, blk_next, run1, nb, xb, w_gate, w_up, w_down)


def _combine_kernel(dest_ref, y_hbm, h_ref, gate_ref, o_ref, ybuf, sem, *, tm, n_tok):
    i = pl.program_id(0)
    n = pl.num_programs(0)

    def row_copy(d, k, r, slot):
        return pltpu.make_async_copy(y_hbm.at[pl.ds(d, 1), :], ybuf.at[slot, k, pl.ds(r, 1), :], sem.at[slot])

    def start_gather(blk, slot):
        def body(r, carry):
            for k in range(TOP_K):
                row_copy(dest_ref[k * n_tok + blk * tm + r], k, r, slot).start(priority=k)
            return carry
        lax.fori_loop(0, tm, body, 0, unroll=8)

    def wait_gather(slot):
        for k in range(TOP_K):
            pltpu.make_async_copy(y_hbm.at[pl.ds(0, tm), :], ybuf.at[slot, k], sem.at[slot]).wait()

    @pl.when(i == 0)
    def _():
        start_gather(0, 0)

    @pl.when(i + 1 < n)
    def _():
        start_gather(i + 1, (i + 1) % 2)

    slot = i % 2
    wait_gather(slot)
    gt = gate_ref[...]
    half = h_ref.shape[1] // 2
    y0_lo, y0_hi = _unpack_bf16_pairs_f32(ybuf[slot, 0])
    y1_lo, y1_hi = _unpack_bf16_pairs_f32(ybuf[slot, 1])
    o_ref[:, 0:half] = h_ref[:, 0:half] + gt[:, 0:1] * y0_lo + gt[:, 1:2] * y1_lo
    o_ref[:, half:] = h_ref[:, half:] + gt[:, 0:1] * y0_hi + gt[:, 1:2] * y1_hi


def _combine(dest, yb, h2, gate, *, tm):
    T, D = h2.shape
    kern = functools.partial(_combine_kernel, tm=tm, n_tok=T)
    grid_spec = pltpu.PrefetchScalarGridSpec(
        num_scalar_prefetch=1,
        grid=(T // tm,),
        in_specs=[
            pl.BlockSpec(memory_space=pl.ANY),
            pl.BlockSpec((tm, D), lambda i, d: (i, 0)),
            pl.BlockSpec((tm, TOP_K), lambda i, d: (i, 0)),
        ],
        out_specs=pl.BlockSpec((tm, D), lambda i, d: (i, 0)),
        scratch_shapes=[
            pltpu.VMEM((2, TOP_K, tm, D // 2), jnp.uint32),
            pltpu.SemaphoreType.DMA((2,)),
        ],
    )
    return pl.pallas_call(
        kern,
        grid_spec=grid_spec,
        out_shape=jax.ShapeDtypeStruct((T, D), F32),
        compiler_params=_cparams(("arbitrary",), 40),
        name="combine",
    )(dest, yb, h2, gate)


class _Tiles(NamedTuple):
    proj_rows: int = 1024
    proj_cols: int = 512
    attn_rows: int = 512
    gla_rows: int = 2048
    token_rows: int = 512
    expert_rows: int = 256


TILES = _Tiles()


def kernel(x, mem, positions, g_attn, w_in, q_norm_g, k_norm_g, lambda_q1, lambda_k1, lambda_q2, lambda_k2, diff_subln_g, gla_w_a2, gla_b_a, gla_out_g, w_out, g_cross, g_mem, w_cq, w_ckv, cq_norm_g, ck_norm_g, w_co, g_ffn, w_router_grp, b_router_grp, w_router_exp, b_router_exp, w_gate, w_up, w_down):
    B, S, D = x.shape
    T = B * S
    n_mem = mem.shape[1]
    l = 0
    x2 = x.reshape(T, D)

    half = DIFF_QKDIM // 2
    freq = ROPE_THETA ** (-jnp.arange(half, dtype=F32) / half)
    freq = jnp.tile(freq, LANES // half)[None, :]
    q_scale = math.log2(math.e) * DIFF_QKDIM ** -0.5
    qkg = jnp.stack([jnp.tile(q_norm_g[l], 2) * q_scale, jnp.tile(k_norm_g[l], 2)])
    score_bound = 1.01 * DIFF_QKDIM * q_scale * jnp.max(jnp.abs(q_norm_g[l])) * jnp.max(jnp.abs(k_norm_g[l]))
    lvec = jnp.stack([lambda_q1[l], lambda_k1[l], lambda_q2[l], lambda_k2[l]])

    qk, mid, log_a, sgr = _inproj(x2, g_attn[l][None], positions.reshape(T, 1), freq, qkg, w_in[l].T,
                                  gla_w_a2[l], gla_b_a[l][None], tm=TILES.proj_rows)
    diffattn = functools.partial(_diffattn, lvec, qk, mid, diff_subln_g[l][None], B=B, S=S, tq=TILES.attn_rows)
    mix_d = lax.cond(score_bound <= SCORE_BOUND,
                     functools.partial(diffattn, bounded=True), functools.partial(diffattn, bounded=False))
    mix_g = _gla(mid, log_a, sgr, gla_out_g[l][None], B=B, S=S, blk=TILES.gla_rows)
    h1 = _outproj(mix_d, mix_g, w_out[l], x2, tm=TILES.proj_rows, tn=TILES.proj_cols)

    hdim = D // CROSS_HEADS
    qc = _normproj(h1, g_cross[l][None], w_cq[l], cq_norm_g[l][None] * (hdim ** -0.5),
                   tm=TILES.proj_rows, tn=hdim, n_norm=CROSS_HEADS, name="cq")
    kv = _normproj(mem.reshape(B * n_mem, D), g_mem[l][None], w_ckv[l], ck_norm_g[l][None],
                   tm=B * n_mem, tn=hdim, n_norm=CROSS_HEADS, name="ckv")
    h2 = _cross(qc, kv, w_co[l], h1, S=S, n_mem=n_mem, tm=TILES.proj_rows, tn=TILES.proj_cols)

    w_rt = jnp.concatenate([w_router_grp[l].T, jnp.zeros((SUBLANES - N_GROUPS, D), F32), w_router_exp[l].T])
    b_r = jnp.concatenate([b_router_grp[l], jnp.zeros((SUBLANES - N_GROUPS,), F32), b_router_exp[l]])[:, None]
    eid, gate, xn = _router(h2, g_ffn[l][None], w_rt, b_r, tm=TILES.token_rows)
    rank, cnt = _rank(eid, tm=TILES.token_rows)

    assert TOP_K == 2
    tb = TILES.expert_rows
    n_blocks = (T * TOP_K + N_EXPERTS * (tb - 1) + tb - 1) // tb
    counts = cnt[:, 0]
    pcounts = ((counts + tb - 1) // tb) * tb
    pends = jnp.cumsum(pcounts)
    pstarts = pends - pcounts
    nb = (pends[-1:] // tb).astype(I32)
    eids = jnp.arange(N_EXPERTS, dtype=I32)
    blk_start = jnp.arange(n_blocks, dtype=I32) * tb
    blk_e = jnp.minimum(jnp.sum(pends[None, :] <= blk_start[:, None], axis=1), N_EXPERTS - 1).astype(I32)
    blk_first = jnp.concatenate([jnp.ones((1,), I32), (blk_e[1:] != blk_e[:-1]).astype(I32)])
    used = jnp.where(counts > 0, eids, N_EXPERTS)
    next_used = jnp.concatenate([lax.cummin(used[::-1])[::-1][1:], jnp.full((1,), N_EXPERTS, I32)])
    next_used = jnp.where(next_used < N_EXPERTS, next_used, -1)

    def table_at(idx, table):
        return jnp.where(idx >= 0, jnp.sum(jnp.where(idx[:, None] == eids, table, 0), axis=1), -1).astype(I32)

    after_next = table_at(next_used, next_used)
    run_of = (jnp.cumsum((counts > 0).astype(I32)) - 1).astype(I32)
    blk_next = table_at(blk_e, after_next)
    blk_slot = table_at(blk_e, run_of % 2)
    run1 = table_at(blk_e[:1], next_used)
    pick = eid[:TOP_K]
    pstart_of = jnp.sum(jnp.where(pick[..., None] == eids, pstarts, 0), axis=-1)
    dest = (pstart_of + rank[:TOP_K]).astype(I32).reshape(-1)
    pad0 = (pstarts + counts).astype(I32)
    npad = (pcounts - counts).astype(I32)

    xb = _dispatch(dest, pad0, npad, nb, xn, tb=tb, n_blocks=n_blocks, tm=TILES.token_rows)
    yb = _experts(xb, blk_e, blk_first, blk_slot, blk_next, run1, nb, w_gate[l], w_up[l], w_down[l],
                  tb=tb, n_blocks=n_blocks)
    out = _combine(dest, yb, h2, gate[:TOP_K].T, tm=TILES.token_rows)
    return out.reshape(B, S, D)
```

```python
import functools
import math
from typing import NamedTuple

import jax
import jax.numpy as jnp
from jax import lax
from jax.experimental import pallas as pl
from jax.experimental.pallas import tpu as pltpu

F32 = jnp.float32
BF16 = jnp.bfloat16
I32 = jnp.int32

LANES = 128
SUBLANES = 8

CHUNK = 64
ROPE_THETA = 10000.0
NORM_EPS = 1e-6
NEG_INF = -1e30
DIFF_HEADS = 8
DIFF_VDIM = 128
DIFF_QKDIM = 64
GLA_HEADS = 4
GLA_VDIM = 256
GLA_KDIM = 128
GLA_GATE_RANK = 16
GLA_TAU = 16.0
CROSS_HEADS = 4
N_GROUPS = 4
EXPERTS_PER_GROUP = 8
N_EXPERTS = N_GROUPS * EXPERTS_PER_GROUP
TOP_K = 2
LAM_INIT = 0.8 - 0.6 * math.exp(-0.3 * 0)

NT_DIMS = (((1,), (1,)), ((), ()))


def _cparams(semantics, vmem_mib):
    return pltpu.CompilerParams(dimension_semantics=semantics,
                                vmem_limit_bytes=vmem_mib * 1024 * 1024)


def _dot(a, b):
    return jnp.dot(a, b, preferred_element_type=F32)


def _dot_nt(a, b):
    return lax.dot_general(a, b, NT_DIMS, preferred_element_type=F32)


def _rms(x, g):
    ms = jnp.mean(x * x, axis=-1, keepdims=True)
    return x * lax.rsqrt(ms + NORM_EPS) * g


def _split_bf16(x):
    hi = x.astype(BF16)
    lo = (x - hi.astype(F32)).astype(BF16)
    return hi, lo


TN = 512
J_QK = 4
J_MID = 6
J_LR = J_QK + J_MID
J_GR = J_LR + 1
N_J = J_GR + 2


def _inproj_kernel(x_ref, g_ref, pos_ref, freq_ref, qkg_ref, w_ref, wgr_ref, wlr_ref, wa2_ref, ba_ref,
                   qk_ref, mid_ref, loga_ref, sgr_ref, n_scr, cos_scr, sin_scr, y_scr):
    j = pl.program_id(1)

    @pl.when(j == 0)
    def _():
        n_scr[...] = _rms(x_ref[...], g_ref[...]).astype(BF16)
        ang = pos_ref[...].astype(F32) * freq_ref[...]
        cos_scr[...] = jnp.cos(ang)
        sin_scr[...] = jnp.sin(ang)

    def qk_epilogue(jq):
        y_prev = y_scr.at[jq % 2]
        lane = lax.broadcasted_iota(I32, (1, LANES), 1)
        low_seg = lane < DIFF_QKDIM
        first_half = (lane % DIFF_QKDIM) < (DIFF_QKDIM // 2)
        gain = qkg_ref[jq // (J_QK // 2):jq // (J_QK // 2) + 1, :]
        cos = cos_scr[...]
        sin = sin_scr[...]
        for c in range(TN // LANES):
            yb = y_prev[:, c * LANES:(c + 1) * LANES]
            y2 = yb * yb
            s_lo = jnp.sum(jnp.where(low_seg, y2, 0.0), axis=-1, keepdims=True)
            s_hi = jnp.sum(jnp.where(low_seg, 0.0, y2), axis=-1, keepdims=True)
            ms = jnp.where(low_seg, s_lo, s_hi) * (1.0 / DIFF_QKDIM)
            yn = yb * lax.rsqrt(ms + NORM_EPS) * gain
            rot = jnp.where(first_half,
                            -pltpu.roll(yn, LANES - DIFF_QKDIM // 2, 1),
                            pltpu.roll(yn, DIFF_QKDIM // 2, 1))
            qk_ref[:, c * LANES:(c + 1) * LANES] = (yn * cos + rot * sin).astype(BF16)

    for jq in range(J_QK + 1):
        @pl.when(j == jq)
        def _():
            y = _dot_nt(n_scr[...], w_ref[...].astype(BF16))
            if jq > 0:
                qk_epilogue(jq - 1)
            if jq < J_QK:
                y_scr[jq % 2] = y
            else:
                mid_ref[...] = y.astype(BF16)

    @pl.when((j > J_QK) & (j < J_LR))
    def _():
        mid_ref[...] = _dot_nt(n_scr[...], w_ref[...].astype(BF16)).astype(BF16)

    @pl.when(j == J_LR)
    def _():
        lr = _dot_nt(n_scr[...], wlr_ref[...].astype(BF16))
        z = _dot(lr.astype(BF16), wa2_ref[...].astype(BF16)) + ba_ref[...]
        log_sig = jnp.minimum(z, 0.0) - jnp.log(1.0 + jnp.exp(-jnp.abs(z)))
        loga_ref[...] = log_sig * (1.0 / GLA_TAU)

    @pl.when(j >= J_GR)
    def _():
        y = _dot_nt(n_scr[...], wgr_ref[...].astype(BF16))
        sgr_ref[...] = (y / (1.0 + jnp.exp(-y))).astype(BF16)


def _inproj(x2, g_attn, pos2, freq, qkg, w_t, w_a2, b_a, *, tm):
    T, D = x2.shape
    n_mid = J_MID * TN
    n_gk = GLA_HEADS * GLA_KDIM
    lr0 = J_LR * TN
    gr0 = lr0 + GLA_GATE_RANK
    n_gr = w_t.shape[0] - gr0
    assert n_gr == 2 * TN and lr0 % GLA_GATE_RANK == 0
    return pl.pallas_call(
        _inproj_kernel,
        grid=(T // tm, N_J),
        in_specs=[
            pl.BlockSpec((tm, D), lambda i, j: (i, 0)),
            pl.BlockSpec((1, D), lambda i, j: (0, 0)),
            pl.BlockSpec((tm, 1), lambda i, j: (i, 0)),
            pl.BlockSpec((1, LANES), lambda i, j: (0, 0)),
            pl.BlockSpec((2, LANES), lambda i, j: (0, 0)),
            pl.BlockSpec((TN, D), lambda i, j: (jnp.minimum(j, J_LR - 1), 0)),
            pl.BlockSpec((pl.Element(TN), pl.Element(D)),
                         lambda i, j: (pl.multiple_of(gr0 + TN * jnp.clip(j - J_GR, 0, 1), SUBLANES), 0)),
            pl.BlockSpec((GLA_GATE_RANK, D), lambda i, j: (lr0 // GLA_GATE_RANK, 0)),
            pl.BlockSpec((GLA_GATE_RANK, n_gk), lambda i, j: (0, 0)),
            pl.BlockSpec((1, n_gk), lambda i, j: (0, 0)),
        ],
        out_specs=[
            pl.BlockSpec((tm, TN), lambda i, j: (i, jnp.clip(j - 1, 0, J_QK - 1))),
            pl.BlockSpec((tm, TN), lambda i, j: (i, jnp.clip(j - J_QK, 0, J_MID - 1))),
            pl.BlockSpec((tm, n_gk), lambda i, j: (i, 0)),
            pl.BlockSpec((tm, TN), lambda i, j: (i, jnp.clip(j - J_GR, 0, 1))),
        ],
        out_shape=[
            jax.ShapeDtypeStruct((T, J_QK * TN), BF16),
            jax.ShapeDtypeStruct((T, n_mid), BF16),
            jax.ShapeDtypeStruct((T, n_gk), F32),
            jax.ShapeDtypeStruct((T, n_gr), BF16),
        ],
        scratch_shapes=[
            pltpu.VMEM((tm, D), BF16),
            pltpu.VMEM((tm, LANES), F32),
            pltpu.VMEM((tm, LANES), F32),
            pltpu.VMEM((2, tm, TN), F32),
        ],
        compiler_params=_cparams(("parallel", "arbitrary"), 56),
        name="inproj",
    )(x2, g_attn, pos2, freq, qkg, w_t, w_t, w_t, w_a2, b_a)


SCORE_BOUND = 80.0


def _diffattn_kernel(ti_ref, tj_ref, lv_ref, q_ref, k_ref, v_ref, sg_ref, o_ref,
                     vext, diag_mask, acc1, acc2, m1, m2, *, tq, n_tiles, bounded):
    S = q_ref.shape[0]
    nq = S // tq
    half = tq // 2
    unroll = 7
    diag_unroll = 4 if nq % 4 == 0 else 2

    vext[:, 0:DIFF_VDIM] = v_ref[...]
    vext[:, DIFF_VDIM:] = jnp.ones((S, DIFF_VDIM), BF16)
    row_chunk = lax.broadcasted_iota(I32, (tq, tq), 0) // CHUNK
    col_chunk = lax.broadcasted_iota(I32, (tq, tq), 1) // CHUNK
    diag_mask[...] = jnp.where(col_chunk <= row_chunk, 1.0, 0.0).astype(BF16)

    lane = lax.broadcasted_iota(I32, (1, LANES), 1)

    def block(i):
        return pl.ds(pl.multiple_of(i * tq, tq), tq)

    def q_comps(i):
        q = q_ref[block(i), :]
        zero = jnp.zeros_like(q)
        return jnp.where(lane < DIFF_QKDIM, q, zero), jnp.where(lane < DIFF_QKDIM, zero, q)

    def diag_tile(i):
        q1, q2 = q_comps(i)
        for qc, acc, m in ((q1, acc1, m1), (q2, acc2, m2)):
            for lo, n_keys in ((0, half), (half, tq)):
                rows = pl.ds(pl.multiple_of(i * tq + lo, half), half)
                keys = pl.ds(pl.multiple_of(i * tq, tq), n_keys)
                mask = diag_mask[lo:lo + half, 0:n_keys]
                s = _dot_nt(qc[lo:lo + half], k_ref[keys, :])
                if bounded:
                    acc[rows, :] = _dot(jnp.exp2(s).astype(BF16) * mask, vext[keys, :])
                else:
                    s = jnp.where(mask > 0, s, NEG_INF)
                    m_new = jnp.max(s, axis=-1, keepdims=True)
                    acc[rows, :] = _dot(jnp.exp2(s - m_new).astype(BF16), vext[keys, :])
                    m[rows, :] = m_new

    def full_tile(i, j):
        q1, q2 = q_comps(i)
        rows = block(i)
        k = k_ref[block(j), :]
        v = vext[block(j), :]
        for qc, acc, m in ((q1, acc1, m1), (q2, acc2, m2)):
            s = _dot_nt(qc, k)
            if bounded:
                acc[rows, :] += _dot(jnp.exp2(s).astype(BF16), v)
            else:
                m_old = m[rows, :]
                m_new = jnp.maximum(m_old, jnp.max(s, axis=-1, keepdims=True))
                p = jnp.exp2(s - m_new)
                acc[rows, :] = jnp.exp2(m_old - m_new) * acc[rows, :] + _dot(p.astype(BF16), v)
                m[rows, :] = m_new

    def diag_body(t, carry):
        for u in range(diag_unroll):
            diag_tile(diag_unroll * t + u)
        return carry
    lax.fori_loop(0, nq // diag_unroll, diag_body, 0)

    def full_body(t, carry):
        for u in range(unroll):
            full_tile(ti_ref[unroll * t + u], tj_ref[unroll * t + u])
        return carry
    lax.fori_loop(0, n_tiles // unroll, full_body, 0)
    for t in range(n_tiles - n_tiles % unroll, n_tiles):
        full_tile(ti_ref[t], tj_ref[t])

    lv = lv_ref[...]
    lam = (jnp.exp(jnp.sum(lv[0:1] * lv[1:2], axis=-1, keepdims=True))
           - jnp.exp(jnp.sum(lv[2:3] * lv[3:4], axis=-1, keepdims=True)) + LAM_INIT)

    def out_body(i, carry):
        rows = block(i)
        a1 = acc1[rows, :]
        a2 = acc2[rows, :]
        o = a1[:, :DIFF_VDIM] / a1[:, DIFF_VDIM:] - lam * (a2[:, :DIFF_VDIM] / a2[:, DIFF_VDIM:])
        o_ref[rows, :] = (_rms(o, sg_ref[...]) * (1.0 - LAM_INIT)).astype(BF16)
        return carry
    lax.fori_loop(0, nq, out_body, 0)


def _diffattn(lvec, qk, mid, subln_g, *, B, S, tq, bounded):
    T = B * S
    nq = S // tq
    assert nq % 2 == 0
    tiles = [(i, j) for i in range(nq) for j in range(i)]
    ti = jnp.asarray([t[0] for t in tiles], I32)
    tj = jnp.asarray([t[1] for t in tiles], I32)
    kern = functools.partial(_diffattn_kernel, tq=tq, n_tiles=len(tiles), bounded=bounded)
    m_rows = SUBLANES if bounded else S
    grid_spec = pltpu.PrefetchScalarGridSpec(
        num_scalar_prefetch=2,
        grid=(B, DIFF_HEADS),
        in_specs=[
            pl.BlockSpec((4, DIFF_QKDIM), lambda b, h, *_: (0, 0)),
            pl.BlockSpec((S, LANES), lambda b, h, *_: (b, h)),
            pl.BlockSpec((S, LANES), lambda b, h, *_: (b, DIFF_HEADS + h)),
            pl.BlockSpec((S, LANES), lambda b, h, *_: (b, h)),
            pl.BlockSpec((1, DIFF_VDIM), lambda b, h, *_: (0, 0)),
        ],
        out_specs=pl.BlockSpec((S, DIFF_VDIM), lambda b, h, *_: (b, h)),
        scratch_shapes=[
            pltpu.VMEM((S, 2 * DIFF_VDIM), BF16),
            pltpu.VMEM((tq, tq), BF16),
            pltpu.VMEM((S, 2 * DIFF_VDIM), F32),
            pltpu.VMEM((S, 2 * DIFF_VDIM), F32),
            pltpu.VMEM((m_rows, 1), F32),
            pltpu.VMEM((m_rows, 1), F32),
        ],
    )
    return pl.pallas_call(
        kern,
        grid_spec=grid_spec,
        out_shape=jax.ShapeDtypeStruct((T, DIFF_HEADS * DIFF_VDIM), BF16),
        compiler_params=_cparams(("parallel", "parallel"), 40),
        name="diffattn_bounded" if bounded else "diffattn_online",
    )(ti, tj, lvec, qk, qk, mid, subln_g)


def _gla_kernel(q_ref, k_ref, v_ref, la_ref, sgr_ref, g_ref, scan_ref, o_ref, state, *, blk):
    @pl.when(pl.program_id(2) == 0)
    def _():
        state[...] = jnp.zeros(state.shape, F32)

    la_t = la_ref[...].T
    k_t = k_ref[...].astype(F32).T
    hi, lo = _split_bf16(la_t)
    scan = scan_ref[...]
    cums, tots = [], []
    for sb in range(blk // LANES):
        slab = slice(sb * LANES, (sb + 1) * LANES)
        r = _dot(hi[:, slab], scan) + _dot(lo[:, slab], scan)
        cums.append(r[:, :LANES])
        tots.append(r[:, LANES:])
    cum_t = jnp.concatenate(cums, axis=1)
    tot_t = jnp.concatenate(tots, axis=1)
    kd_t = k_t * jnp.exp(tot_t - cum_t)

    n_chunks = blk // CHUNK
    lane = lax.broadcasted_iota(I32, (1, LANES), 1)
    d_states = []
    for ck in range(n_chunks):
        pair = slice((ck // 2) * LANES, (ck // 2 + 1) * LANES)
        in_chunk = (lane // CHUNK) == (ck % 2)
        kd = jnp.where(in_chunk, kd_t[:, pair], 0.0).astype(BF16)
        d_states.append(_dot(kd, v_ref[pair, :]))

    st = state[...]
    states = []
    for ck in range(n_chunks):
        decay = jnp.exp(tot_t[:, ck * CHUNK:ck * CHUNK + 1])
        st = decay * st + d_states[ck]
        states.append(st.astype(BF16))
    state[...] = st

    o = jnp.concatenate([_dot(q_ref[ck * CHUNK:(ck + 1) * CHUNK, :], states[ck]) for ck in range(n_chunks)],
                        axis=0) * (GLA_KDIM ** -0.5)
    o_ref[...] = (_rms(o, g_ref[...]) * sgr_ref[...].astype(F32)).astype(BF16)


def _gla(mid, log_a, sgr, out_g, *, B, S, blk):
    T = B * S
    ns = S // blk
    kern = functools.partial(_gla_kernel, blk=blk)
    q_col0 = (DIFF_HEADS * DIFF_VDIM) // GLA_KDIM
    k_col0 = q_col0 + GLA_HEADS
    v_col0 = (DIFF_HEADS * DIFF_VDIM + 2 * GLA_HEADS * GLA_KDIM) // GLA_VDIM
    assert LANES % CHUNK == 0 and blk % LANES == 0
    r = jnp.arange(LANES, dtype=I32)[:, None]
    c = jnp.arange(LANES, dtype=I32)[None, :]
    same = (r // CHUNK) == (c // CHUNK)
    scan = jnp.concatenate([(same & (r <= c)).astype(BF16), same.astype(BF16)], axis=1)
    return pl.pallas_call(
        kern,
        grid=(B, GLA_HEADS, ns),
        in_specs=[
            pl.BlockSpec((blk, GLA_KDIM), lambda b, h, s: (b * ns + s, q_col0 + h)),
            pl.BlockSpec((blk, GLA_KDIM), lambda b, h, s: (b * ns + s, k_col0 + h)),
            pl.BlockSpec((blk, GLA_VDIM), lambda b, h, s: (b * ns + s, v_col0 + h)),
            pl.BlockSpec((blk, GLA_KDIM), lambda b, h, s: (b * ns + s, h)),
            pl.BlockSpec((blk, GLA_VDIM), lambda b, h, s: (b * ns + s, h)),
            pl.BlockSpec((1, GLA_VDIM), lambda b, h, s: (0, 0)),
            pl.BlockSpec((LANES, 2 * LANES), lambda b, h, s: (0, 0)),
        ],
        out_specs=pl.BlockSpec((blk, GLA_VDIM), lambda b, h, s: (b * ns + s, h)),
        out_shape=jax.ShapeDtypeStruct((T, GLA_HEADS * GLA_VDIM), BF16),
        scratch_shapes=[pltpu.VMEM((GLA_KDIM, GLA_VDIM), F32)],
        compiler_params=_cparams(("parallel", "parallel", "arbitrary"), 32),
        name="gla",
    )(mid, mid, mid, log_a, sgr, out_g, scan)


def _resident_w_map(n_j):
    return lambda i, j: (0, jnp.where(i == 0, j, n_j - 1))


def _outproj_kernel(a_ref, b_ref, wa_ref, wb_ref, x_ref, o_ref, w_scr):
    j = pl.program_id(1)
    ka = a_ref.shape[1]

    @pl.when(pl.program_id(0) == 0)
    def _():
        w_scr[j, 0:ka, :] = wa_ref[...].astype(BF16)
        w_scr[j, ka:, :] = wb_ref[...].astype(BF16)

    acc = _dot(a_ref[...], w_scr[j, 0:ka, :]) + _dot(b_ref[...], w_scr[j, ka:, :])
    o_ref[...] = x_ref[...] + acc


def _outproj(a, b, w_out, x2, *, tm, tn):
    T, ka = a.shape
    kb = b.shape[1]
    assert ka == kb
    D = w_out.shape[1]
    n_j = D // tn
    return pl.pallas_call(
        _outproj_kernel,
        grid=(T // tm, n_j),
        in_specs=[
            pl.BlockSpec((tm, ka), lambda i, j: (i, 0)),
            pl.BlockSpec((tm, kb), lambda i, j: (i, 0)),
            pl.BlockSpec((ka, tn), _resident_w_map(n_j)),
            pl.BlockSpec((kb, tn), lambda i, j: (1, jnp.where(i == 0, j, n_j - 1))),
            pl.BlockSpec((tm, tn), lambda i, j: (i, j)),
        ],
        out_specs=pl.BlockSpec((tm, tn), lambda i, j: (i, j)),
        out_shape=jax.ShapeDtypeStruct((T, D), F32),
        scratch_shapes=[pltpu.VMEM((n_j, ka + kb, tn), BF16)],
        compiler_params=_cparams(("arbitrary", "arbitrary"), 48),
        name="outproj",
    )(a, b, w_out, w_out, x2)


def _normproj_kernel(x_ref, g_ref, w_ref, hg_ref, o_ref, n_scr, *w_scr, n_norm):
    j = pl.program_id(1)

    @pl.when(j == 0)
    def _():
        n_scr[...] = _rms(x_ref[...], g_ref[...]).astype(BF16)

    if w_scr:
        @pl.when(pl.program_id(0) == 0)
        def _():
            w_scr[0][j] = w_ref[...].astype(BF16)
        y = _dot(n_scr[...], w_scr[0][j])
    else:
        y = _dot(n_scr[...], w_ref[...].astype(BF16))

    @pl.when(j < n_norm)
    def _():
        o_ref[...] = _rms(y, hg_ref[...]).astype(BF16)

    @pl.when(j >= n_norm)
    def _():
        o_ref[...] = y.astype(BF16)


def _normproj(x2, g, w, head_g, *, tm, tn, n_norm, name):
    T, D = x2.shape
    N = w.shape[1]
    kern = functools.partial(_normproj_kernel, n_norm=n_norm)
    n_j = N // tn
    resident = T // tm > 1
    return pl.pallas_call(
        kern,
        grid=(T // tm, n_j),
        in_specs=[
            pl.BlockSpec((tm, D), lambda i, j: (i, 0)),
            pl.BlockSpec((1, D), lambda i, j: (0, 0)),
            pl.BlockSpec((D, tn), _resident_w_map(n_j) if resident else (lambda i, j: (0, j))),
            pl.BlockSpec((1, tn), lambda i, j: (0, 0)),
        ],
        out_specs=pl.BlockSpec((tm, tn), lambda i, j: (i, j)),
        out_shape=jax.ShapeDtypeStruct((T, N), BF16),
        scratch_shapes=[pltpu.VMEM((tm, D), BF16)] + ([pltpu.VMEM((n_j, D, tn), BF16)] if resident else []),
        compiler_params=_cparams(("arbitrary", "arbitrary"), 48),
        name=name,
    )(x2, g, w, head_g)


def _cross_kernel(q_ref, k_ref, v_ref, w_ref, h_ref, o_ref, att_scr, w_scr, *, hdim):
    j = pl.program_id(1)

    @pl.when(pl.program_id(0) == 0)
    def _():
        w_scr[j] = w_ref[...].astype(BF16)

    @pl.when(j == 0)
    def _():
        for hd in range(CROSS_HEADS):
            cols = slice(hd * hdim, (hd + 1) * hdim)
            s = lax.dot_general(q_ref[:, cols], k_ref[:, cols], NT_DIMS, preferred_element_type=F32)
            p = jnp.exp(s - jnp.max(s, axis=-1, keepdims=True))
            l = jnp.sum(p, axis=-1, keepdims=True)
            att_scr[:, cols] = (_dot(p.astype(BF16), v_ref[:, cols]) / l).astype(BF16)

    o_ref[...] = h_ref[...] + _dot(att_scr[...], w_scr[j])


def _cross(qc, kv, w_co, h1, *, S, n_mem, tm, tn):
    T, D = qc.shape
    per_b = S // tm
    kern = functools.partial(_cross_kernel, hdim=D // CROSS_HEADS)
    n_j = D // tn
    return pl.pallas_call(
        kern,
        grid=(T // tm, n_j),
        in_specs=[
            pl.BlockSpec((tm, D), lambda i, j: (i, 0)),
            pl.BlockSpec((n_mem, D), lambda i, j: (i // per_b, 0)),
            pl.BlockSpec((n_mem, D), lambda i, j: (i // per_b, 1)),
            pl.BlockSpec((D, tn), _resident_w_map(n_j)),
            pl.BlockSpec((tm, tn), lambda i, j: (i, j)),
        ],
        out_specs=pl.BlockSpec((tm, tn), lambda i, j: (i, j)),
        out_shape=jax.ShapeDtypeStruct((T, D), F32),
        scratch_shapes=[pltpu.VMEM((tm, D), BF16), pltpu.VMEM((n_j, D, tn), BF16)],
        compiler_params=_cparams(("arbitrary", "arbitrary"), 48),
        name="cross",
    )(qc, kv, kv, w_co, h1)


R_ROWS = SUBLANES + N_EXPERTS


def _pack_bf16_pairs(xb16):
    c = xb16.shape[1] // 2
    u = lax.bitcast_convert_type(xb16.astype(F32), jnp.uint32)
    return (u[:, :c] >> 16) | (u[:, c:] & jnp.uint32(0xFFFF0000))


def _store_row_tiles(ref, x):
    for g in range(SUBLANES):
        ref[:, g, :] = x[:, g * LANES:(g + 1) * LANES]


def _load_row_tiles(ref):
    return jnp.concatenate([ref[:, g, :] for g in range(SUBLANES)], axis=1)


def _unpack_bf16_pairs_f32(w):
    lo = lax.bitcast_convert_type(w << 16, F32)
    hi = lax.bitcast_convert_type(w & jnp.uint32(0xFFFF0000), F32)
    return lo, hi


def _unpack_bf16_pairs(w):
    lo, hi = _unpack_bf16_pairs_f32(w)
    return lo.astype(BF16), hi.astype(BF16)


def _router_kernel(h_ref, g_ref, wt_ref, b_ref, eid_ref, gate_ref, xn_ref):
    n = _rms(h_ref[...], g_ref[...])
    nh, nl = _split_bf16(n)
    _store_row_tiles(xn_ref, _pack_bf16_pairs(nh))
    wh, wl = _split_bf16(wt_ref[...])
    nt = functools.partial(lax.dot_general, dimension_numbers=NT_DIMS, preferred_element_type=F32)
    lg = nt(wh, nh) + nt(wh, nl) + nt(wl, nh) + b_ref[...]

    tm = lg.shape[1]
    row = lax.broadcasted_iota(I32, (SUBLANES, tm), 0)

    def first_argmax(v, vmax):
        return jnp.min(jnp.where(v == vmax, row, SUBLANES), axis=0, keepdims=True)

    gl = jnp.where(row < N_GROUPS, lg[0:SUBLANES], NEG_INF)
    gmax = jnp.max(gl, axis=0, keepdims=True)
    grp = first_argmax(gl, gmax)
    grp_w = 1.0 / jnp.sum(jnp.exp(gl - gmax), axis=0, keepdims=True)

    sel = jnp.zeros((SUBLANES, tm), F32)
    for gi in range(N_GROUPS):
        lo = SUBLANES + gi * EXPERTS_PER_GROUP
        sel = jnp.where(grp == gi, lg[lo:lo + EXPERTS_PER_GROUP], sel)
    e = jnp.exp(sel - jnp.max(sel, axis=0, keepdims=True))
    prob = e / jnp.sum(e, axis=0, keepdims=True)
    p1 = jnp.max(prob, axis=0, keepdims=True)
    i1 = first_argmax(prob, p1)
    rest = jnp.where(row == i1, -1.0, prob)
    p2 = jnp.max(rest, axis=0, keepdims=True)
    i2 = first_argmax(rest, p2)
    den = p1 + p2
    base = grp * EXPERTS_PER_GROUP
    eid_ref[...] = jnp.where(row == 0, base + i1, jnp.where(row == 1, base + i2, 0))
    gate_ref[...] = jnp.where(row == 0, grp_w * p1 / den, jnp.where(row == 1, grp_w * p2 / den, 0.0))


def _router(h2, g_ffn, w_rt, b_r, *, tm):
    T, D = h2.shape
    return pl.pallas_call(
        _router_kernel,
        grid=(T // tm,),
        in_specs=[
            pl.BlockSpec((tm, D), lambda i: (i, 0)),
            pl.BlockSpec((1, D), lambda i: (0, 0)),
            pl.BlockSpec((R_ROWS, D), lambda i: (0, 0)),
            pl.BlockSpec((R_ROWS, 1), lambda i: (0, 0)),
        ],
        out_specs=[
            pl.BlockSpec((SUBLANES, tm), lambda i: (0, i)),
            pl.BlockSpec((SUBLANES, tm), lambda i: (0, i)),
            pl.BlockSpec((tm, SUBLANES, LANES), lambda i: (i, 0, 0)),
        ],
        out_shape=[
            jax.ShapeDtypeStruct((SUBLANES, T), I32),
            jax.ShapeDtypeStruct((SUBLANES, T), F32),
            jax.ShapeDtypeStruct((T, SUBLANES, LANES), jnp.uint32),
        ],
        compiler_params=_cparams(("parallel",), 32),
        name="router",
    )(h2, g_ffn, w_rt, b_r)


def _rank_kernel(eid_ref, rank_ref, cnt_ref, carry):
    @pl.when(pl.program_id(0) == 0)
    def _():
        carry[...] = jnp.zeros(carry.shape, F32)

    tm = eid_ref.shape[1]
    e0 = eid_ref[0:1, :]
    e1 = eid_ref[1:2, :]
    erow = lax.broadcasted_iota(I32, (N_EXPERTS, tm), 0)
    hit = jnp.where((erow == e0) | (erow == e1), 1.0, 0.0)
    r = lax.broadcasted_iota(I32, (tm, tm), 0)
    c = lax.broadcasted_iota(I32, (tm, tm), 1)
    before = jnp.where(r < c, 1.0, 0.0).astype(BF16)
    pre = _dot(hit.astype(BF16), before) + carry[:, 0:1]
    rank0 = jnp.sum(jnp.where(erow == e0, pre, 0.0), axis=0, keepdims=True)
    rank1 = jnp.sum(jnp.where(erow == e1, pre, 0.0), axis=0, keepdims=True)
    row = lax.broadcasted_iota(I32, (SUBLANES, tm), 0)
    rank_ref[...] = jnp.where(row == 0, rank0, jnp.where(row == 1, rank1, 0.0)).astype(I32)
    total = carry[...] + jnp.sum(hit, axis=1, keepdims=True)
    carry[...] = total
    cnt_ref[...] = total.astype(I32)


def _rank(eid, *, tm):
    T = eid.shape[1]
    return pl.pallas_call(
        _rank_kernel,
        grid=(T // tm,),
        in_specs=[pl.BlockSpec((SUBLANES, tm), lambda i: (0, i))],
        out_specs=[
            pl.BlockSpec((SUBLANES, tm), lambda i: (0, i)),
            pl.BlockSpec((N_EXPERTS, LANES), lambda i: (0, 0)),
        ],
        out_shape=[
            jax.ShapeDtypeStruct((SUBLANES, T), I32),
            jax.ShapeDtypeStruct((N_EXPERTS, LANES), I32),
        ],
        scratch_shapes=[pltpu.VMEM((N_EXPERTS, LANES), F32)],
        compiler_params=_cparams(("arbitrary",), 32),
        name="rank",
    )(eid)


def _dispatch_kernel(dest_ref, pad0_ref, npad_ref, nb_ref, xn_ref, xb_hbm, zbuf, sem, psem,
                     *, n_tok, tb, n_blocks):
    i = pl.program_id(0)
    tm = xn_ref.shape[0]
    base = i * tm

    def row_body(r, carry):
        for k in range(TOP_K):
            pltpu.make_async_copy(xn_ref.at[r], xb_hbm.at[dest_ref[k * n_tok + base + r]], sem).start(priority=k)
        return carry
    lax.fori_loop(0, tm, row_body, 0, unroll=8)

    @pl.when(i == 0)
    def _():
        _dispatch_fill(pad0_ref, npad_ref, nb_ref, xb_hbm, zbuf, psem, tb=tb, n_blocks=n_blocks)

    for k in range(TOP_K):
        pltpu.make_async_copy(xn_ref, xb_hbm.at[pl.ds(0, tm)], sem).wait()


def _dispatch_fill(pad0_ref, npad_ref, nb_ref, xb_hbm, zbuf, psem, *, tb, n_blocks):
    zbuf[...] = jnp.zeros(zbuf.shape, zbuf.dtype)

    def pad_copy(e):
        n = npad_ref[e]
        return pltpu.make_async_copy(zbuf.at[pl.ds(0, n)], xb_hbm.at[pl.ds(pad0_ref[e], n)], psem.at[0])

    def tail_copy(blk):
        return pltpu.make_async_copy(zbuf, xb_hbm.at[pl.ds(pl.multiple_of(blk * tb, tb), tb)], psem.at[1])

    def for_each_pad(fn):
        def body(e, c):
            @pl.when(npad_ref[e] > 0)
            def _():
                fn(e)
            return c
        lax.fori_loop(0, N_EXPERTS, body, 0)

    def for_each_tail(fn):
        def body(b, c):
            fn(b)
            return c
        lax.fori_loop(nb_ref[0], n_blocks, body, 0)

    for_each_pad(lambda e: pad_copy(e).start())
    for_each_tail(lambda b: tail_copy(b).start())
    for_each_pad(lambda e: pad_copy(e).wait())
    for_each_tail(lambda b: tail_copy(b).wait())


def _dispatch(dest, pad0, npad, nb, xn, *, tb, n_blocks, tm):
    T = xn.shape[0]
    tile = xn.shape[1:]
    kern = functools.partial(_dispatch_kernel, n_tok=T, tb=tb, n_blocks=n_blocks)
    grid_spec = pltpu.PrefetchScalarGridSpec(
        num_scalar_prefetch=4,
        grid=(T // tm,),
        in_specs=[pl.BlockSpec((tm,) + tile, lambda i, *_: (i, 0, 0))],
        out_specs=pl.BlockSpec(memory_space=pl.ANY),
        scratch_shapes=[
            pltpu.VMEM((tb,) + tile, xn.dtype),
            pltpu.SemaphoreType.DMA(()),
            pltpu.SemaphoreType.DMA((2,)),
        ],
    )
    return pl.pallas_call(
        kern,
        grid_spec=grid_spec,
        out_shape=jax.ShapeDtypeStruct((n_blocks * tb,) + tile, xn.dtype),
        compiler_params=_cparams(("arbitrary",), 32),
        name="dispatch",
    )(dest, pad0, npad, nb, xn)


def _expert_kernel(be_ref, first_ref, slot_ref, nxt_ref, run1_ref, nb_ref, x_ref, wg_hbm, wu_hbm, wd_hbm, y_ref,
                   wg_f, wu_f, wd_f, wsem, wg_b, wu_b, wd_b):
    i = pl.program_id(0)
    nb = nb_ref[0]

    def weight_copies(e, slot):
        return (pltpu.make_async_copy(wg_hbm.at[e], wg_f.at[slot], wsem.at[3 * slot]),
                pltpu.make_async_copy(wu_hbm.at[e], wu_f.at[slot], wsem.at[3 * slot + 1]),
                pltpu.make_async_copy(wd_hbm.at[e], wd_f.at[slot], wsem.at[3 * slot + 2]))

    @pl.when(i == 0)
    def _():
        for cp in weight_copies(be_ref[0], 0):
            cp.start()

    @pl.when((i == 0) & (run1_ref[0] >= 0))
    def _():
        for cp in weight_copies(jnp.maximum(run1_ref[0], 0), 1):
            cp.start()

    first = (i < nb) & (first_ref[i] == 1)
    slot = slot_ref[i]

    @pl.when(first)
    def _():
        for cp in weight_copies(0, slot):
            cp.wait()
        wg_b[...] = wg_f[slot].astype(BF16)
        wu_b[...] = wu_f[slot].astype(BF16)
        wd_b[...] = wd_f[slot].astype(BF16)

    @pl.when(first & (nxt_ref[i] >= 0))
    def _():
        for cp in weight_copies(jnp.maximum(nxt_ref[i], 0), slot):
            cp.start()

    @pl.when(i < nb)
    def _():
        n_lo, n_hi = _unpack_bf16_pairs(_load_row_tiles(x_ref))
        half = n_lo.shape[1]
        a = _dot(n_lo, wg_b[0:half, :]) + _dot(n_hi, wg_b[half:, :])
        u = _dot(n_lo, wu_b[0:half, :]) + _dot(n_hi, wu_b[half:, :])
        hdn = (a / (1.0 + jnp.exp(-a))) * u
        y_ref[...] = _pack_bf16_pairs(_dot(hdn.astype(BF16), wd_b[...]).astype(BF16))

    @pl.when(i >= nb)
    def _():
        y_ref[...] = jnp.zeros(y_ref.shape, y_ref.dtype)


def _experts(xb, blk_e, blk_first, blk_slot, blk_next, run1, nb, w_gate, w_up, w_down, *, tb, n_blocks):
    D, De = w_gate.shape[1:]
    assert xb.shape[1:] == (SUBLANES, LANES) and D == 2 * SUBLANES * LANES

    def x_map(i, be, first, slot, nxt, r1, nbr):
        return (jnp.minimum(i, nbr[0] - 1), 0, 0)

    grid_spec = pltpu.PrefetchScalarGridSpec(
        num_scalar_prefetch=6,
        grid=(n_blocks,),
        in_specs=[
            pl.BlockSpec((tb, SUBLANES, LANES), x_map),
            pl.BlockSpec(memory_space=pl.ANY),
            pl.BlockSpec(memory_space=pl.ANY),
            pl.BlockSpec(memory_space=pl.ANY),
        ],
        out_specs=pl.BlockSpec((tb, D // 2), lambda i, *_: (i, 0)),
        scratch_shapes=[
            pltpu.VMEM((2, D, De), F32),
            pltpu.VMEM((2, D, De), F32),
            pltpu.VMEM((2, De, D), F32),
            pltpu.SemaphoreType.DMA((6,)),
            pltpu.VMEM((D, De), BF16),
            pltpu.VMEM((D, De), BF16),
            pltpu.VMEM((De, D), BF16),
        ],
    )
    return pl.pallas_call(
        _expert_kernel,
        grid_spec=grid_spec,
        out_shape=jax.ShapeDtypeStruct((n_blocks * tb, D // 2), jnp.uint32),
        compiler_params=_cparams(("arbitrary",), 52),
        name="experts",
    )(blk_e, blk_first, blk_slot, blk_next, run1, nb, xb, w_gate, w_up, w_down)


def _combine_kernel(dest_ref, y_hbm, h_ref, gate_ref, o_ref, ybuf, sem, *, tm, n_tok):
    i = pl.program_id(0)
    n = pl.num_programs(0)

    def row_copy(d, k, r, slot):
        return pltpu.make_async_copy(y_hbm.at[pl.ds(d, 1), :], ybuf.at[slot, k, pl.ds(r, 1), :], sem.at[slot])

    def start_gather(blk, slot):
        def body(r, carry):
            for k in range(TOP_K):
                row_copy(dest_ref[k * n_tok + blk * tm + r], k, r, slot).start(priority=k)
            return carry
        lax.fori_loop(0, tm, body, 0, unroll=8)

    def wait_gather(slot):
        for k in range(TOP_K):
            pltpu.make_async_copy(y_hbm.at[pl.ds(0, tm), :], ybuf.at[slot, k], sem.at[slot]).wait()

    @pl.when(i == 0)
    def _():
        start_gather(0, 0)

    @pl.when(i + 1 < n)
    def _():
        start_gather(i + 1, (i + 1) % 2)

    slot = i % 2
    wait_gather(slot)
    gt = gate_ref[...]
    half = h_ref.shape[1] // 2
    y0_lo, y0_hi = _unpack_bf16_pairs_f32(ybuf[slot, 0])
    y1_lo, y1_hi = _unpack_bf16_pairs_f32(ybuf[slot, 1])
    o_ref[:, 0:half] = h_ref[:, 0:half] + gt[:, 0:1] * y0_lo + gt[:, 1:2] * y1_lo
    o_ref[:, half:] = h_ref[:, half:] + gt[:, 0:1] * y0_hi + gt[:, 1:2] * y1_hi


def _combine(dest, yb, h2, gate, *, tm):
    T, D = h2.shape
    kern = functools.partial(_combine_kernel, tm=tm, n_tok=T)
    grid_spec = pltpu.PrefetchScalarGridSpec(
        num_scalar_prefetch=1,
        grid=(T // tm,),
        in_specs=[
            pl.BlockSpec(memory_space=pl.ANY),
            pl.BlockSpec((tm, D), lambda i, d: (i, 0)),
            pl.BlockSpec((tm, TOP_K), lambda i, d: (i, 0)),
        ],
        out_specs=pl.BlockSpec((tm, D), lambda i, d: (i, 0)),
        scratch_shapes=[
            pltpu.VMEM((2, TOP_K, tm, D // 2), jnp.uint32),
            pltpu.SemaphoreType.DMA((2,)),
        ],
    )
    return pl.pallas_call(
        kern,
        grid_spec=grid_spec,
        out_shape=jax.ShapeDtypeStruct((T, D), F32),
        compiler_params=_cparams(("arbitrary",), 40),
        name="combine",
    )(dest, yb, h2, gate)


class _Tiles(NamedTuple):
    proj_rows: int = 1024
    proj_cols: int = 512
    attn_rows: int = 512
    gla_rows: int = 2048
    token_rows: int = 512
    dispatch_rows: int = 2048
    expert_rows: int = 256


TILES = _Tiles()


def kernel(x, mem, positions, g_attn, w_in, q_norm_g, k_norm_g, lambda_q1, lambda_k1, lambda_q2, lambda_k2, diff_subln_g, gla_w_a2, gla_b_a, gla_out_g, w_out, g_cross, g_mem, w_cq, w_ckv, cq_norm_g, ck_norm_g, w_co, g_ffn, w_router_grp, b_router_grp, w_router_exp, b_router_exp, w_gate, w_up, w_down):
    B, S, D = x.shape
    T = B * S
    n_mem = mem.shape[1]
    l = 0
    x2 = x.reshape(T, D)

    half = DIFF_QKDIM // 2
    freq = ROPE_THETA ** (-jnp.arange(half, dtype=F32) / half)
    freq = jnp.tile(freq, LANES // half)[None, :]
    q_scale = math.log2(math.e) * DIFF_QKDIM ** -0.5
    qkg = jnp.stack([jnp.tile(q_norm_g[l], 2) * q_scale, jnp.tile(k_norm_g[l], 2)])
    score_bound = 1.01 * DIFF_QKDIM * q_scale * jnp.max(jnp.abs(q_norm_g[l])) * jnp.max(jnp.abs(k_norm_g[l]))
    lvec = jnp.stack([lambda_q1[l], lambda_k1[l], lambda_q2[l], lambda_k2[l]])

    qk, mid, log_a, sgr = _inproj(x2, g_attn[l][None], positions.reshape(T, 1), freq, qkg, w_in[l].T,
                                  gla_w_a2[l], gla_b_a[l][None], tm=TILES.proj_rows)
    diffattn = functools.partial(_diffattn, lvec, qk, mid, diff_subln_g[l][None], B=B, S=S, tq=TILES.attn_rows)
    mix_d = lax.cond(score_bound <= SCORE_BOUND,
                     functools.partial(diffattn, bounded=True), functools.partial(diffattn, bounded=False))
    mix_g = _gla(mid, log_a, sgr, gla_out_g[l][None], B=B, S=S, blk=TILES.gla_rows)
    h1 = _outproj(mix_d, mix_g, w_out[l], x2, tm=TILES.proj_rows, tn=TILES.proj_cols)

    hdim = D // CROSS_HEADS
    qc = _normproj(h1, g_cross[l][None], w_cq[l], cq_norm_g[l][None] * (hdim ** -0.5),
                   tm=TILES.proj_rows, tn=hdim, n_norm=CROSS_HEADS, name="cq")
    kv = _normproj(mem.reshape(B * n_mem, D), g_mem[l][None], w_ckv[l], ck_norm_g[l][None],
                   tm=B * n_mem, tn=hdim, n_norm=CROSS_HEADS, name="ckv")
    h2 = _cross(qc, kv, w_co[l], h1, S=S, n_mem=n_mem, tm=TILES.proj_rows, tn=TILES.proj_cols)

    w_rt = jnp.concatenate([w_router_grp[l].T, jnp.zeros((SUBLANES - N_GROUPS, D), F32), w_router_exp[l].T])
    b_r = jnp.concatenate([b_router_grp[l], jnp.zeros((SUBLANES - N_GROUPS,), F32), b_router_exp[l]])[:, None]
    eid, gate, xn = _router(h2, g_ffn[l][None], w_rt, b_r, tm=TILES.token_rows)
    rank, cnt = _rank(eid, tm=TILES.token_rows)

    assert TOP_K == 2
    tb = TILES.expert_rows
    n_blocks = (T * TOP_K + N_EXPERTS * (tb - 1) + tb - 1) // tb
    counts = cnt[:, 0]
    pcounts = ((counts + tb - 1) // tb) * tb
    pends = jnp.cumsum(pcounts)
    pstarts = pends - pcounts
    nb = (pends[-1:] // tb).astype(I32)
    eids = jnp.arange(N_EXPERTS, dtype=I32)
    blk_start = jnp.arange(n_blocks, dtype=I32) * tb
    blk_e = jnp.minimum(jnp.sum(pends[None, :] <= blk_start[:, None], axis=1), N_EXPERTS - 1).astype(I32)
    blk_first = jnp.concatenate([jnp.ones((1,), I32), (blk_e[1:] != blk_e[:-1]).astype(I32)])
    used = jnp.where(counts > 0, eids, N_EXPERTS)
    next_used = jnp.concatenate([lax.cummin(used[::-1])[::-1][1:], jnp.full((1,), N_EXPERTS, I32)])
    next_used = jnp.where(next_used < N_EXPERTS, next_used, -1)

    def table_at(idx, table):
        return jnp.where(idx >= 0, jnp.sum(jnp.where(idx[:, None] == eids, table, 0), axis=1), -1).astype(I32)

    after_next = table_at(next_used, next_used)
    run_of = (jnp.cumsum((counts > 0).astype(I32)) - 1).astype(I32)
    blk_next = table_at(blk_e, after_next)
    blk_slot = table_at(blk_e, run_of % 2)
    run1 = table_at(blk_e[:1], next_used)
    pick = eid[:TOP_K]
    pstart_of = jnp.sum(jnp.where(pick[..., None] == eids, pstarts, 0), axis=-1)
    dest = (pstart_of + rank[:TOP_K]).astype(I32).reshape(-1)
    pad0 = (pstarts + counts).astype(I32)
    npad = (pcounts - counts).astype(I32)

    xb = _dispatch(dest, pad0, npad, nb, xn, tb=tb, n_blocks=n_blocks, tm=TILES.dispatch_rows)
    yb = _experts(xb, blk_e, blk_first, blk_slot, blk_next, run1, nb, w_gate[l], w_up[l], w_down[l],
                  tb=tb, n_blocks=n_blocks)
    out = _combine(dest, yb, h2, gate[:TOP_K].T, tm=TILES.token_rows)
    return out.reshape(B, S, D)
```

```python
import functools
import math
from typing import NamedTuple

import jax
import jax.numpy as jnp
from jax import lax
from jax.experimental import pallas as pl
from jax.experimental.pallas import tpu as pltpu

F32 = jnp.float32
BF16 = jnp.bfloat16
I32 = jnp.int32

LANES = 128
SUBLANES = 8

CHUNK = 64
ROPE_THETA = 10000.0
NORM_EPS = 1e-6
NEG_INF = -1e30
DIFF_HEADS = 8
DIFF_VDIM = 128
DIFF_QKDIM = 64
GLA_HEADS = 4
GLA_VDIM = 256
GLA_KDIM = 128
GLA_GATE_RANK = 16
GLA_TAU = 16.0
CROSS_HEADS = 4
N_GROUPS = 4
EXPERTS_PER_GROUP = 8
N_EXPERTS = N_GROUPS * EXPERTS_PER_GROUP
TOP_K = 2
LAM_INIT = 0.8 - 0.6 * math.exp(-0.3 * 0)

NT_DIMS = (((1,), (1,)), ((), ()))


def _cparams(semantics, vmem_mib):
    return pltpu.CompilerParams(dimension_semantics=semantics,
                                vmem_limit_bytes=vmem_mib * 1024 * 1024)


def _dot(a, b):
    return jnp.dot(a, b, preferred_element_type=F32)


def _dot_nt(a, b):
    return lax.dot_general(a, b, NT_DIMS, preferred_element_type=F32)


def _rms(x, g):
    ms = jnp.mean(x * x, axis=-1, keepdims=True)
    return x * lax.rsqrt(ms + NORM_EPS) * g


def _split_bf16(x):
    hi = x.astype(BF16)
    lo = (x - hi.astype(F32)).astype(BF16)
    return hi, lo


TN = 512
J_QK = 4
J_MID = 6
J_LR = J_QK + J_MID
J_GR = J_LR + 1
N_J = J_GR + 2


def _inproj_kernel(x_ref, g_ref, pos_ref, freq_ref, qkg_ref, w_ref, wgr_ref, wlr_ref, wa2_ref, ba_ref,
                   qk_ref, mid_ref, loga_ref, sgr_ref, n_scr, cos_scr, sin_scr, y_scr):
    j = pl.program_id(1)

    @pl.when(j == 0)
    def _():
        n_scr[...] = _rms(x_ref[...], g_ref[...]).astype(BF16)
        ang = pos_ref[...].astype(F32) * freq_ref[...]
        cos_scr[...] = jnp.cos(ang)
        sin_scr[...] = jnp.sin(ang)

    def qk_epilogue(jq):
        y_prev = y_scr.at[jq % 2]
        lane = lax.broadcasted_iota(I32, (1, LANES), 1)
        low_seg = lane < DIFF_QKDIM
        first_half = (lane % DIFF_QKDIM) < (DIFF_QKDIM // 2)
        gain = qkg_ref[jq // (J_QK // 2):jq // (J_QK // 2) + 1, :]
        cos = cos_scr[...]
        sin = sin_scr[...]
        for c in range(TN // LANES):
            yb = y_prev[:, c * LANES:(c + 1) * LANES]
            y2 = yb * yb
            s_lo = jnp.sum(jnp.where(low_seg, y2, 0.0), axis=-1, keepdims=True)
            s_hi = jnp.sum(jnp.where(low_seg, 0.0, y2), axis=-1, keepdims=True)
            ms = jnp.where(low_seg, s_lo, s_hi) * (1.0 / DIFF_QKDIM)
            yn = yb * lax.rsqrt(ms + NORM_EPS) * gain
            rot = jnp.where(first_half,
                            -pltpu.roll(yn, LANES - DIFF_QKDIM // 2, 1),
                            pltpu.roll(yn, DIFF_QKDIM // 2, 1))
            qk_ref[:, c * LANES:(c + 1) * LANES] = (yn * cos + rot * sin).astype(BF16)

    for jq in range(J_QK + 1):
        @pl.when(j == jq)
        def _():
            y = _dot_nt(n_scr[...], w_ref[...].astype(BF16))
            if jq > 0:
                qk_epilogue(jq - 1)
            if jq < J_QK:
                y_scr[jq % 2] = y
            else:
                mid_ref[...] = y.astype(BF16)

    @pl.when((j > J_QK) & (j < J_LR))
    def _():
        mid_ref[...] = _dot_nt(n_scr[...], w_ref[...].astype(BF16)).astype(BF16)

    @pl.when(j == J_LR)
    def _():
        lr = _dot_nt(n_scr[...], wlr_ref[...].astype(BF16))
        z = _dot(lr.astype(BF16), wa2_ref[...].astype(BF16)) + ba_ref[...]
        log_sig = jnp.minimum(z, 0.0) - jnp.log(1.0 + jnp.exp(-jnp.abs(z)))
        loga_ref[...] = log_sig * (1.0 / GLA_TAU)

    @pl.when(j >= J_GR)
    def _():
        y = _dot_nt(n_scr[...], wgr_ref[...].astype(BF16))
        sgr_ref[...] = (y / (1.0 + jnp.exp(-y))).astype(BF16)


def _inproj(x2, g_attn, pos2, freq, qkg, w_t, w_a2, b_a, *, tm):
    T, D = x2.shape
    n_mid = J_MID * TN
    n_gk = GLA_HEADS * GLA_KDIM
    lr0 = J_LR * TN
    gr0 = lr0 + GLA_GATE_RANK
    n_gr = w_t.shape[0] - gr0
    assert n_gr == 2 * TN and lr0 % GLA_GATE_RANK == 0
    return pl.pallas_call(
        _inproj_kernel,
        grid=(T // tm, N_J),
        in_specs=[
            pl.BlockSpec((tm, D), lambda i, j: (i, 0)),
            pl.BlockSpec((1, D), lambda i, j: (0, 0)),
            pl.BlockSpec((tm, 1), lambda i, j: (i, 0)),
            pl.BlockSpec((1, LANES), lambda i, j: (0, 0)),
            pl.BlockSpec((2, LANES), lambda i, j: (0, 0)),
            pl.BlockSpec((TN, D), lambda i, j: (jnp.minimum(j, J_LR - 1), 0)),
            pl.BlockSpec((pl.Element(TN), pl.Element(D)),
                         lambda i, j: (pl.multiple_of(gr0 + TN * jnp.clip(j - J_GR, 0, 1), SUBLANES), 0)),
            pl.BlockSpec((GLA_GATE_RANK, D), lambda i, j: (lr0 // GLA_GATE_RANK, 0)),
            pl.BlockSpec((GLA_GATE_RANK, n_gk), lambda i, j: (0, 0)),
            pl.BlockSpec((1, n_gk), lambda i, j: (0, 0)),
        ],
        out_specs=[
            pl.BlockSpec((tm, TN), lambda i, j: (i, jnp.clip(j - 1, 0, J_QK - 1))),
            pl.BlockSpec((tm, TN), lambda i, j: (i, jnp.clip(j - J_QK, 0, J_MID - 1))),
            pl.BlockSpec((tm, n_gk), lambda i, j: (i, 0)),
            pl.BlockSpec((tm, TN), lambda i, j: (i, jnp.clip(j - J_GR, 0, 1))),
        ],
        out_shape=[
            jax.ShapeDtypeStruct((T, J_QK * TN), BF16),
            jax.ShapeDtypeStruct((T, n_mid), BF16),
            jax.ShapeDtypeStruct((T, n_gk), F32),
            jax.ShapeDtypeStruct((T, n_gr), BF16),
        ],
        scratch_shapes=[
            pltpu.VMEM((tm, D), BF16),
            pltpu.VMEM((tm, LANES), F32),
            pltpu.VMEM((tm, LANES), F32),
            pltpu.VMEM((2, tm, TN), F32),
        ],
        compiler_params=_cparams(("parallel", "arbitrary"), 56),
        name="inproj",
    )(x2, g_attn, pos2, freq, qkg, w_t, w_t, w_t, w_a2, b_a)


SCORE_BOUND = 80.0


def _diffattn_kernel(ti_ref, tj_ref, lv_ref, q_ref, k_ref, v_ref, sg_ref, o_ref,
                     vext, diag_mask, acc1, acc2, m1, m2, *, tq, n_tiles, bounded):
    S = q_ref.shape[0]
    nq = S // tq
    half = tq // 2
    unroll = 7
    diag_unroll = 4 if nq % 4 == 0 else 2

    vext[:, 0:DIFF_VDIM] = v_ref[...]
    vext[:, DIFF_VDIM:] = jnp.ones((S, DIFF_VDIM), BF16)
    row_chunk = lax.broadcasted_iota(I32, (tq, tq), 0) // CHUNK
    col_chunk = lax.broadcasted_iota(I32, (tq, tq), 1) // CHUNK
    diag_mask[...] = jnp.where(col_chunk <= row_chunk, 1.0, 0.0).astype(BF16)

    lane = lax.broadcasted_iota(I32, (1, LANES), 1)

    def block(i):
        return pl.ds(pl.multiple_of(i * tq, tq), tq)

    def q_comps(i):
        q = q_ref[block(i), :]
        zero = jnp.zeros_like(q)
        return jnp.where(lane < DIFF_QKDIM, q, zero), jnp.where(lane < DIFF_QKDIM, zero, q)

    def diag_tile(i):
        q1, q2 = q_comps(i)
        for qc, acc, m in ((q1, acc1, m1), (q2, acc2, m2)):
            for lo, n_keys in ((0, half), (half, tq)):
                rows = pl.ds(pl.multiple_of(i * tq + lo, half), half)
                keys = pl.ds(pl.multiple_of(i * tq, tq), n_keys)
                mask = diag_mask[lo:lo + half, 0:n_keys]
                s = _dot_nt(qc[lo:lo + half], k_ref[keys, :])
                if bounded:
                    acc[rows, :] = _dot(jnp.exp2(s).astype(BF16) * mask, vext[keys, :])
                else:
                    s = jnp.where(mask > 0, s, NEG_INF)
                    m_new = jnp.max(s, axis=-1, keepdims=True)
                    acc[rows, :] = _dot(jnp.exp2(s - m_new).astype(BF16), vext[keys, :])
                    m[rows, :] = m_new

    def full_tile(i, j):
        q1, q2 = q_comps(i)
        rows = block(i)
        k = k_ref[block(j), :]
        v = vext[block(j), :]
        for qc, acc, m in ((q1, acc1, m1), (q2, acc2, m2)):
            s = _dot_nt(qc, k)
            if bounded:
                acc[rows, :] += _dot(jnp.exp2(s).astype(BF16), v)
            else:
                m_old = m[rows, :]
                m_new = jnp.maximum(m_old, jnp.max(s, axis=-1, keepdims=True))
                p = jnp.exp2(s - m_new)
                acc[rows, :] = jnp.exp2(m_old - m_new) * acc[rows, :] + _dot(p.astype(BF16), v)
                m[rows, :] = m_new

    def diag_body(t, carry):
        for u in range(diag_unroll):
            diag_tile(diag_unroll * t + u)
        return carry
    lax.fori_loop(0, nq // diag_unroll, diag_body, 0)

    def full_body(t, carry):
        for u in range(unroll):
            full_tile(ti_ref[unroll * t + u], tj_ref[unroll * t + u])
        return carry
    lax.fori_loop(0, n_tiles // unroll, full_body, 0)
    for t in range(n_tiles - n_tiles % unroll, n_tiles):
        full_tile(ti_ref[t], tj_ref[t])

    lv = lv_ref[...]
    lam = (jnp.exp(jnp.sum(lv[0:1] * lv[1:2], axis=-1, keepdims=True))
           - jnp.exp(jnp.sum(lv[2:3] * lv[3:4], axis=-1, keepdims=True)) + LAM_INIT)

    def out_body(i, carry):
        rows = block(i)
        a1 = acc1[rows, :]
        a2 = acc2[rows, :]
        o = a1[:, :DIFF_VDIM] / a1[:, DIFF_VDIM:] - lam * (a2[:, :DIFF_VDIM] / a2[:, DIFF_VDIM:])
        o_ref[rows, :] = (_rms(o, sg_ref[...]) * (1.0 - LAM_INIT)).astype(BF16)
        return carry
    lax.fori_loop(0, nq, out_body, 0)


def _diffattn(lvec, qk, mid, subln_g, *, B, S, tq, bounded):
    T = B * S
    nq = S // tq
    assert nq % 2 == 0
    tiles = [(i, j) for i in range(nq) for j in range(i)]
    ti = jnp.asarray([t[0] for t in tiles], I32)
    tj = jnp.asarray([t[1] for t in tiles], I32)
    kern = functools.partial(_diffattn_kernel, tq=tq, n_tiles=len(tiles), bounded=bounded)
    m_rows = SUBLANES if bounded else S
    grid_spec = pltpu.PrefetchScalarGridSpec(
        num_scalar_prefetch=2,
        grid=(B, DIFF_HEADS),
        in_specs=[
            pl.BlockSpec((4, DIFF_QKDIM), lambda b, h, *_: (0, 0)),
            pl.BlockSpec((S, LANES), lambda b, h, *_: (b, h)),
            pl.BlockSpec((S, LANES), lambda b, h, *_: (b, DIFF_HEADS + h)),
            pl.BlockSpec((S, LANES), lambda b, h, *_: (b, h)),
            pl.BlockSpec((1, DIFF_VDIM), lambda b, h, *_: (0, 0)),
        ],
        out_specs=pl.BlockSpec((S, DIFF_VDIM), lambda b, h, *_: (b, h)),
        scratch_shapes=[
            pltpu.VMEM((S, 2 * DIFF_VDIM), BF16),
            pltpu.VMEM((tq, tq), BF16),
            pltpu.VMEM((S, 2 * DIFF_VDIM), F32),
            pltpu.VMEM((S, 2 * DIFF_VDIM), F32),
            pltpu.VMEM((m_rows, 1), F32),
            pltpu.VMEM((m_rows, 1), F32),
        ],
    )
    return pl.pallas_call(
        kern,
        grid_spec=grid_spec,
        out_shape=jax.ShapeDtypeStruct((T, DIFF_HEADS * DIFF_VDIM), BF16),
        compiler_params=_cparams(("parallel", "parallel"), 40),
        name="diffattn_bounded" if bounded else "diffattn_online",
    )(ti, tj, lvec, qk, qk, mid, subln_g)


def _gla_kernel(q_ref, k_ref, v_ref, la_ref, sgr_ref, g_ref, scan_ref, o_ref, state, *, blk):
    @pl.when(pl.program_id(2) == 0)
    def _():
        state[...] = jnp.zeros(state.shape, F32)

    la_t = la_ref[...].T
    k_t = k_ref[...].astype(F32).T
    hi, lo = _split_bf16(la_t)
    scan = scan_ref[...]
    cums, tots = [], []
    for sb in range(blk // LANES):
        slab = slice(sb * LANES, (sb + 1) * LANES)
        r = _dot(hi[:, slab], scan) + _dot(lo[:, slab], scan)
        cums.append(r[:, :LANES])
        tots.append(r[:, LANES:])
    cum_t = jnp.concatenate(cums, axis=1)
    tot_t = jnp.concatenate(tots, axis=1)
    kd_t = k_t * jnp.exp(tot_t - cum_t)

    n_chunks = blk // CHUNK
    lane = lax.broadcasted_iota(I32, (1, LANES), 1)
    d_states = []
    for ck in range(n_chunks):
        pair = slice((ck // 2) * LANES, (ck // 2 + 1) * LANES)
        in_chunk = (lane // CHUNK) == (ck % 2)
        kd = jnp.where(in_chunk, kd_t[:, pair], 0.0).astype(BF16)
        d_states.append(_dot(kd, v_ref[pair, :]))

    st = state[...]
    states = []
    for ck in range(n_chunks):
        decay = jnp.exp(tot_t[:, ck * CHUNK:ck * CHUNK + 1])
        st = decay * st + d_states[ck]
        states.append(st.astype(BF16))
    state[...] = st

    o = jnp.concatenate([_dot(q_ref[ck * CHUNK:(ck + 1) * CHUNK, :], states[ck]) for ck in range(n_chunks)],
                        axis=0) * (GLA_KDIM ** -0.5)
    o_ref[...] = (_rms(o, g_ref[...]) * sgr_ref[...].astype(F32)).astype(BF16)


def _gla(mid, log_a, sgr, out_g, *, B, S, blk):
    T = B * S
    ns = S // blk
    kern = functools.partial(_gla_kernel, blk=blk)
    q_col0 = (DIFF_HEADS * DIFF_VDIM) // GLA_KDIM
    k_col0 = q_col0 + GLA_HEADS
    v_col0 = (DIFF_HEADS * DIFF_VDIM + 2 * GLA_HEADS * GLA_KDIM) // GLA_VDIM
    assert LANES % CHUNK == 0 and blk % LANES == 0
    r = jnp.arange(LANES, dtype=I32)[:, None]
    c = jnp.arange(LANES, dtype=I32)[None, :]
    same = (r // CHUNK) == (c // CHUNK)
    scan = jnp.concatenate([(same & (r <= c)).astype(BF16), same.astype(BF16)], axis=1)
    return pl.pallas_call(
        kern,
        grid=(B, GLA_HEADS, ns),
        in_specs=[
            pl.BlockSpec((blk, GLA_KDIM), lambda b, h, s: (b * ns + s, q_col0 + h)),
            pl.BlockSpec((blk, GLA_KDIM), lambda b, h, s: (b * ns + s, k_col0 + h)),
            pl.BlockSpec((blk, GLA_VDIM), lambda b, h, s: (b * ns + s, v_col0 + h)),
            pl.BlockSpec((blk, GLA_KDIM), lambda b, h, s: (b * ns + s, h)),
            pl.BlockSpec((blk, GLA_VDIM), lambda b, h, s: (b * ns + s, h)),
            pl.BlockSpec((1, GLA_VDIM), lambda b, h, s: (0, 0)),
            pl.BlockSpec((LANES, 2 * LANES), lambda b, h, s: (0, 0)),
        ],
        out_specs=pl.BlockSpec((blk, GLA_VDIM), lambda b, h, s: (b * ns + s, h)),
        out_shape=jax.ShapeDtypeStruct((T, GLA_HEADS * GLA_VDIM), BF16),
        scratch_shapes=[pltpu.VMEM((GLA_KDIM, GLA_VDIM), F32)],
        compiler_params=_cparams(("parallel", "parallel", "arbitrary"), 32),
        name="gla",
    )(mid, mid, mid, log_a, sgr, out_g, scan)


def _resident_w_map(n_j):
    return lambda i, j: (0, jnp.where(i == 0, j, n_j - 1))


def _outproj_kernel(a_ref, b_ref, wa_ref, wb_ref, x_ref, o_ref, w_scr):
    j = pl.program_id(1)
    ka = a_ref.shape[1]

    @pl.when(pl.program_id(0) == 0)
    def _():
        w_scr[j, 0:ka, :] = wa_ref[...].astype(BF16)
        w_scr[j, ka:, :] = wb_ref[...].astype(BF16)

    acc = _dot(a_ref[...], w_scr[j, 0:ka, :]) + _dot(b_ref[...], w_scr[j, ka:, :])
    o_ref[...] = x_ref[...] + acc


def _outproj(a, b, w_out, x2, *, tm, tn):
    T, ka = a.shape
    kb = b.shape[1]
    assert ka == kb
    D = w_out.shape[1]
    n_j = D // tn
    return pl.pallas_call(
        _outproj_kernel,
        grid=(T // tm, n_j),
        in_specs=[
            pl.BlockSpec((tm, ka), lambda i, j: (i, 0)),
            pl.BlockSpec((tm, kb), lambda i, j: (i, 0)),
            pl.BlockSpec((ka, tn), _resident_w_map(n_j)),
            pl.BlockSpec((kb, tn), lambda i, j: (1, jnp.where(i == 0, j, n_j - 1))),
            pl.BlockSpec((tm, tn), lambda i, j: (i, j)),
        ],
        out_specs=pl.BlockSpec((tm, tn), lambda i, j: (i, j)),
        out_shape=jax.ShapeDtypeStruct((T, D), F32),
        scratch_shapes=[pltpu.VMEM((n_j, ka + kb, tn), BF16)],
        compiler_params=_cparams(("arbitrary", "arbitrary"), 48),
        name="outproj",
    )(a, b, w_out, w_out, x2)


def _normproj_kernel(x_ref, g_ref, w_ref, hg_ref, o_ref, n_scr, *w_scr, n_norm):
    j = pl.program_id(1)

    @pl.when(j == 0)
    def _():
        n_scr[...] = _rms(x_ref[...], g_ref[...]).astype(BF16)

    if w_scr:
        @pl.when(pl.program_id(0) == 0)
        def _():
            w_scr[0][j] = w_ref[...].astype(BF16)
        y = _dot(n_scr[...], w_scr[0][j])
    else:
        y = _dot(n_scr[...], w_ref[...].astype(BF16))

    @pl.when(j < n_norm)
    def _():
        o_ref[...] = _rms(y, hg_ref[...]).astype(BF16)

    @pl.when(j >= n_norm)
    def _():
        o_ref[...] = y.astype(BF16)


def _normproj(x2, g, w, head_g, *, tm, tn, n_norm, name):
    T, D = x2.shape
    N = w.shape[1]
    kern = functools.partial(_normproj_kernel, n_norm=n_norm)
    n_j = N // tn
    resident = T // tm > 1
    return pl.pallas_call(
        kern,
        grid=(T // tm, n_j),
        in_specs=[
            pl.BlockSpec((tm, D), lambda i, j: (i, 0)),
            pl.BlockSpec((1, D), lambda i, j: (0, 0)),
            pl.BlockSpec((D, tn), _resident_w_map(n_j) if resident else (lambda i, j: (0, j))),
            pl.BlockSpec((1, tn), lambda i, j: (0, 0)),
        ],
        out_specs=pl.BlockSpec((tm, tn), lambda i, j: (i, j)),
        out_shape=jax.ShapeDtypeStruct((T, N), BF16),
        scratch_shapes=[pltpu.VMEM((tm, D), BF16)] + ([pltpu.VMEM((n_j, D, tn), BF16)] if resident else []),
        compiler_params=_cparams(("arbitrary", "arbitrary"), 48),
        name=name,
    )(x2, g, w, head_g)


def _cross_kernel(q_ref, k_ref, v_ref, w_ref, h_ref, o_ref, att_scr, w_scr, *, hdim):
    j = pl.program_id(1)

    @pl.when(pl.program_id(0) == 0)
    def _():
        w_scr[j] = w_ref[...].astype(BF16)

    @pl.when(j == 0)
    def _():
        for hd in range(CROSS_HEADS):
            cols = slice(hd * hdim, (hd + 1) * hdim)
            s = lax.dot_general(q_ref[:, cols], k_ref[:, cols], NT_DIMS, preferred_element_type=F32)
            p = jnp.exp(s - jnp.max(s, axis=-1, keepdims=True))
            l = jnp.sum(p, axis=-1, keepdims=True)
            att_scr[:, cols] = (_dot(p.astype(BF16), v_ref[:, cols]) / l).astype(BF16)

    o_ref[...] = h_ref[...] + _dot(att_scr[...], w_scr[j])


def _cross(qc, kv, w_co, h1, *, S, n_mem, tm, tn):
    T, D = qc.shape
    per_b = S // tm
    kern = functools.partial(_cross_kernel, hdim=D // CROSS_HEADS)
    n_j = D // tn
    return pl.pallas_call(
        kern,
        grid=(T // tm, n_j),
        in_specs=[
            pl.BlockSpec((tm, D), lambda i, j: (i, 0)),
            pl.BlockSpec((n_mem, D), lambda i, j: (i // per_b, 0)),
            pl.BlockSpec((n_mem, D), lambda i, j: (i // per_b, 1)),
            pl.BlockSpec((D, tn), _resident_w_map(n_j)),
            pl.BlockSpec((tm, tn), lambda i, j: (i, j)),
        ],
        out_specs=pl.BlockSpec((tm, tn), lambda i, j: (i, j)),
        out_shape=jax.ShapeDtypeStruct((T, D), F32),
        scratch_shapes=[pltpu.VMEM((tm, D), BF16), pltpu.VMEM((n_j, D, tn), BF16)],
        compiler_params=_cparams(("arbitrary", "arbitrary"), 48),
        name="cross",
    )(qc, kv, kv, w_co, h1)


R_ROWS = SUBLANES + N_EXPERTS


def _pack_bf16_pairs(xb16):
    c = xb16.shape[1] // 2
    u = lax.bitcast_convert_type(xb16.astype(F32), jnp.uint32)
    return (u[:, :c] >> 16) | (u[:, c:] & jnp.uint32(0xFFFF0000))


def _store_row_tiles(ref, x):
    for g in range(SUBLANES):
        ref[:, g, :] = x[:, g * LANES:(g + 1) * LANES]


def _load_row_tiles(ref):
    return jnp.concatenate([ref[:, g, :] for g in range(SUBLANES)], axis=1)


def _unpack_bf16_pairs_f32(w):
    lo = lax.bitcast_convert_type(w << 16, F32)
    hi = lax.bitcast_convert_type(w & jnp.uint32(0xFFFF0000), F32)
    return lo, hi


def _unpack_bf16_pairs(w):
    lo, hi = _unpack_bf16_pairs_f32(w)
    return lo.astype(BF16), hi.astype(BF16)


def _router_kernel(h_ref, g_ref, wt_ref, b_ref, eid_ref, gate_ref, xn_ref):
    n = _rms(h_ref[...], g_ref[...])
    nh = n.astype(BF16)
    _store_row_tiles(xn_ref, _pack_bf16_pairs(nh))
    lg = _dot_nt(wt_ref[...].astype(BF16), nh) + b_ref[...]

    tm = lg.shape[1]
    row = lax.broadcasted_iota(I32, (SUBLANES, tm), 0)

    def first_argmax(v, vmax):
        return jnp.min(jnp.where(v == vmax, row, SUBLANES), axis=0, keepdims=True)

    gl = jnp.where(row < N_GROUPS, lg[0:SUBLANES], NEG_INF)
    gmax = jnp.max(gl, axis=0, keepdims=True)
    grp = first_argmax(gl, gmax)
    grp_w = 1.0 / jnp.sum(jnp.exp(gl - gmax), axis=0, keepdims=True)

    sel = jnp.zeros((SUBLANES, tm), F32)
    for gi in range(N_GROUPS):
        lo = SUBLANES + gi * EXPERTS_PER_GROUP
        sel = jnp.where(grp == gi, lg[lo:lo + EXPERTS_PER_GROUP], sel)
    e = jnp.exp(sel - jnp.max(sel, axis=0, keepdims=True))
    prob = e / jnp.sum(e, axis=0, keepdims=True)
    p1 = jnp.max(prob, axis=0, keepdims=True)
    i1 = first_argmax(prob, p1)
    rest = jnp.where(row == i1, -1.0, prob)
    p2 = jnp.max(rest, axis=0, keepdims=True)
    i2 = first_argmax(rest, p2)
    den = p1 + p2
    base = grp * EXPERTS_PER_GROUP
    eid_ref[...] = jnp.where(row == 0, base + i1, jnp.where(row == 1, base + i2, 0))
    gate_ref[...] = jnp.where(row == 0, grp_w * p1 / den, jnp.where(row == 1, grp_w * p2 / den, 0.0))


def _router(h2, g_ffn, w_rt, b_r, *, tm):
    T, D = h2.shape
    return pl.pallas_call(
        _router_kernel,
        grid=(T // tm,),
        in_specs=[
            pl.BlockSpec((tm, D), lambda i: (i, 0)),
            pl.BlockSpec((1, D), lambda i: (0, 0)),
            pl.BlockSpec((R_ROWS, D), lambda i: (0, 0)),
            pl.BlockSpec((R_ROWS, 1), lambda i: (0, 0)),
        ],
        out_specs=[
            pl.BlockSpec((SUBLANES, tm), lambda i: (0, i)),
            pl.BlockSpec((SUBLANES, tm), lambda i: (0, i)),
            pl.BlockSpec((tm, SUBLANES, LANES), lambda i: (i, 0, 0)),
        ],
        out_shape=[
            jax.ShapeDtypeStruct((SUBLANES, T), I32),
            jax.ShapeDtypeStruct((SUBLANES, T), F32),
            jax.ShapeDtypeStruct((T, SUBLANES, LANES), jnp.uint32),
        ],
        compiler_params=_cparams(("parallel",), 32),
        name="router",
    )(h2, g_ffn, w_rt, b_r)


def _rank_kernel(eid_ref, rank_ref, cnt_ref, carry):
    @pl.when(pl.program_id(0) == 0)
    def _():
        carry[...] = jnp.zeros(carry.shape, F32)

    tm = eid_ref.shape[1]
    e0 = eid_ref[0:1, :]
    e1 = eid_ref[1:2, :]
    erow = lax.broadcasted_iota(I32, (N_EXPERTS, tm), 0)
    hit = jnp.where((erow == e0) | (erow == e1), 1.0, 0.0)
    r = lax.broadcasted_iota(I32, (tm, tm), 0)
    c = lax.broadcasted_iota(I32, (tm, tm), 1)
    before = jnp.where(r < c, 1.0, 0.0).astype(BF16)
    pre = _dot(hit.astype(BF16), before) + carry[:, 0:1]
    rank0 = jnp.sum(jnp.where(erow == e0, pre, 0.0), axis=0, keepdims=True)
    rank1 = jnp.sum(jnp.where(erow == e1, pre, 0.0), axis=0, keepdims=True)
    row = lax.broadcasted_iota(I32, (SUBLANES, tm), 0)
    rank_ref[...] = jnp.where(row == 0, rank0, jnp.where(row == 1, rank1, 0.0)).astype(I32)
    total = carry[...] + jnp.sum(hit, axis=1, keepdims=True)
    carry[...] = total
    cnt_ref[...] = total.astype(I32)


def _rank(eid, *, tm):
    T = eid.shape[1]
    return pl.pallas_call(
        _rank_kernel,
        grid=(T // tm,),
        in_specs=[pl.BlockSpec((SUBLANES, tm), lambda i: (0, i))],
        out_specs=[
            pl.BlockSpec((SUBLANES, tm), lambda i: (0, i)),
            pl.BlockSpec((N_EXPERTS, LANES), lambda i: (0, 0)),
        ],
        out_shape=[
            jax.ShapeDtypeStruct((SUBLANES, T), I32),
            jax.ShapeDtypeStruct((N_EXPERTS, LANES), I32),
        ],
        scratch_shapes=[pltpu.VMEM((N_EXPERTS, LANES), F32)],
        compiler_params=_cparams(("arbitrary",), 32),
        name="rank",
    )(eid)


def _dispatch_kernel(dest_ref, pad0_ref, npad_ref, nb_ref, xn_ref, xb_hbm, zbuf, sem, psem,
                     *, n_tok, tb, n_blocks):
    i = pl.program_id(0)
    tm = xn_ref.shape[0]
    base = i * tm

    def row_body(r, carry):
        for k in range(TOP_K):
            pltpu.make_async_copy(xn_ref.at[r], xb_hbm.at[dest_ref[k * n_tok + base + r]], sem).start(priority=k)
        return carry
    lax.fori_loop(0, tm, row_body, 0, unroll=8)

    @pl.when(i == 0)
    def _():
        _dispatch_fill(pad0_ref, npad_ref, nb_ref, xb_hbm, zbuf, psem, tb=tb, n_blocks=n_blocks)

    for k in range(TOP_K):
        pltpu.make_async_copy(xn_ref, xb_hbm.at[pl.ds(0, tm)], sem).wait()


def _dispatch_fill(pad0_ref, npad_ref, nb_ref, xb_hbm, zbuf, psem, *, tb, n_blocks):
    zbuf[...] = jnp.zeros(zbuf.shape, zbuf.dtype)

    def pad_copy(e):
        n = npad_ref[e]
        return pltpu.make_async_copy(zbuf.at[pl.ds(0, n)], xb_hbm.at[pl.ds(pad0_ref[e], n)], psem.at[0])

    def tail_copy(blk):
        return pltpu.make_async_copy(zbuf, xb_hbm.at[pl.ds(pl.multiple_of(blk * tb, tb), tb)], psem.at[1])

    def for_each_pad(fn):
        def body(e, c):
            @pl.when(npad_ref[e] > 0)
            def _():
                fn(e)
            return c
        lax.fori_loop(0, N_EXPERTS, body, 0)

    def for_each_tail(fn):
        def body(b, c):
            fn(b)
            return c
        lax.fori_loop(nb_ref[0], n_blocks, body, 0)

    for_each_pad(lambda e: pad_copy(e).start())
    for_each_tail(lambda b: tail_copy(b).start())
    for_each_pad(lambda e: pad_copy(e).wait())
    for_each_tail(lambda b: tail_copy(b).wait())


def _dispatch(dest, pad0, npad, nb, xn, *, tb, n_blocks, tm):
    T = xn.shape[0]
    tile = xn.shape[1:]
    kern = functools.partial(_dispatch_kernel, n_tok=T, tb=tb, n_blocks=n_blocks)
    grid_spec = pltpu.PrefetchScalarGridSpec(
        num_scalar_prefetch=4,
        grid=(T // tm,),
        in_specs=[pl.BlockSpec((tm,) + tile, lambda i, *_: (i, 0, 0))],
        out_specs=pl.BlockSpec(memory_space=pl.ANY),
        scratch_shapes=[
            pltpu.VMEM((tb,) + tile, xn.dtype),
            pltpu.SemaphoreType.DMA(()),
            pltpu.SemaphoreType.DMA((2,)),
        ],
    )
    return pl.pallas_call(
        kern,
        grid_spec=grid_spec,
        out_shape=jax.ShapeDtypeStruct((n_blocks * tb,) + tile, xn.dtype),
        compiler_params=_cparams(("arbitrary",), 32),
        name="dispatch",
    )(dest, pad0, npad, nb, xn)


def _expert_kernel(be_ref, first_ref, slot_ref, nxt_ref, run1_ref, nb_ref, x_ref, wg_hbm, wu_hbm, wd_hbm, y_ref,
                   wg_f, wu_f, wd_f, wsem, wg_b, wu_b, wd_b):
    i = pl.program_id(0)
    nb = nb_ref[0]

    def weight_copies(e, slot):
        return (pltpu.make_async_copy(wg_hbm.at[e], wg_f.at[slot], wsem.at[3 * slot]),
                pltpu.make_async_copy(wu_hbm.at[e], wu_f.at[slot], wsem.at[3 * slot + 1]),
                pltpu.make_async_copy(wd_hbm.at[e], wd_f.at[slot], wsem.at[3 * slot + 2]))

    @pl.when(i == 0)
    def _():
        for cp in weight_copies(be_ref[0], 0):
            cp.start()

    @pl.when((i == 0) & (run1_ref[0] >= 0))
    def _():
        for cp in weight_copies(jnp.maximum(run1_ref[0], 0), 1):
            cp.start()

    first = (i < nb) & (first_ref[i] == 1)
    slot = slot_ref[i]

    @pl.when(first)
    def _():
        for cp in weight_copies(0, slot):
            cp.wait()
        wg_b[...] = wg_f[slot].astype(BF16)
        wu_b[...] = wu_f[slot].astype(BF16)
        wd_b[...] = wd_f[slot].astype(BF16)

    @pl.when(first & (nxt_ref[i] >= 0))
    def _():
        for cp in weight_copies(jnp.maximum(nxt_ref[i], 0), slot):
            cp.start()

    @pl.when(i < nb)
    def _():
        n_lo, n_hi = _unpack_bf16_pairs(_load_row_tiles(x_ref))
        half = n_lo.shape[1]
        a = _dot(n_lo, wg_b[0:half, :]) + _dot(n_hi, wg_b[half:, :])
        u = _dot(n_lo, wu_b[0:half, :]) + _dot(n_hi, wu_b[half:, :])
        hdn = (a / (1.0 + jnp.exp(-a))) * u
        y_ref[...] = _pack_bf16_pairs(_dot(hdn.astype(BF16), wd_b[...]).astype(BF16))

    @pl.when(i >= nb)
    def _():
        y_ref[...] = jnp.zeros(y_ref.shape, y_ref.dtype)


def _experts(xb, blk_e, blk_first, blk_slot, blk_next, run1, nb, w_gate, w_up, w_down, *, tb, n_blocks):
    D, De = w_gate.shape[1:]
    assert xb.shape[1:] == (SUBLANES, LANES) and D == 2 * SUBLANES * LANES

    def x_map(i, be, first, slot, nxt, r1, nbr):
        return (jnp.minimum(i, nbr[0] - 1), 0, 0)

    grid_spec = pltpu.PrefetchScalarGridSpec(
        num_scalar_prefetch=6,
        grid=(n_blocks,),
        in_specs=[
            pl.BlockSpec((tb, SUBLANES, LANES), x_map),
            pl.BlockSpec(memory_space=pl.ANY),
            pl.BlockSpec(memory_space=pl.ANY),
            pl.BlockSpec(memory_space=pl.ANY),
        ],
        out_specs=pl.BlockSpec((tb, D // 2), lambda i, *_: (i, 0)),
        scratch_shapes=[
            pltpu.VMEM((2, D, De), F32),
            pltpu.VMEM((2, D, De), F32),
            pltpu.VMEM((2, De, D), F32),
            pltpu.SemaphoreType.DMA((6,)),
            pltpu.VMEM((D, De), BF16),
            pltpu.VMEM((D, De), BF16),
            pltpu.VMEM((De, D), BF16),
        ],
    )
    return pl.pallas_call(
        _expert_kernel,
        grid_spec=grid_spec,
        out_shape=jax.ShapeDtypeStruct((n_blocks * tb, D // 2), jnp.uint32),
        compiler_params=_cparams(("arbitrary",), 52),
        name="experts",
    )(blk_e, blk_first, blk_slot, blk_next, run1, nb, xb, w_gate, w_up, w_down)


def _combine_kernel(dest_ref, y_hbm, h_ref, gate_ref, o_ref, ybuf, sem, *, tm, n_tok):
    i = pl.program_id(0)
    n = pl.num_programs(0)

    def row_copy(d, k, r, slot):
        return pltpu.make_async_copy(y_hbm.at[pl.ds(d, 1), :], ybuf.at[slot, k, pl.ds(r, 1), :], sem.at[slot])

    def start_gather(blk, slot):
        def body(r, carry):
            for k in range(TOP_K):
                row_copy(dest_ref[k * n_tok + blk * tm + r], k, r, slot).start(priority=k)
            return carry
        lax.fori_loop(0, tm, body, 0, unroll=8)

    def wait_gather(slot):
        for k in range(TOP_K):
            pltpu.make_async_copy(y_hbm.at[pl.ds(0, tm), :], ybuf.at[slot, k], sem.at[slot]).wait()

    @pl.when(i == 0)
    def _():
        start_gather(0, 0)

    @pl.when(i + 1 < n)
    def _():
        start_gather(i + 1, (i + 1) % 2)

    slot = i % 2
    wait_gather(slot)
    gt = gate_ref[...]
    half = h_ref.shape[1] // 2
    y0_lo, y0_hi = _unpack_bf16_pairs_f32(ybuf[slot, 0])
    y1_lo, y1_hi = _unpack_bf16_pairs_f32(ybuf[slot, 1])
    o_ref[:, 0:half] = h_ref[:, 0:half] + gt[:, 0:1] * y0_lo + gt[:, 1:2] * y1_lo
    o_ref[:, half:] = h_ref[:, half:] + gt[:, 0:1] * y0_hi + gt[:, 1:2] * y1_hi


def _combine(dest, yb, h2, gate, *, tm):
    T, D = h2.shape
    kern = functools.partial(_combine_kernel, tm=tm, n_tok=T)
    grid_spec = pltpu.PrefetchScalarGridSpec(
        num_scalar_prefetch=1,
        grid=(T // tm,),
        in_specs=[
            pl.BlockSpec(memory_space=pl.ANY),
            pl.BlockSpec((tm, D), lambda i, d: (i, 0)),
            pl.BlockSpec((tm, TOP_K), lambda i, d: (i, 0)),
        ],
        out_specs=pl.BlockSpec((tm, D), lambda i, d: (i, 0)),
        scratch_shapes=[
            pltpu.VMEM((2, TOP_K, tm, D // 2), jnp.uint32),
            pltpu.SemaphoreType.DMA((2,)),
        ],
    )
    return pl.pallas_call(
        kern,
        grid_spec=grid_spec,
        out_shape=jax.ShapeDtypeStruct((T, D), F32),
        compiler_params=_cparams(("arbitrary",), 40),
        name="combine",
    )(dest, yb, h2, gate)


class _Tiles(NamedTuple):
    proj_rows: int = 1024
    proj_cols: int = 512
    attn_rows: int = 512
    gla_rows: int = 2048
    token_rows: int = 512
    dispatch_rows: int = 2048
    expert_rows: int = 256


TILES = _Tiles()


def kernel(x, mem, positions, g_attn, w_in, q_norm_g, k_norm_g, lambda_q1, lambda_k1, lambda_q2, lambda_k2, diff_subln_g, gla_w_a2, gla_b_a, gla_out_g, w_out, g_cross, g_mem, w_cq, w_ckv, cq_norm_g, ck_norm_g, w_co, g_ffn, w_router_grp, b_router_grp, w_router_exp, b_router_exp, w_gate, w_up, w_down):
    B, S, D = x.shape
    T = B * S
    n_mem = mem.shape[1]
    l = 0
    x2 = x.reshape(T, D)

    half = DIFF_QKDIM // 2
    freq = ROPE_THETA ** (-jnp.arange(half, dtype=F32) / half)
    freq = jnp.tile(freq, LANES // half)[None, :]
    q_scale = math.log2(math.e) * DIFF_QKDIM ** -0.5
    qkg = jnp.stack([jnp.tile(q_norm_g[l], 2) * q_scale, jnp.tile(k_norm_g[l], 2)])
    score_bound = 1.01 * DIFF_QKDIM * q_scale * jnp.max(jnp.abs(q_norm_g[l])) * jnp.max(jnp.abs(k_norm_g[l]))
    lvec = jnp.stack([lambda_q1[l], lambda_k1[l], lambda_q2[l], lambda_k2[l]])

    qk, mid, log_a, sgr = _inproj(x2, g_attn[l][None], positions.reshape(T, 1), freq, qkg, w_in[l].T,
                                  gla_w_a2[l], gla_b_a[l][None], tm=TILES.proj_rows)
    diffattn = functools.partial(_diffattn, lvec, qk, mid, diff_subln_g[l][None], B=B, S=S, tq=TILES.attn_rows)
    mix_d = lax.cond(score_bound <= SCORE_BOUND,
                     functools.partial(diffattn, bounded=True), functools.partial(diffattn, bounded=False))
    mix_g = _gla(mid, log_a, sgr, gla_out_g[l][None], B=B, S=S, blk=TILES.gla_rows)
    h1 = _outproj(mix_d, mix_g, w_out[l], x2, tm=TILES.proj_rows, tn=TILES.proj_cols)

    hdim = D // CROSS_HEADS
    qc = _normproj(h1, g_cross[l][None], w_cq[l], cq_norm_g[l][None] * (hdim ** -0.5),
                   tm=TILES.proj_rows, tn=hdim, n_norm=CROSS_HEADS, name="cq")
    kv = _normproj(mem.reshape(B * n_mem, D), g_mem[l][None], w_ckv[l], ck_norm_g[l][None],
                   tm=B * n_mem, tn=hdim, n_norm=CROSS_HEADS, name="ckv")
    h2 = _cross(qc, kv, w_co[l], h1, S=S, n_mem=n_mem, tm=TILES.proj_rows, tn=TILES.proj_cols)

    w_rt = jnp.concatenate([w_router_grp[l].T, jnp.zeros((SUBLANES - N_GROUPS, D), F32), w_router_exp[l].T])
    b_r = jnp.concatenate([b_router_grp[l], jnp.zeros((SUBLANES - N_GROUPS,), F32), b_router_exp[l]])[:, None]
    eid, gate, xn = _router(h2, g_ffn[l][None], w_rt, b_r, tm=TILES.token_rows)
    rank, cnt = _rank(eid, tm=TILES.token_rows)

    assert TOP_K == 2
    tb = TILES.expert_rows
    n_blocks = (T * TOP_K + N_EXPERTS * (tb - 1) + tb - 1) // tb
    counts = cnt[:, 0]
    pcounts = ((counts + tb - 1) // tb) * tb
    pends = jnp.cumsum(pcounts)
    pstarts = pends - pcounts
    nb = (pends[-1:] // tb).astype(I32)
    eids = jnp.arange(N_EXPERTS, dtype=I32)
    blk_start = jnp.arange(n_blocks, dtype=I32) * tb
    blk_e = jnp.minimum(jnp.sum(pends[None, :] <= blk_start[:, None], axis=1), N_EXPERTS - 1).astype(I32)
    blk_first = jnp.concatenate([jnp.ones((1,), I32), (blk_e[1:] != blk_e[:-1]).astype(I32)])
    used = jnp.where(counts > 0, eids, N_EXPERTS)
    next_used = jnp.concatenate([lax.cummin(used[::-1])[::-1][1:], jnp.full((1,), N_EXPERTS, I32)])
    next_used = jnp.where(next_used < N_EXPERTS, next_used, -1)

    def table_at(idx, table):
        return jnp.where(idx >= 0, jnp.sum(jnp.where(idx[:, None] == eids, table, 0), axis=1), -1).astype(I32)

    after_next = table_at(next_used, next_used)
    run_of = (jnp.cumsum((counts > 0).astype(I32)) - 1).astype(I32)
    blk_next = table_at(blk_e, after_next)
    blk_slot = table_at(blk_e, run_of % 2)
    run1 = table_at(blk_e[:1], next_used)
    pick = eid[:TOP_K]
    pstart_of = jnp.sum(jnp.where(pick[..., None] == eids, pstarts, 0), axis=-1)
    dest = (pstart_of + rank[:TOP_K]).astype(I32).reshape(-1)
    pad0 = (pstarts + counts).astype(I32)
    npad = (pcounts - counts).astype(I32)

    xb = _dispatch(dest, pad0, npad, nb, xn, tb=tb, n_blocks=n_blocks, tm=TILES.dispatch_rows)
    yb = _experts(xb, blk_e, blk_first, blk_slot, blk_next, run1, nb, w_gate[l], w_up[l], w_down[l],
                  tb=tb, n_blocks=n_blocks)
    out = _combine(dest, yb, h2, gate[:TOP_K].T, tm=TILES.token_rows)
    return out.reshape(B, S, D)
```

```python
import functools
import math
from typing import NamedTuple

import jax
import jax.numpy as jnp
from jax import lax
from jax.experimental import pallas as pl
from jax.experimental.pallas import tpu as pltpu

F32 = jnp.float32
BF16 = jnp.bfloat16
I32 = jnp.int32

LANES = 128
SUBLANES = 8

CHUNK = 64
ROPE_THETA = 10000.0
NORM_EPS = 1e-6
NEG_INF = -1e30
DIFF_HEADS = 8
DIFF_VDIM = 128
DIFF_QKDIM = 64
GLA_HEADS = 4
GLA_VDIM = 256
GLA_KDIM = 128
GLA_GATE_RANK = 16
GLA_TAU = 16.0
CROSS_HEADS = 4
N_GROUPS = 4
EXPERTS_PER_GROUP = 8
N_EXPERTS = N_GROUPS * EXPERTS_PER_GROUP
TOP_K = 2
LAM_INIT = 0.8 - 0.6 * math.exp(-0.3 * 0)

NT_DIMS = (((1,), (1,)), ((), ()))


def _cparams(semantics, vmem_mib):
    return pltpu.CompilerParams(dimension_semantics=semantics,
                                vmem_limit_bytes=vmem_mib * 1024 * 1024)


def _dot(a, b):
    return jnp.dot(a, b, preferred_element_type=F32)


def _dot_nt(a, b):
    return lax.dot_general(a, b, NT_DIMS, preferred_element_type=F32)


def _rms(x, g):
    ms = jnp.mean(x * x, axis=-1, keepdims=True)
    return x * lax.rsqrt(ms + NORM_EPS) * g


def _split_bf16(x):
    hi = x.astype(BF16)
    lo = (x - hi.astype(F32)).astype(BF16)
    return hi, lo


TN = 512
J_QK = 4
J_MID = 6
J_LR = J_QK + J_MID
J_GR = J_LR + 1
N_J = J_GR + 2


def _inproj_kernel(x_ref, g_ref, pos_ref, freq_ref, qkg_ref, w_ref, wgr_ref, wlr_ref, wa2_ref, ba_ref,
                   qk_ref, mid_ref, loga_ref, sgr_ref, n_scr, cos_scr, sin_scr, y_scr):
    j = pl.program_id(1)

    @pl.when(j == 0)
    def _():
        n_scr[...] = _rms(x_ref[...], g_ref[...]).astype(BF16)
        ang = pos_ref[...].astype(F32) * freq_ref[...]
        cos_scr[...] = jnp.cos(ang)
        sin_scr[...] = jnp.sin(ang)

    def qk_epilogue(jq):
        y_prev = y_scr.at[jq % 2]
        lane = lax.broadcasted_iota(I32, (1, LANES), 1)
        low_seg = lane < DIFF_QKDIM
        first_half = (lane % DIFF_QKDIM) < (DIFF_QKDIM // 2)
        gain = qkg_ref[jq // (J_QK // 2):jq // (J_QK // 2) + 1, :]
        cos = cos_scr[...]
        sin = sin_scr[...]
        for c in range(TN // LANES):
            yb = y_prev[:, c * LANES:(c + 1) * LANES]
            y2 = yb * yb
            s_lo = jnp.sum(jnp.where(low_seg, y2, 0.0), axis=-1, keepdims=True)
            s_hi = jnp.sum(jnp.where(low_seg, 0.0, y2), axis=-1, keepdims=True)
            ms = jnp.where(low_seg, s_lo, s_hi) * (1.0 / DIFF_QKDIM)
            yn = yb * lax.rsqrt(ms + NORM_EPS) * gain
            rot = jnp.where(first_half,
                            -pltpu.roll(yn, LANES - DIFF_QKDIM // 2, 1),
                            pltpu.roll(yn, DIFF_QKDIM // 2, 1))
            qk_ref[:, c * LANES:(c + 1) * LANES] = (yn * cos + rot * sin).astype(BF16)

    for jq in range(J_QK + 1):
        @pl.when(j == jq)
        def _():
            y = _dot_nt(n_scr[...], w_ref[...].astype(BF16))
            if jq > 0:
                qk_epilogue(jq - 1)
            if jq < J_QK:
                y_scr[jq % 2] = y
            else:
                mid_ref[...] = y.astype(BF16)

    @pl.when((j > J_QK) & (j < J_LR))
    def _():
        mid_ref[...] = _dot_nt(n_scr[...], w_ref[...].astype(BF16)).astype(BF16)

    @pl.when(j == J_LR)
    def _():
        lr = _dot_nt(n_scr[...], wlr_ref[...].astype(BF16))
        z = _dot(lr.astype(BF16), wa2_ref[...].astype(BF16)) + ba_ref[...]
        log_sig = jnp.minimum(z, 0.0) - jnp.log(1.0 + jnp.exp(-jnp.abs(z)))
        loga_ref[...] = log_sig * (1.0 / GLA_TAU)

    @pl.when(j >= J_GR)
    def _():
        y = _dot_nt(n_scr[...], wgr_ref[...].astype(BF16))
        sgr_ref[...] = (y / (1.0 + jnp.exp(-y))).astype(BF16)


def _inproj(x2, g_attn, pos2, freq, qkg, w_t, w_a2, b_a, *, tm):
    T, D = x2.shape
    n_mid = J_MID * TN
    n_gk = GLA_HEADS * GLA_KDIM
    lr0 = J_LR * TN
    gr0 = lr0 + GLA_GATE_RANK
    n_gr = w_t.shape[0] - gr0
    assert n_gr == 2 * TN and lr0 % GLA_GATE_RANK == 0
    return pl.pallas_call(
        _inproj_kernel,
        grid=(T // tm, N_J),
        in_specs=[
            pl.BlockSpec((tm, D), lambda i, j: (i, 0)),
            pl.BlockSpec((1, D), lambda i, j: (0, 0)),
            pl.BlockSpec((tm, 1), lambda i, j: (i, 0)),
            pl.BlockSpec((1, LANES), lambda i, j: (0, 0)),
            pl.BlockSpec((2, LANES), lambda i, j: (0, 0)),
            pl.BlockSpec((TN, D), lambda i, j: (jnp.minimum(j, J_LR - 1), 0)),
            pl.BlockSpec((pl.Element(TN), pl.Element(D)),
                         lambda i, j: (pl.multiple_of(gr0 + TN * jnp.clip(j - J_GR, 0, 1), SUBLANES), 0)),
            pl.BlockSpec((GLA_GATE_RANK, D), lambda i, j: (lr0 // GLA_GATE_RANK, 0)),
            pl.BlockSpec((GLA_GATE_RANK, n_gk), lambda i, j: (0, 0)),
            pl.BlockSpec((1, n_gk), lambda i, j: (0, 0)),
        ],
        out_specs=[
            pl.BlockSpec((tm, TN), lambda i, j: (i, jnp.clip(j - 1, 0, J_QK - 1))),
            pl.BlockSpec((tm, TN), lambda i, j: (i, jnp.clip(j - J_QK, 0, J_MID - 1))),
            pl.BlockSpec((tm, n_gk), lambda i, j: (i, 0)),
            pl.BlockSpec((tm, TN), lambda i, j: (i, jnp.clip(j - J_GR, 0, 1))),
        ],
        out_shape=[
            jax.ShapeDtypeStruct((T, J_QK * TN), BF16),
            jax.ShapeDtypeStruct((T, n_mid), BF16),
            jax.ShapeDtypeStruct((T, n_gk), F32),
            jax.ShapeDtypeStruct((T, n_gr), BF16),
        ],
        scratch_shapes=[
            pltpu.VMEM((tm, D), BF16),
            pltpu.VMEM((tm, LANES), F32),
            pltpu.VMEM((tm, LANES), F32),
            pltpu.VMEM((2, tm, TN), F32),
        ],
        compiler_params=_cparams(("parallel", "arbitrary"), 56),
        name="inproj",
    )(x2, g_attn, pos2, freq, qkg, w_t, w_t, w_t, w_a2, b_a)


SCORE_BOUND = 80.0


def _diffattn_kernel(ti_ref, tj_ref, lv_ref, q_ref, k_ref, v_ref, sg_ref, o_ref,
                     vext, diag_mask, acc1, acc2, m1, m2, *, tq, n_tiles, bounded):
    S = q_ref.shape[0]
    nq = S // tq
    half = tq // 2
    unroll = 28
    diag_unroll = 8 if nq % 8 == 0 else 2

    vext[:, 0:DIFF_VDIM] = v_ref[...]
    vext[:, DIFF_VDIM:] = jnp.ones((S, DIFF_VDIM), BF16)
    row_chunk = lax.broadcasted_iota(I32, (tq, tq), 0) // CHUNK
    col_chunk = lax.broadcasted_iota(I32, (tq, tq), 1) // CHUNK
    diag_mask[...] = jnp.where(col_chunk <= row_chunk, 1.0, 0.0).astype(BF16)

    lane = lax.broadcasted_iota(I32, (1, LANES), 1)

    def block(i):
        return pl.ds(pl.multiple_of(i * tq, tq), tq)

    def q_comps(i):
        q = q_ref[block(i), :]
        zero = jnp.zeros_like(q)
        return jnp.where(lane < DIFF_QKDIM, q, zero), jnp.where(lane < DIFF_QKDIM, zero, q)

    def diag_tile(i):
        q1, q2 = q_comps(i)
        for qc, acc, m in ((q1, acc1, m1), (q2, acc2, m2)):
            for lo, n_keys in ((0, half), (half, tq)):
                rows = pl.ds(pl.multiple_of(i * tq + lo, half), half)
                keys = pl.ds(pl.multiple_of(i * tq, tq), n_keys)
                mask = diag_mask[lo:lo + half, 0:n_keys]
                s = _dot_nt(qc[lo:lo + half], k_ref[keys, :])
                if bounded:
                    acc[rows, :] = _dot(jnp.exp2(s).astype(BF16) * mask, vext[keys, :])
                else:
                    s = jnp.where(mask > 0, s, NEG_INF)
                    m_new = jnp.max(s, axis=-1, keepdims=True)
                    acc[rows, :] = _dot(jnp.exp2(s - m_new).astype(BF16), vext[keys, :])
                    m[rows, :] = m_new

    def full_tile(i, j):
        q1, q2 = q_comps(i)
        rows = block(i)
        k = k_ref[block(j), :]
        v = vext[block(j), :]
        for qc, acc, m in ((q1, acc1, m1), (q2, acc2, m2)):
            s = _dot_nt(qc, k)
            if bounded:
                acc[rows, :] += _dot(jnp.exp2(s).astype(BF16), v)
            else:
                m_old = m[rows, :]
                m_new = jnp.maximum(m_old, jnp.max(s, axis=-1, keepdims=True))
                p = jnp.exp2(s - m_new)
                acc[rows, :] = jnp.exp2(m_old - m_new) * acc[rows, :] + _dot(p.astype(BF16), v)
                m[rows, :] = m_new

    def diag_body(t, carry):
        for u in range(diag_unroll):
            diag_tile(diag_unroll * t + u)
        return carry
    lax.fori_loop(0, nq // diag_unroll, diag_body, 0)

    def full_body(t, carry):
        for u in range(unroll):
            full_tile(ti_ref[unroll * t + u], tj_ref[unroll * t + u])
        return carry
    lax.fori_loop(0, n_tiles // unroll, full_body, 0)
    for t in range(n_tiles - n_tiles % unroll, n_tiles):
        full_tile(ti_ref[t], tj_ref[t])

    lv = lv_ref[...]
    lam = (jnp.exp(jnp.sum(lv[0:1] * lv[1:2], axis=-1, keepdims=True))
           - jnp.exp(jnp.sum(lv[2:3] * lv[3:4], axis=-1, keepdims=True)) + LAM_INIT)

    def out_body(i, carry):
        rows = block(i)
        a1 = acc1[rows, :]
        a2 = acc2[rows, :]
        o = a1[:, :DIFF_VDIM] / a1[:, DIFF_VDIM:] - lam * (a2[:, :DIFF_VDIM] / a2[:, DIFF_VDIM:])
        o_ref[rows, :] = (_rms(o, sg_ref[...]) * (1.0 - LAM_INIT)).astype(BF16)
        return carry
    lax.fori_loop(0, nq, out_body, 0)


def _diffattn(lvec, qk, mid, subln_g, *, B, S, tq, bounded):
    T = B * S
    nq = S // tq
    assert nq % 2 == 0
    tiles = [(i, j) for i in range(nq) for j in range(i)]
    ti = jnp.asarray([t[0] for t in tiles], I32)
    tj = jnp.asarray([t[1] for t in tiles], I32)
    kern = functools.partial(_diffattn_kernel, tq=tq, n_tiles=len(tiles), bounded=bounded)
    m_rows = SUBLANES if bounded else S
    grid_spec = pltpu.PrefetchScalarGridSpec(
        num_scalar_prefetch=2,
        grid=(B, DIFF_HEADS),
        in_specs=[
            pl.BlockSpec((4, DIFF_QKDIM), lambda b, h, *_: (0, 0)),
            pl.BlockSpec((S, LANES), lambda b, h, *_: (b, h)),
            pl.BlockSpec((S, LANES), lambda b, h, *_: (b, DIFF_HEADS + h)),
            pl.BlockSpec((S, LANES), lambda b, h, *_: (b, h)),
            pl.BlockSpec((1, DIFF_VDIM), lambda b, h, *_: (0, 0)),
        ],
        out_specs=pl.BlockSpec((S, DIFF_VDIM), lambda b, h, *_: (b, h)),
        scratch_shapes=[
            pltpu.VMEM((S, 2 * DIFF_VDIM), BF16),
            pltpu.VMEM((tq, tq), BF16),
            pltpu.VMEM((S, 2 * DIFF_VDIM), F32),
            pltpu.VMEM((S, 2 * DIFF_VDIM), F32),
            pltpu.VMEM((m_rows, 1), F32),
            pltpu.VMEM((m_rows, 1), F32),
        ],
    )
    return pl.pallas_call(
        kern,
        grid_spec=grid_spec,
        out_shape=jax.ShapeDtypeStruct((T, DIFF_HEADS * DIFF_VDIM), BF16),
        compiler_params=_cparams(("parallel", "parallel"), 40),
        name="diffattn_bounded" if bounded else "diffattn_online",
    )(ti, tj, lvec, qk, qk, mid, subln_g)


def _gla_kernel(q_ref, k_ref, v_ref, la_ref, sgr_ref, g_ref, scan_ref, o_ref, state, *, blk):
    @pl.when(pl.program_id(2) == 0)
    def _():
        state[...] = jnp.zeros(state.shape, F32)

    la_t = la_ref[...].T
    k_t = k_ref[...].astype(F32).T
    hi, lo = _split_bf16(la_t)
    scan = scan_ref[...]
    cums, tots = [], []
    for sb in range(blk // LANES):
        slab = slice(sb * LANES, (sb + 1) * LANES)
        r = _dot(hi[:, slab], scan) + _dot(lo[:, slab], scan)
        cums.append(r[:, :LANES])
        tots.append(r[:, LANES:])
    cum_t = jnp.concatenate(cums, axis=1)
    tot_t = jnp.concatenate(tots, axis=1)
    kd_t = k_t * jnp.exp(tot_t - cum_t)

    n_chunks = blk // CHUNK
    lane = lax.broadcasted_iota(I32, (1, LANES), 1)
    d_states = []
    for ck in range(n_chunks):
        pair = slice((ck // 2) * LANES, (ck // 2 + 1) * LANES)
        in_chunk = (lane // CHUNK) == (ck % 2)
        kd = jnp.where(in_chunk, kd_t[:, pair], 0.0).astype(BF16)
        d_states.append(_dot(kd, v_ref[pair, :]))

    st = state[...]
    states = []
    for ck in range(n_chunks):
        decay = jnp.exp(tot_t[:, ck * CHUNK:ck * CHUNK + 1])
        st = decay * st + d_states[ck]
        states.append(st.astype(BF16))
    state[...] = st

    o = jnp.concatenate([_dot(q_ref[ck * CHUNK:(ck + 1) * CHUNK, :], states[ck]) for ck in range(n_chunks)],
                        axis=0) * (GLA_KDIM ** -0.5)
    o_ref[...] = (_rms(o, g_ref[...]) * sgr_ref[...].astype(F32)).astype(BF16)


def _gla(mid, log_a, sgr, out_g, *, B, S, blk):
    T = B * S
    ns = S // blk
    kern = functools.partial(_gla_kernel, blk=blk)
    q_col0 = (DIFF_HEADS * DIFF_VDIM) // GLA_KDIM
    k_col0 = q_col0 + GLA_HEADS
    v_col0 = (DIFF_HEADS * DIFF_VDIM + 2 * GLA_HEADS * GLA_KDIM) // GLA_VDIM
    assert LANES % CHUNK == 0 and blk % LANES == 0
    r = jnp.arange(LANES, dtype=I32)[:, None]
    c = jnp.arange(LANES, dtype=I32)[None, :]
    same = (r // CHUNK) == (c // CHUNK)
    scan = jnp.concatenate([(same & (r <= c)).astype(BF16), same.astype(BF16)], axis=1)
    return pl.pallas_call(
        kern,
        grid=(B, GLA_HEADS, ns),
        in_specs=[
            pl.BlockSpec((blk, GLA_KDIM), lambda b, h, s: (b * ns + s, q_col0 + h)),
            pl.BlockSpec((blk, GLA_KDIM), lambda b, h, s: (b * ns + s, k_col0 + h)),
            pl.BlockSpec((blk, GLA_VDIM), lambda b, h, s: (b * ns + s, v_col0 + h)),
            pl.BlockSpec((blk, GLA_KDIM), lambda b, h, s: (b * ns + s, h)),
            pl.BlockSpec((blk, GLA_VDIM), lambda b, h, s: (b * ns + s, h)),
            pl.BlockSpec((1, GLA_VDIM), lambda b, h, s: (0, 0)),
            pl.BlockSpec((LANES, 2 * LANES), lambda b, h, s: (0, 0)),
        ],
        out_specs=pl.BlockSpec((blk, GLA_VDIM), lambda b, h, s: (b * ns + s, h)),
        out_shape=jax.ShapeDtypeStruct((T, GLA_HEADS * GLA_VDIM), BF16),
        scratch_shapes=[pltpu.VMEM((GLA_KDIM, GLA_VDIM), F32)],
        compiler_params=_cparams(("parallel", "parallel", "arbitrary"), 32),
        name="gla",
    )(mid, mid, mid, log_a, sgr, out_g, scan)


def _resident_w_map(n_j):
    return lambda i, j: (0, jnp.where(i == 0, j, n_j - 1))


def _outproj_kernel(a_ref, b_ref, wa_ref, wb_ref, x_ref, o_ref, w_scr):
    j = pl.program_id(1)
    ka = a_ref.shape[1]

    @pl.when(pl.program_id(0) == 0)
    def _():
        w_scr[j, 0:ka, :] = wa_ref[...].astype(BF16)
        w_scr[j, ka:, :] = wb_ref[...].astype(BF16)

    acc = _dot(a_ref[...], w_scr[j, 0:ka, :]) + _dot(b_ref[...], w_scr[j, ka:, :])
    o_ref[...] = x_ref[...] + acc


def _outproj(a, b, w_out, x2, *, tm, tn):
    T, ka = a.shape
    kb = b.shape[1]
    assert ka == kb
    D = w_out.shape[1]
    n_j = D // tn
    return pl.pallas_call(
        _outproj_kernel,
        grid=(T // tm, n_j),
        in_specs=[
            pl.BlockSpec((tm, ka), lambda i, j: (i, 0)),
            pl.BlockSpec((tm, kb), lambda i, j: (i, 0)),
            pl.BlockSpec((ka, tn), _resident_w_map(n_j)),
            pl.BlockSpec((kb, tn), lambda i, j: (1, jnp.where(i == 0, j, n_j - 1))),
            pl.BlockSpec((tm, tn), lambda i, j: (i, j)),
        ],
        out_specs=pl.BlockSpec((tm, tn), lambda i, j: (i, j)),
        out_shape=jax.ShapeDtypeStruct((T, D), F32),
        scratch_shapes=[pltpu.VMEM((n_j, ka + kb, tn), BF16)],
        compiler_params=_cparams(("arbitrary", "arbitrary"), 48),
        name="outproj",
    )(a, b, w_out, w_out, x2)


def _normproj_kernel(x_ref, g_ref, w_ref, hg_ref, o_ref, n_scr, *w_scr, n_norm):
    j = pl.program_id(1)

    @pl.when(j == 0)
    def _():
        n_scr[...] = _rms(x_ref[...], g_ref[...]).astype(BF16)

    if w_scr:
        @pl.when(pl.program_id(0) == 0)
        def _():
            w_scr[0][j] = w_ref[...].astype(BF16)
        y = _dot(n_scr[...], w_scr[0][j])
    else:
        y = _dot(n_scr[...], w_ref[...].astype(BF16))

    @pl.when(j < n_norm)
    def _():
        o_ref[...] = _rms(y, hg_ref[...]).astype(BF16)

    @pl.when(j >= n_norm)
    def _():
        o_ref[...] = y.astype(BF16)


def _normproj(x2, g, w, head_g, *, tm, tn, n_norm, name):
    T, D = x2.shape
    N = w.shape[1]
    kern = functools.partial(_normproj_kernel, n_norm=n_norm)
    n_j = N // tn
    resident = T // tm > 1
    return pl.pallas_call(
        kern,
        grid=(T // tm, n_j),
        in_specs=[
            pl.BlockSpec((tm, D), lambda i, j: (i, 0)),
            pl.BlockSpec((1, D), lambda i, j: (0, 0)),
            pl.BlockSpec((D, tn), _resident_w_map(n_j) if resident else (lambda i, j: (0, j))),
            pl.BlockSpec((1, tn), lambda i, j: (0, 0)),
        ],
        out_specs=pl.BlockSpec((tm, tn), lambda i, j: (i, j)),
        out_shape=jax.ShapeDtypeStruct((T, N), BF16),
        scratch_shapes=[pltpu.VMEM((tm, D), BF16)] + ([pltpu.VMEM((n_j, D, tn), BF16)] if resident else []),
        compiler_params=_cparams(("arbitrary", "arbitrary"), 48),
        name=name,
    )(x2, g, w, head_g)


def _cross_kernel(q_ref, k_ref, v_ref, w_ref, h_ref, o_ref, att_scr, w_scr, *, hdim):
    j = pl.program_id(1)

    @pl.when(pl.program_id(0) == 0)
    def _():
        w_scr[j] = w_ref[...].astype(BF16)

    @pl.when(j == 0)
    def _():
        for hd in range(CROSS_HEADS):
            cols = slice(hd * hdim, (hd + 1) * hdim)
            s = lax.dot_general(q_ref[:, cols], k_ref[:, cols], NT_DIMS, preferred_element_type=F32)
            p = jnp.exp(s - jnp.max(s, axis=-1, keepdims=True))
            l = jnp.sum(p, axis=-1, keepdims=True)
            att_scr[:, cols] = (_dot(p.astype(BF16), v_ref[:, cols]) / l).astype(BF16)

    o_ref[...] = h_ref[...] + _dot(att_scr[...], w_scr[j])


def _cross(qc, kv, w_co, h1, *, S, n_mem, tm, tn):
    T, D = qc.shape
    per_b = S // tm
    kern = functools.partial(_cross_kernel, hdim=D // CROSS_HEADS)
    n_j = D // tn
    return pl.pallas_call(
        kern,
        grid=(T // tm, n_j),
        in_specs=[
            pl.BlockSpec((tm, D), lambda i, j: (i, 0)),
            pl.BlockSpec((n_mem, D), lambda i, j: (i // per_b, 0)),
            pl.BlockSpec((n_mem, D), lambda i, j: (i // per_b, 1)),
            pl.BlockSpec((D, tn), _resident_w_map(n_j)),
            pl.BlockSpec((tm, tn), lambda i, j: (i, j)),
        ],
        out_specs=pl.BlockSpec((tm, tn), lambda i, j: (i, j)),
        out_shape=jax.ShapeDtypeStruct((T, D), F32),
        scratch_shapes=[pltpu.VMEM((tm, D), BF16), pltpu.VMEM((n_j, D, tn), BF16)],
        compiler_params=_cparams(("arbitrary", "arbitrary"), 48),
        name="cross",
    )(qc, kv, kv, w_co, h1)


R_ROWS = SUBLANES + N_EXPERTS


def _pack_bf16_pairs(xb16):
    c = xb16.shape[1] // 2
    u = lax.bitcast_convert_type(xb16.astype(F32), jnp.uint32)
    return (u[:, :c] >> 16) | (u[:, c:] & jnp.uint32(0xFFFF0000))


def _store_row_tiles(ref, x):
    for g in range(SUBLANES):
        ref[:, g, :] = x[:, g * LANES:(g + 1) * LANES]


def _load_row_tiles(ref):
    return jnp.concatenate([ref[:, g, :] for g in range(SUBLANES)], axis=1)


def _unpack_bf16_pairs_f32(w):
    lo = lax.bitcast_convert_type(w << 16, F32)
    hi = lax.bitcast_convert_type(w & jnp.uint32(0xFFFF0000), F32)
    return lo, hi


def _unpack_bf16_pairs(w):
    lo, hi = _unpack_bf16_pairs_f32(w)
    return lo.astype(BF16), hi.astype(BF16)


def _router_kernel(h_ref, g_ref, wt_ref, b_ref, eid_ref, gate_ref, xn_ref):
    n = _rms(h_ref[...], g_ref[...])
    nh = n.astype(BF16)
    _store_row_tiles(xn_ref, _pack_bf16_pairs(nh))
    lg = _dot_nt(wt_ref[...].astype(BF16), nh) + b_ref[...]

    tm = lg.shape[1]
    row = lax.broadcasted_iota(I32, (SUBLANES, tm), 0)

    def first_argmax(v, vmax):
        return jnp.min(jnp.where(v == vmax, row, SUBLANES), axis=0, keepdims=True)

    gl = jnp.where(row < N_GROUPS, lg[0:SUBLANES], NEG_INF)
    gmax = jnp.max(gl, axis=0, keepdims=True)
    grp = first_argmax(gl, gmax)
    grp_w = 1.0 / jnp.sum(jnp.exp(gl - gmax), axis=0, keepdims=True)

    sel = jnp.zeros((SUBLANES, tm), F32)
    for gi in range(N_GROUPS):
        lo = SUBLANES + gi * EXPERTS_PER_GROUP
        sel = jnp.where(grp == gi, lg[lo:lo + EXPERTS_PER_GROUP], sel)
    e = jnp.exp(sel - jnp.max(sel, axis=0, keepdims=True))
    prob = e / jnp.sum(e, axis=0, keepdims=True)
    p1 = jnp.max(prob, axis=0, keepdims=True)
    i1 = first_argmax(prob, p1)
    rest = jnp.where(row == i1, -1.0, prob)
    p2 = jnp.max(rest, axis=0, keepdims=True)
    i2 = first_argmax(rest, p2)
    den = p1 + p2
    base = grp * EXPERTS_PER_GROUP
    eid_ref[...] = jnp.where(row == 0, base + i1, jnp.where(row == 1, base + i2, 0))
    gate_ref[...] = jnp.where(row == 0, grp_w * p1 / den, jnp.where(row == 1, grp_w * p2 / den, 0.0))


def _router(h2, g_ffn, w_rt, b_r, *, tm):
    T, D = h2.shape
    return pl.pallas_call(
        _router_kernel,
        grid=(T // tm,),
        in_specs=[
            pl.BlockSpec((tm, D), lambda i: (i, 0)),
            pl.BlockSpec((1, D), lambda i: (0, 0)),
            pl.BlockSpec((R_ROWS, D), lambda i: (0, 0)),
            pl.BlockSpec((R_ROWS, 1), lambda i: (0, 0)),
        ],
        out_specs=[
            pl.BlockSpec((SUBLANES, tm), lambda i: (0, i)),
            pl.BlockSpec((SUBLANES, tm), lambda i: (0, i)),
            pl.BlockSpec((tm, SUBLANES, LANES), lambda i: (i, 0, 0)),
        ],
        out_shape=[
            jax.ShapeDtypeStruct((SUBLANES, T), I32),
            jax.ShapeDtypeStruct((SUBLANES, T), F32),
            jax.ShapeDtypeStruct((T, SUBLANES, LANES), jnp.uint32),
        ],
        compiler_params=_cparams(("parallel",), 32),
        name="router",
    )(h2, g_ffn, w_rt, b_r)


def _rank_kernel(eid_ref, rank_ref, cnt_ref, carry):
    @pl.when(pl.program_id(0) == 0)
    def _():
        carry[...] = jnp.zeros(carry.shape, F32)

    tm = eid_ref.shape[1]
    e0 = eid_ref[0:1, :]
    e1 = eid_ref[1:2, :]
    erow = lax.broadcasted_iota(I32, (N_EXPERTS, tm), 0)
    hit = jnp.where((erow == e0) | (erow == e1), 1.0, 0.0)
    r = lax.broadcasted_iota(I32, (tm, tm), 0)
    c = lax.broadcasted_iota(I32, (tm, tm), 1)
    before = jnp.where(r < c, 1.0, 0.0).astype(BF16)
    pre = _dot(hit.astype(BF16), before) + carry[:, 0:1]
    rank0 = jnp.sum(jnp.where(erow == e0, pre, 0.0), axis=0, keepdims=True)
    rank1 = jnp.sum(jnp.where(erow == e1, pre, 0.0), axis=0, keepdims=True)
    row = lax.broadcasted_iota(I32, (SUBLANES, tm), 0)
    rank_ref[...] = jnp.where(row == 0, rank0, jnp.where(row == 1, rank1, 0.0)).astype(I32)
    total = carry[...] + jnp.sum(hit, axis=1, keepdims=True)
    carry[...] = total
    cnt_ref[...] = total.astype(I32)


def _rank(eid, *, tm):
    T = eid.shape[1]
    return pl.pallas_call(
        _rank_kernel,
        grid=(T // tm,),
        in_specs=[pl.BlockSpec((SUBLANES, tm), lambda i: (0, i))],
        out_specs=[
            pl.BlockSpec((SUBLANES, tm), lambda i: (0, i)),
            pl.BlockSpec((N_EXPERTS, LANES), lambda i: (0, 0)),
        ],
        out_shape=[
            jax.ShapeDtypeStruct((SUBLANES, T), I32),
            jax.ShapeDtypeStruct((N_EXPERTS, LANES), I32),
        ],
        scratch_shapes=[pltpu.VMEM((N_EXPERTS, LANES), F32)],
        compiler_params=_cparams(("arbitrary",), 32),
        name="rank",
    )(eid)


def _dispatch_kernel(dest_ref, pad0_ref, npad_ref, nb_ref, xn_ref, xb_hbm, zbuf, sem, psem,
                     *, n_tok, tb, n_blocks):
    i = pl.program_id(0)
    tm = xn_ref.shape[0]
    base = i * tm

    def row_body(r, carry):
        for k in range(TOP_K):
            pltpu.make_async_copy(xn_ref.at[r], xb_hbm.at[dest_ref[k * n_tok + base + r]], sem).start(priority=k)
        return carry
    lax.fori_loop(0, tm, row_body, 0, unroll=8)

    @pl.when(i == 0)
    def _():
        _dispatch_fill(pad0_ref, npad_ref, nb_ref, xb_hbm, zbuf, psem, tb=tb, n_blocks=n_blocks)

    for k in range(TOP_K):
        pltpu.make_async_copy(xn_ref, xb_hbm.at[pl.ds(0, tm)], sem).wait()


def _dispatch_fill(pad0_ref, npad_ref, nb_ref, xb_hbm, zbuf, psem, *, tb, n_blocks):
    zbuf[...] = jnp.zeros(zbuf.shape, zbuf.dtype)

    def pad_copy(e):
        n = npad_ref[e]
        return pltpu.make_async_copy(zbuf.at[pl.ds(0, n)], xb_hbm.at[pl.ds(pad0_ref[e], n)], psem.at[0])

    def tail_copy(blk):
        return pltpu.make_async_copy(zbuf, xb_hbm.at[pl.ds(pl.multiple_of(blk * tb, tb), tb)], psem.at[1])

    def for_each_pad(fn):
        def body(e, c):
            @pl.when(npad_ref[e] > 0)
            def _():
                fn(e)
            return c
        lax.fori_loop(0, N_EXPERTS, body, 0)

    def for_each_tail(fn):
        def body(b, c):
            fn(b)
            return c
        lax.fori_loop(nb_ref[0], n_blocks, body, 0)

    for_each_pad(lambda e: pad_copy(e).start())
    for_each_tail(lambda b: tail_copy(b).start())
    for_each_pad(lambda e: pad_copy(e).wait())
    for_each_tail(lambda b: tail_copy(b).wait())


def _dispatch(dest, pad0, npad, nb, xn, *, tb, n_blocks, tm):
    T = xn.shape[0]
    tile = xn.shape[1:]
    kern = functools.partial(_dispatch_kernel, n_tok=T, tb=tb, n_blocks=n_blocks)
    grid_spec = pltpu.PrefetchScalarGridSpec(
        num_scalar_prefetch=4,
        grid=(T // tm,),
        in_specs=[pl.BlockSpec((tm,) + tile, lambda i, *_: (i, 0, 0))],
        out_specs=pl.BlockSpec(memory_space=pl.ANY),
        scratch_shapes=[
            pltpu.VMEM((tb,) + tile, xn.dtype),
            pltpu.SemaphoreType.DMA(()),
            pltpu.SemaphoreType.DMA((2,)),
        ],
    )
    return pl.pallas_call(
        kern,
        grid_spec=grid_spec,
        out_shape=jax.ShapeDtypeStruct((n_blocks * tb,) + tile, xn.dtype),
        compiler_params=_cparams(("arbitrary",), 32),
        name="dispatch",
    )(dest, pad0, npad, nb, xn)


def _expert_kernel(be_ref, first_ref, slot_ref, nxt_ref, run1_ref, nb_ref, x_ref, wg_hbm, wu_hbm, wd_hbm, y_ref,
                   wg_f, wu_f, wd_f, wsem, wg_b, wu_b, wd_b):
    i = pl.program_id(0)
    nb = nb_ref[0]

    def weight_copies(e, slot):
        return (pltpu.make_async_copy(wg_hbm.at[e], wg_f.at[slot], wsem.at[3 * slot]),
                pltpu.make_async_copy(wu_hbm.at[e], wu_f.at[slot], wsem.at[3 * slot + 1]),
                pltpu.make_async_copy(wd_hbm.at[e], wd_f.at[slot], wsem.at[3 * slot + 2]))

    @pl.when(i == 0)
    def _():
        for cp in weight_copies(be_ref[0], 0):
            cp.start()

    @pl.when((i == 0) & (run1_ref[0] >= 0))
    def _():
        for cp in weight_copies(jnp.maximum(run1_ref[0], 0), 1):
            cp.start()

    first = (i < nb) & (first_ref[i] == 1)
    slot = slot_ref[i]

    @pl.when(first)
    def _():
        for cp in weight_copies(0, slot):
            cp.wait()
        wg_b[...] = wg_f[slot].astype(BF16)
        wu_b[...] = wu_f[slot].astype(BF16)
        wd_b[...] = wd_f[slot].astype(BF16)

    @pl.when(first & (nxt_ref[i] >= 0))
    def _():
        for cp in weight_copies(jnp.maximum(nxt_ref[i], 0), slot):
            cp.start()

    @pl.when(i < nb)
    def _():
        n_lo, n_hi = _unpack_bf16_pairs(_load_row_tiles(x_ref))
        half = n_lo.shape[1]
        a = _dot(n_lo, wg_b[0:half, :]) + _dot(n_hi, wg_b[half:, :])
        u = _dot(n_lo, wu_b[0:half, :]) + _dot(n_hi, wu_b[half:, :])
        hdn = (a / (1.0 + jnp.exp(-a))) * u
        y_ref[...] = _pack_bf16_pairs(_dot(hdn.astype(BF16), wd_b[...]).astype(BF16))

    @pl.when(i >= nb)
    def _():
        y_ref[...] = jnp.zeros(y_ref.shape, y_ref.dtype)


def _experts(xb, blk_e, blk_first, blk_slot, blk_next, run1, nb, w_gate, w_up, w_down, *, tb, n_blocks):
    D, De = w_gate.shape[1:]
    assert xb.shape[1:] == (SUBLANES, LANES) and D == 2 * SUBLANES * LANES

    def x_map(i, be, first, slot, nxt, r1, nbr):
        return (jnp.minimum(i, nbr[0] - 1), 0, 0)

    grid_spec = pltpu.PrefetchScalarGridSpec(
        num_scalar_prefetch=6,
        grid=(n_blocks,),
        in_specs=[
            pl.BlockSpec((tb, SUBLANES, LANES), x_map),
            pl.BlockSpec(memory_space=pl.ANY),
            pl.BlockSpec(memory_space=pl.ANY),
            pl.BlockSpec(memory_space=pl.ANY),
        ],
        out_specs=pl.BlockSpec((tb, D // 2), lambda i, *_: (i, 0)),
        scratch_shapes=[
            pltpu.VMEM((2, D, De), F32),
            pltpu.VMEM((2, D, De), F32),
            pltpu.VMEM((2, De, D), F32),
            pltpu.SemaphoreType.DMA((6,)),
            pltpu.VMEM((D, De), BF16),
            pltpu.VMEM((D, De), BF16),
            pltpu.VMEM((De, D), BF16),
        ],
    )
    return pl.pallas_call(
        _expert_kernel,
        grid_spec=grid_spec,
        out_shape=jax.ShapeDtypeStruct((n_blocks * tb, D // 2), jnp.uint32),
        compiler_params=_cparams(("arbitrary",), 52),
        name="experts",
    )(blk_e, blk_first, blk_slot, blk_next, run1, nb, xb, w_gate, w_up, w_down)


def _combine_kernel(dest_ref, y_hbm, h_ref, gate_ref, o_ref, ybuf, sem, *, tm, n_tok):
    i = pl.program_id(0)
    n = pl.num_programs(0)

    def row_copy(d, k, r, slot):
        return pltpu.make_async_copy(y_hbm.at[pl.ds(d, 1), :], ybuf.at[slot, k, pl.ds(r, 1), :], sem.at[slot])

    def start_gather(blk, slot):
        def body(r, carry):
            for k in range(TOP_K):
                row_copy(dest_ref[k * n_tok + blk * tm + r], k, r, slot).start(priority=k)
            return carry
        lax.fori_loop(0, tm, body, 0, unroll=8)

    def wait_gather(slot):
        for k in range(TOP_K):
            pltpu.make_async_copy(y_hbm.at[pl.ds(0, tm), :], ybuf.at[slot, k], sem.at[slot]).wait()

    @pl.when(i == 0)
    def _():
        start_gather(0, 0)

    @pl.when(i + 1 < n)
    def _():
        start_gather(i + 1, (i + 1) % 2)

    slot = i % 2
    wait_gather(slot)
    gt = gate_ref[...]
    half = h_ref.shape[1] // 2
    y0_lo, y0_hi = _unpack_bf16_pairs_f32(ybuf[slot, 0])
    y1_lo, y1_hi = _unpack_bf16_pairs_f32(ybuf[slot, 1])
    o_ref[:, 0:half] = h_ref[:, 0:half] + gt[:, 0:1] * y0_lo + gt[:, 1:2] * y1_lo
    o_ref[:, half:] = h_ref[:, half:] + gt[:, 0:1] * y0_hi + gt[:, 1:2] * y1_hi


def _combine(dest, yb, h2, gate, *, tm):
    T, D = h2.shape
    kern = functools.partial(_combine_kernel, tm=tm, n_tok=T)
    grid_spec = pltpu.PrefetchScalarGridSpec(
        num_scalar_prefetch=1,
        grid=(T // tm,),
        in_specs=[
            pl.BlockSpec(memory_space=pl.ANY),
            pl.BlockSpec((tm, D), lambda i, d: (i, 0)),
            pl.BlockSpec((tm, TOP_K), lambda i, d: (i, 0)),
        ],
        out_specs=pl.BlockSpec((tm, D), lambda i, d: (i, 0)),
        scratch_shapes=[
            pltpu.VMEM((2, TOP_K, tm, D // 2), jnp.uint32),
            pltpu.SemaphoreType.DMA((2,)),
        ],
    )
    return pl.pallas_call(
        kern,
        grid_spec=grid_spec,
        out_shape=jax.ShapeDtypeStruct((T, D), F32),
        compiler_params=_cparams(("arbitrary",), 40),
        name="combine",
    )(dest, yb, h2, gate)


class _Tiles(NamedTuple):
    proj_rows: int = 1024
    proj_cols: int = 512
    attn_rows: int = 512
    gla_rows: int = 2048
    token_rows: int = 512
    dispatch_rows: int = 2048
    expert_rows: int = 256


TILES = _Tiles()


def kernel(x, mem, positions, g_attn, w_in, q_norm_g, k_norm_g, lambda_q1, lambda_k1, lambda_q2, lambda_k2, diff_subln_g, gla_w_a2, gla_b_a, gla_out_g, w_out, g_cross, g_mem, w_cq, w_ckv, cq_norm_g, ck_norm_g, w_co, g_ffn, w_router_grp, b_router_grp, w_router_exp, b_router_exp, w_gate, w_up, w_down):
    B, S, D = x.shape
    T = B * S
    n_mem = mem.shape[1]
    l = 0
    x2 = x.reshape(T, D)

    half = DIFF_QKDIM // 2
    freq = ROPE_THETA ** (-jnp.arange(half, dtype=F32) / half)
    freq = jnp.tile(freq, LANES // half)[None, :]
    q_scale = math.log2(math.e) * DIFF_QKDIM ** -0.5
    qkg = jnp.stack([jnp.tile(q_norm_g[l], 2) * q_scale, jnp.tile(k_norm_g[l], 2)])
    score_bound = 1.01 * DIFF_QKDIM * q_scale * jnp.max(jnp.abs(q_norm_g[l])) * jnp.max(jnp.abs(k_norm_g[l]))
    lvec = jnp.stack([lambda_q1[l], lambda_k1[l], lambda_q2[l], lambda_k2[l]])

    qk, mid, log_a, sgr = _inproj(x2, g_attn[l][None], positions.reshape(T, 1), freq, qkg, w_in[l].T,
                                  gla_w_a2[l], gla_b_a[l][None], tm=TILES.proj_rows)
    diffattn = functools.partial(_diffattn, lvec, qk, mid, diff_subln_g[l][None], B=B, S=S, tq=TILES.attn_rows)
    mix_d = lax.cond(score_bound <= SCORE_BOUND,
                     functools.partial(diffattn, bounded=True), functools.partial(diffattn, bounded=False))
    mix_g = _gla(mid, log_a, sgr, gla_out_g[l][None], B=B, S=S, blk=TILES.gla_rows)
    h1 = _outproj(mix_d, mix_g, w_out[l], x2, tm=TILES.proj_rows, tn=TILES.proj_cols)

    hdim = D // CROSS_HEADS
    qc = _normproj(h1, g_cross[l][None], w_cq[l], cq_norm_g[l][None] * (hdim ** -0.5),
                   tm=TILES.proj_rows, tn=hdim, n_norm=CROSS_HEADS, name="cq")
    kv = _normproj(mem.reshape(B * n_mem, D), g_mem[l][None], w_ckv[l], ck_norm_g[l][None],
                   tm=B * n_mem, tn=hdim, n_norm=CROSS_HEADS, name="ckv")
    h2 = _cross(qc, kv, w_co[l], h1, S=S, n_mem=n_mem, tm=TILES.proj_rows, tn=TILES.proj_cols)

    w_rt = jnp.concatenate([w_router_grp[l].T, jnp.zeros((SUBLANES - N_GROUPS, D), F32), w_router_exp[l].T])
    b_r = jnp.concatenate([b_router_grp[l], jnp.zeros((SUBLANES - N_GROUPS,), F32), b_router_exp[l]])[:, None]
    eid, gate, xn = _router(h2, g_ffn[l][None], w_rt, b_r, tm=TILES.token_rows)
    rank, cnt = _rank(eid, tm=TILES.token_rows)

    assert TOP_K == 2
    tb = TILES.expert_rows
    n_blocks = (T * TOP_K + N_EXPERTS * (tb - 1) + tb - 1) // tb
    counts = cnt[:, 0]
    pcounts = ((counts + tb - 1) // tb) * tb
    pends = jnp.cumsum(pcounts)
    pstarts = pends - pcounts
    nb = (pends[-1:] // tb).astype(I32)
    eids = jnp.arange(N_EXPERTS, dtype=I32)
    blk_start = jnp.arange(n_blocks, dtype=I32) * tb
    blk_e = jnp.minimum(jnp.sum(pends[None, :] <= blk_start[:, None], axis=1), N_EXPERTS - 1).astype(I32)
    blk_first = jnp.concatenate([jnp.ones((1,), I32), (blk_e[1:] != blk_e[:-1]).astype(I32)])
    used = jnp.where(counts > 0, eids, N_EXPERTS)
    next_used = jnp.concatenate([lax.cummin(used[::-1])[::-1][1:], jnp.full((1,), N_EXPERTS, I32)])
    next_used = jnp.where(next_used < N_EXPERTS, next_used, -1)

    def table_at(idx, table):
        return jnp.where(idx >= 0, jnp.sum(jnp.where(idx[:, None] == eids, table, 0), axis=1), -1).astype(I32)

    after_next = table_at(next_used, next_used)
    run_of = (jnp.cumsum((counts > 0).astype(I32)) - 1).astype(I32)
    blk_next = table_at(blk_e, after_next)
    blk_slot = table_at(blk_e, run_of % 2)
    run1 = table_at(blk_e[:1], next_used)
    pick = eid[:TOP_K]
    pstart_of = jnp.sum(jnp.where(pick[..., None] == eids, pstarts, 0), axis=-1)
    dest = (pstart_of + rank[:TOP_K]).astype(I32).reshape(-1)
    pad0 = (pstarts + counts).astype(I32)
    npad = (pcounts - counts).astype(I32)

    xb = _dispatch(dest, pad0, npad, nb, xn, tb=tb, n_blocks=n_blocks, tm=TILES.dispatch_rows)
    yb = _experts(xb, blk_e, blk_first, blk_slot, blk_next, run1, nb, w_gate[l], w_up[l], w_down[l],
                  tb=tb, n_blocks=n_blocks)
    out = _combine(dest, yb, h2, gate[:TOP_K].T, tm=TILES.token_rows)
    return out.reshape(B, S, D)
```
